```python
import math
import jax, jax.numpy as jnp
from jax import lax
import numpy as np

D_MODEL = 1024
BATCH = 2
SEQ = 8192
DEPTH = 2

HEAD_DIM = 64
MIX_WIDTH = D_MODEL
N_MIXERS = 4
HEADS_PER_MIXER = MIX_WIDTH // (N_MIXERS * HEAD_DIM)
GROUP_WIDTH = HEADS_PER_MIXER * HEAD_DIM
N_HEADS_TOTAL = N_MIXERS * HEADS_PER_MIXER
QUERY_BLOCK = 128
NUM_BUCKETS = 32
REL_MAX_DIST = 2048
DILATED_CONFIGS = ((128, 1), (512, 4), (2048, 16))
SWA_WINDOW = 128
SWA_KV_HEADS = HEADS_PER_MIXER // 2
DIFF_QK_DIM = HEAD_DIM // 2
CMP_LEN = 32
CMP_STRIDE = 16
CMP_HIDDEN = 256
SLC_BLOCK = 64
SLC_TOPK = 16
NSA_WINDOW = 512
NSA_BRANCHES = 3
RMS_EPS = 1e-6
NEG_INF = -1e30
FORCE_SELECT = 1e9
TINY = 1e-30
PROJ_SIZES = (
    GROUP_WIDTH, GROUP_WIDTH, GROUP_WIDTH,
    GROUP_WIDTH, SWA_KV_HEADS * HEAD_DIM, SWA_KV_HEADS * HEAD_DIM,
    GROUP_WIDTH, GROUP_WIDTH, GROUP_WIDTH,
    GROUP_WIDTH, HEAD_DIM, HEAD_DIM, HEAD_DIM, HEAD_DIM, HEAD_DIM, HEAD_DIM,
    HEADS_PER_MIXER * NSA_BRANCHES,
    MIX_WIDTH,
)
PROJ_WIDTH = sum(PROJ_SIZES)

kernel_name = 'hybrid_parallel_heads_dilated_swa_diff_nsa'


def rms_norm(t, g):
    tf = t.astype(jnp.float32)
    y = tf * lax.rsqrt(jnp.mean(tf * tf, axis=-1, keepdims=True) + RMS_EPS) * g.astype(jnp.float32)
    return y.astype(t.dtype)


def t5_bucket(dist):
    n = jnp.maximum(dist, 0)
    max_exact = NUM_BUCKETS // 2
    nf = jnp.maximum(n, 1).astype(jnp.float32)
    large = max_exact + (jnp.log(nf / max_exact) / math.log(REL_MAX_DIST / max_exact)
                         * (NUM_BUCKETS - max_exact)).astype(jnp.int32)
    large = jnp.minimum(large, NUM_BUCKETS - 1)
    return jnp.where(n < max_exact, n, large)


def rel_bias(dist, table_h):
    return jnp.take(table_h, t5_bucket(dist), axis=0).astype(jnp.float32)


def masked_probs(s, valid, sink=None):
    s = jnp.where(valid, s, NEG_INF)
    m = jnp.max(s, axis=-1, keepdims=True)
    if sink is not None:
        m = jnp.maximum(m, sink)
    p = jnp.exp(s - m) * valid
    den = jnp.sum(p, axis=-1, keepdims=True)
    if sink is not None:
        den = den + jnp.exp(sink - m)
    return p, m, den


def banded_attention(q, k, v, table_h, max_dist, dist_scale=1, sink=None):
    bt, nh, seq_len, dh = q.shape
    n_prev = -(-max_dist // QUERY_BLOCK)
    nb = -(-seq_len // QUERY_BLOCK)
    pad = nb * QUERY_BLOCK - seq_len
    pad_seq = lambda t: jnp.pad(t, ((0, 0), (0, 0), (0, pad), (0, 0))).reshape(bt, nh, nb, QUERY_BLOCK, dh)
    qb = pad_seq(q)
    front = ((0, 0), (0, 0), (n_prev, 0), (0, 0), (0, 0))
    kbp = jnp.pad(pad_seq(k), front)
    vbp = jnp.pad(pad_seq(v), front)
    kk = jnp.concatenate([kbp[:, :, i:i + nb] for i in range(n_prev + 1)], axis=3)
    vv = jnp.concatenate([vbp[:, :, i:i + nb] for i in range(n_prev + 1)], axis=3)
    s = jnp.einsum('bhnqd,bhnkd->bhnqk', qb, kk).astype(jnp.float32) * (dh ** -0.5)
    qi = jnp.arange(QUERY_BLOCK)
    kj = jnp.arange((n_prev + 1) * QUERY_BLOCK) - n_prev * QUERY_BLOCK
    dist = qi[:, None] - kj[None, :]
    kpos = jnp.arange(nb)[:, None] * QUERY_BLOCK + kj[None, :]
    valid = ((dist >= 0) & (dist <= max_dist))[None] & (kpos >= 0)[:, None, :]
    s = s + rel_bias(dist * dist_scale, table_h).transpose(2, 0, 1)[:, None]
    if sink is not None:
        sink = sink.astype(jnp.float32).reshape(-1, 1, 1, 1)
    p, m, den = masked_probs(s, valid, sink)
    o = jnp.einsum('bhnqk,bhnkd->bhnqd', p, vv.astype(jnp.float32)) / den
    lse = (m + jnp.log(den))[..., 0]
    o = o.reshape(bt, nh, nb * QUERY_BLOCK, dh)[:, :, :seq_len]
    lse = lse.reshape(bt, nh, nb * QUERY_BLOCK)[:, :, :seq_len]
    return o.astype(q.dtype), lse


def dilated_attention(q, k, v, table_h):
    b, nh, s, dh = q.shape
    outs, lses = [], []
    for window, rate in DILATED_CONFIGS:
        fold = lambda t: t.reshape(b, nh, s // rate, rate, dh).transpose(0, 3, 1, 2, 4).reshape(b * rate, nh, s // rate, dh)
        o, lse = banded_attention(fold(q), fold(k), fold(v), table_h, window // rate, dist_scale=rate)
        outs.append(o.reshape(b, rate, nh, s // rate, dh).transpose(0, 2, 3, 1, 4).reshape(b, nh, s, dh).astype(jnp.float32))
        lses.append(lse.reshape(b, rate, nh, s // rate).transpose(0, 2, 3, 1).reshape(b, nh, s))
    wts = jax.nn.softmax(jnp.stack(lses, axis=0), axis=0)
    return jnp.sum(wts[..., None] * jnp.stack(outs, axis=0), axis=0).astype(q.dtype)


def diff_attention(q, k, v, lam, table_h):
    b, nh, s, _, dq = q.shape
    dv = v.shape[-1]
    kpos = jnp.arange(s)
    vf = v.astype(jnp.float32)
    lam = lam.astype(jnp.float32)

    def block(n):
        qb = lax.dynamic_slice_in_dim(q, n * QUERY_BLOCK, QUERY_BLOCK, axis=2)
        sc = jnp.einsum('bhqcd,bhkcd->bhcqk', qb, k).astype(jnp.float32) * (dq ** -0.5)
        qpos = n * QUERY_BLOCK + jnp.arange(QUERY_BLOCK)
        dist = qpos[:, None] - kpos[None, :]
        sc = sc + rel_bias(dist, table_h).transpose(2, 0, 1)[:, None]
        p, _, den = masked_probs(sc, dist >= 0)
        p = p / den
        a = p[:, :, 0] - lam * p[:, :, 1]
        return jnp.einsum('bhqk,bhkd->bhqd', a, vf)

    o = lax.map(block, jnp.arange(s // QUERY_BLOCK))
    return o.transpose(1, 2, 0, 3, 4).reshape(b, nh, s, dv)


def compress(t, pos, w1, b1, w2, b2):
    b, s, dh = t.shape
    chunks = t.reshape(b, s // CMP_STRIDE, CMP_STRIDE, dh)
    blocks = jnp.concatenate([chunks[:, :-1], chunks[:, 1:]], axis=2) + pos
    blocks = blocks.reshape(b, blocks.shape[1], CMP_LEN * dh)
    return jax.nn.gelu(blocks @ w1 + b1) @ w2 + b2


def nsa_attention(q, k_c, v_c, k_s, v_s, k_w, v_w, gates, cmp_pos, cmp_w1, cmp_b1, cmp_w2, cmp_b2, k_gains, table_h):
    b, nh, s, dh = q.shape
    scale = dh ** -0.5
    kc = rms_norm(compress(k_c, cmp_pos[0], cmp_w1[0], cmp_b1[0], cmp_w2[0], cmp_b2[0]), k_gains[0])
    vc = compress(v_c, cmp_pos[1], cmp_w1[1], cmp_b1[1], cmp_w2[1], cmp_b2[1]).astype(jnp.float32)
    n_cmp = kc.shape[1]
    cmp_end = jnp.arange(n_cmp) * CMP_STRIDE + CMP_LEN - 1
    n_blk = s // SLC_BLOCK
    n_sel = min(SLC_TOPK, n_blk)
    blk_ids = jnp.arange(n_blk)
    overlap = ((cmp_end[:, None] - CMP_LEN + 1 < (blk_ids[None, :] + 1) * SLC_BLOCK)
               & (cmp_end[:, None] >= blk_ids[None, :] * SLC_BLOCK)).astype(jnp.float32)
    ks_blocks = rms_norm(k_s, k_gains[1]).reshape(b, n_blk, SLC_BLOCK, dh)
    vs_blocks = v_s.reshape(b, n_blk, SLC_BLOCK, dh)
    b_idx = jnp.arange(b)[:, None, None]

    def chunk(n):
        qb = lax.dynamic_slice_in_dim(q, n * QUERY_BLOCK, QUERY_BLOCK, axis=2)
        qpos = n * QUERY_BLOCK + jnp.arange(QUERY_BLOCK)
        cdist = qpos[:, None] - cmp_end[None, :]
        sc = jnp.einsum('bhqd,bcd->bhqc', qb, kc).astype(jnp.float32) * scale
        sc = sc + rel_bias(cdist, table_h).transpose(2, 0, 1)
        p, _, den = masked_probs(sc, cdist >= 0)
        p = p / jnp.maximum(den, TINY)
        o_cmp = jnp.einsum('bhqc,bcd->bhqd', p, vc)
        imp = jnp.einsum('bhqc,cj->bqj', p, overlap)
        cur = qpos // SLC_BLOCK
        forced = (blk_ids == 0) | (blk_ids == cur[:, None]) | (blk_ids == cur[:, None] - 1)
        imp = jnp.where(forced, FORCE_SELECT, jnp.where(blk_ids <= cur[:, None], imp, NEG_INF))
        _, idx = lax.top_k(imp, n_sel)
        ks = ks_blocks[b_idx, idx].reshape(b, QUERY_BLOCK, n_sel * SLC_BLOCK, dh)
        vs = vs_blocks[b_idx, idx].reshape(b, QUERY_BLOCK, n_sel * SLC_BLOCK, dh)
        kpos = (idx[..., None] * SLC_BLOCK + jnp.arange(SLC_BLOCK)).reshape(b, QUERY_BLOCK, n_sel * SLC_BLOCK)
        sdist = qpos[None, :, None] - kpos
        ss = jnp.einsum('bhqd,bqkd->bhqk', qb, ks).astype(jnp.float32) * scale
        ss = ss + jnp.moveaxis(rel_bias(sdist, table_h), -1, 1)
        p2, _, den2 = masked_probs(ss, (sdist >= 0)[:, None])
        o_slc = jnp.einsum('bhqk,bqkd->bhqd', p2, vs.astype(jnp.float32)) / den2
        return o_cmp, o_slc

    o_cmp, o_slc = lax.map(chunk, jnp.arange(s // QUERY_BLOCK))
    unblock = lambda o: o.transpose(1, 2, 0, 3, 4).reshape(b, nh, s, dh)
    kw = jnp.repeat(rms_norm(k_w, k_gains[2])[:, None], nh, axis=1)
    vw = jnp.repeat(v_w[:, None], nh, axis=1)
    o_win, _ = banded_attention(q, kw, vw, table_h, NSA_WINDOW - 1)
    g = jax.nn.sigmoid(gates.astype(jnp.float32))
    o = g[..., 0:1] * unblock(o_cmp) + g[..., 1:2] * unblock(o_slc) + g[..., 2:3] * o_win.astype(jnp.float32)
    return o.astype(q.dtype)


def setup_inputs(seed: int = 0) -> dict:
    key = jax.random.key(seed)
    ks = jax.random.split(key, 15)
    nrm = lambda k, shape, sc: sc * jax.random.normal(k, shape, jnp.float32)
    return {
        'x': nrm(ks[0], (BATCH, SEQ, D_MODEL), 1.0),
        'rel_bias_table': nrm(ks[1], (NUM_BUCKETS, N_HEADS_TOTAL), 0.5),
        'norm_w': 1.0 + nrm(ks[2], (DEPTH, D_MODEL), 0.02),
        'w_in': nrm(ks[3], (DEPTH, D_MODEL, PROJ_WIDTH), D_MODEL ** -0.5),
        'w_out': nrm(ks[4], (DEPTH, MIX_WIDTH, D_MODEL), MIX_WIDTH ** -0.5),
        'qk_gain': 1.0 + nrm(ks[5], (DEPTH, 8, HEAD_DIM), 0.02),
        'qk_gain_diff': 1.0 + nrm(ks[6], (DEPTH, 2, DIFF_QK_DIM), 0.02),
        'attn_sinks': nrm(ks[7], (DEPTH, HEADS_PER_MIXER), 1.0),
        'diff_lambda': nrm(ks[8], (DEPTH, 4, DIFF_QK_DIM), 0.1),
        'diff_subln': 1.0 + nrm(ks[9], (DEPTH, HEAD_DIM), 0.02),
        'cmp_pos': nrm(ks[10], (DEPTH, 2, CMP_LEN, HEAD_DIM), 0.02),
        'cmp_w1': nrm(ks[11], (DEPTH, 2, CMP_LEN * HEAD_DIM, CMP_HIDDEN), (CMP_LEN * HEAD_DIM) ** -0.5),
        'cmp_b1': nrm(ks[12], (DEPTH, 2, CMP_HIDDEN), 0.02),
        'cmp_w2': nrm(ks[13], (DEPTH, 2, CMP_HIDDEN, HEAD_DIM), CMP_HIDDEN ** -0.5),
        'cmp_b2': nrm(ks[14], (DEPTH, 2, HEAD_DIM), 0.02),
    }


def reference(x, rel_bias_table, norm_w, w_in, w_out, qk_gain, qk_gain_diff, attn_sinks, diff_lambda,
              diff_subln, cmp_pos, cmp_w1, cmp_b1, cmp_w2, cmp_b2):
    b, s, _ = x.shape
    split_points = np.cumsum(PROJ_SIZES)[:-1]
    tables = [rel_bias_table[:, m * HEADS_PER_MIXER:(m + 1) * HEADS_PER_MIXER] for m in range(N_MIXERS)]
    heads = lambda t: t.reshape(b, s, -1, HEAD_DIM).transpose(0, 2, 1, 3)
    merge = lambda o: o.transpose(0, 2, 1, 3).reshape(b, s, -1).astype(x.dtype)
    rep = HEADS_PER_MIXER // SWA_KV_HEADS
    for layer in range(DEPTH):
        g = qk_gain[layer]
        h = rms_norm(x, norm_w[layer]) @ w_in[layer]
        (a_q, a_k, a_v, b_q, b_k, b_v, c_q, c_k, c_v, d_q, d_kc, d_vc, d_ks, d_vs, d_kw, d_vw,
         d_gate, silu_gate) = jnp.split(h, split_points, axis=-1)
        o_a = dilated_attention(rms_norm(heads(a_q), g[0]), rms_norm(heads(a_k), g[1]), heads(a_v), tables[0])
        kb = jnp.repeat(rms_norm(heads(b_k), g[3]), rep, axis=1)
        vb = jnp.repeat(heads(b_v), rep, axis=1)
        o_b, _ = banded_attention(rms_norm(heads(b_q), g[2]), kb, vb, tables[1], SWA_WINDOW - 1,
                                  sink=attn_sinks[layer])
        split2 = lambda t: t.reshape(b, s, HEADS_PER_MIXER, 2, DIFF_QK_DIM).transpose(0, 2, 1, 3, 4)
        lambda_init = 0.8 - 0.6 * math.exp(-0.3 * layer)
        lq1, lk1, lq2, lk2 = diff_lambda[layer, 0], diff_lambda[layer, 1], diff_lambda[layer, 2], diff_lambda[layer, 3]
        lam = jnp.exp(jnp.sum(lq1 * lk1)) - jnp.exp(jnp.sum(lq2 * lk2)) + lambda_init
        o_c = diff_attention(rms_norm(split2(c_q), qk_gain_diff[layer, 0]), rms_norm(split2(c_k), qk_gain_diff[layer, 1]),
                             heads(c_v), lam, tables[2])
        o_c = rms_norm(o_c, diff_subln[layer]) * (1.0 - lambda_init)
        gates = d_gate.reshape(b, s, HEADS_PER_MIXER, NSA_BRANCHES).transpose(0, 2, 1, 3)
        o_d = nsa_attention(rms_norm(heads(d_q), g[4]), d_kc, d_vc, d_ks, d_vs, d_kw, d_vw, gates,
                            cmp_pos[layer], cmp_w1[layer], cmp_b1[layer], cmp_w2[layer], cmp_b2[layer],
                            g[5:8], tables[3])
        y = jnp.concatenate([merge(o_a), merge(o_b), merge(o_c), merge(o_d)], axis=-1) * jax.nn.silu(silu_gate)
        x = x + y @ w_out[layer]
    return x
```

```python
import functools
import math

import numpy as np
import jax
import jax.numpy as jnp
from jax import lax
from jax.experimental import pallas as pl
from jax.experimental.pallas import tpu as pltpu

F32 = jnp.float32
BF16 = jnp.bfloat16

HEAD_DIM = 64
HEADS = 4
GROUP = HEADS * HEAD_DIM
N_MIXERS = 4
NUM_BUCKETS = 32
REL_MAX_DIST = 2048
DILATED_CONFIGS = ((128, 1), (512, 4), (2048, 16))
SWA_WINDOW = 128
DIFF_QK_DIM = HEAD_DIM // 2
CMP_LEN = 32
CMP_STRIDE = 16
CMP_HIDDEN = 256
SLC_BLOCK = 64
SLC_TOPK = 16
NSA_WINDOW = 512
RMS_EPS = 1e-6
NEG_INF = -1e30
FORCE_SELECT = 1e9
TINY = 1e-30

PROJ_ROWS = 512
BAND_TILE = 128
LANES = 128
QT = 256
KT = 256
VMEM_LIMIT = 56 * 1024 * 1024

NT_DIMS = (((1,), (1,)), ((), ()))


def _t5_thresholds():
    n = np.arange(0, 4 * REL_MAX_DIST)
    max_exact = NUM_BUCKETS // 2
    nf = np.maximum(n, 1).astype(np.float32)
    large = max_exact + (np.log(nf / np.float32(max_exact)) / np.float32(math.log(REL_MAX_DIST / max_exact))
                         * np.float32(NUM_BUCKETS - max_exact)).astype(np.int32)
    bucket = np.where(n < max_exact, n, np.minimum(large, NUM_BUCKETS - 1))
    return [int(np.argmax(bucket >= b)) for b in range(NUM_BUCKETS)]


T5_THRESHOLDS = _t5_thresholds()
FAR_DIST = T5_THRESHOLDS[-1]


def _bias_kernel(tbl_ref, out_ref, *, head0, base0, dstep, rs, cs, dscale, max_dist, r_valid, col_tile):
    h = pl.program_id(0)
    d = pl.program_id(1)
    ct = pl.program_id(2)
    rows, cols = out_ref.shape[-2:]
    r = lax.broadcasted_iota(jnp.int32, (rows, cols), 0)
    c = lax.broadcasted_iota(jnp.int32, (rows, cols), 1) + ct * col_tile
    dist = base0 + d * dstep + r * rs + c * cs
    n = dist * dscale
    val = jnp.full((rows, cols), tbl_ref[0, head0 + h], F32)
    for b in range(1, NUM_BUCKETS):
        val = jnp.where(n >= T5_THRESHOLDS[b], tbl_ref[b, head0 + h], val)
    valid = (dist >= 0) & (dist <= max_dist) & (r < r_valid)
    out_ref[0, 0] = jnp.where(valid, val, NEG_INF)


def _build_bias(table, *, head0, n_d, rows, cols, base0, dstep, rs, cs, dscale=1, max_dist=1 << 30,
                r_valid=1 << 30, col_tile=None):
    col_tile = cols if col_tile is None else col_tile
    kern = functools.partial(_bias_kernel, head0=head0, base0=base0, dstep=dstep, rs=rs, cs=cs, dscale=dscale,
                             max_dist=max_dist, r_valid=r_valid, col_tile=col_tile)
    return pl.pallas_call(
        kern,
        grid=(HEADS, n_d, cols // col_tile),
        in_specs=[pl.BlockSpec(memory_space=pltpu.SMEM)],
        out_specs=pl.BlockSpec((1, 1, rows, col_tile), lambda h, d, c: (h, d, 0, c)),
        out_shape=jax.ShapeDtypeStruct((HEADS, n_d, rows, cols), F32),
        name="rel_bias_tiles",
    )(table)


RM_AQ, RM_AK, RM_AV = 0, 256, 512
RM_BQ, RM_BK, RM_BV = 768, 1024, 1280
RM_CK = 1536
RM_KVC = 1792
RM_KSW = 1920
RM_GATE = 2048
RM_COLS = 3072
TR_CQ, TR_CV, TR_DQ, TR_DVS, TR_DVW, TR_DG = 0, 256, 512, 768, 832, 896
TR_ROWS = 912


def _proj_kernel(x_ref, nw_ref, wrm_ref, wt_ref, grm_ref, gt_ref, e64_ref, e32_ref,
                 aq_ref, ak_ref, av_ref, bq_ref, bk_ref, bv_ref, ck_ref, kvc_ref, ksw_ref, gate_ref,
                 cq_ref, cv_ref, dq_ref, dvs_ref, dvw_ref, dg_ref):
    x = x_ref[...]
    ms = jnp.mean(x * x, axis=-1, keepdims=True)
    xn = (x * lax.rsqrt(ms + RMS_EPS) * nw_ref[...]).astype(BF16)
    rows = x.shape[0]

    def rm(c0, width):
        return jnp.dot(xn, wrm_ref[:, c0:c0 + width], preferred_element_type=F32)

    def rm_normed(c0, width, e_ref):
        h = rm(c0, width)
        msq = jnp.dot((h * h).astype(BF16), e_ref[0:width, 0:width], preferred_element_type=F32)
        return h * lax.rsqrt(msq + RMS_EPS) * grm_ref[:, c0:c0 + width]

    aq_ref[...] = rm_normed(RM_AQ, GROUP, e64_ref).astype(aq_ref.dtype)
    ak_ref[...] = rm_normed(RM_AK, GROUP, e64_ref).astype(ak_ref.dtype)
    av_ref[...] = rm(RM_AV, GROUP).astype(av_ref.dtype)
    bq_ref[...] = rm_normed(RM_BQ, GROUP, e64_ref).astype(bq_ref.dtype)
    bk_ref[...] = rm_normed(RM_BK, GROUP, e64_ref).astype(bk_ref.dtype)
    bv_ref[...] = rm(RM_BV, GROUP).astype(bv_ref.dtype)
    ck_ref[...] = rm_normed(RM_CK, GROUP, e32_ref).astype(ck_ref.dtype)
    kvc_ref[...] = rm(RM_KVC, 2 * HEAD_DIM).astype(kvc_ref.dtype)
    ksw_ref[...] = rm_normed(RM_KSW, 2 * HEAD_DIM, e64_ref).astype(ksw_ref.dtype)
    gate_ref[...] = rm(RM_GATE, N_MIXERS * GROUP).astype(gate_ref.dtype)

    def tr(r0, height):
        return lax.dot_general(wt_ref[r0:r0 + height, :], xn, NT_DIMS, preferred_element_type=F32)

    def tr_normed(r0, height, group):
        h3 = tr(r0, height).reshape(height // group, group, rows)
        msq = jnp.mean(h3 * h3, axis=1, keepdims=True)
        return (h3 * lax.rsqrt(msq + RMS_EPS)).reshape(height, rows) * gt_ref[r0:r0 + height, :]

    def put(ref, val):
        for t in range(rows // QT):
            ref[t] = val[:, t * QT:(t + 1) * QT].astype(ref.dtype)

    put(cq_ref, tr_normed(TR_CQ, GROUP, DIFF_QK_DIM))
    put(cv_ref, tr(TR_CV, GROUP))
    put(dq_ref, tr_normed(TR_DQ, GROUP, HEAD_DIM))
    put(dvs_ref, tr(TR_DVS, HEAD_DIM))
    put(dvw_ref, tr(TR_DVW, HEAD_DIM))
    put(dg_ref, tr(TR_DG, 16))


def _project(x2, nw, wrm, wt, grm, gt, e64, e32):
    m, d = x2.shape
    nt = m // QT
    tpr = PROJ_ROWS // QT
    const = lambda shape: pl.BlockSpec(shape, lambda i: (0,) * len(shape))
    rm_out = lambda width, dtype: (jax.ShapeDtypeStruct((m, width), dtype),
                                   pl.BlockSpec((PROJ_ROWS, width), lambda i: (i, 0)))
    tr_out = lambda height, dtype: (jax.ShapeDtypeStruct((nt, height, QT), dtype),
                                    pl.BlockSpec((tpr, height, QT), lambda i: (i, 0, 0)))
    outs = [rm_out(GROUP, BF16)] * 7 + [rm_out(2 * HEAD_DIM, F32), rm_out(2 * HEAD_DIM, BF16),
                                        rm_out(N_MIXERS * GROUP, F32)]
    outs += [tr_out(GROUP, BF16), tr_out(GROUP, BF16), tr_out(GROUP, BF16), tr_out(HEAD_DIM, BF16),
             tr_out(HEAD_DIM, BF16), tr_out(16, F32)]
    return pl.pallas_call(
        _proj_kernel,
        grid=(m // PROJ_ROWS,),
        in_specs=[pl.BlockSpec((PROJ_ROWS, d), lambda i: (i, 0)), const((1, d)), const((d, RM_COLS)),
                  const((TR_ROWS, d)), const((1, RM_COLS)), const((TR_ROWS, 1)), const((GROUP, GROUP)),
                  const((GROUP, GROUP))],
        out_specs=[o[1] for o in outs],
        out_shape=[o[0] for o in outs],
        compiler_params=pltpu.CompilerParams(dimension_semantics=("arbitrary",), vmem_limit_bytes=VMEM_LIMIT),
        name="in_projection",
    )(x2, nw, wrm, wt, grm, gt, e64, e32)


def _band_kernel(sink_ref, q_ref, kp_ref, kc_ref, vp_ref, vc_ref, bias_ref, o_ref, lse_ref, *, use_sink):
    i = pl.program_id(2)
    q = q_ref[0]
    kp, kc, vp, vc = kp_ref[0], kc_ref[0], vp_ref[0], vc_ref[0]
    head_of_lane = lax.broadcasted_iota(jnp.int32, (BAND_TILE, GROUP), 1) // HEAD_DIM
    no_prev = jnp.where(i == 0, NEG_INF, 0.0).astype(F32)
    o_acc = jnp.zeros((BAND_TILE, GROUP), F32)
    lse_acc = jnp.zeros((BAND_TILE, GROUP), F32)
    for h in range(HEADS):
        qh = jnp.where(head_of_lane == h, q, jnp.zeros_like(q))
        sp = lax.dot_general(qh, kp, NT_DIMS, preferred_element_type=F32) + no_prev
        sc = lax.dot_general(qh, kc, NT_DIMS, preferred_element_type=F32)
        s = jnp.concatenate([sp, sc], axis=1) + bias_ref[h, 0]
        m = jnp.max(s, axis=1, keepdims=True)
        if use_sink:
            m = jnp.maximum(m, sink_ref[h])
        p = jnp.exp(s - m)
        den = jnp.sum(p, axis=1, keepdims=True)
        if use_sink:
            den = den + jnp.exp(sink_ref[h] - m)
        pv = (jnp.dot(p[:, :BAND_TILE].astype(BF16), vp, preferred_element_type=F32)
              + jnp.dot(p[:, BAND_TILE:].astype(BF16), vc, preferred_element_type=F32))
        o_acc = jnp.where(head_of_lane == h, pv / den, o_acc)
        lse_acc = jnp.where(head_of_lane == h, m + jnp.log(den), lse_acc)
    o_ref[0] = o_acc
    lse_ref[0] = lse_acc


def _banded(q, k, v, bias, sink, rate, use_sink):
    b, s, _ = q.shape
    ln = s // rate
    fold = lambda t: t.reshape(b, ln, rate * GROUP)
    cur = pl.BlockSpec((1, BAND_TILE, GROUP), lambda bb, r, i: (bb, i, r))
    prev = pl.BlockSpec((1, BAND_TILE, GROUP), lambda bb, r, i: (bb, jnp.maximum(i - 1, 0), r))
    o, lse = pl.pallas_call(
        functools.partial(_band_kernel, use_sink=use_sink),
        grid=(b, rate, ln // BAND_TILE),
        in_specs=[pl.BlockSpec(memory_space=pltpu.SMEM), cur, prev, cur, prev, cur,
                  pl.BlockSpec((HEADS, 1, BAND_TILE, 2 * BAND_TILE), lambda bb, r, i: (0, 0, 0, 0))],
        out_specs=[cur, cur],
        out_shape=[jax.ShapeDtypeStruct((b, ln, rate * GROUP), F32)] * 2,
        compiler_params=pltpu.CompilerParams(dimension_semantics=("arbitrary",) * 3),
        name=f"banded_attention_r{rate}",
    )(sink, fold(q), fold(k), fold(k), fold(v), fold(v), bias)
    return o.reshape(b, s, GROUP), lse.reshape(b, s, GROUP)


def _flash_step(s, m_old, l_old, acc_old, v_t, shift):
    m_new = jnp.maximum(m_old, jnp.max(s, axis=0, keepdims=True) + shift)
    alpha = jnp.exp(m_old - m_new)
    p = jnp.exp(s - (m_new - shift))
    l_new = alpha * l_old + jnp.sum(p, axis=0, keepdims=True)
    acc_new = alpha * acc_old + jnp.dot(v_t, p.astype(BF16), preferred_element_type=F32)
    return m_new, l_new, acc_new


def _flash_init(dv):
    return (jnp.full((1, QT), NEG_INF, F32), jnp.zeros((1, QT), F32), jnp.zeros((dv, QT), F32))


N_NEAR = -(-(FAR_DIST + KT - 1) // QT)


def _diff_kernel(far_ref, q_ref, k_ref, v_ref, bias_ref, lam_ref, subln_ref, o_ref, ot_ref, *, lambda_init):
    i = pl.program_id(1)
    q = q_ref[0]
    row = lax.broadcasted_iota(jnp.int32, (GROUP, QT), 0) // DIFF_QK_DIM
    lam_p = lam_ref[...]
    lam = (jnp.exp(jnp.sum(lam_p[0:1] * lam_p[1:2], axis=1, keepdims=True))
           - jnp.exp(jnp.sum(lam_p[2:3] * lam_p[3:4], axis=1, keepdims=True)) + lambda_init)
    n_far = jnp.maximum(i - (N_NEAR - 1), 0)
    for h in range(HEADS):
        qz = [jnp.where(row == 2 * h + c, q, jnp.zeros_like(q)) for c in range(2)]
        far = far_ref[h]

        def tile(j, carry, bias):
            k = k_ref[0, pl.ds(pl.multiple_of(j * KT, KT), KT), :]
            v_t = v_ref[0, j, h * HEAD_DIM:(h + 1) * HEAD_DIM, :]
            out = []
            for c in range(2):
                s = jnp.dot(k, qz[c], preferred_element_type=F32)
                if bias is None:
                    out.append(_flash_step(s, *carry[c], v_t, far))
                else:
                    out.append(_flash_step(s + bias, *carry[c], v_t, 0.0))
            return tuple(out)

        carry = (_flash_init(HEAD_DIM), _flash_init(HEAD_DIM))
        carry = lax.fori_loop(0, n_far, lambda j, cr: tile(j, cr, None), carry)
        carry = lax.fori_loop(n_far, i + 1, lambda j, cr: tile(j, cr, bias_ref[h, i - j]), carry)
        (_, l0, a0), (_, l1, a1) = carry
        o = a0 / l0 - lam * (a1 / l1)
        msq = jnp.mean(o * o, axis=0, keepdims=True)
        ot_ref[h * HEAD_DIM:(h + 1) * HEAD_DIM, :] = (o * lax.rsqrt(msq + RMS_EPS) * subln_ref[...]
                                                      * (1.0 - lambda_init))
    o_ref[0] = ot_ref[...].T


def _diff_attention(far, q_t, k, v_t, bias, lam_p, subln, lambda_init):
    b, s, _ = k.shape
    nq = s // QT
    nkv = s // KT
    v4 = v_t.reshape(b, nkv, GROUP, KT)
    return pl.pallas_call(
        functools.partial(_diff_kernel, lambda_init=lambda_init),
        grid=(b, nq),
        in_specs=[pl.BlockSpec(memory_space=pltpu.SMEM),
                  pl.BlockSpec((1, GROUP, QT), lambda bb, i: (bb * nq + i, 0, 0)),
                  pl.BlockSpec((1, s, GROUP), lambda bb, i: (bb, 0, 0)),
                  pl.BlockSpec((1, nkv, GROUP, KT), lambda bb, i: (bb, 0, 0, 0)),
                  pl.BlockSpec((HEADS, N_NEAR, KT, QT), lambda bb, i: (0, 0, 0, 0)),
                  pl.BlockSpec((4, DIFF_QK_DIM), lambda bb, i: (0, 0)),
                  pl.BlockSpec((HEAD_DIM, 1), lambda bb, i: (0, 0))],
        out_specs=pl.BlockSpec((1, QT, GROUP), lambda bb, i: (bb, i, 0)),
        out_shape=jax.ShapeDtypeStruct((b, s, GROUP), F32),
        scratch_shapes=[pltpu.VMEM((GROUP, QT), F32)],
        compiler_params=pltpu.CompilerParams(dimension_semantics=("arbitrary", "arbitrary"),
                                             vmem_limit_bytes=VMEM_LIMIT),
        name="diff_attention",
    )(far, q_t, k, v4, bias, lam_p, subln)


def _compress_kernel(ch_ref, ptop_ref, pbot_ref, w1t_ref, w1b_ref, b1_ref, w2k_ref, b2k_ref, w2v_ref, b2v_ref,
                     gk_ref, kc_ref, vct_ref):
    ch = ch_ref[0]
    n_c = ch.shape[0]
    u = jnp.dot((ch + ptop_ref[...]).astype(BF16), w1t_ref[...], preferred_element_type=F32)
    v = jnp.dot((ch + pbot_ref[...]).astype(BF16), w1b_ref[...], preferred_element_type=F32)
    v_next = pltpu.roll(v, n_c - 1, 0)
    hid = jax.nn.gelu(u + v_next + b1_ref[...])
    hk = hid[:, :CMP_HIDDEN].astype(BF16)
    hv = hid[:, CMP_HIDDEN:].astype(BF16)
    kc = jnp.dot(hk, w2k_ref[...], preferred_element_type=F32) + b2k_ref[...]
    msq = jnp.mean(kc * kc, axis=-1, keepdims=True)
    kc_ref[0] = (kc * lax.rsqrt(msq + RMS_EPS) * gk_ref[...]).astype(kc_ref.dtype)
    vct = lax.dot_general(w2v_ref[...], hv, NT_DIMS, preferred_element_type=F32) + b2v_ref[...]
    vct_ref[0] = vct.astype(vct_ref.dtype)


def _compress(chunks, ptop, pbot, w1t, w1b, b1, w2k, b2k, w2v, b2v, gk):
    b, n_c, width = chunks.shape
    const = lambda a: pl.BlockSpec(a.shape, lambda bb: (0,) * a.ndim)
    params = (ptop, pbot, w1t, w1b, b1, w2k, b2k, w2v, b2v, gk)
    return pl.pallas_call(
        _compress_kernel,
        grid=(b,),
        in_specs=[pl.BlockSpec((1, n_c, width), lambda bb: (bb, 0, 0))] + [const(a) for a in params],
        out_specs=[pl.BlockSpec((1, n_c, HEAD_DIM), lambda bb: (bb, 0, 0)),
                   pl.BlockSpec((1, HEAD_DIM, n_c), lambda bb: (bb, 0, 0))],
        out_shape=[jax.ShapeDtypeStruct((b, n_c, HEAD_DIM), BF16), jax.ShapeDtypeStruct((b, HEAD_DIM, n_c), BF16)],
        compiler_params=pltpu.CompilerParams(dimension_semantics=("arbitrary",), vmem_limit_bytes=VMEM_LIMIT),
        name="nsa_compress",
    )(chunks, *params)


def _cmp_attn_kernel(q_ref, kc_ref, vct_ref, bias_ref, o_ref, sel_ref, p_ref, *, n_sel):
    i = pl.program_id(0)
    kc = kc_ref[0]
    vct = vct_ref[0]
    n_c = kc.shape[0]
    n_blk = sel_ref.shape[1]
    psum = jnp.zeros((n_c, QT), F32)
    for h in range(HEADS):
        qh = q_ref[0, h * HEAD_DIM:(h + 1) * HEAD_DIM, :]
        s = jnp.dot(kc, qh, preferred_element_type=F32) + bias_ref[h, 0]
        m = jnp.maximum(jnp.max(s, axis=0, keepdims=True), 0.5 * NEG_INF)
        p = jnp.exp(s - m)
        den = jnp.sum(p, axis=0, keepdims=True)
        p = p / jnp.maximum(den, TINY)
        o_ref[0, h * HEAD_DIM:(h + 1) * HEAD_DIM, :] = jnp.dot(vct, p.astype(BF16), preferred_element_type=F32)
        psum = psum + p
    per_blk = SLC_BLOCK // CMP_STRIDE
    halves = []
    for half in range(QT // LANES):
        p_ref[half, 0:8, :] = jnp.zeros((8, LANES), F32)
        p_ref[half, 8:8 + n_c, :] = psum[:, half * LANES:(half + 1) * LANES]
        p_ref[half, 8 + n_c:16 + n_c, :] = jnp.zeros((8, LANES), F32)
        acc = p_ref[half, pl.ds(7, n_blk, stride=per_blk), :]
        for t in range(per_blk):
            acc = acc + p_ref[half, pl.ds(8 + t, n_blk, stride=per_blk), :]
        halves.append(acc)
    imp = jnp.concatenate(halves, axis=1)
    blk = lax.broadcasted_iota(jnp.int32, (n_blk, QT), 0)
    cur = (i * QT + lax.broadcasted_iota(jnp.int32, (n_blk, QT), 1)) // SLC_BLOCK
    forced = (blk == 0) | (blk == cur) | (blk == cur - 1)
    val = jnp.where(forced, FORCE_SELECT, jnp.where(blk <= cur, imp, NEG_INF))
    sel = jnp.zeros((n_blk, QT), jnp.bool_)
    for _ in range(n_sel):
        top = jnp.max(val, axis=0, keepdims=True)
        idx = jnp.min(jnp.where(val == top, blk, n_blk), axis=0, keepdims=True)
        hit = blk == idx
        sel = sel | hit
        val = jnp.where(hit, -3.0e38, val)
    sel_ref[0] = jnp.where(sel, 1.0, 0.0).astype(sel_ref.dtype)


def _cmp_attention(q_t, kc, vct, bias, b):
    nt = q_t.shape[0]
    nq = nt // b
    n_c = kc.shape[1]
    n_blk = nq * QT // SLC_BLOCK
    return pl.pallas_call(
        functools.partial(_cmp_attn_kernel, n_sel=min(SLC_TOPK, n_blk)),
        grid=(nq, b),
        in_specs=[pl.BlockSpec((1, GROUP, QT), lambda i, bb: (bb * nq + i, 0, 0)),
                  pl.BlockSpec((1, n_c, HEAD_DIM), lambda i, bb: (bb, 0, 0)),
                  pl.BlockSpec((1, HEAD_DIM, n_c), lambda i, bb: (bb, 0, 0)),
                  pl.BlockSpec((HEADS, 1, n_c, QT), lambda i, bb: (0, 0, 0, i))],
        out_specs=[pl.BlockSpec((1, GROUP, QT), lambda i, bb: (bb * nq + i, 0, 0)),
                   pl.BlockSpec((1, n_blk, QT), lambda i, bb: (bb * nq + i, 0, 0))],
        out_shape=[jax.ShapeDtypeStruct((nt, GROUP, QT), F32), jax.ShapeDtypeStruct((nt, n_blk, QT), BF16)],
        scratch_shapes=[pltpu.VMEM((QT // LANES, n_c + 16, LANES), F32)],
        compiler_params=pltpu.CompilerParams(dimension_semantics=("arbitrary", "arbitrary"),
                                             vmem_limit_bytes=VMEM_LIMIT),
        name="nsa_compressed_attention",
    )(q_t, kc, vct, bias)


SEL_REP = 8
N_WIN = -(-(NSA_WINDOW - 1 + KT - 1) // QT)


def _slc_win_kernel(far_ref, q_ref, ksw_ref, vs_ref, vw_ref, sel_ref, rep_ref, ocmp_ref, g_ref, bslc_ref, bwin_ref,
                    o_ref, ot_ref, mask_ref):
    i = pl.program_id(1)
    sel8 = jnp.dot(rep_ref[...], sel_ref[0], preferred_element_type=F32)
    mask_ref[...] = (sel8 - 1.0) * (-NEG_INF)
    blocks_per_tile = KT // SLC_BLOCK
    mrows = blocks_per_tile * SEL_REP
    n_far = jnp.maximum(i - (N_NEAR - 1), 0)
    zeros = jnp.zeros((HEAD_DIM, QT), BF16)
    for h in range(HEADS):
        qh = q_ref[0, h * HEAD_DIM:(h + 1) * HEAD_DIM, :]
        q_slc = jnp.concatenate([qh, zeros], axis=0)
        q_win = jnp.concatenate([zeros, qh], axis=0)
        far = far_ref[h]

        def slc_tile(j, carry, bias):
            k = ksw_ref[0, pl.ds(pl.multiple_of(j * KT, KT), KT), :]
            m8 = mask_ref[pl.ds(pl.multiple_of(j * mrows, mrows), mrows), :]
            mask = jnp.broadcast_to(m8.reshape(blocks_per_tile, 1, SEL_REP, QT),
                                    (blocks_per_tile, SLC_BLOCK // SEL_REP, SEL_REP, QT)).reshape(KT, QT)
            s = jnp.dot(k, q_slc, preferred_element_type=F32) + mask
            if bias is None:
                return _flash_step(s, *carry, vs_ref[0, j], far)
            return _flash_step(s + bias, *carry, vs_ref[0, j], 0.0)

        carry = lax.fori_loop(0, n_far, lambda j, cr: slc_tile(j, cr, None), _flash_init(HEAD_DIM))
        _, l_s, a_s = lax.fori_loop(n_far, i + 1, lambda j, cr: slc_tile(j, cr, bslc_ref[h, i - j]), carry)

        def win_tile(j, carry):
            k = ksw_ref[0, pl.ds(pl.multiple_of(j * KT, KT), KT), :]
            s = jnp.dot(k, q_win, preferred_element_type=F32) + bwin_ref[h, i - j]
            return _flash_step(s, *carry, vw_ref[0, j], 0.0)

        _, l_w, a_w = lax.fori_loop(jnp.maximum(i - (N_WIN - 1), 0), i + 1, win_tile, _flash_init(HEAD_DIM))
        g = jax.nn.sigmoid(g_ref[0, 3 * h:3 * h + 3, :])
        ot_ref[h * HEAD_DIM:(h + 1) * HEAD_DIM, :] = (g[0:1] * ocmp_ref[0, h * HEAD_DIM:(h + 1) * HEAD_DIM, :]
                                                      + g[1:2] * (a_s / l_s) + g[2:3] * (a_w / l_w))
    o_ref[0] = ot_ref[...].T


def _slc_win_attention(far, q_t, ksw, vs_t, vw_t, sel, rep, ocmp, g_t, bslc, bwin):
    b, s, _ = ksw.shape
    nq = s // QT
    nkv = s // KT
    n_blk = s // SLC_BLOCK
    tile = lambda height: pl.BlockSpec((1, height, QT), lambda bb, i: (bb * nq + i, 0, 0))
    whole = lambda a: pl.BlockSpec(a.shape, lambda bb, i: (0,) * a.ndim)
    return pl.pallas_call(
        _slc_win_kernel,
        grid=(b, nq),
        in_specs=[pl.BlockSpec(memory_space=pltpu.SMEM), tile(GROUP),
                  pl.BlockSpec((1, s, 2 * HEAD_DIM), lambda bb, i: (bb, 0, 0)),
                  pl.BlockSpec((1, nkv, HEAD_DIM, KT), lambda bb, i: (bb, 0, 0, 0)),
                  pl.BlockSpec((1, nkv, HEAD_DIM, KT), lambda bb, i: (bb, 0, 0, 0)),
                  tile(n_blk), whole(rep), tile(GROUP), tile(16), whole(bslc), whole(bwin)],
        out_specs=pl.BlockSpec((1, QT, GROUP), lambda bb, i: (bb, i, 0)),
        out_shape=jax.ShapeDtypeStruct((b, s, GROUP), F32),
        scratch_shapes=[pltpu.VMEM((GROUP, QT), F32), pltpu.VMEM((n_blk * SEL_REP, QT), F32)],
        compiler_params=pltpu.CompilerParams(dimension_semantics=("arbitrary", "arbitrary"),
                                             vmem_limit_bytes=VMEM_LIMIT),
        name="nsa_selected_window_attention",
    )(far, q_t, ksw, vs_t.reshape(b, nkv, HEAD_DIM, KT), vw_t.reshape(b, nkv, HEAD_DIM, KT), sel, rep, ocmp, g_t,
      bslc, bwin)


def _out_kernel(x_ref, a0_ref, a1_ref, a2_ref, l0_ref, l1_ref, l2_ref, ob_ref, oc_ref, od_ref, gate_ref, w_ref,
                o_ref):
    l0, l1, l2 = l0_ref[...], l1_ref[...], l2_ref[...]
    mx = jnp.maximum(jnp.maximum(l0, l1), l2)
    e0, e1, e2 = jnp.exp(l0 - mx), jnp.exp(l1 - mx), jnp.exp(l2 - mx)
    den = e0 + e1 + e2
    o_a = (e0 / den) * a0_ref[...] + (e1 / den) * a1_ref[...] + (e2 / den) * a2_ref[...]
    y = jnp.concatenate([o_a, ob_ref[...], oc_ref[...], od_ref[...]], axis=1)
    g = gate_ref[...]
    y = y * (g * jax.nn.sigmoid(g))
    o_ref[...] = x_ref[...] + jnp.dot(y.astype(BF16), w_ref[...], preferred_element_type=F32)


def _out_projection(x2, a_outs, a_lses, o_b, o_c, o_d, gate, w_out):
    m, d = x2.shape
    rowblk = lambda width: pl.BlockSpec((PROJ_ROWS, width), lambda i: (i, 0))
    return pl.pallas_call(
        _out_kernel,
        grid=(m // PROJ_ROWS,),
        in_specs=[rowblk(d)] + [rowblk(GROUP)] * 9 + [rowblk(N_MIXERS * GROUP),
                                                     pl.BlockSpec((N_MIXERS * GROUP, d), lambda i: (0, 0))],
        out_specs=rowblk(d),
        out_shape=jax.ShapeDtypeStruct((m, d), F32),
        compiler_params=pltpu.CompilerParams(dimension_semantics=("arbitrary",), vmem_limit_bytes=VMEM_LIMIT),
        name="out_projection",
    )(x2, *a_outs, *a_lses, o_b, o_c, o_d, gate, w_out)


def _block_diag_mean(group):
    idx = np.arange(GROUP) // group
    return jnp.asarray((idx[:, None] == idx[None, :]).astype(np.float32) / group, BF16)


def _layer_weights(w_in, qk_gain, qk_gain_diff):
    d = w_in.shape[0]
    sizes = (GROUP,) * 3 + (GROUP, GROUP // 2, GROUP // 2) + (GROUP,) * 3 + (GROUP,) + (HEAD_DIM,) * 6 \
        + (HEADS * 3, N_MIXERS * GROUP)
    offs = np.concatenate([[0], np.cumsum(sizes)])
    col = lambda n: w_in[:, offs[n]:offs[n + 1]]
    (a_q, a_k, a_v, b_q, b_k, b_v, c_q, c_k, c_v, d_q, d_kc, d_vc, d_ks, d_vs, d_kw, d_vw, d_g, gate) = \
        [col(n) for n in range(18)]
    rep_kv = lambda w: jnp.repeat(w.reshape(d, 2, HEAD_DIM), 2, axis=1).reshape(d, GROUP)
    wrm = jnp.concatenate([a_q, a_k, a_v, b_q, rep_kv(b_k), rep_kv(b_v), c_k, d_kc, d_vc, d_ks, d_kw, gate], axis=1)
    wt = jnp.concatenate([c_q, c_v, d_q, d_vs, d_vw, d_g, jnp.zeros((d, 16 - HEADS * 3), w_in.dtype)], axis=1).T
    g = qk_gain
    ones = lambda n: jnp.ones((n,), F32)
    tile4 = lambda v: jnp.tile(v, HEADS)
    scale = HEAD_DIM ** -0.5
    grm = jnp.concatenate([tile4(g[0]) * scale, tile4(g[1]), ones(GROUP), tile4(g[2]) * scale, tile4(g[3]),
                           ones(GROUP), jnp.tile(qk_gain_diff[1], 2 * HEADS), ones(2 * HEAD_DIM), g[6], g[7],
                           ones(N_MIXERS * GROUP)])
    gt = jnp.concatenate([jnp.tile(qk_gain_diff[0], 2 * HEADS) * DIFF_QK_DIM ** -0.5, ones(GROUP),
                          tile4(g[4]) * scale, ones(2 * HEAD_DIM + 16)])
    return wrm.astype(BF16), wt.astype(BF16), grm.reshape(1, -1), gt.reshape(-1, 1)


def _compress_weights(cmp_pos, cmp_w1, cmp_b1, cmp_w2, cmp_b2):
    half = CMP_LEN // 2
    pos = jnp.concatenate([cmp_pos[0], cmp_pos[1]], axis=-1)
    ptop = pos[:half].reshape(1, -1)
    pbot = pos[half:].reshape(1, -1)
    w1 = cmp_w1.reshape(2, CMP_LEN, HEAD_DIM, CMP_HIDDEN)
    zeros = jnp.zeros_like(w1[0])
    w1cat = jnp.concatenate([jnp.concatenate([w1[0], zeros], axis=-1),
                             jnp.concatenate([zeros, w1[1]], axis=-1)], axis=1)
    w1t = w1cat[:half].reshape(half * 2 * HEAD_DIM, 2 * CMP_HIDDEN).astype(BF16)
    w1b = w1cat[half:].reshape(half * 2 * HEAD_DIM, 2 * CMP_HIDDEN).astype(BF16)
    b1 = jnp.concatenate([cmp_b1[0], cmp_b1[1]]).reshape(1, -1)
    return (ptop, pbot, w1t, w1b, b1, cmp_w2[0].astype(BF16), cmp_b2[0].reshape(1, -1),
            cmp_w2[1].T.astype(BF16), cmp_b2[1].reshape(-1, 1))


def kernel(x, rel_bias_table, norm_w, w_in, w_out, qk_gain, qk_gain_diff, attn_sinks, diff_lambda, diff_subln,
           cmp_pos, cmp_w1, cmp_b1, cmp_w2, cmp_b2):
    b, s, d = x.shape
    depth = w_in.shape[0]
    n_c = s // CMP_STRIDE
    n_blk = s // SLC_BLOCK
    assert s % (BAND_TILE * DILATED_CONFIGS[-1][1]) == 0 and s % PROJ_ROWS == 0 and d == N_MIXERS * GROUP

    table = rel_bias_table.astype(F32)
    band_bias = [_build_bias(table, head0=0, n_d=1, rows=BAND_TILE, cols=2 * BAND_TILE, base0=BAND_TILE, dstep=0,
                             rs=1, cs=-1, dscale=rate, max_dist=window // rate) for window, rate in DILATED_CONFIGS]
    swa_bias = _build_bias(table, head0=HEADS, n_d=1, rows=BAND_TILE, cols=2 * BAND_TILE, base0=BAND_TILE, dstep=0,
                           rs=1, cs=-1, max_dist=SWA_WINDOW - 1)
    diff_bias = _build_bias(table, head0=2 * HEADS, n_d=N_NEAR, rows=KT, cols=QT, base0=0, dstep=QT, rs=-1, cs=1)
    slc_bias = _build_bias(table, head0=3 * HEADS, n_d=N_NEAR, rows=KT, cols=QT, base0=0, dstep=QT, rs=-1, cs=1)
    win_bias = _build_bias(table, head0=3 * HEADS, n_d=N_WIN, rows=KT, cols=QT, base0=0, dstep=QT, rs=-1, cs=1,
                           max_dist=NSA_WINDOW - 1)
    cmp_bias = _build_bias(table, head0=3 * HEADS, n_d=1, rows=n_c, cols=s, base0=-(CMP_LEN - 1), dstep=0,
                           rs=-CMP_STRIDE, cs=1, r_valid=n_c - 1, col_tile=4 * QT)
    far_c = table[NUM_BUCKETS - 1, 2 * HEADS:3 * HEADS]
    far_d = table[NUM_BUCKETS - 1, 3 * HEADS:4 * HEADS]
    e64, e32 = _block_diag_mean(HEAD_DIM), _block_diag_mean(DIFF_QK_DIM)
    rep_idx = np.arange(n_blk * SEL_REP) // SEL_REP
    rep = jnp.asarray((rep_idx[:, None] == np.arange(n_blk)[None, :]).astype(np.float32), BF16)
    no_sink = jnp.zeros((HEADS,), F32)

    x2 = x.reshape(b * s, d)
    for layer in range(depth):
        wrm, wt, grm, gt = _layer_weights(w_in[layer], qk_gain[layer], qk_gain_diff[layer])
        (a_q, a_k, a_v, b_q, b_k, b_v, c_k, kvc, ksw, gate, c_qt, c_vt, d_qt, d_vst, d_vwt, d_gt) = _project(
            x2, norm_w[layer].reshape(1, d), wrm, wt, grm, gt, e64, e32)
        seq = lambda t: t.reshape(b, s, t.shape[-1])
        a_res = [_banded(seq(a_q), seq(a_k), seq(a_v), band_bias[n], no_sink, rate, False)
                 for n, (_, rate) in enumerate(DILATED_CONFIGS)]
        o_b, _ = _banded(seq(b_q), seq(b_k), seq(b_v), swa_bias, attn_sinks[layer].astype(F32), 1, True)
        lambda_init = 0.8 - 0.6 * math.exp(-0.3 * layer)
        o_c = _diff_attention(far_c, c_qt, seq(c_k), c_vt, diff_bias, diff_lambda[layer].astype(F32),
                              diff_subln[layer].reshape(HEAD_DIM, 1).astype(F32), lambda_init)
        cw = _compress_weights(cmp_pos[layer], cmp_w1[layer], cmp_b1[layer], cmp_w2[layer], cmp_b2[layer])
        kc, vct = _compress(kvc.reshape(b, n_c, CMP_STRIDE * 2 * HEAD_DIM), *cw, qk_gain[layer, 5].reshape(1, -1))
        o_cmp, sel = _cmp_attention(d_qt, kc, vct, cmp_bias, b)
        o_d = _slc_win_attention(far_d, d_qt, seq(ksw), d_vst, d_vwt, sel, rep, o_cmp, d_gt, slc_bias, win_bias)
        x2 = _out_projection(x2, [r[0].reshape(b * s, GROUP) for r in a_res],
                             [r[1].reshape(b * s, GROUP) for r in a_res], o_b.reshape(b * s, GROUP),
                             o_c.reshape(b * s, GROUP), o_d.reshape(b * s, GROUP), gate, w_out[layer].astype(BF16))
    return x2.reshape(b, s, d)
```

```python
import functools
import math

import numpy as np
import jax
import jax.numpy as jnp
from jax import lax
from jax.experimental import pallas as pl
from jax.experimental.pallas import tpu as pltpu

F32 = jnp.float32
BF16 = jnp.bfloat16

HEAD_DIM = 64
HEADS = 4
GROUP = HEADS * HEAD_DIM
N_MIXERS = 4
NUM_BUCKETS = 32
REL_MAX_DIST = 2048
DILATED_CONFIGS = ((128, 1), (512, 4), (2048, 16))
SWA_WINDOW = 128
DIFF_QK_DIM = HEAD_DIM // 2
CMP_LEN = 32
CMP_STRIDE = 16
CMP_HIDDEN = 256
SLC_BLOCK = 64
SLC_TOPK = 16
NSA_WINDOW = 512
RMS_EPS = 1e-6
NEG_INF = -1e30
FORCE_SELECT = 1e9
TINY = 1e-30

PROJ_ROWS = 512
BAND_TILE = 128
LANES = 128
QT = 256
KT = 256
VMEM_LIMIT = 56 * 1024 * 1024
MXU_LOOKAHEAD = 4

NT_DIMS = (((1,), (1,)), ((), ()))


def _t5_thresholds():
    n = np.arange(0, 4 * REL_MAX_DIST)
    max_exact = NUM_BUCKETS // 2
    nf = np.maximum(n, 1).astype(np.float32)
    large = max_exact + (np.log(nf / np.float32(max_exact)) / np.float32(math.log(REL_MAX_DIST / max_exact))
                         * np.float32(NUM_BUCKETS - max_exact)).astype(np.int32)
    bucket = np.where(n < max_exact, n, np.minimum(large, NUM_BUCKETS - 1))
    return [int(np.argmax(bucket >= b)) for b in range(NUM_BUCKETS)]


T5_THRESHOLDS = _t5_thresholds()
FAR_DIST = T5_THRESHOLDS[-1]


def _bias_kernel(tbl_ref, out_ref, *, head0, base0, dstep, rs, cs, dscale, max_dist, r_valid, col_tile):
    h = pl.program_id(0)
    d = pl.program_id(1)
    ct = pl.program_id(2)
    rows, cols = out_ref.shape[-2:]
    r = lax.broadcasted_iota(jnp.int32, (rows, cols), 0)
    c = lax.broadcasted_iota(jnp.int32, (rows, cols), 1) + ct * col_tile
    dist = base0 + d * dstep + r * rs + c * cs
    n = dist * dscale
    val = jnp.full((rows, cols), tbl_ref[0, head0 + h], F32)
    for b in range(1, NUM_BUCKETS):
        val = jnp.where(n >= T5_THRESHOLDS[b], tbl_ref[b, head0 + h], val)
    valid = (dist >= 0) & (dist <= max_dist) & (r < r_valid)
    out_ref[0, 0] = jnp.where(valid, val, NEG_INF)


def _build_bias(table, *, head0, n_d, rows, cols, base0, dstep, rs, cs, dscale=1, max_dist=1 << 30,
                r_valid=1 << 30, col_tile=None):
    col_tile = cols if col_tile is None else col_tile
    kern = functools.partial(_bias_kernel, head0=head0, base0=base0, dstep=dstep, rs=rs, cs=cs, dscale=dscale,
                             max_dist=max_dist, r_valid=r_valid, col_tile=col_tile)
    return pl.pallas_call(
        kern,
        grid=(HEADS, n_d, cols // col_tile),
        in_specs=[pl.BlockSpec(memory_space=pltpu.SMEM)],
        out_specs=pl.BlockSpec((1, 1, rows, col_tile), lambda h, d, c: (h, d, 0, c)),
        out_shape=jax.ShapeDtypeStruct((HEADS, n_d, rows, cols), F32),
        name="rel_bias_tiles",
    )(table)


RM_AQ, RM_AK, RM_AV = 0, 256, 512
RM_BQ, RM_BK, RM_BV = 768, 1024, 1280
RM_CK = 1536
RM_KVC = 1792
RM_KSW = 1920
RM_GATE = 2048
RM_COLS = 3072
TR_CQ, TR_CV, TR_DQ, TR_DVS, TR_DVW, TR_DG = 0, 256, 512, 768, 832, 896
TR_ROWS = 912


def _proj_kernel(x_ref, nw_ref, wrm_ref, wt_ref, grm_ref, gt_ref, e64_ref, e32_ref,
                 aq_ref, ak_ref, av_ref, bq_ref, bk_ref, bv_ref, ck_ref, kvc_ref, ksw_ref, gate_ref,
                 cq_ref, cv_ref, dq_ref, dvs_ref, dvw_ref, dg_ref):
    x = x_ref[...]
    ms = jnp.mean(x * x, axis=-1, keepdims=True)
    xn = (x * lax.rsqrt(ms + RMS_EPS) * nw_ref[...]).astype(BF16)
    rows = x.shape[0]

    def rm(c0, width):
        return jnp.dot(xn, wrm_ref[:, c0:c0 + width], preferred_element_type=F32)

    def rm_normed(c0, width, e_ref):
        h = rm(c0, width)
        msq = jnp.dot((h * h).astype(BF16), e_ref[0:width, 0:width], preferred_element_type=F32)
        return h * lax.rsqrt(msq + RMS_EPS) * grm_ref[:, c0:c0 + width]

    aq_ref[...] = rm_normed(RM_AQ, GROUP, e64_ref).astype(aq_ref.dtype)
    ak_ref[...] = rm_normed(RM_AK, GROUP, e64_ref).astype(ak_ref.dtype)
    av_ref[...] = rm(RM_AV, GROUP).astype(av_ref.dtype)
    bq_ref[...] = rm_normed(RM_BQ, GROUP, e64_ref).astype(bq_ref.dtype)
    bk_ref[...] = rm_normed(RM_BK, GROUP, e64_ref).astype(bk_ref.dtype)
    bv_ref[...] = rm(RM_BV, GROUP).astype(bv_ref.dtype)
    ck_ref[...] = rm_normed(RM_CK, GROUP, e32_ref).astype(ck_ref.dtype)
    kvc_ref[...] = rm(RM_KVC, 2 * HEAD_DIM).astype(kvc_ref.dtype)
    ksw_ref[...] = rm_normed(RM_KSW, 2 * HEAD_DIM, e64_ref).astype(ksw_ref.dtype)
    gate_ref[...] = rm(RM_GATE, N_MIXERS * GROUP).astype(gate_ref.dtype)

    def tr(r0, height):
        return lax.dot_general(wt_ref[r0:r0 + height, :], xn, NT_DIMS, preferred_element_type=F32)

    def tr_normed(r0, height, group):
        h3 = tr(r0, height).reshape(height // group, group, rows)
        msq = jnp.mean(h3 * h3, axis=1, keepdims=True)
        return (h3 * lax.rsqrt(msq + RMS_EPS)).reshape(height, rows) * gt_ref[r0:r0 + height, :]

    def put(ref, val):
        for t in range(rows // QT):
            ref[t] = val[:, t * QT:(t + 1) * QT].astype(ref.dtype)

    put(cq_ref, tr_normed(TR_CQ, GROUP, DIFF_QK_DIM))
    put(cv_ref, tr(TR_CV, GROUP))
    put(dq_ref, tr_normed(TR_DQ, GROUP, HEAD_DIM))
    put(dvs_ref, tr(TR_DVS, HEAD_DIM))
    put(dvw_ref, tr(TR_DVW, HEAD_DIM))
    put(dg_ref, tr(TR_DG, 16))


def _project(x2, nw, wrm, wt, grm, gt, e64, e32):
    m, d = x2.shape
    nt = m // QT
    tpr = PROJ_ROWS // QT
    const = lambda shape: pl.BlockSpec(shape, lambda i: (0,) * len(shape))
    rm_out = lambda width, dtype: (jax.ShapeDtypeStruct((m, width), dtype),
                                   pl.BlockSpec((PROJ_ROWS, width), lambda i: (i, 0)))
    tr_out = lambda height, dtype: (jax.ShapeDtypeStruct((nt, height, QT), dtype),
                                    pl.BlockSpec((tpr, height, QT), lambda i: (i, 0, 0)))
    outs = [rm_out(GROUP, BF16)] * 7 + [rm_out(2 * HEAD_DIM, F32), rm_out(2 * HEAD_DIM, BF16),
                                        rm_out(N_MIXERS * GROUP, F32)]
    outs += [tr_out(GROUP, BF16), tr_out(GROUP, BF16), tr_out(GROUP, BF16), tr_out(HEAD_DIM, BF16),
             tr_out(HEAD_DIM, BF16), tr_out(16, F32)]
    return pl.pallas_call(
        _proj_kernel,
        grid=(m // PROJ_ROWS,),
        in_specs=[pl.BlockSpec((PROJ_ROWS, d), lambda i: (i, 0)), const((1, d)), const((d, RM_COLS)),
                  const((TR_ROWS, d)), const((1, RM_COLS)), const((TR_ROWS, 1)), const((GROUP, GROUP)),
                  const((GROUP, GROUP))],
        out_specs=[o[1] for o in outs],
        out_shape=[o[0] for o in outs],
        compiler_params=pltpu.CompilerParams(dimension_semantics=("arbitrary",), vmem_limit_bytes=VMEM_LIMIT),
        name="in_projection",
    )(x2, nw, wrm, wt, grm, gt, e64, e32)


def _band_kernel(sink_ref, q_ref, kp_ref, kc_ref, vp_ref, vc_ref, bias_ref, o_ref, lse_ref, *, use_sink):
    i = pl.program_id(2)
    q = q_ref[0]
    kp, kc, vp, vc = kp_ref[0], kc_ref[0], vp_ref[0], vc_ref[0]
    head_of_lane = lax.broadcasted_iota(jnp.int32, (BAND_TILE, GROUP), 1) // HEAD_DIM
    no_prev = jnp.where(i == 0, NEG_INF, 0.0).astype(F32)
    o_acc = jnp.zeros((BAND_TILE, GROUP), F32)
    lse_acc = jnp.zeros((BAND_TILE, GROUP), F32)
    for h in range(HEADS):
        qh = jnp.where(head_of_lane == h, q, jnp.zeros_like(q))
        sp = lax.dot_general(qh, kp, NT_DIMS, preferred_element_type=F32) + no_prev
        sc = lax.dot_general(qh, kc, NT_DIMS, preferred_element_type=F32)
        s = jnp.concatenate([sp, sc], axis=1) + bias_ref[h, 0]
        m = jnp.max(s, axis=1, keepdims=True)
        if use_sink:
            m = jnp.maximum(m, sink_ref[h])
        p = jnp.exp(s - m)
        den = jnp.sum(p, axis=1, keepdims=True)
        if use_sink:
            den = den + jnp.exp(sink_ref[h] - m)
        pv = (jnp.dot(p[:, :BAND_TILE].astype(BF16), vp, preferred_element_type=F32)
              + jnp.dot(p[:, BAND_TILE:].astype(BF16), vc, preferred_element_type=F32))
        o_acc = jnp.where(head_of_lane == h, pv / den, o_acc)
        lse_acc = jnp.where(head_of_lane == h, m + jnp.log(den), lse_acc)
    o_ref[0] = o_acc
    lse_ref[0] = lse_acc


def _banded(q, k, v, bias, sink, rate, use_sink):
    b, s, _ = q.shape
    ln = s // rate
    fold = lambda t: t.reshape(b, ln, rate * GROUP)
    cur = pl.BlockSpec((1, BAND_TILE, GROUP), lambda bb, r, i: (bb, i, r))
    prev = pl.BlockSpec((1, BAND_TILE, GROUP), lambda bb, r, i: (bb, jnp.maximum(i - 1, 0), r))
    o, lse = pl.pallas_call(
        functools.partial(_band_kernel, use_sink=use_sink),
        grid=(b, rate, ln // BAND_TILE),
        in_specs=[pl.BlockSpec(memory_space=pltpu.SMEM), cur, prev, cur, prev, cur,
                  pl.BlockSpec((HEADS, 1, BAND_TILE, 2 * BAND_TILE), lambda bb, r, i: (0, 0, 0, 0))],
        out_specs=[cur, cur],
        out_shape=[jax.ShapeDtypeStruct((b, ln, rate * GROUP), F32)] * 2,
        compiler_params=pltpu.CompilerParams(dimension_semantics=("arbitrary",) * 3),
        name=f"banded_attention_r{rate}",
    )(sink, fold(q), fold(k), fold(k), fold(v), fold(v), bias)
    return o.reshape(b, s, GROUP), lse.reshape(b, s, GROUP)


def _flash_reset(m_ref, l_ref, acc_ref):
    m_ref[...] = jnp.full(m_ref.shape, NEG_INF, F32)
    l_ref[...] = jnp.zeros(l_ref.shape, F32)
    acc_ref[...] = jnp.zeros(acc_ref.shape, F32)


def _flash_update(n, s, v_t, shift, m_ref, l_ref, acc_ref):
    m_old = m_ref[n]
    m_new = jnp.maximum(m_old, jnp.max(s, axis=0, keepdims=True) + shift)
    alpha = jnp.exp(m_old - m_new)
    p = jnp.exp(s - (m_new - shift))
    l_ref[n] = alpha * l_ref[n] + jnp.sum(p, axis=0, keepdims=True)
    acc_ref[n] = alpha * acc_ref[n] + jnp.dot(v_t, p.astype(BF16), preferred_element_type=F32)
    m_ref[n] = m_new


def _staggered(n_chains, scores, update):
    pending = {n: scores(n) for n in range(min(MXU_LOOKAHEAD, n_chains))}
    for n in range(n_chains):
        if n + MXU_LOOKAHEAD < n_chains:
            pending[n + MXU_LOOKAHEAD] = scores(n + MXU_LOOKAHEAD)
        update(n, pending.pop(n))


def _flash_scratch(chains, dv):
    return [pltpu.VMEM((chains, 1, QT), F32), pltpu.VMEM((chains, 1, QT), F32), pltpu.VMEM((chains, dv, QT), F32)]


N_NEAR = -(-(FAR_DIST + KT - 1) // QT)


def _diff_kernel(far_ref, q_ref, k_ref, v_ref, bias_ref, lam_ref, subln_ref, o_ref, qz_ref, m_ref, l_ref, acc_ref,
                 ot_ref, *, lambda_init):
    i = pl.program_id(1)
    q = q_ref[0]
    row = lax.broadcasted_iota(jnp.int32, (GROUP, QT), 0) // DIFF_QK_DIM
    for n in range(2 * HEADS):
        qz_ref[n] = jnp.where(row == n, q, jnp.zeros_like(q))
    _flash_reset(m_ref, l_ref, acc_ref)

    def tile(j, near):
        k = k_ref[0, pl.ds(pl.multiple_of(j * KT, KT), KT), :]

        def scores(n):
            s = jnp.dot(k, qz_ref[n], preferred_element_type=F32)
            return s + bias_ref[n // 2, i - j] if near else s

        def update(n, s):
            h = n // 2
            v_t = v_ref[0, j, h * HEAD_DIM:(h + 1) * HEAD_DIM, :]
            _flash_update(n, s, v_t, 0.0 if near else far_ref[h], m_ref, l_ref, acc_ref)

        _staggered(2 * HEADS, scores, update)

    n_far = jnp.maximum(i - (N_NEAR - 1), 0)
    lax.fori_loop(0, n_far, lambda j, _: tile(j, False), None)
    lax.fori_loop(n_far, i + 1, lambda j, _: tile(j, True), None)

    lam_p = lam_ref[...]
    lam = (jnp.exp(jnp.sum(lam_p[0:1] * lam_p[1:2], axis=1, keepdims=True))
           - jnp.exp(jnp.sum(lam_p[2:3] * lam_p[3:4], axis=1, keepdims=True)) + lambda_init)
    for h in range(HEADS):
        o = acc_ref[2 * h] / l_ref[2 * h] - lam * (acc_ref[2 * h + 1] / l_ref[2 * h + 1])
        msq = jnp.mean(o * o, axis=0, keepdims=True)
        ot_ref[h * HEAD_DIM:(h + 1) * HEAD_DIM, :] = (o * lax.rsqrt(msq + RMS_EPS) * subln_ref[...]
                                                      * (1.0 - lambda_init))
    o_ref[0] = ot_ref[...].T


def _diff_attention(far, q_t, k, v_t, bias, lam_p, subln, lambda_init):
    b, s, _ = k.shape
    nq = s // QT
    nkv = s // KT
    v4 = v_t.reshape(b, nkv, GROUP, KT)
    return pl.pallas_call(
        functools.partial(_diff_kernel, lambda_init=lambda_init),
        grid=(b, nq),
        in_specs=[pl.BlockSpec(memory_space=pltpu.SMEM),
                  pl.BlockSpec((1, GROUP, QT), lambda bb, i: (bb * nq + i, 0, 0)),
                  pl.BlockSpec((1, s, GROUP), lambda bb, i: (bb, 0, 0)),
                  pl.BlockSpec((1, nkv, GROUP, KT), lambda bb, i: (bb, 0, 0, 0)),
                  pl.BlockSpec((HEADS, N_NEAR, KT, QT), lambda bb, i: (0, 0, 0, 0)),
                  pl.BlockSpec((4, DIFF_QK_DIM), lambda bb, i: (0, 0)),
                  pl.BlockSpec((HEAD_DIM, 1), lambda bb, i: (0, 0))],
        out_specs=pl.BlockSpec((1, QT, GROUP), lambda bb, i: (bb, i, 0)),
        out_shape=jax.ShapeDtypeStruct((b, s, GROUP), F32),
        scratch_shapes=[pltpu.VMEM((2 * HEADS, GROUP, QT), BF16)] + _flash_scratch(2 * HEADS, HEAD_DIM)
        + [pltpu.VMEM((GROUP, QT), F32)],
        compiler_params=pltpu.CompilerParams(dimension_semantics=("arbitrary", "arbitrary"),
                                             vmem_limit_bytes=VMEM_LIMIT),
        name="diff_attention",
    )(far, q_t, k, v4, bias, lam_p, subln)


def _compress_kernel(ch_ref, ptop_ref, pbot_ref, w1t_ref, w1b_ref, b1_ref, w2k_ref, b2k_ref, w2v_ref, b2v_ref,
                     gk_ref, kc_ref, vct_ref):
    ch = ch_ref[0]
    n_c = ch.shape[0]
    u = jnp.dot((ch + ptop_ref[...]).astype(BF16), w1t_ref[...], preferred_element_type=F32)
    v = jnp.dot((ch + pbot_ref[...]).astype(BF16), w1b_ref[...], preferred_element_type=F32)
    v_next = pltpu.roll(v, n_c - 1, 0)
    hid = jax.nn.gelu(u + v_next + b1_ref[...])
    hk = hid[:, :CMP_HIDDEN].astype(BF16)
    hv = hid[:, CMP_HIDDEN:].astype(BF16)
    kc = jnp.dot(hk, w2k_ref[...], preferred_element_type=F32) + b2k_ref[...]
    msq = jnp.mean(kc * kc, axis=-1, keepdims=True)
    kc_ref[0] = (kc * lax.rsqrt(msq + RMS_EPS) * gk_ref[...]).astype(kc_ref.dtype)
    vct = lax.dot_general(w2v_ref[...], hv, NT_DIMS, preferred_element_type=F32) + b2v_ref[...]
    vct_ref[0] = vct.astype(vct_ref.dtype)


def _compress(chunks, ptop, pbot, w1t, w1b, b1, w2k, b2k, w2v, b2v, gk):
    b, n_c, width = chunks.shape
    const = lambda a: pl.BlockSpec(a.shape, lambda bb: (0,) * a.ndim)
    params = (ptop, pbot, w1t, w1b, b1, w2k, b2k, w2v, b2v, gk)
    return pl.pallas_call(
        _compress_kernel,
        grid=(b,),
        in_specs=[pl.BlockSpec((1, n_c, width), lambda bb: (bb, 0, 0))] + [const(a) for a in params],
        out_specs=[pl.BlockSpec((1, n_c, HEAD_DIM), lambda bb: (bb, 0, 0)),
                   pl.BlockSpec((1, HEAD_DIM, n_c), lambda bb: (bb, 0, 0))],
        out_shape=[jax.ShapeDtypeStruct((b, n_c, HEAD_DIM), BF16), jax.ShapeDtypeStruct((b, HEAD_DIM, n_c), BF16)],
        compiler_params=pltpu.CompilerParams(dimension_semantics=("arbitrary",), vmem_limit_bytes=VMEM_LIMIT),
        name="nsa_compress",
    )(chunks, *params)


def _cmp_attn_kernel(q_ref, kc_ref, vct_ref, bias_ref, o_ref, sel_ref, p_ref, *, n_sel):
    i = pl.program_id(0)
    kc = kc_ref[0]
    vct = vct_ref[0]
    n_c = kc.shape[0]
    n_blk = sel_ref.shape[1]
    psum = jnp.zeros((n_c, QT), F32)
    for h in range(HEADS):
        qh = q_ref[0, h * HEAD_DIM:(h + 1) * HEAD_DIM, :]
        s = jnp.dot(kc, qh, preferred_element_type=F32) + bias_ref[h, 0]
        m = jnp.maximum(jnp.max(s, axis=0, keepdims=True), 0.5 * NEG_INF)
        p = jnp.exp(s - m)
        den = jnp.sum(p, axis=0, keepdims=True)
        p = p / jnp.maximum(den, TINY)
        o_ref[0, h * HEAD_DIM:(h + 1) * HEAD_DIM, :] = jnp.dot(vct, p.astype(BF16), preferred_element_type=F32)
        psum = psum + p
    per_blk = SLC_BLOCK // CMP_STRIDE
    halves = []
    for half in range(QT // LANES):
        p_ref[half, 0:8, :] = jnp.zeros((8, LANES), F32)
        p_ref[half, 8:8 + n_c, :] = psum[:, half * LANES:(half + 1) * LANES]
        p_ref[half, 8 + n_c:16 + n_c, :] = jnp.zeros((8, LANES), F32)
        acc = p_ref[half, pl.ds(7, n_blk, stride=per_blk), :]
        for t in range(per_blk):
            acc = acc + p_ref[half, pl.ds(8 + t, n_blk, stride=per_blk), :]
        halves.append(acc)
    imp = jnp.concatenate(halves, axis=1)
    blk = lax.broadcasted_iota(jnp.int32, (n_blk, QT), 0)
    cur = (i * QT + lax.broadcasted_iota(jnp.int32, (n_blk, QT), 1)) // SLC_BLOCK
    forced = (blk == 0) | (blk == cur) | (blk == cur - 1)
    val = jnp.where(forced, FORCE_SELECT, jnp.where(blk <= cur, imp, NEG_INF))
    sel = jnp.zeros((n_blk, QT), jnp.bool_)
    for _ in range(n_sel):
        top = jnp.max(val, axis=0, keepdims=True)
        idx = jnp.min(jnp.where(val == top, blk, n_blk), axis=0, keepdims=True)
        hit = blk == idx
        sel = sel | hit
        val = jnp.where(hit, -3.0e38, val)
    sel_ref[0] = jnp.where(sel, 1.0, 0.0).astype(sel_ref.dtype)


def _cmp_attention(q_t, kc, vct, bias, b):
    nt = q_t.shape[0]
    nq = nt // b
    n_c = kc.shape[1]
    n_blk = nq * QT // SLC_BLOCK
    return pl.pallas_call(
        functools.partial(_cmp_attn_kernel, n_sel=min(SLC_TOPK, n_blk)),
        grid=(nq, b),
        in_specs=[pl.BlockSpec((1, GROUP, QT), lambda i, bb: (bb * nq + i, 0, 0)),
                  pl.BlockSpec((1, n_c, HEAD_DIM), lambda i, bb: (bb, 0, 0)),
                  pl.BlockSpec((1, HEAD_DIM, n_c), lambda i, bb: (bb, 0, 0)),
                  pl.BlockSpec((HEADS, 1, n_c, QT), lambda i, bb: (0, 0, 0, i))],
        out_specs=[pl.BlockSpec((1, GROUP, QT), lambda i, bb: (bb * nq + i, 0, 0)),
                   pl.BlockSpec((1, n_blk, QT), lambda i, bb: (bb * nq + i, 0, 0))],
        out_shape=[jax.ShapeDtypeStruct((nt, GROUP, QT), F32), jax.ShapeDtypeStruct((nt, n_blk, QT), BF16)],
        scratch_shapes=[pltpu.VMEM((QT // LANES, n_c + 16, LANES), F32)],
        compiler_params=pltpu.CompilerParams(dimension_semantics=("arbitrary", "arbitrary"),
                                             vmem_limit_bytes=VMEM_LIMIT),
        name="nsa_compressed_attention",
    )(q_t, kc, vct, bias)


SEL_REP = 8
N_WIN = -(-(NSA_WINDOW - 1 + KT - 1) // QT)


def _slc_win_kernel(far_ref, q_ref, ksw_ref, vs_ref, vw_ref, sel_ref, rep_ref, ocmp_ref, g_ref, bslc_ref, bwin_ref,
                    o_ref, qz_ref, m_ref, l_ref, acc_ref, ot_ref, mask_ref):
    i = pl.program_id(1)
    sel8 = jnp.dot(rep_ref[...], sel_ref[0], preferred_element_type=F32)
    mask_ref[...] = (sel8 - 1.0) * (-NEG_INF)
    blocks_per_tile = KT // SLC_BLOCK
    mrows = blocks_per_tile * SEL_REP
    zeros = jnp.zeros((HEAD_DIM, QT), BF16)
    for h in range(HEADS):
        qh = q_ref[0, h * HEAD_DIM:(h + 1) * HEAD_DIM, :]
        qz_ref[h] = jnp.concatenate([qh, zeros], axis=0)
        qz_ref[HEADS + h] = jnp.concatenate([zeros, qh], axis=0)
    _flash_reset(m_ref, l_ref, acc_ref)

    def slc_tile(j, near):
        k = ksw_ref[0, pl.ds(pl.multiple_of(j * KT, KT), KT), :]
        m8 = mask_ref[pl.ds(pl.multiple_of(j * mrows, mrows), mrows), :]
        mask = jnp.broadcast_to(m8.reshape(blocks_per_tile, 1, SEL_REP, QT),
                                (blocks_per_tile, SLC_BLOCK // SEL_REP, SEL_REP, QT)).reshape(KT, QT)
        v_t = vs_ref[0, j]

        def scores(h):
            s = jnp.dot(k, qz_ref[h], preferred_element_type=F32) + mask
            return s + bslc_ref[h, i - j] if near else s

        def update(h, s):
            _flash_update(h, s, v_t, 0.0 if near else far_ref[h], m_ref, l_ref, acc_ref)

        _staggered(HEADS, scores, update)

    def win_tile(j, _):
        k = ksw_ref[0, pl.ds(pl.multiple_of(j * KT, KT), KT), :]
        v_t = vw_ref[0, j]

        def scores(h):
            return jnp.dot(k, qz_ref[HEADS + h], preferred_element_type=F32) + bwin_ref[h, i - j]

        def update(h, s):
            _flash_update(HEADS + h, s, v_t, 0.0, m_ref, l_ref, acc_ref)

        _staggered(HEADS, scores, update)

    n_far = jnp.maximum(i - (N_NEAR - 1), 0)
    lax.fori_loop(0, n_far, lambda j, _: slc_tile(j, False), None)
    lax.fori_loop(n_far, i + 1, lambda j, _: slc_tile(j, True), None)
    lax.fori_loop(jnp.maximum(i - (N_WIN - 1), 0), i + 1, win_tile, None)

    for h in range(HEADS):
        g = jax.nn.sigmoid(g_ref[0, 3 * h:3 * h + 3, :])
        ot_ref[h * HEAD_DIM:(h + 1) * HEAD_DIM, :] = (g[0:1] * ocmp_ref[0, h * HEAD_DIM:(h + 1) * HEAD_DIM, :]
                                                      + g[1:2] * (acc_ref[h] / l_ref[h])
                                                      + g[2:3] * (acc_ref[HEADS + h] / l_ref[HEADS + h]))
    o_ref[0] = ot_ref[...].T


def _slc_win_attention(far, q_t, ksw, vs_t, vw_t, sel, rep, ocmp, g_t, bslc, bwin):
    b, s, _ = ksw.shape
    nq = s // QT
    nkv = s // KT
    n_blk = s // SLC_BLOCK
    tile = lambda height: pl.BlockSpec((1, height, QT), lambda bb, i: (bb * nq + i, 0, 0))
    whole = lambda a: pl.BlockSpec(a.shape, lambda bb, i: (0,) * a.ndim)
    return pl.pallas_call(
        _slc_win_kernel,
        grid=(b, nq),
        in_specs=[pl.BlockSpec(memory_space=pltpu.SMEM), tile(GROUP),
                  pl.BlockSpec((1, s, 2 * HEAD_DIM), lambda bb, i: (bb, 0, 0)),
                  pl.BlockSpec((1, nkv, HEAD_DIM, KT), lambda bb, i: (bb, 0, 0, 0)),
                  pl.BlockSpec((1, nkv, HEAD_DIM, KT), lambda bb, i: (bb, 0, 0, 0)),
                  tile(n_blk), whole(rep), tile(GROUP), tile(16), whole(bslc), whole(bwin)],
        out_specs=pl.BlockSpec((1, QT, GROUP), lambda bb, i: (bb, i, 0)),
        out_shape=jax.ShapeDtypeStruct((b, s, GROUP), F32),
        scratch_shapes=[pltpu.VMEM((2 * HEADS, 2 * HEAD_DIM, QT), BF16)] + _flash_scratch(2 * HEADS, HEAD_DIM)
        + [pltpu.VMEM((GROUP, QT), F32), pltpu.VMEM((n_blk * SEL_REP, QT), F32)],
        compiler_params=pltpu.CompilerParams(dimension_semantics=("arbitrary", "arbitrary"),
                                             vmem_limit_bytes=VMEM_LIMIT),
        name="nsa_selected_window_attention",
    )(far, q_t, ksw, vs_t.reshape(b, nkv, HEAD_DIM, KT), vw_t.reshape(b, nkv, HEAD_DIM, KT), sel, rep, ocmp, g_t,
      bslc, bwin)


def _out_kernel(x_ref, a0_ref, a1_ref, a2_ref, l0_ref, l1_ref, l2_ref, ob_ref, oc_ref, od_ref, gate_ref, w_ref,
                o_ref):
    l0, l1, l2 = l0_ref[...], l1_ref[...], l2_ref[...]
    mx = jnp.maximum(jnp.maximum(l0, l1), l2)
    e0, e1, e2 = jnp.exp(l0 - mx), jnp.exp(l1 - mx), jnp.exp(l2 - mx)
    den = e0 + e1 + e2
    o_a = (e0 / den) * a0_ref[...] + (e1 / den) * a1_ref[...] + (e2 / den) * a2_ref[...]
    y = jnp.concatenate([o_a, ob_ref[...], oc_ref[...], od_ref[...]], axis=1)
    g = gate_ref[...]
    y = y * (g * jax.nn.sigmoid(g))
    o_ref[...] = x_ref[...] + jnp.dot(y.astype(BF16), w_ref[...], preferred_element_type=F32)


def _out_projection(x2, a_outs, a_lses, o_b, o_c, o_d, gate, w_out):
    m, d = x2.shape
    rowblk = lambda width: pl.BlockSpec((PROJ_ROWS, width), lambda i: (i, 0))
    return pl.pallas_call(
        _out_kernel,
        grid=(m // PROJ_ROWS,),
        in_specs=[rowblk(d)] + [rowblk(GROUP)] * 9 + [rowblk(N_MIXERS * GROUP),
                                                     pl.BlockSpec((N_MIXERS * GROUP, d), lambda i: (0, 0))],
        out_specs=rowblk(d),
        out_shape=jax.ShapeDtypeStruct((m, d), F32),
        compiler_params=pltpu.CompilerParams(dimension_semantics=("arbitrary",), vmem_limit_bytes=VMEM_LIMIT),
        name="out_projection",
    )(x2, *a_outs, *a_lses, o_b, o_c, o_d, gate, w_out)


def _block_diag_mean(group):
    idx = np.arange(GROUP) // group
    return jnp.asarray((idx[:, None] == idx[None, :]).astype(np.float32) / group, BF16)


def _layer_weights(w_in, qk_gain, qk_gain_diff):
    d = w_in.shape[0]
    sizes = (GROUP,) * 3 + (GROUP, GROUP // 2, GROUP // 2) + (GROUP,) * 3 + (GROUP,) + (HEAD_DIM,) * 6 \
        + (HEADS * 3, N_MIXERS * GROUP)
    offs = np.concatenate([[0], np.cumsum(sizes)])
    col = lambda n: w_in[:, offs[n]:offs[n + 1]]
    (a_q, a_k, a_v, b_q, b_k, b_v, c_q, c_k, c_v, d_q, d_kc, d_vc, d_ks, d_vs, d_kw, d_vw, d_g, gate) = \
        [col(n) for n in range(18)]
    rep_kv = lambda w: jnp.repeat(w.reshape(d, 2, HEAD_DIM), 2, axis=1).reshape(d, GROUP)
    wrm = jnp.concatenate([a_q, a_k, a_v, b_q, rep_kv(b_k), rep_kv(b_v), c_k, d_kc, d_vc, d_ks, d_kw, gate], axis=1)
    wt = jnp.concatenate([c_q, c_v, d_q, d_vs, d_vw, d_g, jnp.zeros((d, 16 - HEADS * 3), w_in.dtype)], axis=1).T
    g = qk_gain
    ones = lambda n: jnp.ones((n,), F32)
    tile4 = lambda v: jnp.tile(v, HEADS)
    scale = HEAD_DIM ** -0.5
    grm = jnp.concatenate([tile4(g[0]) * scale, tile4(g[1]), ones(GROUP), tile4(g[2]) * scale, tile4(g[3]),
                           ones(GROUP), jnp.tile(qk_gain_diff[1], 2 * HEADS), ones(2 * HEAD_DIM), g[6], g[7],
                           ones(N_MIXERS * GROUP)])
    gt = jnp.concatenate([jnp.tile(qk_gain_diff[0], 2 * HEADS) * DIFF_QK_DIM ** -0.5, ones(GROUP),
                          tile4(g[4]) * scale, ones(2 * HEAD_DIM + 16)])
    return wrm.astype(BF16), wt.astype(BF16), grm.reshape(1, -1), gt.reshape(-1, 1)


def _compress_weights(cmp_pos, cmp_w1, cmp_b1, cmp_w2, cmp_b2):
    half = CMP_LEN // 2
    pos = jnp.concatenate([cmp_pos[0], cmp_pos[1]], axis=-1)
    ptop = pos[:half].reshape(1, -1)
    pbot = pos[half:].reshape(1, -1)
    w1 = cmp_w1.reshape(2, CMP_LEN, HEAD_DIM, CMP_HIDDEN)
    zeros = jnp.zeros_like(w1[0])
    w1cat = jnp.concatenate([jnp.concatenate([w1[0], zeros], axis=-1),
                             jnp.concatenate([zeros, w1[1]], axis=-1)], axis=1)
    w1t = w1cat[:half].reshape(half * 2 * HEAD_DIM, 2 * CMP_HIDDEN).astype(BF16)
    w1b = w1cat[half:].reshape(half * 2 * HEAD_DIM, 2 * CMP_HIDDEN).astype(BF16)
    b1 = jnp.concatenate([cmp_b1[0], cmp_b1[1]]).reshape(1, -1)
    return (ptop, pbot, w1t, w1b, b1, cmp_w2[0].astype(BF16), cmp_b2[0].reshape(1, -1),
            cmp_w2[1].T.astype(BF16), cmp_b2[1].reshape(-1, 1))


def kernel(x, rel_bias_table, norm_w, w_in, w_out, qk_gain, qk_gain_diff, attn_sinks, diff_lambda, diff_subln,
           cmp_pos, cmp_w1, cmp_b1, cmp_w2, cmp_b2):
    b, s, d = x.shape
    depth = w_in.shape[0]
    n_c = s // CMP_STRIDE
    n_blk = s // SLC_BLOCK
    assert s % (BAND_TILE * DILATED_CONFIGS[-1][1]) == 0 and s % PROJ_ROWS == 0 and d == N_MIXERS * GROUP

    table = rel_bias_table.astype(F32)
    band_bias = [_build_bias(table, head0=0, n_d=1, rows=BAND_TILE, cols=2 * BAND_TILE, base0=BAND_TILE, dstep=0,
                             rs=1, cs=-1, dscale=rate, max_dist=window // rate) for window, rate in DILATED_CONFIGS]
    swa_bias = _build_bias(table, head0=HEADS, n_d=1, rows=BAND_TILE, cols=2 * BAND_TILE, base0=BAND_TILE, dstep=0,
                           rs=1, cs=-1, max_dist=SWA_WINDOW - 1)
    diff_bias = _build_bias(table, head0=2 * HEADS, n_d=N_NEAR, rows=KT, cols=QT, base0=0, dstep=QT, rs=-1, cs=1)
    slc_bias = _build_bias(table, head0=3 * HEADS, n_d=N_NEAR, rows=KT, cols=QT, base0=0, dstep=QT, rs=-1, cs=1)
    win_bias = _build_bias(table, head0=3 * HEADS, n_d=N_WIN, rows=KT, cols=QT, base0=0, dstep=QT, rs=-1, cs=1,
                           max_dist=NSA_WINDOW - 1)
    cmp_bias = _build_bias(table, head0=3 * HEADS, n_d=1, rows=n_c, cols=s, base0=-(CMP_LEN - 1), dstep=0,
                           rs=-CMP_STRIDE, cs=1, r_valid=n_c - 1, col_tile=4 * QT)
    far_c = table[NUM_BUCKETS - 1, 2 * HEADS:3 * HEADS]
    far_d = table[NUM_BUCKETS - 1, 3 * HEADS:4 * HEADS]
    e64, e32 = _block_diag_mean(HEAD_DIM), _block_diag_mean(DIFF_QK_DIM)
    rep_idx = np.arange(n_blk * SEL_REP) // SEL_REP
    rep = jnp.asarray((rep_idx[:, None] == np.arange(n_blk)[None, :]).astype(np.float32), BF16)
    no_sink = jnp.zeros((HEADS,), F32)

    x2 = x.reshape(b * s, d)
    for layer in range(depth):
        wrm, wt, grm, gt = _layer_weights(w_in[layer], qk_gain[layer], qk_gain_diff[layer])
        (a_q, a_k, a_v, b_q, b_k, b_v, c_k, kvc, ksw, gate, c_qt, c_vt, d_qt, d_vst, d_vwt, d_gt) = _project(
            x2, norm_w[layer].reshape(1, d), wrm, wt, grm, gt, e64, e32)
        seq = lambda t: t.reshape(b, s, t.shape[-1])
        a_res = [_banded(seq(a_q), seq(a_k), seq(a_v), band_bias[n], no_sink, rate, False)
                 for n, (_, rate) in enumerate(DILATED_CONFIGS)]
        o_b, _ = _banded(seq(b_q), seq(b_k), seq(b_v), swa_bias, attn_sinks[layer].astype(F32), 1, True)
        lambda_init = 0.8 - 0.6 * math.exp(-0.3 * layer)
        o_c = _diff_attention(far_c, c_qt, seq(c_k), c_vt, diff_bias, diff_lambda[layer].astype(F32),
                              diff_subln[layer].reshape(HEAD_DIM, 1).astype(F32), lambda_init)
        cw = _compress_weights(cmp_pos[layer], cmp_w1[layer], cmp_b1[layer], cmp_w2[layer], cmp_b2[layer])
        kc, vct = _compress(kvc.reshape(b, n_c, CMP_STRIDE * 2 * HEAD_DIM), *cw, qk_gain[layer, 5].reshape(1, -1))
        o_cmp, sel = _cmp_attention(d_qt, kc, vct, cmp_bias, b)
        o_d = _slc_win_attention(far_d, d_qt, seq(ksw), d_vst, d_vwt, sel, rep, o_cmp, d_gt, slc_bias, win_bias)
        x2 = _out_projection(x2, [r[0].reshape(b * s, GROUP) for r in a_res],
                             [r[1].reshape(b * s, GROUP) for r in a_res], o_b.reshape(b * s, GROUP),
                             o_c.reshape(b * s, GROUP), o_d.reshape(b * s, GROUP), gate, w_out[layer].astype(BF16))
    return x2.reshape(b, s, d)
```

```python
import functools
import math

import numpy as np
import jax
import jax.numpy as jnp
from jax import lax
from jax.experimental import pallas as pl
from jax.experimental.pallas import tpu as pltpu

F32 = jnp.float32
BF16 = jnp.bfloat16

HEAD_DIM = 64
HEADS = 4
GROUP = HEADS * HEAD_DIM
N_MIXERS = 4
NUM_BUCKETS = 32
REL_MAX_DIST = 2048
DILATED_CONFIGS = ((128, 1), (512, 4), (2048, 16))
SWA_WINDOW = 128
DIFF_QK_DIM = HEAD_DIM // 2
CMP_LEN = 32
CMP_STRIDE = 16
CMP_HIDDEN = 256
SLC_BLOCK = 64
SLC_TOPK = 16
NSA_WINDOW = 512
RMS_EPS = 1e-6
NEG_INF = -1e30
FORCE_SELECT = 1e9
TINY = 1e-30
LOG2E = math.log2(math.e)

PROJ_ROWS = 512
BAND_TILE = 128
LANES = 128
QT = 256
KT = 256
VMEM_LIMIT = 56 * 1024 * 1024
MXU_LOOKAHEAD = 4
V_ROWS = HEAD_DIM + 16

NT_DIMS = (((1,), (1,)), ((), ()))


def _t5_thresholds():
    n = np.arange(0, 4 * REL_MAX_DIST)
    max_exact = NUM_BUCKETS // 2
    nf = np.maximum(n, 1).astype(np.float32)
    large = max_exact + (np.log(nf / np.float32(max_exact)) / np.float32(math.log(REL_MAX_DIST / max_exact))
                         * np.float32(NUM_BUCKETS - max_exact)).astype(np.int32)
    bucket = np.where(n < max_exact, n, np.minimum(large, NUM_BUCKETS - 1))
    return [int(np.argmax(bucket >= b)) for b in range(NUM_BUCKETS)]


T5_THRESHOLDS = _t5_thresholds()
FAR_DIST = T5_THRESHOLDS[-1]


def _bias_kernel(tbl_ref, out_ref, *, head0, base0, dstep, rs, cs, dscale, max_dist, r_valid, col_tile, scale):
    h = pl.program_id(0)
    d = pl.program_id(1)
    ct = pl.program_id(2)
    rows, cols = out_ref.shape[-2:]
    r = lax.broadcasted_iota(jnp.int32, (rows, cols), 0)
    c = lax.broadcasted_iota(jnp.int32, (rows, cols), 1) + ct * col_tile
    dist = base0 + d * dstep + r * rs + c * cs
    n = dist * dscale
    val = jnp.full((rows, cols), tbl_ref[0, head0 + h], F32)
    for b in range(1, NUM_BUCKETS):
        val = jnp.where(n >= T5_THRESHOLDS[b], tbl_ref[b, head0 + h], val)
    valid = (dist >= 0) & (dist <= max_dist) & (r < r_valid)
    out_ref[0, 0] = jnp.where(valid, val * scale, NEG_INF)


def _build_bias(table, *, head0, n_d, rows, cols, base0, dstep, rs, cs, dscale=1, max_dist=1 << 30,
                r_valid=1 << 30, col_tile=None, scale=1.0):
    col_tile = cols if col_tile is None else col_tile
    kern = functools.partial(_bias_kernel, head0=head0, base0=base0, dstep=dstep, rs=rs, cs=cs, dscale=dscale,
                             max_dist=max_dist, r_valid=r_valid, col_tile=col_tile, scale=scale)
    return pl.pallas_call(
        kern,
        grid=(HEADS, n_d, cols // col_tile),
        in_specs=[pl.BlockSpec(memory_space=pltpu.SMEM)],
        out_specs=pl.BlockSpec((1, 1, rows, col_tile), lambda h, d, c: (h, d, 0, c)),
        out_shape=jax.ShapeDtypeStruct((HEADS, n_d, rows, cols), F32),
        name="rel_bias_tiles",
    )(table)


RM_AQ, RM_AK, RM_AV = 0, 256, 512
RM_BQ, RM_BK, RM_BV = 768, 1024, 1280
RM_CK = 1536
RM_KVC = 1792
RM_KSW = 1920
RM_GATE = 2048
RM_COLS = 3072
TR_CQ, TR_CV, TR_DQ, TR_DVS, TR_DVW, TR_DG = 0, 256, 512, 768, 832, 896
TR_ROWS = 912


def _proj_kernel(x_ref, nw_ref, wrm_ref, wt_ref, grm_ref, gt_ref, e64_ref, e32_ref,
                 aq_ref, ak_ref, av_ref, bq_ref, bk_ref, bv_ref, ck_ref, kvc_ref, ksw_ref, gate_ref,
                 cq_ref, cv_ref, dq_ref, dvs_ref, dvw_ref, dg_ref):
    x = x_ref[...]
    ms = jnp.mean(x * x, axis=-1, keepdims=True)
    xn = (x * lax.rsqrt(ms + RMS_EPS) * nw_ref[...]).astype(BF16)
    rows = x.shape[0]

    def rm(c0, width):
        return jnp.dot(xn, wrm_ref[:, c0:c0 + width], preferred_element_type=F32)

    def rm_normed(c0, width, e_ref):
        h = rm(c0, width)
        msq = jnp.dot((h * h).astype(BF16), e_ref[0:width, 0:width], preferred_element_type=F32)
        return h * lax.rsqrt(msq + RMS_EPS) * grm_ref[:, c0:c0 + width]

    aq_ref[...] = rm_normed(RM_AQ, GROUP, e64_ref).astype(aq_ref.dtype)
    ak_ref[...] = rm_normed(RM_AK, GROUP, e64_ref).astype(ak_ref.dtype)
    av_ref[...] = rm(RM_AV, GROUP).astype(av_ref.dtype)
    bq_ref[...] = rm_normed(RM_BQ, GROUP, e64_ref).astype(bq_ref.dtype)
    bk_ref[...] = rm_normed(RM_BK, GROUP, e64_ref).astype(bk_ref.dtype)
    bv_ref[...] = rm(RM_BV, GROUP).astype(bv_ref.dtype)
    ck_ref[...] = rm_normed(RM_CK, GROUP, e32_ref).astype(ck_ref.dtype)
    kvc_ref[...] = rm(RM_KVC, 2 * HEAD_DIM).astype(kvc_ref.dtype)
    ksw_ref[...] = rm_normed(RM_KSW, 2 * HEAD_DIM, e64_ref).astype(ksw_ref.dtype)
    gate_ref[...] = rm(RM_GATE, N_MIXERS * GROUP).astype(gate_ref.dtype)

    def tr(r0, height):
        return lax.dot_general(wt_ref[r0:r0 + height, :], xn, NT_DIMS, preferred_element_type=F32)

    def tr_normed(r0, height, group):
        h3 = tr(r0, height).reshape(height // group, group, rows)
        msq = jnp.mean(h3 * h3, axis=1, keepdims=True)
        return (h3 * lax.rsqrt(msq + RMS_EPS)).reshape(height, rows) * gt_ref[r0:r0 + height, :]

    def put(ref, val):
        for t in range(rows // QT):
            ref[t] = val[:, t * QT:(t + 1) * QT].astype(ref.dtype)

    def with_ones(v):
        ones = jnp.ones((V_ROWS - HEAD_DIM, rows), F32)
        parts = []
        for h in range(v.shape[0] // HEAD_DIM):
            parts += [v[h * HEAD_DIM:(h + 1) * HEAD_DIM], ones]
        return jnp.concatenate(parts, axis=0)

    put(cq_ref, tr_normed(TR_CQ, GROUP, DIFF_QK_DIM))
    put(cv_ref, with_ones(tr(TR_CV, GROUP)))
    put(dq_ref, tr_normed(TR_DQ, GROUP, HEAD_DIM))
    put(dvs_ref, with_ones(tr(TR_DVS, HEAD_DIM)))
    put(dvw_ref, with_ones(tr(TR_DVW, HEAD_DIM)))
    put(dg_ref, tr(TR_DG, 16))


def _project(x2, nw, wrm, wt, grm, gt, e64, e32):
    m, d = x2.shape
    nt = m // QT
    tpr = PROJ_ROWS // QT
    const = lambda shape: pl.BlockSpec(shape, lambda i: (0,) * len(shape))
    rm_out = lambda width, dtype: (jax.ShapeDtypeStruct((m, width), dtype),
                                   pl.BlockSpec((PROJ_ROWS, width), lambda i: (i, 0)))
    tr_out = lambda height, dtype: (jax.ShapeDtypeStruct((nt, height, QT), dtype),
                                    pl.BlockSpec((tpr, height, QT), lambda i: (i, 0, 0)))
    outs = [rm_out(GROUP, BF16)] * 7 + [rm_out(2 * HEAD_DIM, F32), rm_out(2 * HEAD_DIM, BF16),
                                        rm_out(N_MIXERS * GROUP, F32)]
    outs += [tr_out(GROUP, BF16), tr_out(HEADS * V_ROWS, BF16), tr_out(GROUP, BF16), tr_out(V_ROWS, BF16),
             tr_out(V_ROWS, BF16), tr_out(16, F32)]
    return pl.pallas_call(
        _proj_kernel,
        grid=(m // PROJ_ROWS,),
        in_specs=[pl.BlockSpec((PROJ_ROWS, d), lambda i: (i, 0)), const((1, d)), const((d, RM_COLS)),
                  const((TR_ROWS, d)), const((1, RM_COLS)), const((TR_ROWS, 1)), const((GROUP, GROUP)),
                  const((GROUP, GROUP))],
        out_specs=[o[1] for o in outs],
        out_shape=[o[0] for o in outs],
        compiler_params=pltpu.CompilerParams(dimension_semantics=("arbitrary",), vmem_limit_bytes=VMEM_LIMIT),
        name="in_projection",
    )(x2, nw, wrm, wt, grm, gt, e64, e32)


def _band_kernel(sink_ref, q_ref, kp_ref, kc_ref, vp_ref, vc_ref, bias_ref, o_ref, lse_ref, *, use_sink):
    i = pl.program_id(2)
    q = q_ref[0]
    kp, kc, vp, vc = kp_ref[0], kc_ref[0], vp_ref[0], vc_ref[0]
    head_of_lane = lax.broadcasted_iota(jnp.int32, (BAND_TILE, GROUP), 1) // HEAD_DIM
    no_prev = jnp.where(i == 0, NEG_INF, 0.0).astype(F32)
    o_acc = jnp.zeros((BAND_TILE, GROUP), F32)
    lse_acc = jnp.zeros((BAND_TILE, GROUP), F32)
    for h in range(HEADS):
        qh = jnp.where(head_of_lane == h, q, jnp.zeros_like(q))
        sp = lax.dot_general(qh, kp, NT_DIMS, preferred_element_type=F32) + no_prev
        sc = lax.dot_general(qh, kc, NT_DIMS, preferred_element_type=F32)
        s = jnp.concatenate([sp, sc], axis=1) + bias_ref[h, 0]
        m = jnp.max(s, axis=1, keepdims=True)
        if use_sink:
            m = jnp.maximum(m, sink_ref[h])
        p = jnp.exp(s - m)
        den = jnp.sum(p, axis=1, keepdims=True)
        if use_sink:
            den = den + jnp.exp(sink_ref[h] - m)
        pv = (jnp.dot(p[:, :BAND_TILE].astype(BF16), vp, preferred_element_type=F32)
              + jnp.dot(p[:, BAND_TILE:].astype(BF16), vc, preferred_element_type=F32))
        o_acc = jnp.where(head_of_lane == h, pv / den, o_acc)
        lse_acc = jnp.where(head_of_lane == h, m + jnp.log(den), lse_acc)
    o_ref[0] = o_acc
    lse_ref[0] = lse_acc


def _banded(q, k, v, bias, sink, rate, use_sink):
    b, s, _ = q.shape
    ln = s // rate
    fold = lambda t: t.reshape(b, ln, rate * GROUP)
    cur = pl.BlockSpec((1, BAND_TILE, GROUP), lambda bb, r, i: (bb, i, r))
    prev = pl.BlockSpec((1, BAND_TILE, GROUP), lambda bb, r, i: (bb, jnp.maximum(i - 1, 0), r))
    o, lse = pl.pallas_call(
        functools.partial(_band_kernel, use_sink=use_sink),
        grid=(b, rate, ln // BAND_TILE),
        in_specs=[pl.BlockSpec(memory_space=pltpu.SMEM), cur, prev, cur, prev, cur,
                  pl.BlockSpec((HEADS, 1, BAND_TILE, 2 * BAND_TILE), lambda bb, r, i: (0, 0, 0, 0))],
        out_specs=[cur, cur],
        out_shape=[jax.ShapeDtypeStruct((b, ln, rate * GROUP), F32)] * 2,
        compiler_params=pltpu.CompilerParams(dimension_semantics=("arbitrary",) * 3),
        name=f"banded_attention_r{rate}",
    )(sink, fold(q), fold(k), fold(k), fold(v), fold(v), bias)
    return o.reshape(b, s, GROUP), lse.reshape(b, s, GROUP)


def _flash_reset(m_ref, acc_ref):
    m_ref[...] = jnp.full(m_ref.shape, NEG_INF, F32)
    acc_ref[...] = jnp.zeros(acc_ref.shape, F32)


def _flash_update(n, s, v_t, m_ref, acc_ref):
    m_old = m_ref[n]
    m_new = jnp.maximum(m_old, jnp.max(s, axis=0, keepdims=True))
    alpha = jnp.exp2(m_old - m_new)
    p = jnp.exp2(s - m_new)
    acc_ref[n] = alpha * acc_ref[n] + jnp.dot(v_t, p.astype(BF16), preferred_element_type=F32)
    m_ref[n] = m_new


def _flash_result(n, acc_ref):
    return acc_ref[n, 0:HEAD_DIM, :] / acc_ref[n, HEAD_DIM:HEAD_DIM + 1, :]


def _staggered(n_items, scores, update):
    pending = {n: scores(n) for n in range(min(MXU_LOOKAHEAD, n_items))}
    for n in range(n_items):
        if n + MXU_LOOKAHEAD < n_items:
            pending[n + MXU_LOOKAHEAD] = scores(n + MXU_LOOKAHEAD)
        update(n, pending.pop(n))


def _pipelined_tiles(n_tiles, n_chains, load_tile, scores, update, next_ref):
    assert n_chains >= MXU_LOOKAHEAD

    def body(j, _):
        j_next = jnp.minimum(j + 1, n_tiles - 1)
        tile, tile_next = load_tile(j), load_tile(j_next)
        pending = {}
        for n in range(n_chains):
            cur = next_ref[n] if n < MXU_LOOKAHEAD else pending.pop(n)
            ahead = n + MXU_LOOKAHEAD
            if ahead < n_chains:
                pending[ahead] = scores(tile, j, ahead)
            else:
                next_ref[ahead - n_chains] = scores(tile_next, j_next, ahead - n_chains)
            update(j, n, cur)

    first = load_tile(0)
    for n in range(MXU_LOOKAHEAD):
        next_ref[n] = scores(first, 0, n)
    lax.fori_loop(0, n_tiles, body, None)


def _flash_scratch(chains):
    return [pltpu.VMEM((chains, 1, QT), F32), pltpu.VMEM((chains, V_ROWS, QT), F32),
            pltpu.VMEM((MXU_LOOKAHEAD, KT, QT), F32)]


N_NEAR = -(-(FAR_DIST + KT - 1) // QT)


def _diff_kernel(q_ref, k_ref, v_ref, bias_ref, lam_ref, subln_ref, o_ref, qz_ref, m_ref, acc_ref, next_ref, ot_ref, *,
                 lambda_init):
    i = pl.program_id(1)
    q = q_ref[0]
    row = lax.broadcasted_iota(jnp.int32, (GROUP, QT), 0) // DIFF_QK_DIM
    for n in range(2 * HEADS):
        qz_ref[n] = jnp.where(row == n, q, jnp.zeros_like(q))
    _flash_reset(m_ref, acc_ref)

    def load_tile(j):
        return k_ref[0, pl.ds(pl.multiple_of(j * KT, KT), KT), :]

    def scores(k, j, n):
        return jnp.dot(k, qz_ref[n], preferred_element_type=F32) + bias_ref[n // 2, jnp.minimum(i - j, N_NEAR)]

    def update(j, n, s):
        h = n // 2
        _flash_update(n, s, v_ref[0, j, h * V_ROWS:(h + 1) * V_ROWS, :], m_ref, acc_ref)

    _pipelined_tiles(i + 1, 2 * HEADS, load_tile, scores, update, next_ref)

    lam_p = lam_ref[...]
    lam = (jnp.exp(jnp.sum(lam_p[0:1] * lam_p[1:2], axis=1, keepdims=True))
           - jnp.exp(jnp.sum(lam_p[2:3] * lam_p[3:4], axis=1, keepdims=True)) + lambda_init)
    for h in range(HEADS):
        o = _flash_result(2 * h, acc_ref) - lam * _flash_result(2 * h + 1, acc_ref)
        msq = jnp.mean(o * o, axis=0, keepdims=True)
        ot_ref[h * HEAD_DIM:(h + 1) * HEAD_DIM, :] = (o * lax.rsqrt(msq + RMS_EPS) * subln_ref[...]
                                                      * (1.0 - lambda_init))
    o_ref[0] = ot_ref[...].T


def _diff_attention(q_t, k, v_t, bias, lam_p, subln, lambda_init):
    b, s, _ = k.shape
    nq = s // QT
    nkv = s // KT
    v4 = v_t.reshape(b, nkv, HEADS * V_ROWS, KT)
    return pl.pallas_call(
        functools.partial(_diff_kernel, lambda_init=lambda_init),
        grid=(b, nq),
        in_specs=[pl.BlockSpec((1, GROUP, QT), lambda bb, i: (bb * nq + i, 0, 0)),
                  pl.BlockSpec((1, s, GROUP), lambda bb, i: (bb, 0, 0)),
                  pl.BlockSpec((1, nkv, HEADS * V_ROWS, KT), lambda bb, i: (bb, 0, 0, 0)),
                  pl.BlockSpec((HEADS, N_NEAR + 1, KT, QT), lambda bb, i: (0, 0, 0, 0)),
                  pl.BlockSpec((4, DIFF_QK_DIM), lambda bb, i: (0, 0)),
                  pl.BlockSpec((HEAD_DIM, 1), lambda bb, i: (0, 0))],
        out_specs=pl.BlockSpec((1, QT, GROUP), lambda bb, i: (bb, i, 0)),
        out_shape=jax.ShapeDtypeStruct((b, s, GROUP), F32),
        scratch_shapes=[pltpu.VMEM((2 * HEADS, GROUP, QT), BF16)] + _flash_scratch(2 * HEADS)
        + [pltpu.VMEM((GROUP, QT), F32)],
        compiler_params=pltpu.CompilerParams(dimension_semantics=("arbitrary", "arbitrary"),
                                             vmem_limit_bytes=VMEM_LIMIT),
        name="diff_attention",
    )(q_t, k, v4, bias, lam_p, subln)


def _compress_kernel(ch_ref, ptop_ref, pbot_ref, w1t_ref, w1b_ref, b1_ref, w2k_ref, b2k_ref, w2v_ref, b2v_ref,
                     gk_ref, kc_ref, vct_ref):
    ch = ch_ref[0]
    n_c = ch.shape[0]
    u = jnp.dot((ch + ptop_ref[...]).astype(BF16), w1t_ref[...], preferred_element_type=F32)
    v = jnp.dot((ch + pbot_ref[...]).astype(BF16), w1b_ref[...], preferred_element_type=F32)
    v_next = pltpu.roll(v, n_c - 1, 0)
    hid = jax.nn.gelu(u + v_next + b1_ref[...])
    hk = hid[:, :CMP_HIDDEN].astype(BF16)
    hv = hid[:, CMP_HIDDEN:].astype(BF16)
    kc = jnp.dot(hk, w2k_ref[...], preferred_element_type=F32) + b2k_ref[...]
    msq = jnp.mean(kc * kc, axis=-1, keepdims=True)
    kc_ref[0] = (kc * lax.rsqrt(msq + RMS_EPS) * gk_ref[...]).astype(kc_ref.dtype)
    vct = lax.dot_general(w2v_ref[...], hv, NT_DIMS, preferred_element_type=F32) + b2v_ref[...]
    vct_ref[0] = vct.astype(vct_ref.dtype)


def _compress(chunks, ptop, pbot, w1t, w1b, b1, w2k, b2k, w2v, b2v, gk):
    b, n_c, width = chunks.shape
    const = lambda a: pl.BlockSpec(a.shape, lambda bb: (0,) * a.ndim)
    params = (ptop, pbot, w1t, w1b, b1, w2k, b2k, w2v, b2v, gk)
    return pl.pallas_call(
        _compress_kernel,
        grid=(b,),
        in_specs=[pl.BlockSpec((1, n_c, width), lambda bb: (bb, 0, 0))] + [const(a) for a in params],
        out_specs=[pl.BlockSpec((1, n_c, HEAD_DIM), lambda bb: (bb, 0, 0)),
                   pl.BlockSpec((1, HEAD_DIM, n_c), lambda bb: (bb, 0, 0))],
        out_shape=[jax.ShapeDtypeStruct((b, n_c, HEAD_DIM), BF16), jax.ShapeDtypeStruct((b, HEAD_DIM, n_c), BF16)],
        compiler_params=pltpu.CompilerParams(dimension_semantics=("arbitrary",), vmem_limit_bytes=VMEM_LIMIT),
        name="nsa_compress",
    )(chunks, *params)


def _cmp_attn_kernel(q_ref, kc_ref, vct_ref, bias_ref, o_ref, sel_ref, p_ref, *, n_sel):
    i = pl.program_id(0)
    kc = kc_ref[0]
    vct = vct_ref[0]
    n_c = kc.shape[0]
    n_blk = sel_ref.shape[1]
    psum = jnp.zeros((n_c, QT), F32)
    for h in range(HEADS):
        qh = q_ref[0, h * HEAD_DIM:(h + 1) * HEAD_DIM, :]
        s = jnp.dot(kc, qh, preferred_element_type=F32) + bias_ref[h, 0]
        m = jnp.maximum(jnp.max(s, axis=0, keepdims=True), 0.5 * NEG_INF)
        p = jnp.exp2(s - m)
        den = jnp.sum(p, axis=0, keepdims=True)
        p = p / jnp.maximum(den, TINY)
        o_ref[0, h * HEAD_DIM:(h + 1) * HEAD_DIM, :] = jnp.dot(vct, p.astype(BF16), preferred_element_type=F32)
        psum = psum + p
    per_blk = SLC_BLOCK // CMP_STRIDE
    halves = []
    for half in range(QT // LANES):
        p_ref[half, 0:8, :] = jnp.zeros((8, LANES), F32)
        p_ref[half, 8:8 + n_c, :] = psum[:, half * LANES:(half + 1) * LANES]
        p_ref[half, 8 + n_c:16 + n_c, :] = jnp.zeros((8, LANES), F32)
        acc = p_ref[half, pl.ds(7, n_blk, stride=per_blk), :]
        for t in range(per_blk):
            acc = acc + p_ref[half, pl.ds(8 + t, n_blk, stride=per_blk), :]
        halves.append(acc)
    imp = jnp.concatenate(halves, axis=1)
    blk = lax.broadcasted_iota(jnp.int32, (n_blk, QT), 0)
    cur = (i * QT + lax.broadcasted_iota(jnp.int32, (n_blk, QT), 1)) // SLC_BLOCK
    forced = (blk == 0) | (blk == cur) | (blk == cur - 1)
    val = jnp.where(forced, FORCE_SELECT, jnp.where(blk <= cur, imp, NEG_INF))
    sel = jnp.zeros((n_blk, QT), jnp.bool_)
    for _ in range(n_sel):
        top = jnp.max(val, axis=0, keepdims=True)
        idx = jnp.min(jnp.where(val == top, blk, n_blk), axis=0, keepdims=True)
        hit = blk == idx
        sel = sel | hit
        val = jnp.where(hit, -3.0e38, val)
    sel_ref[0] = jnp.where(sel, 1.0, 0.0).astype(sel_ref.dtype)


def _cmp_attention(q_t, kc, vct, bias, b):
    nt = q_t.shape[0]
    nq = nt // b
    n_c = kc.shape[1]
    n_blk = nq * QT // SLC_BLOCK
    return pl.pallas_call(
        functools.partial(_cmp_attn_kernel, n_sel=min(SLC_TOPK, n_blk)),
        grid=(nq, b),
        in_specs=[pl.BlockSpec((1, GROUP, QT), lambda i, bb: (bb * nq + i, 0, 0)),
                  pl.BlockSpec((1, n_c, HEAD_DIM), lambda i, bb: (bb, 0, 0)),
                  pl.BlockSpec((1, HEAD_DIM, n_c), lambda i, bb: (bb, 0, 0)),
                  pl.BlockSpec((HEADS, 1, n_c, QT), lambda i, bb: (0, 0, 0, i))],
        out_specs=[pl.BlockSpec((1, GROUP, QT), lambda i, bb: (bb * nq + i, 0, 0)),
                   pl.BlockSpec((1, n_blk, QT), lambda i, bb: (bb * nq + i, 0, 0))],
        out_shape=[jax.ShapeDtypeStruct((nt, GROUP, QT), F32), jax.ShapeDtypeStruct((nt, n_blk, QT), BF16)],
        scratch_shapes=[pltpu.VMEM((QT // LANES, n_c + 16, LANES), F32)],
        compiler_params=pltpu.CompilerParams(dimension_semantics=("arbitrary", "arbitrary"),
                                             vmem_limit_bytes=VMEM_LIMIT),
        name="nsa_compressed_attention",
    )(q_t, kc, vct, bias)


SEL_REP = 8
N_WIN = -(-(NSA_WINDOW - 1 + KT - 1) // QT)


def _slc_win_kernel(q_ref, ksw_ref, vs_ref, vw_ref, sel_ref, rep_ref, ocmp_ref, g_ref, bslc_ref, bwin_ref,
                    o_ref, qz_ref, m_ref, acc_ref, next_ref, ot_ref, mask_ref):
    i = pl.program_id(1)
    sel8 = jnp.dot(rep_ref[...], sel_ref[0], preferred_element_type=F32)
    mask_ref[...] = (sel8 - 1.0) * (-NEG_INF)
    blocks_per_tile = KT // SLC_BLOCK
    mrows = blocks_per_tile * SEL_REP
    zeros = jnp.zeros((HEAD_DIM, QT), BF16)
    for h in range(HEADS):
        qh = q_ref[0, h * HEAD_DIM:(h + 1) * HEAD_DIM, :]
        qz_ref[h] = jnp.concatenate([qh, zeros], axis=0)
        qz_ref[HEADS + h] = jnp.concatenate([zeros, qh], axis=0)
    _flash_reset(m_ref, acc_ref)

    def load_keys(j):
        return ksw_ref[0, pl.ds(pl.multiple_of(j * KT, KT), KT), :]

    def load_tile(j):
        m8 = mask_ref[pl.ds(pl.multiple_of(j * mrows, mrows), mrows), :]
        mask = jnp.broadcast_to(m8.reshape(blocks_per_tile, 1, SEL_REP, QT),
                                (blocks_per_tile, SLC_BLOCK // SEL_REP, SEL_REP, QT)).reshape(KT, QT)
        return load_keys(j), mask

    def slc_scores(tile, j, h):
        k, mask = tile
        return jnp.dot(k, qz_ref[h], preferred_element_type=F32) + mask + bslc_ref[h, jnp.minimum(i - j, N_NEAR)]

    def slc_update(j, h, s):
        _flash_update(h, s, vs_ref[0, j], m_ref, acc_ref)

    _pipelined_tiles(i + 1, HEADS, load_tile, slc_scores, slc_update, next_ref)

    def win_tile(n):
        d = N_WIN - 1 - n // HEADS
        return d, n % HEADS, jnp.maximum(i - d, 0)

    def win_scores(n):
        d, h, j = win_tile(n)
        missing = jnp.where(i < d, NEG_INF, 0.0).astype(F32)
        return jnp.dot(load_keys(j), qz_ref[HEADS + h], preferred_element_type=F32) + (bwin_ref[h, d] + missing)

    def win_update(n, s):
        _, h, j = win_tile(n)
        _flash_update(HEADS + h, s, vw_ref[0, j], m_ref, acc_ref)

    _staggered(N_WIN * HEADS, win_scores, win_update)

    for h in range(HEADS):
        g = jax.nn.sigmoid(g_ref[0, 3 * h:3 * h + 3, :])
        ot_ref[h * HEAD_DIM:(h + 1) * HEAD_DIM, :] = (g[0:1] * ocmp_ref[0, h * HEAD_DIM:(h + 1) * HEAD_DIM, :]
                                                      + g[1:2] * _flash_result(h, acc_ref)
                                                      + g[2:3] * _flash_result(HEADS + h, acc_ref))
    o_ref[0] = ot_ref[...].T


def _slc_win_attention(q_t, ksw, vs_t, vw_t, sel, rep, ocmp, g_t, bslc, bwin):
    b, s, _ = ksw.shape
    nq = s // QT
    nkv = s // KT
    n_blk = s // SLC_BLOCK
    tile = lambda height: pl.BlockSpec((1, height, QT), lambda bb, i: (bb * nq + i, 0, 0))
    whole = lambda a: pl.BlockSpec(a.shape, lambda bb, i: (0,) * a.ndim)
    return pl.pallas_call(
        _slc_win_kernel,
        grid=(b, nq),
        in_specs=[tile(GROUP),
                  pl.BlockSpec((1, s, 2 * HEAD_DIM), lambda bb, i: (bb, 0, 0)),
                  pl.BlockSpec((1, nkv, V_ROWS, KT), lambda bb, i: (bb, 0, 0, 0)),
                  pl.BlockSpec((1, nkv, V_ROWS, KT), lambda bb, i: (bb, 0, 0, 0)),
                  tile(n_blk), whole(rep), tile(GROUP), tile(16), whole(bslc), whole(bwin)],
        out_specs=pl.BlockSpec((1, QT, GROUP), lambda bb, i: (bb, i, 0)),
        out_shape=jax.ShapeDtypeStruct((b, s, GROUP), F32),
        scratch_shapes=[pltpu.VMEM((2 * HEADS, 2 * HEAD_DIM, QT), BF16)] + _flash_scratch(2 * HEADS)
        + [pltpu.VMEM((GROUP, QT), F32), pltpu.VMEM((n_blk * SEL_REP, QT), F32)],
        compiler_params=pltpu.CompilerParams(dimension_semantics=("arbitrary", "arbitrary"),
                                             vmem_limit_bytes=VMEM_LIMIT),
        name="nsa_selected_window_attention",
    )(q_t, ksw, vs_t.reshape(b, nkv, V_ROWS, KT), vw_t.reshape(b, nkv, V_ROWS, KT), sel, rep, ocmp, g_t,
      bslc, bwin)


def _out_kernel(x_ref, a0_ref, a1_ref, a2_ref, l0_ref, l1_ref, l2_ref, ob_ref, oc_ref, od_ref, gate_ref, w_ref,
                o_ref):
    l0, l1, l2 = l0_ref[...], l1_ref[...], l2_ref[...]
    mx = jnp.maximum(jnp.maximum(l0, l1), l2)
    e0, e1, e2 = jnp.exp(l0 - mx), jnp.exp(l1 - mx), jnp.exp(l2 - mx)
    den = e0 + e1 + e2
    o_a = (e0 / den) * a0_ref[...] + (e1 / den) * a1_ref[...] + (e2 / den) * a2_ref[...]
    y = jnp.concatenate([o_a, ob_ref[...], oc_ref[...], od_ref[...]], axis=1)
    g = gate_ref[...]
    y = y * (g * jax.nn.sigmoid(g))
    o_ref[...] = x_ref[...] + jnp.dot(y.astype(BF16), w_ref[...], preferred_element_type=F32)


def _out_projection(x2, a_outs, a_lses, o_b, o_c, o_d, gate, w_out):
    m, d = x2.shape
    rowblk = lambda width: pl.BlockSpec((PROJ_ROWS, width), lambda i: (i, 0))
    return pl.pallas_call(
        _out_kernel,
        grid=(m // PROJ_ROWS,),
        in_specs=[rowblk(d)] + [rowblk(GROUP)] * 9 + [rowblk(N_MIXERS * GROUP),
                                                     pl.BlockSpec((N_MIXERS * GROUP, d), lambda i: (0, 0))],
        out_specs=rowblk(d),
        out_shape=jax.ShapeDtypeStruct((m, d), F32),
        compiler_params=pltpu.CompilerParams(dimension_semantics=("arbitrary",), vmem_limit_bytes=VMEM_LIMIT),
        name="out_projection",
    )(x2, *a_outs, *a_lses, o_b, o_c, o_d, gate, w_out)


def _block_diag_mean(group):
    idx = np.arange(GROUP) // group
    return jnp.asarray((idx[:, None] == idx[None, :]).astype(np.float32) / group, BF16)


def _layer_weights(w_in, qk_gain, qk_gain_diff):
    d = w_in.shape[0]
    sizes = (GROUP,) * 3 + (GROUP, GROUP // 2, GROUP // 2) + (GROUP,) * 3 + (GROUP,) + (HEAD_DIM,) * 6 \
        + (HEADS * 3, N_MIXERS * GROUP)
    offs = np.concatenate([[0], np.cumsum(sizes)])
    col = lambda n: w_in[:, offs[n]:offs[n + 1]]
    (a_q, a_k, a_v, b_q, b_k, b_v, c_q, c_k, c_v, d_q, d_kc, d_vc, d_ks, d_vs, d_kw, d_vw, d_g, gate) = \
        [col(n) for n in range(18)]
    rep_kv = lambda w: jnp.repeat(w.reshape(d, 2, HEAD_DIM), 2, axis=1).reshape(d, GROUP)
    wrm = jnp.concatenate([a_q, a_k, a_v, b_q, rep_kv(b_k), rep_kv(b_v), c_k, d_kc, d_vc, d_ks, d_kw, gate], axis=1)
    wt = jnp.concatenate([c_q, c_v, d_q, d_vs, d_vw, d_g, jnp.zeros((d, 16 - HEADS * 3), w_in.dtype)], axis=1).T
    g = qk_gain
    ones = lambda n: jnp.ones((n,), F32)
    tile4 = lambda v: jnp.tile(v, HEADS)
    scale = HEAD_DIM ** -0.5
    grm = jnp.concatenate([tile4(g[0]) * scale, tile4(g[1]), ones(GROUP), tile4(g[2]) * scale, tile4(g[3]),
                           ones(GROUP), jnp.tile(qk_gain_diff[1], 2 * HEADS), ones(2 * HEAD_DIM), g[6], g[7],
                           ones(N_MIXERS * GROUP)])
    gt = jnp.concatenate([jnp.tile(qk_gain_diff[0], 2 * HEADS) * (DIFF_QK_DIM ** -0.5 * LOG2E), ones(GROUP),
                          tile4(g[4]) * (scale * LOG2E), ones(2 * HEAD_DIM + 16)])
    return wrm.astype(BF16), wt.astype(BF16), grm.reshape(1, -1), gt.reshape(-1, 1)


def _compress_weights(cmp_pos, cmp_w1, cmp_b1, cmp_w2, cmp_b2):
    half = CMP_LEN // 2
    pos = jnp.concatenate([cmp_pos[0], cmp_pos[1]], axis=-1)
    ptop = pos[:half].reshape(1, -1)
    pbot = pos[half:].reshape(1, -1)
    w1 = cmp_w1.reshape(2, CMP_LEN, HEAD_DIM, CMP_HIDDEN)
    zeros = jnp.zeros_like(w1[0])
    w1cat = jnp.concatenate([jnp.concatenate([w1[0], zeros], axis=-1),
                             jnp.concatenate([zeros, w1[1]], axis=-1)], axis=1)
    w1t = w1cat[:half].reshape(half * 2 * HEAD_DIM, 2 * CMP_HIDDEN).astype(BF16)
    w1b = w1cat[half:].reshape(half * 2 * HEAD_DIM, 2 * CMP_HIDDEN).astype(BF16)
    b1 = jnp.concatenate([cmp_b1[0], cmp_b1[1]]).reshape(1, -1)
    return (ptop, pbot, w1t, w1b, b1, cmp_w2[0].astype(BF16), cmp_b2[0].reshape(1, -1),
            cmp_w2[1].T.astype(BF16), cmp_b2[1].reshape(-1, 1))


def kernel(x, rel_bias_table, norm_w, w_in, w_out, qk_gain, qk_gain_diff, attn_sinks, diff_lambda, diff_subln,
           cmp_pos, cmp_w1, cmp_b1, cmp_w2, cmp_b2):
    b, s, d = x.shape
    depth = w_in.shape[0]
    n_c = s // CMP_STRIDE
    n_blk = s // SLC_BLOCK
    assert s % (BAND_TILE * DILATED_CONFIGS[-1][1]) == 0 and s % PROJ_ROWS == 0 and d == N_MIXERS * GROUP

    table = rel_bias_table.astype(F32)
    band_bias = [_build_bias(table, head0=0, n_d=1, rows=BAND_TILE, cols=2 * BAND_TILE, base0=BAND_TILE, dstep=0,
                             rs=1, cs=-1, dscale=rate, max_dist=window // rate) for window, rate in DILATED_CONFIGS]
    swa_bias = _build_bias(table, head0=HEADS, n_d=1, rows=BAND_TILE, cols=2 * BAND_TILE, base0=BAND_TILE, dstep=0,
                           rs=1, cs=-1, max_dist=SWA_WINDOW - 1)
    flash_tiles = dict(rows=KT, cols=QT, base0=0, dstep=QT, rs=-1, cs=1, scale=LOG2E)
    diff_bias = _build_bias(table, head0=2 * HEADS, n_d=N_NEAR + 1, **flash_tiles)
    slc_bias = _build_bias(table, head0=3 * HEADS, n_d=N_NEAR + 1, **flash_tiles)
    win_bias = _build_bias(table, head0=3 * HEADS, n_d=N_WIN, max_dist=NSA_WINDOW - 1, **flash_tiles)
    cmp_bias = _build_bias(table, head0=3 * HEADS, n_d=1, rows=n_c, cols=s, base0=-(CMP_LEN - 1), dstep=0,
                           rs=-CMP_STRIDE, cs=1, r_valid=n_c - 1, col_tile=4 * QT, scale=LOG2E)
    e64, e32 = _block_diag_mean(HEAD_DIM), _block_diag_mean(DIFF_QK_DIM)
    rep_idx = np.arange(n_blk * SEL_REP) // SEL_REP
    rep = jnp.asarray((rep_idx[:, None] == np.arange(n_blk)[None, :]).astype(np.float32), BF16)
    no_sink = jnp.zeros((HEADS,), F32)

    x2 = x.reshape(b * s, d)
    for layer in range(depth):
        wrm, wt, grm, gt = _layer_weights(w_in[layer], qk_gain[layer], qk_gain_diff[layer])
        (a_q, a_k, a_v, b_q, b_k, b_v, c_k, kvc, ksw, gate, c_qt, c_vt, d_qt, d_vst, d_vwt, d_gt) = _project(
            x2, norm_w[layer].reshape(1, d), wrm, wt, grm, gt, e64, e32)
        seq = lambda t: t.reshape(b, s, t.shape[-1])
        a_res = [_banded(seq(a_q), seq(a_k), seq(a_v), band_bias[n], no_sink, rate, False)
                 for n, (_, rate) in enumerate(DILATED_CONFIGS)]
        o_b, _ = _banded(seq(b_q), seq(b_k), seq(b_v), swa_bias, attn_sinks[layer].astype(F32), 1, True)
        lambda_init = 0.8 - 0.6 * math.exp(-0.3 * layer)
        o_c = _diff_attention(c_qt, seq(c_k), c_vt, diff_bias, diff_lambda[layer].astype(F32),
                              diff_subln[layer].reshape(HEAD_DIM, 1).astype(F32), lambda_init)
        cw = _compress_weights(cmp_pos[layer], cmp_w1[layer], cmp_b1[layer], cmp_w2[layer], cmp_b2[layer])
        kc, vct = _compress(kvc.reshape(b, n_c, CMP_STRIDE * 2 * HEAD_DIM), *cw, qk_gain[layer, 5].reshape(1, -1))
        o_cmp, sel = _cmp_attention(d_qt, kc, vct, cmp_bias, b)
        o_d = _slc_win_attention(d_qt, seq(ksw), d_vst, d_vwt, sel, rep, o_cmp, d_gt, slc_bias, win_bias)
        x2 = _out_projection(x2, [r[0].reshape(b * s, GROUP) for r in a_res],
                             [r[1].reshape(b * s, GROUP) for r in a_res], o_b.reshape(b * s, GROUP),
                             o_c.reshape(b * s, GROUP), o_d.reshape(b * s, GROUP), gate, w_out[layer].astype(BF16))
    return x2.reshape(b, s, d)
```

```python
import functools
import math

import numpy as np
import jax
import jax.numpy as jnp
from jax import lax
from jax.experimental import pallas as pl
from jax.experimental.pallas import tpu as pltpu

F32 = jnp.float32
BF16 = jnp.bfloat16

HEAD_DIM = 64
HEADS = 4
GROUP = HEADS * HEAD_DIM
N_MIXERS = 4
NUM_BUCKETS = 32
REL_MAX_DIST = 2048
DILATED_CONFIGS = ((128, 1), (512, 4), (2048, 16))
SWA_WINDOW = 128
DIFF_QK_DIM = HEAD_DIM // 2
CMP_LEN = 32
CMP_STRIDE = 16
CMP_HIDDEN = 256
SLC_BLOCK = 64
SLC_TOPK = 16
NSA_WINDOW = 512
RMS_EPS = 1e-6
NEG_INF = -1e30
FORCE_SELECT = 1e9
TINY = 1e-30
LOG2E = math.log2(math.e)

PROJ_ROWS = 512
BAND_TILE = 128
BAND_STEP = 512
LANES = 128
QT = 256
KT = 256
VMEM_LIMIT = 56 * 1024 * 1024
MXU_LOOKAHEAD = 4
V_ROWS = HEAD_DIM + 16

NT_DIMS = (((1,), (1,)), ((), ()))


def _t5_thresholds():
    n = np.arange(0, 4 * REL_MAX_DIST)
    max_exact = NUM_BUCKETS // 2
    nf = np.maximum(n, 1).astype(np.float32)
    large = max_exact + (np.log(nf / np.float32(max_exact)) / np.float32(math.log(REL_MAX_DIST / max_exact))
                         * np.float32(NUM_BUCKETS - max_exact)).astype(np.int32)
    bucket = np.where(n < max_exact, n, np.minimum(large, NUM_BUCKETS - 1))
    return [int(np.argmax(bucket >= b)) for b in range(NUM_BUCKETS)]


T5_THRESHOLDS = _t5_thresholds()
FAR_DIST = T5_THRESHOLDS[-1]


def _bias_kernel(tbl_ref, out_ref, *, head0, base0, dstep, rs, cs, dscale, max_dist, r_valid, d_valid, col_tile,
                 scale):
    h = pl.program_id(0)
    d = pl.program_id(1)
    ct = pl.program_id(2)
    rows, cols = out_ref.shape[-2:]
    r = lax.broadcasted_iota(jnp.int32, (rows, cols), 0)
    c = lax.broadcasted_iota(jnp.int32, (rows, cols), 1) + ct * col_tile
    dist = base0 + d * dstep + r * rs + c * cs
    n = dist * dscale
    val = jnp.full((rows, cols), tbl_ref[0, head0 + h], F32)
    for b in range(1, NUM_BUCKETS):
        val = jnp.where(n >= T5_THRESHOLDS[b], tbl_ref[b, head0 + h], val)
    valid = (dist >= 0) & (dist <= max_dist) & (r < r_valid) & (d < d_valid)
    out_ref[0, 0] = jnp.where(valid, val * scale, NEG_INF)


def _build_bias(table, *, head0, n_d, rows, cols, base0, dstep, rs, cs, dscale=1, max_dist=1 << 30,
                r_valid=1 << 30, d_valid=1 << 30, col_tile=None, scale=1.0):
    col_tile = cols if col_tile is None else col_tile
    kern = functools.partial(_bias_kernel, head0=head0, base0=base0, dstep=dstep, rs=rs, cs=cs, dscale=dscale,
                             max_dist=max_dist, r_valid=r_valid, d_valid=d_valid, col_tile=col_tile, scale=scale)
    return pl.pallas_call(
        kern,
        grid=(HEADS, n_d, cols // col_tile),
        in_specs=[pl.BlockSpec(memory_space=pltpu.SMEM)],
        out_specs=pl.BlockSpec((1, 1, rows, col_tile), lambda h, d, c: (h, d, 0, c)),
        out_shape=jax.ShapeDtypeStruct((HEADS, n_d, rows, cols), F32),
        name="rel_bias_tiles",
    )(table)


RM_AQ, RM_AK, RM_AV = 0, 256, 512
RM_BQ, RM_BK, RM_BV = 768, 1024, 1280
RM_CK = 1536
RM_KVC = 1792
RM_KSW = 1920
RM_GATE = 2048
RM_COLS = 3072
TR_CQ, TR_CV, TR_DQ, TR_DVS, TR_DVW, TR_DG = 0, 256, 512, 768, 832, 896
TR_ROWS = 912


def _proj_kernel(x_ref, nw_ref, wrm_ref, wt_ref, grm_ref, gt_ref, e64_ref, e32_ref,
                 aq_ref, ak_ref, av_ref, bq_ref, bk_ref, bv_ref, ck_ref, kvc_ref, ksw_ref, gate_ref,
                 cq_ref, cv_ref, dq_ref, dvs_ref, dvw_ref, dg_ref):
    x = x_ref[...]
    ms = jnp.mean(x * x, axis=-1, keepdims=True)
    xn = (x * lax.rsqrt(ms + RMS_EPS) * nw_ref[...]).astype(BF16)
    rows = x.shape[0]

    def rm(c0, width):
        return jnp.dot(xn, wrm_ref[:, c0:c0 + width], preferred_element_type=F32)

    def rm_normed(c0, width, e_ref):
        h = rm(c0, width)
        msq = jnp.dot((h * h).astype(BF16), e_ref[0:width, 0:width], preferred_element_type=F32)
        return h * lax.rsqrt(msq + RMS_EPS) * grm_ref[:, c0:c0 + width]

    aq_ref[...] = rm_normed(RM_AQ, GROUP, e64_ref).astype(aq_ref.dtype)
    ak_ref[...] = rm_normed(RM_AK, GROUP, e64_ref).astype(ak_ref.dtype)
    av_ref[...] = rm(RM_AV, GROUP).astype(av_ref.dtype)
    bq_ref[...] = rm_normed(RM_BQ, GROUP, e64_ref).astype(bq_ref.dtype)
    bk_ref[...] = rm_normed(RM_BK, GROUP, e64_ref).astype(bk_ref.dtype)
    bv_ref[...] = rm(RM_BV, GROUP).astype(bv_ref.dtype)
    ck_ref[...] = rm_normed(RM_CK, GROUP, e32_ref).astype(ck_ref.dtype)
    kvc_ref[...] = rm(RM_KVC, 2 * HEAD_DIM).astype(kvc_ref.dtype)
    ksw_ref[...] = rm_normed(RM_KSW, 2 * HEAD_DIM, e64_ref).astype(ksw_ref.dtype)
    gate_ref[...] = rm(RM_GATE, N_MIXERS * GROUP).astype(gate_ref.dtype)

    def tr(r0, height):
        return lax.dot_general(wt_ref[r0:r0 + height, :], xn, NT_DIMS, preferred_element_type=F32)

    def tr_normed(r0, height, group):
        h3 = tr(r0, height).reshape(height // group, group, rows)
        msq = jnp.mean(h3 * h3, axis=1, keepdims=True)
        return (h3 * lax.rsqrt(msq + RMS_EPS)).reshape(height, rows) * gt_ref[r0:r0 + height, :]

    def put(ref, val):
        for t in range(rows // QT):
            ref[t] = val[:, t * QT:(t + 1) * QT].astype(ref.dtype)

    def with_ones(v):
        ones = jnp.ones((V_ROWS - HEAD_DIM, rows), F32)
        parts = []
        for h in range(v.shape[0] // HEAD_DIM):
            parts += [v[h * HEAD_DIM:(h + 1) * HEAD_DIM], ones]
        return jnp.concatenate(parts, axis=0)

    put(cq_ref, tr_normed(TR_CQ, GROUP, DIFF_QK_DIM))
    put(cv_ref, with_ones(tr(TR_CV, GROUP)))
    put(dq_ref, tr_normed(TR_DQ, GROUP, HEAD_DIM))
    put(dvs_ref, with_ones(tr(TR_DVS, HEAD_DIM)))
    put(dvw_ref, with_ones(tr(TR_DVW, HEAD_DIM)))
    put(dg_ref, tr(TR_DG, 16))


def _project(x2, nw, wrm, wt, grm, gt, e64, e32):
    m, d = x2.shape
    nt = m // QT
    tpr = PROJ_ROWS // QT
    const = lambda shape: pl.BlockSpec(shape, lambda i: (0,) * len(shape))
    rm_out = lambda width, dtype: (jax.ShapeDtypeStruct((m, width), dtype),
                                   pl.BlockSpec((PROJ_ROWS, width), lambda i: (i, 0)))
    tr_out = lambda height, dtype: (jax.ShapeDtypeStruct((nt, height, QT), dtype),
                                    pl.BlockSpec((tpr, height, QT), lambda i: (i, 0, 0)))
    outs = [rm_out(GROUP, BF16)] * 7 + [rm_out(2 * HEAD_DIM, F32), rm_out(2 * HEAD_DIM, BF16),
                                        rm_out(N_MIXERS * GROUP, F32)]
    outs += [tr_out(GROUP, BF16), tr_out(HEADS * V_ROWS, BF16), tr_out(GROUP, BF16), tr_out(V_ROWS, BF16),
             tr_out(V_ROWS, BF16), tr_out(16, F32)]
    return pl.pallas_call(
        _proj_kernel,
        grid=(m // PROJ_ROWS,),
        in_specs=[pl.BlockSpec((PROJ_ROWS, d), lambda i: (i, 0)), const((1, d)), const((d, RM_COLS)),
                  const((TR_ROWS, d)), const((1, RM_COLS)), const((TR_ROWS, 1)), const((GROUP, GROUP)),
                  const((GROUP, GROUP))],
        out_specs=[o[1] for o in outs],
        out_shape=[o[0] for o in outs],
        compiler_params=pltpu.CompilerParams(dimension_semantics=("arbitrary",), vmem_limit_bytes=VMEM_LIMIT),
        name="in_projection",
    )(x2, nw, wrm, wt, grm, gt, e64, e32)


def _band_kernel(sink_ref, q_ref, kp_ref, kc_ref, vp_ref, vc_ref, bias_ref, o_ref, lse_ref, p_ref, *, use_sink):
    i = pl.program_id(2)
    n_blocks = q_ref.shape[1] // BAND_TILE
    head_q = lax.broadcasted_iota(jnp.int32, (BAND_TILE, GROUP), 1) // HEAD_DIM
    head_v = lax.broadcasted_iota(jnp.int32, (2 * BAND_TILE, GROUP), 1) // HEAD_DIM
    lane = lax.broadcasted_iota(jnp.int32, (BAND_TILE, LANES), 1)
    in_prev = lax.broadcasted_iota(jnp.int32, (1, 2 * BAND_TILE), 1) < BAND_TILE
    no_prev = jnp.where(in_prev & (i == 0), NEG_INF, 0.0).astype(F32)

    def window(cur_ref, prev_ref, m):
        if m == 0:
            return jnp.concatenate([prev_ref[0], cur_ref[0, 0:BAND_TILE, :]], axis=0)
        return cur_ref[0, (m - 1) * BAND_TILE:(m + 1) * BAND_TILE, :]

    def scores(n):
        m, h = divmod(n, HEADS)
        q = q_ref[0, m * BAND_TILE:(m + 1) * BAND_TILE, :]
        qh = jnp.where(head_q == h, q, jnp.zeros_like(q))
        bias = bias_ref[h, 0] + no_prev if m == 0 else bias_ref[h, 0]
        return lax.dot_general(qh, window(kc_ref, kp_ref, m), NT_DIMS, preferred_element_type=F32) + bias

    lse_tiles = {}

    def update(n, s):
        m, h = divmod(n, HEADS)
        mx = jnp.max(s, axis=1, keepdims=True)
        if use_sink:
            mx = jnp.maximum(mx, sink_ref[h])
        p = jnp.exp(s - mx)
        den = jnp.sum(p, axis=1, keepdims=True)
        if use_sink:
            den = den + jnp.exp(sink_ref[h] - mx)
        p_ref[m % 2, :, h * 2 * BAND_TILE:(h + 1) * 2 * BAND_TILE] = (p * (1.0 / den)).astype(BF16)
        lse_tiles[m] = jnp.where(lane == h, mx + jnp.log(den), lse_tiles.get(m, jnp.zeros((BAND_TILE, LANES), F32)))
        if h == HEADS - 1:
            v = window(vc_ref, vp_ref, m)
            v_heads = jnp.concatenate([jnp.where(head_v == hh, v, jnp.zeros_like(v)) for hh in range(HEADS)], axis=0)
            rows = slice(m * BAND_TILE, (m + 1) * BAND_TILE)
            o_ref[0, rows, :] = jnp.dot(p_ref[m % 2], v_heads, preferred_element_type=F32)
            lse_ref[0, rows, :] = lse_tiles.pop(m)

    _staggered(n_blocks * HEADS, scores, update)


def _banded(q, k, v, bias, sink, rate, use_sink):
    b, s, _ = q.shape
    ln = s // rate
    step = min(BAND_STEP, ln)
    per_step = step // BAND_TILE
    fold = lambda t: t.reshape(b, ln, rate * GROUP)
    cur = pl.BlockSpec((1, step, GROUP), lambda bb, r, i: (bb, i, r))
    prev = pl.BlockSpec((1, BAND_TILE, GROUP), lambda bb, r, i: (bb, jnp.maximum(i * per_step - 1, 0), r))
    o, lse = pl.pallas_call(
        functools.partial(_band_kernel, use_sink=use_sink),
        grid=(b, rate, ln // step),
        in_specs=[pl.BlockSpec(memory_space=pltpu.SMEM), cur, prev, cur, prev, cur,
                  pl.BlockSpec((HEADS, 1, BAND_TILE, 2 * BAND_TILE), lambda bb, r, i: (0, 0, 0, 0))],
        out_specs=[cur, pl.BlockSpec((1, step, LANES), lambda bb, r, i: (bb, i, r))],
        out_shape=[jax.ShapeDtypeStruct((b, ln, rate * GROUP), F32), jax.ShapeDtypeStruct((b, ln, rate * LANES), F32)],
        scratch_shapes=[pltpu.VMEM((2, BAND_TILE, HEADS * 2 * BAND_TILE), BF16)],
        compiler_params=pltpu.CompilerParams(dimension_semantics=("arbitrary",) * 3),
        name=f"banded_attention_r{rate}",
    )(sink, fold(q), fold(k), fold(k), fold(v), fold(v), bias)
    return o.reshape(b, s, GROUP), lse.reshape(b, s, LANES)


def _flash_reset(m_ref, acc_ref):
    m_ref[...] = jnp.full(m_ref.shape, NEG_INF, F32)
    acc_ref[...] = jnp.zeros(acc_ref.shape, F32)


def _flash_update(n, s, v_t, m_ref, acc_ref):
    m_old = m_ref[n]
    m_new = jnp.maximum(m_old, jnp.max(s, axis=0, keepdims=True))
    alpha = jnp.exp2(m_old - m_new)
    p = jnp.exp2(s - m_new)
    acc_ref[n] = alpha * acc_ref[n] + jnp.dot(v_t, p.astype(BF16), preferred_element_type=F32)
    m_ref[n] = m_new


def _flash_result(n, acc_ref):
    return acc_ref[n, 0:HEAD_DIM, :] / acc_ref[n, HEAD_DIM:HEAD_DIM + 1, :]


def _staggered(n_items, scores, update):
    pending = {n: scores(n) for n in range(min(MXU_LOOKAHEAD, n_items))}
    for n in range(n_items):
        if n + MXU_LOOKAHEAD < n_items:
            pending[n + MXU_LOOKAHEAD] = scores(n + MXU_LOOKAHEAD)
        update(n, pending.pop(n))


def _pipelined_tiles(n_tiles, n_chains, group, load_tile, scores, update, next_ref):
    ahead = next_ref.shape[0]
    n_items = group * n_chains
    assert ahead <= n_chains

    def body(trip, _):
        base = trip * group
        tiles, pending = {}, {}
        for n in range(n_items):
            cur = next_ref[n] if n < ahead else pending.pop(n)
            g, c = divmod(n + ahead, n_chains)
            if g not in tiles:
                tiles[g] = load_tile(base + g)
            new = scores(tiles[g], base + g, c)
            if n + ahead < n_items:
                pending[n + ahead] = new
            else:
                next_ref[n + ahead - n_items] = new
            update(base + n // n_chains, n % n_chains, cur)

    first = load_tile(0)
    for n in range(ahead):
        next_ref[n] = scores(first, 0, n)
    lax.fori_loop(0, (n_tiles + group - 1) // group, body, None)


def _flash_scratch(chains, ahead):
    return [pltpu.VMEM((chains, 1, QT), F32), pltpu.VMEM((chains, V_ROWS, QT), F32),
            pltpu.VMEM((ahead, KT, QT), F32)]


N_NEAR = -(-(FAR_DIST + KT - 1) // QT)
N_BIAS_TILES = N_NEAR + 2
DIFF_TILE_GROUP = 2
SLC_TILE_GROUP = 2


def _bias_tile_index(i, j):
    return jnp.where(j > i, N_NEAR + 1, jnp.minimum(i - j, N_NEAR))


def _diff_kernel(q_ref, k_ref, v_ref, bias_ref, lam_ref, subln_ref, o_ref, qz_ref, m_ref, acc_ref, next_ref, ot_ref, *,
                 lambda_init):
    i = pl.program_id(1)
    q = q_ref[0]
    row = lax.broadcasted_iota(jnp.int32, (GROUP, QT), 0) // DIFF_QK_DIM
    for n in range(2 * HEADS):
        qz_ref[n] = jnp.where(row == n, q, jnp.zeros_like(q))
    _flash_reset(m_ref, acc_ref)

    def load_tile(j):
        return k_ref[0, pl.ds(pl.multiple_of(jnp.minimum(j, i) * KT, KT), KT), :]

    def scores(k, j, n):
        return jnp.dot(k, qz_ref[n], preferred_element_type=F32) + bias_ref[n // 2, _bias_tile_index(i, j)]

    def update(j, n, s):
        h = n // 2
        _flash_update(n, s, v_ref[0, jnp.minimum(j, i), h * V_ROWS:(h + 1) * V_ROWS, :], m_ref, acc_ref)

    _pipelined_tiles(i + 1, 2 * HEADS, DIFF_TILE_GROUP, load_tile, scores, update, next_ref)

    lam_p = lam_ref[...]
    lam = (jnp.exp(jnp.sum(lam_p[0:1] * lam_p[1:2], axis=1, keepdims=True))
           - jnp.exp(jnp.sum(lam_p[2:3] * lam_p[3:4], axis=1, keepdims=True)) + lambda_init)
    for h in range(HEADS):
        o = _flash_result(2 * h, acc_ref) - lam * _flash_result(2 * h + 1, acc_ref)
        msq = jnp.mean(o * o, axis=0, keepdims=True)
        ot_ref[h * HEAD_DIM:(h + 1) * HEAD_DIM, :] = (o * lax.rsqrt(msq + RMS_EPS) * subln_ref[...]
                                                      * (1.0 - lambda_init))
    o_ref[0] = ot_ref[...].T


def _diff_attention(q_t, k, v_t, bias, lam_p, subln, lambda_init):
    b, s, _ = k.shape
    nq = s // QT
    nkv = s // KT
    v4 = v_t.reshape(b, nkv, HEADS * V_ROWS, KT)
    return pl.pallas_call(
        functools.partial(_diff_kernel, lambda_init=lambda_init),
        grid=(b, nq),
        in_specs=[pl.BlockSpec((1, GROUP, QT), lambda bb, i: (bb * nq + i, 0, 0)),
                  pl.BlockSpec((1, s, GROUP), lambda bb, i: (bb, 0, 0)),
                  pl.BlockSpec((1, nkv, HEADS * V_ROWS, KT), lambda bb, i: (bb, 0, 0, 0)),
                  pl.BlockSpec((HEADS, N_BIAS_TILES, KT, QT), lambda bb, i: (0, 0, 0, 0)),
                  pl.BlockSpec((4, DIFF_QK_DIM), lambda bb, i: (0, 0)),
                  pl.BlockSpec((HEAD_DIM, 1), lambda bb, i: (0, 0))],
        out_specs=pl.BlockSpec((1, QT, GROUP), lambda bb, i: (bb, i, 0)),
        out_shape=jax.ShapeDtypeStruct((b, s, GROUP), F32),
        scratch_shapes=[pltpu.VMEM((2 * HEADS, GROUP, QT), BF16)] + _flash_scratch(2 * HEADS, MXU_LOOKAHEAD)
        + [pltpu.VMEM((GROUP, QT), F32)],
        compiler_params=pltpu.CompilerParams(dimension_semantics=("arbitrary", "arbitrary"),
                                             vmem_limit_bytes=VMEM_LIMIT),
        name="diff_attention",
    )(q_t, k, v4, bias, lam_p, subln)


def _compress_kernel(ch_ref, ptop_ref, pbot_ref, w1t_ref, w1b_ref, b1_ref, w2k_ref, b2k_ref, w2v_ref, b2v_ref,
                     gk_ref, kc_ref, vct_ref):
    ch = ch_ref[0]
    n_c = ch.shape[0]
    u = jnp.dot((ch + ptop_ref[...]).astype(BF16), w1t_ref[...], preferred_element_type=F32)
    v = jnp.dot((ch + pbot_ref[...]).astype(BF16), w1b_ref[...], preferred_element_type=F32)
    v_next = pltpu.roll(v, n_c - 1, 0)
    hid = jax.nn.gelu(u + v_next + b1_ref[...])
    hk = hid[:, :CMP_HIDDEN].astype(BF16)
    hv = hid[:, CMP_HIDDEN:].astype(BF16)
    kc = jnp.dot(hk, w2k_ref[...], preferred_element_type=F32) + b2k_ref[...]
    msq = jnp.mean(kc * kc, axis=-1, keepdims=True)
    kc_ref[0] = (kc * lax.rsqrt(msq + RMS_EPS) * gk_ref[...]).astype(kc_ref.dtype)
    vct = lax.dot_general(w2v_ref[...], hv, NT_DIMS, preferred_element_type=F32) + b2v_ref[...]
    vct_ref[0] = vct.astype(vct_ref.dtype)


def _compress(chunks, ptop, pbot, w1t, w1b, b1, w2k, b2k, w2v, b2v, gk):
    b, n_c, width = chunks.shape
    const = lambda a: pl.BlockSpec(a.shape, lambda bb: (0,) * a.ndim)
    params = (ptop, pbot, w1t, w1b, b1, w2k, b2k, w2v, b2v, gk)
    return pl.pallas_call(
        _compress_kernel,
        grid=(b,),
        in_specs=[pl.BlockSpec((1, n_c, width), lambda bb: (bb, 0, 0))] + [const(a) for a in params],
        out_specs=[pl.BlockSpec((1, n_c, HEAD_DIM), lambda bb: (bb, 0, 0)),
                   pl.BlockSpec((1, HEAD_DIM, n_c), lambda bb: (bb, 0, 0))],
        out_shape=[jax.ShapeDtypeStruct((b, n_c, HEAD_DIM), BF16), jax.ShapeDtypeStruct((b, HEAD_DIM, n_c), BF16)],
        compiler_params=pltpu.CompilerParams(dimension_semantics=("arbitrary",), vmem_limit_bytes=VMEM_LIMIT),
        name="nsa_compress",
    )(chunks, *params)


def _cmp_attn_kernel(q_ref, kc_ref, vct_ref, bias_ref, o_ref, sel_ref, p_ref, *, n_sel):
    i = pl.program_id(0)
    kc = kc_ref[0]
    vct = vct_ref[0]
    n_c = kc.shape[0]
    n_blk = sel_ref.shape[1]
    psum = jnp.zeros((n_c, QT), F32)
    for h in range(HEADS):
        qh = q_ref[0, h * HEAD_DIM:(h + 1) * HEAD_DIM, :]
        s = jnp.dot(kc, qh, preferred_element_type=F32) + bias_ref[h, 0]
        m = jnp.maximum(jnp.max(s, axis=0, keepdims=True), 0.5 * NEG_INF)
        p = jnp.exp2(s - m)
        den = jnp.sum(p, axis=0, keepdims=True)
        p = p / jnp.maximum(den, TINY)
        o_ref[0, h * HEAD_DIM:(h + 1) * HEAD_DIM, :] = jnp.dot(vct, p.astype(BF16), preferred_element_type=F32)
        psum = psum + p
    per_blk = SLC_BLOCK // CMP_STRIDE
    halves = []
    for half in range(QT // LANES):
        p_ref[half, 0:8, :] = jnp.zeros((8, LANES), F32)
        p_ref[half, 8:8 + n_c, :] = psum[:, half * LANES:(half + 1) * LANES]
        p_ref[half, 8 + n_c:16 + n_c, :] = jnp.zeros((8, LANES), F32)
        acc = p_ref[half, pl.ds(7, n_blk, stride=per_blk), :]
        for t in range(per_blk):
            acc = acc + p_ref[half, pl.ds(8 + t, n_blk, stride=per_blk), :]
        halves.append(acc)
    imp = jnp.concatenate(halves, axis=1)
    blk = lax.broadcasted_iota(jnp.int32, (n_blk, QT), 0)
    cur = (i * QT + lax.broadcasted_iota(jnp.int32, (n_blk, QT), 1)) // SLC_BLOCK
    forced = (blk == 0) | (blk == cur) | (blk == cur - 1)
    val = jnp.where(forced, FORCE_SELECT, jnp.where(blk <= cur, imp, NEG_INF))
    sel = jnp.zeros((n_blk, QT), jnp.bool_)
    for _ in range(n_sel):
        top = jnp.max(val, axis=0, keepdims=True)
        idx = jnp.min(jnp.where(val == top, blk, n_blk), axis=0, keepdims=True)
        hit = blk == idx
        sel = sel | hit
        val = jnp.where(hit, -3.0e38, val)
    sel_ref[0] = jnp.where(sel, 1.0, 0.0).astype(sel_ref.dtype)


def _cmp_attention(q_t, kc, vct, bias, b):
    nt = q_t.shape[0]
    nq = nt // b
    n_c = kc.shape[1]
    n_blk = nq * QT // SLC_BLOCK
    return pl.pallas_call(
        functools.partial(_cmp_attn_kernel, n_sel=min(SLC_TOPK, n_blk)),
        grid=(nq, b),
        in_specs=[pl.BlockSpec((1, GROUP, QT), lambda i, bb: (bb * nq + i, 0, 0)),
                  pl.BlockSpec((1, n_c, HEAD_DIM), lambda i, bb: (bb, 0, 0)),
                  pl.BlockSpec((1, HEAD_DIM, n_c), lambda i, bb: (bb, 0, 0)),
                  pl.BlockSpec((HEADS, 1, n_c, QT), lambda i, bb: (0, 0, 0, i))],
        out_specs=[pl.BlockSpec((1, GROUP, QT), lambda i, bb: (bb * nq + i, 0, 0)),
                   pl.BlockSpec((1, n_blk, QT), lambda i, bb: (bb * nq + i, 0, 0))],
        out_shape=[jax.ShapeDtypeStruct((nt, GROUP, QT), F32), jax.ShapeDtypeStruct((nt, n_blk, QT), BF16)],
        scratch_shapes=[pltpu.VMEM((QT // LANES, n_c + 16, LANES), F32)],
        compiler_params=pltpu.CompilerParams(dimension_semantics=("arbitrary", "arbitrary"),
                                             vmem_limit_bytes=VMEM_LIMIT),
        name="nsa_compressed_attention",
    )(q_t, kc, vct, bias)


SEL_REP = 8
N_WIN = -(-(NSA_WINDOW - 1 + KT - 1) // QT)


def _slc_win_kernel(q_ref, ksw_ref, vs_ref, vw_ref, sel_ref, rep_ref, ocmp_ref, g_ref, bslc_ref, bwin_ref,
                    o_ref, qz_ref, m_ref, acc_ref, next_ref, ot_ref, mask_ref):
    i = pl.program_id(1)
    sel8 = jnp.dot(rep_ref[...], sel_ref[0], preferred_element_type=F32)
    mask_ref[...] = (sel8 - 1.0) * (-NEG_INF)
    blocks_per_tile = KT // SLC_BLOCK
    mrows = blocks_per_tile * SEL_REP
    zeros = jnp.zeros((HEAD_DIM, QT), BF16)
    for h in range(HEADS):
        qh = q_ref[0, h * HEAD_DIM:(h + 1) * HEAD_DIM, :]
        qz_ref[h] = jnp.concatenate([qh, zeros], axis=0)
        qz_ref[HEADS + h] = jnp.concatenate([zeros, qh], axis=0)
    _flash_reset(m_ref, acc_ref)

    def load_keys(j):
        return ksw_ref[0, pl.ds(pl.multiple_of(j * KT, KT), KT), :]

    def load_tile(j):
        j = jnp.minimum(j, i)
        m8 = mask_ref[pl.ds(pl.multiple_of(j * mrows, mrows), mrows), :]
        mask = jnp.broadcast_to(m8.reshape(blocks_per_tile, 1, SEL_REP, QT),
                                (blocks_per_tile, SLC_BLOCK // SEL_REP, SEL_REP, QT)).reshape(KT, QT)
        return load_keys(j), mask

    def slc_scores(tile, j, h):
        k, mask = tile
        return jnp.dot(k, qz_ref[h], preferred_element_type=F32) + mask + bslc_ref[h, _bias_tile_index(i, j)]

    def slc_update(j, h, s):
        _flash_update(h, s, vs_ref[0, jnp.minimum(j, i)], m_ref, acc_ref)

    _pipelined_tiles(i + 1, HEADS, SLC_TILE_GROUP, load_tile, slc_scores, slc_update, next_ref)

    def win_tile(n):
        d = N_WIN - 1 - n // HEADS
        return d, n % HEADS, jnp.maximum(i - d, 0)

    def win_scores(n):
        d, h, j = win_tile(n)
        missing = jnp.where(i < d, NEG_INF, 0.0).astype(F32)
        return jnp.dot(load_keys(j), qz_ref[HEADS + h], preferred_element_type=F32) + (bwin_ref[h, d] + missing)

    def win_update(n, s):
        _, h, j = win_tile(n)
        _flash_update(HEADS + h, s, vw_ref[0, j], m_ref, acc_ref)

    _staggered(N_WIN * HEADS, win_scores, win_update)

    for h in range(HEADS):
        g = jax.nn.sigmoid(g_ref[0, 3 * h:3 * h + 3, :])
        ot_ref[h * HEAD_DIM:(h + 1) * HEAD_DIM, :] = (g[0:1] * ocmp_ref[0, h * HEAD_DIM:(h + 1) * HEAD_DIM, :]
                                                      + g[1:2] * _flash_result(h, acc_ref)
                                                      + g[2:3] * _flash_result(HEADS + h, acc_ref))
    o_ref[0] = ot_ref[...].T


def _slc_win_attention(q_t, ksw, vs_t, vw_t, sel, rep, ocmp, g_t, bslc, bwin):
    b, s, _ = ksw.shape
    nq = s // QT
    nkv = s // KT
    n_blk = s // SLC_BLOCK
    tile = lambda height: pl.BlockSpec((1, height, QT), lambda bb, i: (bb * nq + i, 0, 0))
    whole = lambda a: pl.BlockSpec(a.shape, lambda bb, i: (0,) * a.ndim)
    return pl.pallas_call(
        _slc_win_kernel,
        grid=(b, nq),
        in_specs=[tile(GROUP),
                  pl.BlockSpec((1, s, 2 * HEAD_DIM), lambda bb, i: (bb, 0, 0)),
                  pl.BlockSpec((1, nkv, V_ROWS, KT), lambda bb, i: (bb, 0, 0, 0)),
                  pl.BlockSpec((1, nkv, V_ROWS, KT), lambda bb, i: (bb, 0, 0, 0)),
                  tile(n_blk), whole(rep), tile(GROUP), tile(16), whole(bslc), whole(bwin)],
        out_specs=pl.BlockSpec((1, QT, GROUP), lambda bb, i: (bb, i, 0)),
        out_shape=jax.ShapeDtypeStruct((b, s, GROUP), F32),
        scratch_shapes=[pltpu.VMEM((2 * HEADS, 2 * HEAD_DIM, QT), BF16)] + _flash_scratch(2 * HEADS, MXU_LOOKAHEAD)
        + [pltpu.VMEM((GROUP, QT), F32), pltpu.VMEM((n_blk * SEL_REP, QT), F32)],
        compiler_params=pltpu.CompilerParams(dimension_semantics=("arbitrary", "arbitrary"),
                                             vmem_limit_bytes=VMEM_LIMIT),
        name="nsa_selected_window_attention",
    )(q_t, ksw, vs_t.reshape(b, nkv, V_ROWS, KT), vw_t.reshape(b, nkv, V_ROWS, KT), sel, rep, ocmp, g_t,
      bslc, bwin)


def _out_kernel(x_ref, a0_ref, a1_ref, a2_ref, l0_ref, l1_ref, l2_ref, ob_ref, oc_ref, od_ref, gate_ref, e_ref,
                w_ref, o_ref):
    l0, l1, l2 = l0_ref[...], l1_ref[...], l2_ref[...]
    mx = jnp.maximum(jnp.maximum(l0, l1), l2)
    e0, e1, e2 = jnp.exp(l0 - mx), jnp.exp(l1 - mx), jnp.exp(l2 - mx)
    den = e0 + e1 + e2

    def per_head_lanes(w):
        hi = w.astype(BF16)
        lo = (w - hi.astype(F32)).astype(BF16)
        return (jnp.dot(hi, e_ref[...], preferred_element_type=F32)
                + jnp.dot(lo, e_ref[...], preferred_element_type=F32))

    o_a = (per_head_lanes(e0 / den) * a0_ref[...] + per_head_lanes(e1 / den) * a1_ref[...]
           + per_head_lanes(e2 / den) * a2_ref[...])
    y = jnp.concatenate([o_a, ob_ref[...], oc_ref[...], od_ref[...]], axis=1)
    g = gate_ref[...]
    y = y * (g * jax.nn.sigmoid(g))
    o_ref[...] = x_ref[...] + jnp.dot(y.astype(BF16), w_ref[...], preferred_element_type=F32)


def _out_projection(x2, a_outs, a_lses, o_b, o_c, o_d, gate, w_out):
    m, d = x2.shape
    rowblk = lambda width: pl.BlockSpec((PROJ_ROWS, width), lambda i: (i, 0))
    head_of_lane = np.arange(GROUP) // HEAD_DIM
    expand = jnp.asarray((np.arange(LANES)[:, None] == head_of_lane[None, :]).astype(np.float32), BF16)
    return pl.pallas_call(
        _out_kernel,
        grid=(m // PROJ_ROWS,),
        in_specs=[rowblk(d)] + [rowblk(GROUP)] * 3 + [rowblk(LANES)] * 3 + [rowblk(GROUP)] * 3
        + [rowblk(N_MIXERS * GROUP), pl.BlockSpec((LANES, GROUP), lambda i: (0, 0)),
           pl.BlockSpec((N_MIXERS * GROUP, d), lambda i: (0, 0))],
        out_specs=rowblk(d),
        out_shape=jax.ShapeDtypeStruct((m, d), F32),
        compiler_params=pltpu.CompilerParams(dimension_semantics=("arbitrary",), vmem_limit_bytes=VMEM_LIMIT),
        name="out_projection",
    )(x2, *a_outs, *a_lses, o_b, o_c, o_d, gate, expand, w_out)


def _block_diag_mean(group):
    idx = np.arange(GROUP) // group
    return jnp.asarray((idx[:, None] == idx[None, :]).astype(np.float32) / group, BF16)


def _layer_weights(w_in, qk_gain, qk_gain_diff):
    d = w_in.shape[0]
    sizes = (GROUP,) * 3 + (GROUP, GROUP // 2, GROUP // 2) + (GROUP,) * 3 + (GROUP,) + (HEAD_DIM,) * 6 \
        + (HEADS * 3, N_MIXERS * GROUP)
    offs = np.concatenate([[0], np.cumsum(sizes)])
    col = lambda n: w_in[:, offs[n]:offs[n + 1]]
    (a_q, a_k, a_v, b_q, b_k, b_v, c_q, c_k, c_v, d_q, d_kc, d_vc, d_ks, d_vs, d_kw, d_vw, d_g, gate) = \
        [col(n) for n in range(18)]
    rep_kv = lambda w: jnp.repeat(w.reshape(d, 2, HEAD_DIM), 2, axis=1).reshape(d, GROUP)
    wrm = jnp.concatenate([a_q, a_k, a_v, b_q, rep_kv(b_k), rep_kv(b_v), c_k, d_kc, d_vc, d_ks, d_kw, gate], axis=1)
    wt = jnp.concatenate([c_q, c_v, d_q, d_vs, d_vw, d_g, jnp.zeros((d, 16 - HEADS * 3), w_in.dtype)], axis=1).T
    g = qk_gain
    ones = lambda n: jnp.ones((n,), F32)
    tile4 = lambda v: jnp.tile(v, HEADS)
    scale = HEAD_DIM ** -0.5
    grm = jnp.concatenate([tile4(g[0]) * scale, tile4(g[1]), ones(GROUP), tile4(g[2]) * scale, tile4(g[3]),
                           ones(GROUP), jnp.tile(qk_gain_diff[1], 2 * HEADS), ones(2 * HEAD_DIM), g[6], g[7],
                           ones(N_MIXERS * GROUP)])
    gt = jnp.concatenate([jnp.tile(qk_gain_diff[0], 2 * HEADS) * (DIFF_QK_DIM ** -0.5 * LOG2E), ones(GROUP),
                          tile4(g[4]) * (scale * LOG2E), ones(2 * HEAD_DIM + 16)])
    return wrm.astype(BF16), wt.astype(BF16), grm.reshape(1, -1), gt.reshape(-1, 1)


def _compress_weights(cmp_pos, cmp_w1, cmp_b1, cmp_w2, cmp_b2):
    half = CMP_LEN // 2
    pos = jnp.concatenate([cmp_pos[0], cmp_pos[1]], axis=-1)
    ptop = pos[:half].reshape(1, -1)
    pbot = pos[half:].reshape(1, -1)
    w1 = cmp_w1.reshape(2, CMP_LEN, HEAD_DIM, CMP_HIDDEN)
    zeros = jnp.zeros_like(w1[0])
    w1cat = jnp.concatenate([jnp.concatenate([w1[0], zeros], axis=-1),
                             jnp.concatenate([zeros, w1[1]], axis=-1)], axis=1)
    w1t = w1cat[:half].reshape(half * 2 * HEAD_DIM, 2 * CMP_HIDDEN).astype(BF16)
    w1b = w1cat[half:].reshape(half * 2 * HEAD_DIM, 2 * CMP_HIDDEN).astype(BF16)
    b1 = jnp.concatenate([cmp_b1[0], cmp_b1[1]]).reshape(1, -1)
    return (ptop, pbot, w1t, w1b, b1, cmp_w2[0].astype(BF16), cmp_b2[0].reshape(1, -1),
            cmp_w2[1].T.astype(BF16), cmp_b2[1].reshape(-1, 1))


def kernel(x, rel_bias_table, norm_w, w_in, w_out, qk_gain, qk_gain_diff, attn_sinks, diff_lambda, diff_subln,
           cmp_pos, cmp_w1, cmp_b1, cmp_w2, cmp_b2):
    b, s, d = x.shape
    depth = w_in.shape[0]
    n_c = s // CMP_STRIDE
    n_blk = s // SLC_BLOCK
    assert s % (BAND_TILE * DILATED_CONFIGS[-1][1]) == 0 and s % PROJ_ROWS == 0 and d == N_MIXERS * GROUP

    table = rel_bias_table.astype(F32)
    band_bias = [_build_bias(table, head0=0, n_d=1, rows=BAND_TILE, cols=2 * BAND_TILE, base0=BAND_TILE, dstep=0,
                             rs=1, cs=-1, dscale=rate, max_dist=window // rate) for window, rate in DILATED_CONFIGS]
    swa_bias = _build_bias(table, head0=HEADS, n_d=1, rows=BAND_TILE, cols=2 * BAND_TILE, base0=BAND_TILE, dstep=0,
                           rs=1, cs=-1, max_dist=SWA_WINDOW - 1)
    flash_tiles = dict(rows=KT, cols=QT, base0=0, dstep=QT, rs=-1, cs=1, scale=LOG2E)
    diff_bias = _build_bias(table, head0=2 * HEADS, n_d=N_BIAS_TILES, d_valid=N_NEAR + 1, **flash_tiles)
    slc_bias = _build_bias(table, head0=3 * HEADS, n_d=N_BIAS_TILES, d_valid=N_NEAR + 1, **flash_tiles)
    win_bias = _build_bias(table, head0=3 * HEADS, n_d=N_WIN, max_dist=NSA_WINDOW - 1, **flash_tiles)
    cmp_bias = _build_bias(table, head0=3 * HEADS, n_d=1, rows=n_c, cols=s, base0=-(CMP_LEN - 1), dstep=0,
                           rs=-CMP_STRIDE, cs=1, r_valid=n_c - 1, col_tile=4 * QT, scale=LOG2E)
    e64, e32 = _block_diag_mean(HEAD_DIM), _block_diag_mean(DIFF_QK_DIM)
    rep_idx = np.arange(n_blk * SEL_REP) // SEL_REP
    rep = jnp.asarray((rep_idx[:, None] == np.arange(n_blk)[None, :]).astype(np.float32), BF16)
    no_sink = jnp.zeros((HEADS,), F32)

    x2 = x.reshape(b * s, d)
    for layer in range(depth):
        wrm, wt, grm, gt = _layer_weights(w_in[layer], qk_gain[layer], qk_gain_diff[layer])
        (a_q, a_k, a_v, b_q, b_k, b_v, c_k, kvc, ksw, gate, c_qt, c_vt, d_qt, d_vst, d_vwt, d_gt) = _project(
            x2, norm_w[layer].reshape(1, d), wrm, wt, grm, gt, e64, e32)
        seq = lambda t: t.reshape(b, s, t.shape[-1])
        a_res = [_banded(seq(a_q), seq(a_k), seq(a_v), band_bias[n], no_sink, rate, False)
                 for n, (_, rate) in enumerate(DILATED_CONFIGS)]
        o_b, _ = _banded(seq(b_q), seq(b_k), seq(b_v), swa_bias, attn_sinks[layer].astype(F32), 1, True)
        lambda_init = 0.8 - 0.6 * math.exp(-0.3 * layer)
        o_c = _diff_attention(c_qt, seq(c_k), c_vt, diff_bias, diff_lambda[layer].astype(F32),
                              diff_subln[layer].reshape(HEAD_DIM, 1).astype(F32), lambda_init)
        cw = _compress_weights(cmp_pos[layer], cmp_w1[layer], cmp_b1[layer], cmp_w2[layer], cmp_b2[layer])
        kc, vct = _compress(kvc.reshape(b, n_c, CMP_STRIDE * 2 * HEAD_DIM), *cw, qk_gain[layer, 5].reshape(1, -1))
        o_cmp, sel = _cmp_attention(d_qt, kc, vct, cmp_bias, b)
        o_d = _slc_win_attention(d_qt, seq(ksw), d_vst, d_vwt, sel, rep, o_cmp, d_gt, slc_bias, win_bias)
        x2 = _out_projection(x2, [r[0].reshape(b * s, GROUP) for r in a_res],
                             [r[1].reshape(b * s, LANES) for r in a_res], o_b.reshape(b * s, GROUP),
                             o_c.reshape(b * s, GROUP), o_d.reshape(b * s, GROUP), gate, w_out[layer].astype(BF16))
    return x2.reshape(b, s, d)
```

```python
import functools
import math

import numpy as np
import jax
import jax.numpy as jnp
from jax import lax
from jax.experimental import pallas as pl
from jax.experimental.pallas import tpu as pltpu

F32 = jnp.float32
BF16 = jnp.bfloat16

HEAD_DIM = 64
HEADS = 4
GROUP = HEADS * HEAD_DIM
N_MIXERS = 4
NUM_BUCKETS = 32
REL_MAX_DIST = 2048
DILATED_CONFIGS = ((128, 1), (512, 4), (2048, 16))
SWA_WINDOW = 128
DIFF_QK_DIM = HEAD_DIM // 2
CMP_LEN = 32
CMP_STRIDE = 16
CMP_HIDDEN = 256
SLC_BLOCK = 64
SLC_TOPK = 16
NSA_WINDOW = 512
RMS_EPS = 1e-6
NEG_INF = -1e30
FORCE_SELECT = 1e9
TINY = 1e-30
LOG2E = math.log2(math.e)

PROJ_ROWS = 512
BAND_TILE = 128
BAND_STEP = 512
BIAS_ROW_CHUNK = 64
LANES = 128
QT = 256
KT = 256
VMEM_LIMIT = 56 * 1024 * 1024
MXU_LOOKAHEAD = 4
V_ROWS = HEAD_DIM + 16

NT_DIMS = (((1,), (1,)), ((), ()))


def _t5_thresholds():
    n = np.arange(0, 4 * REL_MAX_DIST)
    max_exact = NUM_BUCKETS // 2
    nf = np.maximum(n, 1).astype(np.float32)
    large = max_exact + (np.log(nf / np.float32(max_exact)) / np.float32(math.log(REL_MAX_DIST / max_exact))
                         * np.float32(NUM_BUCKETS - max_exact)).astype(np.int32)
    bucket = np.where(n < max_exact, n, np.minimum(large, NUM_BUCKETS - 1))
    return [int(np.argmax(bucket >= b)) for b in range(NUM_BUCKETS)]


T5_THRESHOLDS = _t5_thresholds()
FAR_DIST = T5_THRESHOLDS[-1]


def _bias_kernel(tbl_ref, out_ref, *, head0, base0, dstep, rs, cs, dscale, max_dist, r_valid, d_valid, col_tile,
                 scale):
    h = pl.program_id(0)
    d = pl.program_id(1)
    ct = pl.program_id(2)
    rows, cols = out_ref.shape[-2:]
    chunk = BIAS_ROW_CHUNK if rows % BIAS_ROW_CHUNK == 0 else rows
    for r0 in range(0, rows, chunk):
        origin = base0 + d * dstep + r0 * rs + ct * col_tile * cs
        corners = [origin + dr * rs + dc * cs for dr in (0, chunk - 1) for dc in (0, cols - 1)]
        lo = functools.reduce(jnp.minimum, corners)
        hi = functools.reduce(jnp.maximum, corners)
        all_masked = (hi < 0) | (lo > max_dist) | (d >= d_valid) | (r0 >= r_valid)
        all_far = (lo * dscale >= FAR_DIST) & (hi <= max_dist) & (d < d_valid) & (r0 + chunk <= r_valid)
        out = out_ref.at[0, 0, r0:r0 + chunk, :]

        @pl.when(all_masked)
        def _():
            out[...] = jnp.full((chunk, cols), NEG_INF, F32)

        @pl.when(all_far)
        def _():
            out[...] = jnp.full((chunk, cols), tbl_ref[NUM_BUCKETS - 1, head0 + h] * scale, F32)

        @pl.when(jnp.logical_not(all_masked | all_far))
        def _():
            r = lax.broadcasted_iota(jnp.int32, (chunk, cols), 0)
            c = lax.broadcasted_iota(jnp.int32, (chunk, cols), 1)
            dist = origin + r * rs + c * cs
            n = dist * dscale
            val = jnp.full((chunk, cols), tbl_ref[0, head0 + h], F32)
            for b in range(1, NUM_BUCKETS):
                val = jnp.where(n >= T5_THRESHOLDS[b], tbl_ref[b, head0 + h], val)
            valid = (dist >= 0) & (dist <= max_dist) & (r + r0 < r_valid) & (d < d_valid)
            out[...] = jnp.where(valid, val * scale, NEG_INF)


def _build_bias(table, *, head0, n_d, rows, cols, base0, dstep, rs, cs, dscale=1, max_dist=1 << 30,
                r_valid=1 << 30, d_valid=1 << 30, col_tile=None, scale=1.0):
    col_tile = cols if col_tile is None else col_tile
    kern = functools.partial(_bias_kernel, head0=head0, base0=base0, dstep=dstep, rs=rs, cs=cs, dscale=dscale,
                             max_dist=max_dist, r_valid=r_valid, d_valid=d_valid, col_tile=col_tile, scale=scale)
    return pl.pallas_call(
        kern,
        grid=(HEADS, n_d, cols // col_tile),
        in_specs=[pl.BlockSpec(memory_space=pltpu.SMEM)],
        out_specs=pl.BlockSpec((1, 1, rows, col_tile), lambda h, d, c: (h, d, 0, c)),
        out_shape=jax.ShapeDtypeStruct((HEADS, n_d, rows, cols), F32),
        name="rel_bias_tiles",
    )(table)


RM_AQ, RM_AK, RM_AV = 0, 256, 512
RM_BQ, RM_BK, RM_BV = 768, 1024, 1280
RM_CK = 1536
RM_KVC = 1792
RM_KSW = 1920
RM_GATE = 2048
RM_COLS = 3072
TR_CQ, TR_CV, TR_DQ, TR_DVS, TR_DVW, TR_DG = 0, 256, 512, 768, 832, 896
TR_ROWS = 912


def _proj_kernel(x_ref, nw_ref, wrm_ref, wt_ref, grm_ref, gt_ref, e64_ref, e32_ref,
                 aq_ref, ak_ref, av_ref, bq_ref, bk_ref, bv_ref, ck_ref, kvc_ref, ksw_ref, gate_ref,
                 cq_ref, cv_ref, dq_ref, dvs_ref, dvw_ref, dg_ref):
    x = x_ref[...]
    ms = jnp.mean(x * x, axis=-1, keepdims=True)
    xn = (x * lax.rsqrt(ms + RMS_EPS) * nw_ref[...]).astype(BF16)
    rows = x.shape[0]

    def rm(c0, width):
        return jnp.dot(xn, wrm_ref[:, c0:c0 + width], preferred_element_type=F32)

    def rm_normed(c0, width, e_ref):
        h = rm(c0, width)
        msq = jnp.dot((h * h).astype(BF16), e_ref[0:width, 0:width], preferred_element_type=F32)
        return h * lax.rsqrt(msq + RMS_EPS) * grm_ref[:, c0:c0 + width]

    aq_ref[...] = rm_normed(RM_AQ, GROUP, e64_ref).astype(aq_ref.dtype)
    ak_ref[...] = rm_normed(RM_AK, GROUP, e64_ref).astype(ak_ref.dtype)
    av_ref[...] = rm(RM_AV, GROUP).astype(av_ref.dtype)
    bq_ref[...] = rm_normed(RM_BQ, GROUP, e64_ref).astype(bq_ref.dtype)
    bk_ref[...] = rm_normed(RM_BK, GROUP, e64_ref).astype(bk_ref.dtype)
    bv_ref[...] = rm(RM_BV, GROUP).astype(bv_ref.dtype)
    ck_ref[...] = rm_normed(RM_CK, GROUP, e32_ref).astype(ck_ref.dtype)
    kvc_ref[...] = rm(RM_KVC, 2 * HEAD_DIM).astype(kvc_ref.dtype)
    ksw_ref[...] = rm_normed(RM_KSW, 2 * HEAD_DIM, e64_ref).astype(ksw_ref.dtype)
    gate_ref[...] = rm(RM_GATE, N_MIXERS * GROUP).astype(gate_ref.dtype)

    def tr(r0, height):
        return lax.dot_general(wt_ref[r0:r0 + height, :], xn, NT_DIMS, preferred_element_type=F32)

    def tr_normed(r0, height, group):
        h3 = tr(r0, height).reshape(height // group, group, rows)
        msq = jnp.mean(h3 * h3, axis=1, keepdims=True)
        return (h3 * lax.rsqrt(msq + RMS_EPS)).reshape(height, rows) * gt_ref[r0:r0 + height, :]

    def put(ref, val):
        for t in range(rows // QT):
            ref[t] = val[:, t * QT:(t + 1) * QT].astype(ref.dtype)

    def with_ones(v):
        ones = jnp.ones((V_ROWS - HEAD_DIM, rows), F32)
        parts = []
        for h in range(v.shape[0] // HEAD_DIM):
            parts += [v[h * HEAD_DIM:(h + 1) * HEAD_DIM], ones]
        return jnp.concatenate(parts, axis=0)

    put(cq_ref, tr_normed(TR_CQ, GROUP, DIFF_QK_DIM))
    put(cv_ref, with_ones(tr(TR_CV, GROUP)))
    put(dq_ref, tr_normed(TR_DQ, GROUP, HEAD_DIM))
    put(dvs_ref, with_ones(tr(TR_DVS, HEAD_DIM)))
    put(dvw_ref, with_ones(tr(TR_DVW, HEAD_DIM)))
    put(dg_ref, tr(TR_DG, 16))


def _project(x2, nw, wrm, wt, grm, gt, e64, e32):
    m, d = x2.shape
    nt = m // QT
    tpr = PROJ_ROWS // QT
    const = lambda shape: pl.BlockSpec(shape, lambda i: (0,) * len(shape))
    rm_out = lambda width, dtype: (jax.ShapeDtypeStruct((m, width), dtype),
                                   pl.BlockSpec((PROJ_ROWS, width), lambda i: (i, 0)))
    tr_out = lambda height, dtype: (jax.ShapeDtypeStruct((nt, height, QT), dtype),
                                    pl.BlockSpec((tpr, height, QT), lambda i: (i, 0, 0)))
    outs = [rm_out(GROUP, BF16)] * 7 + [rm_out(2 * HEAD_DIM, F32), rm_out(2 * HEAD_DIM, BF16),
                                        rm_out(N_MIXERS * GROUP, BF16)]
    outs += [tr_out(GROUP, BF16), tr_out(HEADS * V_ROWS, BF16), tr_out(GROUP, BF16), tr_out(V_ROWS, BF16),
             tr_out(V_ROWS, BF16), tr_out(16, F32)]
    return pl.pallas_call(
        _proj_kernel,
        grid=(m // PROJ_ROWS,),
        in_specs=[pl.BlockSpec((PROJ_ROWS, d), lambda i: (i, 0)), const((1, d)), const((d, RM_COLS)),
                  const((TR_ROWS, d)), const((1, RM_COLS)), const((TR_ROWS, 1)), const((GROUP, GROUP)),
                  const((GROUP, GROUP))],
        out_specs=[o[1] for o in outs],
        out_shape=[o[0] for o in outs],
        compiler_params=pltpu.CompilerParams(dimension_semantics=("arbitrary",), vmem_limit_bytes=VMEM_LIMIT),
        name="in_projection",
    )(x2, nw, wrm, wt, grm, gt, e64, e32)


def _band_kernel(sink_ref, q_ref, kp_ref, kc_ref, vp_ref, vc_ref, bias_ref, o_ref, lse_ref, p_ref, *, use_sink):
    i = pl.program_id(2)
    n_blocks = q_ref.shape[1] // BAND_TILE
    head_q = lax.broadcasted_iota(jnp.int32, (BAND_TILE, GROUP), 1) // HEAD_DIM
    head_v = lax.broadcasted_iota(jnp.int32, (2 * BAND_TILE, GROUP), 1) // HEAD_DIM
    lane = lax.broadcasted_iota(jnp.int32, (BAND_TILE, LANES), 1)
    in_prev = lax.broadcasted_iota(jnp.int32, (1, 2 * BAND_TILE), 1) < BAND_TILE
    no_prev = jnp.where(in_prev & (i == 0), NEG_INF, 0.0).astype(F32)

    def window(cur_ref, prev_ref, m):
        if m == 0:
            return jnp.concatenate([prev_ref[0], cur_ref[0, 0:BAND_TILE, :]], axis=0)
        return cur_ref[0, (m - 1) * BAND_TILE:(m + 1) * BAND_TILE, :]

    def scores(n):
        m, h = divmod(n, HEADS)
        q = q_ref[0, m * BAND_TILE:(m + 1) * BAND_TILE, :]
        qh = jnp.where(head_q == h, q, jnp.zeros_like(q))
        bias = bias_ref[h, 0] + no_prev if m == 0 else bias_ref[h, 0]
        return lax.dot_general(qh, window(kc_ref, kp_ref, m), NT_DIMS, preferred_element_type=F32) + bias

    lse_tiles = {}

    def update(n, s):
        m, h = divmod(n, HEADS)
        mx = jnp.max(s, axis=1, keepdims=True)
        if use_sink:
            mx = jnp.maximum(mx, sink_ref[h])
        p = jnp.exp(s - mx)
        den = jnp.sum(p, axis=1, keepdims=True)
        if use_sink:
            den = den + jnp.exp(sink_ref[h] - mx)
        p_ref[m % 2, :, h * 2 * BAND_TILE:(h + 1) * 2 * BAND_TILE] = (p * (1.0 / den)).astype(BF16)
        lse_tiles[m] = jnp.where(lane == h, mx + jnp.log(den), lse_tiles.get(m, jnp.zeros((BAND_TILE, LANES), F32)))
        if h == HEADS - 1:
            v = window(vc_ref, vp_ref, m)
            v_heads = jnp.concatenate([jnp.where(head_v == hh, v, jnp.zeros_like(v)) for hh in range(HEADS)], axis=0)
            rows = slice(m * BAND_TILE, (m + 1) * BAND_TILE)
            o_ref[0, rows, :] = jnp.dot(p_ref[m % 2], v_heads, preferred_element_type=F32).astype(o_ref.dtype)
            lse_ref[0, rows, :] = lse_tiles.pop(m)

    _staggered(n_blocks * HEADS, scores, update)


def _banded(q, k, v, bias, sink, rate, use_sink):
    b, s, _ = q.shape
    ln = s // rate
    step = min(BAND_STEP, ln)
    per_step = step // BAND_TILE
    fold = lambda t: t.reshape(b, ln, rate * GROUP)
    cur = pl.BlockSpec((1, step, GROUP), lambda bb, r, i: (bb, i, r))
    prev = pl.BlockSpec((1, BAND_TILE, GROUP), lambda bb, r, i: (bb, jnp.maximum(i * per_step - 1, 0), r))
    o, lse = pl.pallas_call(
        functools.partial(_band_kernel, use_sink=use_sink),
        grid=(b, rate, ln // step),
        in_specs=[pl.BlockSpec(memory_space=pltpu.SMEM), cur, prev, cur, prev, cur,
                  pl.BlockSpec((HEADS, 1, BAND_TILE, 2 * BAND_TILE), lambda bb, r, i: (0, 0, 0, 0))],
        out_specs=[cur, pl.BlockSpec((1, step, LANES), lambda bb, r, i: (bb, i, r))],
        out_shape=[jax.ShapeDtypeStruct((b, ln, rate * GROUP), BF16), jax.ShapeDtypeStruct((b, ln, rate * LANES), F32)],
        scratch_shapes=[pltpu.VMEM((2, BAND_TILE, HEADS * 2 * BAND_TILE), BF16)],
        compiler_params=pltpu.CompilerParams(dimension_semantics=("arbitrary",) * 3),
        name=f"banded_attention_r{rate}",
    )(sink, fold(q), fold(k), fold(k), fold(v), fold(v), bias)
    return o.reshape(b, s, GROUP), lse.reshape(b, s, LANES)


def _flash_reset(m_ref, acc_ref):
    m_ref[...] = jnp.full(m_ref.shape, NEG_INF, F32)
    acc_ref[...] = jnp.zeros(acc_ref.shape, F32)


def _flash_update(n, s, v_t, m_ref, acc_ref):
    m_old = m_ref[n]
    m_new = jnp.maximum(m_old, jnp.max(s, axis=0, keepdims=True))
    alpha = jnp.exp2(m_old - m_new)
    p = jnp.exp2(s - m_new)
    acc_ref[n] = alpha * acc_ref[n] + jnp.dot(v_t, p.astype(BF16), preferred_element_type=F32)
    m_ref[n] = m_new


def _flash_result(n, acc_ref):
    return acc_ref[n, 0:HEAD_DIM, :] / acc_ref[n, HEAD_DIM:HEAD_DIM + 1, :]


def _staggered(n_items, scores, update):
    pending = {n: scores(n) for n in range(min(MXU_LOOKAHEAD, n_items))}
    for n in range(n_items):
        if n + MXU_LOOKAHEAD < n_items:
            pending[n + MXU_LOOKAHEAD] = scores(n + MXU_LOOKAHEAD)
        update(n, pending.pop(n))


def _pipelined_tiles(n_tiles, n_chains, group, load_tile, scores, update, next_ref):
    ahead = next_ref.shape[0]
    n_items = group * n_chains
    assert ahead <= n_chains

    def body(trip, _):
        base = trip * group
        tiles, pending = {}, {}
        for n in range(n_items):
            cur = next_ref[n] if n < ahead else pending.pop(n)
            g, c = divmod(n + ahead, n_chains)
            if g not in tiles:
                tiles[g] = load_tile(base + g)
            new = scores(tiles[g], base + g, c)
            if n + ahead < n_items:
                pending[n + ahead] = new
            else:
                next_ref[n + ahead - n_items] = new
            update(base + n // n_chains, n % n_chains, cur)

    first = load_tile(0)
    for n in range(ahead):
        next_ref[n] = scores(first, 0, n)
    lax.fori_loop(0, (n_tiles + group - 1) // group, body, None)


def _flash_scratch(chains, ahead):
    return [pltpu.VMEM((chains, 1, QT), F32), pltpu.VMEM((chains, V_ROWS, QT), F32),
            pltpu.VMEM((ahead, KT, QT), F32)]


N_NEAR = -(-(FAR_DIST + KT - 1) // QT)
N_BIAS_TILES = N_NEAR + 2
DIFF_TILE_GROUP = 2
SLC_TILE_GROUP = 2


def _bias_tile_index(i, j):
    return jnp.where(j > i, N_NEAR + 1, jnp.minimum(i - j, N_NEAR))


def _diff_kernel(q_ref, k_ref, v_ref, bias_ref, lam_ref, subln_ref, o_ref, qz_ref, m_ref, acc_ref, next_ref, ot_ref, *,
                 lambda_init):
    i = pl.program_id(1)
    q = q_ref[0]
    row = lax.broadcasted_iota(jnp.int32, (GROUP, QT), 0) // DIFF_QK_DIM
    for n in range(2 * HEADS):
        qz_ref[n] = jnp.where(row == n, q, jnp.zeros_like(q))
    _flash_reset(m_ref, acc_ref)

    def load_tile(j):
        return k_ref[0, pl.ds(pl.multiple_of(jnp.minimum(j, i) * KT, KT), KT), :]

    def scores(k, j, n):
        return jnp.dot(k, qz_ref[n], preferred_element_type=F32) + bias_ref[n // 2, _bias_tile_index(i, j)]

    def update(j, n, s):
        h = n // 2
        _flash_update(n, s, v_ref[0, jnp.minimum(j, i), h * V_ROWS:(h + 1) * V_ROWS, :], m_ref, acc_ref)

    _pipelined_tiles(i + 1, 2 * HEADS, DIFF_TILE_GROUP, load_tile, scores, update, next_ref)

    lam_p = lam_ref[...]
    lam = (jnp.exp(jnp.sum(lam_p[0:1] * lam_p[1:2], axis=1, keepdims=True))
           - jnp.exp(jnp.sum(lam_p[2:3] * lam_p[3:4], axis=1, keepdims=True)) + lambda_init)
    for h in range(HEADS):
        o = _flash_result(2 * h, acc_ref) - lam * _flash_result(2 * h + 1, acc_ref)
        msq = jnp.mean(o * o, axis=0, keepdims=True)
        ot_ref[h * HEAD_DIM:(h + 1) * HEAD_DIM, :] = (o * lax.rsqrt(msq + RMS_EPS) * subln_ref[...]
                                                      * (1.0 - lambda_init))
    o_ref[0] = ot_ref[...].T.astype(o_ref.dtype)


def _diff_attention(q_t, k, v_t, bias, lam_p, subln, lambda_init):
    b, s, _ = k.shape
    nq = s // QT
    nkv = s // KT
    v4 = v_t.reshape(b, nkv, HEADS * V_ROWS, KT)
    return pl.pallas_call(
        functools.partial(_diff_kernel, lambda_init=lambda_init),
        grid=(b, nq),
        in_specs=[pl.BlockSpec((1, GROUP, QT), lambda bb, i: (bb * nq + i, 0, 0)),
                  pl.BlockSpec((1, s, GROUP), lambda bb, i: (bb, 0, 0)),
                  pl.BlockSpec((1, nkv, HEADS * V_ROWS, KT), lambda bb, i: (bb, 0, 0, 0)),
                  pl.BlockSpec((HEADS, N_BIAS_TILES, KT, QT), lambda bb, i: (0, 0, 0, 0)),
                  pl.BlockSpec((4, DIFF_QK_DIM), lambda bb, i: (0, 0)),
                  pl.BlockSpec((HEAD_DIM, 1), lambda bb, i: (0, 0))],
        out_specs=pl.BlockSpec((1, QT, GROUP), lambda bb, i: (bb, i, 0)),
        out_shape=jax.ShapeDtypeStruct((b, s, GROUP), BF16),
        scratch_shapes=[pltpu.VMEM((2 * HEADS, GROUP, QT), BF16)] + _flash_scratch(2 * HEADS, MXU_LOOKAHEAD)
        + [pltpu.VMEM((GROUP, QT), F32)],
        compiler_params=pltpu.CompilerParams(dimension_semantics=("arbitrary", "arbitrary"),
                                             vmem_limit_bytes=VMEM_LIMIT),
        name="diff_attention",
    )(q_t, k, v4, bias, lam_p, subln)


def _compress_kernel(ch_ref, ptop_ref, pbot_ref, w1t_ref, w1b_ref, b1_ref, w2k_ref, b2k_ref, w2v_ref, b2v_ref,
                     gk_ref, kc_ref, vct_ref):
    ch = ch_ref[0]
    n_c = ch.shape[0]
    u = jnp.dot((ch + ptop_ref[...]).astype(BF16), w1t_ref[...], preferred_element_type=F32)
    v = jnp.dot((ch + pbot_ref[...]).astype(BF16), w1b_ref[...], preferred_element_type=F32)
    v_next = pltpu.roll(v, n_c - 1, 0)
    hid = jax.nn.gelu(u + v_next + b1_ref[...])
    hk = hid[:, :CMP_HIDDEN].astype(BF16)
    hv = hid[:, CMP_HIDDEN:].astype(BF16)
    kc = jnp.dot(hk, w2k_ref[...], preferred_element_type=F32) + b2k_ref[...]
    msq = jnp.mean(kc * kc, axis=-1, keepdims=True)
    kc_ref[0] = (kc * lax.rsqrt(msq + RMS_EPS) * gk_ref[...]).astype(kc_ref.dtype)
    vct = lax.dot_general(w2v_ref[...], hv, NT_DIMS, preferred_element_type=F32) + b2v_ref[...]
    vct_ref[0] = vct.astype(vct_ref.dtype)


def _compress(chunks, ptop, pbot, w1t, w1b, b1, w2k, b2k, w2v, b2v, gk):
    b, n_c, width = chunks.shape
    const = lambda a: pl.BlockSpec(a.shape, lambda bb: (0,) * a.ndim)
    params = (ptop, pbot, w1t, w1b, b1, w2k, b2k, w2v, b2v, gk)
    return pl.pallas_call(
        _compress_kernel,
        grid=(b,),
        in_specs=[pl.BlockSpec((1, n_c, width), lambda bb: (bb, 0, 0))] + [const(a) for a in params],
        out_specs=[pl.BlockSpec((1, n_c, HEAD_DIM), lambda bb: (bb, 0, 0)),
                   pl.BlockSpec((1, HEAD_DIM, n_c), lambda bb: (bb, 0, 0))],
        out_shape=[jax.ShapeDtypeStruct((b, n_c, HEAD_DIM), BF16), jax.ShapeDtypeStruct((b, HEAD_DIM, n_c), BF16)],
        compiler_params=pltpu.CompilerParams(dimension_semantics=("arbitrary",), vmem_limit_bytes=VMEM_LIMIT),
        name="nsa_compress",
    )(chunks, *params)


def _cmp_attn_kernel(q_ref, kc_ref, vct_ref, bias_ref, o_ref, sel_ref, p_ref, *, n_sel):
    i = pl.program_id(0)
    kc = kc_ref[0]
    vct = vct_ref[0]
    n_c = kc.shape[0]
    n_blk = sel_ref.shape[1]
    probs = []

    def scores(h):
        return jnp.dot(kc, q_ref[0, h * HEAD_DIM:(h + 1) * HEAD_DIM, :], preferred_element_type=F32) + bias_ref[h, 0]

    def update(h, s):
        m = jnp.maximum(jnp.max(s, axis=0, keepdims=True), 0.5 * NEG_INF)
        p = jnp.exp2(s - m)
        den = jnp.sum(p, axis=0, keepdims=True)
        p = p * (1.0 / jnp.maximum(den, TINY))
        o_ref[0, h * HEAD_DIM:(h + 1) * HEAD_DIM, :] = jnp.dot(vct, p.astype(BF16),
                                                               preferred_element_type=F32).astype(o_ref.dtype)
        probs.append(p)

    _staggered(HEADS, scores, update)
    psum = (probs[0] + probs[1]) + (probs[2] + probs[3])
    per_blk = SLC_BLOCK // CMP_STRIDE
    halves = []
    for half in range(QT // LANES):
        p_ref[half, 0:8, :] = jnp.zeros((8, LANES), F32)
        p_ref[half, 8:8 + n_c, :] = psum[:, half * LANES:(half + 1) * LANES]
        p_ref[half, 8 + n_c:16 + n_c, :] = jnp.zeros((8, LANES), F32)
        acc = p_ref[half, pl.ds(7, n_blk, stride=per_blk), :]
        for t in range(per_blk):
            acc = acc + p_ref[half, pl.ds(8 + t, n_blk, stride=per_blk), :]
        halves.append(acc)
    imp = jnp.concatenate(halves, axis=1)
    blk = lax.broadcasted_iota(jnp.int32, (n_blk, QT), 0)
    cur = (i * QT + lax.broadcasted_iota(jnp.int32, (n_blk, QT), 1)) // SLC_BLOCK
    forced = (blk == 0) | (blk == cur) | (blk == cur - 1)
    val = jnp.where(forced, FORCE_SELECT, jnp.where(blk <= cur, imp, NEG_INF))
    sel = jnp.zeros((n_blk, QT), jnp.bool_)
    for _ in range(n_sel):
        top = jnp.max(val, axis=0, keepdims=True)
        idx = jnp.min(jnp.where(val == top, blk, n_blk), axis=0, keepdims=True)
        hit = blk == idx
        sel = sel | hit
        val = jnp.where(hit, -3.0e38, val)
    sel_ref[0] = jnp.where(sel, 1.0, 0.0).astype(sel_ref.dtype)


def _cmp_attention(q_t, kc, vct, bias, b):
    nt = q_t.shape[0]
    nq = nt // b
    n_c = kc.shape[1]
    n_blk = nq * QT // SLC_BLOCK
    return pl.pallas_call(
        functools.partial(_cmp_attn_kernel, n_sel=min(SLC_TOPK, n_blk)),
        grid=(nq, b),
        in_specs=[pl.BlockSpec((1, GROUP, QT), lambda i, bb: (bb * nq + i, 0, 0)),
                  pl.BlockSpec((1, n_c, HEAD_DIM), lambda i, bb: (bb, 0, 0)),
                  pl.BlockSpec((1, HEAD_DIM, n_c), lambda i, bb: (bb, 0, 0)),
                  pl.BlockSpec((HEADS, 1, n_c, QT), lambda i, bb: (0, 0, 0, i))],
        out_specs=[pl.BlockSpec((1, GROUP, QT), lambda i, bb: (bb * nq + i, 0, 0)),
                   pl.BlockSpec((1, n_blk, QT), lambda i, bb: (bb * nq + i, 0, 0))],
        out_shape=[jax.ShapeDtypeStruct((nt, GROUP, QT), BF16), jax.ShapeDtypeStruct((nt, n_blk, QT), BF16)],
        scratch_shapes=[pltpu.VMEM((QT // LANES, n_c + 16, LANES), F32)],
        compiler_params=pltpu.CompilerParams(dimension_semantics=("arbitrary", "arbitrary"),
                                             vmem_limit_bytes=VMEM_LIMIT),
        name="nsa_compressed_attention",
    )(q_t, kc, vct, bias)


SEL_REP = 8
N_WIN = -(-(NSA_WINDOW - 1 + KT - 1) // QT)


def _slc_win_kernel(q_ref, ksw_ref, vs_ref, vw_ref, sel_ref, rep_ref, ocmp_ref, g_ref, bslc_ref, bwin_ref,
                    o_ref, qz_ref, m_ref, acc_ref, next_ref, ot_ref, mask_ref):
    i = pl.program_id(1)
    sel8 = jnp.dot(rep_ref[...], sel_ref[0], preferred_element_type=F32)
    mask_ref[...] = (sel8 - 1.0) * (-NEG_INF)
    blocks_per_tile = KT // SLC_BLOCK
    mrows = blocks_per_tile * SEL_REP
    zeros = jnp.zeros((HEAD_DIM, QT), BF16)
    for h in range(HEADS):
        qh = q_ref[0, h * HEAD_DIM:(h + 1) * HEAD_DIM, :]
        qz_ref[h] = jnp.concatenate([qh, zeros], axis=0)
        qz_ref[HEADS + h] = jnp.concatenate([zeros, qh], axis=0)
    _flash_reset(m_ref, acc_ref)

    def load_keys(j):
        return ksw_ref[0, pl.ds(pl.multiple_of(j * KT, KT), KT), :]

    def load_tile(j):
        j = jnp.minimum(j, i)
        m8 = mask_ref[pl.ds(pl.multiple_of(j * mrows, mrows), mrows), :]
        mask = jnp.broadcast_to(m8.reshape(blocks_per_tile, 1, SEL_REP, QT),
                                (blocks_per_tile, SLC_BLOCK // SEL_REP, SEL_REP, QT)).reshape(KT, QT)
        return load_keys(j), mask

    def slc_scores(tile, j, h):
        k, mask = tile
        return jnp.dot(k, qz_ref[h], preferred_element_type=F32) + mask + bslc_ref[h, _bias_tile_index(i, j)]

    def slc_update(j, h, s):
        _flash_update(h, s, vs_ref[0, jnp.minimum(j, i)], m_ref, acc_ref)

    _pipelined_tiles(i + 1, HEADS, SLC_TILE_GROUP, load_tile, slc_scores, slc_update, next_ref)

    def win_tile(n):
        d = N_WIN - 1 - n // HEADS
        return d, n % HEADS, jnp.maximum(i - d, 0)

    def win_scores(n):
        d, h, j = win_tile(n)
        missing = jnp.where(i < d, NEG_INF, 0.0).astype(F32)
        return jnp.dot(load_keys(j), qz_ref[HEADS + h], preferred_element_type=F32) + (bwin_ref[h, d] + missing)

    def win_update(n, s):
        _, h, j = win_tile(n)
        _flash_update(HEADS + h, s, vw_ref[0, j], m_ref, acc_ref)

    _staggered(N_WIN * HEADS, win_scores, win_update)

    for h in range(HEADS):
        g = jax.nn.sigmoid(g_ref[0, 3 * h:3 * h + 3, :])
        ot_ref[h * HEAD_DIM:(h + 1) * HEAD_DIM, :] = (g[0:1] * ocmp_ref[0, h * HEAD_DIM:(h + 1) * HEAD_DIM, :]
                                                      + g[1:2] * _flash_result(h, acc_ref)
                                                      + g[2:3] * _flash_result(HEADS + h, acc_ref))
    o_ref[0] = ot_ref[...].T.astype(o_ref.dtype)


def _slc_win_attention(q_t, ksw, vs_t, vw_t, sel, rep, ocmp, g_t, bslc, bwin):
    b, s, _ = ksw.shape
    nq = s // QT
    nkv = s // KT
    n_blk = s // SLC_BLOCK
    tile = lambda height: pl.BlockSpec((1, height, QT), lambda bb, i: (bb * nq + i, 0, 0))
    whole = lambda a: pl.BlockSpec(a.shape, lambda bb, i: (0,) * a.ndim)
    return pl.pallas_call(
        _slc_win_kernel,
        grid=(b, nq),
        in_specs=[tile(GROUP),
                  pl.BlockSpec((1, s, 2 * HEAD_DIM), lambda bb, i: (bb, 0, 0)),
                  pl.BlockSpec((1, nkv, V_ROWS, KT), lambda bb, i: (bb, 0, 0, 0)),
                  pl.BlockSpec((1, nkv, V_ROWS, KT), lambda bb, i: (bb, 0, 0, 0)),
                  tile(n_blk), whole(rep), tile(GROUP), tile(16), whole(bslc), whole(bwin)],
        out_specs=pl.BlockSpec((1, QT, GROUP), lambda bb, i: (bb, i, 0)),
        out_shape=jax.ShapeDtypeStruct((b, s, GROUP), BF16),
        scratch_shapes=[pltpu.VMEM((2 * HEADS, 2 * HEAD_DIM, QT), BF16)] + _flash_scratch(2 * HEADS, MXU_LOOKAHEAD)
        + [pltpu.VMEM((GROUP, QT), F32), pltpu.VMEM((n_blk * SEL_REP, QT), F32)],
        compiler_params=pltpu.CompilerParams(dimension_semantics=("arbitrary", "arbitrary"),
                                             vmem_limit_bytes=VMEM_LIMIT),
        name="nsa_selected_window_attention",
    )(q_t, ksw, vs_t.reshape(b, nkv, V_ROWS, KT), vw_t.reshape(b, nkv, V_ROWS, KT), sel, rep, ocmp, g_t,
      bslc, bwin)


def _out_kernel(x_ref, a0_ref, a1_ref, a2_ref, l0_ref, l1_ref, l2_ref, ob_ref, oc_ref, od_ref, gate_ref, e_ref,
                w_ref, o_ref):
    l0, l1, l2 = l0_ref[...], l1_ref[...], l2_ref[...]
    mx = jnp.maximum(jnp.maximum(l0, l1), l2)
    e0, e1, e2 = jnp.exp(l0 - mx), jnp.exp(l1 - mx), jnp.exp(l2 - mx)
    den = e0 + e1 + e2

    def per_head_lanes(w):
        hi = w.astype(BF16)
        lo = (w - hi.astype(F32)).astype(BF16)
        return (jnp.dot(hi, e_ref[...], preferred_element_type=F32)
                + jnp.dot(lo, e_ref[...], preferred_element_type=F32))

    o_a = (per_head_lanes(e0 / den) * a0_ref[...] + per_head_lanes(e1 / den) * a1_ref[...]
           + per_head_lanes(e2 / den) * a2_ref[...])
    y = jnp.concatenate([o_a, ob_ref[...].astype(F32), oc_ref[...].astype(F32), od_ref[...].astype(F32)], axis=1)
    g = gate_ref[...].astype(F32)
    y = y * (g * jax.nn.sigmoid(g))
    o_ref[...] = x_ref[...] + jnp.dot(y.astype(BF16), w_ref[...], preferred_element_type=F32)


def _out_projection(x2, a_outs, a_lses, o_b, o_c, o_d, gate, w_out):
    m, d = x2.shape
    rowblk = lambda width: pl.BlockSpec((PROJ_ROWS, width), lambda i: (i, 0))
    head_of_lane = np.arange(GROUP) // HEAD_DIM
    expand = jnp.asarray((np.arange(LANES)[:, None] == head_of_lane[None, :]).astype(np.float32), BF16)
    return pl.pallas_call(
        _out_kernel,
        grid=(m // PROJ_ROWS,),
        in_specs=[rowblk(d)] + [rowblk(GROUP)] * 3 + [rowblk(LANES)] * 3 + [rowblk(GROUP)] * 3
        + [rowblk(N_MIXERS * GROUP), pl.BlockSpec((LANES, GROUP), lambda i: (0, 0)),
           pl.BlockSpec((N_MIXERS * GROUP, d), lambda i: (0, 0))],
        out_specs=rowblk(d),
        out_shape=jax.ShapeDtypeStruct((m, d), F32),
        compiler_params=pltpu.CompilerParams(dimension_semantics=("arbitrary",), vmem_limit_bytes=VMEM_LIMIT),
        name="out_projection",
    )(x2, *a_outs, *a_lses, o_b, o_c, o_d, gate, expand, w_out)


def _block_diag_mean(group):
    idx = np.arange(GROUP) // group
    return jnp.asarray((idx[:, None] == idx[None, :]).astype(np.float32) / group, BF16)


def _layer_weights(w_in, qk_gain, qk_gain_diff):
    d = w_in.shape[0]
    sizes = (GROUP,) * 3 + (GROUP, GROUP // 2, GROUP // 2) + (GROUP,) * 3 + (GROUP,) + (HEAD_DIM,) * 6 \
        + (HEADS * 3, N_MIXERS * GROUP)
    offs = np.concatenate([[0], np.cumsum(sizes)])
    col = lambda n: w_in[:, offs[n]:offs[n + 1]]
    (a_q, a_k, a_v, b_q, b_k, b_v, c_q, c_k, c_v, d_q, d_kc, d_vc, d_ks, d_vs, d_kw, d_vw, d_g, gate) = \
        [col(n) for n in range(18)]
    rep_kv = lambda w: jnp.repeat(w.reshape(d, 2, HEAD_DIM), 2, axis=1).reshape(d, GROUP)
    wrm = jnp.concatenate([a_q, a_k, a_v, b_q, rep_kv(b_k), rep_kv(b_v), c_k, d_kc, d_vc, d_ks, d_kw, gate], axis=1)
    wt = jnp.concatenate([c_q, c_v, d_q, d_vs, d_vw, d_g, jnp.zeros((d, 16 - HEADS * 3), w_in.dtype)], axis=1).T
    g = qk_gain
    ones = lambda n: jnp.ones((n,), F32)
    tile4 = lambda v: jnp.tile(v, HEADS)
    scale = HEAD_DIM ** -0.5
    grm = jnp.concatenate([tile4(g[0]) * scale, tile4(g[1]), ones(GROUP), tile4(g[2]) * scale, tile4(g[3]),
                           ones(GROUP), jnp.tile(qk_gain_diff[1], 2 * HEADS), ones(2 * HEAD_DIM), g[6], g[7],
                           ones(N_MIXERS * GROUP)])
    gt = jnp.concatenate([jnp.tile(qk_gain_diff[0], 2 * HEADS) * (DIFF_QK_DIM ** -0.5 * LOG2E), ones(GROUP),
                          tile4(g[4]) * (scale * LOG2E), ones(2 * HEAD_DIM + 16)])
    return wrm.astype(BF16), wt.astype(BF16), grm.reshape(1, -1), gt.reshape(-1, 1)


def _compress_weights(cmp_pos, cmp_w1, cmp_b1, cmp_w2, cmp_b2):
    half = CMP_LEN // 2
    pos = jnp.concatenate([cmp_pos[0], cmp_pos[1]], axis=-1)
    ptop = pos[:half].reshape(1, -1)
    pbot = pos[half:].reshape(1, -1)
    w1 = cmp_w1.reshape(2, CMP_LEN, HEAD_DIM, CMP_HIDDEN)
    zeros = jnp.zeros_like(w1[0])
    w1cat = jnp.concatenate([jnp.concatenate([w1[0], zeros], axis=-1),
                             jnp.concatenate([zeros, w1[1]], axis=-1)], axis=1)
    w1t = w1cat[:half].reshape(half * 2 * HEAD_DIM, 2 * CMP_HIDDEN).astype(BF16)
    w1b = w1cat[half:].reshape(half * 2 * HEAD_DIM, 2 * CMP_HIDDEN).astype(BF16)
    b1 = jnp.concatenate([cmp_b1[0], cmp_b1[1]]).reshape(1, -1)
    return (ptop, pbot, w1t, w1b, b1, cmp_w2[0].astype(BF16), cmp_b2[0].reshape(1, -1),
            cmp_w2[1].T.astype(BF16), cmp_b2[1].reshape(-1, 1))


def kernel(x, rel_bias_table, norm_w, w_in, w_out, qk_gain, qk_gain_diff, attn_sinks, diff_lambda, diff_subln,
           cmp_pos, cmp_w1, cmp_b1, cmp_w2, cmp_b2):
    b, s, d = x.shape
    depth = w_in.shape[0]
    n_c = s // CMP_STRIDE
    n_blk = s // SLC_BLOCK
    assert s % (BAND_TILE * DILATED_CONFIGS[-1][1]) == 0 and s % PROJ_ROWS == 0 and d == N_MIXERS * GROUP

    table = rel_bias_table.astype(F32)
    band_bias = [_build_bias(table, head0=0, n_d=1, rows=BAND_TILE, cols=2 * BAND_TILE, base0=BAND_TILE, dstep=0,
                             rs=1, cs=-1, dscale=rate, max_dist=window // rate) for window, rate in DILATED_CONFIGS]
    swa_bias = _build_bias(table, head0=HEADS, n_d=1, rows=BAND_TILE, cols=2 * BAND_TILE, base0=BAND_TILE, dstep=0,
                           rs=1, cs=-1, max_dist=SWA_WINDOW - 1)
    flash_tiles = dict(rows=KT, cols=QT, base0=0, dstep=QT, rs=-1, cs=1, scale=LOG2E)
    diff_bias = _build_bias(table, head0=2 * HEADS, n_d=N_BIAS_TILES, d_valid=N_NEAR + 1, **flash_tiles)
    slc_bias = _build_bias(table, head0=3 * HEADS, n_d=N_BIAS_TILES, d_valid=N_NEAR + 1, **flash_tiles)
    win_bias = _build_bias(table, head0=3 * HEADS, n_d=N_WIN, max_dist=NSA_WINDOW - 1, **flash_tiles)
    cmp_bias = _build_bias(table, head0=3 * HEADS, n_d=1, rows=n_c, cols=s, base0=-(CMP_LEN - 1), dstep=0,
                           rs=-CMP_STRIDE, cs=1, r_valid=n_c - 1, col_tile=4 * QT, scale=LOG2E)
    e64, e32 = _block_diag_mean(HEAD_DIM), _block_diag_mean(DIFF_QK_DIM)
    rep_idx = np.arange(n_blk * SEL_REP) // SEL_REP
    rep = jnp.asarray((rep_idx[:, None] == np.arange(n_blk)[None, :]).astype(np.float32), BF16)
    no_sink = jnp.zeros((HEADS,), F32)

    x2 = x.reshape(b * s, d)
    for layer in range(depth):
        wrm, wt, grm, gt = _layer_weights(w_in[layer], qk_gain[layer], qk_gain_diff[layer])
        (a_q, a_k, a_v, b_q, b_k, b_v, c_k, kvc, ksw, gate, c_qt, c_vt, d_qt, d_vst, d_vwt, d_gt) = _project(
            x2, norm_w[layer].reshape(1, d), wrm, wt, grm, gt, e64, e32)
        seq = lambda t: t.reshape(b, s, t.shape[-1])
        a_res = [_banded(seq(a_q), seq(a_k), seq(a_v), band_bias[n], no_sink, rate, False)
                 for n, (_, rate) in enumerate(DILATED_CONFIGS)]
        o_b, _ = _banded(seq(b_q), seq(b_k), seq(b_v), swa_bias, attn_sinks[layer].astype(F32), 1, True)
        lambda_init = 0.8 - 0.6 * math.exp(-0.3 * layer)
        o_c = _diff_attention(c_qt, seq(c_k), c_vt, diff_bias, diff_lambda[layer].astype(F32),
                              diff_subln[layer].reshape(HEAD_DIM, 1).astype(F32), lambda_init)
        cw = _compress_weights(cmp_pos[layer], cmp_w1[layer], cmp_b1[layer], cmp_w2[layer], cmp_b2[layer])
        kc, vct = _compress(kvc.reshape(b, n_c, CMP_STRIDE * 2 * HEAD_DIM), *cw, qk_gain[layer, 5].reshape(1, -1))
        o_cmp, sel = _cmp_attention(d_qt, kc, vct, cmp_bias, b)
        o_d = _slc_win_attention(d_qt, seq(ksw), d_vst, d_vwt, sel, rep, o_cmp, d_gt, slc_bias, win_bias)
        x2 = _out_projection(x2, [r[0].reshape(b * s, GROUP) for r in a_res],
                             [r[1].reshape(b * s, LANES) for r in a_res], o_b.reshape(b * s, GROUP),
                             o_c.reshape(b * s, GROUP), o_d.reshape(b * s, GROUP), gate, w_out[layer].astype(BF16))
    return x2.reshape(b, s, d)
```

```python
import functools
import math

import numpy as np
import jax
import jax.numpy as jnp
from jax import lax
from jax.experimental import pallas as pl
from jax.experimental.pallas import tpu as pltpu

F32 = jnp.float32
BF16 = jnp.bfloat16

HEAD_DIM = 64
HEADS = 4
GROUP = HEADS * HEAD_DIM
N_MIXERS = 4
NUM_BUCKETS = 32
REL_MAX_DIST = 2048
DILATED_CONFIGS = ((128, 1), (512, 4), (2048, 16))
FOLD_RATES = tuple(rate for _, rate in DILATED_CONFIGS if rate > 1)
SWA_WINDOW = 128
DIFF_QK_DIM = HEAD_DIM // 2
CMP_LEN = 32
CMP_STRIDE = 16
CMP_HIDDEN = 256
SLC_BLOCK = 64
SLC_TOPK = 16
NSA_WINDOW = 512
RMS_EPS = 1e-6
NEG_INF = -1e30
FORCE_SELECT = 1e9
TINY = 1e-30
LOG2E = math.log2(math.e)

PROJ_ROWS = 512
BAND_TILE = 128
BAND_STEP = 512
BIAS_ROW_CHUNK = 64
LANES = 128
QT = 256
KT = 256
VMEM_LIMIT = 56 * 1024 * 1024
MXU_LOOKAHEAD = 4
V_ROWS = HEAD_DIM + 16

NT_DIMS = (((1,), (1,)), ((), ()))


def _t5_thresholds():
    n = np.arange(0, 4 * REL_MAX_DIST)
    max_exact = NUM_BUCKETS // 2
    nf = np.maximum(n, 1).astype(np.float32)
    large = max_exact + (np.log(nf / np.float32(max_exact)) / np.float32(math.log(REL_MAX_DIST / max_exact))
                         * np.float32(NUM_BUCKETS - max_exact)).astype(np.int32)
    bucket = np.where(n < max_exact, n, np.minimum(large, NUM_BUCKETS - 1))
    return [int(np.argmax(bucket >= b)) for b in range(NUM_BUCKETS)]


T5_THRESHOLDS = _t5_thresholds()
FAR_DIST = T5_THRESHOLDS[-1]


def _bias_kernel(tbl_ref, out_ref, *, head0, base0, dstep, rs, cs, dscale, max_dist, r_valid, d_valid, col_tile,
                 scale):
    h = pl.program_id(0)
    d = pl.program_id(1)
    ct = pl.program_id(2)
    rows, cols = out_ref.shape[-2:]
    chunk = BIAS_ROW_CHUNK if rows % BIAS_ROW_CHUNK == 0 else rows
    for r0 in range(0, rows, chunk):
        origin = base0 + d * dstep + r0 * rs + ct * col_tile * cs
        corners = [origin + dr * rs + dc * cs for dr in (0, chunk - 1) for dc in (0, cols - 1)]
        lo = functools.reduce(jnp.minimum, corners)
        hi = functools.reduce(jnp.maximum, corners)
        all_masked = (hi < 0) | (lo > max_dist) | (d >= d_valid) | (r0 >= r_valid)
        all_far = (lo * dscale >= FAR_DIST) & (hi <= max_dist) & (d < d_valid) & (r0 + chunk <= r_valid)
        out = out_ref.at[0, 0, r0:r0 + chunk, :]

        @pl.when(all_masked)
        def _():
            out[...] = jnp.full((chunk, cols), NEG_INF, F32)

        @pl.when(all_far)
        def _():
            out[...] = jnp.full((chunk, cols), tbl_ref[NUM_BUCKETS - 1, head0 + h] * scale, F32)

        @pl.when(jnp.logical_not(all_masked | all_far))
        def _():
            r = lax.broadcasted_iota(jnp.int32, (chunk, cols), 0)
            c = lax.broadcasted_iota(jnp.int32, (chunk, cols), 1)
            dist = origin + r * rs + c * cs
            n = dist * dscale
            val = jnp.full((chunk, cols), tbl_ref[0, head0 + h], F32)
            for b in range(1, NUM_BUCKETS):
                val = jnp.where(n >= T5_THRESHOLDS[b], tbl_ref[b, head0 + h], val)
            valid = (dist >= 0) & (dist <= max_dist) & (r + r0 < r_valid) & (d < d_valid)
            out[...] = jnp.where(valid, val * scale, NEG_INF)


def _build_bias(table, *, head0, n_d, rows, cols, base0, dstep, rs, cs, dscale=1, max_dist=1 << 30,
                r_valid=1 << 30, d_valid=1 << 30, col_tile=None, scale=1.0):
    col_tile = cols if col_tile is None else col_tile
    kern = functools.partial(_bias_kernel, head0=head0, base0=base0, dstep=dstep, rs=rs, cs=cs, dscale=dscale,
                             max_dist=max_dist, r_valid=r_valid, d_valid=d_valid, col_tile=col_tile, scale=scale)
    return pl.pallas_call(
        kern,
        grid=(HEADS, n_d, cols // col_tile),
        in_specs=[pl.BlockSpec(memory_space=pltpu.SMEM)],
        out_specs=pl.BlockSpec((1, 1, rows, col_tile), lambda h, d, c: (h, d, 0, c)),
        out_shape=jax.ShapeDtypeStruct((HEADS, n_d, rows, cols), F32),
        name="rel_bias_tiles",
    )(table)


RM_AQ, RM_AK, RM_AV = 0, 256, 512
RM_BQ, RM_BK, RM_BV = 768, 1024, 1280
RM_CK = 1536
RM_KVC = 1792
RM_KSW = 1920
RM_GATE = 2048
RM_COLS = 3072
TR_CQ, TR_CV, TR_DQ, TR_DVS, TR_DVW, TR_DG = 0, 256, 512, 768, 832, 896
TR_ROWS = 912


def _proj_kernel(x_ref, nw_ref, wrm_ref, wt_ref, grm_ref, gt_ref, e64_ref, e32_ref,
                 aq_ref, ak_ref, av_ref, bq_ref, bk_ref, bv_ref, ck_ref, kvc_ref, ksw_ref, gate_ref,
                 aq4_ref, ak4_ref, av4_ref, aq16_ref, ak16_ref, av16_ref,
                 cq_ref, cv_ref, dq_ref, dvs_ref, dvw_ref, dg_ref, fold_ref):
    x = x_ref[...]
    ms = jnp.mean(x * x, axis=-1, keepdims=True)
    xn = (x * lax.rsqrt(ms + RMS_EPS) * nw_ref[...]).astype(BF16)
    rows = x.shape[0]

    def rm(c0, width):
        return jnp.dot(xn, wrm_ref[:, c0:c0 + width], preferred_element_type=F32)

    def rm_normed(c0, width, e_ref):
        h = rm(c0, width)
        msq = jnp.dot((h * h).astype(BF16), e_ref[0:width, 0:width], preferred_element_type=F32)
        return h * lax.rsqrt(msq + RMS_EPS) * grm_ref[:, c0:c0 + width]

    def put_folded(val, ref, folded_refs):
        ref[...] = val.astype(ref.dtype)
        for half in range(GROUP // LANES):
            fold_ref[half] = val[:, half * LANES:(half + 1) * LANES]
        for rate, fref in zip(FOLD_RATES, folded_refs):
            for rho in range(rate):
                for half in range(GROUP // LANES):
                    c0 = rho * GROUP + half * LANES
                    fref[:, c0:c0 + LANES] = fold_ref[half, pl.ds(rho, rows // rate, stride=rate), :].astype(fref.dtype)

    put_folded(rm_normed(RM_AQ, GROUP, e64_ref), aq_ref, (aq4_ref, aq16_ref))
    put_folded(rm_normed(RM_AK, GROUP, e64_ref), ak_ref, (ak4_ref, ak16_ref))
    put_folded(rm(RM_AV, GROUP), av_ref, (av4_ref, av16_ref))
    bq_ref[...] = rm_normed(RM_BQ, GROUP, e64_ref).astype(bq_ref.dtype)
    bk_ref[...] = rm_normed(RM_BK, GROUP, e64_ref).astype(bk_ref.dtype)
    bv_ref[...] = rm(RM_BV, GROUP).astype(bv_ref.dtype)
    ck_ref[...] = rm_normed(RM_CK, GROUP, e32_ref).astype(ck_ref.dtype)
    kvc_ref[...] = rm(RM_KVC, 2 * HEAD_DIM).astype(kvc_ref.dtype)
    ksw_ref[...] = rm_normed(RM_KSW, 2 * HEAD_DIM, e64_ref).astype(ksw_ref.dtype)
    gate_ref[...] = rm(RM_GATE, N_MIXERS * GROUP).astype(gate_ref.dtype)

    def tr(r0, height):
        return lax.dot_general(wt_ref[r0:r0 + height, :], xn, NT_DIMS, preferred_element_type=F32)

    def tr_normed(r0, height, group):
        h3 = tr(r0, height).reshape(height // group, group, rows)
        msq = jnp.mean(h3 * h3, axis=1, keepdims=True)
        return (h3 * lax.rsqrt(msq + RMS_EPS)).reshape(height, rows) * gt_ref[r0:r0 + height, :]

    def put(ref, val):
        for t in range(rows // QT):
            ref[t] = val[:, t * QT:(t + 1) * QT].astype(ref.dtype)

    def with_ones(v):
        ones = jnp.ones((V_ROWS - HEAD_DIM, rows), F32)
        parts = []
        for h in range(v.shape[0] // HEAD_DIM):
            parts += [v[h * HEAD_DIM:(h + 1) * HEAD_DIM], ones]
        return jnp.concatenate(parts, axis=0)

    put(cq_ref, tr_normed(TR_CQ, GROUP, DIFF_QK_DIM))
    put(cv_ref, with_ones(tr(TR_CV, GROUP)))
    put(dq_ref, tr_normed(TR_DQ, GROUP, HEAD_DIM))
    put(dvs_ref, with_ones(tr(TR_DVS, HEAD_DIM)))
    put(dvw_ref, with_ones(tr(TR_DVW, HEAD_DIM)))
    put(dg_ref, tr(TR_DG, 16))


def _project(x2, nw, wrm, wt, grm, gt, e64, e32):
    m, d = x2.shape
    nt = m // QT
    tpr = PROJ_ROWS // QT
    const = lambda shape: pl.BlockSpec(shape, lambda i: (0,) * len(shape))
    rm_out = lambda width, dtype: (jax.ShapeDtypeStruct((m, width), dtype),
                                   pl.BlockSpec((PROJ_ROWS, width), lambda i: (i, 0)))
    tr_out = lambda height, dtype: (jax.ShapeDtypeStruct((nt, height, QT), dtype),
                                    pl.BlockSpec((tpr, height, QT), lambda i: (i, 0, 0)))
    outs = [rm_out(GROUP, BF16)] * 7 + [rm_out(2 * HEAD_DIM, F32), rm_out(2 * HEAD_DIM, BF16),
                                        rm_out(N_MIXERS * GROUP, BF16)]
    fold_out = lambda rate: (jax.ShapeDtypeStruct((m // rate, rate * GROUP), BF16),
                             pl.BlockSpec((PROJ_ROWS // rate, rate * GROUP), lambda i: (i, 0)))
    outs += [fold_out(rate) for rate in FOLD_RATES for _ in range(3)]
    outs += [tr_out(GROUP, BF16), tr_out(HEADS * V_ROWS, BF16), tr_out(GROUP, BF16), tr_out(V_ROWS, BF16),
             tr_out(V_ROWS, BF16), tr_out(16, F32)]
    return pl.pallas_call(
        _proj_kernel,
        grid=(m // PROJ_ROWS,),
        in_specs=[pl.BlockSpec((PROJ_ROWS, d), lambda i: (i, 0)), const((1, d)), const((d, RM_COLS)),
                  const((TR_ROWS, d)), const((1, RM_COLS)), const((TR_ROWS, 1)), const((GROUP, GROUP)),
                  const((GROUP, GROUP))],
        out_specs=[o[1] for o in outs],
        out_shape=[o[0] for o in outs],
        scratch_shapes=[pltpu.VMEM((GROUP // LANES, PROJ_ROWS, LANES), F32)],
        compiler_params=pltpu.CompilerParams(dimension_semantics=("arbitrary",), vmem_limit_bytes=VMEM_LIMIT),
        name="in_projection",
    )(x2, nw, wrm, wt, grm, gt, e64, e32)


def _band_kernel(sink_ref, q_ref, kp_ref, kc_ref, vp_ref, vc_ref, bias_ref, o_ref, lse_ref, p_ref, *, use_sink):
    i = pl.program_id(2)
    n_blocks = q_ref.shape[1] // BAND_TILE
    head_q = lax.broadcasted_iota(jnp.int32, (BAND_TILE, GROUP), 1) // HEAD_DIM
    head_v = lax.broadcasted_iota(jnp.int32, (2 * BAND_TILE, GROUP), 1) // HEAD_DIM
    lane = lax.broadcasted_iota(jnp.int32, (BAND_TILE, LANES), 1)
    in_prev = lax.broadcasted_iota(jnp.int32, (1, 2 * BAND_TILE), 1) < BAND_TILE
    no_prev = jnp.where(in_prev & (i == 0), NEG_INF, 0.0).astype(F32)

    def window(cur_ref, prev_ref, m):
        if m == 0:
            return jnp.concatenate([prev_ref[0], cur_ref[0, 0:BAND_TILE, :]], axis=0)
        return cur_ref[0, (m - 1) * BAND_TILE:(m + 1) * BAND_TILE, :]

    def scores(n):
        m, h = divmod(n, HEADS)
        q = q_ref[0, m * BAND_TILE:(m + 1) * BAND_TILE, :]
        qh = jnp.where(head_q == h, q, jnp.zeros_like(q))
        bias = bias_ref[h, 0] + no_prev if m == 0 else bias_ref[h, 0]
        return lax.dot_general(qh, window(kc_ref, kp_ref, m), NT_DIMS, preferred_element_type=F32) + bias

    lse_tiles = {}

    def update(n, s):
        m, h = divmod(n, HEADS)
        mx = jnp.max(s, axis=1, keepdims=True)
        if use_sink:
            mx = jnp.maximum(mx, sink_ref[h])
        p = jnp.exp(s - mx)
        den = jnp.sum(p, axis=1, keepdims=True)
        if use_sink:
            den = den + jnp.exp(sink_ref[h] - mx)
        p_ref[m % 2, :, h * 2 * BAND_TILE:(h + 1) * 2 * BAND_TILE] = (p * (1.0 / den)).astype(BF16)
        lse_tiles[m] = jnp.where(lane == h, mx + jnp.log(den), lse_tiles.get(m, jnp.zeros((BAND_TILE, LANES), F32)))
        if h == HEADS - 1:
            v = window(vc_ref, vp_ref, m)
            v_heads = jnp.concatenate([jnp.where(head_v == hh, v, jnp.zeros_like(v)) for hh in range(HEADS)], axis=0)
            rows = slice(m * BAND_TILE, (m + 1) * BAND_TILE)
            o_ref[0, rows, :] = jnp.dot(p_ref[m % 2], v_heads, preferred_element_type=F32).astype(o_ref.dtype)
            lse_ref[0, rows, :] = lse_tiles.pop(m)

    _staggered(n_blocks * HEADS, scores, update)


def _banded(q, k, v, bias, sink, rate, use_sink):
    b, ln, _ = q.shape
    step = min(BAND_STEP, ln)
    per_step = step // BAND_TILE
    cur = pl.BlockSpec((1, step, GROUP), lambda bb, r, i: (bb, i, r))
    prev = pl.BlockSpec((1, BAND_TILE, GROUP), lambda bb, r, i: (bb, jnp.maximum(i * per_step - 1, 0), r))
    o, lse = pl.pallas_call(
        functools.partial(_band_kernel, use_sink=use_sink),
        grid=(b, rate, ln // step),
        in_specs=[pl.BlockSpec(memory_space=pltpu.SMEM), cur, prev, cur, prev, cur,
                  pl.BlockSpec((HEADS, 1, BAND_TILE, 2 * BAND_TILE), lambda bb, r, i: (0, 0, 0, 0))],
        out_specs=[cur, pl.BlockSpec((1, step, LANES), lambda bb, r, i: (bb, i, r))],
        out_shape=[jax.ShapeDtypeStruct((b, ln, rate * GROUP), BF16), jax.ShapeDtypeStruct((b, ln, rate * LANES), F32)],
        scratch_shapes=[pltpu.VMEM((2, BAND_TILE, HEADS * 2 * BAND_TILE), BF16)],
        compiler_params=pltpu.CompilerParams(dimension_semantics=("arbitrary",) * 3),
        name=f"banded_attention_r{rate}",
    )(sink, q, k, k, v, v, bias)
    return o, lse


def _flash_reset(m_ref, acc_ref):
    m_ref[...] = jnp.full(m_ref.shape, NEG_INF, F32)
    acc_ref[...] = jnp.zeros(acc_ref.shape, F32)


def _flash_update(n, s, v_t, m_ref, acc_ref):
    m_old = m_ref[n]
    m_new = jnp.maximum(m_old, jnp.max(s, axis=0, keepdims=True))
    alpha = jnp.exp2(m_old - m_new)
    p = jnp.exp2(s - m_new)
    acc_ref[n] = alpha * acc_ref[n] + jnp.dot(v_t, p.astype(BF16), preferred_element_type=F32)
    m_ref[n] = m_new


def _flash_result(n, acc_ref):
    return acc_ref[n, 0:HEAD_DIM, :] / acc_ref[n, HEAD_DIM:HEAD_DIM + 1, :]


def _staggered(n_items, scores, update):
    pending = {n: scores(n) for n in range(min(MXU_LOOKAHEAD, n_items))}
    for n in range(n_items):
        if n + MXU_LOOKAHEAD < n_items:
            pending[n + MXU_LOOKAHEAD] = scores(n + MXU_LOOKAHEAD)
        update(n, pending.pop(n))


def _pipelined_tiles(n_tiles, n_chains, group, load_tile, scores, update, next_ref):
    ahead = next_ref.shape[0]
    n_items = group * n_chains
    assert ahead <= n_chains

    def body(trip, _):
        base = trip * group
        tiles, pending = {}, {}
        for n in range(n_items):
            cur = next_ref[n] if n < ahead else pending.pop(n)
            g, c = divmod(n + ahead, n_chains)
            if g not in tiles:
                tiles[g] = load_tile(base + g)
            new = scores(tiles[g], base + g, c)
            if n + ahead < n_items:
                pending[n + ahead] = new
            else:
                next_ref[n + ahead - n_items] = new
            update(base + n // n_chains, n % n_chains, cur)

    first = load_tile(0)
    for n in range(ahead):
        next_ref[n] = scores(first, 0, n)
    lax.fori_loop(0, (n_tiles + group - 1) // group, body, None)


def _flash_scratch(chains, ahead):
    return [pltpu.VMEM((chains, 1, QT), F32), pltpu.VMEM((chains, V_ROWS, QT), F32),
            pltpu.VMEM((ahead, KT, QT), F32)]


N_NEAR = -(-(FAR_DIST + KT - 1) // QT)
N_BIAS_TILES = N_NEAR + 2
DIFF_TILE_GROUP = 2
SLC_TILE_GROUP = 2


def _bias_tile_index(i, j):
    return jnp.where(j > i, N_NEAR + 1, jnp.minimum(i - j, N_NEAR))


def _diff_kernel(q_ref, k_ref, v_ref, bias_ref, lam_ref, subln_ref, o_ref, qz_ref, m_ref, acc_ref, next_ref, ot_ref, *,
                 lambda_init):
    i = pl.program_id(1)
    q = q_ref[0]
    row = lax.broadcasted_iota(jnp.int32, (GROUP, QT), 0) // DIFF_QK_DIM
    for n in range(2 * HEADS):
        qz_ref[n] = jnp.where(row == n, q, jnp.zeros_like(q))
    _flash_reset(m_ref, acc_ref)

    def load_tile(j):
        return k_ref[0, pl.ds(pl.multiple_of(jnp.minimum(j, i) * KT, KT), KT), :]

    def scores(k, j, n):
        return jnp.dot(k, qz_ref[n], preferred_element_type=F32) + bias_ref[n // 2, _bias_tile_index(i, j)]

    def update(j, n, s):
        h = n // 2
        _flash_update(n, s, v_ref[0, jnp.minimum(j, i), h * V_ROWS:(h + 1) * V_ROWS, :], m_ref, acc_ref)

    _pipelined_tiles(i + 1, 2 * HEADS, DIFF_TILE_GROUP, load_tile, scores, update, next_ref)

    lam_p = lam_ref[...]
    lam = (jnp.exp(jnp.sum(lam_p[0:1] * lam_p[1:2], axis=1, keepdims=True))
           - jnp.exp(jnp.sum(lam_p[2:3] * lam_p[3:4], axis=1, keepdims=True)) + lambda_init)
    for h in range(HEADS):
        o = _flash_result(2 * h, acc_ref) - lam * _flash_result(2 * h + 1, acc_ref)
        msq = jnp.mean(o * o, axis=0, keepdims=True)
        ot_ref[h * HEAD_DIM:(h + 1) * HEAD_DIM, :] = (o * lax.rsqrt(msq + RMS_EPS) * subln_ref[...]
                                                      * (1.0 - lambda_init))
    o_ref[0] = ot_ref[...].T.astype(o_ref.dtype)


def _diff_attention(q_t, k, v_t, bias, lam_p, subln, lambda_init):
    b, s, _ = k.shape
    nq = s // QT
    nkv = s // KT
    v4 = v_t.reshape(b, nkv, HEADS * V_ROWS, KT)
    return pl.pallas_call(
        functools.partial(_diff_kernel, lambda_init=lambda_init),
        grid=(b, nq),
        in_specs=[pl.BlockSpec((1, GROUP, QT), lambda bb, i: (bb * nq + i, 0, 0)),
                  pl.BlockSpec((1, s, GROUP), lambda bb, i: (bb, 0, 0)),
                  pl.BlockSpec((1, nkv, HEADS * V_ROWS, KT), lambda bb, i: (bb, 0, 0, 0)),
                  pl.BlockSpec((HEADS, N_BIAS_TILES, KT, QT), lambda bb, i: (0, 0, 0, 0)),
                  pl.BlockSpec((4, DIFF_QK_DIM), lambda bb, i: (0, 0)),
                  pl.BlockSpec((HEAD_DIM, 1), lambda bb, i: (0, 0))],
        out_specs=pl.BlockSpec((1, QT, GROUP), lambda bb, i: (bb, i, 0)),
        out_shape=jax.ShapeDtypeStruct((b, s, GROUP), BF16),
        scratch_shapes=[pltpu.VMEM((2 * HEADS, GROUP, QT), BF16)] + _flash_scratch(2 * HEADS, MXU_LOOKAHEAD)
        + [pltpu.VMEM((GROUP, QT), F32)],
        compiler_params=pltpu.CompilerParams(dimension_semantics=("arbitrary", "arbitrary"),
                                             vmem_limit_bytes=VMEM_LIMIT),
        name="diff_attention",
    )(q_t, k, v4, bias, lam_p, subln)


def _compress_kernel(ch_ref, ptop_ref, pbot_ref, w1t_ref, w1b_ref, b1_ref, w2k_ref, b2k_ref, w2v_ref, b2v_ref,
                     gk_ref, kc_ref, vct_ref):
    ch = ch_ref[0]
    n_c = ch.shape[0]
    u = jnp.dot((ch + ptop_ref[...]).astype(BF16), w1t_ref[...], preferred_element_type=F32)
    v = jnp.dot((ch + pbot_ref[...]).astype(BF16), w1b_ref[...], preferred_element_type=F32)
    v_next = pltpu.roll(v, n_c - 1, 0)
    hid = jax.nn.gelu(u + v_next + b1_ref[...])
    hk = hid[:, :CMP_HIDDEN].astype(BF16)
    hv = hid[:, CMP_HIDDEN:].astype(BF16)
    kc = jnp.dot(hk, w2k_ref[...], preferred_element_type=F32) + b2k_ref[...]
    msq = jnp.mean(kc * kc, axis=-1, keepdims=True)
    kc_ref[0] = (kc * lax.rsqrt(msq + RMS_EPS) * gk_ref[...]).astype(kc_ref.dtype)
    vct = lax.dot_general(w2v_ref[...], hv, NT_DIMS, preferred_element_type=F32) + b2v_ref[...]
    vct_ref[0] = vct.astype(vct_ref.dtype)


def _compress(chunks, ptop, pbot, w1t, w1b, b1, w2k, b2k, w2v, b2v, gk):
    b, n_c, width = chunks.shape
    const = lambda a: pl.BlockSpec(a.shape, lambda bb: (0,) * a.ndim)
    params = (ptop, pbot, w1t, w1b, b1, w2k, b2k, w2v, b2v, gk)
    return pl.pallas_call(
        _compress_kernel,
        grid=(b,),
        in_specs=[pl.BlockSpec((1, n_c, width), lambda bb: (bb, 0, 0))] + [const(a) for a in params],
        out_specs=[pl.BlockSpec((1, n_c, HEAD_DIM), lambda bb: (bb, 0, 0)),
                   pl.BlockSpec((1, HEAD_DIM, n_c), lambda bb: (bb, 0, 0))],
        out_shape=[jax.ShapeDtypeStruct((b, n_c, HEAD_DIM), BF16), jax.ShapeDtypeStruct((b, HEAD_DIM, n_c), BF16)],
        compiler_params=pltpu.CompilerParams(dimension_semantics=("arbitrary",), vmem_limit_bytes=VMEM_LIMIT),
        name="nsa_compress",
    )(chunks, *params)


def _cmp_attn_kernel(q_ref, kc_ref, vct_ref, bias_ref, o_ref, sel_ref, p_ref, *, n_sel):
    i = pl.program_id(0)
    kc = kc_ref[0]
    vct = vct_ref[0]
    n_c = kc.shape[0]
    n_blk = sel_ref.shape[1]
    probs = []

    def scores(h):
        return jnp.dot(kc, q_ref[0, h * HEAD_DIM:(h + 1) * HEAD_DIM, :], preferred_element_type=F32) + bias_ref[h, 0]

    def update(h, s):
        m = jnp.maximum(jnp.max(s, axis=0, keepdims=True), 0.5 * NEG_INF)
        p = jnp.exp2(s - m)
        den = jnp.sum(p, axis=0, keepdims=True)
        p = p * (1.0 / jnp.maximum(den, TINY))
        o_ref[0, h * HEAD_DIM:(h + 1) * HEAD_DIM, :] = jnp.dot(vct, p.astype(BF16),
                                                               preferred_element_type=F32).astype(o_ref.dtype)
        probs.append(p)

    _staggered(HEADS, scores, update)
    psum = (probs[0] + probs[1]) + (probs[2] + probs[3])
    per_blk = SLC_BLOCK // CMP_STRIDE
    halves = []
    for half in range(QT // LANES):
        p_ref[half, 0:8, :] = jnp.zeros((8, LANES), F32)
        p_ref[half, 8:8 + n_c, :] = psum[:, half * LANES:(half + 1) * LANES]
        p_ref[half, 8 + n_c:16 + n_c, :] = jnp.zeros((8, LANES), F32)
        acc = p_ref[half, pl.ds(7, n_blk, stride=per_blk), :]
        for t in range(per_blk):
            acc = acc + p_ref[half, pl.ds(8 + t, n_blk, stride=per_blk), :]
        halves.append(acc)
    imp = jnp.concatenate(halves, axis=1)
    blk = lax.broadcasted_iota(jnp.int32, (n_blk, QT), 0)
    cur = (i * QT + lax.broadcasted_iota(jnp.int32, (n_blk, QT), 1)) // SLC_BLOCK
    forced = (blk == 0) | (blk == cur) | (blk == cur - 1)
    val = jnp.where(forced, FORCE_SELECT, jnp.where(blk <= cur, imp, NEG_INF))
    sel = jnp.zeros((n_blk, QT), jnp.bool_)
    for _ in range(n_sel):
        top = jnp.max(val, axis=0, keepdims=True)
        idx = jnp.min(jnp.where(val == top, blk, n_blk), axis=0, keepdims=True)
        hit = blk == idx
        sel = sel | hit
        val = jnp.where(hit, -3.0e38, val)
    sel_ref[0] = jnp.where(sel, 1.0, 0.0).astype(sel_ref.dtype)


def _cmp_attention(q_t, kc, vct, bias, b):
    nt = q_t.shape[0]
    nq = nt // b
    n_c = kc.shape[1]
    n_blk = nq * QT // SLC_BLOCK
    return pl.pallas_call(
        functools.partial(_cmp_attn_kernel, n_sel=min(SLC_TOPK, n_blk)),
        grid=(nq, b),
        in_specs=[pl.BlockSpec((1, GROUP, QT), lambda i, bb: (bb * nq + i, 0, 0)),
                  pl.BlockSpec((1, n_c, HEAD_DIM), lambda i, bb: (bb, 0, 0)),
                  pl.BlockSpec((1, HEAD_DIM, n_c), lambda i, bb: (bb, 0, 0)),
                  pl.BlockSpec((HEADS, 1, n_c, QT), lambda i, bb: (0, 0, 0, i))],
        out_specs=[pl.BlockSpec((1, GROUP, QT), lambda i, bb: (bb * nq + i, 0, 0)),
                   pl.BlockSpec((1, n_blk, QT), lambda i, bb: (bb * nq + i, 0, 0))],
        out_shape=[jax.ShapeDtypeStruct((nt, GROUP, QT), BF16), jax.ShapeDtypeStruct((nt, n_blk, QT), BF16)],
        scratch_shapes=[pltpu.VMEM((QT // LANES, n_c + 16, LANES), F32)],
        compiler_params=pltpu.CompilerParams(dimension_semantics=("arbitrary", "arbitrary"),
                                             vmem_limit_bytes=VMEM_LIMIT),
        name="nsa_compressed_attention",
    )(q_t, kc, vct, bias)


SEL_REP = 8
N_WIN = -(-(NSA_WINDOW - 1 + KT - 1) // QT)


def _slc_win_kernel(q_ref, ksw_ref, vs_ref, vw_ref, sel_ref, rep_ref, ocmp_ref, g_ref, bslc_ref, bwin_ref,
                    o_ref, qz_ref, m_ref, acc_ref, next_ref, ot_ref, mask_ref):
    i = pl.program_id(1)
    sel8 = jnp.dot(rep_ref[...], sel_ref[0], preferred_element_type=F32)
    mask_ref[...] = (sel8 - 1.0) * (-NEG_INF)
    blocks_per_tile = KT // SLC_BLOCK
    mrows = blocks_per_tile * SEL_REP
    zeros = jnp.zeros((HEAD_DIM, QT), BF16)
    for h in range(HEADS):
        qh = q_ref[0, h * HEAD_DIM:(h + 1) * HEAD_DIM, :]
        qz_ref[h] = jnp.concatenate([qh, zeros], axis=0)
        qz_ref[HEADS + h] = jnp.concatenate([zeros, qh], axis=0)
    _flash_reset(m_ref, acc_ref)

    def load_keys(j):
        return ksw_ref[0, pl.ds(pl.multiple_of(j * KT, KT), KT), :]

    def load_tile(j):
        j = jnp.minimum(j, i)
        m8 = mask_ref[pl.ds(pl.multiple_of(j * mrows, mrows), mrows), :]
        mask = jnp.broadcast_to(m8.reshape(blocks_per_tile, 1, SEL_REP, QT),
                                (blocks_per_tile, SLC_BLOCK // SEL_REP, SEL_REP, QT)).reshape(KT, QT)
        return load_keys(j), mask

    def slc_scores(tile, j, h):
        k, mask = tile
        return jnp.dot(k, qz_ref[h], preferred_element_type=F32) + mask + bslc_ref[h, _bias_tile_index(i, j)]

    def slc_update(j, h, s):
        _flash_update(h, s, vs_ref[0, jnp.minimum(j, i)], m_ref, acc_ref)

    _pipelined_tiles(i + 1, HEADS, SLC_TILE_GROUP, load_tile, slc_scores, slc_update, next_ref)

    def win_tile(n):
        d = N_WIN - 1 - n // HEADS
        return d, n % HEADS, jnp.maximum(i - d, 0)

    def win_scores(n):
        d, h, j = win_tile(n)
        missing = jnp.where(i < d, NEG_INF, 0.0).astype(F32)
        return jnp.dot(load_keys(j), qz_ref[HEADS + h], preferred_element_type=F32) + (bwin_ref[h, d] + missing)

    def win_update(n, s):
        _, h, j = win_tile(n)
        _flash_update(HEADS + h, s, vw_ref[0, j], m_ref, acc_ref)

    _staggered(N_WIN * HEADS, win_scores, win_update)

    for h in range(HEADS):
        g = jax.nn.sigmoid(g_ref[0, 3 * h:3 * h + 3, :])
        ot_ref[h * HEAD_DIM:(h + 1) * HEAD_DIM, :] = (g[0:1] * ocmp_ref[0, h * HEAD_DIM:(h + 1) * HEAD_DIM, :]
                                                      + g[1:2] * _flash_result(h, acc_ref)
                                                      + g[2:3] * _flash_result(HEADS + h, acc_ref))
    o_ref[0] = ot_ref[...].T.astype(o_ref.dtype)


def _slc_win_attention(q_t, ksw, vs_t, vw_t, sel, rep, ocmp, g_t, bslc, bwin):
    b, s, _ = ksw.shape
    nq = s // QT
    nkv = s // KT
    n_blk = s // SLC_BLOCK
    tile = lambda height: pl.BlockSpec((1, height, QT), lambda bb, i: (bb * nq + i, 0, 0))
    whole = lambda a: pl.BlockSpec(a.shape, lambda bb, i: (0,) * a.ndim)
    return pl.pallas_call(
        _slc_win_kernel,
        grid=(b, nq),
        in_specs=[tile(GROUP),
                  pl.BlockSpec((1, s, 2 * HEAD_DIM), lambda bb, i: (bb, 0, 0)),
                  pl.BlockSpec((1, nkv, V_ROWS, KT), lambda bb, i: (bb, 0, 0, 0)),
                  pl.BlockSpec((1, nkv, V_ROWS, KT), lambda bb, i: (bb, 0, 0, 0)),
                  tile(n_blk), whole(rep), tile(GROUP), tile(16), whole(bslc), whole(bwin)],
        out_specs=pl.BlockSpec((1, QT, GROUP), lambda bb, i: (bb, i, 0)),
        out_shape=jax.ShapeDtypeStruct((b, s, GROUP), BF16),
        scratch_shapes=[pltpu.VMEM((2 * HEADS, 2 * HEAD_DIM, QT), BF16)] + _flash_scratch(2 * HEADS, MXU_LOOKAHEAD)
        + [pltpu.VMEM((GROUP, QT), F32), pltpu.VMEM((n_blk * SEL_REP, QT), F32)],
        compiler_params=pltpu.CompilerParams(dimension_semantics=("arbitrary", "arbitrary"),
                                             vmem_limit_bytes=VMEM_LIMIT),
        name="nsa_selected_window_attention",
    )(q_t, ksw, vs_t.reshape(b, nkv, V_ROWS, KT), vw_t.reshape(b, nkv, V_ROWS, KT), sel, rep, ocmp, g_t,
      bslc, bwin)


def _out_kernel(x_ref, a0_ref, a1_ref, a2_ref, l0_ref, l1_ref, l2_ref, ob_ref, oc_ref, od_ref, gate_ref, e_ref,
                w_ref, o_ref, unfold_ref):
    rows = x_ref.shape[0]

    def unfolded(ref, rate):
        width = ref.shape[1] // rate
        for rho in range(rate):
            for part in range(width // LANES):
                c0 = rho * width + part * LANES
                unfold_ref[part, pl.ds(rho, rows // rate, stride=rate), :] = ref[:, c0:c0 + LANES].astype(F32)
        return jnp.concatenate([unfold_ref[part] for part in range(width // LANES)], axis=1)

    a0, l0 = a0_ref[...], l0_ref[...]
    a1, l1 = unfolded(a1_ref, FOLD_RATES[0]), unfolded(l1_ref, FOLD_RATES[0])
    a2, l2 = unfolded(a2_ref, FOLD_RATES[1]), unfolded(l2_ref, FOLD_RATES[1])
    mx = jnp.maximum(jnp.maximum(l0, l1), l2)
    e0, e1, e2 = jnp.exp(l0 - mx), jnp.exp(l1 - mx), jnp.exp(l2 - mx)
    den = e0 + e1 + e2

    def per_head_lanes(w):
        hi = w.astype(BF16)
        lo = (w - hi.astype(F32)).astype(BF16)
        return (jnp.dot(hi, e_ref[...], preferred_element_type=F32)
                + jnp.dot(lo, e_ref[...], preferred_element_type=F32))

    o_a = per_head_lanes(e0 / den) * a0 + per_head_lanes(e1 / den) * a1 + per_head_lanes(e2 / den) * a2
    y = jnp.concatenate([o_a, ob_ref[...].astype(F32), oc_ref[...].astype(F32), od_ref[...].astype(F32)], axis=1)
    g = gate_ref[...].astype(F32)
    y = y * (g * jax.nn.sigmoid(g))
    o_ref[...] = x_ref[...] + jnp.dot(y.astype(BF16), w_ref[...], preferred_element_type=F32)


def _out_projection(x2, a_outs, a_lses, o_b, o_c, o_d, gate, w_out):
    m, d = x2.shape
    rowblk = lambda width: pl.BlockSpec((PROJ_ROWS, width), lambda i: (i, 0))
    folded = lambda width, rate: pl.BlockSpec((PROJ_ROWS // rate, rate * width), lambda i: (i, 0))
    head_of_lane = np.arange(GROUP) // HEAD_DIM
    expand = jnp.asarray((np.arange(LANES)[:, None] == head_of_lane[None, :]).astype(np.float32), BF16)
    return pl.pallas_call(
        _out_kernel,
        grid=(m // PROJ_ROWS,),
        in_specs=[rowblk(d)] + [folded(GROUP, rate) for rate in (1,) + FOLD_RATES]
        + [folded(LANES, rate) for rate in (1,) + FOLD_RATES] + [rowblk(GROUP)] * 3
        + [rowblk(N_MIXERS * GROUP), pl.BlockSpec((LANES, GROUP), lambda i: (0, 0)),
           pl.BlockSpec((N_MIXERS * GROUP, d), lambda i: (0, 0))],
        out_specs=rowblk(d),
        out_shape=jax.ShapeDtypeStruct((m, d), F32),
        scratch_shapes=[pltpu.VMEM((GROUP // LANES, PROJ_ROWS, LANES), F32)],
        compiler_params=pltpu.CompilerParams(dimension_semantics=("arbitrary",), vmem_limit_bytes=VMEM_LIMIT),
        name="out_projection",
    )(x2, *a_outs, *a_lses, o_b, o_c, o_d, gate, expand, w_out)


def _block_diag_mean(group):
    idx = np.arange(GROUP) // group
    return jnp.asarray((idx[:, None] == idx[None, :]).astype(np.float32) / group, BF16)


def _layer_weights(w_in, qk_gain, qk_gain_diff):
    d = w_in.shape[0]
    sizes = (GROUP,) * 3 + (GROUP, GROUP // 2, GROUP // 2) + (GROUP,) * 3 + (GROUP,) + (HEAD_DIM,) * 6 \
        + (HEADS * 3, N_MIXERS * GROUP)
    offs = np.concatenate([[0], np.cumsum(sizes)])
    col = lambda n: w_in[:, offs[n]:offs[n + 1]]
    (a_q, a_k, a_v, b_q, b_k, b_v, c_q, c_k, c_v, d_q, d_kc, d_vc, d_ks, d_vs, d_kw, d_vw, d_g, gate) = \
        [col(n) for n in range(18)]
    rep_kv = lambda w: jnp.repeat(w.reshape(d, 2, HEAD_DIM), 2, axis=1).reshape(d, GROUP)
    wrm = jnp.concatenate([a_q, a_k, a_v, b_q, rep_kv(b_k), rep_kv(b_v), c_k, d_kc, d_vc, d_ks, d_kw, gate], axis=1)
    wt = jnp.concatenate([c_q, c_v, d_q, d_vs, d_vw, d_g, jnp.zeros((d, 16 - HEADS * 3), w_in.dtype)], axis=1).T
    g = qk_gain
    ones = lambda n: jnp.ones((n,), F32)
    tile4 = lambda v: jnp.tile(v, HEADS)
    scale = HEAD_DIM ** -0.5
    grm = jnp.concatenate([tile4(g[0]) * scale, tile4(g[1]), ones(GROUP), tile4(g[2]) * scale, tile4(g[3]),
                           ones(GROUP), jnp.tile(qk_gain_diff[1], 2 * HEADS), ones(2 * HEAD_DIM), g[6], g[7],
                           ones(N_MIXERS * GROUP)])
    gt = jnp.concatenate([jnp.tile(qk_gain_diff[0], 2 * HEADS) * (DIFF_QK_DIM ** -0.5 * LOG2E), ones(GROUP),
                          tile4(g[4]) * (scale * LOG2E), ones(2 * HEAD_DIM + 16)])
    return wrm.astype(BF16), wt.astype(BF16), grm.reshape(1, -1), gt.reshape(-1, 1)


def _compress_weights(cmp_pos, cmp_w1, cmp_b1, cmp_w2, cmp_b2):
    half = CMP_LEN // 2
    pos = jnp.concatenate([cmp_pos[0], cmp_pos[1]], axis=-1)
    ptop = pos[:half].reshape(1, -1)
    pbot = pos[half:].reshape(1, -1)
    w1 = cmp_w1.reshape(2, CMP_LEN, HEAD_DIM, CMP_HIDDEN)
    zeros = jnp.zeros_like(w1[0])
    w1cat = jnp.concatenate([jnp.concatenate([w1[0], zeros], axis=-1),
                             jnp.concatenate([zeros, w1[1]], axis=-1)], axis=1)
    w1t = w1cat[:half].reshape(half * 2 * HEAD_DIM, 2 * CMP_HIDDEN).astype(BF16)
    w1b = w1cat[half:].reshape(half * 2 * HEAD_DIM, 2 * CMP_HIDDEN).astype(BF16)
    b1 = jnp.concatenate([cmp_b1[0], cmp_b1[1]]).reshape(1, -1)
    return (ptop, pbot, w1t, w1b, b1, cmp_w2[0].astype(BF16), cmp_b2[0].reshape(1, -1),
            cmp_w2[1].T.astype(BF16), cmp_b2[1].reshape(-1, 1))


def kernel(x, rel_bias_table, norm_w, w_in, w_out, qk_gain, qk_gain_diff, attn_sinks, diff_lambda, diff_subln,
           cmp_pos, cmp_w1, cmp_b1, cmp_w2, cmp_b2):
    b, s, d = x.shape
    depth = w_in.shape[0]
    n_c = s // CMP_STRIDE
    n_blk = s // SLC_BLOCK
    assert s % (BAND_TILE * DILATED_CONFIGS[-1][1]) == 0 and s % PROJ_ROWS == 0 and d == N_MIXERS * GROUP

    table = rel_bias_table.astype(F32)
    band_bias = [_build_bias(table, head0=0, n_d=1, rows=BAND_TILE, cols=2 * BAND_TILE, base0=BAND_TILE, dstep=0,
                             rs=1, cs=-1, dscale=rate, max_dist=window // rate) for window, rate in DILATED_CONFIGS]
    swa_bias = _build_bias(table, head0=HEADS, n_d=1, rows=BAND_TILE, cols=2 * BAND_TILE, base0=BAND_TILE, dstep=0,
                           rs=1, cs=-1, max_dist=SWA_WINDOW - 1)
    flash_tiles = dict(rows=KT, cols=QT, base0=0, dstep=QT, rs=-1, cs=1, scale=LOG2E)
    diff_bias = _build_bias(table, head0=2 * HEADS, n_d=N_BIAS_TILES, d_valid=N_NEAR + 1, **flash_tiles)
    slc_bias = _build_bias(table, head0=3 * HEADS, n_d=N_BIAS_TILES, d_valid=N_NEAR + 1, **flash_tiles)
    win_bias = _build_bias(table, head0=3 * HEADS, n_d=N_WIN, max_dist=NSA_WINDOW - 1, **flash_tiles)
    cmp_bias = _build_bias(table, head0=3 * HEADS, n_d=1, rows=n_c, cols=s, base0=-(CMP_LEN - 1), dstep=0,
                           rs=-CMP_STRIDE, cs=1, r_valid=n_c - 1, col_tile=4 * QT, scale=LOG2E)
    e64, e32 = _block_diag_mean(HEAD_DIM), _block_diag_mean(DIFF_QK_DIM)
    rep_idx = np.arange(n_blk * SEL_REP) // SEL_REP
    rep = jnp.asarray((rep_idx[:, None] == np.arange(n_blk)[None, :]).astype(np.float32), BF16)
    no_sink = jnp.zeros((HEADS,), F32)

    x2 = x.reshape(b * s, d)
    for layer in range(depth):
        wrm, wt, grm, gt = _layer_weights(w_in[layer], qk_gain[layer], qk_gain_diff[layer])
        (a_q, a_k, a_v, b_q, b_k, b_v, c_k, kvc, ksw, gate, a_q4, a_k4, a_v4, a_q16, a_k16, a_v16,
         c_qt, c_vt, d_qt, d_vst, d_vwt, d_gt) = _project(x2, norm_w[layer].reshape(1, d), wrm, wt, grm, gt, e64, e32)
        seq = lambda t: t.reshape(b, s, t.shape[-1])
        per_batch = lambda t: t.reshape(b, t.shape[0] // b, t.shape[1])
        flat = lambda t: t.reshape(b * t.shape[1], t.shape[2])
        a_in = ((a_q, a_k, a_v), (a_q4, a_k4, a_v4), (a_q16, a_k16, a_v16))
        a_res = [_banded(*map(per_batch, a_in[n]), band_bias[n], no_sink, rate, False)
                 for n, (_, rate) in enumerate(DILATED_CONFIGS)]
        o_b, _ = _banded(seq(b_q), seq(b_k), seq(b_v), swa_bias, attn_sinks[layer].astype(F32), 1, True)
        lambda_init = 0.8 - 0.6 * math.exp(-0.3 * layer)
        o_c = _diff_attention(c_qt, seq(c_k), c_vt, diff_bias, diff_lambda[layer].astype(F32),
                              diff_subln[layer].reshape(HEAD_DIM, 1).astype(F32), lambda_init)
        cw = _compress_weights(cmp_pos[layer], cmp_w1[layer], cmp_b1[layer], cmp_w2[layer], cmp_b2[layer])
        kc, vct = _compress(kvc.reshape(b, n_c, CMP_STRIDE * 2 * HEAD_DIM), *cw, qk_gain[layer, 5].reshape(1, -1))
        o_cmp, sel = _cmp_attention(d_qt, kc, vct, cmp_bias, b)
        o_d = _slc_win_attention(d_qt, seq(ksw), d_vst, d_vwt, sel, rep, o_cmp, d_gt, slc_bias, win_bias)
        x2 = _out_projection(x2, [flat(r[0]) for r in a_res], [flat(r[1]) for r in a_res], o_b.reshape(b * s, GROUP),
                             o_c.reshape(b * s, GROUP), o_d.reshape(b * s, GROUP), gate, w_out[layer].astype(BF16))
    return x2.reshape(b, s, d)
```

```python
import functools
import math

import numpy as np
import jax
import jax.numpy as jnp
from jax import lax
from jax.experimental import pallas as pl
from jax.experimental.pallas import tpu as pltpu

F32 = jnp.float32
BF16 = jnp.bfloat16

HEAD_DIM = 64
HEADS = 4
GROUP = HEADS * HEAD_DIM
N_MIXERS = 4
NUM_BUCKETS = 32
REL_MAX_DIST = 2048
DILATED_CONFIGS = ((128, 1), (512, 4), (2048, 16))
FOLD_RATES = tuple(rate for _, rate in DILATED_CONFIGS if rate > 1)
SWA_WINDOW = 128
DIFF_QK_DIM = HEAD_DIM // 2
CMP_LEN = 32
CMP_STRIDE = 16
CMP_HIDDEN = 256
SLC_BLOCK = 64
SLC_TOPK = 16
NSA_WINDOW = 512
RMS_EPS = 1e-6
NEG_INF = -1e30
FORCE_SELECT = 1e9
TINY = 1e-30
LOG2E = math.log2(math.e)

PROJ_ROWS = 512
BAND_TILE = 128
BAND_STEP = 1024
BAND_LOOKAHEAD = 8
BIAS_ROW_CHUNK = 64
LANES = 128
QT = 256
KT = 256
VMEM_LIMIT = 56 * 1024 * 1024
MXU_LOOKAHEAD = 4
V_ROWS = HEAD_DIM + 16

NT_DIMS = (((1,), (1,)), ((), ()))


def _t5_thresholds():
    n = np.arange(0, 4 * REL_MAX_DIST)
    max_exact = NUM_BUCKETS // 2
    nf = np.maximum(n, 1).astype(np.float32)
    large = max_exact + (np.log(nf / np.float32(max_exact)) / np.float32(math.log(REL_MAX_DIST / max_exact))
                         * np.float32(NUM_BUCKETS - max_exact)).astype(np.int32)
    bucket = np.where(n < max_exact, n, np.minimum(large, NUM_BUCKETS - 1))
    return [int(np.argmax(bucket >= b)) for b in range(NUM_BUCKETS)]


T5_THRESHOLDS = _t5_thresholds()
FAR_DIST = T5_THRESHOLDS[-1]


def _bias_kernel(tbl_ref, out_ref, *, head0, base0, dstep, rs, cs, dscale, max_dist, r_valid, d_valid, col_tile,
                 scale):
    h = pl.program_id(0)
    d = pl.program_id(1)
    ct = pl.program_id(2)
    rows, cols = out_ref.shape[-2:]
    chunk = BIAS_ROW_CHUNK if rows % BIAS_ROW_CHUNK == 0 else rows
    for r0 in range(0, rows, chunk):
        origin = base0 + d * dstep + r0 * rs + ct * col_tile * cs
        corners = [origin + dr * rs + dc * cs for dr in (0, chunk - 1) for dc in (0, cols - 1)]
        lo = functools.reduce(jnp.minimum, corners)
        hi = functools.reduce(jnp.maximum, corners)
        all_masked = (hi < 0) | (lo > max_dist) | (d >= d_valid) | (r0 >= r_valid)
        all_far = (lo * dscale >= FAR_DIST) & (hi <= max_dist) & (d < d_valid) & (r0 + chunk <= r_valid)
        out = out_ref.at[0, 0, r0:r0 + chunk, :]

        @pl.when(all_masked)
        def _():
            out[...] = jnp.full((chunk, cols), NEG_INF, F32)

        @pl.when(all_far)
        def _():
            out[...] = jnp.full((chunk, cols), tbl_ref[NUM_BUCKETS - 1, head0 + h] * scale, F32)

        @pl.when(jnp.logical_not(all_masked | all_far))
        def _():
            r = lax.broadcasted_iota(jnp.int32, (chunk, cols), 0)
            c = lax.broadcasted_iota(jnp.int32, (chunk, cols), 1)
            dist = origin + r * rs + c * cs
            n = dist * dscale
            val = jnp.full((chunk, cols), tbl_ref[0, head0 + h], F32)
            for b in range(1, NUM_BUCKETS):
                val = jnp.where(n >= T5_THRESHOLDS[b], tbl_ref[b, head0 + h], val)
            valid = (dist >= 0) & (dist <= max_dist) & (r + r0 < r_valid) & (d < d_valid)
            out[...] = jnp.where(valid, val * scale, NEG_INF)


def _build_bias(table, *, head0, n_d, rows, cols, base0, dstep, rs, cs, dscale=1, max_dist=1 << 30,
                r_valid=1 << 30, d_valid=1 << 30, col_tile=None, scale=1.0):
    col_tile = cols if col_tile is None else col_tile
    kern = functools.partial(_bias_kernel, head0=head0, base0=base0, dstep=dstep, rs=rs, cs=cs, dscale=dscale,
                             max_dist=max_dist, r_valid=r_valid, d_valid=d_valid, col_tile=col_tile, scale=scale)
    return pl.pallas_call(
        kern,
        grid=(HEADS, n_d, cols // col_tile),
        in_specs=[pl.BlockSpec(memory_space=pltpu.SMEM)],
        out_specs=pl.BlockSpec((1, 1, rows, col_tile), lambda h, d, c: (h, d, 0, c)),
        out_shape=jax.ShapeDtypeStruct((HEADS, n_d, rows, cols), F32),
        name="rel_bias_tiles",
    )(table)


RM_AQ, RM_AK, RM_AV = 0, 256, 512
RM_BQ, RM_BK, RM_BV = 768, 1024, 1280
RM_CK = 1536
RM_KVC = 1792
RM_KSW = 1920
RM_GATE = 2048
RM_COLS = 3072
TR_CQ, TR_CV, TR_DQ, TR_DVS, TR_DVW, TR_DG = 0, 256, 512, 768, 832, 896
TR_ROWS = 912


def _proj_kernel(x_ref, nw_ref, wrm_ref, wt_ref, grm_ref, gt_ref, e64_ref, e32_ref,
                 aq_ref, ak_ref, av_ref, bq_ref, bk_ref, bv_ref, ck_ref, kvc_ref, ksw_ref, gate_ref,
                 aq4_ref, ak4_ref, av4_ref, aq16_ref, ak16_ref, av16_ref,
                 cq_ref, cv_ref, dq_ref, dvs_ref, dvw_ref, dg_ref, fold_ref):
    x = x_ref[...]
    ms = jnp.mean(x * x, axis=-1, keepdims=True)
    xn = (x * lax.rsqrt(ms + RMS_EPS) * nw_ref[...]).astype(BF16)
    rows = x.shape[0]

    def rm(c0, width):
        return jnp.dot(xn, wrm_ref[:, c0:c0 + width], preferred_element_type=F32)

    def rm_normed(c0, width, e_ref):
        h = rm(c0, width)
        msq = jnp.dot((h * h).astype(BF16), e_ref[0:width, 0:width], preferred_element_type=F32)
        return h * lax.rsqrt(msq + RMS_EPS) * grm_ref[:, c0:c0 + width]

    def put_folded(val, ref, folded_refs):
        ref[...] = val.astype(ref.dtype)
        for half in range(GROUP // LANES):
            fold_ref[half] = val[:, half * LANES:(half + 1) * LANES]
        for rate, fref in zip(FOLD_RATES, folded_refs):
            for rho in range(rate):
                for half in range(GROUP // LANES):
                    c0 = rho * GROUP + half * LANES
                    fref[:, c0:c0 + LANES] = fold_ref[half, pl.ds(rho, rows // rate, stride=rate), :].astype(fref.dtype)

    put_folded(rm_normed(RM_AQ, GROUP, e64_ref), aq_ref, (aq4_ref, aq16_ref))
    put_folded(rm_normed(RM_AK, GROUP, e64_ref), ak_ref, (ak4_ref, ak16_ref))
    put_folded(rm(RM_AV, GROUP), av_ref, (av4_ref, av16_ref))
    bq_ref[...] = rm_normed(RM_BQ, GROUP, e64_ref).astype(bq_ref.dtype)
    bk_ref[...] = rm_normed(RM_BK, GROUP, e64_ref).astype(bk_ref.dtype)
    bv_ref[...] = rm(RM_BV, GROUP).astype(bv_ref.dtype)
    ck_ref[...] = rm_normed(RM_CK, GROUP, e32_ref).astype(ck_ref.dtype)
    kvc_ref[...] = rm(RM_KVC, 2 * HEAD_DIM).astype(kvc_ref.dtype)
    ksw_ref[...] = rm_normed(RM_KSW, 2 * HEAD_DIM, e64_ref).astype(ksw_ref.dtype)
    gate_ref[...] = rm(RM_GATE, N_MIXERS * GROUP).astype(gate_ref.dtype)

    def tr(r0, height):
        return lax.dot_general(wt_ref[r0:r0 + height, :], xn, NT_DIMS, preferred_element_type=F32)

    def tr_normed(r0, height, group):
        h3 = tr(r0, height).reshape(height // group, group, rows)
        msq = jnp.mean(h3 * h3, axis=1, keepdims=True)
        return (h3 * lax.rsqrt(msq + RMS_EPS)).reshape(height, rows) * gt_ref[r0:r0 + height, :]

    def put(ref, val):
        for t in range(rows // QT):
            ref[t] = val[:, t * QT:(t + 1) * QT].astype(ref.dtype)

    def with_ones(v):
        ones = jnp.ones((V_ROWS - HEAD_DIM, rows), F32)
        parts = []
        for h in range(v.shape[0] // HEAD_DIM):
            parts += [v[h * HEAD_DIM:(h + 1) * HEAD_DIM], ones]
        return jnp.concatenate(parts, axis=0)

    put(cq_ref, tr_normed(TR_CQ, GROUP, DIFF_QK_DIM))
    put(cv_ref, with_ones(tr(TR_CV, GROUP)))
    put(dq_ref, tr_normed(TR_DQ, GROUP, HEAD_DIM))
    put(dvs_ref, with_ones(tr(TR_DVS, HEAD_DIM)))
    put(dvw_ref, with_ones(tr(TR_DVW, HEAD_DIM)))
    put(dg_ref, tr(TR_DG, 16))


def _project(x2, nw, wrm, wt, grm, gt, e64, e32):
    m, d = x2.shape
    nt = m // QT
    tpr = PROJ_ROWS // QT
    const = lambda shape: pl.BlockSpec(shape, lambda i: (0,) * len(shape))
    rm_out = lambda width, dtype: (jax.ShapeDtypeStruct((m, width), dtype),
                                   pl.BlockSpec((PROJ_ROWS, width), lambda i: (i, 0)))
    tr_out = lambda height, dtype: (jax.ShapeDtypeStruct((nt, height, QT), dtype),
                                    pl.BlockSpec((tpr, height, QT), lambda i: (i, 0, 0)))
    outs = [rm_out(GROUP, BF16)] * 7 + [rm_out(2 * HEAD_DIM, F32), rm_out(2 * HEAD_DIM, BF16),
                                        rm_out(N_MIXERS * GROUP, BF16)]
    fold_out = lambda rate: (jax.ShapeDtypeStruct((m // rate, rate * GROUP), BF16),
                             pl.BlockSpec((PROJ_ROWS // rate, rate * GROUP), lambda i: (i, 0)))
    outs += [fold_out(rate) for rate in FOLD_RATES for _ in range(3)]
    outs += [tr_out(GROUP, BF16), tr_out(HEADS * V_ROWS, BF16), tr_out(GROUP, BF16), tr_out(V_ROWS, BF16),
             tr_out(V_ROWS, BF16), tr_out(16, F32)]
    return pl.pallas_call(
        _proj_kernel,
        grid=(m // PROJ_ROWS,),
        in_specs=[pl.BlockSpec((PROJ_ROWS, d), lambda i: (i, 0)), const((1, d)), const((d, RM_COLS)),
                  const((TR_ROWS, d)), const((1, RM_COLS)), const((TR_ROWS, 1)), const((GROUP, GROUP)),
                  const((GROUP, GROUP))],
        out_specs=[o[1] for o in outs],
        out_shape=[o[0] for o in outs],
        scratch_shapes=[pltpu.VMEM((GROUP // LANES, PROJ_ROWS, LANES), F32)],
        compiler_params=pltpu.CompilerParams(dimension_semantics=("arbitrary",), vmem_limit_bytes=VMEM_LIMIT),
        name="in_projection",
    )(x2, nw, wrm, wt, grm, gt, e64, e32)


def _band_kernel(sink_ref, q_ref, kp_ref, kc_ref, vp_ref, vc_ref, bias_ref, o_ref, lse_ref, p_ref, *, use_sink):
    i = pl.program_id(2)
    n_blocks = q_ref.shape[1] // BAND_TILE
    head_q = lax.broadcasted_iota(jnp.int32, (BAND_TILE, GROUP), 1) // HEAD_DIM
    head_v = lax.broadcasted_iota(jnp.int32, (2 * BAND_TILE, GROUP), 1) // HEAD_DIM
    lane = lax.broadcasted_iota(jnp.int32, (BAND_TILE, LANES), 1)
    in_prev = lax.broadcasted_iota(jnp.int32, (1, 2 * BAND_TILE), 1) < BAND_TILE
    no_prev = jnp.where(in_prev & (i == 0), NEG_INF, 0.0).astype(F32)

    def window(cur_ref, prev_ref, m):
        if m == 0:
            return jnp.concatenate([prev_ref[0], cur_ref[0, 0:BAND_TILE, :]], axis=0)
        return cur_ref[0, (m - 1) * BAND_TILE:(m + 1) * BAND_TILE, :]

    def scores(n):
        m, h = divmod(n, HEADS)
        q = q_ref[0, m * BAND_TILE:(m + 1) * BAND_TILE, :]
        qh = jnp.where(head_q == h, q, jnp.zeros_like(q))
        bias = bias_ref[h, 0] + no_prev if m == 0 else bias_ref[h, 0]
        return lax.dot_general(qh, window(kc_ref, kp_ref, m), NT_DIMS, preferred_element_type=F32) + bias

    lse_tiles = {}

    def update(n, s):
        m, h = divmod(n, HEADS)
        mx = jnp.max(s, axis=1, keepdims=True)
        if use_sink:
            mx = jnp.maximum(mx, sink_ref[h])
        p = jnp.exp(s - mx)
        den = jnp.sum(p, axis=1, keepdims=True)
        if use_sink:
            den = den + jnp.exp(sink_ref[h] - mx)
        p_ref[m % 2, :, h * 2 * BAND_TILE:(h + 1) * 2 * BAND_TILE] = (p * (1.0 / den)).astype(BF16)
        lse_tiles[m] = jnp.where(lane == h, mx + jnp.log(den), lse_tiles.get(m, jnp.zeros((BAND_TILE, LANES), F32)))
        if h == HEADS - 1:
            v = window(vc_ref, vp_ref, m)
            v_heads = jnp.concatenate([jnp.where(head_v == hh, v, jnp.zeros_like(v)) for hh in range(HEADS)], axis=0)
            rows = slice(m * BAND_TILE, (m + 1) * BAND_TILE)
            o_ref[0, rows, :] = jnp.dot(p_ref[m % 2], v_heads, preferred_element_type=F32).astype(o_ref.dtype)
            lse_ref[0, rows, :] = lse_tiles.pop(m)

    _staggered(n_blocks * HEADS, scores, update, ahead=BAND_LOOKAHEAD)


def _banded(q, k, v, bias, sink, rate, use_sink):
    b, ln, _ = q.shape
    step = min(BAND_STEP, ln)
    per_step = step // BAND_TILE
    cur = pl.BlockSpec((1, step, GROUP), lambda bb, r, i: (bb, i, r))
    prev = pl.BlockSpec((1, BAND_TILE, GROUP), lambda bb, r, i: (bb, jnp.maximum(i * per_step - 1, 0), r))
    o, lse = pl.pallas_call(
        functools.partial(_band_kernel, use_sink=use_sink),
        grid=(b, rate, ln // step),
        in_specs=[pl.BlockSpec(memory_space=pltpu.SMEM), cur, prev, cur, prev, cur,
                  pl.BlockSpec((HEADS, 1, BAND_TILE, 2 * BAND_TILE), lambda bb, r, i: (0, 0, 0, 0))],
        out_specs=[cur, pl.BlockSpec((1, step, LANES), lambda bb, r, i: (bb, i, r))],
        out_shape=[jax.ShapeDtypeStruct((b, ln, rate * GROUP), BF16), jax.ShapeDtypeStruct((b, ln, rate * LANES), F32)],
        scratch_shapes=[pltpu.VMEM((2, BAND_TILE, HEADS * 2 * BAND_TILE), BF16)],
        compiler_params=pltpu.CompilerParams(dimension_semantics=("arbitrary",) * 3),
        name=f"banded_attention_r{rate}",
    )(sink, q, k, k, v, v, bias)
    return o, lse


def _flash_reset(m_ref, acc_ref):
    m_ref[...] = jnp.full(m_ref.shape, NEG_INF, F32)
    acc_ref[...] = jnp.zeros(acc_ref.shape, F32)


def _flash_update(n, s, v_t, m_ref, acc_ref, shift=None):
    m_old = m_ref[n]
    if shift is None:
        m_new = jnp.maximum(m_old, jnp.max(s, axis=0, keepdims=True))
        p = jnp.exp2(s - m_new)
    else:
        m_new = jnp.maximum(m_old, jnp.max(s, axis=0, keepdims=True) + shift)
        p = jnp.exp2(s - (m_new - shift))
    alpha = jnp.exp2(m_old - m_new)
    acc_ref[n] = alpha * acc_ref[n] + jnp.dot(v_t, p.astype(BF16), preferred_element_type=F32)
    m_ref[n] = m_new


def _flash_result(n, acc_ref):
    return acc_ref[n, 0:HEAD_DIM, :] / acc_ref[n, HEAD_DIM:HEAD_DIM + 1, :]


def _staggered(n_items, scores, update, ahead=MXU_LOOKAHEAD):
    pending = {n: scores(n) for n in range(min(ahead, n_items))}
    for n in range(n_items):
        if n + ahead < n_items:
            pending[n + ahead] = scores(n + ahead)
        update(n, pending.pop(n))


def _pipelined_tiles(first, n_tiles, n_chains, group, load_tile, scores, update, next_ref):
    ahead = next_ref.shape[0]
    n_items = group * n_chains
    assert ahead <= n_chains

    def body(trip, _):
        base = first + trip * group
        tiles, pending = {}, {}
        for n in range(n_items):
            cur = next_ref[n] if n < ahead else pending.pop(n)
            g, c = divmod(n + ahead, n_chains)
            if g not in tiles:
                tiles[g] = load_tile(base + g)
            new = scores(tiles[g], base + g, c)
            if n + ahead < n_items:
                pending[n + ahead] = new
            else:
                next_ref[n + ahead - n_items] = new
            update(base + n // n_chains, n % n_chains, cur)

    first_tile = load_tile(first)
    for n in range(ahead):
        next_ref[n] = scores(first_tile, first, n)
    lax.fori_loop(0, (n_tiles + group - 1) // group, body, None)


def _flash_scratch(chains, ahead):
    return [pltpu.VMEM((chains, 1, QT), F32), pltpu.VMEM((chains, V_ROWS, QT), F32),
            pltpu.VMEM((ahead, KT, QT), F32)]


N_NEAR = -(-(FAR_DIST + KT - 1) // QT)
N_BIAS_TILES = N_NEAR + 2
DIFF_TILE_GROUP = 2
SLC_TILE_GROUP = 2


def _bias_tile_index(i, j):
    return jnp.where(j > i, N_NEAR + 1, jnp.minimum(i - j, N_NEAR))


def _whole_far_groups(i, group):
    return jnp.maximum(i - (N_NEAR - 1), 0) // group * group


def _diff_kernel(far_ref, q_ref, k_ref, v_ref, bias_ref, lam_ref, subln_ref, o_ref, qz_ref, m_ref, acc_ref, next_ref,
                 ot_ref, *, lambda_init):
    i = pl.program_id(1)
    q = q_ref[0]
    row = lax.broadcasted_iota(jnp.int32, (GROUP, QT), 0) // DIFF_QK_DIM
    for n in range(2 * HEADS):
        qz_ref[n] = jnp.where(row == n, q, jnp.zeros_like(q))
    _flash_reset(m_ref, acc_ref)

    def load_tile(j):
        return k_ref[0, pl.ds(pl.multiple_of(jnp.minimum(j, i) * KT, KT), KT), :]

    def values(j, n):
        h = n // 2
        return v_ref[0, jnp.minimum(j, i), h * V_ROWS:(h + 1) * V_ROWS, :]

    n_far = _whole_far_groups(i, DIFF_TILE_GROUP)
    _pipelined_tiles(0, n_far, 2 * HEADS, DIFF_TILE_GROUP, load_tile,
                     lambda k, j, n: jnp.dot(k, qz_ref[n], preferred_element_type=F32),
                     lambda j, n, s: _flash_update(n, s, values(j, n), m_ref, acc_ref, shift=far_ref[n // 2]),
                     next_ref)

    def scores(k, j, n):
        return jnp.dot(k, qz_ref[n], preferred_element_type=F32) + bias_ref[n // 2, _bias_tile_index(i, j)]

    _pipelined_tiles(n_far, i + 1 - n_far, 2 * HEADS, DIFF_TILE_GROUP, load_tile, scores,
                     lambda j, n, s: _flash_update(n, s, values(j, n), m_ref, acc_ref), next_ref)

    lam_p = lam_ref[...]
    lam = (jnp.exp(jnp.sum(lam_p[0:1] * lam_p[1:2], axis=1, keepdims=True))
           - jnp.exp(jnp.sum(lam_p[2:3] * lam_p[3:4], axis=1, keepdims=True)) + lambda_init)
    for h in range(HEADS):
        o = _flash_result(2 * h, acc_ref) - lam * _flash_result(2 * h + 1, acc_ref)
        msq = jnp.mean(o * o, axis=0, keepdims=True)
        ot_ref[h * HEAD_DIM:(h + 1) * HEAD_DIM, :] = (o * lax.rsqrt(msq + RMS_EPS) * subln_ref[...]
                                                      * (1.0 - lambda_init))
    o_ref[0] = ot_ref[...].T.astype(o_ref.dtype)


def _diff_attention(far, q_t, k, v_t, bias, lam_p, subln, lambda_init):
    b, s, _ = k.shape
    nq = s // QT
    nkv = s // KT
    v4 = v_t.reshape(b, nkv, HEADS * V_ROWS, KT)
    return pl.pallas_call(
        functools.partial(_diff_kernel, lambda_init=lambda_init),
        grid=(b, nq),
        in_specs=[pl.BlockSpec(memory_space=pltpu.SMEM),
                  pl.BlockSpec((1, GROUP, QT), lambda bb, i: (bb * nq + i, 0, 0)),
                  pl.BlockSpec((1, s, GROUP), lambda bb, i: (bb, 0, 0)),
                  pl.BlockSpec((1, nkv, HEADS * V_ROWS, KT), lambda bb, i: (bb, 0, 0, 0)),
                  pl.BlockSpec((HEADS, N_BIAS_TILES, KT, QT), lambda bb, i: (0, 0, 0, 0)),
                  pl.BlockSpec((4, DIFF_QK_DIM), lambda bb, i: (0, 0)),
                  pl.BlockSpec((HEAD_DIM, 1), lambda bb, i: (0, 0))],
        out_specs=pl.BlockSpec((1, QT, GROUP), lambda bb, i: (bb, i, 0)),
        out_shape=jax.ShapeDtypeStruct((b, s, GROUP), BF16),
        scratch_shapes=[pltpu.VMEM((2 * HEADS, GROUP, QT), BF16)] + _flash_scratch(2 * HEADS, MXU_LOOKAHEAD)
        + [pltpu.VMEM((GROUP, QT), F32)],
        compiler_params=pltpu.CompilerParams(dimension_semantics=("arbitrary", "arbitrary"),
                                             vmem_limit_bytes=VMEM_LIMIT),
        name="diff_attention",
    )(far, q_t, k, v4, bias, lam_p, subln)


def _compress_kernel(ch_ref, ptop_ref, pbot_ref, w1t_ref, w1b_ref, b1_ref, w2k_ref, b2k_ref, w2v_ref, b2v_ref,
                     gk_ref, kc_ref, vct_ref):
    ch = ch_ref[0]
    n_c = ch.shape[0]
    u = jnp.dot((ch + ptop_ref[...]).astype(BF16), w1t_ref[...], preferred_element_type=F32)
    v = jnp.dot((ch + pbot_ref[...]).astype(BF16), w1b_ref[...], preferred_element_type=F32)
    v_next = pltpu.roll(v, n_c - 1, 0)
    hid = jax.nn.gelu(u + v_next + b1_ref[...])
    hk = hid[:, :CMP_HIDDEN].astype(BF16)
    hv = hid[:, CMP_HIDDEN:].astype(BF16)
    kc = jnp.dot(hk, w2k_ref[...], preferred_element_type=F32) + b2k_ref[...]
    msq = jnp.mean(kc * kc, axis=-1, keepdims=True)
    kc_ref[0] = (kc * lax.rsqrt(msq + RMS_EPS) * gk_ref[...]).astype(kc_ref.dtype)
    vct = lax.dot_general(w2v_ref[...], hv, NT_DIMS, preferred_element_type=F32) + b2v_ref[...]
    vct_ref[0] = vct.astype(vct_ref.dtype)


def _compress(chunks, ptop, pbot, w1t, w1b, b1, w2k, b2k, w2v, b2v, gk):
    b, n_c, width = chunks.shape
    const = lambda a: pl.BlockSpec(a.shape, lambda bb: (0,) * a.ndim)
    params = (ptop, pbot, w1t, w1b, b1, w2k, b2k, w2v, b2v, gk)
    return pl.pallas_call(
        _compress_kernel,
        grid=(b,),
        in_specs=[pl.BlockSpec((1, n_c, width), lambda bb: (bb, 0, 0))] + [const(a) for a in params],
        out_specs=[pl.BlockSpec((1, n_c, HEAD_DIM), lambda bb: (bb, 0, 0)),
                   pl.BlockSpec((1, HEAD_DIM, n_c), lambda bb: (bb, 0, 0))],
        out_shape=[jax.ShapeDtypeStruct((b, n_c, HEAD_DIM), BF16), jax.ShapeDtypeStruct((b, HEAD_DIM, n_c), BF16)],
        compiler_params=pltpu.CompilerParams(dimension_semantics=("arbitrary",), vmem_limit_bytes=VMEM_LIMIT),
        name="nsa_compress",
    )(chunks, *params)


def _cmp_attn_kernel(q_ref, kc_ref, vct_ref, bias_ref, o_ref, sel_ref, p_ref, *, n_sel):
    i = pl.program_id(0)
    kc = kc_ref[0]
    vct = vct_ref[0]
    n_c = kc.shape[0]
    n_blk = sel_ref.shape[1]
    probs = []

    def scores(h):
        return jnp.dot(kc, q_ref[0, h * HEAD_DIM:(h + 1) * HEAD_DIM, :], preferred_element_type=F32) + bias_ref[h, 0]

    def update(h, s):
        m = jnp.maximum(jnp.max(s, axis=0, keepdims=True), 0.5 * NEG_INF)
        p = jnp.exp2(s - m)
        den = jnp.sum(p, axis=0, keepdims=True)
        p = p * (1.0 / jnp.maximum(den, TINY))
        o_ref[0, h * HEAD_DIM:(h + 1) * HEAD_DIM, :] = jnp.dot(vct, p.astype(BF16),
                                                               preferred_element_type=F32).astype(o_ref.dtype)
        probs.append(p)

    _staggered(HEADS, scores, update)
    psum = (probs[0] + probs[1]) + (probs[2] + probs[3])
    per_blk = SLC_BLOCK // CMP_STRIDE
    halves = []
    for half in range(QT // LANES):
        p_ref[half, 0:8, :] = jnp.zeros((8, LANES), F32)
        p_ref[half, 8:8 + n_c, :] = psum[:, half * LANES:(half + 1) * LANES]
        p_ref[half, 8 + n_c:16 + n_c, :] = jnp.zeros((8, LANES), F32)
        acc = p_ref[half, pl.ds(7, n_blk, stride=per_blk), :]
        for t in range(per_blk):
            acc = acc + p_ref[half, pl.ds(8 + t, n_blk, stride=per_blk), :]
        halves.append(acc)
    imp = jnp.concatenate(halves, axis=1)
    blk = lax.broadcasted_iota(jnp.int32, (n_blk, QT), 0)
    cur = (i * QT + lax.broadcasted_iota(jnp.int32, (n_blk, QT), 1)) // SLC_BLOCK
    forced = (blk == 0) | (blk == cur) | (blk == cur - 1)
    val = jnp.where(forced, FORCE_SELECT, jnp.where(blk <= cur, imp, NEG_INF))
    sel = jnp.zeros((n_blk, QT), jnp.bool_)
    for _ in range(n_sel):
        top = jnp.max(val, axis=0, keepdims=True)
        idx = jnp.min(jnp.where(val == top, blk, n_blk), axis=0, keepdims=True)
        hit = blk == idx
        sel = sel | hit
        val = jnp.where(hit, -3.0e38, val)
    sel_ref[0] = jnp.where(sel, 1.0, 0.0).astype(sel_ref.dtype)


def _cmp_attention(q_t, kc, vct, bias, b):
    nt = q_t.shape[0]
    nq = nt // b
    n_c = kc.shape[1]
    n_blk = nq * QT // SLC_BLOCK
    return pl.pallas_call(
        functools.partial(_cmp_attn_kernel, n_sel=min(SLC_TOPK, n_blk)),
        grid=(nq, b),
        in_specs=[pl.BlockSpec((1, GROUP, QT), lambda i, bb: (bb * nq + i, 0, 0)),
                  pl.BlockSpec((1, n_c, HEAD_DIM), lambda i, bb: (bb, 0, 0)),
                  pl.BlockSpec((1, HEAD_DIM, n_c), lambda i, bb: (bb, 0, 0)),
                  pl.BlockSpec((HEADS, 1, n_c, QT), lambda i, bb: (0, 0, 0, i))],
        out_specs=[pl.BlockSpec((1, GROUP, QT), lambda i, bb: (bb * nq + i, 0, 0)),
                   pl.BlockSpec((1, n_blk, QT), lambda i, bb: (bb * nq + i, 0, 0))],
        out_shape=[jax.ShapeDtypeStruct((nt, GROUP, QT), BF16), jax.ShapeDtypeStruct((nt, n_blk, QT), BF16)],
        scratch_shapes=[pltpu.VMEM((QT // LANES, n_c + 16, LANES), F32)],
        compiler_params=pltpu.CompilerParams(dimension_semantics=("arbitrary", "arbitrary"),
                                             vmem_limit_bytes=VMEM_LIMIT),
        name="nsa_compressed_attention",
    )(q_t, kc, vct, bias)


SEL_REP = 8
N_WIN = -(-(NSA_WINDOW - 1 + KT - 1) // QT)


def _slc_win_kernel(far_ref, q_ref, ksw_ref, vs_ref, vw_ref, sel_ref, rep_ref, ocmp_ref, g_ref, bslc_ref, bwin_ref,
                    o_ref, qz_ref, m_ref, acc_ref, next_ref, ot_ref, mask_ref):
    i = pl.program_id(1)
    sel8 = jnp.dot(rep_ref[...], sel_ref[0], preferred_element_type=F32)
    mask_ref[...] = (sel8 - 1.0) * (-NEG_INF)
    blocks_per_tile = KT // SLC_BLOCK
    mrows = blocks_per_tile * SEL_REP
    zeros = jnp.zeros((HEAD_DIM, QT), BF16)
    for h in range(HEADS):
        qh = q_ref[0, h * HEAD_DIM:(h + 1) * HEAD_DIM, :]
        qz_ref[h] = jnp.concatenate([qh, zeros], axis=0)
        qz_ref[HEADS + h] = jnp.concatenate([zeros, qh], axis=0)
    _flash_reset(m_ref, acc_ref)

    def load_keys(j):
        return ksw_ref[0, pl.ds(pl.multiple_of(j * KT, KT), KT), :]

    def load_tile(j):
        j = jnp.minimum(j, i)
        m8 = mask_ref[pl.ds(pl.multiple_of(j * mrows, mrows), mrows), :]
        mask = jnp.broadcast_to(m8.reshape(blocks_per_tile, 1, SEL_REP, QT),
                                (blocks_per_tile, SLC_BLOCK // SEL_REP, SEL_REP, QT)).reshape(KT, QT)
        return load_keys(j), mask

    def values(j):
        return vs_ref[0, jnp.minimum(j, i)]

    n_far = _whole_far_groups(i, SLC_TILE_GROUP)
    _pipelined_tiles(0, n_far, HEADS, SLC_TILE_GROUP, load_tile,
                     lambda tile, j, h: jnp.dot(tile[0], qz_ref[h], preferred_element_type=F32) + tile[1],
                     lambda j, h, s: _flash_update(h, s, values(j), m_ref, acc_ref, shift=far_ref[h]), next_ref)

    def slc_scores(tile, j, h):
        k, mask = tile
        return jnp.dot(k, qz_ref[h], preferred_element_type=F32) + mask + bslc_ref[h, _bias_tile_index(i, j)]

    _pipelined_tiles(n_far, i + 1 - n_far, HEADS, SLC_TILE_GROUP, load_tile, slc_scores,
                     lambda j, h, s: _flash_update(h, s, values(j), m_ref, acc_ref), next_ref)

    def win_tile(n):
        d = N_WIN - 1 - n // HEADS
        return d, n % HEADS, jnp.maximum(i - d, 0)

    def win_scores(n):
        d, h, j = win_tile(n)
        missing = jnp.where(i < d, NEG_INF, 0.0).astype(F32)
        return jnp.dot(load_keys(j), qz_ref[HEADS + h], preferred_element_type=F32) + (bwin_ref[h, d] + missing)

    def win_update(n, s):
        _, h, j = win_tile(n)
        _flash_update(HEADS + h, s, vw_ref[0, j], m_ref, acc_ref)

    _staggered(N_WIN * HEADS, win_scores, win_update)

    for h in range(HEADS):
        g = jax.nn.sigmoid(g_ref[0, 3 * h:3 * h + 3, :])
        ot_ref[h * HEAD_DIM:(h + 1) * HEAD_DIM, :] = (g[0:1] * ocmp_ref[0, h * HEAD_DIM:(h + 1) * HEAD_DIM, :]
                                                      + g[1:2] * _flash_result(h, acc_ref)
                                                      + g[2:3] * _flash_result(HEADS + h, acc_ref))
    o_ref[0] = ot_ref[...].T.astype(o_ref.dtype)


def _slc_win_attention(far, q_t, ksw, vs_t, vw_t, sel, rep, ocmp, g_t, bslc, bwin):
    b, s, _ = ksw.shape
    nq = s // QT
    nkv = s // KT
    n_blk = s // SLC_BLOCK
    tile = lambda height: pl.BlockSpec((1, height, QT), lambda bb, i: (bb * nq + i, 0, 0))
    whole = lambda a: pl.BlockSpec(a.shape, lambda bb, i: (0,) * a.ndim)
    return pl.pallas_call(
        _slc_win_kernel,
        grid=(b, nq),
        in_specs=[pl.BlockSpec(memory_space=pltpu.SMEM), tile(GROUP),
                  pl.BlockSpec((1, s, 2 * HEAD_DIM), lambda bb, i: (bb, 0, 0)),
                  pl.BlockSpec((1, nkv, V_ROWS, KT), lambda bb, i: (bb, 0, 0, 0)),
                  pl.BlockSpec((1, nkv, V_ROWS, KT), lambda bb, i: (bb, 0, 0, 0)),
                  tile(n_blk), whole(rep), tile(GROUP), tile(16), whole(bslc), whole(bwin)],
        out_specs=pl.BlockSpec((1, QT, GROUP), lambda bb, i: (bb, i, 0)),
        out_shape=jax.ShapeDtypeStruct((b, s, GROUP), BF16),
        scratch_shapes=[pltpu.VMEM((2 * HEADS, 2 * HEAD_DIM, QT), BF16)] + _flash_scratch(2 * HEADS, MXU_LOOKAHEAD)
        + [pltpu.VMEM((GROUP, QT), F32), pltpu.VMEM((n_blk * SEL_REP, QT), F32)],
        compiler_params=pltpu.CompilerParams(dimension_semantics=("arbitrary", "arbitrary"),
                                             vmem_limit_bytes=VMEM_LIMIT),
        name="nsa_selected_window_attention",
    )(far, q_t, ksw, vs_t.reshape(b, nkv, V_ROWS, KT), vw_t.reshape(b, nkv, V_ROWS, KT), sel, rep, ocmp, g_t,
      bslc, bwin)


def _out_kernel(x_ref, a0_ref, a1_ref, a2_ref, l0_ref, l1_ref, l2_ref, ob_ref, oc_ref, od_ref, gate_ref, e_ref,
                w_ref, o_ref, unfold_ref):
    rows = x_ref.shape[0]

    def unfolded(ref, rate):
        width = ref.shape[1] // rate
        for rho in range(rate):
            for part in range(width // LANES):
                c0 = rho * width + part * LANES
                unfold_ref[part, pl.ds(rho, rows // rate, stride=rate), :] = ref[:, c0:c0 + LANES].astype(F32)
        return jnp.concatenate([unfold_ref[part] for part in range(width // LANES)], axis=1)

    a0, l0 = a0_ref[...], l0_ref[...]
    a1, l1 = unfolded(a1_ref, FOLD_RATES[0]), unfolded(l1_ref, FOLD_RATES[0])
    a2, l2 = unfolded(a2_ref, FOLD_RATES[1]), unfolded(l2_ref, FOLD_RATES[1])
    mx = jnp.maximum(jnp.maximum(l0, l1), l2)
    e0, e1, e2 = jnp.exp(l0 - mx), jnp.exp(l1 - mx), jnp.exp(l2 - mx)
    den = e0 + e1 + e2

    def per_head_lanes(w):
        hi = w.astype(BF16)
        lo = (w - hi.astype(F32)).astype(BF16)
        return (jnp.dot(hi, e_ref[...], preferred_element_type=F32)
                + jnp.dot(lo, e_ref[...], preferred_element_type=F32))

    o_a = per_head_lanes(e0 / den) * a0 + per_head_lanes(e1 / den) * a1 + per_head_lanes(e2 / den) * a2
    y = jnp.concatenate([o_a, ob_ref[...].astype(F32), oc_ref[...].astype(F32), od_ref[...].astype(F32)], axis=1)
    g = gate_ref[...].astype(F32)
    y = y * (g * jax.nn.sigmoid(g))
    o_ref[...] = x_ref[...] + jnp.dot(y.astype(BF16), w_ref[...], preferred_element_type=F32)


def _out_projection(x2, a_outs, a_lses, o_b, o_c, o_d, gate, w_out):
    m, d = x2.shape
    rowblk = lambda width: pl.BlockSpec((PROJ_ROWS, width), lambda i: (i, 0))
    folded = lambda width, rate: pl.BlockSpec((PROJ_ROWS // rate, rate * width), lambda i: (i, 0))
    head_of_lane = np.arange(GROUP) // HEAD_DIM
    expand = jnp.asarray((np.arange(LANES)[:, None] == head_of_lane[None, :]).astype(np.float32), BF16)
    return pl.pallas_call(
        _out_kernel,
        grid=(m // PROJ_ROWS,),
        in_specs=[rowblk(d)] + [folded(GROUP, rate) for rate in (1,) + FOLD_RATES]
        + [folded(LANES, rate) for rate in (1,) + FOLD_RATES] + [rowblk(GROUP)] * 3
        + [rowblk(N_MIXERS * GROUP), pl.BlockSpec((LANES, GROUP), lambda i: (0, 0)),
           pl.BlockSpec((N_MIXERS * GROUP, d), lambda i: (0, 0))],
        out_specs=rowblk(d),
        out_shape=jax.ShapeDtypeStruct((m, d), F32),
        scratch_shapes=[pltpu.VMEM((GROUP // LANES, PROJ_ROWS, LANES), F32)],
        compiler_params=pltpu.CompilerParams(dimension_semantics=("arbitrary",), vmem_limit_bytes=VMEM_LIMIT),
        name="out_projection",
    )(x2, *a_outs, *a_lses, o_b, o_c, o_d, gate, expand, w_out)


def _block_diag_mean(group):
    idx = np.arange(GROUP) // group
    return jnp.asarray((idx[:, None] == idx[None, :]).astype(np.float32) / group, BF16)


def _layer_weights(w_in, qk_gain, qk_gain_diff):
    d = w_in.shape[0]
    sizes = (GROUP,) * 3 + (GROUP, GROUP // 2, GROUP // 2) + (GROUP,) * 3 + (GROUP,) + (HEAD_DIM,) * 6 \
        + (HEADS * 3, N_MIXERS * GROUP)
    offs = np.concatenate([[0], np.cumsum(sizes)])
    col = lambda n: w_in[:, offs[n]:offs[n + 1]]
    (a_q, a_k, a_v, b_q, b_k, b_v, c_q, c_k, c_v, d_q, d_kc, d_vc, d_ks, d_vs, d_kw, d_vw, d_g, gate) = \
        [col(n) for n in range(18)]
    rep_kv = lambda w: jnp.repeat(w.reshape(d, 2, HEAD_DIM), 2, axis=1).reshape(d, GROUP)
    wrm = jnp.concatenate([a_q, a_k, a_v, b_q, rep_kv(b_k), rep_kv(b_v), c_k, d_kc, d_vc, d_ks, d_kw, gate], axis=1)
    wt = jnp.concatenate([c_q, c_v, d_q, d_vs, d_vw, d_g, jnp.zeros((d, 16 - HEADS * 3), w_in.dtype)], axis=1).T
    g = qk_gain
    ones = lambda n: jnp.ones((n,), F32)
    tile4 = lambda v: jnp.tile(v, HEADS)
    scale = HEAD_DIM ** -0.5
    grm = jnp.concatenate([tile4(g[0]) * scale, tile4(g[1]), ones(GROUP), tile4(g[2]) * scale, tile4(g[3]),
                           ones(GROUP), jnp.tile(qk_gain_diff[1], 2 * HEADS), ones(2 * HEAD_DIM), g[6], g[7],
                           ones(N_MIXERS * GROUP)])
    gt = jnp.concatenate([jnp.tile(qk_gain_diff[0], 2 * HEADS) * (DIFF_QK_DIM ** -0.5 * LOG2E), ones(GROUP),
                          tile4(g[4]) * (scale * LOG2E), ones(2 * HEAD_DIM + 16)])
    return wrm.astype(BF16), wt.astype(BF16), grm.reshape(1, -1), gt.reshape(-1, 1)


def _compress_weights(cmp_pos, cmp_w1, cmp_b1, cmp_w2, cmp_b2):
    half = CMP_LEN // 2
    pos = jnp.concatenate([cmp_pos[0], cmp_pos[1]], axis=-1)
    ptop = pos[:half].reshape(1, -1)
    pbot = pos[half:].reshape(1, -1)
    w1 = cmp_w1.reshape(2, CMP_LEN, HEAD_DIM, CMP_HIDDEN)
    zeros = jnp.zeros_like(w1[0])
    w1cat = jnp.concatenate([jnp.concatenate([w1[0], zeros], axis=-1),
                             jnp.concatenate([zeros, w1[1]], axis=-1)], axis=1)
    w1t = w1cat[:half].reshape(half * 2 * HEAD_DIM, 2 * CMP_HIDDEN).astype(BF16)
    w1b = w1cat[half:].reshape(half * 2 * HEAD_DIM, 2 * CMP_HIDDEN).astype(BF16)
    b1 = jnp.concatenate([cmp_b1[0], cmp_b1[1]]).reshape(1, -1)
    return (ptop, pbot, w1t, w1b, b1, cmp_w2[0].astype(BF16), cmp_b2[0].reshape(1, -1),
            cmp_w2[1].T.astype(BF16), cmp_b2[1].reshape(-1, 1))


def kernel(x, rel_bias_table, norm_w, w_in, w_out, qk_gain, qk_gain_diff, attn_sinks, diff_lambda, diff_subln,
           cmp_pos, cmp_w1, cmp_b1, cmp_w2, cmp_b2):
    b, s, d = x.shape
    depth = w_in.shape[0]
    n_c = s // CMP_STRIDE
    n_blk = s // SLC_BLOCK
    assert s % (BAND_TILE * DILATED_CONFIGS[-1][1]) == 0 and s % PROJ_ROWS == 0 and d == N_MIXERS * GROUP

    table = rel_bias_table.astype(F32)
    band_bias = [_build_bias(table, head0=0, n_d=1, rows=BAND_TILE, cols=2 * BAND_TILE, base0=BAND_TILE, dstep=0,
                             rs=1, cs=-1, dscale=rate, max_dist=window // rate) for window, rate in DILATED_CONFIGS]
    swa_bias = _build_bias(table, head0=HEADS, n_d=1, rows=BAND_TILE, cols=2 * BAND_TILE, base0=BAND_TILE, dstep=0,
                           rs=1, cs=-1, max_dist=SWA_WINDOW - 1)
    flash_tiles = dict(rows=KT, cols=QT, base0=0, dstep=QT, rs=-1, cs=1, scale=LOG2E)
    diff_bias = _build_bias(table, head0=2 * HEADS, n_d=N_BIAS_TILES, d_valid=N_NEAR + 1, **flash_tiles)
    slc_bias = _build_bias(table, head0=3 * HEADS, n_d=N_BIAS_TILES, d_valid=N_NEAR + 1, **flash_tiles)
    win_bias = _build_bias(table, head0=3 * HEADS, n_d=N_WIN, max_dist=NSA_WINDOW - 1, **flash_tiles)
    far_bias = table[NUM_BUCKETS - 1] * LOG2E
    cmp_bias = _build_bias(table, head0=3 * HEADS, n_d=1, rows=n_c, cols=s, base0=-(CMP_LEN - 1), dstep=0,
                           rs=-CMP_STRIDE, cs=1, r_valid=n_c - 1, col_tile=4 * QT, scale=LOG2E)
    e64, e32 = _block_diag_mean(HEAD_DIM), _block_diag_mean(DIFF_QK_DIM)
    rep_idx = np.arange(n_blk * SEL_REP) // SEL_REP
    rep = jnp.asarray((rep_idx[:, None] == np.arange(n_blk)[None, :]).astype(np.float32), BF16)
    no_sink = jnp.zeros((HEADS,), F32)

    x2 = x.reshape(b * s, d)
    for layer in range(depth):
        wrm, wt, grm, gt = _layer_weights(w_in[layer], qk_gain[layer], qk_gain_diff[layer])
        (a_q, a_k, a_v, b_q, b_k, b_v, c_k, kvc, ksw, gate, a_q4, a_k4, a_v4, a_q16, a_k16, a_v16,
         c_qt, c_vt, d_qt, d_vst, d_vwt, d_gt) = _project(x2, norm_w[layer].reshape(1, d), wrm, wt, grm, gt, e64, e32)
        seq = lambda t: t.reshape(b, s, t.shape[-1])
        per_batch = lambda t: t.reshape(b, t.shape[0] // b, t.shape[1])
        flat = lambda t: t.reshape(b * t.shape[1], t.shape[2])
        a_in = ((a_q, a_k, a_v), (a_q4, a_k4, a_v4), (a_q16, a_k16, a_v16))
        a_res = [_banded(*map(per_batch, a_in[n]), band_bias[n], no_sink, rate, False)
                 for n, (_, rate) in enumerate(DILATED_CONFIGS)]
        o_b, _ = _banded(seq(b_q), seq(b_k), seq(b_v), swa_bias, attn_sinks[layer].astype(F32), 1, True)
        lambda_init = 0.8 - 0.6 * math.exp(-0.3 * layer)
        o_c = _diff_attention(far_bias[2 * HEADS:3 * HEADS], c_qt, seq(c_k), c_vt, diff_bias, diff_lambda[layer].astype(F32),
                              diff_subln[layer].reshape(HEAD_DIM, 1).astype(F32), lambda_init)
        cw = _compress_weights(cmp_pos[layer], cmp_w1[layer], cmp_b1[layer], cmp_w2[layer], cmp_b2[layer])
        kc, vct = _compress(kvc.reshape(b, n_c, CMP_STRIDE * 2 * HEAD_DIM), *cw, qk_gain[layer, 5].reshape(1, -1))
        o_cmp, sel = _cmp_attention(d_qt, kc, vct, cmp_bias, b)
        o_d = _slc_win_attention(far_bias[3 * HEADS:4 * HEADS], d_qt, seq(ksw), d_vst, d_vwt, sel, rep, o_cmp, d_gt, slc_bias, win_bias)
        x2 = _out_projection(x2, [flat(r[0]) for r in a_res], [flat(r[1]) for r in a_res], o_b.reshape(b * s, GROUP),
                             o_c.reshape(b * s, GROUP), o_d.reshape(b * s, GROUP), gate, w_out[layer].astype(BF16))
    return x2.reshape(b, s, d)
```

```python
import functools
import math

import numpy as np
import jax
import jax.numpy as jnp
from jax import lax
from jax.experimental import pallas as pl
from jax.experimental.pallas import tpu as pltpu

F32 = jnp.float32
BF16 = jnp.bfloat16

HEAD_DIM = 64
HEADS = 4
GROUP = HEADS * HEAD_DIM
N_MIXERS = 4
NUM_BUCKETS = 32
REL_MAX_DIST = 2048
DILATED_CONFIGS = ((128, 1), (512, 4), (2048, 16))
FOLD_RATES = tuple(rate for _, rate in DILATED_CONFIGS if rate > 1)
SWA_WINDOW = 128
DIFF_QK_DIM = HEAD_DIM // 2
CMP_LEN = 32
CMP_STRIDE = 16
CMP_HIDDEN = 256
SLC_BLOCK = 64
SLC_TOPK = 16
NSA_WINDOW = 512
RMS_EPS = 1e-6
NEG_INF = -1e30
FORCE_SELECT = 1e9
TINY = 1e-30
LOG2E = math.log2(math.e)

PROJ_ROWS = 512
BAND_TILE = 128
BAND_STEP = 1024
BAND_LOOKAHEAD = 8
BIAS_ROW_CHUNK = 64
LANES = 128
QT = 256
KT = 256
VMEM_LIMIT = 56 * 1024 * 1024
MXU_LOOKAHEAD = 4
V_ROWS = HEAD_DIM + 16

NT_DIMS = (((1,), (1,)), ((), ()))


def _t5_thresholds():
    n = np.arange(0, 4 * REL_MAX_DIST)
    max_exact = NUM_BUCKETS // 2
    nf = np.maximum(n, 1).astype(np.float32)
    large = max_exact + (np.log(nf / np.float32(max_exact)) / np.float32(math.log(REL_MAX_DIST / max_exact))
                         * np.float32(NUM_BUCKETS - max_exact)).astype(np.int32)
    bucket = np.where(n < max_exact, n, np.minimum(large, NUM_BUCKETS - 1))
    return [int(np.argmax(bucket >= b)) for b in range(NUM_BUCKETS)]


T5_THRESHOLDS = _t5_thresholds()
FAR_DIST = T5_THRESHOLDS[-1]


def _bias_kernel(tbl_ref, out_ref, *, head0, base0, dstep, rs, cs, dscale, max_dist, r_valid, d_valid, col_tile,
                 scale):
    h = pl.program_id(0)
    d = pl.program_id(1)
    ct = pl.program_id(2)
    rows, cols = out_ref.shape[-2:]
    chunk = BIAS_ROW_CHUNK if rows % BIAS_ROW_CHUNK == 0 else rows
    for r0 in range(0, rows, chunk):
        origin = base0 + d * dstep + r0 * rs + ct * col_tile * cs
        corners = [origin + dr * rs + dc * cs for dr in (0, chunk - 1) for dc in (0, cols - 1)]
        lo = functools.reduce(jnp.minimum, corners)
        hi = functools.reduce(jnp.maximum, corners)
        all_masked = (hi < 0) | (lo > max_dist) | (d >= d_valid) | (r0 >= r_valid)
        all_far = (lo * dscale >= FAR_DIST) & (hi <= max_dist) & (d < d_valid) & (r0 + chunk <= r_valid)
        out = out_ref.at[0, 0, r0:r0 + chunk, :]

        @pl.when(all_masked)
        def _():
            out[...] = jnp.full((chunk, cols), NEG_INF, F32)

        @pl.when(all_far)
        def _():
            out[...] = jnp.full((chunk, cols), tbl_ref[NUM_BUCKETS - 1, head0 + h] * scale, F32)

        @pl.when(jnp.logical_not(all_masked | all_far))
        def _():
            r = lax.broadcasted_iota(jnp.int32, (chunk, cols), 0)
            c = lax.broadcasted_iota(jnp.int32, (chunk, cols), 1)
            dist = origin + r * rs + c * cs
            n = dist * dscale
            val = jnp.full((chunk, cols), tbl_ref[0, head0 + h], F32)
            for b in range(1, NUM_BUCKETS):
                val = jnp.where(n >= T5_THRESHOLDS[b], tbl_ref[b, head0 + h], val)
            valid = (dist >= 0) & (dist <= max_dist) & (r + r0 < r_valid) & (d < d_valid)
            out[...] = jnp.where(valid, val * scale, NEG_INF)


def _build_bias(table, *, head0, n_d, rows, cols, base0, dstep, rs, cs, dscale=1, max_dist=1 << 30,
                r_valid=1 << 30, d_valid=1 << 30, col_tile=None, scale=1.0):
    col_tile = cols if col_tile is None else col_tile
    kern = functools.partial(_bias_kernel, head0=head0, base0=base0, dstep=dstep, rs=rs, cs=cs, dscale=dscale,
                             max_dist=max_dist, r_valid=r_valid, d_valid=d_valid, col_tile=col_tile, scale=scale)
    return pl.pallas_call(
        kern,
        grid=(HEADS, n_d, cols // col_tile),
        in_specs=[pl.BlockSpec(memory_space=pltpu.SMEM)],
        out_specs=pl.BlockSpec((1, 1, rows, col_tile), lambda h, d, c: (h, d, 0, c)),
        out_shape=jax.ShapeDtypeStruct((HEADS, n_d, rows, cols), F32),
        name="rel_bias_tiles",
    )(table)


RM_AQ, RM_AK, RM_AV = 0, 256, 512
RM_BQ, RM_BK, RM_BV = 768, 1024, 1280
RM_CK = 1536
RM_KVC = 1792
RM_KSW = 1920
RM_GATE = 2048
RM_COLS = 3072
TR_CQ, TR_CV, TR_DQ, TR_DVS, TR_DVW, TR_DG = 0, 256, 512, 768, 832, 896
TR_ROWS = 912


def _proj_kernel(x_ref, nw_ref, wrm_ref, wt_ref, grm_ref, gt_ref, e64_ref, e32_ref,
                 aq_ref, ak_ref, av_ref, bq_ref, bk_ref, bv_ref, ck_ref, kvc_ref, ksw_ref, gate_ref,
                 aq4_ref, ak4_ref, av4_ref, aq16_ref, ak16_ref, av16_ref,
                 cq_ref, cv_ref, dq_ref, dvs_ref, dvw_ref, dg_ref, fold_ref):
    x = x_ref[...]
    ms = jnp.mean(x * x, axis=-1, keepdims=True)
    xn = (x * lax.rsqrt(ms + RMS_EPS) * nw_ref[...]).astype(BF16)
    rows = x.shape[0]

    def rm(c0, width):
        return jnp.dot(xn, wrm_ref[:, c0:c0 + width], preferred_element_type=F32)

    def rm_normed(c0, width, e_ref):
        h = rm(c0, width)
        msq = jnp.dot((h * h).astype(BF16), e_ref[0:width, 0:width], preferred_element_type=F32)
        return h * lax.rsqrt(msq + RMS_EPS) * grm_ref[:, c0:c0 + width]

    def put_folded(val, ref, folded_refs):
        ref[...] = val.astype(ref.dtype)
        for half in range(GROUP // LANES):
            fold_ref[half] = val[:, half * LANES:(half + 1) * LANES]
        for rate, fref in zip(FOLD_RATES, folded_refs):
            for rho in range(rate):
                for half in range(GROUP // LANES):
                    c0 = rho * GROUP + half * LANES
                    fref[:, c0:c0 + LANES] = fold_ref[half, pl.ds(rho, rows // rate, stride=rate), :].astype(fref.dtype)

    put_folded(rm_normed(RM_AQ, GROUP, e64_ref), aq_ref, (aq4_ref, aq16_ref))
    put_folded(rm_normed(RM_AK, GROUP, e64_ref), ak_ref, (ak4_ref, ak16_ref))
    put_folded(rm(RM_AV, GROUP), av_ref, (av4_ref, av16_ref))
    bq_ref[...] = rm_normed(RM_BQ, GROUP, e64_ref).astype(bq_ref.dtype)
    bk_ref[...] = rm_normed(RM_BK, GROUP, e64_ref).astype(bk_ref.dtype)
    bv_ref[...] = rm(RM_BV, GROUP).astype(bv_ref.dtype)
    ck_ref[...] = rm_normed(RM_CK, GROUP, e32_ref).astype(ck_ref.dtype)
    kvc_ref[...] = rm(RM_KVC, 2 * HEAD_DIM).astype(kvc_ref.dtype)
    ksw_ref[...] = rm_normed(RM_KSW, 2 * HEAD_DIM, e64_ref).astype(ksw_ref.dtype)
    gate_ref[...] = rm(RM_GATE, N_MIXERS * GROUP).astype(gate_ref.dtype)

    def tr(r0, height):
        return lax.dot_general(wt_ref[r0:r0 + height, :], xn, NT_DIMS, preferred_element_type=F32)

    def tr_normed(r0, height, group):
        h3 = tr(r0, height).reshape(height // group, group, rows)
        msq = jnp.mean(h3 * h3, axis=1, keepdims=True)
        return (h3 * lax.rsqrt(msq + RMS_EPS)).reshape(height, rows) * gt_ref[r0:r0 + height, :]

    def put(ref, val):
        for t in range(rows // QT):
            ref[t] = val[:, t * QT:(t + 1) * QT].astype(ref.dtype)

    def with_ones(v):
        ones = jnp.ones((V_ROWS - HEAD_DIM, rows), F32)
        parts = []
        for h in range(v.shape[0] // HEAD_DIM):
            parts += [v[h * HEAD_DIM:(h + 1) * HEAD_DIM], ones]
        return jnp.concatenate(parts, axis=0)

    put(cq_ref, tr_normed(TR_CQ, GROUP, DIFF_QK_DIM))
    put(cv_ref, with_ones(tr(TR_CV, GROUP)))
    put(dq_ref, tr_normed(TR_DQ, GROUP, HEAD_DIM))
    put(dvs_ref, with_ones(tr(TR_DVS, HEAD_DIM)))
    put(dvw_ref, with_ones(tr(TR_DVW, HEAD_DIM)))
    put(dg_ref, tr(TR_DG, 16))


def _project(x2, nw, wrm, wt, grm, gt, e64, e32):
    m, d = x2.shape
    nt = m // QT
    tpr = PROJ_ROWS // QT
    const = lambda shape: pl.BlockSpec(shape, lambda i: (0,) * len(shape))
    rm_out = lambda width, dtype: (jax.ShapeDtypeStruct((m, width), dtype),
                                   pl.BlockSpec((PROJ_ROWS, width), lambda i: (i, 0)))
    tr_out = lambda height, dtype: (jax.ShapeDtypeStruct((nt, height, QT), dtype),
                                    pl.BlockSpec((tpr, height, QT), lambda i: (i, 0, 0)))
    outs = [rm_out(GROUP, BF16)] * 7 + [rm_out(2 * HEAD_DIM, F32), rm_out(2 * HEAD_DIM, BF16),
                                        rm_out(N_MIXERS * GROUP, BF16)]
    fold_out = lambda rate: (jax.ShapeDtypeStruct((m // rate, rate * GROUP), BF16),
                             pl.BlockSpec((PROJ_ROWS // rate, rate * GROUP), lambda i: (i, 0)))
    outs += [fold_out(rate) for rate in FOLD_RATES for _ in range(3)]
    outs += [tr_out(GROUP, BF16), tr_out(HEADS * V_ROWS, BF16), tr_out(GROUP, BF16), tr_out(V_ROWS, BF16),
             tr_out(V_ROWS, BF16), tr_out(16, F32)]
    return pl.pallas_call(
        _proj_kernel,
        grid=(m // PROJ_ROWS,),
        in_specs=[pl.BlockSpec((PROJ_ROWS, d), lambda i: (i, 0)), const((1, d)), const((d, RM_COLS)),
                  const((TR_ROWS, d)), const((1, RM_COLS)), const((TR_ROWS, 1)), const((GROUP, GROUP)),
                  const((GROUP, GROUP))],
        out_specs=[o[1] for o in outs],
        out_shape=[o[0] for o in outs],
        scratch_shapes=[pltpu.VMEM((GROUP // LANES, PROJ_ROWS, LANES), F32)],
        compiler_params=pltpu.CompilerParams(dimension_semantics=("arbitrary",), vmem_limit_bytes=VMEM_LIMIT),
        name="in_projection",
    )(x2, nw, wrm, wt, grm, gt, e64, e32)


def _band_kernel(sink_ref, q_ref, kp_ref, kc_ref, vp_ref, vc_ref, bias_ref, o_ref, lse_ref, p_ref, *, use_sink):
    i = pl.program_id(2)
    n_blocks = q_ref.shape[1] // BAND_TILE
    head_q = lax.broadcasted_iota(jnp.int32, (BAND_TILE, GROUP), 1) // HEAD_DIM
    head_v = lax.broadcasted_iota(jnp.int32, (2 * BAND_TILE, GROUP), 1) // HEAD_DIM
    lane = lax.broadcasted_iota(jnp.int32, (BAND_TILE, LANES), 1)
    in_prev = lax.broadcasted_iota(jnp.int32, (1, 2 * BAND_TILE), 1) < BAND_TILE
    no_prev = jnp.where(in_prev & (i == 0), NEG_INF, 0.0).astype(F32)

    def window(cur_ref, prev_ref, m):
        if m == 0:
            return jnp.concatenate([prev_ref[0], cur_ref[0, 0:BAND_TILE, :]], axis=0)
        return cur_ref[0, (m - 1) * BAND_TILE:(m + 1) * BAND_TILE, :]

    def scores(n):
        m, h = divmod(n, HEADS)
        q = q_ref[0, m * BAND_TILE:(m + 1) * BAND_TILE, :]
        qh = jnp.where(head_q == h, q, jnp.zeros_like(q))
        bias = bias_ref[h, 0] + no_prev if m == 0 else bias_ref[h, 0]
        return lax.dot_general(qh, window(kc_ref, kp_ref, m), NT_DIMS, preferred_element_type=F32) + bias

    lse_tiles = {}

    def update(n, s):
        m, h = divmod(n, HEADS)
        mx = jnp.max(s, axis=1, keepdims=True)
        if use_sink:
            mx = jnp.maximum(mx, sink_ref[h])
        p = jnp.exp(s - mx)
        den = jnp.sum(p, axis=1, keepdims=True)
        if use_sink:
            den = den + jnp.exp(sink_ref[h] - mx)
        p_ref[m % 2, :, h * 2 * BAND_TILE:(h + 1) * 2 * BAND_TILE] = (p * (1.0 / den)).astype(BF16)
        lse_tiles[m] = jnp.where(lane == h, mx + jnp.log(den), lse_tiles.get(m, jnp.zeros((BAND_TILE, LANES), F32)))
        if h == HEADS - 1:
            v = window(vc_ref, vp_ref, m)
            v_heads = jnp.concatenate([jnp.where(head_v == hh, v, jnp.zeros_like(v)) for hh in range(HEADS)], axis=0)
            rows = slice(m * BAND_TILE, (m + 1) * BAND_TILE)
            o_ref[0, rows, :] = jnp.dot(p_ref[m % 2], v_heads, preferred_element_type=F32).astype(o_ref.dtype)
            lse_ref[0, rows, :] = lse_tiles.pop(m)

    _staggered(n_blocks * HEADS, scores, update, ahead=BAND_LOOKAHEAD)


def _banded(q, k, v, bias, sink, rate, use_sink):
    b, ln, _ = q.shape
    step = min(BAND_STEP, ln)
    per_step = step // BAND_TILE
    cur = pl.BlockSpec((1, step, GROUP), lambda bb, r, i: (bb, i, r))
    prev = pl.BlockSpec((1, BAND_TILE, GROUP), lambda bb, r, i: (bb, jnp.maximum(i * per_step - 1, 0), r))
    o, lse = pl.pallas_call(
        functools.partial(_band_kernel, use_sink=use_sink),
        grid=(b, rate, ln // step),
        in_specs=[pl.BlockSpec(memory_space=pltpu.SMEM), cur, prev, cur, prev, cur,
                  pl.BlockSpec((HEADS, 1, BAND_TILE, 2 * BAND_TILE), lambda bb, r, i: (0, 0, 0, 0))],
        out_specs=[cur, pl.BlockSpec((1, step, LANES), lambda bb, r, i: (bb, i, r))],
        out_shape=[jax.ShapeDtypeStruct((b, ln, rate * GROUP), BF16), jax.ShapeDtypeStruct((b, ln, rate * LANES), F32)],
        scratch_shapes=[pltpu.VMEM((2, BAND_TILE, HEADS * 2 * BAND_TILE), BF16)],
        compiler_params=pltpu.CompilerParams(dimension_semantics=("arbitrary",) * 3),
        name=f"banded_attention_r{rate}",
    )(sink, q, k, k, v, v, bias)
    return o, lse


def _flash_reset(m_ref, acc_ref):
    m_ref[...] = jnp.full(m_ref.shape, NEG_INF, F32)
    acc_ref[...] = jnp.zeros(acc_ref.shape, F32)


def _flash_update(n, s, v_t, m_ref, acc_ref, shift=None):
    m_old = m_ref[n]
    if shift is None:
        m_new = jnp.maximum(m_old, jnp.max(s, axis=0, keepdims=True))
        p = jnp.exp2(s - m_new)
    else:
        m_new = jnp.maximum(m_old, jnp.max(s, axis=0, keepdims=True) + shift)
        p = jnp.exp2(s - (m_new - shift))
    alpha = jnp.exp2(m_old - m_new)
    acc_ref[n] = alpha * acc_ref[n] + jnp.dot(v_t, p.astype(BF16), preferred_element_type=F32)
    m_ref[n] = m_new


def _flash_result(n, acc_ref):
    return acc_ref[n, 0:HEAD_DIM, :] / acc_ref[n, HEAD_DIM:HEAD_DIM + 1, :]


def _staggered(n_items, scores, update, ahead=MXU_LOOKAHEAD):
    pending = {n: scores(n) for n in range(min(ahead, n_items))}
    for n in range(n_items):
        if n + ahead < n_items:
            pending[n + ahead] = scores(n + ahead)
        update(n, pending.pop(n))


def _pipelined_tiles(first, n_tiles, n_chains, group, load_tile, scores, update, next_ref, left_by_previous=None):
    ahead = next_ref.shape[0]
    n_items = group * n_chains
    assert ahead <= n_chains

    def body(trip, _):
        base = first + trip * group
        tiles, pending = {}, {}
        for n in range(n_items):
            cur = next_ref[n] if n < ahead else pending.pop(n)
            g, c = divmod(n + ahead, n_chains)
            if g not in tiles:
                tiles[g] = load_tile(base + g)
            new = scores(tiles[g], base + g, c)
            if n + ahead < n_items:
                pending[n + ahead] = new
            else:
                next_ref[n + ahead - n_items] = new
            update(base + n // n_chains, n % n_chains, cur)

    if left_by_previous is None:
        first_tile = load_tile(first)
        for n in range(ahead):
            next_ref[n] = scores(first_tile, first, n)
    else:
        for n in range(ahead):
            next_ref[n] = next_ref[n] + left_by_previous(first, n)
    lax.fori_loop(0, (n_tiles + group - 1) // group, body, None)


def _flash_scratch(chains, ahead):
    return [pltpu.VMEM((chains, 1, QT), F32), pltpu.VMEM((chains, V_ROWS, QT), F32),
            pltpu.VMEM((ahead, KT, QT), F32)]


N_NEAR = -(-(FAR_DIST + KT - 1) // QT)
N_BIAS_TILES = N_NEAR + 2
DIFF_TILE_GROUP = 2
SLC_TILE_GROUP = 2
FAR_TILE_GROUP = 4


def _bias_tile_index(i, j):
    return jnp.where(j > i, N_NEAR + 1, jnp.minimum(i - j, N_NEAR))


def _whole_far_groups(i, group):
    return jnp.maximum(i - (N_NEAR - 1), 0) // group * group


def _diff_kernel(far_ref, q_ref, k_ref, v_ref, bias_ref, lam_ref, subln_ref, o_ref, qz_ref, m_ref, acc_ref, next_ref,
                 ot_ref, *, lambda_init):
    i = pl.program_id(1)
    q = q_ref[0]
    row = lax.broadcasted_iota(jnp.int32, (GROUP, QT), 0) // DIFF_QK_DIM
    for n in range(2 * HEADS):
        qz_ref[n] = jnp.where(row == n, q, jnp.zeros_like(q))
    _flash_reset(m_ref, acc_ref)

    def load_tile(j):
        return k_ref[0, pl.ds(pl.multiple_of(jnp.minimum(j, i) * KT, KT), KT), :]

    def values(j, n):
        h = n // 2
        return v_ref[0, jnp.minimum(j, i), h * V_ROWS:(h + 1) * V_ROWS, :]

    n_far = _whole_far_groups(i, FAR_TILE_GROUP)
    _pipelined_tiles(0, n_far, 2 * HEADS, FAR_TILE_GROUP, load_tile,
                     lambda k, j, n: jnp.dot(k, qz_ref[n], preferred_element_type=F32),
                     lambda j, n, s: _flash_update(n, s, values(j, n), m_ref, acc_ref, shift=far_ref[n // 2]),
                     next_ref)

    def scores(k, j, n):
        return jnp.dot(k, qz_ref[n], preferred_element_type=F32) + bias_ref[n // 2, _bias_tile_index(i, j)]

    _pipelined_tiles(n_far, i + 1 - n_far, 2 * HEADS, DIFF_TILE_GROUP, load_tile, scores,
                     lambda j, n, s: _flash_update(n, s, values(j, n), m_ref, acc_ref), next_ref,
                     left_by_previous=lambda j, n: bias_ref[n // 2, _bias_tile_index(i, j)])

    lam_p = lam_ref[...]
    lam = (jnp.exp(jnp.sum(lam_p[0:1] * lam_p[1:2], axis=1, keepdims=True))
           - jnp.exp(jnp.sum(lam_p[2:3] * lam_p[3:4], axis=1, keepdims=True)) + lambda_init)
    for h in range(HEADS):
        o = _flash_result(2 * h, acc_ref) - lam * _flash_result(2 * h + 1, acc_ref)
        msq = jnp.mean(o * o, axis=0, keepdims=True)
        ot_ref[h * HEAD_DIM:(h + 1) * HEAD_DIM, :] = (o * lax.rsqrt(msq + RMS_EPS) * subln_ref[...]
                                                      * (1.0 - lambda_init))
    o_ref[0] = ot_ref[...].T.astype(o_ref.dtype)


def _diff_attention(far, q_t, k, v_t, bias, lam_p, subln, lambda_init):
    b, s, _ = k.shape
    nq = s // QT
    nkv = s // KT
    v4 = v_t.reshape(b, nkv, HEADS * V_ROWS, KT)
    return pl.pallas_call(
        functools.partial(_diff_kernel, lambda_init=lambda_init),
        grid=(b, nq),
        in_specs=[pl.BlockSpec(memory_space=pltpu.SMEM),
                  pl.BlockSpec((1, GROUP, QT), lambda bb, i: (bb * nq + i, 0, 0)),
                  pl.BlockSpec((1, s, GROUP), lambda bb, i: (bb, 0, 0)),
                  pl.BlockSpec((1, nkv, HEADS * V_ROWS, KT), lambda bb, i: (bb, 0, 0, 0)),
                  pl.BlockSpec((HEADS, N_BIAS_TILES, KT, QT), lambda bb, i: (0, 0, 0, 0)),
                  pl.BlockSpec((4, DIFF_QK_DIM), lambda bb, i: (0, 0)),
                  pl.BlockSpec((HEAD_DIM, 1), lambda bb, i: (0, 0))],
        out_specs=pl.BlockSpec((1, QT, GROUP), lambda bb, i: (bb, i, 0)),
        out_shape=jax.ShapeDtypeStruct((b, s, GROUP), BF16),
        scratch_shapes=[pltpu.VMEM((2 * HEADS, GROUP, QT), BF16)] + _flash_scratch(2 * HEADS, MXU_LOOKAHEAD)
        + [pltpu.VMEM((GROUP, QT), F32)],
        compiler_params=pltpu.CompilerParams(dimension_semantics=("arbitrary", "arbitrary"),
                                             vmem_limit_bytes=VMEM_LIMIT),
        name="diff_attention",
    )(far, q_t, k, v4, bias, lam_p, subln)


def _compress_kernel(ch_ref, ptop_ref, pbot_ref, w1t_ref, w1b_ref, b1_ref, w2k_ref, b2k_ref, w2v_ref, b2v_ref,
                     gk_ref, kc_ref, vct_ref):
    ch = ch_ref[0]
    n_c = ch.shape[0]
    u = jnp.dot((ch + ptop_ref[...]).astype(BF16), w1t_ref[...], preferred_element_type=F32)
    v = jnp.dot((ch + pbot_ref[...]).astype(BF16), w1b_ref[...], preferred_element_type=F32)
    v_next = pltpu.roll(v, n_c - 1, 0)
    hid = jax.nn.gelu(u + v_next + b1_ref[...])
    hk = hid[:, :CMP_HIDDEN].astype(BF16)
    hv = hid[:, CMP_HIDDEN:].astype(BF16)
    kc = jnp.dot(hk, w2k_ref[...], preferred_element_type=F32) + b2k_ref[...]
    msq = jnp.mean(kc * kc, axis=-1, keepdims=True)
    kc_ref[0] = (kc * lax.rsqrt(msq + RMS_EPS) * gk_ref[...]).astype(kc_ref.dtype)
    vct = lax.dot_general(w2v_ref[...], hv, NT_DIMS, preferred_element_type=F32) + b2v_ref[...]
    vct_ref[0] = vct.astype(vct_ref.dtype)


def _compress(chunks, ptop, pbot, w1t, w1b, b1, w2k, b2k, w2v, b2v, gk):
    b, n_c, width = chunks.shape
    const = lambda a: pl.BlockSpec(a.shape, lambda bb: (0,) * a.ndim)
    params = (ptop, pbot, w1t, w1b, b1, w2k, b2k, w2v, b2v, gk)
    return pl.pallas_call(
        _compress_kernel,
        grid=(b,),
        in_specs=[pl.BlockSpec((1, n_c, width), lambda bb: (bb, 0, 0))] + [const(a) for a in params],
        out_specs=[pl.BlockSpec((1, n_c, HEAD_DIM), lambda bb: (bb, 0, 0)),
                   pl.BlockSpec((1, HEAD_DIM, n_c), lambda bb: (bb, 0, 0))],
        out_shape=[jax.ShapeDtypeStruct((b, n_c, HEAD_DIM), BF16), jax.ShapeDtypeStruct((b, HEAD_DIM, n_c), BF16)],
        compiler_params=pltpu.CompilerParams(dimension_semantics=("arbitrary",), vmem_limit_bytes=VMEM_LIMIT),
        name="nsa_compress",
    )(chunks, *params)


def _cmp_attn_kernel(q_ref, kc_ref, vct_ref, bias_ref, o_ref, sel_ref, p_ref, *, n_sel):
    i = pl.program_id(0)
    kc = kc_ref[0]
    vct = vct_ref[0]
    n_c = kc.shape[0]
    n_blk = sel_ref.shape[1]
    probs = []

    def scores(h):
        return jnp.dot(kc, q_ref[0, h * HEAD_DIM:(h + 1) * HEAD_DIM, :], preferred_element_type=F32) + bias_ref[h, 0]

    def update(h, s):
        m = jnp.maximum(jnp.max(s, axis=0, keepdims=True), 0.5 * NEG_INF)
        p = jnp.exp2(s - m)
        den = jnp.sum(p, axis=0, keepdims=True)
        p = p * (1.0 / jnp.maximum(den, TINY))
        o_ref[0, h * HEAD_DIM:(h + 1) * HEAD_DIM, :] = jnp.dot(vct, p.astype(BF16),
                                                               preferred_element_type=F32).astype(o_ref.dtype)
        probs.append(p)

    _staggered(HEADS, scores, update)
    psum = (probs[0] + probs[1]) + (probs[2] + probs[3])
    per_blk = SLC_BLOCK // CMP_STRIDE
    halves = []
    for half in range(QT // LANES):
        p_ref[half, 0:8, :] = jnp.zeros((8, LANES), F32)
        p_ref[half, 8:8 + n_c, :] = psum[:, half * LANES:(half + 1) * LANES]
        p_ref[half, 8 + n_c:16 + n_c, :] = jnp.zeros((8, LANES), F32)
        acc = p_ref[half, pl.ds(7, n_blk, stride=per_blk), :]
        for t in range(per_blk):
            acc = acc + p_ref[half, pl.ds(8 + t, n_blk, stride=per_blk), :]
        halves.append(acc)
    imp = jnp.concatenate(halves, axis=1)
    blk = lax.broadcasted_iota(jnp.int32, (n_blk, QT), 0)
    cur = (i * QT + lax.broadcasted_iota(jnp.int32, (n_blk, QT), 1)) // SLC_BLOCK
    forced = (blk == 0) | (blk == cur) | (blk == cur - 1)
    val = jnp.where(forced, FORCE_SELECT, jnp.where(blk <= cur, imp, NEG_INF))
    sel = jnp.zeros((n_blk, QT), jnp.bool_)
    for _ in range(n_sel):
        top = jnp.max(val, axis=0, keepdims=True)
        idx = jnp.min(jnp.where(val == top, blk, n_blk), axis=0, keepdims=True)
        hit = blk == idx
        sel = sel | hit
        val = jnp.where(hit, -3.0e38, val)
    sel_ref[0] = jnp.where(sel, 1.0, 0.0).astype(sel_ref.dtype)


def _cmp_attention(q_t, kc, vct, bias, b):
    nt = q_t.shape[0]
    nq = nt // b
    n_c = kc.shape[1]
    n_blk = nq * QT // SLC_BLOCK
    return pl.pallas_call(
        functools.partial(_cmp_attn_kernel, n_sel=min(SLC_TOPK, n_blk)),
        grid=(nq, b),
        in_specs=[pl.BlockSpec((1, GROUP, QT), lambda i, bb: (bb * nq + i, 0, 0)),
                  pl.BlockSpec((1, n_c, HEAD_DIM), lambda i, bb: (bb, 0, 0)),
                  pl.BlockSpec((1, HEAD_DIM, n_c), lambda i, bb: (bb, 0, 0)),
                  pl.BlockSpec((HEADS, 1, n_c, QT), lambda i, bb: (0, 0, 0, i))],
        out_specs=[pl.BlockSpec((1, GROUP, QT), lambda i, bb: (bb * nq + i, 0, 0)),
                   pl.BlockSpec((1, n_blk, QT), lambda i, bb: (bb * nq + i, 0, 0))],
        out_shape=[jax.ShapeDtypeStruct((nt, GROUP, QT), BF16), jax.ShapeDtypeStruct((nt, n_blk, QT), BF16)],
        scratch_shapes=[pltpu.VMEM((QT // LANES, n_c + 16, LANES), F32)],
        compiler_params=pltpu.CompilerParams(dimension_semantics=("arbitrary", "arbitrary"),
                                             vmem_limit_bytes=VMEM_LIMIT),
        name="nsa_compressed_attention",
    )(q_t, kc, vct, bias)


SEL_REP = 8
N_WIN = -(-(NSA_WINDOW - 1 + KT - 1) // QT)


def _slc_win_kernel(far_ref, q_ref, ksw_ref, vs_ref, vw_ref, sel_ref, rep_ref, ocmp_ref, g_ref, bslc_ref, bwin_ref,
                    o_ref, qz_ref, m_ref, acc_ref, next_ref, ot_ref, mask_ref):
    i = pl.program_id(1)
    sel8 = jnp.dot(rep_ref[...], sel_ref[0], preferred_element_type=F32)
    mask_ref[...] = (sel8 - 1.0) * (-NEG_INF)
    blocks_per_tile = KT // SLC_BLOCK
    mrows = blocks_per_tile * SEL_REP
    zeros = jnp.zeros((HEAD_DIM, QT), BF16)
    for h in range(HEADS):
        qh = q_ref[0, h * HEAD_DIM:(h + 1) * HEAD_DIM, :]
        qz_ref[h] = jnp.concatenate([qh, zeros], axis=0)
        qz_ref[HEADS + h] = jnp.concatenate([zeros, qh], axis=0)
    _flash_reset(m_ref, acc_ref)

    def load_keys(j):
        return ksw_ref[0, pl.ds(pl.multiple_of(j * KT, KT), KT), :]

    def load_tile(j):
        j = jnp.minimum(j, i)
        m8 = mask_ref[pl.ds(pl.multiple_of(j * mrows, mrows), mrows), :]
        mask = jnp.broadcast_to(m8.reshape(blocks_per_tile, 1, SEL_REP, QT),
                                (blocks_per_tile, SLC_BLOCK // SEL_REP, SEL_REP, QT)).reshape(KT, QT)
        return load_keys(j), mask

    def values(j):
        return vs_ref[0, jnp.minimum(j, i)]

    n_far = _whole_far_groups(i, FAR_TILE_GROUP)
    _pipelined_tiles(0, n_far, HEADS, FAR_TILE_GROUP, load_tile,
                     lambda tile, j, h: jnp.dot(tile[0], qz_ref[h], preferred_element_type=F32) + tile[1],
                     lambda j, h, s: _flash_update(h, s, values(j), m_ref, acc_ref, shift=far_ref[h]), next_ref)

    def slc_scores(tile, j, h):
        k, mask = tile
        return jnp.dot(k, qz_ref[h], preferred_element_type=F32) + mask + bslc_ref[h, _bias_tile_index(i, j)]

    _pipelined_tiles(n_far, i + 1 - n_far, HEADS, SLC_TILE_GROUP, load_tile, slc_scores,
                     lambda j, h, s: _flash_update(h, s, values(j), m_ref, acc_ref), next_ref,
                     left_by_previous=lambda j, h: bslc_ref[h, _bias_tile_index(i, j)])

    def win_tile(n):
        d = N_WIN - 1 - n // HEADS
        return d, n % HEADS, jnp.maximum(i - d, 0)

    def win_scores(n):
        d, h, j = win_tile(n)
        missing = jnp.where(i < d, NEG_INF, 0.0).astype(F32)
        return jnp.dot(load_keys(j), qz_ref[HEADS + h], preferred_element_type=F32) + (bwin_ref[h, d] + missing)

    def win_update(n, s):
        _, h, j = win_tile(n)
        _flash_update(HEADS + h, s, vw_ref[0, j], m_ref, acc_ref)

    _staggered(N_WIN * HEADS, win_scores, win_update)

    for h in range(HEADS):
        g = jax.nn.sigmoid(g_ref[0, 3 * h:3 * h + 3, :])
        ot_ref[h * HEAD_DIM:(h + 1) * HEAD_DIM, :] = (g[0:1] * ocmp_ref[0, h * HEAD_DIM:(h + 1) * HEAD_DIM, :]
                                                      + g[1:2] * _flash_result(h, acc_ref)
                                                      + g[2:3] * _flash_result(HEADS + h, acc_ref))
    o_ref[0] = ot_ref[...].T.astype(o_ref.dtype)


def _slc_win_attention(far, q_t, ksw, vs_t, vw_t, sel, rep, ocmp, g_t, bslc, bwin):
    b, s, _ = ksw.shape
    nq = s // QT
    nkv = s // KT
    n_blk = s // SLC_BLOCK
    tile = lambda height: pl.BlockSpec((1, height, QT), lambda bb, i: (bb * nq + i, 0, 0))
    whole = lambda a: pl.BlockSpec(a.shape, lambda bb, i: (0,) * a.ndim)
    return pl.pallas_call(
        _slc_win_kernel,
        grid=(b, nq),
        in_specs=[pl.BlockSpec(memory_space=pltpu.SMEM), tile(GROUP),
                  pl.BlockSpec((1, s, 2 * HEAD_DIM), lambda bb, i: (bb, 0, 0)),
                  pl.BlockSpec((1, nkv, V_ROWS, KT), lambda bb, i: (bb, 0, 0, 0)),
                  pl.BlockSpec((1, nkv, V_ROWS, KT), lambda bb, i: (bb, 0, 0, 0)),
                  tile(n_blk), whole(rep), tile(GROUP), tile(16), whole(bslc), whole(bwin)],
        out_specs=pl.BlockSpec((1, QT, GROUP), lambda bb, i: (bb, i, 0)),
        out_shape=jax.ShapeDtypeStruct((b, s, GROUP), BF16),
        scratch_shapes=[pltpu.VMEM((2 * HEADS, 2 * HEAD_DIM, QT), BF16)] + _flash_scratch(2 * HEADS, MXU_LOOKAHEAD)
        + [pltpu.VMEM((GROUP, QT), F32), pltpu.VMEM((n_blk * SEL_REP, QT), F32)],
        compiler_params=pltpu.CompilerParams(dimension_semantics=("arbitrary", "arbitrary"),
                                             vmem_limit_bytes=VMEM_LIMIT),
        name="nsa_selected_window_attention",
    )(far, q_t, ksw, vs_t.reshape(b, nkv, V_ROWS, KT), vw_t.reshape(b, nkv, V_ROWS, KT), sel, rep, ocmp, g_t,
      bslc, bwin)


def _out_kernel(x_ref, a0_ref, a1_ref, a2_ref, l0_ref, l1_ref, l2_ref, ob_ref, oc_ref, od_ref, gate_ref, e_ref,
                w_ref, o_ref, unfold_ref):
    rows = x_ref.shape[0]

    def unfolded(ref, rate):
        width = ref.shape[1] // rate
        for rho in range(rate):
            for part in range(width // LANES):
                c0 = rho * width + part * LANES
                unfold_ref[part, pl.ds(rho, rows // rate, stride=rate), :] = ref[:, c0:c0 + LANES].astype(F32)
        return jnp.concatenate([unfold_ref[part] for part in range(width // LANES)], axis=1)

    a0, l0 = a0_ref[...], l0_ref[...]
    a1, l1 = unfolded(a1_ref, FOLD_RATES[0]), unfolded(l1_ref, FOLD_RATES[0])
    a2, l2 = unfolded(a2_ref, FOLD_RATES[1]), unfolded(l2_ref, FOLD_RATES[1])
    mx = jnp.maximum(jnp.maximum(l0, l1), l2)
    e0, e1, e2 = jnp.exp(l0 - mx), jnp.exp(l1 - mx), jnp.exp(l2 - mx)
    den = e0 + e1 + e2

    def per_head_lanes(w):
        hi = w.astype(BF16)
        lo = (w - hi.astype(F32)).astype(BF16)
        return (jnp.dot(hi, e_ref[...], preferred_element_type=F32)
                + jnp.dot(lo, e_ref[...], preferred_element_type=F32))

    o_a = per_head_lanes(e0 / den) * a0 + per_head_lanes(e1 / den) * a1 + per_head_lanes(e2 / den) * a2
    y = jnp.concatenate([o_a, ob_ref[...].astype(F32), oc_ref[...].astype(F32), od_ref[...].astype(F32)], axis=1)
    g = gate_ref[...].astype(F32)
    y = y * (g * jax.nn.sigmoid(g))
    o_ref[...] = x_ref[...] + jnp.dot(y.astype(BF16), w_ref[...], preferred_element_type=F32)


def _out_projection(x2, a_outs, a_lses, o_b, o_c, o_d, gate, w_out):
    m, d = x2.shape
    rowblk = lambda width: pl.BlockSpec((PROJ_ROWS, width), lambda i: (i, 0))
    folded = lambda width, rate: pl.BlockSpec((PROJ_ROWS // rate, rate * width), lambda i: (i, 0))
    head_of_lane = np.arange(GROUP) // HEAD_DIM
    expand = jnp.asarray((np.arange(LANES)[:, None] == head_of_lane[None, :]).astype(np.float32), BF16)
    return pl.pallas_call(
        _out_kernel,
        grid=(m // PROJ_ROWS,),
        in_specs=[rowblk(d)] + [folded(GROUP, rate) for rate in (1,) + FOLD_RATES]
        + [folded(LANES, rate) for rate in (1,) + FOLD_RATES] + [rowblk(GROUP)] * 3
        + [rowblk(N_MIXERS * GROUP), pl.BlockSpec((LANES, GROUP), lambda i: (0, 0)),
           pl.BlockSpec((N_MIXERS * GROUP, d), lambda i: (0, 0))],
        out_specs=rowblk(d),
        out_shape=jax.ShapeDtypeStruct((m, d), F32),
        scratch_shapes=[pltpu.VMEM((GROUP // LANES, PROJ_ROWS, LANES), F32)],
        compiler_params=pltpu.CompilerParams(dimension_semantics=("arbitrary",), vmem_limit_bytes=VMEM_LIMIT),
        name="out_projection",
    )(x2, *a_outs, *a_lses, o_b, o_c, o_d, gate, expand, w_out)


def _block_diag_mean(group):
    idx = np.arange(GROUP) // group
    return jnp.asarray((idx[:, None] == idx[None, :]).astype(np.float32) / group, BF16)


def _layer_weights(w_in, qk_gain, qk_gain_diff):
    d = w_in.shape[0]
    sizes = (GROUP,) * 3 + (GROUP, GROUP // 2, GROUP // 2) + (GROUP,) * 3 + (GROUP,) + (HEAD_DIM,) * 6 \
        + (HEADS * 3, N_MIXERS * GROUP)
    offs = np.concatenate([[0], np.cumsum(sizes)])
    col = lambda n: w_in[:, offs[n]:offs[n + 1]]
    (a_q, a_k, a_v, b_q, b_k, b_v, c_q, c_k, c_v, d_q, d_kc, d_vc, d_ks, d_vs, d_kw, d_vw, d_g, gate) = \
        [col(n) for n in range(18)]
    rep_kv = lambda w: jnp.repeat(w.reshape(d, 2, HEAD_DIM), 2, axis=1).reshape(d, GROUP)
    wrm = jnp.concatenate([a_q, a_k, a_v, b_q, rep_kv(b_k), rep_kv(b_v), c_k, d_kc, d_vc, d_ks, d_kw, gate], axis=1)
    wt = jnp.concatenate([c_q, c_v, d_q, d_vs, d_vw, d_g, jnp.zeros((d, 16 - HEADS * 3), w_in.dtype)], axis=1).T
    g = qk_gain
    ones = lambda n: jnp.ones((n,), F32)
    tile4 = lambda v: jnp.tile(v, HEADS)
    scale = HEAD_DIM ** -0.5
    grm = jnp.concatenate([tile4(g[0]) * scale, tile4(g[1]), ones(GROUP), tile4(g[2]) * scale, tile4(g[3]),
                           ones(GROUP), jnp.tile(qk_gain_diff[1], 2 * HEADS), ones(2 * HEAD_DIM), g[6], g[7],
                           ones(N_MIXERS * GROUP)])
    gt = jnp.concatenate([jnp.tile(qk_gain_diff[0], 2 * HEADS) * (DIFF_QK_DIM ** -0.5 * LOG2E), ones(GROUP),
                          tile4(g[4]) * (scale * LOG2E), ones(2 * HEAD_DIM + 16)])
    return wrm.astype(BF16), wt.astype(BF16), grm.reshape(1, -1), gt.reshape(-1, 1)


def _compress_weights(cmp_pos, cmp_w1, cmp_b1, cmp_w2, cmp_b2):
    half = CMP_LEN // 2
    pos = jnp.concatenate([cmp_pos[0], cmp_pos[1]], axis=-1)
    ptop = pos[:half].reshape(1, -1)
    pbot = pos[half:].reshape(1, -1)
    w1 = cmp_w1.reshape(2, CMP_LEN, HEAD_DIM, CMP_HIDDEN)
    zeros = jnp.zeros_like(w1[0])
    w1cat = jnp.concatenate([jnp.concatenate([w1[0], zeros], axis=-1),
                             jnp.concatenate([zeros, w1[1]], axis=-1)], axis=1)
    w1t = w1cat[:half].reshape(half * 2 * HEAD_DIM, 2 * CMP_HIDDEN).astype(BF16)
    w1b = w1cat[half:].reshape(half * 2 * HEAD_DIM, 2 * CMP_HIDDEN).astype(BF16)
    b1 = jnp.concatenate([cmp_b1[0], cmp_b1[1]]).reshape(1, -1)
    return (ptop, pbot, w1t, w1b, b1, cmp_w2[0].astype(BF16), cmp_b2[0].reshape(1, -1),
            cmp_w2[1].T.astype(BF16), cmp_b2[1].reshape(-1, 1))


def kernel(x, rel_bias_table, norm_w, w_in, w_out, qk_gain, qk_gain_diff, attn_sinks, diff_lambda, diff_subln,
           cmp_pos, cmp_w1, cmp_b1, cmp_w2, cmp_b2):
    b, s, d = x.shape
    depth = w_in.shape[0]
    n_c = s // CMP_STRIDE
    n_blk = s // SLC_BLOCK
    assert s % (BAND_TILE * DILATED_CONFIGS[-1][1]) == 0 and s % PROJ_ROWS == 0 and d == N_MIXERS * GROUP

    table = rel_bias_table.astype(F32)
    band_bias = [_build_bias(table, head0=0, n_d=1, rows=BAND_TILE, cols=2 * BAND_TILE, base0=BAND_TILE, dstep=0,
                             rs=1, cs=-1, dscale=rate, max_dist=window // rate) for window, rate in DILATED_CONFIGS]
    swa_bias = _build_bias(table, head0=HEADS, n_d=1, rows=BAND_TILE, cols=2 * BAND_TILE, base0=BAND_TILE, dstep=0,
                           rs=1, cs=-1, max_dist=SWA_WINDOW - 1)
    flash_tiles = dict(rows=KT, cols=QT, base0=0, dstep=QT, rs=-1, cs=1, scale=LOG2E)
    diff_bias = _build_bias(table, head0=2 * HEADS, n_d=N_BIAS_TILES, d_valid=N_NEAR + 1, **flash_tiles)
    slc_bias = _build_bias(table, head0=3 * HEADS, n_d=N_BIAS_TILES, d_valid=N_NEAR + 1, **flash_tiles)
    win_bias = _build_bias(table, head0=3 * HEADS, n_d=N_WIN, max_dist=NSA_WINDOW - 1, **flash_tiles)
    far_bias = table[NUM_BUCKETS - 1] * LOG2E
    cmp_bias = _build_bias(table, head0=3 * HEADS, n_d=1, rows=n_c, cols=s, base0=-(CMP_LEN - 1), dstep=0,
                           rs=-CMP_STRIDE, cs=1, r_valid=n_c - 1, col_tile=4 * QT, scale=LOG2E)
    e64, e32 = _block_diag_mean(HEAD_DIM), _block_diag_mean(DIFF_QK_DIM)
    rep_idx = np.arange(n_blk * SEL_REP) // SEL_REP
    rep = jnp.asarray((rep_idx[:, None] == np.arange(n_blk)[None, :]).astype(np.float32), BF16)
    no_sink = jnp.zeros((HEADS,), F32)

    x2 = x.reshape(b * s, d)
    for layer in range(depth):
        wrm, wt, grm, gt = _layer_weights(w_in[layer], qk_gain[layer], qk_gain_diff[layer])
        (a_q, a_k, a_v, b_q, b_k, b_v, c_k, kvc, ksw, gate, a_q4, a_k4, a_v4, a_q16, a_k16, a_v16,
         c_qt, c_vt, d_qt, d_vst, d_vwt, d_gt) = _project(x2, norm_w[layer].reshape(1, d), wrm, wt, grm, gt, e64, e32)
        seq = lambda t: t.reshape(b, s, t.shape[-1])
        per_batch = lambda t: t.reshape(b, t.shape[0] // b, t.shape[1])
        flat = lambda t: t.reshape(b * t.shape[1], t.shape[2])
        a_in = ((a_q, a_k, a_v), (a_q4, a_k4, a_v4), (a_q16, a_k16, a_v16))
        a_res = [_banded(*map(per_batch, a_in[n]), band_bias[n], no_sink, rate, False)
                 for n, (_, rate) in enumerate(DILATED_CONFIGS)]
        o_b, _ = _banded(seq(b_q), seq(b_k), seq(b_v), swa_bias, attn_sinks[layer].astype(F32), 1, True)
        lambda_init = 0.8 - 0.6 * math.exp(-0.3 * layer)
        o_c = _diff_attention(far_bias[2 * HEADS:3 * HEADS], c_qt, seq(c_k), c_vt, diff_bias, diff_lambda[layer].astype(F32),
                              diff_subln[layer].reshape(HEAD_DIM, 1).astype(F32), lambda_init)
        cw = _compress_weights(cmp_pos[layer], cmp_w1[layer], cmp_b1[layer], cmp_w2[layer], cmp_b2[layer])
        kc, vct = _compress(kvc.reshape(b, n_c, CMP_STRIDE * 2 * HEAD_DIM), *cw, qk_gain[layer, 5].reshape(1, -1))
        o_cmp, sel = _cmp_attention(d_qt, kc, vct, cmp_bias, b)
        o_d = _slc_win_attention(far_bias[3 * HEADS:4 * HEADS], d_qt, seq(ksw), d_vst, d_vwt, sel, rep, o_cmp, d_gt, slc_bias, win_bias)
        x2 = _out_projection(x2, [flat(r[0]) for r in a_res], [flat(r[1]) for r in a_res], o_b.reshape(b * s, GROUP),
                             o_c.reshape(b * s, GROUP), o_d.reshape(b * s, GROUP), gate, w_out[layer].astype(BF16))
    return x2.reshape(b, s, d)
```

```python
import functools
import math

import numpy as np
import jax
import jax.numpy as jnp
from jax import lax
from jax.experimental import pallas as pl
from jax.experimental.pallas import tpu as pltpu

F32 = jnp.float32
BF16 = jnp.bfloat16

HEAD_DIM = 64
HEADS = 4
GROUP = HEADS * HEAD_DIM
N_MIXERS = 4
NUM_BUCKETS = 32
REL_MAX_DIST = 2048
DILATED_CONFIGS = ((128, 1), (512, 4), (2048, 16))
FOLD_RATES = tuple(rate for _, rate in DILATED_CONFIGS if rate > 1)
SWA_WINDOW = 128
DIFF_QK_DIM = HEAD_DIM // 2
CMP_LEN = 32
CMP_STRIDE = 16
CMP_HIDDEN = 256
SLC_BLOCK = 64
SLC_TOPK = 16
NSA_WINDOW = 512
RMS_EPS = 1e-6
NEG_INF = -1e30
FORCE_SELECT = 1e9
TINY = 1e-30
LOG2E = math.log2(math.e)

PROJ_ROWS = 512
BAND_TILE = 128
BAND_STEP = 1024
BAND_LOOKAHEAD = 8
BIAS_ROW_CHUNK = 64
LANES = 128
QT = 256
KT = 256
VMEM_LIMIT = 56 * 1024 * 1024
MXU_LOOKAHEAD = 4
V_ROWS = HEAD_DIM + 16

NT_DIMS = (((1,), (1,)), ((), ()))


def _t5_thresholds():
    n = np.arange(0, 4 * REL_MAX_DIST)
    max_exact = NUM_BUCKETS // 2
    nf = np.maximum(n, 1).astype(np.float32)
    large = max_exact + (np.log(nf / np.float32(max_exact)) / np.float32(math.log(REL_MAX_DIST / max_exact))
                         * np.float32(NUM_BUCKETS - max_exact)).astype(np.int32)
    bucket = np.where(n < max_exact, n, np.minimum(large, NUM_BUCKETS - 1))
    return [int(np.argmax(bucket >= b)) for b in range(NUM_BUCKETS)]


T5_THRESHOLDS = _t5_thresholds()
FAR_DIST = T5_THRESHOLDS[-1]


def _bias_kernel(tbl_ref, out_ref, *, head0, base0, dstep, rs, cs, dscale, max_dist, r_valid, d_valid, col_tile,
                 scale):
    h = pl.program_id(0)
    d = pl.program_id(1)
    ct = pl.program_id(2)
    rows, cols = out_ref.shape[-2:]
    chunk = BIAS_ROW_CHUNK if rows % BIAS_ROW_CHUNK == 0 else rows
    for r0 in range(0, rows, chunk):
        origin = base0 + d * dstep + r0 * rs + ct * col_tile * cs
        corners = [origin + dr * rs + dc * cs for dr in (0, chunk - 1) for dc in (0, cols - 1)]
        lo = functools.reduce(jnp.minimum, corners)
        hi = functools.reduce(jnp.maximum, corners)
        all_masked = (hi < 0) | (lo > max_dist) | (d >= d_valid) | (r0 >= r_valid)
        all_far = (lo * dscale >= FAR_DIST) & (hi <= max_dist) & (d < d_valid) & (r0 + chunk <= r_valid)
        out = out_ref.at[0, 0, r0:r0 + chunk, :]

        @pl.when(all_masked)
        def _():
            out[...] = jnp.full((chunk, cols), NEG_INF, F32)

        @pl.when(all_far)
        def _():
            out[...] = jnp.full((chunk, cols), tbl_ref[NUM_BUCKETS - 1, head0 + h] * scale, F32)

        @pl.when(jnp.logical_not(all_masked | all_far))
        def _():
            r = lax.broadcasted_iota(jnp.int32, (chunk, cols), 0)
            c = lax.broadcasted_iota(jnp.int32, (chunk, cols), 1)
            dist = origin + r * rs + c * cs
            n = dist * dscale
            val = jnp.full((chunk, cols), tbl_ref[0, head0 + h], F32)
            for b in range(1, NUM_BUCKETS):
                val = jnp.where(n >= T5_THRESHOLDS[b], tbl_ref[b, head0 + h], val)
            valid = (dist >= 0) & (dist <= max_dist) & (r + r0 < r_valid) & (d < d_valid)
            out[...] = jnp.where(valid, val * scale, NEG_INF)


def _build_bias(table, *, head0, n_d, rows, cols, base0, dstep, rs, cs, dscale=1, max_dist=1 << 30,
                r_valid=1 << 30, d_valid=1 << 30, col_tile=None, scale=1.0):
    col_tile = cols if col_tile is None else col_tile
    kern = functools.partial(_bias_kernel, head0=head0, base0=base0, dstep=dstep, rs=rs, cs=cs, dscale=dscale,
                             max_dist=max_dist, r_valid=r_valid, d_valid=d_valid, col_tile=col_tile, scale=scale)
    return pl.pallas_call(
        kern,
        grid=(HEADS, n_d, cols // col_tile),
        in_specs=[pl.BlockSpec(memory_space=pltpu.SMEM)],
        out_specs=pl.BlockSpec((1, 1, rows, col_tile), lambda h, d, c: (h, d, 0, c)),
        out_shape=jax.ShapeDtypeStruct((HEADS, n_d, rows, cols), F32),
        name="rel_bias_tiles",
    )(table)


RM_AQ, RM_AK, RM_AV = 0, 256, 512
RM_BQ, RM_BK, RM_BV = 768, 1024, 1280
RM_CK = 1536
RM_KVC = 1792
RM_KSW = 1920
RM_GATE = 2048
RM_COLS = 3072
TR_CQ, TR_CV, TR_DQ, TR_DVS, TR_DVW, TR_DG = 0, 256, 512, 768, 832, 896
TR_ROWS = 912


def _proj_kernel(x_ref, nw_ref, wrm_ref, wt_ref, grm_ref, gt_ref, e64_ref, e32_ref,
                 aq_ref, ak_ref, av_ref, bq_ref, bk_ref, bv_ref, ck_ref, kvc_ref, ksw_ref, gate_ref,
                 aq4_ref, ak4_ref, av4_ref, aq16_ref, ak16_ref, av16_ref,
                 cq_ref, cv_ref, dq_ref, dvs_ref, dvw_ref, dg_ref, fold_ref):
    x = x_ref[...]
    ms = jnp.mean(x * x, axis=-1, keepdims=True)
    xn = (x * lax.rsqrt(ms + RMS_EPS) * nw_ref[...]).astype(BF16)
    rows = x.shape[0]

    def rm(c0, width):
        return jnp.dot(xn, wrm_ref[:, c0:c0 + width], preferred_element_type=F32)

    def rm_normed(c0, width, e_ref):
        h = rm(c0, width)
        msq = jnp.dot((h * h).astype(BF16), e_ref[0:width, 0:width], preferred_element_type=F32)
        return h * lax.rsqrt(msq + RMS_EPS) * grm_ref[:, c0:c0 + width]

    def put_folded(val, ref, folded_refs):
        ref[...] = val.astype(ref.dtype)
        for half in range(GROUP // LANES):
            fold_ref[half] = val[:, half * LANES:(half + 1) * LANES]
        for rate, fref in zip(FOLD_RATES, folded_refs):
            for rho in range(rate):
                for half in range(GROUP // LANES):
                    c0 = rho * GROUP + half * LANES
                    fref[:, c0:c0 + LANES] = fold_ref[half, pl.ds(rho, rows // rate, stride=rate), :].astype(fref.dtype)

    put_folded(rm_normed(RM_AQ, GROUP, e64_ref), aq_ref, (aq4_ref, aq16_ref))
    put_folded(rm_normed(RM_AK, GROUP, e64_ref), ak_ref, (ak4_ref, ak16_ref))
    put_folded(rm(RM_AV, GROUP), av_ref, (av4_ref, av16_ref))
    bq_ref[...] = rm_normed(RM_BQ, GROUP, e64_ref).astype(bq_ref.dtype)
    bk_ref[...] = rm_normed(RM_BK, GROUP, e64_ref).astype(bk_ref.dtype)
    bv_ref[...] = rm(RM_BV, GROUP).astype(bv_ref.dtype)
    ck_ref[...] = rm_normed(RM_CK, GROUP, e32_ref).astype(ck_ref.dtype)
    kvc_ref[...] = rm(RM_KVC, 2 * HEAD_DIM).astype(kvc_ref.dtype)
    ksw_ref[...] = rm_normed(RM_KSW, 2 * HEAD_DIM, e64_ref).astype(ksw_ref.dtype)
    gate_ref[...] = rm(RM_GATE, N_MIXERS * GROUP).astype(gate_ref.dtype)

    def tr(r0, height):
        return lax.dot_general(wt_ref[r0:r0 + height, :], xn, NT_DIMS, preferred_element_type=F32)

    def tr_normed(r0, height, group):
        h3 = tr(r0, height).reshape(height // group, group, rows)
        msq = jnp.mean(h3 * h3, axis=1, keepdims=True)
        return (h3 * lax.rsqrt(msq + RMS_EPS)).reshape(height, rows) * gt_ref[r0:r0 + height, :]

    def put(ref, val):
        for t in range(rows // QT):
            ref[t] = val[:, t * QT:(t + 1) * QT].astype(ref.dtype)

    def with_ones(v):
        ones = jnp.ones((V_ROWS - HEAD_DIM, rows), F32)
        parts = []
        for h in range(v.shape[0] // HEAD_DIM):
            parts += [v[h * HEAD_DIM:(h + 1) * HEAD_DIM], ones]
        return jnp.concatenate(parts, axis=0)

    put(cq_ref, tr_normed(TR_CQ, GROUP, DIFF_QK_DIM))
    put(cv_ref, with_ones(tr(TR_CV, GROUP)))
    put(dq_ref, tr_normed(TR_DQ, GROUP, HEAD_DIM))
    put(dvs_ref, with_ones(tr(TR_DVS, HEAD_DIM)))
    put(dvw_ref, with_ones(tr(TR_DVW, HEAD_DIM)))
    put(dg_ref, tr(TR_DG, 16))


def _project(x2, nw, wrm, wt, grm, gt, e64, e32):
    m, d = x2.shape
    nt = m // QT
    tpr = PROJ_ROWS // QT
    const = lambda shape: pl.BlockSpec(shape, lambda i: (0,) * len(shape))
    rm_out = lambda width, dtype: (jax.ShapeDtypeStruct((m, width), dtype),
                                   pl.BlockSpec((PROJ_ROWS, width), lambda i: (i, 0)))
    tr_out = lambda height, dtype: (jax.ShapeDtypeStruct((nt, height, QT), dtype),
                                    pl.BlockSpec((tpr, height, QT), lambda i: (i, 0, 0)))
    outs = [rm_out(GROUP, BF16)] * 7 + [rm_out(2 * HEAD_DIM, F32), rm_out(2 * HEAD_DIM, BF16),
                                        rm_out(N_MIXERS * GROUP, BF16)]
    fold_out = lambda rate: (jax.ShapeDtypeStruct((m // rate, rate * GROUP), BF16),
                             pl.BlockSpec((PROJ_ROWS // rate, rate * GROUP), lambda i: (i, 0)))
    outs += [fold_out(rate) for rate in FOLD_RATES for _ in range(3)]
    outs += [tr_out(GROUP, BF16), tr_out(HEADS * V_ROWS, BF16), tr_out(GROUP, BF16), tr_out(V_ROWS, BF16),
             tr_out(V_ROWS, BF16), tr_out(16, F32)]
    return pl.pallas_call(
        _proj_kernel,
        grid=(m // PROJ_ROWS,),
        in_specs=[pl.BlockSpec((PROJ_ROWS, d), lambda i: (i, 0)), const((1, d)), const((d, RM_COLS)),
                  const((TR_ROWS, d)), const((1, RM_COLS)), const((TR_ROWS, 1)), const((GROUP, GROUP)),
                  const((GROUP, GROUP))],
        out_specs=[o[1] for o in outs],
        out_shape=[o[0] for o in outs],
        scratch_shapes=[pltpu.VMEM((GROUP // LANES, PROJ_ROWS, LANES), F32)],
        compiler_params=pltpu.CompilerParams(dimension_semantics=("arbitrary",), vmem_limit_bytes=VMEM_LIMIT),
        name="in_projection",
    )(x2, nw, wrm, wt, grm, gt, e64, e32)


def _band_kernel(sink_ref, q_ref, kp_ref, kc_ref, vp_ref, vc_ref, bias_ref, o_ref, lse_ref, p_ref, *, use_sink):
    i = pl.program_id(2)
    n_blocks = q_ref.shape[1] // BAND_TILE
    head_q = lax.broadcasted_iota(jnp.int32, (BAND_TILE, GROUP), 1) // HEAD_DIM
    head_v = lax.broadcasted_iota(jnp.int32, (2 * BAND_TILE, GROUP), 1) // HEAD_DIM
    lane = lax.broadcasted_iota(jnp.int32, (BAND_TILE, LANES), 1)
    in_prev = lax.broadcasted_iota(jnp.int32, (1, 2 * BAND_TILE), 1) < BAND_TILE
    no_prev = jnp.where(in_prev & (i == 0), NEG_INF, 0.0).astype(F32)

    def window(cur_ref, prev_ref, m):
        if m == 0:
            return jnp.concatenate([prev_ref[0], cur_ref[0, 0:BAND_TILE, :]], axis=0)
        return cur_ref[0, (m - 1) * BAND_TILE:(m + 1) * BAND_TILE, :]

    def scores(n):
        m, h = divmod(n, HEADS)
        q = q_ref[0, m * BAND_TILE:(m + 1) * BAND_TILE, :]
        qh = jnp.where(head_q == h, q, jnp.zeros_like(q))
        bias = bias_ref[h, 0] + no_prev if m == 0 else bias_ref[h, 0]
        return lax.dot_general(qh, window(kc_ref, kp_ref, m), NT_DIMS, preferred_element_type=F32) + bias

    lse_tiles = {}

    def update(n, s):
        m, h = divmod(n, HEADS)
        mx = jnp.max(s, axis=1, keepdims=True)
        if use_sink:
            mx = jnp.maximum(mx, sink_ref[h])
        p = jnp.exp(s - mx)
        den = jnp.sum(p, axis=1, keepdims=True)
        if use_sink:
            den = den + jnp.exp(sink_ref[h] - mx)
        p_ref[m % 2, :, h * 2 * BAND_TILE:(h + 1) * 2 * BAND_TILE] = (p * (1.0 / den)).astype(BF16)
        lse_tiles[m] = jnp.where(lane == h, mx + jnp.log(den), lse_tiles.get(m, jnp.zeros((BAND_TILE, LANES), F32)))
        if h == HEADS - 1:
            v = window(vc_ref, vp_ref, m)
            v_heads = jnp.concatenate([jnp.where(head_v == hh, v, jnp.zeros_like(v)) for hh in range(HEADS)], axis=0)
            rows = slice(m * BAND_TILE, (m + 1) * BAND_TILE)
            o_ref[0, rows, :] = jnp.dot(p_ref[m % 2], v_heads, preferred_element_type=F32).astype(o_ref.dtype)
            lse_ref[0, rows, :] = lse_tiles.pop(m)

    _staggered(n_blocks * HEADS, scores, update, ahead=BAND_LOOKAHEAD)


def _banded(q, k, v, bias, sink, rate, use_sink):
    b, ln, _ = q.shape
    step = min(BAND_STEP, ln)
    per_step = step // BAND_TILE
    cur = pl.BlockSpec((1, step, GROUP), lambda bb, r, i: (bb, i, r))
    prev = pl.BlockSpec((1, BAND_TILE, GROUP), lambda bb, r, i: (bb, jnp.maximum(i * per_step - 1, 0), r))
    o, lse = pl.pallas_call(
        functools.partial(_band_kernel, use_sink=use_sink),
        grid=(b, rate, ln // step),
        in_specs=[pl.BlockSpec(memory_space=pltpu.SMEM), cur, prev, cur, prev, cur,
                  pl.BlockSpec((HEADS, 1, BAND_TILE, 2 * BAND_TILE), lambda bb, r, i: (0, 0, 0, 0))],
        out_specs=[cur, pl.BlockSpec((1, step, LANES), lambda bb, r, i: (bb, i, r))],
        out_shape=[jax.ShapeDtypeStruct((b, ln, rate * GROUP), BF16), jax.ShapeDtypeStruct((b, ln, rate * LANES), F32)],
        scratch_shapes=[pltpu.VMEM((2, BAND_TILE, HEADS * 2 * BAND_TILE), BF16)],
        compiler_params=pltpu.CompilerParams(dimension_semantics=("arbitrary",) * 3),
        name=f"banded_attention_r{rate}",
    )(sink, q, k, k, v, v, bias)
    return o, lse


def _flash_reset(m_ref, acc_ref):
    m_ref[...] = jnp.full(m_ref.shape, NEG_INF, F32)
    acc_ref[...] = jnp.zeros(acc_ref.shape, F32)


def _flash_update(n, s, v_t, m_ref, acc_ref, shift=None):
    m_old = m_ref[n]
    if shift is None:
        m_new = jnp.maximum(m_old, jnp.max(s, axis=0, keepdims=True))
        p = jnp.exp2(s - m_new)
    else:
        m_new = jnp.maximum(m_old, jnp.max(s, axis=0, keepdims=True) + shift)
        p = jnp.exp2(s - (m_new - shift))
    alpha = jnp.exp2(m_old - m_new)
    acc_ref[n] = alpha * acc_ref[n] + jnp.dot(v_t, p.astype(BF16), preferred_element_type=F32)
    m_ref[n] = m_new


def _flash_result(n, acc_ref):
    return acc_ref[n, 0:HEAD_DIM, :] / acc_ref[n, HEAD_DIM:HEAD_DIM + 1, :]


def _staggered(n_items, scores, update, ahead=MXU_LOOKAHEAD):
    pending = {n: scores(n) for n in range(min(ahead, n_items))}
    for n in range(n_items):
        if n + ahead < n_items:
            pending[n + ahead] = scores(n + ahead)
        update(n, pending.pop(n))


def _pipelined_tiles(first, n_tiles, n_chains, group, load_tile, scores, update, next_ref, left_by_previous=None,
                     last_of_sweep=False):
    ahead = next_ref.shape[0]
    n_items = group * n_chains
    assert ahead <= n_chains

    def body(trip, _, issue_next=True):
        base = first + trip * group
        tiles, pending = {}, {}
        for n in range(n_items):
            cur = next_ref[n] if n < ahead else pending.pop(n)
            if n + ahead < n_items or issue_next:
                g, c = divmod(n + ahead, n_chains)
                if g not in tiles:
                    tiles[g] = load_tile(base + g)
                new = scores(tiles[g], base + g, c)
                if n + ahead < n_items:
                    pending[n + ahead] = new
                else:
                    next_ref[n + ahead - n_items] = new
            update(base + n // n_chains, n % n_chains, cur)

    if left_by_previous is None:
        first_tile = load_tile(first)
        for n in range(ahead):
            next_ref[n] = scores(first_tile, first, n)
    else:
        for n in range(ahead):
            next_ref[n] = next_ref[n] + left_by_previous(first, n)
    n_trips = (n_tiles + group - 1) // group
    if last_of_sweep:
        lax.fori_loop(0, n_trips - 1, body, None)
        body(n_trips - 1, None, issue_next=False)
    else:
        lax.fori_loop(0, n_trips, body, None)


def _flash_scratch(chains, ahead):
    return [pltpu.VMEM((chains, 1, QT), F32), pltpu.VMEM((chains, V_ROWS, QT), F32),
            pltpu.VMEM((ahead, KT, QT), F32)]


N_NEAR = -(-(FAR_DIST + KT - 1) // QT)
N_BIAS_TILES = N_NEAR + 2
DIFF_TILE_GROUP = 2
SLC_TILE_GROUP = 2
FAR_TILE_GROUP = 4


def _bias_tile_index(i, j):
    return jnp.where(j > i, N_NEAR + 1, jnp.minimum(i - j, N_NEAR))


def _whole_far_groups(i, group):
    return jnp.maximum(i - (N_NEAR - 1), 0) // group * group


def _diff_kernel(far_ref, q_ref, k_ref, v_ref, bias_ref, lam_ref, subln_ref, o_ref, qz_ref, m_ref, acc_ref, next_ref,
                 ot_ref, *, lambda_init):
    i = pl.program_id(1)
    q = q_ref[0]
    row = lax.broadcasted_iota(jnp.int32, (GROUP, QT), 0) // DIFF_QK_DIM
    for n in range(2 * HEADS):
        qz_ref[n] = jnp.where(row == n, q, jnp.zeros_like(q))
    _flash_reset(m_ref, acc_ref)

    def load_tile(j):
        return k_ref[0, pl.ds(pl.multiple_of(jnp.minimum(j, i) * KT, KT), KT), :]

    def values(j, n):
        h = n // 2
        return v_ref[0, jnp.minimum(j, i), h * V_ROWS:(h + 1) * V_ROWS, :]

    n_far = _whole_far_groups(i, FAR_TILE_GROUP)
    _pipelined_tiles(0, n_far, 2 * HEADS, FAR_TILE_GROUP, load_tile,
                     lambda k, j, n: jnp.dot(k, qz_ref[n], preferred_element_type=F32),
                     lambda j, n, s: _flash_update(n, s, values(j, n), m_ref, acc_ref, shift=far_ref[n // 2]),
                     next_ref)

    def scores(k, j, n):
        return jnp.dot(k, qz_ref[n], preferred_element_type=F32) + bias_ref[n // 2, _bias_tile_index(i, j)]

    _pipelined_tiles(n_far, i + 1 - n_far, 2 * HEADS, DIFF_TILE_GROUP, load_tile, scores,
                     lambda j, n, s: _flash_update(n, s, values(j, n), m_ref, acc_ref), next_ref,
                     left_by_previous=lambda j, n: bias_ref[n // 2, _bias_tile_index(i, j)], last_of_sweep=True)

    lam_p = lam_ref[...]
    lam = (jnp.exp(jnp.sum(lam_p[0:1] * lam_p[1:2], axis=1, keepdims=True))
           - jnp.exp(jnp.sum(lam_p[2:3] * lam_p[3:4], axis=1, keepdims=True)) + lambda_init)
    for h in range(HEADS):
        o = _flash_result(2 * h, acc_ref) - lam * _flash_result(2 * h + 1, acc_ref)
        msq = jnp.mean(o * o, axis=0, keepdims=True)
        ot_ref[h * HEAD_DIM:(h + 1) * HEAD_DIM, :] = (o * lax.rsqrt(msq + RMS_EPS) * subln_ref[...]
                                                      * (1.0 - lambda_init))
    o_ref[0] = ot_ref[...].T.astype(o_ref.dtype)


def _diff_attention(far, q_t, k, v_t, bias, lam_p, subln, lambda_init):
    b, s, _ = k.shape
    nq = s // QT
    nkv = s // KT
    v4 = v_t.reshape(b, nkv, HEADS * V_ROWS, KT)
    return pl.pallas_call(
        functools.partial(_diff_kernel, lambda_init=lambda_init),
        grid=(b, nq),
        in_specs=[pl.BlockSpec(memory_space=pltpu.SMEM),
                  pl.BlockSpec((1, GROUP, QT), lambda bb, i: (bb * nq + i, 0, 0)),
                  pl.BlockSpec((1, s, GROUP), lambda bb, i: (bb, 0, 0)),
                  pl.BlockSpec((1, nkv, HEADS * V_ROWS, KT), lambda bb, i: (bb, 0, 0, 0)),
                  pl.BlockSpec((HEADS, N_BIAS_TILES, KT, QT), lambda bb, i: (0, 0, 0, 0)),
                  pl.BlockSpec((4, DIFF_QK_DIM), lambda bb, i: (0, 0)),
                  pl.BlockSpec((HEAD_DIM, 1), lambda bb, i: (0, 0))],
        out_specs=pl.BlockSpec((1, QT, GROUP), lambda bb, i: (bb, i, 0)),
        out_shape=jax.ShapeDtypeStruct((b, s, GROUP), BF16),
        scratch_shapes=[pltpu.VMEM((2 * HEADS, GROUP, QT), BF16)] + _flash_scratch(2 * HEADS, MXU_LOOKAHEAD)
        + [pltpu.VMEM((GROUP, QT), F32)],
        compiler_params=pltpu.CompilerParams(dimension_semantics=("arbitrary", "arbitrary"),
                                             vmem_limit_bytes=VMEM_LIMIT),
        name="diff_attention",
    )(far, q_t, k, v4, bias, lam_p, subln)


def _compress_kernel(ch_ref, ptop_ref, pbot_ref, w1t_ref, w1b_ref, b1_ref, w2k_ref, b2k_ref, w2v_ref, b2v_ref,
                     gk_ref, kc_ref, vct_ref):
    ch = ch_ref[0]
    n_c = ch.shape[0]
    u = jnp.dot((ch + ptop_ref[...]).astype(BF16), w1t_ref[...], preferred_element_type=F32)
    v = jnp.dot((ch + pbot_ref[...]).astype(BF16), w1b_ref[...], preferred_element_type=F32)
    v_next = pltpu.roll(v, n_c - 1, 0)
    hid = jax.nn.gelu(u + v_next + b1_ref[...])
    hk = hid[:, :CMP_HIDDEN].astype(BF16)
    hv = hid[:, CMP_HIDDEN:].astype(BF16)
    kc = jnp.dot(hk, w2k_ref[...], preferred_element_type=F32) + b2k_ref[...]
    msq = jnp.mean(kc * kc, axis=-1, keepdims=True)
    kc_ref[0] = (kc * lax.rsqrt(msq + RMS_EPS) * gk_ref[...]).astype(kc_ref.dtype)
    vct = lax.dot_general(w2v_ref[...], hv, NT_DIMS, preferred_element_type=F32) + b2v_ref[...]
    vct_ref[0] = vct.astype(vct_ref.dtype)


def _compress(chunks, ptop, pbot, w1t, w1b, b1, w2k, b2k, w2v, b2v, gk):
    b, n_c, width = chunks.shape
    const = lambda a: pl.BlockSpec(a.shape, lambda bb: (0,) * a.ndim)
    params = (ptop, pbot, w1t, w1b, b1, w2k, b2k, w2v, b2v, gk)
    return pl.pallas_call(
        _compress_kernel,
        grid=(b,),
        in_specs=[pl.BlockSpec((1, n_c, width), lambda bb: (bb, 0, 0))] + [const(a) for a in params],
        out_specs=[pl.BlockSpec((1, n_c, HEAD_DIM), lambda bb: (bb, 0, 0)),
                   pl.BlockSpec((1, HEAD_DIM, n_c), lambda bb: (bb, 0, 0))],
        out_shape=[jax.ShapeDtypeStruct((b, n_c, HEAD_DIM), BF16), jax.ShapeDtypeStruct((b, HEAD_DIM, n_c), BF16)],
        compiler_params=pltpu.CompilerParams(dimension_semantics=("arbitrary",), vmem_limit_bytes=VMEM_LIMIT),
        name="nsa_compress",
    )(chunks, *params)


def _cmp_attn_kernel(q_ref, kc_ref, vct_ref, bias_ref, o_ref, sel_ref, p_ref, *, n_sel):
    i = pl.program_id(0)
    kc = kc_ref[0]
    vct = vct_ref[0]
    n_c = kc.shape[0]
    n_blk = sel_ref.shape[1]
    probs = []

    def scores(h):
        return jnp.dot(kc, q_ref[0, h * HEAD_DIM:(h + 1) * HEAD_DIM, :], preferred_element_type=F32) + bias_ref[h, 0]

    def update(h, s):
        m = jnp.maximum(jnp.max(s, axis=0, keepdims=True), 0.5 * NEG_INF)
        p = jnp.exp2(s - m)
        den = jnp.sum(p, axis=0, keepdims=True)
        p = p * (1.0 / jnp.maximum(den, TINY))
        o_ref[0, h * HEAD_DIM:(h + 1) * HEAD_DIM, :] = jnp.dot(vct, p.astype(BF16),
                                                               preferred_element_type=F32).astype(o_ref.dtype)
        probs.append(p)

    _staggered(HEADS, scores, update)
    psum = (probs[0] + probs[1]) + (probs[2] + probs[3])
    per_blk = SLC_BLOCK // CMP_STRIDE
    halves = []
    for half in range(QT // LANES):
        p_ref[half, 0:8, :] = jnp.zeros((8, LANES), F32)
        p_ref[half, 8:8 + n_c, :] = psum[:, half * LANES:(half + 1) * LANES]
        p_ref[half, 8 + n_c:16 + n_c, :] = jnp.zeros((8, LANES), F32)
        acc = p_ref[half, pl.ds(7, n_blk, stride=per_blk), :]
        for t in range(per_blk):
            acc = acc + p_ref[half, pl.ds(8 + t, n_blk, stride=per_blk), :]
        halves.append(acc)
    imp = jnp.concatenate(halves, axis=1)
    blk = lax.broadcasted_iota(jnp.int32, (n_blk, QT), 0)
    cur = (i * QT + lax.broadcasted_iota(jnp.int32, (n_blk, QT), 1)) // SLC_BLOCK
    forced = (blk == 0) | (blk == cur) | (blk == cur - 1)
    val = jnp.where(forced, FORCE_SELECT, jnp.where(blk <= cur, imp, NEG_INF))
    sel = jnp.zeros((n_blk, QT), jnp.bool_)
    for _ in range(n_sel):
        top = jnp.max(val, axis=0, keepdims=True)
        idx = jnp.min(jnp.where(val == top, blk, n_blk), axis=0, keepdims=True)
        hit = blk == idx
        sel = sel | hit
        val = jnp.where(hit, -3.0e38, val)
    sel_ref[0] = jnp.where(sel, 1.0, 0.0).astype(sel_ref.dtype)


def _cmp_attention(q_t, kc, vct, bias, b):
    nt = q_t.shape[0]
    nq = nt // b
    n_c = kc.shape[1]
    n_blk = nq * QT // SLC_BLOCK
    return pl.pallas_call(
        functools.partial(_cmp_attn_kernel, n_sel=min(SLC_TOPK, n_blk)),
        grid=(nq, b),
        in_specs=[pl.BlockSpec((1, GROUP, QT), lambda i, bb: (bb * nq + i, 0, 0)),
                  pl.BlockSpec((1, n_c, HEAD_DIM), lambda i, bb: (bb, 0, 0)),
                  pl.BlockSpec((1, HEAD_DIM, n_c), lambda i, bb: (bb, 0, 0)),
                  pl.BlockSpec((HEADS, 1, n_c, QT), lambda i, bb: (0, 0, 0, i))],
        out_specs=[pl.BlockSpec((1, GROUP, QT), lambda i, bb: (bb * nq + i, 0, 0)),
                   pl.BlockSpec((1, n_blk, QT), lambda i, bb: (bb * nq + i, 0, 0))],
        out_shape=[jax.ShapeDtypeStruct((nt, GROUP, QT), BF16), jax.ShapeDtypeStruct((nt, n_blk, QT), BF16)],
        scratch_shapes=[pltpu.VMEM((QT // LANES, n_c + 16, LANES), F32)],
        compiler_params=pltpu.CompilerParams(dimension_semantics=("arbitrary", "arbitrary"),
                                             vmem_limit_bytes=VMEM_LIMIT),
        name="nsa_compressed_attention",
    )(q_t, kc, vct, bias)


SEL_REP = 8
MASK_CHUNK = 128
N_WIN = -(-(NSA_WINDOW - 1 + KT - 1) // QT)


def _slc_win_kernel(far_ref, q_ref, ksw_ref, vs_ref, vw_ref, sel_ref, rep_ref, ocmp_ref, g_ref, bslc_ref, bwin_ref,
                    o_ref, qz_ref, m_ref, acc_ref, next_ref, ot_ref, mask_ref):
    i = pl.program_id(1)
    blocks_per_tile = KT // SLC_BLOCK
    mrows = blocks_per_tile * SEL_REP

    def build_mask(c, _):
        rows = pl.ds(pl.multiple_of(c * MASK_CHUNK, MASK_CHUNK), MASK_CHUNK)
        sel8 = jnp.dot(rep_ref[rows, :], sel_ref[0], preferred_element_type=F32)
        mask_ref[rows, :] = (sel8 - 1.0) * (-NEG_INF)

    lax.fori_loop(0, ((i + 1) * mrows + MASK_CHUNK - 1) // MASK_CHUNK, build_mask, None)
    zeros = jnp.zeros((HEAD_DIM, QT), BF16)
    for h in range(HEADS):
        qh = q_ref[0, h * HEAD_DIM:(h + 1) * HEAD_DIM, :]
        qz_ref[h] = jnp.concatenate([qh, zeros], axis=0)
        qz_ref[HEADS + h] = jnp.concatenate([zeros, qh], axis=0)
    _flash_reset(m_ref, acc_ref)

    def load_keys(j):
        return ksw_ref[0, pl.ds(pl.multiple_of(j * KT, KT), KT), :]

    def load_tile(j):
        j = jnp.minimum(j, i)
        m8 = mask_ref[pl.ds(pl.multiple_of(j * mrows, mrows), mrows), :]
        mask = jnp.broadcast_to(m8.reshape(blocks_per_tile, 1, SEL_REP, QT),
                                (blocks_per_tile, SLC_BLOCK // SEL_REP, SEL_REP, QT)).reshape(KT, QT)
        return load_keys(j), mask

    def values(j):
        return vs_ref[0, jnp.minimum(j, i)]

    n_far = _whole_far_groups(i, FAR_TILE_GROUP)
    _pipelined_tiles(0, n_far, HEADS, FAR_TILE_GROUP, load_tile,
                     lambda tile, j, h: jnp.dot(tile[0], qz_ref[h], preferred_element_type=F32) + tile[1],
                     lambda j, h, s: _flash_update(h, s, values(j), m_ref, acc_ref, shift=far_ref[h]), next_ref)

    def slc_scores(tile, j, h):
        k, mask = tile
        return jnp.dot(k, qz_ref[h], preferred_element_type=F32) + mask + bslc_ref[h, _bias_tile_index(i, j)]

    _pipelined_tiles(n_far, i + 1 - n_far, HEADS, SLC_TILE_GROUP, load_tile, slc_scores,
                     lambda j, h, s: _flash_update(h, s, values(j), m_ref, acc_ref), next_ref,
                     left_by_previous=lambda j, h: bslc_ref[h, _bias_tile_index(i, j)], last_of_sweep=True)

    def win_tile(n):
        d = N_WIN - 1 - n // HEADS
        return d, n % HEADS, jnp.maximum(i - d, 0)

    def win_scores(n):
        d, h, j = win_tile(n)
        missing = jnp.where(i < d, NEG_INF, 0.0).astype(F32)
        return jnp.dot(load_keys(j), qz_ref[HEADS + h], preferred_element_type=F32) + (bwin_ref[h, d] + missing)

    def win_update(n, s):
        _, h, j = win_tile(n)
        _flash_update(HEADS + h, s, vw_ref[0, j], m_ref, acc_ref)

    _staggered(N_WIN * HEADS, win_scores, win_update)

    for h in range(HEADS):
        g = jax.nn.sigmoid(g_ref[0, 3 * h:3 * h + 3, :])
        ot_ref[h * HEAD_DIM:(h + 1) * HEAD_DIM, :] = (g[0:1] * ocmp_ref[0, h * HEAD_DIM:(h + 1) * HEAD_DIM, :]
                                                      + g[1:2] * _flash_result(h, acc_ref)
                                                      + g[2:3] * _flash_result(HEADS + h, acc_ref))
    o_ref[0] = ot_ref[...].T.astype(o_ref.dtype)


def _slc_win_attention(far, q_t, ksw, vs_t, vw_t, sel, rep, ocmp, g_t, bslc, bwin):
    b, s, _ = ksw.shape
    nq = s // QT
    nkv = s // KT
    n_blk = s // SLC_BLOCK
    tile = lambda height: pl.BlockSpec((1, height, QT), lambda bb, i: (bb * nq + i, 0, 0))
    whole = lambda a: pl.BlockSpec(a.shape, lambda bb, i: (0,) * a.ndim)
    return pl.pallas_call(
        _slc_win_kernel,
        grid=(b, nq),
        in_specs=[pl.BlockSpec(memory_space=pltpu.SMEM), tile(GROUP),
                  pl.BlockSpec((1, s, 2 * HEAD_DIM), lambda bb, i: (bb, 0, 0)),
                  pl.BlockSpec((1, nkv, V_ROWS, KT), lambda bb, i: (bb, 0, 0, 0)),
                  pl.BlockSpec((1, nkv, V_ROWS, KT), lambda bb, i: (bb, 0, 0, 0)),
                  tile(n_blk), whole(rep), tile(GROUP), tile(16), whole(bslc), whole(bwin)],
        out_specs=pl.BlockSpec((1, QT, GROUP), lambda bb, i: (bb, i, 0)),
        out_shape=jax.ShapeDtypeStruct((b, s, GROUP), BF16),
        scratch_shapes=[pltpu.VMEM((2 * HEADS, 2 * HEAD_DIM, QT), BF16)] + _flash_scratch(2 * HEADS, MXU_LOOKAHEAD)
        + [pltpu.VMEM((GROUP, QT), F32), pltpu.VMEM((n_blk * SEL_REP, QT), F32)],
        compiler_params=pltpu.CompilerParams(dimension_semantics=("arbitrary", "arbitrary"),
                                             vmem_limit_bytes=VMEM_LIMIT),
        name="nsa_selected_window_attention",
    )(far, q_t, ksw, vs_t.reshape(b, nkv, V_ROWS, KT), vw_t.reshape(b, nkv, V_ROWS, KT), sel, rep, ocmp, g_t,
      bslc, bwin)


def _out_kernel(x_ref, a0_ref, a1_ref, a2_ref, l0_ref, l1_ref, l2_ref, ob_ref, oc_ref, od_ref, gate_ref, e_ref,
                w_ref, o_ref, unfold_ref):
    rows = x_ref.shape[0]

    def unfolded(ref, rate):
        width = ref.shape[1] // rate
        for rho in range(rate):
            for part in range(width // LANES):
                c0 = rho * width + part * LANES
                unfold_ref[part, pl.ds(rho, rows // rate, stride=rate), :] = ref[:, c0:c0 + LANES].astype(F32)
        return jnp.concatenate([unfold_ref[part] for part in range(width // LANES)], axis=1)

    a0, l0 = a0_ref[...], l0_ref[...]
    a1, l1 = unfolded(a1_ref, FOLD_RATES[0]), unfolded(l1_ref, FOLD_RATES[0])
    a2, l2 = unfolded(a2_ref, FOLD_RATES[1]), unfolded(l2_ref, FOLD_RATES[1])
    mx = jnp.maximum(jnp.maximum(l0, l1), l2)
    e0, e1, e2 = jnp.exp(l0 - mx), jnp.exp(l1 - mx), jnp.exp(l2 - mx)
    den = e0 + e1 + e2

    def per_head_lanes(w):
        hi = w.astype(BF16)
        lo = (w - hi.astype(F32)).astype(BF16)
        return (jnp.dot(hi, e_ref[...], preferred_element_type=F32)
                + jnp.dot(lo, e_ref[...], preferred_element_type=F32))

    o_a = per_head_lanes(e0 / den) * a0 + per_head_lanes(e1 / den) * a1 + per_head_lanes(e2 / den) * a2
    y = jnp.concatenate([o_a, ob_ref[...].astype(F32), oc_ref[...].astype(F32), od_ref[...].astype(F32)], axis=1)
    g = gate_ref[...].astype(F32)
    y = y * (g * jax.nn.sigmoid(g))
    o_ref[...] = x_ref[...] + jnp.dot(y.astype(BF16), w_ref[...], preferred_element_type=F32)


def _out_projection(x2, a_outs, a_lses, o_b, o_c, o_d, gate, w_out):
    m, d = x2.shape
    rowblk = lambda width: pl.BlockSpec((PROJ_ROWS, width), lambda i: (i, 0))
    folded = lambda width, rate: pl.BlockSpec((PROJ_ROWS // rate, rate * width), lambda i: (i, 0))
    head_of_lane = np.arange(GROUP) // HEAD_DIM
    expand = jnp.asarray((np.arange(LANES)[:, None] == head_of_lane[None, :]).astype(np.float32), BF16)
    return pl.pallas_call(
        _out_kernel,
        grid=(m // PROJ_ROWS,),
        in_specs=[rowblk(d)] + [folded(GROUP, rate) for rate in (1,) + FOLD_RATES]
        + [folded(LANES, rate) for rate in (1,) + FOLD_RATES] + [rowblk(GROUP)] * 3
        + [rowblk(N_MIXERS * GROUP), pl.BlockSpec((LANES, GROUP), lambda i: (0, 0)),
           pl.BlockSpec((N_MIXERS * GROUP, d), lambda i: (0, 0))],
        out_specs=rowblk(d),
        out_shape=jax.ShapeDtypeStruct((m, d), F32),
        scratch_shapes=[pltpu.VMEM((GROUP // LANES, PROJ_ROWS, LANES), F32)],
        compiler_params=pltpu.CompilerParams(dimension_semantics=("arbitrary",), vmem_limit_bytes=VMEM_LIMIT),
        name="out_projection",
    )(x2, *a_outs, *a_lses, o_b, o_c, o_d, gate, expand, w_out)


def _block_diag_mean(group):
    idx = np.arange(GROUP) // group
    return jnp.asarray((idx[:, None] == idx[None, :]).astype(np.float32) / group, BF16)


def _layer_weights(w_in, qk_gain, qk_gain_diff):
    d = w_in.shape[0]
    sizes = (GROUP,) * 3 + (GROUP, GROUP // 2, GROUP // 2) + (GROUP,) * 3 + (GROUP,) + (HEAD_DIM,) * 6 \
        + (HEADS * 3, N_MIXERS * GROUP)
    offs = np.concatenate([[0], np.cumsum(sizes)])
    col = lambda n: w_in[:, offs[n]:offs[n + 1]]
    (a_q, a_k, a_v, b_q, b_k, b_v, c_q, c_k, c_v, d_q, d_kc, d_vc, d_ks, d_vs, d_kw, d_vw, d_g, gate) = \
        [col(n) for n in range(18)]
    rep_kv = lambda w: jnp.repeat(w.reshape(d, 2, HEAD_DIM), 2, axis=1).reshape(d, GROUP)
    wrm = jnp.concatenate([a_q, a_k, a_v, b_q, rep_kv(b_k), rep_kv(b_v), c_k, d_kc, d_vc, d_ks, d_kw, gate], axis=1)
    wt = jnp.concatenate([c_q, c_v, d_q, d_vs, d_vw, d_g, jnp.zeros((d, 16 - HEADS * 3), w_in.dtype)], axis=1).T
    g = qk_gain
    ones = lambda n: jnp.ones((n,), F32)
    tile4 = lambda v: jnp.tile(v, HEADS)
    scale = HEAD_DIM ** -0.5
    grm = jnp.concatenate([tile4(g[0]) * scale, tile4(g[1]), ones(GROUP), tile4(g[2]) * scale, tile4(g[3]),
                           ones(GROUP), jnp.tile(qk_gain_diff[1], 2 * HEADS), ones(2 * HEAD_DIM), g[6], g[7],
                           ones(N_MIXERS * GROUP)])
    gt = jnp.concatenate([jnp.tile(qk_gain_diff[0], 2 * HEADS) * (DIFF_QK_DIM ** -0.5 * LOG2E), ones(GROUP),
                          tile4(g[4]) * (scale * LOG2E), ones(2 * HEAD_DIM + 16)])
    return wrm.astype(BF16), wt.astype(BF16), grm.reshape(1, -1), gt.reshape(-1, 1)


def _compress_weights(cmp_pos, cmp_w1, cmp_b1, cmp_w2, cmp_b2):
    half = CMP_LEN // 2
    pos = jnp.concatenate([cmp_pos[0], cmp_pos[1]], axis=-1)
    ptop = pos[:half].reshape(1, -1)
    pbot = pos[half:].reshape(1, -1)
    w1 = cmp_w1.reshape(2, CMP_LEN, HEAD_DIM, CMP_HIDDEN)
    zeros = jnp.zeros_like(w1[0])
    w1cat = jnp.concatenate([jnp.concatenate([w1[0], zeros], axis=-1),
                             jnp.concatenate([zeros, w1[1]], axis=-1)], axis=1)
    w1t = w1cat[:half].reshape(half * 2 * HEAD_DIM, 2 * CMP_HIDDEN).astype(BF16)
    w1b = w1cat[half:].reshape(half * 2 * HEAD_DIM, 2 * CMP_HIDDEN).astype(BF16)
    b1 = jnp.concatenate([cmp_b1[0], cmp_b1[1]]).reshape(1, -1)
    return (ptop, pbot, w1t, w1b, b1, cmp_w2[0].astype(BF16), cmp_b2[0].reshape(1, -1),
            cmp_w2[1].T.astype(BF16), cmp_b2[1].reshape(-1, 1))


def kernel(x, rel_bias_table, norm_w, w_in, w_out, qk_gain, qk_gain_diff, attn_sinks, diff_lambda, diff_subln,
           cmp_pos, cmp_w1, cmp_b1, cmp_w2, cmp_b2):
    b, s, d = x.shape
    depth = w_in.shape[0]
    n_c = s // CMP_STRIDE
    n_blk = s // SLC_BLOCK
    assert s % (BAND_TILE * DILATED_CONFIGS[-1][1]) == 0 and s % PROJ_ROWS == 0 and d == N_MIXERS * GROUP

    table = rel_bias_table.astype(F32)
    band_bias = [_build_bias(table, head0=0, n_d=1, rows=BAND_TILE, cols=2 * BAND_TILE, base0=BAND_TILE, dstep=0,
                             rs=1, cs=-1, dscale=rate, max_dist=window // rate) for window, rate in DILATED_CONFIGS]
    swa_bias = _build_bias(table, head0=HEADS, n_d=1, rows=BAND_TILE, cols=2 * BAND_TILE, base0=BAND_TILE, dstep=0,
                           rs=1, cs=-1, max_dist=SWA_WINDOW - 1)
    flash_tiles = dict(rows=KT, cols=QT, base0=0, dstep=QT, rs=-1, cs=1, scale=LOG2E)
    diff_bias = _build_bias(table, head0=2 * HEADS, n_d=N_BIAS_TILES, d_valid=N_NEAR + 1, **flash_tiles)
    slc_bias = _build_bias(table, head0=3 * HEADS, n_d=N_BIAS_TILES, d_valid=N_NEAR + 1, **flash_tiles)
    win_bias = _build_bias(table, head0=3 * HEADS, n_d=N_WIN, max_dist=NSA_WINDOW - 1, **flash_tiles)
    far_bias = table[NUM_BUCKETS - 1] * LOG2E
    cmp_bias = _build_bias(table, head0=3 * HEADS, n_d=1, rows=n_c, cols=s, base0=-(CMP_LEN - 1), dstep=0,
                           rs=-CMP_STRIDE, cs=1, r_valid=n_c - 1, col_tile=4 * QT, scale=LOG2E)
    e64, e32 = _block_diag_mean(HEAD_DIM), _block_diag_mean(DIFF_QK_DIM)
    rep_idx = np.arange(n_blk * SEL_REP) // SEL_REP
    rep = jnp.asarray((rep_idx[:, None] == np.arange(n_blk)[None, :]).astype(np.float32), BF16)
    no_sink = jnp.zeros((HEADS,), F32)

    x2 = x.reshape(b * s, d)
    for layer in range(depth):
        wrm, wt, grm, gt = _layer_weights(w_in[layer], qk_gain[layer], qk_gain_diff[layer])
        (a_q, a_k, a_v, b_q, b_k, b_v, c_k, kvc, ksw, gate, a_q4, a_k4, a_v4, a_q16, a_k16, a_v16,
         c_qt, c_vt, d_qt, d_vst, d_vwt, d_gt) = _project(x2, norm_w[layer].reshape(1, d), wrm, wt, grm, gt, e64, e32)
        seq = lambda t: t.reshape(b, s, t.shape[-1])
        per_batch = lambda t: t.reshape(b, t.shape[0] // b, t.shape[1])
        flat = lambda t: t.reshape(b * t.shape[1], t.shape[2])
        a_in = ((a_q, a_k, a_v), (a_q4, a_k4, a_v4), (a_q16, a_k16, a_v16))
        a_res = [_banded(*map(per_batch, a_in[n]), band_bias[n], no_sink, rate, False)
                 for n, (_, rate) in enumerate(DILATED_CONFIGS)]
        o_b, _ = _banded(seq(b_q), seq(b_k), seq(b_v), swa_bias, attn_sinks[layer].astype(F32), 1, True)
        lambda_init = 0.8 - 0.6 * math.exp(-0.3 * layer)
        o_c = _diff_attention(far_bias[2 * HEADS:3 * HEADS], c_qt, seq(c_k), c_vt, diff_bias, diff_lambda[layer].astype(F32),
                              diff_subln[layer].reshape(HEAD_DIM, 1).astype(F32), lambda_init)
        cw = _compress_weights(cmp_pos[layer], cmp_w1[layer], cmp_b1[layer], cmp_w2[layer], cmp_b2[layer])
        kc, vct = _compress(kvc.reshape(b, n_c, CMP_STRIDE * 2 * HEAD_DIM), *cw, qk_gain[layer, 5].reshape(1, -1))
        o_cmp, sel = _cmp_attention(d_qt, kc, vct, cmp_bias, b)
        o_d = _slc_win_attention(far_bias[3 * HEADS:4 * HEADS], d_qt, seq(ksw), d_vst, d_vwt, sel, rep, o_cmp, d_gt, slc_bias, win_bias)
        x2 = _out_projection(x2, [flat(r[0]) for r in a_res], [flat(r[1]) for r in a_res], o_b.reshape(b * s, GROUP),
                             o_c.reshape(b * s, GROUP), o_d.reshape(b * s, GROUP), gate, w_out[layer].astype(BF16))
    return x2.reshape(b, s, d)
```

```python
import functools
import math

import numpy as np
import jax
import jax.numpy as jnp
from jax import lax
from jax.experimental import pallas as pl
from jax.experimental.pallas import tpu as pltpu

F32 = jnp.float32
BF16 = jnp.bfloat16

HEAD_DIM = 64
HEADS = 4
GROUP = HEADS * HEAD_DIM
N_MIXERS = 4
NUM_BUCKETS = 32
REL_MAX_DIST = 2048
DILATED_CONFIGS = ((128, 1), (512, 4), (2048, 16))
FOLD_RATES = tuple(rate for _, rate in DILATED_CONFIGS if rate > 1)
SWA_WINDOW = 128
DIFF_QK_DIM = HEAD_DIM // 2
CMP_LEN = 32
CMP_STRIDE = 16
CMP_HIDDEN = 256
SLC_BLOCK = 64
SLC_TOPK = 16
NSA_WINDOW = 512
RMS_EPS = 1e-6
NEG_INF = -1e30
FORCE_SELECT = 1e9
TINY = 1e-30
LOG2E = math.log2(math.e)

PROJ_ROWS = 512
BAND_TILE = 128
BAND_STEP = 1024
BAND_LOOKAHEAD = 8
BIAS_ROW_CHUNK = 64
LANES = 128
QT = 256
KT = 256
VMEM_LIMIT = 56 * 1024 * 1024
MXU_LOOKAHEAD = 4
V_ROWS = HEAD_DIM + 16

NT_DIMS = (((1,), (1,)), ((), ()))


def _t5_thresholds():
    n = np.arange(0, 4 * REL_MAX_DIST)
    max_exact = NUM_BUCKETS // 2
    nf = np.maximum(n, 1).astype(np.float32)
    large = max_exact + (np.log(nf / np.float32(max_exact)) / np.float32(math.log(REL_MAX_DIST / max_exact))
                         * np.float32(NUM_BUCKETS - max_exact)).astype(np.int32)
    bucket = np.where(n < max_exact, n, np.minimum(large, NUM_BUCKETS - 1))
    return [int(np.argmax(bucket >= b)) for b in range(NUM_BUCKETS)]


T5_THRESHOLDS = _t5_thresholds()
FAR_DIST = T5_THRESHOLDS[-1]


def _bias_kernel(tbl_ref, out_ref, *, head0, base0, dstep, rs, cs, dscale, max_dist, r_valid, d_valid, col_tile,
                 scale):
    h = pl.program_id(0)
    d = pl.program_id(1)
    ct = pl.program_id(2)
    rows, cols = out_ref.shape[-2:]
    chunk = BIAS_ROW_CHUNK if rows % BIAS_ROW_CHUNK == 0 else rows
    for r0 in range(0, rows, chunk):
        origin = base0 + d * dstep + r0 * rs + ct * col_tile * cs
        corners = [origin + dr * rs + dc * cs for dr in (0, chunk - 1) for dc in (0, cols - 1)]
        lo = functools.reduce(jnp.minimum, corners)
        hi = functools.reduce(jnp.maximum, corners)
        all_masked = (hi < 0) | (lo > max_dist) | (d >= d_valid) | (r0 >= r_valid)
        all_far = (lo * dscale >= FAR_DIST) & (hi <= max_dist) & (d < d_valid) & (r0 + chunk <= r_valid)
        out = out_ref.at[0, 0, r0:r0 + chunk, :]

        @pl.when(all_masked)
        def _():
            out[...] = jnp.full((chunk, cols), NEG_INF, F32)

        @pl.when(all_far)
        def _():
            out[...] = jnp.full((chunk, cols), tbl_ref[NUM_BUCKETS - 1, head0 + h] * scale, F32)

        @pl.when(jnp.logical_not(all_masked | all_far))
        def _():
            r = lax.broadcasted_iota(jnp.int32, (chunk, cols), 0)
            c = lax.broadcasted_iota(jnp.int32, (chunk, cols), 1)
            dist = origin + r * rs + c * cs
            n = dist * dscale
            val = jnp.full((chunk, cols), tbl_ref[0, head0 + h], F32)
            for b in range(1, NUM_BUCKETS):
                val = jnp.where(n >= T5_THRESHOLDS[b], tbl_ref[b, head0 + h], val)
            valid = (dist >= 0) & (dist <= max_dist) & (r + r0 < r_valid) & (d < d_valid)
            out[...] = jnp.where(valid, val * scale, NEG_INF)


def _build_bias(table, *, head0, n_d, rows, cols, base0, dstep, rs, cs, dscale=1, max_dist=1 << 30,
                r_valid=1 << 30, d_valid=1 << 30, col_tile=None, scale=1.0):
    col_tile = cols if col_tile is None else col_tile
    kern = functools.partial(_bias_kernel, head0=head0, base0=base0, dstep=dstep, rs=rs, cs=cs, dscale=dscale,
                             max_dist=max_dist, r_valid=r_valid, d_valid=d_valid, col_tile=col_tile, scale=scale)
    return pl.pallas_call(
        kern,
        grid=(HEADS, n_d, cols // col_tile),
        in_specs=[pl.BlockSpec(memory_space=pltpu.SMEM)],
        out_specs=pl.BlockSpec((1, 1, rows, col_tile), lambda h, d, c: (h, d, 0, c)),
        out_shape=jax.ShapeDtypeStruct((HEADS, n_d, rows, cols), F32),
        name="rel_bias_tiles",
    )(table)


RM_AQ, RM_AK, RM_AV = 0, 256, 512
RM_BQ, RM_BK, RM_BV = 768, 1024, 1280
RM_CK = 1536
RM_KVC = 1792
RM_KSW = 1920
RM_GATE = 2048
RM_COLS = 3072
TR_CQ, TR_CV, TR_DQ, TR_DVS, TR_DVW, TR_DG = 0, 256, 512, 768, 832, 896
TR_ROWS = 912


def _proj_kernel(x_ref, nw_ref, wrm_ref, wt_ref, grm_ref, gt_ref, e64_ref, e32_ref,
                 aq_ref, ak_ref, av_ref, bq_ref, bk_ref, bv_ref, ck_ref, kvc_ref, ksw_ref, gate_ref,
                 aq4_ref, ak4_ref, av4_ref, aq16_ref, ak16_ref, av16_ref,
                 cq_ref, cv_ref, dq_ref, dvs_ref, dvw_ref, dg_ref, fold_ref, *, seq_len):
    x = x_ref[...]
    ms = jnp.mean(x * x, axis=-1, keepdims=True)
    xn = (x * lax.rsqrt(ms + RMS_EPS) * nw_ref[...]).astype(BF16)
    rows = x.shape[0]

    def rm(c0, width):
        return jnp.dot(xn, wrm_ref[:, c0:c0 + width], preferred_element_type=F32)

    def rm_normed(c0, width, e_ref):
        h = rm(c0, width)
        msq = jnp.dot((h * h).astype(BF16), e_ref[0:width, 0:width], preferred_element_type=F32)
        return h * lax.rsqrt(msq + RMS_EPS) * grm_ref[:, c0:c0 + width]

    def put_folded(val, ref, folded_refs):
        ref[...] = val.astype(ref.dtype)
        for half in range(GROUP // LANES):
            fold_ref[half] = val[:, half * LANES:(half + 1) * LANES]
        for rate, fref in zip(FOLD_RATES, folded_refs):
            for rho in range(rate):
                for half in range(GROUP // LANES):
                    c0 = rho * GROUP + half * LANES
                    fref[:, c0:c0 + LANES] = fold_ref[half, pl.ds(rho, rows // rate, stride=rate), :].astype(fref.dtype)

    put_folded(rm_normed(RM_AQ, GROUP, e64_ref), aq_ref, (aq4_ref, aq16_ref))
    put_folded(rm_normed(RM_AK, GROUP, e64_ref), ak_ref, (ak4_ref, ak16_ref))
    put_folded(rm(RM_AV, GROUP), av_ref, (av4_ref, av16_ref))
    bq_ref[...] = rm_normed(RM_BQ, GROUP, e64_ref).astype(bq_ref.dtype)
    bk_ref[...] = rm_normed(RM_BK, GROUP, e64_ref).astype(bk_ref.dtype)
    bv_ref[...] = rm(RM_BV, GROUP).astype(bv_ref.dtype)
    ck_ref[...] = rm_normed(RM_CK, GROUP, e32_ref).astype(ck_ref.dtype)
    kvc_ref[...] = rm(RM_KVC, 2 * HEAD_DIM).astype(kvc_ref.dtype)
    ksw_ref[:, 0:2 * HEAD_DIM] = rm_normed(RM_KSW, 2 * HEAD_DIM, e64_ref).astype(ksw_ref.dtype)
    pos = (pl.program_id(0) * rows + lax.broadcasted_iota(jnp.int32, (rows, SEL_LANES), 0)) % seq_len
    block_lane = lax.broadcasted_iota(jnp.int32, (rows, SEL_LANES), 1)
    ksw_ref[:, 2 * HEAD_DIM:] = jnp.where(pos // SLC_BLOCK == block_lane, 1.0, 0.0).astype(ksw_ref.dtype)
    gate_ref[...] = rm(RM_GATE, N_MIXERS * GROUP).astype(gate_ref.dtype)

    def tr(r0, height):
        return lax.dot_general(wt_ref[r0:r0 + height, :], xn, NT_DIMS, preferred_element_type=F32)

    def tr_normed(r0, height, group):
        h3 = tr(r0, height).reshape(height // group, group, rows)
        msq = jnp.mean(h3 * h3, axis=1, keepdims=True)
        return (h3 * lax.rsqrt(msq + RMS_EPS)).reshape(height, rows) * gt_ref[r0:r0 + height, :]

    def put(ref, val):
        for t in range(rows // QT):
            ref[t] = val[:, t * QT:(t + 1) * QT].astype(ref.dtype)

    def with_ones(v):
        ones = jnp.ones((V_ROWS - HEAD_DIM, rows), F32)
        parts = []
        for h in range(v.shape[0] // HEAD_DIM):
            parts += [v[h * HEAD_DIM:(h + 1) * HEAD_DIM], ones]
        return jnp.concatenate(parts, axis=0)

    put(cq_ref, tr_normed(TR_CQ, GROUP, DIFF_QK_DIM))
    put(cv_ref, with_ones(tr(TR_CV, GROUP)))
    put(dq_ref, tr_normed(TR_DQ, GROUP, HEAD_DIM))
    put(dvs_ref, with_ones(tr(TR_DVS, HEAD_DIM)))
    put(dvw_ref, with_ones(tr(TR_DVW, HEAD_DIM)))
    put(dg_ref, tr(TR_DG, 16))


def _project(x2, nw, wrm, wt, grm, gt, e64, e32, seq_len):
    m, d = x2.shape
    nt = m // QT
    tpr = PROJ_ROWS // QT
    const = lambda shape: pl.BlockSpec(shape, lambda i: (0,) * len(shape))
    rm_out = lambda width, dtype: (jax.ShapeDtypeStruct((m, width), dtype),
                                   pl.BlockSpec((PROJ_ROWS, width), lambda i: (i, 0)))
    tr_out = lambda height, dtype: (jax.ShapeDtypeStruct((nt, height, QT), dtype),
                                    pl.BlockSpec((tpr, height, QT), lambda i: (i, 0, 0)))
    outs = [rm_out(GROUP, BF16)] * 7 + [rm_out(2 * HEAD_DIM, F32), rm_out(2 * HEAD_DIM + SEL_LANES, BF16),
                                        rm_out(N_MIXERS * GROUP, BF16)]
    fold_out = lambda rate: (jax.ShapeDtypeStruct((m // rate, rate * GROUP), BF16),
                             pl.BlockSpec((PROJ_ROWS // rate, rate * GROUP), lambda i: (i, 0)))
    outs += [fold_out(rate) for rate in FOLD_RATES for _ in range(3)]
    outs += [tr_out(GROUP, BF16), tr_out(HEADS * V_ROWS, BF16), tr_out(GROUP, BF16), tr_out(V_ROWS, BF16),
             tr_out(V_ROWS, BF16), tr_out(16, F32)]
    return pl.pallas_call(
        functools.partial(_proj_kernel, seq_len=seq_len),
        grid=(m // PROJ_ROWS,),
        in_specs=[pl.BlockSpec((PROJ_ROWS, d), lambda i: (i, 0)), const((1, d)), const((d, RM_COLS)),
                  const((TR_ROWS, d)), const((1, RM_COLS)), const((TR_ROWS, 1)), const((GROUP, GROUP)),
                  const((GROUP, GROUP))],
        out_specs=[o[1] for o in outs],
        out_shape=[o[0] for o in outs],
        scratch_shapes=[pltpu.VMEM((GROUP // LANES, PROJ_ROWS, LANES), F32)],
        compiler_params=pltpu.CompilerParams(dimension_semantics=("arbitrary",), vmem_limit_bytes=VMEM_LIMIT),
        name="in_projection",
    )(x2, nw, wrm, wt, grm, gt, e64, e32)


def _band_kernel(sink_ref, q_ref, kp_ref, kc_ref, vp_ref, vc_ref, bias_ref, o_ref, lse_ref, p_ref, *, use_sink):
    i = pl.program_id(2)
    n_blocks = q_ref.shape[1] // BAND_TILE
    head_q = lax.broadcasted_iota(jnp.int32, (BAND_TILE, GROUP), 1) // HEAD_DIM
    head_v = lax.broadcasted_iota(jnp.int32, (2 * BAND_TILE, GROUP), 1) // HEAD_DIM
    lane = lax.broadcasted_iota(jnp.int32, (BAND_TILE, LANES), 1)
    in_prev = lax.broadcasted_iota(jnp.int32, (1, 2 * BAND_TILE), 1) < BAND_TILE
    no_prev = jnp.where(in_prev & (i == 0), NEG_INF, 0.0).astype(F32)

    def window(cur_ref, prev_ref, m):
        if m == 0:
            return jnp.concatenate([prev_ref[0], cur_ref[0, 0:BAND_TILE, :]], axis=0)
        return cur_ref[0, (m - 1) * BAND_TILE:(m + 1) * BAND_TILE, :]

    def scores(n):
        m, h = divmod(n, HEADS)
        q = q_ref[0, m * BAND_TILE:(m + 1) * BAND_TILE, :]
        qh = jnp.where(head_q == h, q, jnp.zeros_like(q))
        bias = bias_ref[h, 0] + no_prev if m == 0 else bias_ref[h, 0]
        return lax.dot_general(qh, window(kc_ref, kp_ref, m), NT_DIMS, preferred_element_type=F32) + bias

    lse_tiles = {}

    def update(n, s):
        m, h = divmod(n, HEADS)
        mx = jnp.max(s, axis=1, keepdims=True)
        if use_sink:
            mx = jnp.maximum(mx, sink_ref[h])
        p = jnp.exp(s - mx)
        den = jnp.sum(p, axis=1, keepdims=True)
        if use_sink:
            den = den + jnp.exp(sink_ref[h] - mx)
        p_ref[m % 2, :, h * 2 * BAND_TILE:(h + 1) * 2 * BAND_TILE] = (p * (1.0 / den)).astype(BF16)
        lse_tiles[m] = jnp.where(lane == h, mx + jnp.log(den), lse_tiles.get(m, jnp.zeros((BAND_TILE, LANES), F32)))
        if h == HEADS - 1:
            v = window(vc_ref, vp_ref, m)
            v_heads = jnp.concatenate([jnp.where(head_v == hh, v, jnp.zeros_like(v)) for hh in range(HEADS)], axis=0)
            rows = slice(m * BAND_TILE, (m + 1) * BAND_TILE)
            o_ref[0, rows, :] = jnp.dot(p_ref[m % 2], v_heads, preferred_element_type=F32).astype(o_ref.dtype)
            lse_ref[0, rows, :] = lse_tiles.pop(m)

    _staggered(n_blocks * HEADS, scores, update, ahead=BAND_LOOKAHEAD)


def _banded(q, k, v, bias, sink, rate, use_sink):
    b, ln, _ = q.shape
    step = min(BAND_STEP, ln)
    per_step = step // BAND_TILE
    cur = pl.BlockSpec((1, step, GROUP), lambda bb, r, i: (bb, i, r))
    prev = pl.BlockSpec((1, BAND_TILE, GROUP), lambda bb, r, i: (bb, jnp.maximum(i * per_step - 1, 0), r))
    o, lse = pl.pallas_call(
        functools.partial(_band_kernel, use_sink=use_sink),
        grid=(b, rate, ln // step),
        in_specs=[pl.BlockSpec(memory_space=pltpu.SMEM), cur, prev, cur, prev, cur,
                  pl.BlockSpec((HEADS, 1, BAND_TILE, 2 * BAND_TILE), lambda bb, r, i: (0, 0, 0, 0))],
        out_specs=[cur, pl.BlockSpec((1, step, LANES), lambda bb, r, i: (bb, i, r))],
        out_shape=[jax.ShapeDtypeStruct((b, ln, rate * GROUP), BF16), jax.ShapeDtypeStruct((b, ln, rate * LANES), F32)],
        scratch_shapes=[pltpu.VMEM((2, BAND_TILE, HEADS * 2 * BAND_TILE), BF16)],
        compiler_params=pltpu.CompilerParams(dimension_semantics=("arbitrary",) * 3),
        name=f"banded_attention_r{rate}",
    )(sink, q, k, k, v, v, bias)
    return o, lse


def _flash_reset(m_ref, acc_ref):
    m_ref[...] = jnp.full(m_ref.shape, NEG_INF, F32)
    acc_ref[...] = jnp.zeros(acc_ref.shape, F32)


def _flash_update(n, s, v_t, m_ref, acc_ref, shift=None):
    m_old = m_ref[n]
    if shift is None:
        m_new = jnp.maximum(m_old, jnp.max(s, axis=0, keepdims=True))
        p = jnp.exp2(s - m_new)
    else:
        m_new = jnp.maximum(m_old, jnp.max(s, axis=0, keepdims=True) + shift)
        p = jnp.exp2(s - (m_new - shift))
    alpha = jnp.exp2(m_old - m_new)
    acc_ref[n] = alpha * acc_ref[n] + jnp.dot(v_t, p.astype(BF16), preferred_element_type=F32)
    m_ref[n] = m_new


def _flash_result(n, acc_ref):
    return acc_ref[n, 0:HEAD_DIM, :] / acc_ref[n, HEAD_DIM:HEAD_DIM + 1, :]


def _staggered(n_items, scores, update, ahead=MXU_LOOKAHEAD):
    pending = {n: scores(n) for n in range(min(ahead, n_items))}
    for n in range(n_items):
        if n + ahead < n_items:
            pending[n + ahead] = scores(n + ahead)
        update(n, pending.pop(n))


def _pipelined_tiles(first, n_tiles, n_chains, group, load_tile, scores, update, next_ref, left_by_previous=None,
                     last_of_sweep=False):
    ahead = next_ref.shape[0]
    n_items = group * n_chains
    assert ahead <= n_chains

    def body(trip, _, issue_next=True):
        base = first + trip * group
        tiles, pending = {}, {}
        for n in range(n_items):
            cur = next_ref[n] if n < ahead else pending.pop(n)
            if n + ahead < n_items or issue_next:
                g, c = divmod(n + ahead, n_chains)
                if g not in tiles:
                    tiles[g] = load_tile(base + g)
                new = scores(tiles[g], base + g, c)
                if n + ahead < n_items:
                    pending[n + ahead] = new
                else:
                    next_ref[n + ahead - n_items] = new
            update(base + n // n_chains, n % n_chains, cur)

    if left_by_previous is None:
        first_tile = load_tile(first)
        for n in range(ahead):
            next_ref[n] = scores(first_tile, first, n)
    else:
        for n in range(ahead):
            next_ref[n] = next_ref[n] + left_by_previous(first, n)
    n_trips = (n_tiles + group - 1) // group
    if last_of_sweep:
        lax.fori_loop(0, n_trips - 1, body, None)
        body(n_trips - 1, None, issue_next=False)
    else:
        lax.fori_loop(0, n_trips, body, None)


def _flash_scratch(chains, ahead):
    return [pltpu.VMEM((chains, 1, QT), F32), pltpu.VMEM((chains, V_ROWS, QT), F32),
            pltpu.VMEM((ahead, KT, QT), F32)]


N_NEAR = -(-(FAR_DIST + KT - 1) // QT)
N_BIAS_TILES = N_NEAR + 2
DIFF_TILE_GROUP = 2
SLC_TILE_GROUP = 2
FAR_TILE_GROUP = 4


def _bias_tile_index(i, j):
    return jnp.where(j > i, N_NEAR + 1, jnp.minimum(i - j, N_NEAR))


def _whole_far_groups(i, group):
    return jnp.maximum(i - (N_NEAR - 1), 0) // group * group


def _diff_kernel(far_ref, q_ref, k_ref, v_ref, bias_ref, lam_ref, subln_ref, o_ref, qz_ref, m_ref, acc_ref, next_ref,
                 ot_ref, *, lambda_init):
    i = pl.program_id(1)
    q = q_ref[0]
    row = lax.broadcasted_iota(jnp.int32, (GROUP, QT), 0) // DIFF_QK_DIM
    for n in range(2 * HEADS):
        qz_ref[n] = jnp.where(row == n, q, jnp.zeros_like(q))
    _flash_reset(m_ref, acc_ref)

    def load_tile(j):
        return k_ref[0, pl.ds(pl.multiple_of(jnp.minimum(j, i) * KT, KT), KT), :]

    def values(j, n):
        h = n // 2
        return v_ref[0, jnp.minimum(j, i), h * V_ROWS:(h + 1) * V_ROWS, :]

    n_far = _whole_far_groups(i, FAR_TILE_GROUP)
    _pipelined_tiles(0, n_far, 2 * HEADS, FAR_TILE_GROUP, load_tile,
                     lambda k, j, n: jnp.dot(k, qz_ref[n], preferred_element_type=F32),
                     lambda j, n, s: _flash_update(n, s, values(j, n), m_ref, acc_ref, shift=far_ref[n // 2]),
                     next_ref)

    def scores(k, j, n):
        return jnp.dot(k, qz_ref[n], preferred_element_type=F32) + bias_ref[n // 2, _bias_tile_index(i, j)]

    _pipelined_tiles(n_far, i + 1 - n_far, 2 * HEADS, DIFF_TILE_GROUP, load_tile, scores,
                     lambda j, n, s: _flash_update(n, s, values(j, n), m_ref, acc_ref), next_ref,
                     left_by_previous=lambda j, n: bias_ref[n // 2, _bias_tile_index(i, j)], last_of_sweep=True)

    lam_p = lam_ref[...]
    lam = (jnp.exp(jnp.sum(lam_p[0:1] * lam_p[1:2], axis=1, keepdims=True))
           - jnp.exp(jnp.sum(lam_p[2:3] * lam_p[3:4], axis=1, keepdims=True)) + lambda_init)
    for h in range(HEADS):
        o = _flash_result(2 * h, acc_ref) - lam * _flash_result(2 * h + 1, acc_ref)
        msq = jnp.mean(o * o, axis=0, keepdims=True)
        ot_ref[h * HEAD_DIM:(h + 1) * HEAD_DIM, :] = (o * lax.rsqrt(msq + RMS_EPS) * subln_ref[...]
                                                      * (1.0 - lambda_init))
    o_ref[0] = ot_ref[...].T.astype(o_ref.dtype)


def _diff_attention(far, q_t, k, v_t, bias, lam_p, subln, lambda_init):
    b, s, _ = k.shape
    nq = s // QT
    nkv = s // KT
    v4 = v_t.reshape(b, nkv, HEADS * V_ROWS, KT)
    return pl.pallas_call(
        functools.partial(_diff_kernel, lambda_init=lambda_init),
        grid=(b, nq),
        in_specs=[pl.BlockSpec(memory_space=pltpu.SMEM),
                  pl.BlockSpec((1, GROUP, QT), lambda bb, i: (bb * nq + i, 0, 0)),
                  pl.BlockSpec((1, s, GROUP), lambda bb, i: (bb, 0, 0)),
                  pl.BlockSpec((1, nkv, HEADS * V_ROWS, KT), lambda bb, i: (bb, 0, 0, 0)),
                  pl.BlockSpec((HEADS, N_BIAS_TILES, KT, QT), lambda bb, i: (0, 0, 0, 0)),
                  pl.BlockSpec((4, DIFF_QK_DIM), lambda bb, i: (0, 0)),
                  pl.BlockSpec((HEAD_DIM, 1), lambda bb, i: (0, 0))],
        out_specs=pl.BlockSpec((1, QT, GROUP), lambda bb, i: (bb, i, 0)),
        out_shape=jax.ShapeDtypeStruct((b, s, GROUP), BF16),
        scratch_shapes=[pltpu.VMEM((2 * HEADS, GROUP, QT), BF16)] + _flash_scratch(2 * HEADS, MXU_LOOKAHEAD)
        + [pltpu.VMEM((GROUP, QT), F32)],
        compiler_params=pltpu.CompilerParams(dimension_semantics=("arbitrary", "arbitrary"),
                                             vmem_limit_bytes=VMEM_LIMIT),
        name="diff_attention",
    )(far, q_t, k, v4, bias, lam_p, subln)


def _compress_kernel(ch_ref, ptop_ref, pbot_ref, w1t_ref, w1b_ref, b1_ref, w2k_ref, b2k_ref, w2v_ref, b2v_ref,
                     gk_ref, kc_ref, vct_ref):
    ch = ch_ref[0]
    n_c = ch.shape[0]
    u = jnp.dot((ch + ptop_ref[...]).astype(BF16), w1t_ref[...], preferred_element_type=F32)
    v = jnp.dot((ch + pbot_ref[...]).astype(BF16), w1b_ref[...], preferred_element_type=F32)
    v_next = pltpu.roll(v, n_c - 1, 0)
    hid = jax.nn.gelu(u + v_next + b1_ref[...])
    hk = hid[:, :CMP_HIDDEN].astype(BF16)
    hv = hid[:, CMP_HIDDEN:].astype(BF16)
    kc = jnp.dot(hk, w2k_ref[...], preferred_element_type=F32) + b2k_ref[...]
    msq = jnp.mean(kc * kc, axis=-1, keepdims=True)
    kc_ref[0] = (kc * lax.rsqrt(msq + RMS_EPS) * gk_ref[...]).astype(kc_ref.dtype)
    vct = lax.dot_general(w2v_ref[...], hv, NT_DIMS, preferred_element_type=F32) + b2v_ref[...]
    vct_ref[0] = vct.astype(vct_ref.dtype)


def _compress(chunks, ptop, pbot, w1t, w1b, b1, w2k, b2k, w2v, b2v, gk):
    b, n_c, width = chunks.shape
    const = lambda a: pl.BlockSpec(a.shape, lambda bb: (0,) * a.ndim)
    params = (ptop, pbot, w1t, w1b, b1, w2k, b2k, w2v, b2v, gk)
    return pl.pallas_call(
        _compress_kernel,
        grid=(b,),
        in_specs=[pl.BlockSpec((1, n_c, width), lambda bb: (bb, 0, 0))] + [const(a) for a in params],
        out_specs=[pl.BlockSpec((1, n_c, HEAD_DIM), lambda bb: (bb, 0, 0)),
                   pl.BlockSpec((1, HEAD_DIM, n_c), lambda bb: (bb, 0, 0))],
        out_shape=[jax.ShapeDtypeStruct((b, n_c, HEAD_DIM), BF16), jax.ShapeDtypeStruct((b, HEAD_DIM, n_c), BF16)],
        compiler_params=pltpu.CompilerParams(dimension_semantics=("arbitrary",), vmem_limit_bytes=VMEM_LIMIT),
        name="nsa_compress",
    )(chunks, *params)


def _cmp_attn_kernel(q_ref, kc_ref, vct_ref, bias_ref, o_ref, sel_ref, p_ref, *, n_sel):
    i = pl.program_id(0)
    kc = kc_ref[0]
    vct = vct_ref[0]
    n_c = kc.shape[0]
    n_blk = sel_ref.shape[1]
    probs = []

    def scores(h):
        return jnp.dot(kc, q_ref[0, h * HEAD_DIM:(h + 1) * HEAD_DIM, :], preferred_element_type=F32) + bias_ref[h, 0]

    def update(h, s):
        m = jnp.maximum(jnp.max(s, axis=0, keepdims=True), 0.5 * NEG_INF)
        p = jnp.exp2(s - m)
        den = jnp.sum(p, axis=0, keepdims=True)
        p = p * (1.0 / jnp.maximum(den, TINY))
        o_ref[0, h * HEAD_DIM:(h + 1) * HEAD_DIM, :] = jnp.dot(vct, p.astype(BF16),
                                                               preferred_element_type=F32).astype(o_ref.dtype)
        probs.append(p)

    _staggered(HEADS, scores, update)
    psum = (probs[0] + probs[1]) + (probs[2] + probs[3])
    per_blk = SLC_BLOCK // CMP_STRIDE
    halves = []
    for half in range(QT // LANES):
        p_ref[half, 0:8, :] = jnp.zeros((8, LANES), F32)
        p_ref[half, 8:8 + n_c, :] = psum[:, half * LANES:(half + 1) * LANES]
        p_ref[half, 8 + n_c:16 + n_c, :] = jnp.zeros((8, LANES), F32)
        acc = p_ref[half, pl.ds(7, n_blk, stride=per_blk), :]
        for t in range(per_blk):
            acc = acc + p_ref[half, pl.ds(8 + t, n_blk, stride=per_blk), :]
        halves.append(acc)
    imp = jnp.concatenate(halves, axis=1)
    blk = lax.broadcasted_iota(jnp.int32, (n_blk, QT), 0)
    cur = (i * QT + lax.broadcasted_iota(jnp.int32, (n_blk, QT), 1)) // SLC_BLOCK
    forced = (blk == 0) | (blk == cur) | (blk == cur - 1)
    val = jnp.where(forced, FORCE_SELECT, jnp.where(blk <= cur, imp, NEG_INF))
    sel = jnp.zeros((n_blk, QT), jnp.bool_)
    for _ in range(n_sel):
        top = jnp.max(val, axis=0, keepdims=True)
        idx = jnp.min(jnp.where(val == top, blk, n_blk), axis=0, keepdims=True)
        hit = blk == idx
        sel = sel | hit
        val = jnp.where(hit, -3.0e38, val)
    sel_ref[0] = jnp.where(sel, 0.0, NEG_INF).astype(sel_ref.dtype)


def _cmp_attention(q_t, kc, vct, bias, b):
    nt = q_t.shape[0]
    nq = nt // b
    n_c = kc.shape[1]
    n_blk = nq * QT // SLC_BLOCK
    return pl.pallas_call(
        functools.partial(_cmp_attn_kernel, n_sel=min(SLC_TOPK, n_blk)),
        grid=(nq, b),
        in_specs=[pl.BlockSpec((1, GROUP, QT), lambda i, bb: (bb * nq + i, 0, 0)),
                  pl.BlockSpec((1, n_c, HEAD_DIM), lambda i, bb: (bb, 0, 0)),
                  pl.BlockSpec((1, HEAD_DIM, n_c), lambda i, bb: (bb, 0, 0)),
                  pl.BlockSpec((HEADS, 1, n_c, QT), lambda i, bb: (0, 0, 0, i))],
        out_specs=[pl.BlockSpec((1, GROUP, QT), lambda i, bb: (bb * nq + i, 0, 0)),
                   pl.BlockSpec((1, n_blk, QT), lambda i, bb: (bb * nq + i, 0, 0))],
        out_shape=[jax.ShapeDtypeStruct((nt, GROUP, QT), BF16), jax.ShapeDtypeStruct((nt, n_blk, QT), BF16)],
        scratch_shapes=[pltpu.VMEM((QT // LANES, n_c + 16, LANES), F32)],
        compiler_params=pltpu.CompilerParams(dimension_semantics=("arbitrary", "arbitrary"),
                                             vmem_limit_bytes=VMEM_LIMIT),
        name="nsa_compressed_attention",
    )(q_t, kc, vct, bias)


SEL_LANES = 128
N_WIN = -(-(NSA_WINDOW - 1 + KT - 1) // QT)


def _slc_win_kernel(far_ref, q_ref, ksw_ref, vs_ref, vw_ref, sel_ref, ocmp_ref, g_ref, bslc_ref, bwin_ref,
                    o_ref, qz_ref, m_ref, acc_ref, next_ref, ot_ref):
    i = pl.program_id(1)
    zeros = jnp.zeros((HEAD_DIM, QT), BF16)
    mask = sel_ref[0]
    if mask.shape[0] < SEL_LANES:
        mask = jnp.concatenate([mask, jnp.zeros((SEL_LANES - mask.shape[0], QT), BF16)], axis=0)
    no_mask = jnp.zeros((SEL_LANES, QT), BF16)
    for h in range(HEADS):
        qh = q_ref[0, h * HEAD_DIM:(h + 1) * HEAD_DIM, :]
        qz_ref[h] = jnp.concatenate([qh, zeros, mask], axis=0)
        qz_ref[HEADS + h] = jnp.concatenate([zeros, qh, no_mask], axis=0)
    _flash_reset(m_ref, acc_ref)

    def load_tile(j):
        return ksw_ref[0, pl.ds(pl.multiple_of(jnp.minimum(j, i) * KT, KT), KT), :]

    def values(j):
        return vs_ref[0, jnp.minimum(j, i)]

    n_far = _whole_far_groups(i, FAR_TILE_GROUP)
    _pipelined_tiles(0, n_far, HEADS, FAR_TILE_GROUP, load_tile,
                     lambda k, j, h: jnp.dot(k, qz_ref[h], preferred_element_type=F32),
                     lambda j, h, s: _flash_update(h, s, values(j), m_ref, acc_ref, shift=far_ref[h]), next_ref)

    def slc_scores(k, j, h):
        return jnp.dot(k, qz_ref[h], preferred_element_type=F32) + bslc_ref[h, _bias_tile_index(i, j)]

    _pipelined_tiles(n_far, i + 1 - n_far, HEADS, SLC_TILE_GROUP, load_tile, slc_scores,
                     lambda j, h, s: _flash_update(h, s, values(j), m_ref, acc_ref), next_ref,
                     left_by_previous=lambda j, h: bslc_ref[h, _bias_tile_index(i, j)], last_of_sweep=True)

    def win_tile(n):
        d = N_WIN - 1 - n // HEADS
        return d, n % HEADS, jnp.maximum(i - d, 0)

    def win_scores(n):
        d, h, j = win_tile(n)
        missing = jnp.where(i < d, NEG_INF, 0.0).astype(F32)
        return jnp.dot(load_tile(j), qz_ref[HEADS + h], preferred_element_type=F32) + (bwin_ref[h, d] + missing)

    def win_update(n, s):
        _, h, j = win_tile(n)
        _flash_update(HEADS + h, s, vw_ref[0, j], m_ref, acc_ref)

    _staggered(N_WIN * HEADS, win_scores, win_update)

    for h in range(HEADS):
        g = jax.nn.sigmoid(g_ref[0, 3 * h:3 * h + 3, :])
        ot_ref[h * HEAD_DIM:(h + 1) * HEAD_DIM, :] = (g[0:1] * ocmp_ref[0, h * HEAD_DIM:(h + 1) * HEAD_DIM, :]
                                                      + g[1:2] * _flash_result(h, acc_ref)
                                                      + g[2:3] * _flash_result(HEADS + h, acc_ref))
    o_ref[0] = ot_ref[...].T.astype(o_ref.dtype)


def _slc_win_attention(far, q_t, ksw, vs_t, vw_t, sel, ocmp, g_t, bslc, bwin):
    b, s, width = ksw.shape
    nq = s // QT
    nkv = s // KT
    n_blk = s // SLC_BLOCK
    assert n_blk <= SEL_LANES and width == 2 * HEAD_DIM + SEL_LANES
    tile = lambda height: pl.BlockSpec((1, height, QT), lambda bb, i: (bb * nq + i, 0, 0))
    whole = lambda a: pl.BlockSpec(a.shape, lambda bb, i: (0,) * a.ndim)
    return pl.pallas_call(
        _slc_win_kernel,
        grid=(b, nq),
        in_specs=[pl.BlockSpec(memory_space=pltpu.SMEM), tile(GROUP),
                  pl.BlockSpec((1, s, width), lambda bb, i: (bb, 0, 0)),
                  pl.BlockSpec((1, nkv, V_ROWS, KT), lambda bb, i: (bb, 0, 0, 0)),
                  pl.BlockSpec((1, nkv, V_ROWS, KT), lambda bb, i: (bb, 0, 0, 0)),
                  tile(n_blk), tile(GROUP), tile(16), whole(bslc), whole(bwin)],
        out_specs=pl.BlockSpec((1, QT, GROUP), lambda bb, i: (bb, i, 0)),
        out_shape=jax.ShapeDtypeStruct((b, s, GROUP), BF16),
        scratch_shapes=[pltpu.VMEM((2 * HEADS, width, QT), BF16)] + _flash_scratch(2 * HEADS, MXU_LOOKAHEAD)
        + [pltpu.VMEM((GROUP, QT), F32)],
        compiler_params=pltpu.CompilerParams(dimension_semantics=("arbitrary", "arbitrary"),
                                             vmem_limit_bytes=VMEM_LIMIT),
        name="nsa_selected_window_attention",
    )(far, q_t, ksw, vs_t.reshape(b, nkv, V_ROWS, KT), vw_t.reshape(b, nkv, V_ROWS, KT), sel, ocmp, g_t, bslc, bwin)


def _out_kernel(x_ref, a0_ref, a1_ref, a2_ref, l0_ref, l1_ref, l2_ref, ob_ref, oc_ref, od_ref, gate_ref, e_ref,
                w_ref, o_ref, unfold_ref):
    rows = x_ref.shape[0]

    def unfolded(ref, rate):
        width = ref.shape[1] // rate
        for rho in range(rate):
            for part in range(width // LANES):
                c0 = rho * width + part * LANES
                unfold_ref[part, pl.ds(rho, rows // rate, stride=rate), :] = ref[:, c0:c0 + LANES].astype(F32)
        return jnp.concatenate([unfold_ref[part] for part in range(width // LANES)], axis=1)

    a0, l0 = a0_ref[...], l0_ref[...]
    a1, l1 = unfolded(a1_ref, FOLD_RATES[0]), unfolded(l1_ref, FOLD_RATES[0])
    a2, l2 = unfolded(a2_ref, FOLD_RATES[1]), unfolded(l2_ref, FOLD_RATES[1])
    mx = jnp.maximum(jnp.maximum(l0, l1), l2)
    e0, e1, e2 = jnp.exp(l0 - mx), jnp.exp(l1 - mx), jnp.exp(l2 - mx)
    den = e0 + e1 + e2

    def per_head_lanes(w):
        hi = w.astype(BF16)
        lo = (w - hi.astype(F32)).astype(BF16)
        return (jnp.dot(hi, e_ref[...], preferred_element_type=F32)
                + jnp.dot(lo, e_ref[...], preferred_element_type=F32))

    o_a = per_head_lanes(e0 / den) * a0 + per_head_lanes(e1 / den) * a1 + per_head_lanes(e2 / den) * a2
    y = jnp.concatenate([o_a, ob_ref[...].astype(F32), oc_ref[...].astype(F32), od_ref[...].astype(F32)], axis=1)
    g = gate_ref[...].astype(F32)
    y = y * (g * jax.nn.sigmoid(g))
    o_ref[...] = x_ref[...] + jnp.dot(y.astype(BF16), w_ref[...], preferred_element_type=F32)


def _out_projection(x2, a_outs, a_lses, o_b, o_c, o_d, gate, w_out):
    m, d = x2.shape
    rowblk = lambda width: pl.BlockSpec((PROJ_ROWS, width), lambda i: (i, 0))
    folded = lambda width, rate: pl.BlockSpec((PROJ_ROWS // rate, rate * width), lambda i: (i, 0))
    head_of_lane = np.arange(GROUP) // HEAD_DIM
    expand = jnp.asarray((np.arange(LANES)[:, None] == head_of_lane[None, :]).astype(np.float32), BF16)
    return pl.pallas_call(
        _out_kernel,
        grid=(m // PROJ_ROWS,),
        in_specs=[rowblk(d)] + [folded(GROUP, rate) for rate in (1,) + FOLD_RATES]
        + [folded(LANES, rate) for rate in (1,) + FOLD_RATES] + [rowblk(GROUP)] * 3
        + [rowblk(N_MIXERS * GROUP), pl.BlockSpec((LANES, GROUP), lambda i: (0, 0)),
           pl.BlockSpec((N_MIXERS * GROUP, d), lambda i: (0, 0))],
        out_specs=rowblk(d),
        out_shape=jax.ShapeDtypeStruct((m, d), F32),
        scratch_shapes=[pltpu.VMEM((GROUP // LANES, PROJ_ROWS, LANES), F32)],
        compiler_params=pltpu.CompilerParams(dimension_semantics=("arbitrary",), vmem_limit_bytes=VMEM_LIMIT),
        name="out_projection",
    )(x2, *a_outs, *a_lses, o_b, o_c, o_d, gate, expand, w_out)


def _block_diag_mean(group):
    idx = np.arange(GROUP) // group
    return jnp.asarray((idx[:, None] == idx[None, :]).astype(np.float32) / group, BF16)


def _layer_weights(w_in, qk_gain, qk_gain_diff):
    d = w_in.shape[0]
    sizes = (GROUP,) * 3 + (GROUP, GROUP // 2, GROUP // 2) + (GROUP,) * 3 + (GROUP,) + (HEAD_DIM,) * 6 \
        + (HEADS * 3, N_MIXERS * GROUP)
    offs = np.concatenate([[0], np.cumsum(sizes)])
    col = lambda n: w_in[:, offs[n]:offs[n + 1]]
    (a_q, a_k, a_v, b_q, b_k, b_v, c_q, c_k, c_v, d_q, d_kc, d_vc, d_ks, d_vs, d_kw, d_vw, d_g, gate) = \
        [col(n) for n in range(18)]
    rep_kv = lambda w: jnp.repeat(w.reshape(d, 2, HEAD_DIM), 2, axis=1).reshape(d, GROUP)
    wrm = jnp.concatenate([a_q, a_k, a_v, b_q, rep_kv(b_k), rep_kv(b_v), c_k, d_kc, d_vc, d_ks, d_kw, gate], axis=1)
    wt = jnp.concatenate([c_q, c_v, d_q, d_vs, d_vw, d_g, jnp.zeros((d, 16 - HEADS * 3), w_in.dtype)], axis=1).T
    g = qk_gain
    ones = lambda n: jnp.ones((n,), F32)
    tile4 = lambda v: jnp.tile(v, HEADS)
    scale = HEAD_DIM ** -0.5
    grm = jnp.concatenate([tile4(g[0]) * scale, tile4(g[1]), ones(GROUP), tile4(g[2]) * scale, tile4(g[3]),
                           ones(GROUP), jnp.tile(qk_gain_diff[1], 2 * HEADS), ones(2 * HEAD_DIM), g[6], g[7],
                           ones(N_MIXERS * GROUP)])
    gt = jnp.concatenate([jnp.tile(qk_gain_diff[0], 2 * HEADS) * (DIFF_QK_DIM ** -0.5 * LOG2E), ones(GROUP),
                          tile4(g[4]) * (scale * LOG2E), ones(2 * HEAD_DIM + 16)])
    return wrm.astype(BF16), wt.astype(BF16), grm.reshape(1, -1), gt.reshape(-1, 1)


def _compress_weights(cmp_pos, cmp_w1, cmp_b1, cmp_w2, cmp_b2):
    half = CMP_LEN // 2
    pos = jnp.concatenate([cmp_pos[0], cmp_pos[1]], axis=-1)
    ptop = pos[:half].reshape(1, -1)
    pbot = pos[half:].reshape(1, -1)
    w1 = cmp_w1.reshape(2, CMP_LEN, HEAD_DIM, CMP_HIDDEN)
    zeros = jnp.zeros_like(w1[0])
    w1cat = jnp.concatenate([jnp.concatenate([w1[0], zeros], axis=-1),
                             jnp.concatenate([zeros, w1[1]], axis=-1)], axis=1)
    w1t = w1cat[:half].reshape(half * 2 * HEAD_DIM, 2 * CMP_HIDDEN).astype(BF16)
    w1b = w1cat[half:].reshape(half * 2 * HEAD_DIM, 2 * CMP_HIDDEN).astype(BF16)
    b1 = jnp.concatenate([cmp_b1[0], cmp_b1[1]]).reshape(1, -1)
    return (ptop, pbot, w1t, w1b, b1, cmp_w2[0].astype(BF16), cmp_b2[0].reshape(1, -1),
            cmp_w2[1].T.astype(BF16), cmp_b2[1].reshape(-1, 1))


def kernel(x, rel_bias_table, norm_w, w_in, w_out, qk_gain, qk_gain_diff, attn_sinks, diff_lambda, diff_subln,
           cmp_pos, cmp_w1, cmp_b1, cmp_w2, cmp_b2):
    b, s, d = x.shape
    depth = w_in.shape[0]
    n_c = s // CMP_STRIDE
    n_blk = s // SLC_BLOCK
    assert s % (BAND_TILE * DILATED_CONFIGS[-1][1]) == 0 and s % PROJ_ROWS == 0 and d == N_MIXERS * GROUP

    table = rel_bias_table.astype(F32)
    band_bias = [_build_bias(table, head0=0, n_d=1, rows=BAND_TILE, cols=2 * BAND_TILE, base0=BAND_TILE, dstep=0,
                             rs=1, cs=-1, dscale=rate, max_dist=window // rate) for window, rate in DILATED_CONFIGS]
    swa_bias = _build_bias(table, head0=HEADS, n_d=1, rows=BAND_TILE, cols=2 * BAND_TILE, base0=BAND_TILE, dstep=0,
                           rs=1, cs=-1, max_dist=SWA_WINDOW - 1)
    flash_tiles = dict(rows=KT, cols=QT, base0=0, dstep=QT, rs=-1, cs=1, scale=LOG2E)
    diff_bias = _build_bias(table, head0=2 * HEADS, n_d=N_BIAS_TILES, d_valid=N_NEAR + 1, **flash_tiles)
    slc_bias = _build_bias(table, head0=3 * HEADS, n_d=N_BIAS_TILES, d_valid=N_NEAR + 1, **flash_tiles)
    win_bias = _build_bias(table, head0=3 * HEADS, n_d=N_WIN, max_dist=NSA_WINDOW - 1, **flash_tiles)
    far_bias = table[NUM_BUCKETS - 1] * LOG2E
    cmp_bias = _build_bias(table, head0=3 * HEADS, n_d=1, rows=n_c, cols=s, base0=-(CMP_LEN - 1), dstep=0,
                           rs=-CMP_STRIDE, cs=1, r_valid=n_c - 1, col_tile=4 * QT, scale=LOG2E)
    e64, e32 = _block_diag_mean(HEAD_DIM), _block_diag_mean(DIFF_QK_DIM)
    no_sink = jnp.zeros((HEADS,), F32)

    x2 = x.reshape(b * s, d)
    for layer in range(depth):
        wrm, wt, grm, gt = _layer_weights(w_in[layer], qk_gain[layer], qk_gain_diff[layer])
        (a_q, a_k, a_v, b_q, b_k, b_v, c_k, kvc, ksw, gate, a_q4, a_k4, a_v4, a_q16, a_k16, a_v16,
         c_qt, c_vt, d_qt, d_vst, d_vwt, d_gt) = _project(x2, norm_w[layer].reshape(1, d), wrm, wt, grm, gt, e64, e32, s)
        seq = lambda t: t.reshape(b, s, t.shape[-1])
        per_batch = lambda t: t.reshape(b, t.shape[0] // b, t.shape[1])
        flat = lambda t: t.reshape(b * t.shape[1], t.shape[2])
        a_in = ((a_q, a_k, a_v), (a_q4, a_k4, a_v4), (a_q16, a_k16, a_v16))
        a_res = [_banded(*map(per_batch, a_in[n]), band_bias[n], no_sink, rate, False)
                 for n, (_, rate) in enumerate(DILATED_CONFIGS)]
        o_b, _ = _banded(seq(b_q), seq(b_k), seq(b_v), swa_bias, attn_sinks[layer].astype(F32), 1, True)
        lambda_init = 0.8 - 0.6 * math.exp(-0.3 * layer)
        o_c = _diff_attention(far_bias[2 * HEADS:3 * HEADS], c_qt, seq(c_k), c_vt, diff_bias, diff_lambda[layer].astype(F32),
                              diff_subln[layer].reshape(HEAD_DIM, 1).astype(F32), lambda_init)
        cw = _compress_weights(cmp_pos[layer], cmp_w1[layer], cmp_b1[layer], cmp_w2[layer], cmp_b2[layer])
        kc, vct = _compress(kvc.reshape(b, n_c, CMP_STRIDE * 2 * HEAD_DIM), *cw, qk_gain[layer, 5].reshape(1, -1))
        o_cmp, sel = _cmp_attention(d_qt, kc, vct, cmp_bias, b)
        o_d = _slc_win_attention(far_bias[3 * HEADS:4 * HEADS], d_qt, seq(ksw), d_vst, d_vwt, sel, o_cmp, d_gt, slc_bias,
                                 win_bias)
        x2 = _out_projection(x2, [flat(r[0]) for r in a_res], [flat(r[1]) for r in a_res], o_b.reshape(b * s, GROUP),
                             o_c.reshape(b * s, GROUP), o_d.reshape(b * s, GROUP), gate, w_out[layer].astype(BF16))
    return x2.reshape(b, s, d)
```

```python
import functools
import math

import numpy as np
import jax
import jax.numpy as jnp
from jax import lax
from jax.experimental import pallas as pl
from jax.experimental.pallas import tpu as pltpu

F32 = jnp.float32
BF16 = jnp.bfloat16

HEAD_DIM = 64
HEADS = 4
GROUP = HEADS * HEAD_DIM
N_MIXERS = 4
NUM_BUCKETS = 32
REL_MAX_DIST = 2048
DILATED_CONFIGS = ((128, 1), (512, 4), (2048, 16))
FOLD_RATES = tuple(rate for _, rate in DILATED_CONFIGS if rate > 1)
SWA_WINDOW = 128
DIFF_QK_DIM = HEAD_DIM // 2
CMP_LEN = 32
CMP_STRIDE = 16
CMP_HIDDEN = 256
SLC_BLOCK = 64
SLC_TOPK = 16
NSA_WINDOW = 512
RMS_EPS = 1e-6
NEG_INF = -1e30
FORCE_SELECT = 1e9
TINY = 1e-30
LOG2E = math.log2(math.e)

PROJ_ROWS = 512
BAND_TILE = 128
BAND_STEP = 1024
BAND_LOOKAHEAD = 8
BIAS_ROW_CHUNK = 64
LANES = 128
QT = 256
KT = 256
VMEM_LIMIT = 56 * 1024 * 1024
MXU_LOOKAHEAD = 4
V_ROWS = HEAD_DIM + 16

NT_DIMS = (((1,), (1,)), ((), ()))


def _t5_thresholds():
    n = np.arange(0, 4 * REL_MAX_DIST)
    max_exact = NUM_BUCKETS // 2
    nf = np.maximum(n, 1).astype(np.float32)
    large = max_exact + (np.log(nf / np.float32(max_exact)) / np.float32(math.log(REL_MAX_DIST / max_exact))
                         * np.float32(NUM_BUCKETS - max_exact)).astype(np.int32)
    bucket = np.where(n < max_exact, n, np.minimum(large, NUM_BUCKETS - 1))
    return [int(np.argmax(bucket >= b)) for b in range(NUM_BUCKETS)]


T5_THRESHOLDS = _t5_thresholds()
FAR_DIST = T5_THRESHOLDS[-1]


def _bias_kernel(tbl_ref, out_ref, *, head0, base0, dstep, rs, cs, dscale, max_dist, r_valid, d_valid, col_tile,
                 scale):
    h = pl.program_id(0)
    d = pl.program_id(1)
    ct = pl.program_id(2)
    rows, cols = out_ref.shape[-2:]
    chunk = BIAS_ROW_CHUNK if rows % BIAS_ROW_CHUNK == 0 else rows
    for r0 in range(0, rows, chunk):
        origin = base0 + d * dstep + r0 * rs + ct * col_tile * cs
        corners = [origin + dr * rs + dc * cs for dr in (0, chunk - 1) for dc in (0, cols - 1)]
        lo = functools.reduce(jnp.minimum, corners)
        hi = functools.reduce(jnp.maximum, corners)
        all_masked = (hi < 0) | (lo > max_dist) | (d >= d_valid) | (r0 >= r_valid)
        all_far = (lo * dscale >= FAR_DIST) & (hi <= max_dist) & (d < d_valid) & (r0 + chunk <= r_valid)
        out = out_ref.at[0, 0, r0:r0 + chunk, :]

        @pl.when(all_masked)
        def _():
            out[...] = jnp.full((chunk, cols), NEG_INF, F32)

        @pl.when(all_far)
        def _():
            out[...] = jnp.full((chunk, cols), tbl_ref[NUM_BUCKETS - 1, head0 + h] * scale, F32)

        @pl.when(jnp.logical_not(all_masked | all_far))
        def _():
            r = lax.broadcasted_iota(jnp.int32, (chunk, cols), 0)
            c = lax.broadcasted_iota(jnp.int32, (chunk, cols), 1)
            dist = origin + r * rs + c * cs
            n = dist * dscale
            val = jnp.full((chunk, cols), tbl_ref[0, head0 + h], F32)
            for b in range(1, NUM_BUCKETS):
                val = jnp.where(n >= T5_THRESHOLDS[b], tbl_ref[b, head0 + h], val)
            valid = (dist >= 0) & (dist <= max_dist) & (r + r0 < r_valid) & (d < d_valid)
            out[...] = jnp.where(valid, val * scale, NEG_INF)


def _build_bias(table, *, head0, n_d, rows, cols, base0, dstep, rs, cs, dscale=1, max_dist=1 << 30,
                r_valid=1 << 30, d_valid=1 << 30, col_tile=None, scale=1.0):
    col_tile = cols if col_tile is None else col_tile
    kern = functools.partial(_bias_kernel, head0=head0, base0=base0, dstep=dstep, rs=rs, cs=cs, dscale=dscale,
                             max_dist=max_dist, r_valid=r_valid, d_valid=d_valid, col_tile=col_tile, scale=scale)
    return pl.pallas_call(
        kern,
        grid=(HEADS, n_d, cols // col_tile),
        in_specs=[pl.BlockSpec(memory_space=pltpu.SMEM)],
        out_specs=pl.BlockSpec((1, 1, rows, col_tile), lambda h, d, c: (h, d, 0, c)),
        out_shape=jax.ShapeDtypeStruct((HEADS, n_d, rows, cols), F32),
        name="rel_bias_tiles",
    )(table)


RM_AQ, RM_AK, RM_AV = 0, 256, 512
RM_BQ, RM_BK, RM_BV = 768, 1024, 1280
RM_CK = 1536
RM_KVC = 1792
RM_KSW = 1920
RM_GATE = 2048
RM_COLS = 3072
TR_CQ, TR_CV, TR_DQ, TR_DVS, TR_DVW, TR_DG = 0, 256, 512, 768, 832, 896
TR_ROWS = 912


def _proj_kernel(x_ref, nw_ref, wrm_ref, wt_ref, grm_ref, gt_ref, e64_ref, e32_ref,
                 aq_ref, ak_ref, av_ref, bq_ref, bk_ref, bv_ref, ck_ref, kvc_ref, ksw_ref, gate_ref,
                 aq4_ref, ak4_ref, av4_ref, aq16_ref, ak16_ref, av16_ref,
                 cq_ref, cv_ref, dq_ref, dvs_ref, dvw_ref, dg_ref, fold_ref):
    x = x_ref[...]
    ms = jnp.mean(x * x, axis=-1, keepdims=True)
    xn = (x * lax.rsqrt(ms + RMS_EPS) * nw_ref[...]).astype(BF16)
    rows = x.shape[0]

    def rm(c0, width):
        return jnp.dot(xn, wrm_ref[:, c0:c0 + width], preferred_element_type=F32)

    def rm_normed(c0, width, e_ref):
        h = rm(c0, width)
        msq = jnp.dot((h * h).astype(BF16), e_ref[0:width, 0:width], preferred_element_type=F32)
        return h * lax.rsqrt(msq + RMS_EPS) * grm_ref[:, c0:c0 + width]

    def put_folded(val, ref, folded_refs):
        ref[...] = val.astype(ref.dtype)
        for half in range(GROUP // LANES):
            fold_ref[half] = val[:, half * LANES:(half + 1) * LANES]
        for rate, fref in zip(FOLD_RATES, folded_refs):
            for rho in range(rate):
                for half in range(GROUP // LANES):
                    c0 = rho * GROUP + half * LANES
                    fref[:, c0:c0 + LANES] = fold_ref[half, pl.ds(rho, rows // rate, stride=rate), :].astype(fref.dtype)

    put_folded(rm_normed(RM_AQ, GROUP, e64_ref), aq_ref, (aq4_ref, aq16_ref))
    put_folded(rm_normed(RM_AK, GROUP, e64_ref), ak_ref, (ak4_ref, ak16_ref))
    put_folded(rm(RM_AV, GROUP), av_ref, (av4_ref, av16_ref))
    bq_ref[...] = rm_normed(RM_BQ, GROUP, e64_ref).astype(bq_ref.dtype)
    bk_ref[...] = rm_normed(RM_BK, GROUP, e64_ref).astype(bk_ref.dtype)
    bv_ref[...] = rm(RM_BV, GROUP).astype(bv_ref.dtype)
    ck_ref[...] = rm_normed(RM_CK, GROUP, e32_ref).astype(ck_ref.dtype)
    kvc_ref[...] = rm(RM_KVC, 2 * HEAD_DIM).astype(kvc_ref.dtype)
    ksw_ref[...] = rm_normed(RM_KSW, 2 * HEAD_DIM, e64_ref).astype(ksw_ref.dtype)
    gate_ref[...] = rm(RM_GATE, N_MIXERS * GROUP).astype(gate_ref.dtype)

    def tr(r0, height):
        return lax.dot_general(wt_ref[r0:r0 + height, :], xn, NT_DIMS, preferred_element_type=F32)

    def tr_normed(r0, height, group):
        h3 = tr(r0, height).reshape(height // group, group, rows)
        msq = jnp.mean(h3 * h3, axis=1, keepdims=True)
        return (h3 * lax.rsqrt(msq + RMS_EPS)).reshape(height, rows) * gt_ref[r0:r0 + height, :]

    def put(ref, val):
        for t in range(rows // QT):
            ref[t] = val[:, t * QT:(t + 1) * QT].astype(ref.dtype)

    def with_ones(v):
        ones = jnp.ones((V_ROWS - HEAD_DIM, rows), F32)
        parts = []
        for h in range(v.shape[0] // HEAD_DIM):
            parts += [v[h * HEAD_DIM:(h + 1) * HEAD_DIM], ones]
        return jnp.concatenate(parts, axis=0)

    put(cq_ref, tr_normed(TR_CQ, GROUP, DIFF_QK_DIM))
    put(cv_ref, with_ones(tr(TR_CV, GROUP)))
    put(dq_ref, tr_normed(TR_DQ, GROUP, HEAD_DIM))
    put(dvs_ref, with_ones(tr(TR_DVS, HEAD_DIM)))
    put(dvw_ref, with_ones(tr(TR_DVW, HEAD_DIM)))
    put(dg_ref, tr(TR_DG, 16))


def _project(x2, nw, wrm, wt, grm, gt, e64, e32):
    m, d = x2.shape
    nt = m // QT
    tpr = PROJ_ROWS // QT
    const = lambda shape: pl.BlockSpec(shape, lambda i: (0,) * len(shape))
    rm_out = lambda width, dtype: (jax.ShapeDtypeStruct((m, width), dtype),
                                   pl.BlockSpec((PROJ_ROWS, width), lambda i: (i, 0)))
    tr_out = lambda height, dtype: (jax.ShapeDtypeStruct((nt, height, QT), dtype),
                                    pl.BlockSpec((tpr, height, QT), lambda i: (i, 0, 0)))
    outs = [rm_out(GROUP, BF16)] * 7 + [rm_out(2 * HEAD_DIM, F32), rm_out(2 * HEAD_DIM, BF16),
                                        rm_out(N_MIXERS * GROUP, BF16)]
    fold_out = lambda rate: (jax.ShapeDtypeStruct((m // rate, rate * GROUP), BF16),
                             pl.BlockSpec((PROJ_ROWS // rate, rate * GROUP), lambda i: (i, 0)))
    outs += [fold_out(rate) for rate in FOLD_RATES for _ in range(3)]
    outs += [tr_out(GROUP, BF16), tr_out(HEADS * V_ROWS, BF16), tr_out(GROUP, BF16), tr_out(V_ROWS, BF16),
             tr_out(V_ROWS, BF16), tr_out(16, F32)]
    return pl.pallas_call(
        _proj_kernel,
        grid=(m // PROJ_ROWS,),
        in_specs=[pl.BlockSpec((PROJ_ROWS, d), lambda i: (i, 0)), const((1, d)), const((d, RM_COLS)),
                  const((TR_ROWS, d)), const((1, RM_COLS)), const((TR_ROWS, 1)), const((GROUP, GROUP)),
                  const((GROUP, GROUP))],
        out_specs=[o[1] for o in outs],
        out_shape=[o[0] for o in outs],
        scratch_shapes=[pltpu.VMEM((GROUP // LANES, PROJ_ROWS, LANES), F32)],
        compiler_params=pltpu.CompilerParams(dimension_semantics=("arbitrary",), vmem_limit_bytes=VMEM_LIMIT),
        name="in_projection",
    )(x2, nw, wrm, wt, grm, gt, e64, e32)


def _band_kernel(sink_ref, q_ref, kp_ref, kc_ref, vp_ref, vc_ref, bias_ref, o_ref, lse_ref, p_ref, *, use_sink):
    i = pl.program_id(2)
    n_blocks = q_ref.shape[1] // BAND_TILE
    head_q = lax.broadcasted_iota(jnp.int32, (BAND_TILE, GROUP), 1) // HEAD_DIM
    head_v = lax.broadcasted_iota(jnp.int32, (2 * BAND_TILE, GROUP), 1) // HEAD_DIM
    lane = lax.broadcasted_iota(jnp.int32, (BAND_TILE, LANES), 1)
    in_prev = lax.broadcasted_iota(jnp.int32, (1, 2 * BAND_TILE), 1) < BAND_TILE
    no_prev = jnp.where(in_prev & (i == 0), NEG_INF, 0.0).astype(F32)

    def window(cur_ref, prev_ref, m):
        if m == 0:
            return jnp.concatenate([prev_ref[0], cur_ref[0, 0:BAND_TILE, :]], axis=0)
        return cur_ref[0, (m - 1) * BAND_TILE:(m + 1) * BAND_TILE, :]

    def scores(n):
        m, h = divmod(n, HEADS)
        q = q_ref[0, m * BAND_TILE:(m + 1) * BAND_TILE, :]
        qh = jnp.where(head_q == h, q, jnp.zeros_like(q))
        bias = bias_ref[h, 0] + no_prev if m == 0 else bias_ref[h, 0]
        return lax.dot_general(qh, window(kc_ref, kp_ref, m), NT_DIMS, preferred_element_type=F32) + bias

    lse_tiles = {}

    def update(n, s):
        m, h = divmod(n, HEADS)
        mx = jnp.max(s, axis=1, keepdims=True)
        if use_sink:
            mx = jnp.maximum(mx, sink_ref[h])
        p = jnp.exp(s - mx)
        den = jnp.sum(p, axis=1, keepdims=True)
        if use_sink:
            den = den + jnp.exp(sink_ref[h] - mx)
        p_ref[m % 2, :, h * 2 * BAND_TILE:(h + 1) * 2 * BAND_TILE] = (p * (1.0 / den)).astype(BF16)
        lse_tiles[m] = jnp.where(lane == h, mx + jnp.log(den), lse_tiles.get(m, jnp.zeros((BAND_TILE, LANES), F32)))
        if h == HEADS - 1:
            v = window(vc_ref, vp_ref, m)
            v_heads = jnp.concatenate([jnp.where(head_v == hh, v, jnp.zeros_like(v)) for hh in range(HEADS)], axis=0)
            rows = slice(m * BAND_TILE, (m + 1) * BAND_TILE)
            o_ref[0, rows, :] = jnp.dot(p_ref[m % 2], v_heads, preferred_element_type=F32).astype(o_ref.dtype)
            lse_ref[0, rows, :] = lse_tiles.pop(m)

    _staggered(n_blocks * HEADS, scores, update, ahead=BAND_LOOKAHEAD)


def _banded(q, k, v, bias, sink, rate, use_sink):
    b, ln, _ = q.shape
    step = min(BAND_STEP, ln)
    per_step = step // BAND_TILE
    cur = pl.BlockSpec((1, step, GROUP), lambda bb, r, i: (bb, i, r))
    prev = pl.BlockSpec((1, BAND_TILE, GROUP), lambda bb, r, i: (bb, jnp.maximum(i * per_step - 1, 0), r))
    o, lse = pl.pallas_call(
        functools.partial(_band_kernel, use_sink=use_sink),
        grid=(b, rate, ln // step),
        in_specs=[pl.BlockSpec(memory_space=pltpu.SMEM), cur, prev, cur, prev, cur,
                  pl.BlockSpec((HEADS, 1, BAND_TILE, 2 * BAND_TILE), lambda bb, r, i: (0, 0, 0, 0))],
        out_specs=[cur, pl.BlockSpec((1, step, LANES), lambda bb, r, i: (bb, i, r))],
        out_shape=[jax.ShapeDtypeStruct((b, ln, rate * GROUP), BF16), jax.ShapeDtypeStruct((b, ln, rate * LANES), F32)],
        scratch_shapes=[pltpu.VMEM((2, BAND_TILE, HEADS * 2 * BAND_TILE), BF16)],
        compiler_params=pltpu.CompilerParams(dimension_semantics=("arbitrary",) * 3),
        name=f"banded_attention_r{rate}",
    )(sink, q, k, k, v, v, bias)
    return o, lse


def _flash_reset(m_ref, acc_ref):
    m_ref[...] = jnp.full(m_ref.shape, NEG_INF, F32)
    acc_ref[...] = jnp.zeros(acc_ref.shape, F32)


def _flash_update(n, s, v_t, m_ref, acc_ref, shift=None):
    m_old = m_ref[n]
    if shift is None:
        m_new = jnp.maximum(m_old, jnp.max(s, axis=0, keepdims=True))
        p = jnp.exp2(s - m_new)
    else:
        m_new = jnp.maximum(m_old, jnp.max(s, axis=0, keepdims=True) + shift)
        p = jnp.exp2(s - (m_new - shift))
    alpha = jnp.exp2(m_old - m_new)
    acc_ref[n] = alpha * acc_ref[n] + jnp.dot(v_t, p.astype(BF16), preferred_element_type=F32)
    m_ref[n] = m_new


def _flash_result(n, acc_ref):
    return acc_ref[n, 0:HEAD_DIM, :] / acc_ref[n, HEAD_DIM:HEAD_DIM + 1, :]


def _staggered(n_items, scores, update, ahead=MXU_LOOKAHEAD):
    pending = {n: scores(n) for n in range(min(ahead, n_items))}
    for n in range(n_items):
        if n + ahead < n_items:
            pending[n + ahead] = scores(n + ahead)
        update(n, pending.pop(n))


def _pipelined_tiles(first, n_tiles, n_chains, group, load_tile, scores, update, next_ref, left_by_previous=None,
                     last_of_sweep=False):
    ahead = next_ref.shape[0]
    n_items = group * n_chains
    assert ahead <= n_chains

    def body(trip, _, issue_next=True):
        base = first + trip * group
        tiles, pending = {}, {}
        for n in range(n_items):
            cur = next_ref[n] if n < ahead else pending.pop(n)
            if n + ahead < n_items or issue_next:
                g, c = divmod(n + ahead, n_chains)
                if g not in tiles:
                    tiles[g] = load_tile(base + g)
                new = scores(tiles[g], base + g, c)
                if n + ahead < n_items:
                    pending[n + ahead] = new
                else:
                    next_ref[n + ahead - n_items] = new
            update(base + n // n_chains, n % n_chains, cur)

    if left_by_previous is None:
        first_tile = load_tile(first)
        for n in range(ahead):
            next_ref[n] = scores(first_tile, first, n)
    else:
        for n in range(ahead):
            next_ref[n] = next_ref[n] + left_by_previous(first, n)
    n_trips = (n_tiles + group - 1) // group
    if last_of_sweep:
        lax.fori_loop(0, n_trips - 1, body, None)
        body(n_trips - 1, None, issue_next=False)
    else:
        lax.fori_loop(0, n_trips, body, None)


def _flash_scratch(chains, ahead):
    return [pltpu.VMEM((chains, 1, QT), F32), pltpu.VMEM((chains, V_ROWS, QT), F32),
            pltpu.VMEM((ahead, KT, QT), F32)]


N_NEAR = -(-(FAR_DIST + KT - 1) // QT)
N_BIAS_TILES = N_NEAR + 2
DIFF_TILE_GROUP = 2
SLC_TILE_GROUP = 2
FAR_TILE_GROUP = 4


def _bias_tile_index(i, j):
    return jnp.where(j > i, N_NEAR + 1, jnp.minimum(i - j, N_NEAR))


def _whole_far_groups(i, group):
    return jnp.maximum(i - (N_NEAR - 1), 0) // group * group


def _diff_kernel(far_ref, q_ref, k_ref, v_ref, bias_ref, lam_ref, subln_ref, o_ref, qz_ref, m_ref, acc_ref, next_ref,
                 ot_ref, *, lambda_init):
    i = pl.program_id(1)
    q = q_ref[0]
    row = lax.broadcasted_iota(jnp.int32, (GROUP, QT), 0) // DIFF_QK_DIM
    for n in range(2 * HEADS):
        qz_ref[n] = jnp.where(row == n, q, jnp.zeros_like(q))
    _flash_reset(m_ref, acc_ref)

    def load_tile(j):
        return k_ref[0, pl.ds(pl.multiple_of(jnp.minimum(j, i) * KT, KT), KT), :]

    def values(j, n):
        h = n // 2
        return v_ref[0, jnp.minimum(j, i), h * V_ROWS:(h + 1) * V_ROWS, :]

    n_far = _whole_far_groups(i, FAR_TILE_GROUP)
    _pipelined_tiles(0, n_far, 2 * HEADS, FAR_TILE_GROUP, load_tile,
                     lambda k, j, n: jnp.dot(k, qz_ref[n], preferred_element_type=F32),
                     lambda j, n, s: _flash_update(n, s, values(j, n), m_ref, acc_ref, shift=far_ref[n // 2]),
                     next_ref)

    def scores(k, j, n):
        return jnp.dot(k, qz_ref[n], preferred_element_type=F32) + bias_ref[n // 2, _bias_tile_index(i, j)]

    _pipelined_tiles(n_far, i + 1 - n_far, 2 * HEADS, DIFF_TILE_GROUP, load_tile, scores,
                     lambda j, n, s: _flash_update(n, s, values(j, n), m_ref, acc_ref), next_ref,
                     left_by_previous=lambda j, n: bias_ref[n // 2, _bias_tile_index(i, j)], last_of_sweep=True)

    lam_p = lam_ref[...]
    lam = (jnp.exp(jnp.sum(lam_p[0:1] * lam_p[1:2], axis=1, keepdims=True))
           - jnp.exp(jnp.sum(lam_p[2:3] * lam_p[3:4], axis=1, keepdims=True)) + lambda_init)
    for h in range(HEADS):
        o = _flash_result(2 * h, acc_ref) - lam * _flash_result(2 * h + 1, acc_ref)
        msq = jnp.mean(o * o, axis=0, keepdims=True)
        ot_ref[h * HEAD_DIM:(h + 1) * HEAD_DIM, :] = (o * lax.rsqrt(msq + RMS_EPS) * subln_ref[...]
                                                      * (1.0 - lambda_init))
    o_ref[0] = ot_ref[...].T.astype(o_ref.dtype)


def _diff_attention(far, q_t, k, v_t, bias, lam_p, subln, lambda_init):
    b, s, _ = k.shape
    nq = s // QT
    nkv = s // KT
    v4 = v_t.reshape(b, nkv, HEADS * V_ROWS, KT)
    return pl.pallas_call(
        functools.partial(_diff_kernel, lambda_init=lambda_init),
        grid=(b, nq),
        in_specs=[pl.BlockSpec(memory_space=pltpu.SMEM),
                  pl.BlockSpec((1, GROUP, QT), lambda bb, i: (bb * nq + i, 0, 0)),
                  pl.BlockSpec((1, s, GROUP), lambda bb, i: (bb, 0, 0)),
                  pl.BlockSpec((1, nkv, HEADS * V_ROWS, KT), lambda bb, i: (bb, 0, 0, 0)),
                  pl.BlockSpec((HEADS, N_BIAS_TILES, KT, QT), lambda bb, i: (0, 0, 0, 0)),
                  pl.BlockSpec((4, DIFF_QK_DIM), lambda bb, i: (0, 0)),
                  pl.BlockSpec((HEAD_DIM, 1), lambda bb, i: (0, 0))],
        out_specs=pl.BlockSpec((1, QT, GROUP), lambda bb, i: (bb, i, 0)),
        out_shape=jax.ShapeDtypeStruct((b, s, GROUP), BF16),
        scratch_shapes=[pltpu.VMEM((2 * HEADS, GROUP, QT), BF16)] + _flash_scratch(2 * HEADS, MXU_LOOKAHEAD)
        + [pltpu.VMEM((GROUP, QT), F32)],
        compiler_params=pltpu.CompilerParams(dimension_semantics=("arbitrary", "arbitrary"),
                                             vmem_limit_bytes=VMEM_LIMIT),
        name="diff_attention",
    )(far, q_t, k, v4, bias, lam_p, subln)


def _compress_kernel(ch_ref, ptop_ref, pbot_ref, w1t_ref, w1b_ref, b1_ref, w2k_ref, b2k_ref, w2v_ref, b2v_ref,
                     gk_ref, kc_ref, vct_ref):
    ch = ch_ref[0]
    n_c = ch.shape[0]
    u = jnp.dot((ch + ptop_ref[...]).astype(BF16), w1t_ref[...], preferred_element_type=F32)
    v = jnp.dot((ch + pbot_ref[...]).astype(BF16), w1b_ref[...], preferred_element_type=F32)
    v_next = pltpu.roll(v, n_c - 1, 0)
    hid = jax.nn.gelu(u + v_next + b1_ref[...])
    hk = hid[:, :CMP_HIDDEN].astype(BF16)
    hv = hid[:, CMP_HIDDEN:].astype(BF16)
    kc = jnp.dot(hk, w2k_ref[...], preferred_element_type=F32) + b2k_ref[...]
    msq = jnp.mean(kc * kc, axis=-1, keepdims=True)
    kc_ref[0] = (kc * lax.rsqrt(msq + RMS_EPS) * gk_ref[...]).astype(kc_ref.dtype)
    vct = lax.dot_general(w2v_ref[...], hv, NT_DIMS, preferred_element_type=F32) + b2v_ref[...]
    vct_ref[0] = vct.astype(vct_ref.dtype)


def _compress(chunks, ptop, pbot, w1t, w1b, b1, w2k, b2k, w2v, b2v, gk):
    b, n_c, width = chunks.shape
    const = lambda a: pl.BlockSpec(a.shape, lambda bb: (0,) * a.ndim)
    params = (ptop, pbot, w1t, w1b, b1, w2k, b2k, w2v, b2v, gk)
    return pl.pallas_call(
        _compress_kernel,
        grid=(b,),
        in_specs=[pl.BlockSpec((1, n_c, width), lambda bb: (bb, 0, 0))] + [const(a) for a in params],
        out_specs=[pl.BlockSpec((1, n_c, HEAD_DIM), lambda bb: (bb, 0, 0)),
                   pl.BlockSpec((1, HEAD_DIM, n_c), lambda bb: (bb, 0, 0))],
        out_shape=[jax.ShapeDtypeStruct((b, n_c, HEAD_DIM), BF16), jax.ShapeDtypeStruct((b, HEAD_DIM, n_c), BF16)],
        compiler_params=pltpu.CompilerParams(dimension_semantics=("arbitrary",), vmem_limit_bytes=VMEM_LIMIT),
        name="nsa_compress",
    )(chunks, *params)


def _cmp_attn_kernel(q_ref, kc_ref, vct_ref, bias_ref, o_ref, sel_ref, p_ref, *, n_sel):
    i = pl.program_id(0)
    kc = kc_ref[0]
    vct = vct_ref[0]
    n_c = kc.shape[0]
    n_blk = sel_ref.shape[1]
    probs = []

    def scores(h):
        return jnp.dot(kc, q_ref[0, h * HEAD_DIM:(h + 1) * HEAD_DIM, :], preferred_element_type=F32) + bias_ref[h, 0]

    def update(h, s):
        m = jnp.maximum(jnp.max(s, axis=0, keepdims=True), 0.5 * NEG_INF)
        p = jnp.exp2(s - m)
        den = jnp.sum(p, axis=0, keepdims=True)
        p = p * (1.0 / jnp.maximum(den, TINY))
        o_ref[0, h * HEAD_DIM:(h + 1) * HEAD_DIM, :] = jnp.dot(vct, p.astype(BF16),
                                                               preferred_element_type=F32).astype(o_ref.dtype)
        probs.append(p)

    _staggered(HEADS, scores, update)
    psum = (probs[0] + probs[1]) + (probs[2] + probs[3])
    per_blk = SLC_BLOCK // CMP_STRIDE
    halves = []
    for half in range(QT // LANES):
        p_ref[half, 0:8, :] = jnp.zeros((8, LANES), F32)
        p_ref[half, 8:8 + n_c, :] = psum[:, half * LANES:(half + 1) * LANES]
        p_ref[half, 8 + n_c:16 + n_c, :] = jnp.zeros((8, LANES), F32)
        acc = p_ref[half, pl.ds(7, n_blk, stride=per_blk), :]
        for t in range(per_blk):
            acc = acc + p_ref[half, pl.ds(8 + t, n_blk, stride=per_blk), :]
        halves.append(acc)
    imp = jnp.concatenate(halves, axis=1)
    blk = lax.broadcasted_iota(jnp.int32, (n_blk, QT), 0)
    cur = (i * QT + lax.broadcasted_iota(jnp.int32, (n_blk, QT), 1)) // SLC_BLOCK
    forced = (blk == 0) | (blk == cur) | (blk == cur - 1)
    val = jnp.where(forced, FORCE_SELECT, jnp.where(blk <= cur, imp, NEG_INF))
    sel = jnp.zeros((n_blk, QT), jnp.bool_)
    for _ in range(n_sel):
        top = jnp.max(val, axis=0, keepdims=True)
        idx = jnp.min(jnp.where(val == top, blk, n_blk), axis=0, keepdims=True)
        hit = blk == idx
        sel = sel | hit
        val = jnp.where(hit, -3.0e38, val)
    sel_ref[0] = jnp.where(sel, 1.0, 0.0).astype(sel_ref.dtype)


def _cmp_attention(q_t, kc, vct, bias, b):
    nt = q_t.shape[0]
    nq = nt // b
    n_c = kc.shape[1]
    n_blk = nq * QT // SLC_BLOCK
    return pl.pallas_call(
        functools.partial(_cmp_attn_kernel, n_sel=min(SLC_TOPK, n_blk)),
        grid=(nq, b),
        in_specs=[pl.BlockSpec((1, GROUP, QT), lambda i, bb: (bb * nq + i, 0, 0)),
                  pl.BlockSpec((1, n_c, HEAD_DIM), lambda i, bb: (bb, 0, 0)),
                  pl.BlockSpec((1, HEAD_DIM, n_c), lambda i, bb: (bb, 0, 0)),
                  pl.BlockSpec((HEADS, 1, n_c, QT), lambda i, bb: (0, 0, 0, i))],
        out_specs=[pl.BlockSpec((1, GROUP, QT), lambda i, bb: (bb * nq + i, 0, 0)),
                   pl.BlockSpec((1, n_blk, QT), lambda i, bb: (bb * nq + i, 0, 0))],
        out_shape=[jax.ShapeDtypeStruct((nt, GROUP, QT), BF16), jax.ShapeDtypeStruct((nt, n_blk, QT), BF16)],
        scratch_shapes=[pltpu.VMEM((QT // LANES, n_c + 16, LANES), F32)],
        compiler_params=pltpu.CompilerParams(dimension_semantics=("arbitrary", "arbitrary"),
                                             vmem_limit_bytes=VMEM_LIMIT),
        name="nsa_compressed_attention",
    )(q_t, kc, vct, bias)


SEL_REP = 8
N_WIN = -(-(NSA_WINDOW - 1 + KT - 1) // QT)


def _slc_win_kernel(far_ref, q_ref, ksw_ref, vs_ref, vw_ref, sel_ref, rep_ref, ocmp_ref, g_ref, bslc_ref, bwin_ref,
                    o_ref, qz_ref, m_ref, acc_ref, next_ref, ot_ref, mask_ref):
    i = pl.program_id(1)
    sel8 = jnp.dot(rep_ref[...], sel_ref[0], preferred_element_type=F32)
    mask_ref[...] = (sel8 - 1.0) * (-NEG_INF)
    blocks_per_tile = KT // SLC_BLOCK
    mrows = blocks_per_tile * SEL_REP
    zeros = jnp.zeros((HEAD_DIM, QT), BF16)
    for h in range(HEADS):
        qh = q_ref[0, h * HEAD_DIM:(h + 1) * HEAD_DIM, :]
        qz_ref[h] = jnp.concatenate([qh, zeros], axis=0)
        qz_ref[HEADS + h] = jnp.concatenate([zeros, qh], axis=0)
    _flash_reset(m_ref, acc_ref)

    def load_keys(j):
        return ksw_ref[0, pl.ds(pl.multiple_of(j * KT, KT), KT), :]

    def load_tile(j):
        j = jnp.minimum(j, i)
        m8 = mask_ref[pl.ds(pl.multiple_of(j * mrows, mrows), mrows), :]
        mask = jnp.broadcast_to(m8.reshape(blocks_per_tile, 1, SEL_REP, QT),
                                (blocks_per_tile, SLC_BLOCK // SEL_REP, SEL_REP, QT)).reshape(KT, QT)
        return load_keys(j), mask

    def values(j):
        return vs_ref[0, jnp.minimum(j, i)]

    n_far = _whole_far_groups(i, FAR_TILE_GROUP)
    _pipelined_tiles(0, n_far, HEADS, FAR_TILE_GROUP, load_tile,
                     lambda tile, j, h: jnp.dot(tile[0], qz_ref[h], preferred_element_type=F32) + tile[1],
                     lambda j, h, s: _flash_update(h, s, values(j), m_ref, acc_ref, shift=far_ref[h]), next_ref)

    def slc_scores(tile, j, h):
        k, mask = tile
        return jnp.dot(k, qz_ref[h], preferred_element_type=F32) + mask + bslc_ref[h, _bias_tile_index(i, j)]

    _pipelined_tiles(n_far, i + 1 - n_far, HEADS, SLC_TILE_GROUP, load_tile, slc_scores,
                     lambda j, h, s: _flash_update(h, s, values(j), m_ref, acc_ref), next_ref,
                     left_by_previous=lambda j, h: bslc_ref[h, _bias_tile_index(i, j)], last_of_sweep=True)

    def win_tile(n):
        d = N_WIN - 1 - n // HEADS
        return d, n % HEADS, jnp.maximum(i - d, 0)

    def win_scores(n):
        d, h, j = win_tile(n)
        missing = jnp.where(i < d, NEG_INF, 0.0).astype(F32)
        return jnp.dot(load_keys(j), qz_ref[HEADS + h], preferred_element_type=F32) + (bwin_ref[h, d] + missing)

    def win_update(n, s):
        _, h, j = win_tile(n)
        _flash_update(HEADS + h, s, vw_ref[0, j], m_ref, acc_ref)

    _staggered(N_WIN * HEADS, win_scores, win_update)

    for h in range(HEADS):
        g = jax.nn.sigmoid(g_ref[0, 3 * h:3 * h + 3, :])
        ot_ref[h * HEAD_DIM:(h + 1) * HEAD_DIM, :] = (g[0:1] * ocmp_ref[0, h * HEAD_DIM:(h + 1) * HEAD_DIM, :]
                                                      + g[1:2] * _flash_result(h, acc_ref)
                                                      + g[2:3] * _flash_result(HEADS + h, acc_ref))
    o_ref[0] = ot_ref[...].T.astype(o_ref.dtype)


def _slc_win_attention(far, q_t, ksw, vs_t, vw_t, sel, rep, ocmp, g_t, bslc, bwin):
    b, s, _ = ksw.shape
    nq = s // QT
    nkv = s // KT
    n_blk = s // SLC_BLOCK
    tile = lambda height: pl.BlockSpec((1, height, QT), lambda bb, i: (bb * nq + i, 0, 0))
    whole = lambda a: pl.BlockSpec(a.shape, lambda bb, i: (0,) * a.ndim)
    return pl.pallas_call(
        _slc_win_kernel,
        grid=(b, nq),
        in_specs=[pl.BlockSpec(memory_space=pltpu.SMEM), tile(GROUP),
                  pl.BlockSpec((1, s, 2 * HEAD_DIM), lambda bb, i: (bb, 0, 0)),
                  pl.BlockSpec((1, nkv, V_ROWS, KT), lambda bb, i: (bb, 0, 0, 0)),
                  pl.BlockSpec((1, nkv, V_ROWS, KT), lambda bb, i: (bb, 0, 0, 0)),
                  tile(n_blk), whole(rep), tile(GROUP), tile(16), whole(bslc), whole(bwin)],
        out_specs=pl.BlockSpec((1, QT, GROUP), lambda bb, i: (bb, i, 0)),
        out_shape=jax.ShapeDtypeStruct((b, s, GROUP), BF16),
        scratch_shapes=[pltpu.VMEM((2 * HEADS, 2 * HEAD_DIM, QT), BF16)] + _flash_scratch(2 * HEADS, MXU_LOOKAHEAD)
        + [pltpu.VMEM((GROUP, QT), F32), pltpu.VMEM((n_blk * SEL_REP, QT), F32)],
        compiler_params=pltpu.CompilerParams(dimension_semantics=("arbitrary", "arbitrary"),
                                             vmem_limit_bytes=VMEM_LIMIT),
        name="nsa_selected_window_attention",
    )(far, q_t, ksw, vs_t.reshape(b, nkv, V_ROWS, KT), vw_t.reshape(b, nkv, V_ROWS, KT), sel, rep, ocmp, g_t,
      bslc, bwin)


def _out_kernel(x_ref, a0_ref, a1_ref, a2_ref, l0_ref, l1_ref, l2_ref, ob_ref, oc_ref, od_ref, gate_ref, e_ref,
                w_ref, o_ref, unfold_ref):
    rows = x_ref.shape[0]

    def unfolded(ref, rate):
        width = ref.shape[1] // rate
        for rho in range(rate):
            for part in range(width // LANES):
                c0 = rho * width + part * LANES
                unfold_ref[part, pl.ds(rho, rows // rate, stride=rate), :] = ref[:, c0:c0 + LANES].astype(F32)
        return jnp.concatenate([unfold_ref[part] for part in range(width // LANES)], axis=1)

    a0, l0 = a0_ref[...], l0_ref[...]
    a1, l1 = unfolded(a1_ref, FOLD_RATES[0]), unfolded(l1_ref, FOLD_RATES[0])
    a2, l2 = unfolded(a2_ref, FOLD_RATES[1]), unfolded(l2_ref, FOLD_RATES[1])
    mx = jnp.maximum(jnp.maximum(l0, l1), l2)
    e0, e1, e2 = jnp.exp(l0 - mx), jnp.exp(l1 - mx), jnp.exp(l2 - mx)
    den = e0 + e1 + e2

    def per_head_lanes(w):
        hi = w.astype(BF16)
        lo = (w - hi.astype(F32)).astype(BF16)
        return (jnp.dot(hi, e_ref[...], preferred_element_type=F32)
                + jnp.dot(lo, e_ref[...], preferred_element_type=F32))

    o_a = per_head_lanes(e0 / den) * a0 + per_head_lanes(e1 / den) * a1 + per_head_lanes(e2 / den) * a2
    y = jnp.concatenate([o_a, ob_ref[...].astype(F32), oc_ref[...].astype(F32), od_ref[...].astype(F32)], axis=1)
    g = gate_ref[...].astype(F32)
    y = y * (g * jax.nn.sigmoid(g))
    o_ref[...] = x_ref[...] + jnp.dot(y.astype(BF16), w_ref[...], preferred_element_type=F32)


def _out_projection(x2, a_outs, a_lses, o_b, o_c, o_d, gate, w_out):
    m, d = x2.shape
    rowblk = lambda width: pl.BlockSpec((PROJ_ROWS, width), lambda i: (i, 0))
    folded = lambda width, rate: pl.BlockSpec((PROJ_ROWS // rate, rate * width), lambda i: (i, 0))
    head_of_lane = np.arange(GROUP) // HEAD_DIM
    expand = jnp.asarray((np.arange(LANES)[:, None] == head_of_lane[None, :]).astype(np.float32), BF16)
    return pl.pallas_call(
        _out_kernel,
        grid=(m // PROJ_ROWS,),
        in_specs=[rowblk(d)] + [folded(GROUP, rate) for rate in (1,) + FOLD_RATES]
        + [folded(LANES, rate) for rate in (1,) + FOLD_RATES] + [rowblk(GROUP)] * 3
        + [rowblk(N_MIXERS * GROUP), pl.BlockSpec((LANES, GROUP), lambda i: (0, 0)),
           pl.BlockSpec((N_MIXERS * GROUP, d), lambda i: (0, 0))],
        out_specs=rowblk(d),
        out_shape=jax.ShapeDtypeStruct((m, d), F32),
        scratch_shapes=[pltpu.VMEM((GROUP // LANES, PROJ_ROWS, LANES), F32)],
        compiler_params=pltpu.CompilerParams(dimension_semantics=("arbitrary",), vmem_limit_bytes=VMEM_LIMIT),
        name="out_projection",
    )(x2, *a_outs, *a_lses, o_b, o_c, o_d, gate, expand, w_out)


def _block_diag_mean(group):
    idx = np.arange(GROUP) // group
    return jnp.asarray((idx[:, None] == idx[None, :]).astype(np.float32) / group, BF16)


def _layer_weights(w_in, qk_gain, qk_gain_diff):
    d = w_in.shape[0]
    sizes = (GROUP,) * 3 + (GROUP, GROUP // 2, GROUP // 2) + (GROUP,) * 3 + (GROUP,) + (HEAD_DIM,) * 6 \
        + (HEADS * 3, N_MIXERS * GROUP)
    offs = np.concatenate([[0], np.cumsum(sizes)])
    col = lambda n: w_in[:, offs[n]:offs[n + 1]]
    (a_q, a_k, a_v, b_q, b_k, b_v, c_q, c_k, c_v, d_q, d_kc, d_vc, d_ks, d_vs, d_kw, d_vw, d_g, gate) = \
        [col(n) for n in range(18)]
    rep_kv = lambda w: jnp.repeat(w.reshape(d, 2, HEAD_DIM), 2, axis=1).reshape(d, GROUP)
    wrm = jnp.concatenate([a_q, a_k, a_v, b_q, rep_kv(b_k), rep_kv(b_v), c_k, d_kc, d_vc, d_ks, d_kw, gate], axis=1)
    wt = jnp.concatenate([c_q, c_v, d_q, d_vs, d_vw, d_g, jnp.zeros((d, 16 - HEADS * 3), w_in.dtype)], axis=1).T
    g = qk_gain
    ones = lambda n: jnp.ones((n,), F32)
    tile4 = lambda v: jnp.tile(v, HEADS)
    scale = HEAD_DIM ** -0.5
    grm = jnp.concatenate([tile4(g[0]) * scale, tile4(g[1]), ones(GROUP), tile4(g[2]) * scale, tile4(g[3]),
                           ones(GROUP), jnp.tile(qk_gain_diff[1], 2 * HEADS), ones(2 * HEAD_DIM), g[6], g[7],
                           ones(N_MIXERS * GROUP)])
    gt = jnp.concatenate([jnp.tile(qk_gain_diff[0], 2 * HEADS) * (DIFF_QK_DIM ** -0.5 * LOG2E), ones(GROUP),
                          tile4(g[4]) * (scale * LOG2E), ones(2 * HEAD_DIM + 16)])
    return wrm.astype(BF16), wt.astype(BF16), grm.reshape(1, -1), gt.reshape(-1, 1)


def _compress_weights(cmp_pos, cmp_w1, cmp_b1, cmp_w2, cmp_b2):
    half = CMP_LEN // 2
    pos = jnp.concatenate([cmp_pos[0], cmp_pos[1]], axis=-1)
    ptop = pos[:half].reshape(1, -1)
    pbot = pos[half:].reshape(1, -1)
    w1 = cmp_w1.reshape(2, CMP_LEN, HEAD_DIM, CMP_HIDDEN)
    zeros = jnp.zeros_like(w1[0])
    w1cat = jnp.concatenate([jnp.concatenate([w1[0], zeros], axis=-1),
                             jnp.concatenate([zeros, w1[1]], axis=-1)], axis=1)
    w1t = w1cat[:half].reshape(half * 2 * HEAD_DIM, 2 * CMP_HIDDEN).astype(BF16)
    w1b = w1cat[half:].reshape(half * 2 * HEAD_DIM, 2 * CMP_HIDDEN).astype(BF16)
    b1 = jnp.concatenate([cmp_b1[0], cmp_b1[1]]).reshape(1, -1)
    return (ptop, pbot, w1t, w1b, b1, cmp_w2[0].astype(BF16), cmp_b2[0].reshape(1, -1),
            cmp_w2[1].T.astype(BF16), cmp_b2[1].reshape(-1, 1))


def kernel(x, rel_bias_table, norm_w, w_in, w_out, qk_gain, qk_gain_diff, attn_sinks, diff_lambda, diff_subln,
           cmp_pos, cmp_w1, cmp_b1, cmp_w2, cmp_b2):
    b, s, d = x.shape
    depth = w_in.shape[0]
    n_c = s // CMP_STRIDE
    n_blk = s // SLC_BLOCK
    assert s % (BAND_TILE * DILATED_CONFIGS[-1][1]) == 0 and s % PROJ_ROWS == 0 and d == N_MIXERS * GROUP

    table = rel_bias_table.astype(F32)
    band_bias = [_build_bias(table, head0=0, n_d=1, rows=BAND_TILE, cols=2 * BAND_TILE, base0=BAND_TILE, dstep=0,
                             rs=1, cs=-1, dscale=rate, max_dist=window // rate) for window, rate in DILATED_CONFIGS]
    swa_bias = _build_bias(table, head0=HEADS, n_d=1, rows=BAND_TILE, cols=2 * BAND_TILE, base0=BAND_TILE, dstep=0,
                           rs=1, cs=-1, max_dist=SWA_WINDOW - 1)
    flash_tiles = dict(rows=KT, cols=QT, base0=0, dstep=QT, rs=-1, cs=1, scale=LOG2E)
    diff_bias = _build_bias(table, head0=2 * HEADS, n_d=N_BIAS_TILES, d_valid=N_NEAR + 1, **flash_tiles)
    slc_bias = _build_bias(table, head0=3 * HEADS, n_d=N_BIAS_TILES, d_valid=N_NEAR + 1, **flash_tiles)
    win_bias = _build_bias(table, head0=3 * HEADS, n_d=N_WIN, max_dist=NSA_WINDOW - 1, **flash_tiles)
    far_bias = table[NUM_BUCKETS - 1] * LOG2E
    cmp_bias = _build_bias(table, head0=3 * HEADS, n_d=1, rows=n_c, cols=s, base0=-(CMP_LEN - 1), dstep=0,
                           rs=-CMP_STRIDE, cs=1, r_valid=n_c - 1, col_tile=4 * QT, scale=LOG2E)
    e64, e32 = _block_diag_mean(HEAD_DIM), _block_diag_mean(DIFF_QK_DIM)
    rep_idx = np.arange(n_blk * SEL_REP) // SEL_REP
    rep = jnp.asarray((rep_idx[:, None] == np.arange(n_blk)[None, :]).astype(np.float32), BF16)
    no_sink = jnp.zeros((HEADS,), F32)

    x2 = x.reshape(b * s, d)
    for layer in range(depth):
        wrm, wt, grm, gt = _layer_weights(w_in[layer], qk_gain[layer], qk_gain_diff[layer])
        (a_q, a_k, a_v, b_q, b_k, b_v, c_k, kvc, ksw, gate, a_q4, a_k4, a_v4, a_q16, a_k16, a_v16,
         c_qt, c_vt, d_qt, d_vst, d_vwt, d_gt) = _project(x2, norm_w[layer].reshape(1, d), wrm, wt, grm, gt, e64, e32)
        seq = lambda t: t.reshape(b, s, t.shape[-1])
        per_batch = lambda t: t.reshape(b, t.shape[0] // b, t.shape[1])
        flat = lambda t: t.reshape(b * t.shape[1], t.shape[2])
        a_in = ((a_q, a_k, a_v), (a_q4, a_k4, a_v4), (a_q16, a_k16, a_v16))
        a_res = [_banded(*map(per_batch, a_in[n]), band_bias[n], no_sink, rate, False)
                 for n, (_, rate) in enumerate(DILATED_CONFIGS)]
        o_b, _ = _banded(seq(b_q), seq(b_k), seq(b_v), swa_bias, attn_sinks[layer].astype(F32), 1, True)
        lambda_init = 0.8 - 0.6 * math.exp(-0.3 * layer)
        o_c = _diff_attention(far_bias[2 * HEADS:3 * HEADS], c_qt, seq(c_k), c_vt, diff_bias, diff_lambda[layer].astype(F32),
                              diff_subln[layer].reshape(HEAD_DIM, 1).astype(F32), lambda_init)
        cw = _compress_weights(cmp_pos[layer], cmp_w1[layer], cmp_b1[layer], cmp_w2[layer], cmp_b2[layer])
        kc, vct = _compress(kvc.reshape(b, n_c, CMP_STRIDE * 2 * HEAD_DIM), *cw, qk_gain[layer, 5].reshape(1, -1))
        o_cmp, sel = _cmp_attention(d_qt, kc, vct, cmp_bias, b)
        o_d = _slc_win_attention(far_bias[3 * HEADS:4 * HEADS], d_qt, seq(ksw), d_vst, d_vwt, sel, rep, o_cmp, d_gt,
                                 slc_bias, win_bias)
        x2 = _out_projection(x2, [flat(r[0]) for r in a_res], [flat(r[1]) for r in a_res], o_b.reshape(b * s, GROUP),
                             o_c.reshape(b * s, GROUP), o_d.reshape(b * s, GROUP), gate, w_out[layer].astype(BF16))
    return x2.reshape(b, s, d)
```

```python
import functools
import math

import numpy as np
import jax
import jax.numpy as jnp
from jax import lax
from jax.experimental import pallas as pl
from jax.experimental.pallas import tpu as pltpu

F32 = jnp.float32
BF16 = jnp.bfloat16

HEAD_DIM = 64
HEADS = 4
GROUP = HEADS * HEAD_DIM
N_MIXERS = 4
NUM_BUCKETS = 32
REL_MAX_DIST = 2048
DILATED_CONFIGS = ((128, 1), (512, 4), (2048, 16))
FOLD_RATES = tuple(rate for _, rate in DILATED_CONFIGS if rate > 1)
SWA_WINDOW = 128
DIFF_QK_DIM = HEAD_DIM // 2
CMP_LEN = 32
CMP_STRIDE = 16
CMP_HIDDEN = 256
SLC_BLOCK = 64
SLC_TOPK = 16
CMP_PARTS = 4
NSA_WINDOW = 512
RMS_EPS = 1e-6
NEG_INF = -1e30
FORCE_SELECT = 1e9
TINY = 1e-30
LOG2E = math.log2(math.e)

PROJ_ROWS = 512
BAND_TILE = 128
BAND_STEP = 1024
BAND_LOOKAHEAD = 8
BIAS_ROW_CHUNK = 64
LANES = 128
QT = 256
KT = 256
VMEM_LIMIT = 56 * 1024 * 1024
MXU_LOOKAHEAD = 4
V_ROWS = HEAD_DIM + 16

NT_DIMS = (((1,), (1,)), ((), ()))


def _t5_thresholds():
    n = np.arange(0, 4 * REL_MAX_DIST)
    max_exact = NUM_BUCKETS // 2
    nf = np.maximum(n, 1).astype(np.float32)
    large = max_exact + (np.log(nf / np.float32(max_exact)) / np.float32(math.log(REL_MAX_DIST / max_exact))
                         * np.float32(NUM_BUCKETS - max_exact)).astype(np.int32)
    bucket = np.where(n < max_exact, n, np.minimum(large, NUM_BUCKETS - 1))
    return [int(np.argmax(bucket >= b)) for b in range(NUM_BUCKETS)]


T5_THRESHOLDS = _t5_thresholds()
FAR_DIST = T5_THRESHOLDS[-1]


def _bias_kernel(tbl_ref, out_ref, *, head0, base0, dstep, rs, cs, dscale, max_dist, r_valid, d_valid, col_tile,
                 scale):
    h = pl.program_id(0)
    d = pl.program_id(1)
    ct = pl.program_id(2)
    rows, cols = out_ref.shape[-2:]
    chunk = BIAS_ROW_CHUNK if rows % BIAS_ROW_CHUNK == 0 else rows
    for r0 in range(0, rows, chunk):
        origin = base0 + d * dstep + r0 * rs + ct * col_tile * cs
        corners = [origin + dr * rs + dc * cs for dr in (0, chunk - 1) for dc in (0, cols - 1)]
        lo = functools.reduce(jnp.minimum, corners)
        hi = functools.reduce(jnp.maximum, corners)
        all_masked = (hi < 0) | (lo > max_dist) | (d >= d_valid) | (r0 >= r_valid)
        all_far = (lo * dscale >= FAR_DIST) & (hi <= max_dist) & (d < d_valid) & (r0 + chunk <= r_valid)
        out = out_ref.at[0, 0, r0:r0 + chunk, :]

        @pl.when(all_masked)
        def _():
            out[...] = jnp.full((chunk, cols), NEG_INF, F32)

        @pl.when(all_far)
        def _():
            out[...] = jnp.full((chunk, cols), tbl_ref[NUM_BUCKETS - 1, head0 + h] * scale, F32)

        @pl.when(jnp.logical_not(all_masked | all_far))
        def _():
            r = lax.broadcasted_iota(jnp.int32, (chunk, cols), 0)
            c = lax.broadcasted_iota(jnp.int32, (chunk, cols), 1)
            dist = origin + r * rs + c * cs
            n = dist * dscale
            val = jnp.full((chunk, cols), tbl_ref[0, head0 + h], F32)
            for b in range(1, NUM_BUCKETS):
                val = jnp.where(n >= T5_THRESHOLDS[b], tbl_ref[b, head0 + h], val)
            valid = (dist >= 0) & (dist <= max_dist) & (r + r0 < r_valid) & (d < d_valid)
            out[...] = jnp.where(valid, val * scale, NEG_INF)


def _build_bias(table, *, head0, n_d, rows, cols, base0, dstep, rs, cs, dscale=1, max_dist=1 << 30,
                r_valid=1 << 30, d_valid=1 << 30, col_tile=None, scale=1.0):
    col_tile = cols if col_tile is None else col_tile
    kern = functools.partial(_bias_kernel, head0=head0, base0=base0, dstep=dstep, rs=rs, cs=cs, dscale=dscale,
                             max_dist=max_dist, r_valid=r_valid, d_valid=d_valid, col_tile=col_tile, scale=scale)
    return pl.pallas_call(
        kern,
        grid=(HEADS, n_d, cols // col_tile),
        in_specs=[pl.BlockSpec(memory_space=pltpu.SMEM)],
        out_specs=pl.BlockSpec((1, 1, rows, col_tile), lambda h, d, c: (h, d, 0, c)),
        out_shape=jax.ShapeDtypeStruct((HEADS, n_d, rows, cols), F32),
        name="rel_bias_tiles",
    )(table)


RM_AQ, RM_AK, RM_AV = 0, 256, 512
RM_BQ, RM_BK, RM_BV = 768, 1024, 1280
RM_CK = 1536
RM_KVC = 1792
RM_KSW = 1920
RM_GATE = 2048
RM_COLS = 3072
TR_CQ, TR_CV, TR_DQ, TR_DVS, TR_DVW, TR_DG = 0, 256, 512, 768, 832, 896
TR_ROWS = 912


def _proj_kernel(x_ref, nw_ref, wrm_ref, wt_ref, grm_ref, gt_ref, e64_ref, e32_ref,
                 aq_ref, ak_ref, av_ref, bq_ref, bk_ref, bv_ref, ck_ref, kvc_ref, ksw_ref, gate_ref,
                 aq4_ref, ak4_ref, av4_ref, aq16_ref, ak16_ref, av16_ref,
                 cq_ref, cv_ref, dq_ref, dvs_ref, dvw_ref, dg_ref, fold_ref):
    x = x_ref[...]
    ms = jnp.mean(x * x, axis=-1, keepdims=True)
    xn = (x * lax.rsqrt(ms + RMS_EPS) * nw_ref[...]).astype(BF16)
    rows = x.shape[0]

    def rm(c0, width):
        return jnp.dot(xn, wrm_ref[:, c0:c0 + width], preferred_element_type=F32)

    def rm_normed(c0, width, e_ref):
        h = rm(c0, width)
        msq = jnp.dot((h * h).astype(BF16), e_ref[0:width, 0:width], preferred_element_type=F32)
        return h * lax.rsqrt(msq + RMS_EPS) * grm_ref[:, c0:c0 + width]

    def put_folded(val, ref, folded_refs):
        ref[...] = val.astype(ref.dtype)
        for half in range(GROUP // LANES):
            fold_ref[half] = val[:, half * LANES:(half + 1) * LANES]
        for rate, fref in zip(FOLD_RATES, folded_refs):
            for rho in range(rate):
                for half in range(GROUP // LANES):
                    c0 = rho * GROUP + half * LANES
                    fref[:, c0:c0 + LANES] = fold_ref[half, pl.ds(rho, rows // rate, stride=rate), :].astype(fref.dtype)

    put_folded(rm_normed(RM_AQ, GROUP, e64_ref), aq_ref, (aq4_ref, aq16_ref))
    put_folded(rm_normed(RM_AK, GROUP, e64_ref), ak_ref, (ak4_ref, ak16_ref))
    put_folded(rm(RM_AV, GROUP), av_ref, (av4_ref, av16_ref))
    bq_ref[...] = rm_normed(RM_BQ, GROUP, e64_ref).astype(bq_ref.dtype)
    bk_ref[...] = rm_normed(RM_BK, GROUP, e64_ref).astype(bk_ref.dtype)
    bv_ref[...] = rm(RM_BV, GROUP).astype(bv_ref.dtype)
    ck_ref[...] = rm_normed(RM_CK, GROUP, e32_ref).astype(ck_ref.dtype)
    kvc_ref[...] = rm(RM_KVC, 2 * HEAD_DIM).astype(kvc_ref.dtype)
    ksw_ref[...] = rm_normed(RM_KSW, 2 * HEAD_DIM, e64_ref).astype(ksw_ref.dtype)
    gate_ref[...] = rm(RM_GATE, N_MIXERS * GROUP).astype(gate_ref.dtype)

    def tr(r0, height):
        return lax.dot_general(wt_ref[r0:r0 + height, :], xn, NT_DIMS, preferred_element_type=F32)

    def tr_normed(r0, height, group):
        h3 = tr(r0, height).reshape(height // group, group, rows)
        msq = jnp.mean(h3 * h3, axis=1, keepdims=True)
        return (h3 * lax.rsqrt(msq + RMS_EPS)).reshape(height, rows) * gt_ref[r0:r0 + height, :]

    def put(ref, val):
        for t in range(rows // QT):
            ref[t] = val[:, t * QT:(t + 1) * QT].astype(ref.dtype)

    def with_ones(v):
        ones = jnp.ones((V_ROWS - HEAD_DIM, rows), F32)
        parts = []
        for h in range(v.shape[0] // HEAD_DIM):
            parts += [v[h * HEAD_DIM:(h + 1) * HEAD_DIM], ones]
        return jnp.concatenate(parts, axis=0)

    put(cq_ref, tr_normed(TR_CQ, GROUP, DIFF_QK_DIM))
    put(cv_ref, with_ones(tr(TR_CV, GROUP)))
    put(dq_ref, tr_normed(TR_DQ, GROUP, HEAD_DIM))
    put(dvs_ref, with_ones(tr(TR_DVS, HEAD_DIM)))
    put(dvw_ref, with_ones(tr(TR_DVW, HEAD_DIM)))
    put(dg_ref, tr(TR_DG, 16))


def _project(x2, nw, wrm, wt, grm, gt, e64, e32):
    m, d = x2.shape
    nt = m // QT
    tpr = PROJ_ROWS // QT
    const = lambda shape: pl.BlockSpec(shape, lambda i: (0,) * len(shape))
    rm_out = lambda width, dtype: (jax.ShapeDtypeStruct((m, width), dtype),
                                   pl.BlockSpec((PROJ_ROWS, width), lambda i: (i, 0)))
    tr_out = lambda height, dtype: (jax.ShapeDtypeStruct((nt, height, QT), dtype),
                                    pl.BlockSpec((tpr, height, QT), lambda i: (i, 0, 0)))
    outs = [rm_out(GROUP, BF16)] * 7 + [rm_out(2 * HEAD_DIM, F32), rm_out(2 * HEAD_DIM, BF16),
                                        rm_out(N_MIXERS * GROUP, BF16)]
    fold_out = lambda rate: (jax.ShapeDtypeStruct((m // rate, rate * GROUP), BF16),
                             pl.BlockSpec((PROJ_ROWS // rate, rate * GROUP), lambda i: (i, 0)))
    outs += [fold_out(rate) for rate in FOLD_RATES for _ in range(3)]
    outs += [tr_out(GROUP, BF16), tr_out(HEADS * V_ROWS, BF16), tr_out(GROUP, BF16), tr_out(V_ROWS, BF16),
             tr_out(V_ROWS, BF16), tr_out(16, F32)]
    return pl.pallas_call(
        _proj_kernel,
        grid=(m // PROJ_ROWS,),
        in_specs=[pl.BlockSpec((PROJ_ROWS, d), lambda i: (i, 0)), const((1, d)), const((d, RM_COLS)),
                  const((TR_ROWS, d)), const((1, RM_COLS)), const((TR_ROWS, 1)), const((GROUP, GROUP)),
                  const((GROUP, GROUP))],
        out_specs=[o[1] for o in outs],
        out_shape=[o[0] for o in outs],
        scratch_shapes=[pltpu.VMEM((GROUP // LANES, PROJ_ROWS, LANES), F32)],
        compiler_params=pltpu.CompilerParams(dimension_semantics=("arbitrary",), vmem_limit_bytes=VMEM_LIMIT),
        name="in_projection",
    )(x2, nw, wrm, wt, grm, gt, e64, e32)


def _band_kernel(sink_ref, q_ref, kp_ref, kc_ref, vp_ref, vc_ref, bias_ref, o_ref, lse_ref, p_ref, *, use_sink):
    i = pl.program_id(2)
    n_blocks = q_ref.shape[1] // BAND_TILE
    head_q = lax.broadcasted_iota(jnp.int32, (BAND_TILE, GROUP), 1) // HEAD_DIM
    head_v = lax.broadcasted_iota(jnp.int32, (2 * BAND_TILE, GROUP), 1) // HEAD_DIM
    lane = lax.broadcasted_iota(jnp.int32, (BAND_TILE, LANES), 1)
    in_prev = lax.broadcasted_iota(jnp.int32, (1, 2 * BAND_TILE), 1) < BAND_TILE
    no_prev = jnp.where(in_prev & (i == 0), NEG_INF, 0.0).astype(F32)

    def window(cur_ref, prev_ref, m):
        if m == 0:
            return jnp.concatenate([prev_ref[0], cur_ref[0, 0:BAND_TILE, :]], axis=0)
        return cur_ref[0, (m - 1) * BAND_TILE:(m + 1) * BAND_TILE, :]

    def scores(n):
        m, h = divmod(n, HEADS)
        q = q_ref[0, m * BAND_TILE:(m + 1) * BAND_TILE, :]
        qh = jnp.where(head_q == h, q, jnp.zeros_like(q))
        bias = bias_ref[h, 0] + no_prev if m == 0 else bias_ref[h, 0]
        return lax.dot_general(qh, window(kc_ref, kp_ref, m), NT_DIMS, preferred_element_type=F32) + bias

    lse_tiles = {}

    def update(n, s):
        m, h = divmod(n, HEADS)
        mx = jnp.max(s, axis=1, keepdims=True)
        if use_sink:
            mx = jnp.maximum(mx, sink_ref[h])
        p = jnp.exp(s - mx)
        den = jnp.sum(p, axis=1, keepdims=True)
        if use_sink:
            den = den + jnp.exp(sink_ref[h] - mx)
        p_ref[m % 2, :, h * 2 * BAND_TILE:(h + 1) * 2 * BAND_TILE] = (p * (1.0 / den)).astype(BF16)
        lse_tiles[m] = jnp.where(lane == h, mx + jnp.log(den), lse_tiles.get(m, jnp.zeros((BAND_TILE, LANES), F32)))
        if h == HEADS - 1:
            v = window(vc_ref, vp_ref, m)
            v_heads = jnp.concatenate([jnp.where(head_v == hh, v, jnp.zeros_like(v)) for hh in range(HEADS)], axis=0)
            rows = slice(m * BAND_TILE, (m + 1) * BAND_TILE)
            o_ref[0, rows, :] = jnp.dot(p_ref[m % 2], v_heads, preferred_element_type=F32).astype(o_ref.dtype)
            lse_ref[0, rows, :] = lse_tiles.pop(m)

    _staggered(n_blocks * HEADS, scores, update, ahead=BAND_LOOKAHEAD)


def _banded(q, k, v, bias, sink, rate, use_sink):
    b, ln, _ = q.shape
    step = min(BAND_STEP, ln)
    per_step = step // BAND_TILE
    cur = pl.BlockSpec((1, step, GROUP), lambda bb, r, i: (bb, i, r))
    prev = pl.BlockSpec((1, BAND_TILE, GROUP), lambda bb, r, i: (bb, jnp.maximum(i * per_step - 1, 0), r))
    o, lse = pl.pallas_call(
        functools.partial(_band_kernel, use_sink=use_sink),
        grid=(b, rate, ln // step),
        in_specs=[pl.BlockSpec(memory_space=pltpu.SMEM), cur, prev, cur, prev, cur,
                  pl.BlockSpec((HEADS, 1, BAND_TILE, 2 * BAND_TILE), lambda bb, r, i: (0, 0, 0, 0))],
        out_specs=[cur, pl.BlockSpec((1, step, LANES), lambda bb, r, i: (bb, i, r))],
        out_shape=[jax.ShapeDtypeStruct((b, ln, rate * GROUP), BF16), jax.ShapeDtypeStruct((b, ln, rate * LANES), F32)],
        scratch_shapes=[pltpu.VMEM((2, BAND_TILE, HEADS * 2 * BAND_TILE), BF16)],
        compiler_params=pltpu.CompilerParams(dimension_semantics=("arbitrary",) * 3),
        name=f"banded_attention_r{rate}",
    )(sink, q, k, k, v, v, bias)
    return o, lse


def _flash_reset(m_ref, acc_ref):
    m_ref[...] = jnp.full(m_ref.shape, NEG_INF, F32)
    acc_ref[...] = jnp.zeros(acc_ref.shape, F32)


def _flash_update(n, s, v_t, m_ref, acc_ref, shift=None):
    m_old = m_ref[n]
    if shift is None:
        m_new = jnp.maximum(m_old, jnp.max(s, axis=0, keepdims=True))
        p = jnp.exp2(s - m_new)
    else:
        m_new = jnp.maximum(m_old, jnp.max(s, axis=0, keepdims=True) + shift)
        p = jnp.exp2(s - (m_new - shift))
    alpha = jnp.exp2(m_old - m_new)
    acc_ref[n] = alpha * acc_ref[n] + jnp.dot(v_t, p.astype(BF16), preferred_element_type=F32)
    m_ref[n] = m_new


def _flash_result(n, acc_ref):
    return acc_ref[n, 0:HEAD_DIM, :] / acc_ref[n, HEAD_DIM:HEAD_DIM + 1, :]


def _staggered(n_items, scores, update, ahead=MXU_LOOKAHEAD):
    pending = {n: scores(n) for n in range(min(ahead, n_items))}
    for n in range(n_items):
        if n + ahead < n_items:
            pending[n + ahead] = scores(n + ahead)
        update(n, pending.pop(n))


def _pipelined_tiles(first, n_tiles, n_chains, group, load_tile, scores, update, next_ref, left_by_previous=None,
                     last_of_sweep=False):
    ahead = next_ref.shape[0]
    n_items = group * n_chains
    assert ahead <= n_chains

    def body(trip, _, issue_next=True):
        base = first + trip * group
        tiles, pending = {}, {}
        for n in range(n_items):
            cur = next_ref[n] if n < ahead else pending.pop(n)
            if n + ahead < n_items or issue_next:
                g, c = divmod(n + ahead, n_chains)
                if g not in tiles:
                    tiles[g] = load_tile(base + g)
                new = scores(tiles[g], base + g, c)
                if n + ahead < n_items:
                    pending[n + ahead] = new
                else:
                    next_ref[n + ahead - n_items] = new
            update(base + n // n_chains, n % n_chains, cur)

    if left_by_previous is None:
        first_tile = load_tile(first)
        for n in range(ahead):
            next_ref[n] = scores(first_tile, first, n)
    else:
        for n in range(ahead):
            next_ref[n] = next_ref[n] + left_by_previous(first, n)
    n_trips = (n_tiles + group - 1) // group
    if last_of_sweep:
        lax.fori_loop(0, n_trips - 1, body, None)
        body(n_trips - 1, None, issue_next=False)
    else:
        lax.fori_loop(0, n_trips, body, None)


def _flash_scratch(chains, ahead):
    return [pltpu.VMEM((chains, 1, QT), F32), pltpu.VMEM((chains, V_ROWS, QT), F32),
            pltpu.VMEM((ahead, KT, QT), F32)]


N_NEAR = -(-(FAR_DIST + KT - 1) // QT)
N_BIAS_TILES = N_NEAR + 2
DIFF_TILE_GROUP = 2
SLC_TILE_GROUP = 2
FAR_TILE_GROUP = 4


def _bias_tile_index(i, j):
    return jnp.where(j > i, N_NEAR + 1, jnp.minimum(i - j, N_NEAR))


def _whole_far_groups(i, group):
    return jnp.maximum(i - (N_NEAR - 1), 0) // group * group


def _diff_kernel(far_ref, q_ref, k_ref, v_ref, bias_ref, lam_ref, subln_ref, o_ref, qz_ref, m_ref, acc_ref, next_ref,
                 ot_ref, *, lambda_init):
    i = pl.program_id(1)
    q = q_ref[0]
    row = lax.broadcasted_iota(jnp.int32, (GROUP, QT), 0) // DIFF_QK_DIM
    for n in range(2 * HEADS):
        qz_ref[n] = jnp.where(row == n, q, jnp.zeros_like(q))
    _flash_reset(m_ref, acc_ref)

    def load_tile(j):
        return k_ref[0, pl.ds(pl.multiple_of(jnp.minimum(j, i) * KT, KT), KT), :]

    def values(j, n):
        h = n // 2
        return v_ref[0, jnp.minimum(j, i), h * V_ROWS:(h + 1) * V_ROWS, :]

    n_far = _whole_far_groups(i, FAR_TILE_GROUP)
    _pipelined_tiles(0, n_far, 2 * HEADS, FAR_TILE_GROUP, load_tile,
                     lambda k, j, n: jnp.dot(k, qz_ref[n], preferred_element_type=F32),
                     lambda j, n, s: _flash_update(n, s, values(j, n), m_ref, acc_ref, shift=far_ref[n // 2]),
                     next_ref)

    def scores(k, j, n):
        return jnp.dot(k, qz_ref[n], preferred_element_type=F32) + bias_ref[n // 2, _bias_tile_index(i, j)]

    _pipelined_tiles(n_far, i + 1 - n_far, 2 * HEADS, DIFF_TILE_GROUP, load_tile, scores,
                     lambda j, n, s: _flash_update(n, s, values(j, n), m_ref, acc_ref), next_ref,
                     left_by_previous=lambda j, n: bias_ref[n // 2, _bias_tile_index(i, j)], last_of_sweep=True)

    lam_p = lam_ref[...]
    lam = (jnp.exp(jnp.sum(lam_p[0:1] * lam_p[1:2], axis=1, keepdims=True))
           - jnp.exp(jnp.sum(lam_p[2:3] * lam_p[3:4], axis=1, keepdims=True)) + lambda_init)
    for h in range(HEADS):
        o = _flash_result(2 * h, acc_ref) - lam * _flash_result(2 * h + 1, acc_ref)
        msq = jnp.mean(o * o, axis=0, keepdims=True)
        ot_ref[h * HEAD_DIM:(h + 1) * HEAD_DIM, :] = (o * lax.rsqrt(msq + RMS_EPS) * subln_ref[...]
                                                      * (1.0 - lambda_init))
    o_ref[0] = ot_ref[...].T.astype(o_ref.dtype)


def _diff_attention(far, q_t, k, v_t, bias, lam_p, subln, lambda_init):
    b, s, _ = k.shape
    nq = s // QT
    nkv = s // KT
    v4 = v_t.reshape(b, nkv, HEADS * V_ROWS, KT)
    return pl.pallas_call(
        functools.partial(_diff_kernel, lambda_init=lambda_init),
        grid=(b, nq),
        in_specs=[pl.BlockSpec(memory_space=pltpu.SMEM),
                  pl.BlockSpec((1, GROUP, QT), lambda bb, i: (bb * nq + i, 0, 0)),
                  pl.BlockSpec((1, s, GROUP), lambda bb, i: (bb, 0, 0)),
                  pl.BlockSpec((1, nkv, HEADS * V_ROWS, KT), lambda bb, i: (bb, 0, 0, 0)),
                  pl.BlockSpec((HEADS, N_BIAS_TILES, KT, QT), lambda bb, i: (0, 0, 0, 0)),
                  pl.BlockSpec((4, DIFF_QK_DIM), lambda bb, i: (0, 0)),
                  pl.BlockSpec((HEAD_DIM, 1), lambda bb, i: (0, 0))],
        out_specs=pl.BlockSpec((1, QT, GROUP), lambda bb, i: (bb, i, 0)),
        out_shape=jax.ShapeDtypeStruct((b, s, GROUP), BF16),
        scratch_shapes=[pltpu.VMEM((2 * HEADS, GROUP, QT), BF16)] + _flash_scratch(2 * HEADS, MXU_LOOKAHEAD)
        + [pltpu.VMEM((GROUP, QT), F32)],
        compiler_params=pltpu.CompilerParams(dimension_semantics=("arbitrary", "arbitrary"),
                                             vmem_limit_bytes=VMEM_LIMIT),
        name="diff_attention",
    )(far, q_t, k, v4, bias, lam_p, subln)


def _compress_kernel(ch_ref, ptop_ref, pbot_ref, w1t_ref, w1b_ref, b1_ref, w2k_ref, b2k_ref, w2v_ref, b2v_ref,
                     gk_ref, kc_ref, vct_ref):
    ch = ch_ref[0]
    n_c = ch.shape[0]
    u = jnp.dot((ch + ptop_ref[...]).astype(BF16), w1t_ref[...], preferred_element_type=F32)
    v = jnp.dot((ch + pbot_ref[...]).astype(BF16), w1b_ref[...], preferred_element_type=F32)
    v_next = pltpu.roll(v, n_c - 1, 0)
    hid = jax.nn.gelu(u + v_next + b1_ref[...])
    hk = hid[:, :CMP_HIDDEN].astype(BF16)
    hv = hid[:, CMP_HIDDEN:].astype(BF16)
    kc = jnp.dot(hk, w2k_ref[...], preferred_element_type=F32) + b2k_ref[...]
    msq = jnp.mean(kc * kc, axis=-1, keepdims=True)
    kc_ref[0] = (kc * lax.rsqrt(msq + RMS_EPS) * gk_ref[...]).astype(kc_ref.dtype)
    vct = lax.dot_general(w2v_ref[...], hv, NT_DIMS, preferred_element_type=F32) + b2v_ref[...]
    vct_ref[0] = vct.astype(vct_ref.dtype)


def _compress(chunks, ptop, pbot, w1t, w1b, b1, w2k, b2k, w2v, b2v, gk):
    b, n_c, width = chunks.shape
    const = lambda a: pl.BlockSpec(a.shape, lambda bb: (0,) * a.ndim)
    params = (ptop, pbot, w1t, w1b, b1, w2k, b2k, w2v, b2v, gk)
    return pl.pallas_call(
        _compress_kernel,
        grid=(b,),
        in_specs=[pl.BlockSpec((1, n_c, width), lambda bb: (bb, 0, 0))] + [const(a) for a in params],
        out_specs=[pl.BlockSpec((1, n_c, HEAD_DIM), lambda bb: (bb, 0, 0)),
                   pl.BlockSpec((1, HEAD_DIM, n_c), lambda bb: (bb, 0, 0))],
        out_shape=[jax.ShapeDtypeStruct((b, n_c, HEAD_DIM), BF16), jax.ShapeDtypeStruct((b, HEAD_DIM, n_c), BF16)],
        compiler_params=pltpu.CompilerParams(dimension_semantics=("arbitrary",), vmem_limit_bytes=VMEM_LIMIT),
        name="nsa_compress",
    )(chunks, *params)


def _cmp_attn_kernel(q_ref, kc_ref, vct_ref, bias_ref, o_ref, sel_ref, p_ref, *, n_sel):
    i = pl.program_id(0)
    n_tiles = pl.num_programs(0)
    for part in range(1, CMP_PARTS + 1):
        @pl.when((i * CMP_PARTS >= (part - 1) * n_tiles) & (i * CMP_PARTS < part * n_tiles))
        def _(part=part):
            _cmp_attn_body(i, kc_ref.shape[1] * part // CMP_PARTS, q_ref, kc_ref, vct_ref, bias_ref, o_ref, sel_ref,
                           p_ref, n_sel)


def _cmp_attn_body(i, n_c, q_ref, kc_ref, vct_ref, bias_ref, o_ref, sel_ref, p_ref, n_sel):
    kc = kc_ref[0, 0:n_c, :]
    vct = vct_ref[0, :, 0:n_c]
    n_blk = n_c * CMP_STRIDE // SLC_BLOCK
    probs = []

    def scores(h):
        return (jnp.dot(kc, q_ref[0, h * HEAD_DIM:(h + 1) * HEAD_DIM, :], preferred_element_type=F32)
                + bias_ref[h, 0, 0:n_c, :])

    def update(h, s):
        m = jnp.maximum(jnp.max(s, axis=0, keepdims=True), 0.5 * NEG_INF)
        p = jnp.exp2(s - m)
        den = jnp.sum(p, axis=0, keepdims=True)
        p = p * (1.0 / jnp.maximum(den, TINY))
        o_ref[0, h * HEAD_DIM:(h + 1) * HEAD_DIM, :] = jnp.dot(vct, p.astype(BF16),
                                                               preferred_element_type=F32).astype(o_ref.dtype)
        probs.append(p)

    _staggered(HEADS, scores, update)
    psum = (probs[0] + probs[1]) + (probs[2] + probs[3])
    per_blk = SLC_BLOCK // CMP_STRIDE
    halves = []
    for half in range(QT // LANES):
        p_ref[half, 0:8, :] = jnp.zeros((8, LANES), F32)
        p_ref[half, 8:8 + n_c, :] = psum[:, half * LANES:(half + 1) * LANES]
        p_ref[half, 8 + n_c:16 + n_c, :] = jnp.zeros((8, LANES), F32)
        acc = p_ref[half, pl.ds(7, n_blk, stride=per_blk), :]
        for t in range(per_blk):
            acc = acc + p_ref[half, pl.ds(8 + t, n_blk, stride=per_blk), :]
        halves.append(acc)
    imp = jnp.concatenate(halves, axis=1)
    blk = lax.broadcasted_iota(jnp.int32, (n_blk, QT), 0)
    cur = (i * QT + lax.broadcasted_iota(jnp.int32, (n_blk, QT), 1)) // SLC_BLOCK
    forced = (blk == 0) | (blk == cur) | (blk == cur - 1)
    val = jnp.where(forced, FORCE_SELECT, jnp.where(blk <= cur, imp, NEG_INF))
    sel = jnp.zeros((n_blk, QT), jnp.bool_)
    for _ in range(n_sel):
        top = jnp.max(val, axis=0, keepdims=True)
        idx = jnp.min(jnp.where(val == top, blk, n_blk), axis=0, keepdims=True)
        hit = blk == idx
        sel = sel | hit
        val = jnp.where(hit, -3.0e38, val)
    sel_ref[0, 0:n_blk, :] = jnp.where(sel, 1.0, 0.0).astype(sel_ref.dtype)
    if n_blk < sel_ref.shape[1]:
        sel_ref[0, n_blk:, :] = jnp.zeros((sel_ref.shape[1] - n_blk, QT), sel_ref.dtype)


def _cmp_attention(q_t, kc, vct, bias, b):
    nt = q_t.shape[0]
    nq = nt // b
    n_c = kc.shape[1]
    n_blk = nq * QT // SLC_BLOCK
    return pl.pallas_call(
        functools.partial(_cmp_attn_kernel, n_sel=min(SLC_TOPK, n_blk)),
        grid=(nq, b),
        in_specs=[pl.BlockSpec((1, GROUP, QT), lambda i, bb: (bb * nq + i, 0, 0)),
                  pl.BlockSpec((1, n_c, HEAD_DIM), lambda i, bb: (bb, 0, 0)),
                  pl.BlockSpec((1, HEAD_DIM, n_c), lambda i, bb: (bb, 0, 0)),
                  pl.BlockSpec((HEADS, 1, n_c, QT), lambda i, bb: (0, 0, 0, i))],
        out_specs=[pl.BlockSpec((1, GROUP, QT), lambda i, bb: (bb * nq + i, 0, 0)),
                   pl.BlockSpec((1, n_blk, QT), lambda i, bb: (bb * nq + i, 0, 0))],
        out_shape=[jax.ShapeDtypeStruct((nt, GROUP, QT), BF16), jax.ShapeDtypeStruct((nt, n_blk, QT), BF16)],
        scratch_shapes=[pltpu.VMEM((QT // LANES, n_c + 16, LANES), F32)],
        compiler_params=pltpu.CompilerParams(dimension_semantics=("arbitrary", "arbitrary"),
                                             vmem_limit_bytes=VMEM_LIMIT),
        name="nsa_compressed_attention",
    )(q_t, kc, vct, bias)


SEL_REP = 8
N_WIN = -(-(NSA_WINDOW - 1 + KT - 1) // QT)


def _slc_win_kernel(far_ref, q_ref, ksw_ref, vs_ref, vw_ref, sel_ref, rep_ref, ocmp_ref, g_ref, bslc_ref, bwin_ref,
                    o_ref, qz_ref, m_ref, acc_ref, next_ref, ot_ref, mask_ref):
    i = pl.program_id(1)
    sel8 = jnp.dot(rep_ref[...], sel_ref[0], preferred_element_type=F32)
    mask_ref[...] = (sel8 - 1.0) * (-NEG_INF)
    blocks_per_tile = KT // SLC_BLOCK
    mrows = blocks_per_tile * SEL_REP
    zeros = jnp.zeros((HEAD_DIM, QT), BF16)
    for h in range(HEADS):
        qh = q_ref[0, h * HEAD_DIM:(h + 1) * HEAD_DIM, :]
        qz_ref[h] = jnp.concatenate([qh, zeros], axis=0)
        qz_ref[HEADS + h] = jnp.concatenate([zeros, qh], axis=0)
    _flash_reset(m_ref, acc_ref)

    def load_keys(j):
        return ksw_ref[0, pl.ds(pl.multiple_of(j * KT, KT), KT), :]

    def load_tile(j):
        j = jnp.minimum(j, i)
        m8 = mask_ref[pl.ds(pl.multiple_of(j * mrows, mrows), mrows), :]
        mask = jnp.broadcast_to(m8.reshape(blocks_per_tile, 1, SEL_REP, QT),
                                (blocks_per_tile, SLC_BLOCK // SEL_REP, SEL_REP, QT)).reshape(KT, QT)
        return load_keys(j), mask

    def values(j):
        return vs_ref[0, jnp.minimum(j, i)]

    n_far = _whole_far_groups(i, FAR_TILE_GROUP)
    _pipelined_tiles(0, n_far, HEADS, FAR_TILE_GROUP, load_tile,
                     lambda tile, j, h: jnp.dot(tile[0], qz_ref[h], preferred_element_type=F32) + tile[1],
                     lambda j, h, s: _flash_update(h, s, values(j), m_ref, acc_ref, shift=far_ref[h]), next_ref)

    def slc_scores(tile, j, h):
        k, mask = tile
        return jnp.dot(k, qz_ref[h], preferred_element_type=F32) + mask + bslc_ref[h, _bias_tile_index(i, j)]

    _pipelined_tiles(n_far, i + 1 - n_far, HEADS, SLC_TILE_GROUP, load_tile, slc_scores,
                     lambda j, h, s: _flash_update(h, s, values(j), m_ref, acc_ref), next_ref,
                     left_by_previous=lambda j, h: bslc_ref[h, _bias_tile_index(i, j)], last_of_sweep=True)

    def win_tile(n):
        d = N_WIN - 1 - n // HEADS
        return d, n % HEADS, jnp.maximum(i - d, 0)

    def win_scores(n):
        d, h, j = win_tile(n)
        missing = jnp.where(i < d, NEG_INF, 0.0).astype(F32)
        return jnp.dot(load_keys(j), qz_ref[HEADS + h], preferred_element_type=F32) + (bwin_ref[h, d] + missing)

    def win_update(n, s):
        _, h, j = win_tile(n)
        _flash_update(HEADS + h, s, vw_ref[0, j], m_ref, acc_ref)

    _staggered(N_WIN * HEADS, win_scores, win_update)

    for h in range(HEADS):
        g = jax.nn.sigmoid(g_ref[0, 3 * h:3 * h + 3, :])
        ot_ref[h * HEAD_DIM:(h + 1) * HEAD_DIM, :] = (g[0:1] * ocmp_ref[0, h * HEAD_DIM:(h + 1) * HEAD_DIM, :]
                                                      + g[1:2] * _flash_result(h, acc_ref)
                                                      + g[2:3] * _flash_result(HEADS + h, acc_ref))
    o_ref[0] = ot_ref[...].T.astype(o_ref.dtype)


def _slc_win_attention(far, q_t, ksw, vs_t, vw_t, sel, rep, ocmp, g_t, bslc, bwin):
    b, s, _ = ksw.shape
    nq = s // QT
    nkv = s // KT
    n_blk = s // SLC_BLOCK
    tile = lambda height: pl.BlockSpec((1, height, QT), lambda bb, i: (bb * nq + i, 0, 0))
    whole = lambda a: pl.BlockSpec(a.shape, lambda bb, i: (0,) * a.ndim)
    return pl.pallas_call(
        _slc_win_kernel,
        grid=(b, nq),
        in_specs=[pl.BlockSpec(memory_space=pltpu.SMEM), tile(GROUP),
                  pl.BlockSpec((1, s, 2 * HEAD_DIM), lambda bb, i: (bb, 0, 0)),
                  pl.BlockSpec((1, nkv, V_ROWS, KT), lambda bb, i: (bb, 0, 0, 0)),
                  pl.BlockSpec((1, nkv, V_ROWS, KT), lambda bb, i: (bb, 0, 0, 0)),
                  tile(n_blk), whole(rep), tile(GROUP), tile(16), whole(bslc), whole(bwin)],
        out_specs=pl.BlockSpec((1, QT, GROUP), lambda bb, i: (bb, i, 0)),
        out_shape=jax.ShapeDtypeStruct((b, s, GROUP), BF16),
        scratch_shapes=[pltpu.VMEM((2 * HEADS, 2 * HEAD_DIM, QT), BF16)] + _flash_scratch(2 * HEADS, MXU_LOOKAHEAD)
        + [pltpu.VMEM((GROUP, QT), F32), pltpu.VMEM((n_blk * SEL_REP, QT), F32)],
        compiler_params=pltpu.CompilerParams(dimension_semantics=("arbitrary", "arbitrary"),
                                             vmem_limit_bytes=VMEM_LIMIT),
        name="nsa_selected_window_attention",
    )(far, q_t, ksw, vs_t.reshape(b, nkv, V_ROWS, KT), vw_t.reshape(b, nkv, V_ROWS, KT), sel, rep, ocmp, g_t,
      bslc, bwin)


def _out_kernel(x_ref, a0_ref, a1_ref, a2_ref, l0_ref, l1_ref, l2_ref, ob_ref, oc_ref, od_ref, gate_ref, e_ref,
                w_ref, o_ref, unfold_ref):
    rows = x_ref.shape[0]

    def unfolded(ref, rate):
        width = ref.shape[1] // rate
        for rho in range(rate):
            for part in range(width // LANES):
                c0 = rho * width + part * LANES
                unfold_ref[part, pl.ds(rho, rows // rate, stride=rate), :] = ref[:, c0:c0 + LANES].astype(F32)
        return jnp.concatenate([unfold_ref[part] for part in range(width // LANES)], axis=1)

    a0, l0 = a0_ref[...], l0_ref[...]
    a1, l1 = unfolded(a1_ref, FOLD_RATES[0]), unfolded(l1_ref, FOLD_RATES[0])
    a2, l2 = unfolded(a2_ref, FOLD_RATES[1]), unfolded(l2_ref, FOLD_RATES[1])
    mx = jnp.maximum(jnp.maximum(l0, l1), l2)
    e0, e1, e2 = jnp.exp(l0 - mx), jnp.exp(l1 - mx), jnp.exp(l2 - mx)
    den = e0 + e1 + e2

    def per_head_lanes(w):
        hi = w.astype(BF16)
        lo = (w - hi.astype(F32)).astype(BF16)
        return (jnp.dot(hi, e_ref[...], preferred_element_type=F32)
                + jnp.dot(lo, e_ref[...], preferred_element_type=F32))

    o_a = per_head_lanes(e0 / den) * a0 + per_head_lanes(e1 / den) * a1 + per_head_lanes(e2 / den) * a2
    y = jnp.concatenate([o_a, ob_ref[...].astype(F32), oc_ref[...].astype(F32), od_ref[...].astype(F32)], axis=1)
    g = gate_ref[...].astype(F32)
    y = y * (g * jax.nn.sigmoid(g))
    o_ref[...] = x_ref[...] + jnp.dot(y.astype(BF16), w_ref[...], preferred_element_type=F32)


def _out_projection(x2, a_outs, a_lses, o_b, o_c, o_d, gate, w_out):
    m, d = x2.shape
    rowblk = lambda width: pl.BlockSpec((PROJ_ROWS, width), lambda i: (i, 0))
    folded = lambda width, rate: pl.BlockSpec((PROJ_ROWS // rate, rate * width), lambda i: (i, 0))
    head_of_lane = np.arange(GROUP) // HEAD_DIM
    expand = jnp.asarray((np.arange(LANES)[:, None] == head_of_lane[None, :]).astype(np.float32), BF16)
    return pl.pallas_call(
        _out_kernel,
        grid=(m // PROJ_ROWS,),
        in_specs=[rowblk(d)] + [folded(GROUP, rate) for rate in (1,) + FOLD_RATES]
        + [folded(LANES, rate) for rate in (1,) + FOLD_RATES] + [rowblk(GROUP)] * 3
        + [rowblk(N_MIXERS * GROUP), pl.BlockSpec((LANES, GROUP), lambda i: (0, 0)),
           pl.BlockSpec((N_MIXERS * GROUP, d), lambda i: (0, 0))],
        out_specs=rowblk(d),
        out_shape=jax.ShapeDtypeStruct((m, d), F32),
        scratch_shapes=[pltpu.VMEM((GROUP // LANES, PROJ_ROWS, LANES), F32)],
        compiler_params=pltpu.CompilerParams(dimension_semantics=("arbitrary",), vmem_limit_bytes=VMEM_LIMIT),
        name="out_projection",
    )(x2, *a_outs, *a_lses, o_b, o_c, o_d, gate, expand, w_out)


def _block_diag_mean(group):
    idx = np.arange(GROUP) // group
    return jnp.asarray((idx[:, None] == idx[None, :]).astype(np.float32) / group, BF16)


def _layer_weights(w_in, qk_gain, qk_gain_diff):
    d = w_in.shape[0]
    sizes = (GROUP,) * 3 + (GROUP, GROUP // 2, GROUP // 2) + (GROUP,) * 3 + (GROUP,) + (HEAD_DIM,) * 6 \
        + (HEADS * 3, N_MIXERS * GROUP)
    offs = np.concatenate([[0], np.cumsum(sizes)])
    col = lambda n: w_in[:, offs[n]:offs[n + 1]]
    (a_q, a_k, a_v, b_q, b_k, b_v, c_q, c_k, c_v, d_q, d_kc, d_vc, d_ks, d_vs, d_kw, d_vw, d_g, gate) = \
        [col(n) for n in range(18)]
    rep_kv = lambda w: jnp.repeat(w.reshape(d, 2, HEAD_DIM), 2, axis=1).reshape(d, GROUP)
    wrm = jnp.concatenate([a_q, a_k, a_v, b_q, rep_kv(b_k), rep_kv(b_v), c_k, d_kc, d_vc, d_ks, d_kw, gate], axis=1)
    wt = jnp.concatenate([c_q, c_v, d_q, d_vs, d_vw, d_g, jnp.zeros((d, 16 - HEADS * 3), w_in.dtype)], axis=1).T
    g = qk_gain
    ones = lambda n: jnp.ones((n,), F32)
    tile4 = lambda v: jnp.tile(v, HEADS)
    scale = HEAD_DIM ** -0.5
    grm = jnp.concatenate([tile4(g[0]) * scale, tile4(g[1]), ones(GROUP), tile4(g[2]) * scale, tile4(g[3]),
                           ones(GROUP), jnp.tile(qk_gain_diff[1], 2 * HEADS), ones(2 * HEAD_DIM), g[6], g[7],
                           ones(N_MIXERS * GROUP)])
    gt = jnp.concatenate([jnp.tile(qk_gain_diff[0], 2 * HEADS) * (DIFF_QK_DIM ** -0.5 * LOG2E), ones(GROUP),
                          tile4(g[4]) * (scale * LOG2E), ones(2 * HEAD_DIM + 16)])
    return wrm.astype(BF16), wt.astype(BF16), grm.reshape(1, -1), gt.reshape(-1, 1)


def _compress_weights(cmp_pos, cmp_w1, cmp_b1, cmp_w2, cmp_b2):
    half = CMP_LEN // 2
    pos = jnp.concatenate([cmp_pos[0], cmp_pos[1]], axis=-1)
    ptop = pos[:half].reshape(1, -1)
    pbot = pos[half:].reshape(1, -1)
    w1 = cmp_w1.reshape(2, CMP_LEN, HEAD_DIM, CMP_HIDDEN)
    zeros = jnp.zeros_like(w1[0])
    w1cat = jnp.concatenate([jnp.concatenate([w1[0], zeros], axis=-1),
                             jnp.concatenate([zeros, w1[1]], axis=-1)], axis=1)
    w1t = w1cat[:half].reshape(half * 2 * HEAD_DIM, 2 * CMP_HIDDEN).astype(BF16)
    w1b = w1cat[half:].reshape(half * 2 * HEAD_DIM, 2 * CMP_HIDDEN).astype(BF16)
    b1 = jnp.concatenate([cmp_b1[0], cmp_b1[1]]).reshape(1, -1)
    return (ptop, pbot, w1t, w1b, b1, cmp_w2[0].astype(BF16), cmp_b2[0].reshape(1, -1),
            cmp_w2[1].T.astype(BF16), cmp_b2[1].reshape(-1, 1))


def kernel(x, rel_bias_table, norm_w, w_in, w_out, qk_gain, qk_gain_diff, attn_sinks, diff_lambda, diff_subln,
           cmp_pos, cmp_w1, cmp_b1, cmp_w2, cmp_b2):
    b, s, d = x.shape
    depth = w_in.shape[0]
    n_c = s // CMP_STRIDE
    n_blk = s // SLC_BLOCK
    assert s % (BAND_TILE * DILATED_CONFIGS[-1][1]) == 0 and s % PROJ_ROWS == 0 and d == N_MIXERS * GROUP

    table = rel_bias_table.astype(F32)
    band_bias = [_build_bias(table, head0=0, n_d=1, rows=BAND_TILE, cols=2 * BAND_TILE, base0=BAND_TILE, dstep=0,
                             rs=1, cs=-1, dscale=rate, max_dist=window // rate) for window, rate in DILATED_CONFIGS]
    swa_bias = _build_bias(table, head0=HEADS, n_d=1, rows=BAND_TILE, cols=2 * BAND_TILE, base0=BAND_TILE, dstep=0,
                           rs=1, cs=-1, max_dist=SWA_WINDOW - 1)
    flash_tiles = dict(rows=KT, cols=QT, base0=0, dstep=QT, rs=-1, cs=1, scale=LOG2E)
    diff_bias = _build_bias(table, head0=2 * HEADS, n_d=N_BIAS_TILES, d_valid=N_NEAR + 1, **flash_tiles)
    slc_bias = _build_bias(table, head0=3 * HEADS, n_d=N_BIAS_TILES, d_valid=N_NEAR + 1, **flash_tiles)
    win_bias = _build_bias(table, head0=3 * HEADS, n_d=N_WIN, max_dist=NSA_WINDOW - 1, **flash_tiles)
    far_bias = table[NUM_BUCKETS - 1] * LOG2E
    cmp_bias = _build_bias(table, head0=3 * HEADS, n_d=1, rows=n_c, cols=s, base0=-(CMP_LEN - 1), dstep=0,
                           rs=-CMP_STRIDE, cs=1, r_valid=n_c - 1, col_tile=4 * QT, scale=LOG2E)
    e64, e32 = _block_diag_mean(HEAD_DIM), _block_diag_mean(DIFF_QK_DIM)
    rep_idx = np.arange(n_blk * SEL_REP) // SEL_REP
    rep = jnp.asarray((rep_idx[:, None] == np.arange(n_blk)[None, :]).astype(np.float32), BF16)
    no_sink = jnp.zeros((HEADS,), F32)

    x2 = x.reshape(b * s, d)
    for layer in range(depth):
        wrm, wt, grm, gt = _layer_weights(w_in[layer], qk_gain[layer], qk_gain_diff[layer])
        (a_q, a_k, a_v, b_q, b_k, b_v, c_k, kvc, ksw, gate, a_q4, a_k4, a_v4, a_q16, a_k16, a_v16,
         c_qt, c_vt, d_qt, d_vst, d_vwt, d_gt) = _project(x2, norm_w[layer].reshape(1, d), wrm, wt, grm, gt, e64, e32)
        seq = lambda t: t.reshape(b, s, t.shape[-1])
        per_batch = lambda t: t.reshape(b, t.shape[0] // b, t.shape[1])
        flat = lambda t: t.reshape(b * t.shape[1], t.shape[2])
        a_in = ((a_q, a_k, a_v), (a_q4, a_k4, a_v4), (a_q16, a_k16, a_v16))
        a_res = [_banded(*map(per_batch, a_in[n]), band_bias[n], no_sink, rate, False)
                 for n, (_, rate) in enumerate(DILATED_CONFIGS)]
        o_b, _ = _banded(seq(b_q), seq(b_k), seq(b_v), swa_bias, attn_sinks[layer].astype(F32), 1, True)
        lambda_init = 0.8 - 0.6 * math.exp(-0.3 * layer)
        o_c = _diff_attention(far_bias[2 * HEADS:3 * HEADS], c_qt, seq(c_k), c_vt, diff_bias, diff_lambda[layer].astype(F32),
                              diff_subln[layer].reshape(HEAD_DIM, 1).astype(F32), lambda_init)
        cw = _compress_weights(cmp_pos[layer], cmp_w1[layer], cmp_b1[layer], cmp_w2[layer], cmp_b2[layer])
        kc, vct = _compress(kvc.reshape(b, n_c, CMP_STRIDE * 2 * HEAD_DIM), *cw, qk_gain[layer, 5].reshape(1, -1))
        o_cmp, sel = _cmp_attention(d_qt, kc, vct, cmp_bias, b)
        o_d = _slc_win_attention(far_bias[3 * HEADS:4 * HEADS], d_qt, seq(ksw), d_vst, d_vwt, sel, rep, o_cmp, d_gt,
                                 slc_bias, win_bias)
        x2 = _out_projection(x2, [flat(r[0]) for r in a_res], [flat(r[1]) for r in a_res], o_b.reshape(b * s, GROUP),
                             o_c.reshape(b * s, GROUP), o_d.reshape(b * s, GROUP), gate, w_out[layer].astype(BF16))
    return x2.reshape(b, s, d)
```

```python
import functools
import math

import numpy as np
import jax
import jax.numpy as jnp
from jax import lax
from jax.experimental import pallas as pl
from jax.experimental.pallas import tpu as pltpu

F32 = jnp.float32
BF16 = jnp.bfloat16

HEAD_DIM = 64
HEADS = 4
GROUP = HEADS * HEAD_DIM
N_MIXERS = 4
NUM_BUCKETS = 32
REL_MAX_DIST = 2048
DILATED_CONFIGS = ((128, 1), (512, 4), (2048, 16))
FOLD_RATES = tuple(rate for _, rate in DILATED_CONFIGS if rate > 1)
SWA_WINDOW = 128
DIFF_QK_DIM = HEAD_DIM // 2
CMP_LEN = 32
CMP_STRIDE = 16
CMP_HIDDEN = 256
SLC_BLOCK = 64
SLC_TOPK = 16
CMP_PARTS = 4
NSA_WINDOW = 512
RMS_EPS = 1e-6
NEG_INF = -1e30
FORCE_SELECT = 1e9
TINY = 1e-30
LOG2E = math.log2(math.e)

PROJ_ROWS = 512
BAND_TILE = 128
BAND_STEP = 1024
BAND_LOOKAHEAD = 8
BIAS_ROW_CHUNK = 64
LANES = 128
QT = 256
KT = 256
VMEM_LIMIT = 56 * 1024 * 1024
MXU_LOOKAHEAD = 4
V_ROWS = HEAD_DIM + 16

NT_DIMS = (((1,), (1,)), ((), ()))


def _t5_thresholds():
    n = np.arange(0, 4 * REL_MAX_DIST)
    max_exact = NUM_BUCKETS // 2
    nf = np.maximum(n, 1).astype(np.float32)
    large = max_exact + (np.log(nf / np.float32(max_exact)) / np.float32(math.log(REL_MAX_DIST / max_exact))
                         * np.float32(NUM_BUCKETS - max_exact)).astype(np.int32)
    bucket = np.where(n < max_exact, n, np.minimum(large, NUM_BUCKETS - 1))
    return [int(np.argmax(bucket >= b)) for b in range(NUM_BUCKETS)]


T5_THRESHOLDS = _t5_thresholds()
FAR_DIST = T5_THRESHOLDS[-1]


def _bias_kernel(tbl_ref, out_ref, *, head0, base0, dstep, rs, cs, dscale, max_dist, r_valid, d_valid, col_tile,
                 scale):
    h = pl.program_id(0)
    d = pl.program_id(1)
    ct = pl.program_id(2)
    rows, cols = out_ref.shape[-2:]
    chunk = BIAS_ROW_CHUNK if rows % BIAS_ROW_CHUNK == 0 else rows
    for r0 in range(0, rows, chunk):
        origin = base0 + d * dstep + r0 * rs + ct * col_tile * cs
        corners = [origin + dr * rs + dc * cs for dr in (0, chunk - 1) for dc in (0, cols - 1)]
        lo = functools.reduce(jnp.minimum, corners)
        hi = functools.reduce(jnp.maximum, corners)
        all_masked = (hi < 0) | (lo > max_dist) | (d >= d_valid) | (r0 >= r_valid)
        all_far = (lo * dscale >= FAR_DIST) & (hi <= max_dist) & (d < d_valid) & (r0 + chunk <= r_valid)
        out = out_ref.at[0, 0, r0:r0 + chunk, :]

        @pl.when(all_masked)
        def _():
            out[...] = jnp.full((chunk, cols), NEG_INF, F32)

        @pl.when(all_far)
        def _():
            out[...] = jnp.full((chunk, cols), tbl_ref[NUM_BUCKETS - 1, head0 + h] * scale, F32)

        @pl.when(jnp.logical_not(all_masked | all_far))
        def _():
            r = lax.broadcasted_iota(jnp.int32, (chunk, cols), 0)
            c = lax.broadcasted_iota(jnp.int32, (chunk, cols), 1)
            dist = origin + r * rs + c * cs
            n = dist * dscale
            val = jnp.full((chunk, cols), tbl_ref[0, head0 + h], F32)
            for b in range(1, NUM_BUCKETS):
                val = jnp.where(n >= T5_THRESHOLDS[b], tbl_ref[b, head0 + h], val)
            valid = (dist >= 0) & (dist <= max_dist) & (r + r0 < r_valid) & (d < d_valid)
            out[...] = jnp.where(valid, val * scale, NEG_INF)


def _build_bias(table, *, head0, n_d, rows, cols, base0, dstep, rs, cs, dscale=1, max_dist=1 << 30,
                r_valid=1 << 30, d_valid=1 << 30, col_tile=None, scale=1.0):
    col_tile = cols if col_tile is None else col_tile
    kern = functools.partial(_bias_kernel, head0=head0, base0=base0, dstep=dstep, rs=rs, cs=cs, dscale=dscale,
                             max_dist=max_dist, r_valid=r_valid, d_valid=d_valid, col_tile=col_tile, scale=scale)
    return pl.pallas_call(
        kern,
        grid=(HEADS, n_d, cols // col_tile),
        in_specs=[pl.BlockSpec(memory_space=pltpu.SMEM)],
        out_specs=pl.BlockSpec((1, 1, rows, col_tile), lambda h, d, c: (h, d, 0, c)),
        out_shape=jax.ShapeDtypeStruct((HEADS, n_d, rows, cols), F32),
        name="rel_bias_tiles",
    )(table)


RM_AQ, RM_AK, RM_AV = 0, 256, 512
RM_BQ, RM_BK, RM_BV = 768, 1024, 1280
RM_CK = 1536
RM_KVC = 1792
RM_KSW = 1920
RM_GATE = 2048
RM_COLS = 3072
TR_CQ, TR_CV, TR_DQ, TR_DVS, TR_DVW, TR_DG = 0, 256, 512, 768, 832, 896
TR_ROWS = 912


def _proj_kernel(x_ref, nw_ref, wrm_ref, wt_ref, grm_ref, gt_ref, e64_ref, e32_ref,
                 aq_ref, ak_ref, av_ref, bq_ref, bk_ref, bv_ref, ck_ref, kvc_ref, ksw_ref, gate_ref,
                 aq4_ref, ak4_ref, av4_ref, aq16_ref, ak16_ref, av16_ref,
                 cq_ref, cv_ref, dq_ref, dvs_ref, dvw_ref, dg_ref, fold_ref):
    x = x_ref[...]
    ms = jnp.mean(x * x, axis=-1, keepdims=True)
    xn = (x * lax.rsqrt(ms + RMS_EPS) * nw_ref[...]).astype(BF16)
    rows = x.shape[0]

    def rm(c0, width):
        return jnp.dot(xn, wrm_ref[:, c0:c0 + width], preferred_element_type=F32)

    def rm_normed(c0, width, e_ref):
        h = rm(c0, width)
        msq = jnp.dot((h * h).astype(BF16), e_ref[0:width, 0:width], preferred_element_type=F32)
        return h * lax.rsqrt(msq + RMS_EPS) * grm_ref[:, c0:c0 + width]

    def put_folded(val, ref, folded_refs):
        ref[...] = val.astype(ref.dtype)
        for half in range(GROUP // LANES):
            fold_ref[half] = val[:, half * LANES:(half + 1) * LANES]
        for rate, fref in zip(FOLD_RATES, folded_refs):
            for rho in range(rate):
                for half in range(GROUP // LANES):
                    c0 = rho * GROUP + half * LANES
                    fref[:, c0:c0 + LANES] = fold_ref[half, pl.ds(rho, rows // rate, stride=rate), :].astype(fref.dtype)

    put_folded(rm_normed(RM_AQ, GROUP, e64_ref), aq_ref, (aq4_ref, aq16_ref))
    put_folded(rm_normed(RM_AK, GROUP, e64_ref), ak_ref, (ak4_ref, ak16_ref))
    put_folded(rm(RM_AV, GROUP), av_ref, (av4_ref, av16_ref))
    bq_ref[...] = rm_normed(RM_BQ, GROUP, e64_ref).astype(bq_ref.dtype)
    bk_ref[...] = rm_normed(RM_BK, GROUP, e64_ref).astype(bk_ref.dtype)
    bv_ref[...] = rm(RM_BV, GROUP).astype(bv_ref.dtype)
    ck_ref[...] = rm_normed(RM_CK, GROUP, e32_ref).astype(ck_ref.dtype)
    kvc_ref[...] = rm(RM_KVC, 2 * HEAD_DIM).astype(kvc_ref.dtype)
    ksw_ref[...] = rm_normed(RM_KSW, 2 * HEAD_DIM, e64_ref).astype(ksw_ref.dtype)
    gate_ref[...] = rm(RM_GATE, N_MIXERS * GROUP).astype(gate_ref.dtype)

    key_major = lax.dot_general(wt_ref[...], xn, NT_DIMS, preferred_element_type=F32)

    def tr(r0, height):
        return key_major[r0:r0 + height]

    def tr_normed(r0, height, group):
        h3 = tr(r0, height).reshape(height // group, group, rows)
        msq = jnp.mean(h3 * h3, axis=1, keepdims=True)
        return (h3 * lax.rsqrt(msq + RMS_EPS)).reshape(height, rows) * gt_ref[r0:r0 + height, :]

    def put(ref, val):
        for t in range(rows // QT):
            ref[t] = val[:, t * QT:(t + 1) * QT].astype(ref.dtype)

    def with_ones(v):
        ones = jnp.ones((V_ROWS - HEAD_DIM, rows), F32)
        parts = []
        for h in range(v.shape[0] // HEAD_DIM):
            parts += [v[h * HEAD_DIM:(h + 1) * HEAD_DIM], ones]
        return jnp.concatenate(parts, axis=0)

    put(cq_ref, tr_normed(TR_CQ, GROUP, DIFF_QK_DIM))
    put(cv_ref, with_ones(tr(TR_CV, GROUP)))
    put(dq_ref, tr_normed(TR_DQ, GROUP, HEAD_DIM))
    put(dvs_ref, with_ones(tr(TR_DVS, HEAD_DIM)))
    put(dvw_ref, with_ones(tr(TR_DVW, HEAD_DIM)))
    put(dg_ref, tr(TR_DG, 16))


def _project(x2, nw, wrm, wt, grm, gt, e64, e32):
    m, d = x2.shape
    nt = m // QT
    tpr = PROJ_ROWS // QT
    const = lambda shape: pl.BlockSpec(shape, lambda i: (0,) * len(shape))
    rm_out = lambda width, dtype: (jax.ShapeDtypeStruct((m, width), dtype),
                                   pl.BlockSpec((PROJ_ROWS, width), lambda i: (i, 0)))
    tr_out = lambda height, dtype: (jax.ShapeDtypeStruct((nt, height, QT), dtype),
                                    pl.BlockSpec((tpr, height, QT), lambda i: (i, 0, 0)))
    outs = [rm_out(GROUP, BF16)] * 7 + [rm_out(2 * HEAD_DIM, F32), rm_out(2 * HEAD_DIM, BF16),
                                        rm_out(N_MIXERS * GROUP, BF16)]
    fold_out = lambda rate: (jax.ShapeDtypeStruct((m // rate, rate * GROUP), BF16),
                             pl.BlockSpec((PROJ_ROWS // rate, rate * GROUP), lambda i: (i, 0)))
    outs += [fold_out(rate) for rate in FOLD_RATES for _ in range(3)]
    outs += [tr_out(GROUP, BF16), tr_out(HEADS * V_ROWS, BF16), tr_out(GROUP, BF16), tr_out(V_ROWS, BF16),
             tr_out(V_ROWS, BF16), tr_out(16, F32)]
    return pl.pallas_call(
        _proj_kernel,
        grid=(m // PROJ_ROWS,),
        in_specs=[pl.BlockSpec((PROJ_ROWS, d), lambda i: (i, 0)), const((1, d)), const((d, RM_COLS)),
                  const((TR_ROWS, d)), const((1, RM_COLS)), const((TR_ROWS, 1)), const((GROUP, GROUP)),
                  const((GROUP, GROUP))],
        out_specs=[o[1] for o in outs],
        out_shape=[o[0] for o in outs],
        scratch_shapes=[pltpu.VMEM((GROUP // LANES, PROJ_ROWS, LANES), F32)],
        compiler_params=pltpu.CompilerParams(dimension_semantics=("arbitrary",), vmem_limit_bytes=VMEM_LIMIT),
        name="in_projection",
    )(x2, nw, wrm, wt, grm, gt, e64, e32)


def _band_kernel(sink_ref, q_ref, kp_ref, kc_ref, vp_ref, vc_ref, bias_ref, o_ref, lse_ref, p_ref, *, use_sink):
    i = pl.program_id(2)
    n_blocks = q_ref.shape[1] // BAND_TILE
    head_q = lax.broadcasted_iota(jnp.int32, (BAND_TILE, GROUP), 1) // HEAD_DIM
    head_v = lax.broadcasted_iota(jnp.int32, (2 * BAND_TILE, GROUP), 1) // HEAD_DIM
    lane = lax.broadcasted_iota(jnp.int32, (BAND_TILE, LANES), 1)
    in_prev = lax.broadcasted_iota(jnp.int32, (1, 2 * BAND_TILE), 1) < BAND_TILE
    no_prev = jnp.where(in_prev & (i == 0), NEG_INF, 0.0).astype(F32)

    def window(cur_ref, prev_ref, m):
        if m == 0:
            return jnp.concatenate([prev_ref[0], cur_ref[0, 0:BAND_TILE, :]], axis=0)
        return cur_ref[0, (m - 1) * BAND_TILE:(m + 1) * BAND_TILE, :]

    def scores(n):
        m, h = divmod(n, HEADS)
        q = q_ref[0, m * BAND_TILE:(m + 1) * BAND_TILE, :]
        qh = jnp.where(head_q == h, q, jnp.zeros_like(q))
        bias = bias_ref[h, 0] + no_prev if m == 0 else bias_ref[h, 0]
        return lax.dot_general(qh, window(kc_ref, kp_ref, m), NT_DIMS, preferred_element_type=F32) + bias

    lse_tiles = {}

    def update(n, s):
        m, h = divmod(n, HEADS)
        mx = jnp.max(s, axis=1, keepdims=True)
        if use_sink:
            mx = jnp.maximum(mx, sink_ref[h])
        p = jnp.exp(s - mx)
        den = jnp.sum(p, axis=1, keepdims=True)
        if use_sink:
            den = den + jnp.exp(sink_ref[h] - mx)
        p_ref[m % 2, :, h * 2 * BAND_TILE:(h + 1) * 2 * BAND_TILE] = (p * (1.0 / den)).astype(BF16)
        lse_tiles[m] = jnp.where(lane == h, mx + jnp.log(den), lse_tiles.get(m, jnp.zeros((BAND_TILE, LANES), F32)))
        if h == HEADS - 1:
            v = window(vc_ref, vp_ref, m)
            v_heads = jnp.concatenate([jnp.where(head_v == hh, v, jnp.zeros_like(v)) for hh in range(HEADS)], axis=0)
            rows = slice(m * BAND_TILE, (m + 1) * BAND_TILE)
            o_ref[0, rows, :] = jnp.dot(p_ref[m % 2], v_heads, preferred_element_type=F32).astype(o_ref.dtype)
            lse_ref[0, rows, :] = lse_tiles.pop(m)

    _staggered(n_blocks * HEADS, scores, update, ahead=BAND_LOOKAHEAD)


def _banded(q, k, v, bias, sink, rate, use_sink):
    b, ln, _ = q.shape
    step = min(BAND_STEP, ln)
    per_step = step // BAND_TILE
    cur = pl.BlockSpec((1, step, GROUP), lambda bb, r, i: (bb, i, r))
    prev = pl.BlockSpec((1, BAND_TILE, GROUP), lambda bb, r, i: (bb, jnp.maximum(i * per_step - 1, 0), r))
    o, lse = pl.pallas_call(
        functools.partial(_band_kernel, use_sink=use_sink),
        grid=(b, rate, ln // step),
        in_specs=[pl.BlockSpec(memory_space=pltpu.SMEM), cur, prev, cur, prev, cur,
                  pl.BlockSpec((HEADS, 1, BAND_TILE, 2 * BAND_TILE), lambda bb, r, i: (0, 0, 0, 0))],
        out_specs=[cur, pl.BlockSpec((1, step, LANES), lambda bb, r, i: (bb, i, r))],
        out_shape=[jax.ShapeDtypeStruct((b, ln, rate * GROUP), BF16), jax.ShapeDtypeStruct((b, ln, rate * LANES), F32)],
        scratch_shapes=[pltpu.VMEM((2, BAND_TILE, HEADS * 2 * BAND_TILE), BF16)],
        compiler_params=pltpu.CompilerParams(dimension_semantics=("arbitrary",) * 3),
        name=f"banded_attention_r{rate}",
    )(sink, q, k, k, v, v, bias)
    return o, lse


def _flash_reset(m_ref, acc_ref):
    m_ref[...] = jnp.full(m_ref.shape, NEG_INF, F32)
    acc_ref[...] = jnp.zeros(acc_ref.shape, F32)


def _flash_update(n, s, v_t, m_ref, acc_ref, shift=None):
    m_old = m_ref[n]
    if shift is None:
        m_new = jnp.maximum(m_old, jnp.max(s, axis=0, keepdims=True))
        p = jnp.exp2(s - m_new)
    else:
        m_new = jnp.maximum(m_old, jnp.max(s, axis=0, keepdims=True) + shift)
        p = jnp.exp2(s - (m_new - shift))
    alpha = jnp.exp2(m_old - m_new)
    acc_ref[n] = alpha * acc_ref[n] + jnp.dot(v_t, p.astype(BF16), preferred_element_type=F32)
    m_ref[n] = m_new


def _flash_result(n, acc_ref):
    return acc_ref[n, 0:HEAD_DIM, :] / acc_ref[n, HEAD_DIM:HEAD_DIM + 1, :]


def _staggered(n_items, scores, update, ahead=MXU_LOOKAHEAD):
    pending = {n: scores(n) for n in range(min(ahead, n_items))}
    for n in range(n_items):
        if n + ahead < n_items:
            pending[n + ahead] = scores(n + ahead)
        update(n, pending.pop(n))


def _pipelined_tiles(first, n_tiles, n_chains, group, load_tile, scores, update, next_ref, left_by_previous=None,
                     last_of_sweep=False):
    ahead = next_ref.shape[0]
    n_items = group * n_chains
    assert ahead <= n_chains

    def body(trip, _, issue_next=True):
        base = first + trip * group
        tiles, pending = {}, {}
        for n in range(n_items):
            cur = next_ref[n] if n < ahead else pending.pop(n)
            if n + ahead < n_items or issue_next:
                g, c = divmod(n + ahead, n_chains)
                if g not in tiles:
                    tiles[g] = load_tile(base + g)
                new = scores(tiles[g], base + g, c)
                if n + ahead < n_items:
                    pending[n + ahead] = new
                else:
                    next_ref[n + ahead - n_items] = new
            update(base + n // n_chains, n % n_chains, cur)

    if left_by_previous is None:
        first_tile = load_tile(first)
        for n in range(ahead):
            next_ref[n] = scores(first_tile, first, n)
    else:
        for n in range(ahead):
            next_ref[n] = next_ref[n] + left_by_previous(first, n)
    n_trips = (n_tiles + group - 1) // group
    if last_of_sweep:
        lax.fori_loop(0, n_trips - 1, body, None)
        body(n_trips - 1, None, issue_next=False)
    else:
        lax.fori_loop(0, n_trips, body, None)


def _flash_scratch(chains, ahead):
    return [pltpu.VMEM((chains, 1, QT), F32), pltpu.VMEM((chains, V_ROWS, QT), F32),
            pltpu.VMEM((ahead, KT, QT), F32)]


N_NEAR = -(-(FAR_DIST + KT - 1) // QT)
N_BIAS_TILES = N_NEAR + 2
DIFF_TILE_GROUP = 2
SLC_TILE_GROUP = 2
FAR_TILE_GROUP = 4


def _bias_tile_index(i, j):
    return jnp.where(j > i, N_NEAR + 1, jnp.minimum(i - j, N_NEAR))


def _whole_far_groups(i, group):
    return jnp.maximum(i - (N_NEAR - 1), 0) // group * group


def _diff_kernel(far_ref, q_ref, k_ref, v_ref, bias_ref, lam_ref, subln_ref, o_ref, qz_ref, m_ref, acc_ref, next_ref,
                 ot_ref, *, lambda_init):
    i = pl.program_id(1)
    q = q_ref[0]
    row = lax.broadcasted_iota(jnp.int32, (GROUP, QT), 0) // DIFF_QK_DIM
    for n in range(2 * HEADS):
        qz_ref[n] = jnp.where(row == n, q, jnp.zeros_like(q))
    _flash_reset(m_ref, acc_ref)

    def load_tile(j):
        return k_ref[0, pl.ds(pl.multiple_of(jnp.minimum(j, i) * KT, KT), KT), :]

    def values(j, n):
        h = n // 2
        return v_ref[0, jnp.minimum(j, i), h * V_ROWS:(h + 1) * V_ROWS, :]

    n_far = _whole_far_groups(i, FAR_TILE_GROUP)
    _pipelined_tiles(0, n_far, 2 * HEADS, FAR_TILE_GROUP, load_tile,
                     lambda k, j, n: jnp.dot(k, qz_ref[n], preferred_element_type=F32),
                     lambda j, n, s: _flash_update(n, s, values(j, n), m_ref, acc_ref, shift=far_ref[n // 2]),
                     next_ref)

    def scores(k, j, n):
        return jnp.dot(k, qz_ref[n], preferred_element_type=F32) + bias_ref[n // 2, _bias_tile_index(i, j)]

    _pipelined_tiles(n_far, i + 1 - n_far, 2 * HEADS, DIFF_TILE_GROUP, load_tile, scores,
                     lambda j, n, s: _flash_update(n, s, values(j, n), m_ref, acc_ref), next_ref,
                     left_by_previous=lambda j, n: bias_ref[n // 2, _bias_tile_index(i, j)], last_of_sweep=True)

    lam_p = lam_ref[...]
    lam = (jnp.exp(jnp.sum(lam_p[0:1] * lam_p[1:2], axis=1, keepdims=True))
           - jnp.exp(jnp.sum(lam_p[2:3] * lam_p[3:4], axis=1, keepdims=True)) + lambda_init)
    for h in range(HEADS):
        o = _flash_result(2 * h, acc_ref) - lam * _flash_result(2 * h + 1, acc_ref)
        msq = jnp.mean(o * o, axis=0, keepdims=True)
        ot_ref[h * HEAD_DIM:(h + 1) * HEAD_DIM, :] = (o * lax.rsqrt(msq + RMS_EPS) * subln_ref[...]
                                                      * (1.0 - lambda_init))
    o_ref[0] = ot_ref[...].T.astype(o_ref.dtype)


def _diff_attention(far, q_t, k, v_t, bias, lam_p, subln, lambda_init):
    b, s, _ = k.shape
    nq = s // QT
    nkv = s // KT
    v4 = v_t.reshape(b, nkv, HEADS * V_ROWS, KT)
    return pl.pallas_call(
        functools.partial(_diff_kernel, lambda_init=lambda_init),
        grid=(b, nq),
        in_specs=[pl.BlockSpec(memory_space=pltpu.SMEM),
                  pl.BlockSpec((1, GROUP, QT), lambda bb, i: (bb * nq + i, 0, 0)),
                  pl.BlockSpec((1, s, GROUP), lambda bb, i: (bb, 0, 0)),
                  pl.BlockSpec((1, nkv, HEADS * V_ROWS, KT), lambda bb, i: (bb, 0, 0, 0)),
                  pl.BlockSpec((HEADS, N_BIAS_TILES, KT, QT), lambda bb, i: (0, 0, 0, 0)),
                  pl.BlockSpec((4, DIFF_QK_DIM), lambda bb, i: (0, 0)),
                  pl.BlockSpec((HEAD_DIM, 1), lambda bb, i: (0, 0))],
        out_specs=pl.BlockSpec((1, QT, GROUP), lambda bb, i: (bb, i, 0)),
        out_shape=jax.ShapeDtypeStruct((b, s, GROUP), BF16),
        scratch_shapes=[pltpu.VMEM((2 * HEADS, GROUP, QT), BF16)] + _flash_scratch(2 * HEADS, MXU_LOOKAHEAD)
        + [pltpu.VMEM((GROUP, QT), F32)],
        compiler_params=pltpu.CompilerParams(dimension_semantics=("arbitrary", "arbitrary"),
                                             vmem_limit_bytes=VMEM_LIMIT),
        name="diff_attention",
    )(far, q_t, k, v4, bias, lam_p, subln)


def _compress_kernel(ch_ref, ptop_ref, pbot_ref, w1t_ref, w1b_ref, b1_ref, w2k_ref, b2k_ref, w2v_ref, b2v_ref,
                     gk_ref, kc_ref, vct_ref):
    ch = ch_ref[0]
    n_c = ch.shape[0]
    u = jnp.dot((ch + ptop_ref[...]).astype(BF16), w1t_ref[...], preferred_element_type=F32)
    v = jnp.dot((ch + pbot_ref[...]).astype(BF16), w1b_ref[...], preferred_element_type=F32)
    v_next = pltpu.roll(v, n_c - 1, 0)
    hid = jax.nn.gelu(u + v_next + b1_ref[...])
    hk = hid[:, :CMP_HIDDEN].astype(BF16)
    hv = hid[:, CMP_HIDDEN:].astype(BF16)
    kc = jnp.dot(hk, w2k_ref[...], preferred_element_type=F32) + b2k_ref[...]
    msq = jnp.mean(kc * kc, axis=-1, keepdims=True)
    kc_ref[0] = (kc * lax.rsqrt(msq + RMS_EPS) * gk_ref[...]).astype(kc_ref.dtype)
    vct = lax.dot_general(w2v_ref[...], hv, NT_DIMS, preferred_element_type=F32) + b2v_ref[...]
    vct_ref[0] = vct.astype(vct_ref.dtype)


def _compress(chunks, ptop, pbot, w1t, w1b, b1, w2k, b2k, w2v, b2v, gk):
    b, n_c, width = chunks.shape
    const = lambda a: pl.BlockSpec(a.shape, lambda bb: (0,) * a.ndim)
    params = (ptop, pbot, w1t, w1b, b1, w2k, b2k, w2v, b2v, gk)
    return pl.pallas_call(
        _compress_kernel,
        grid=(b,),
        in_specs=[pl.BlockSpec((1, n_c, width), lambda bb: (bb, 0, 0))] + [const(a) for a in params],
        out_specs=[pl.BlockSpec((1, n_c, HEAD_DIM), lambda bb: (bb, 0, 0)),
                   pl.BlockSpec((1, HEAD_DIM, n_c), lambda bb: (bb, 0, 0))],
        out_shape=[jax.ShapeDtypeStruct((b, n_c, HEAD_DIM), BF16), jax.ShapeDtypeStruct((b, HEAD_DIM, n_c), BF16)],
        compiler_params=pltpu.CompilerParams(dimension_semantics=("arbitrary",), vmem_limit_bytes=VMEM_LIMIT),
        name="nsa_compress",
    )(chunks, *params)


def _cmp_attn_kernel(q_ref, kc_ref, vct_ref, bias_ref, o_ref, sel_ref, p_ref, *, n_sel):
    i = pl.program_id(0)
    n_tiles = pl.num_programs(0)
    for part in range(1, CMP_PARTS + 1):
        @pl.when((i * CMP_PARTS >= (part - 1) * n_tiles) & (i * CMP_PARTS < part * n_tiles))
        def _(part=part):
            _cmp_attn_body(i, kc_ref.shape[1] * part // CMP_PARTS, q_ref, kc_ref, vct_ref, bias_ref, o_ref, sel_ref,
                           p_ref, n_sel)


def _cmp_attn_body(i, n_c, q_ref, kc_ref, vct_ref, bias_ref, o_ref, sel_ref, p_ref, n_sel):
    kc = kc_ref[0, 0:n_c, :]
    vct = vct_ref[0, :, 0:n_c]
    n_blk = n_c * CMP_STRIDE // SLC_BLOCK
    probs = []

    def scores(h):
        return (jnp.dot(kc, q_ref[0, h * HEAD_DIM:(h + 1) * HEAD_DIM, :], preferred_element_type=F32)
                + bias_ref[h, 0, 0:n_c, :])

    def update(h, s):
        m = jnp.maximum(jnp.max(s, axis=0, keepdims=True), 0.5 * NEG_INF)
        p = jnp.exp2(s - m)
        den = jnp.sum(p, axis=0, keepdims=True)
        p = p * (1.0 / jnp.maximum(den, TINY))
        o_ref[0, h * HEAD_DIM:(h + 1) * HEAD_DIM, :] = jnp.dot(vct, p.astype(BF16),
                                                               preferred_element_type=F32).astype(o_ref.dtype)
        probs.append(p)

    _staggered(HEADS, scores, update)
    psum = (probs[0] + probs[1]) + (probs[2] + probs[3])
    per_blk = SLC_BLOCK // CMP_STRIDE
    halves = []
    for half in range(QT // LANES):
        p_ref[half, 0:8, :] = jnp.zeros((8, LANES), F32)
        p_ref[half, 8:8 + n_c, :] = psum[:, half * LANES:(half + 1) * LANES]
        p_ref[half, 8 + n_c:16 + n_c, :] = jnp.zeros((8, LANES), F32)
        acc = p_ref[half, pl.ds(7, n_blk, stride=per_blk), :]
        for t in range(per_blk):
            acc = acc + p_ref[half, pl.ds(8 + t, n_blk, stride=per_blk), :]
        halves.append(acc)
    imp = jnp.concatenate(halves, axis=1)
    blk = lax.broadcasted_iota(jnp.int32, (n_blk, QT), 0)
    cur = (i * QT + lax.broadcasted_iota(jnp.int32, (n_blk, QT), 1)) // SLC_BLOCK
    forced = (blk == 0) | (blk == cur) | (blk == cur - 1)
    val = jnp.where(forced, FORCE_SELECT, jnp.where(blk <= cur, imp, NEG_INF))
    sel = jnp.zeros((n_blk, QT), jnp.bool_)
    for _ in range(n_sel):
        top = jnp.max(val, axis=0, keepdims=True)
        idx = jnp.min(jnp.where(val == top, blk, n_blk), axis=0, keepdims=True)
        hit = blk == idx
        sel = sel | hit
        val = jnp.where(hit, -3.0e38, val)
    sel_ref[0, 0:n_blk, :] = jnp.where(sel, 1.0, 0.0).astype(sel_ref.dtype)
    if n_blk < sel_ref.shape[1]:
        sel_ref[0, n_blk:, :] = jnp.zeros((sel_ref.shape[1] - n_blk, QT), sel_ref.dtype)


def _cmp_attention(q_t, kc, vct, bias, b):
    nt = q_t.shape[0]
    nq = nt // b
    n_c = kc.shape[1]
    n_blk = nq * QT // SLC_BLOCK
    return pl.pallas_call(
        functools.partial(_cmp_attn_kernel, n_sel=min(SLC_TOPK, n_blk)),
        grid=(nq, b),
        in_specs=[pl.BlockSpec((1, GROUP, QT), lambda i, bb: (bb * nq + i, 0, 0)),
                  pl.BlockSpec((1, n_c, HEAD_DIM), lambda i, bb: (bb, 0, 0)),
                  pl.BlockSpec((1, HEAD_DIM, n_c), lambda i, bb: (bb, 0, 0)),
                  pl.BlockSpec((HEADS, 1, n_c, QT), lambda i, bb: (0, 0, 0, i))],
        out_specs=[pl.BlockSpec((1, GROUP, QT), lambda i, bb: (bb * nq + i, 0, 0)),
                   pl.BlockSpec((1, n_blk, QT), lambda i, bb: (bb * nq + i, 0, 0))],
        out_shape=[jax.ShapeDtypeStruct((nt, GROUP, QT), BF16), jax.ShapeDtypeStruct((nt, n_blk, QT), BF16)],
        scratch_shapes=[pltpu.VMEM((QT // LANES, n_c + 16, LANES), F32)],
        compiler_params=pltpu.CompilerParams(dimension_semantics=("arbitrary", "arbitrary"),
                                             vmem_limit_bytes=VMEM_LIMIT),
        name="nsa_compressed_attention",
    )(q_t, kc, vct, bias)


SEL_REP = 8
N_WIN = -(-(NSA_WINDOW - 1 + KT - 1) // QT)


def _slc_win_kernel(far_ref, q_ref, ksw_ref, vs_ref, vw_ref, sel_ref, rep_ref, ocmp_ref, g_ref, bslc_ref, bwin_ref,
                    o_ref, qz_ref, m_ref, acc_ref, next_ref, ot_ref, mask_ref):
    i = pl.program_id(1)
    sel8 = jnp.dot(rep_ref[...], sel_ref[0], preferred_element_type=F32)
    mask_ref[...] = (sel8 - 1.0) * (-NEG_INF)
    blocks_per_tile = KT // SLC_BLOCK
    mrows = blocks_per_tile * SEL_REP
    zeros = jnp.zeros((HEAD_DIM, QT), BF16)
    for h in range(HEADS):
        qh = q_ref[0, h * HEAD_DIM:(h + 1) * HEAD_DIM, :]
        qz_ref[h] = jnp.concatenate([qh, zeros], axis=0)
        qz_ref[HEADS + h] = jnp.concatenate([zeros, qh], axis=0)
    _flash_reset(m_ref, acc_ref)

    def load_keys(j):
        return ksw_ref[0, pl.ds(pl.multiple_of(j * KT, KT), KT), :]

    def load_tile(j):
        j = jnp.minimum(j, i)
        m8 = mask_ref[pl.ds(pl.multiple_of(j * mrows, mrows), mrows), :]
        mask = jnp.broadcast_to(m8.reshape(blocks_per_tile, 1, SEL_REP, QT),
                                (blocks_per_tile, SLC_BLOCK // SEL_REP, SEL_REP, QT)).reshape(KT, QT)
        return load_keys(j), mask

    def values(j):
        return vs_ref[0, jnp.minimum(j, i)]

    n_far = _whole_far_groups(i, FAR_TILE_GROUP)
    _pipelined_tiles(0, n_far, HEADS, FAR_TILE_GROUP, load_tile,
                     lambda tile, j, h: jnp.dot(tile[0], qz_ref[h], preferred_element_type=F32) + tile[1],
                     lambda j, h, s: _flash_update(h, s, values(j), m_ref, acc_ref, shift=far_ref[h]), next_ref)

    def slc_scores(tile, j, h):
        k, mask = tile
        return jnp.dot(k, qz_ref[h], preferred_element_type=F32) + mask + bslc_ref[h, _bias_tile_index(i, j)]

    _pipelined_tiles(n_far, i + 1 - n_far, HEADS, SLC_TILE_GROUP, load_tile, slc_scores,
                     lambda j, h, s: _flash_update(h, s, values(j), m_ref, acc_ref), next_ref,
                     left_by_previous=lambda j, h: bslc_ref[h, _bias_tile_index(i, j)], last_of_sweep=True)

    def win_tile(n):
        d = N_WIN - 1 - n // HEADS
        return d, n % HEADS, jnp.maximum(i - d, 0)

    def win_scores(n):
        d, h, j = win_tile(n)
        missing = jnp.where(i < d, NEG_INF, 0.0).astype(F32)
        return jnp.dot(load_keys(j), qz_ref[HEADS + h], preferred_element_type=F32) + (bwin_ref[h, d] + missing)

    def win_update(n, s):
        _, h, j = win_tile(n)
        _flash_update(HEADS + h, s, vw_ref[0, j], m_ref, acc_ref)

    _staggered(N_WIN * HEADS, win_scores, win_update)

    for h in range(HEADS):
        g = jax.nn.sigmoid(g_ref[0, 3 * h:3 * h + 3, :])
        ot_ref[h * HEAD_DIM:(h + 1) * HEAD_DIM, :] = (g[0:1] * ocmp_ref[0, h * HEAD_DIM:(h + 1) * HEAD_DIM, :]
                                                      + g[1:2] * _flash_result(h, acc_ref)
                                                      + g[2:3] * _flash_result(HEADS + h, acc_ref))
    o_ref[0] = ot_ref[...].T.astype(o_ref.dtype)


def _slc_win_attention(far, q_t, ksw, vs_t, vw_t, sel, rep, ocmp, g_t, bslc, bwin):
    b, s, _ = ksw.shape
    nq = s // QT
    nkv = s // KT
    n_blk = s // SLC_BLOCK
    tile = lambda height: pl.BlockSpec((1, height, QT), lambda bb, i: (bb * nq + i, 0, 0))
    whole = lambda a: pl.BlockSpec(a.shape, lambda bb, i: (0,) * a.ndim)
    return pl.pallas_call(
        _slc_win_kernel,
        grid=(b, nq),
        in_specs=[pl.BlockSpec(memory_space=pltpu.SMEM), tile(GROUP),
                  pl.BlockSpec((1, s, 2 * HEAD_DIM), lambda bb, i: (bb, 0, 0)),
                  pl.BlockSpec((1, nkv, V_ROWS, KT), lambda bb, i: (bb, 0, 0, 0)),
                  pl.BlockSpec((1, nkv, V_ROWS, KT), lambda bb, i: (bb, 0, 0, 0)),
                  tile(n_blk), whole(rep), tile(GROUP), tile(16), whole(bslc), whole(bwin)],
        out_specs=pl.BlockSpec((1, QT, GROUP), lambda bb, i: (bb, i, 0)),
        out_shape=jax.ShapeDtypeStruct((b, s, GROUP), BF16),
        scratch_shapes=[pltpu.VMEM((2 * HEADS, 2 * HEAD_DIM, QT), BF16)] + _flash_scratch(2 * HEADS, MXU_LOOKAHEAD)
        + [pltpu.VMEM((GROUP, QT), F32), pltpu.VMEM((n_blk * SEL_REP, QT), F32)],
        compiler_params=pltpu.CompilerParams(dimension_semantics=("arbitrary", "arbitrary"),
                                             vmem_limit_bytes=VMEM_LIMIT),
        name="nsa_selected_window_attention",
    )(far, q_t, ksw, vs_t.reshape(b, nkv, V_ROWS, KT), vw_t.reshape(b, nkv, V_ROWS, KT), sel, rep, ocmp, g_t,
      bslc, bwin)


def _out_kernel(x_ref, a0_ref, a1_ref, a2_ref, l0_ref, l1_ref, l2_ref, ob_ref, oc_ref, od_ref, gate_ref, e_ref,
                w_ref, o_ref, unfold_ref):
    rows = x_ref.shape[0]

    def unfolded(ref, rate):
        width = ref.shape[1] // rate
        for rho in range(rate):
            for part in range(width // LANES):
                c0 = rho * width + part * LANES
                unfold_ref[part, pl.ds(rho, rows // rate, stride=rate), :] = ref[:, c0:c0 + LANES].astype(F32)
        return jnp.concatenate([unfold_ref[part] for part in range(width // LANES)], axis=1)

    a0, l0 = a0_ref[...], l0_ref[...]
    a1, l1 = unfolded(a1_ref, FOLD_RATES[0]), unfolded(l1_ref, FOLD_RATES[0])
    a2, l2 = unfolded(a2_ref, FOLD_RATES[1]), unfolded(l2_ref, FOLD_RATES[1])
    mx = jnp.maximum(jnp.maximum(l0, l1), l2)
    e0, e1, e2 = jnp.exp(l0 - mx), jnp.exp(l1 - mx), jnp.exp(l2 - mx)
    den = e0 + e1 + e2

    def per_head_lanes(w):
        hi = w.astype(BF16)
        lo = (w - hi.astype(F32)).astype(BF16)
        return (jnp.dot(hi, e_ref[...], preferred_element_type=F32)
                + jnp.dot(lo, e_ref[...], preferred_element_type=F32))

    o_a = per_head_lanes(e0 / den) * a0 + per_head_lanes(e1 / den) * a1 + per_head_lanes(e2 / den) * a2
    y = jnp.concatenate([o_a, ob_ref[...].astype(F32), oc_ref[...].astype(F32), od_ref[...].astype(F32)], axis=1)
    g = gate_ref[...].astype(F32)
    y = y * (g * jax.nn.sigmoid(g))
    o_ref[...] = x_ref[...] + jnp.dot(y.astype(BF16), w_ref[...], preferred_element_type=F32)


def _out_projection(x2, a_outs, a_lses, o_b, o_c, o_d, gate, w_out):
    m, d = x2.shape
    rowblk = lambda width: pl.BlockSpec((PROJ_ROWS, width), lambda i: (i, 0))
    folded = lambda width, rate: pl.BlockSpec((PROJ_ROWS // rate, rate * width), lambda i: (i, 0))
    head_of_lane = np.arange(GROUP) // HEAD_DIM
    expand = jnp.asarray((np.arange(LANES)[:, None] == head_of_lane[None, :]).astype(np.float32), BF16)
    return pl.pallas_call(
        _out_kernel,
        grid=(m // PROJ_ROWS,),
        in_specs=[rowblk(d)] + [folded(GROUP, rate) for rate in (1,) + FOLD_RATES]
        + [folded(LANES, rate) for rate in (1,) + FOLD_RATES] + [rowblk(GROUP)] * 3
        + [rowblk(N_MIXERS * GROUP), pl.BlockSpec((LANES, GROUP), lambda i: (0, 0)),
           pl.BlockSpec((N_MIXERS * GROUP, d), lambda i: (0, 0))],
        out_specs=rowblk(d),
        out_shape=jax.ShapeDtypeStruct((m, d), F32),
        scratch_shapes=[pltpu.VMEM((GROUP // LANES, PROJ_ROWS, LANES), F32)],
        compiler_params=pltpu.CompilerParams(dimension_semantics=("arbitrary",), vmem_limit_bytes=VMEM_LIMIT),
        name="out_projection",
    )(x2, *a_outs, *a_lses, o_b, o_c, o_d, gate, expand, w_out)


def _block_diag_mean(group):
    idx = np.arange(GROUP) // group
    return jnp.asarray((idx[:, None] == idx[None, :]).astype(np.float32) / group, BF16)


def _layer_weights(w_in, qk_gain, qk_gain_diff):
    d = w_in.shape[0]
    sizes = (GROUP,) * 3 + (GROUP, GROUP // 2, GROUP // 2) + (GROUP,) * 3 + (GROUP,) + (HEAD_DIM,) * 6 \
        + (HEADS * 3, N_MIXERS * GROUP)
    offs = np.concatenate([[0], np.cumsum(sizes)])
    col = lambda n: w_in[:, offs[n]:offs[n + 1]]
    (a_q, a_k, a_v, b_q, b_k, b_v, c_q, c_k, c_v, d_q, d_kc, d_vc, d_ks, d_vs, d_kw, d_vw, d_g, gate) = \
        [col(n) for n in range(18)]
    rep_kv = lambda w: jnp.repeat(w.reshape(d, 2, HEAD_DIM), 2, axis=1).reshape(d, GROUP)
    wrm = jnp.concatenate([a_q, a_k, a_v, b_q, rep_kv(b_k), rep_kv(b_v), c_k, d_kc, d_vc, d_ks, d_kw, gate], axis=1)
    wt = jnp.concatenate([c_q, c_v, d_q, d_vs, d_vw, d_g, jnp.zeros((d, 16 - HEADS * 3), w_in.dtype)], axis=1).T
    g = qk_gain
    ones = lambda n: jnp.ones((n,), F32)
    tile4 = lambda v: jnp.tile(v, HEADS)
    scale = HEAD_DIM ** -0.5
    grm = jnp.concatenate([tile4(g[0]) * scale, tile4(g[1]), ones(GROUP), tile4(g[2]) * scale, tile4(g[3]),
                           ones(GROUP), jnp.tile(qk_gain_diff[1], 2 * HEADS), ones(2 * HEAD_DIM), g[6], g[7],
                           ones(N_MIXERS * GROUP)])
    gt = jnp.concatenate([jnp.tile(qk_gain_diff[0], 2 * HEADS) * (DIFF_QK_DIM ** -0.5 * LOG2E), ones(GROUP),
                          tile4(g[4]) * (scale * LOG2E), ones(2 * HEAD_DIM + 16)])
    return wrm.astype(BF16), wt.astype(BF16), grm.reshape(1, -1), gt.reshape(-1, 1)


def _compress_weights(cmp_pos, cmp_w1, cmp_b1, cmp_w2, cmp_b2):
    half = CMP_LEN // 2
    pos = jnp.concatenate([cmp_pos[0], cmp_pos[1]], axis=-1)
    ptop = pos[:half].reshape(1, -1)
    pbot = pos[half:].reshape(1, -1)
    w1 = cmp_w1.reshape(2, CMP_LEN, HEAD_DIM, CMP_HIDDEN)
    zeros = jnp.zeros_like(w1[0])
    w1cat = jnp.concatenate([jnp.concatenate([w1[0], zeros], axis=-1),
                             jnp.concatenate([zeros, w1[1]], axis=-1)], axis=1)
    w1t = w1cat[:half].reshape(half * 2 * HEAD_DIM, 2 * CMP_HIDDEN).astype(BF16)
    w1b = w1cat[half:].reshape(half * 2 * HEAD_DIM, 2 * CMP_HIDDEN).astype(BF16)
    b1 = jnp.concatenate([cmp_b1[0], cmp_b1[1]]).reshape(1, -1)
    return (ptop, pbot, w1t, w1b, b1, cmp_w2[0].astype(BF16), cmp_b2[0].reshape(1, -1),
            cmp_w2[1].T.astype(BF16), cmp_b2[1].reshape(-1, 1))


def kernel(x, rel_bias_table, norm_w, w_in, w_out, qk_gain, qk_gain_diff, attn_sinks, diff_lambda, diff_subln,
           cmp_pos, cmp_w1, cmp_b1, cmp_w2, cmp_b2):
    b, s, d = x.shape
    depth = w_in.shape[0]
    n_c = s // CMP_STRIDE
    n_blk = s // SLC_BLOCK
    assert s % (BAND_TILE * DILATED_CONFIGS[-1][1]) == 0 and s % PROJ_ROWS == 0 and d == N_MIXERS * GROUP

    table = rel_bias_table.astype(F32)
    band_bias = [_build_bias(table, head0=0, n_d=1, rows=BAND_TILE, cols=2 * BAND_TILE, base0=BAND_TILE, dstep=0,
                             rs=1, cs=-1, dscale=rate, max_dist=window // rate) for window, rate in DILATED_CONFIGS]
    swa_bias = _build_bias(table, head0=HEADS, n_d=1, rows=BAND_TILE, cols=2 * BAND_TILE, base0=BAND_TILE, dstep=0,
                           rs=1, cs=-1, max_dist=SWA_WINDOW - 1)
    flash_tiles = dict(rows=KT, cols=QT, base0=0, dstep=QT, rs=-1, cs=1, scale=LOG2E)
    diff_bias = _build_bias(table, head0=2 * HEADS, n_d=N_BIAS_TILES, d_valid=N_NEAR + 1, **flash_tiles)
    slc_bias = _build_bias(table, head0=3 * HEADS, n_d=N_BIAS_TILES, d_valid=N_NEAR + 1, **flash_tiles)
    win_bias = _build_bias(table, head0=3 * HEADS, n_d=N_WIN, max_dist=NSA_WINDOW - 1, **flash_tiles)
    far_bias = table[NUM_BUCKETS - 1] * LOG2E
    cmp_bias = _build_bias(table, head0=3 * HEADS, n_d=1, rows=n_c, cols=s, base0=-(CMP_LEN - 1), dstep=0,
                           rs=-CMP_STRIDE, cs=1, r_valid=n_c - 1, col_tile=4 * QT, scale=LOG2E)
    e64, e32 = _block_diag_mean(HEAD_DIM), _block_diag_mean(DIFF_QK_DIM)
    rep_idx = np.arange(n_blk * SEL_REP) // SEL_REP
    rep = jnp.asarray((rep_idx[:, None] == np.arange(n_blk)[None, :]).astype(np.float32), BF16)
    no_sink = jnp.zeros((HEADS,), F32)

    x2 = x.reshape(b * s, d)
    w_in_bf16 = w_in.astype(BF16)
    for layer in range(depth):
        wrm, wt, grm, gt = _layer_weights(w_in_bf16[layer], qk_gain[layer], qk_gain_diff[layer])
        (a_q, a_k, a_v, b_q, b_k, b_v, c_k, kvc, ksw, gate, a_q4, a_k4, a_v4, a_q16, a_k16, a_v16,
         c_qt, c_vt, d_qt, d_vst, d_vwt, d_gt) = _project(x2, norm_w[layer].reshape(1, d), wrm, wt, grm, gt, e64, e32)
        seq = lambda t: t.reshape(b, s, t.shape[-1])
        per_batch = lambda t: t.reshape(b, t.shape[0] // b, t.shape[1])
        flat = lambda t: t.reshape(b * t.shape[1], t.shape[2])
        a_in = ((a_q, a_k, a_v), (a_q4, a_k4, a_v4), (a_q16, a_k16, a_v16))
        a_res = [_banded(*map(per_batch, a_in[n]), band_bias[n], no_sink, rate, False)
                 for n, (_, rate) in enumerate(DILATED_CONFIGS)]
        o_b, _ = _banded(seq(b_q), seq(b_k), seq(b_v), swa_bias, attn_sinks[layer].astype(F32), 1, True)
        lambda_init = 0.8 - 0.6 * math.exp(-0.3 * layer)
        o_c = _diff_attention(far_bias[2 * HEADS:3 * HEADS], c_qt, seq(c_k), c_vt, diff_bias, diff_lambda[layer].astype(F32),
                              diff_subln[layer].reshape(HEAD_DIM, 1).astype(F32), lambda_init)
        cw = _compress_weights(cmp_pos[layer], cmp_w1[layer], cmp_b1[layer], cmp_w2[layer], cmp_b2[layer])
        kc, vct = _compress(kvc.reshape(b, n_c, CMP_STRIDE * 2 * HEAD_DIM), *cw, qk_gain[layer, 5].reshape(1, -1))
        o_cmp, sel = _cmp_attention(d_qt, kc, vct, cmp_bias, b)
        o_d = _slc_win_attention(far_bias[3 * HEADS:4 * HEADS], d_qt, seq(ksw), d_vst, d_vwt, sel, rep, o_cmp, d_gt,
                                 slc_bias, win_bias)
        x2 = _out_projection(x2, [flat(r[0]) for r in a_res], [flat(r[1]) for r in a_res], o_b.reshape(b * s, GROUP),
                             o_c.reshape(b * s, GROUP), o_d.reshape(b * s, GROUP), gate, w_out[layer].astype(BF16))
    return x2.reshape(b, s, d)
```

```python
import functools
import math

import numpy as np
import jax
import jax.numpy as jnp
from jax import lax
from jax.experimental import pallas as pl
from jax.experimental.pallas import tpu as pltpu

F32 = jnp.float32
BF16 = jnp.bfloat16

HEAD_DIM = 64
HEADS = 4
GROUP = HEADS * HEAD_DIM
N_MIXERS = 4
NUM_BUCKETS = 32
REL_MAX_DIST = 2048
DILATED_CONFIGS = ((128, 1), (512, 4), (2048, 16))
FOLD_RATES = tuple(rate for _, rate in DILATED_CONFIGS if rate > 1)
SWA_WINDOW = 128
DIFF_QK_DIM = HEAD_DIM // 2
CMP_LEN = 32
CMP_STRIDE = 16
CMP_HIDDEN = 256
SLC_BLOCK = 64
SLC_TOPK = 16
CMP_PARTS = 4
NSA_WINDOW = 512
RMS_EPS = 1e-6
NEG_INF = -1e30
FORCE_SELECT = 1e9
TINY = 1e-30
LOG2E = math.log2(math.e)

PROJ_ROWS = 512
BAND_TILE = 128
BAND_STEP = 1024
BAND_LOOKAHEAD = 8
BIAS_ROW_CHUNK = 64
LANES = 128
QT = 256
KT = 256
VMEM_LIMIT = 56 * 1024 * 1024
MXU_LOOKAHEAD = 4
V_ROWS = HEAD_DIM + 16

NT_DIMS = (((1,), (1,)), ((), ()))


def _t5_thresholds():
    n = np.arange(0, 4 * REL_MAX_DIST)
    max_exact = NUM_BUCKETS // 2
    nf = np.maximum(n, 1).astype(np.float32)
    large = max_exact + (np.log(nf / np.float32(max_exact)) / np.float32(math.log(REL_MAX_DIST / max_exact))
                         * np.float32(NUM_BUCKETS - max_exact)).astype(np.int32)
    bucket = np.where(n < max_exact, n, np.minimum(large, NUM_BUCKETS - 1))
    return [int(np.argmax(bucket >= b)) for b in range(NUM_BUCKETS)]


T5_THRESHOLDS = _t5_thresholds()
FAR_DIST = T5_THRESHOLDS[-1]


def _bias_kernel(tbl_ref, out_ref, *, head0, base0, dstep, rs, cs, dscale, max_dist, r_valid, d_valid, col_tile,
                 scale):
    h = pl.program_id(0)
    d = pl.program_id(1)
    ct = pl.program_id(2)
    rows, cols = out_ref.shape[-2:]
    chunk = BIAS_ROW_CHUNK if rows % BIAS_ROW_CHUNK == 0 else rows
    for r0 in range(0, rows, chunk):
        origin = base0 + d * dstep + r0 * rs + ct * col_tile * cs
        corners = [origin + dr * rs + dc * cs for dr in (0, chunk - 1) for dc in (0, cols - 1)]
        lo = functools.reduce(jnp.minimum, corners)
        hi = functools.reduce(jnp.maximum, corners)
        all_masked = (hi < 0) | (lo > max_dist) | (d >= d_valid) | (r0 >= r_valid)
        all_far = (lo * dscale >= FAR_DIST) & (hi <= max_dist) & (d < d_valid) & (r0 + chunk <= r_valid)
        out = out_ref.at[0, 0, r0:r0 + chunk, :]

        @pl.when(all_masked)
        def _():
            out[...] = jnp.full((chunk, cols), NEG_INF, F32)

        @pl.when(all_far)
        def _():
            out[...] = jnp.full((chunk, cols), tbl_ref[NUM_BUCKETS - 1, head0 + h] * scale, F32)

        @pl.when(jnp.logical_not(all_masked | all_far))
        def _():
            r = lax.broadcasted_iota(jnp.int32, (chunk, cols), 0)
            c = lax.broadcasted_iota(jnp.int32, (chunk, cols), 1)
            dist = origin + r * rs + c * cs
            n = dist * dscale
            val = jnp.full((chunk, cols), tbl_ref[0, head0 + h], F32)
            for b in range(1, NUM_BUCKETS):
                val = jnp.where(n >= T5_THRESHOLDS[b], tbl_ref[b, head0 + h], val)
            valid = (dist >= 0) & (dist <= max_dist) & (r + r0 < r_valid) & (d < d_valid)
            out[...] = jnp.where(valid, val * scale, NEG_INF)


def _build_bias(table, *, head0, n_d, rows, cols, base0, dstep, rs, cs, dscale=1, max_dist=1 << 30,
                r_valid=1 << 30, d_valid=1 << 30, col_tile=None, scale=1.0):
    col_tile = cols if col_tile is None else col_tile
    kern = functools.partial(_bias_kernel, head0=head0, base0=base0, dstep=dstep, rs=rs, cs=cs, dscale=dscale,
                             max_dist=max_dist, r_valid=r_valid, d_valid=d_valid, col_tile=col_tile, scale=scale)
    return pl.pallas_call(
        kern,
        grid=(HEADS, n_d, cols // col_tile),
        in_specs=[pl.BlockSpec(memory_space=pltpu.SMEM)],
        out_specs=pl.BlockSpec((1, 1, rows, col_tile), lambda h, d, c: (h, d, 0, c)),
        out_shape=jax.ShapeDtypeStruct((HEADS, n_d, rows, cols), F32),
        name="rel_bias_tiles",
    )(table)


RM_AQ, RM_AK, RM_AV = 0, 256, 512
RM_BQ, RM_BK, RM_BV = 768, 1024, 1280
RM_CK = 1536
RM_KVC = 1792
RM_KSW = 1920
RM_GATE = 2048
RM_COLS = 3072
TR_CQ, TR_CV, TR_DQ, TR_DVS, TR_DVW, TR_DG = 0, 256, 512, 768, 832, 896
GATE_ROWS = 16
TR_ROWS = TR_DG + GATE_ROWS


def _proj_kernel(x_ref, nw_ref, wrm_ref, wt_ref, grm_ref, gt_ref, e64_ref, e32_ref,
                 aq_ref, ak_ref, av_ref, bq_ref, bk_ref, bv_ref, ck_ref, kvc_ref, ksw_ref, gate_ref,
                 aq4_ref, ak4_ref, av4_ref, aq16_ref, ak16_ref, av16_ref,
                 cq_ref, cv_ref, dq_ref, dvs_ref, dvw_ref, dg_ref, fold_ref):
    x = x_ref[...]
    ms = jnp.mean(x * x, axis=-1, keepdims=True)
    xn = (x * lax.rsqrt(ms + RMS_EPS) * nw_ref[...]).astype(BF16)
    rows = x.shape[0]

    def rm(c0, width):
        return jnp.dot(xn, wrm_ref[:, c0:c0 + width], preferred_element_type=F32)

    def rm_normed(c0, width, e_ref):
        h = rm(c0, width)
        msq = jnp.dot((h * h).astype(BF16), e_ref[0:width, 0:width], preferred_element_type=F32)
        return h * lax.rsqrt(msq + RMS_EPS) * grm_ref[:, c0:c0 + width]

    def put_folded(val, ref, folded_refs):
        ref[...] = val.astype(ref.dtype)
        for half in range(GROUP // LANES):
            fold_ref[half] = val[:, half * LANES:(half + 1) * LANES]
        for rate, fref in zip(FOLD_RATES, folded_refs):
            for rho in range(rate):
                for half in range(GROUP // LANES):
                    c0 = rho * GROUP + half * LANES
                    fref[:, c0:c0 + LANES] = fold_ref[half, pl.ds(rho, rows // rate, stride=rate), :].astype(fref.dtype)

    put_folded(rm_normed(RM_AQ, GROUP, e64_ref), aq_ref, (aq4_ref, aq16_ref))
    put_folded(rm_normed(RM_AK, GROUP, e64_ref), ak_ref, (ak4_ref, ak16_ref))
    put_folded(rm(RM_AV, GROUP), av_ref, (av4_ref, av16_ref))
    bq_ref[...] = rm_normed(RM_BQ, GROUP, e64_ref).astype(bq_ref.dtype)
    bk_ref[...] = rm_normed(RM_BK, GROUP, e64_ref).astype(bk_ref.dtype)
    bv_ref[...] = rm(RM_BV, GROUP).astype(bv_ref.dtype)
    ck_ref[...] = rm_normed(RM_CK, GROUP, e32_ref).astype(ck_ref.dtype)
    kvc_ref[...] = rm(RM_KVC, 2 * HEAD_DIM).astype(kvc_ref.dtype)
    ksw_ref[...] = rm_normed(RM_KSW, 2 * HEAD_DIM, e64_ref).astype(ksw_ref.dtype)
    gate_ref[...] = rm(RM_GATE, N_MIXERS * GROUP).astype(gate_ref.dtype)

    key_major = lax.dot_general(wt_ref[...], xn, NT_DIMS, preferred_element_type=F32)

    def tr(r0, height):
        return key_major[r0:r0 + height]

    def tr_normed(r0, height, group):
        h3 = tr(r0, height).reshape(height // group, group, rows)
        msq = jnp.mean(h3 * h3, axis=1, keepdims=True)
        return (h3 * lax.rsqrt(msq + RMS_EPS)).reshape(height, rows) * gt_ref[r0:r0 + height, :]

    def put(ref, val):
        for t in range(rows // QT):
            ref[t] = val[:, t * QT:(t + 1) * QT].astype(ref.dtype)

    def with_ones(v):
        ones = jnp.ones((V_ROWS - HEAD_DIM, rows), F32)
        parts = []
        for h in range(v.shape[0] // HEAD_DIM):
            parts += [v[h * HEAD_DIM:(h + 1) * HEAD_DIM], ones]
        return jnp.concatenate(parts, axis=0)

    put(cq_ref, tr_normed(TR_CQ, GROUP, DIFF_QK_DIM))
    put(cv_ref, with_ones(tr(TR_CV, GROUP)))
    put(dq_ref, tr_normed(TR_DQ, GROUP, HEAD_DIM))
    put(dvs_ref, with_ones(tr(TR_DVS, HEAD_DIM)))
    put(dvw_ref, with_ones(tr(TR_DVW, HEAD_DIM)))
    put(dg_ref, tr(TR_DG, GATE_ROWS))


def _project(x2, nw, wrm, wt, grm, gt, e64, e32):
    m, d = x2.shape
    nt = m // QT
    tpr = PROJ_ROWS // QT
    const = lambda shape: pl.BlockSpec(shape, lambda i: (0,) * len(shape))
    rm_out = lambda width, dtype: (jax.ShapeDtypeStruct((m, width), dtype),
                                   pl.BlockSpec((PROJ_ROWS, width), lambda i: (i, 0)))
    tr_out = lambda height, dtype: (jax.ShapeDtypeStruct((nt, height, QT), dtype),
                                    pl.BlockSpec((tpr, height, QT), lambda i: (i, 0, 0)))
    outs = [rm_out(GROUP, BF16)] * 7 + [rm_out(2 * HEAD_DIM, F32), rm_out(2 * HEAD_DIM, BF16),
                                        rm_out(N_MIXERS * GROUP, BF16)]
    fold_out = lambda rate: (jax.ShapeDtypeStruct((m // rate, rate * GROUP), BF16),
                             pl.BlockSpec((PROJ_ROWS // rate, rate * GROUP), lambda i: (i, 0)))
    outs += [fold_out(rate) for rate in FOLD_RATES for _ in range(3)]
    outs += [tr_out(GROUP, BF16), tr_out(HEADS * V_ROWS, BF16), tr_out(GROUP, BF16), tr_out(V_ROWS, BF16),
             tr_out(V_ROWS, BF16), tr_out(GATE_ROWS, F32)]
    return pl.pallas_call(
        _proj_kernel,
        grid=(m // PROJ_ROWS,),
        in_specs=[pl.BlockSpec((PROJ_ROWS, d), lambda i: (i, 0)), const((1, d)), const((d, RM_COLS)),
                  const((TR_ROWS, d)), const((1, RM_COLS)), const((TR_ROWS, 1)), const((GROUP, GROUP)),
                  const((GROUP, GROUP))],
        out_specs=[o[1] for o in outs],
        out_shape=[o[0] for o in outs],
        scratch_shapes=[pltpu.VMEM((GROUP // LANES, PROJ_ROWS, LANES), F32)],
        compiler_params=pltpu.CompilerParams(dimension_semantics=("arbitrary",), vmem_limit_bytes=VMEM_LIMIT),
        name="in_projection",
    )(x2, nw, wrm, wt, grm, gt, e64, e32)


def _band_kernel(sink_ref, q_ref, kp_ref, kc_ref, vp_ref, vc_ref, bias_ref, o_ref, lse_ref, p_ref, *, use_sink):
    i = pl.program_id(2)
    n_blocks = q_ref.shape[1] // BAND_TILE
    head_q = lax.broadcasted_iota(jnp.int32, (BAND_TILE, GROUP), 1) // HEAD_DIM
    head_v = lax.broadcasted_iota(jnp.int32, (2 * BAND_TILE, GROUP), 1) // HEAD_DIM
    lane = lax.broadcasted_iota(jnp.int32, (BAND_TILE, LANES), 1)
    in_prev = lax.broadcasted_iota(jnp.int32, (1, 2 * BAND_TILE), 1) < BAND_TILE
    no_prev = jnp.where(in_prev & (i == 0), NEG_INF, 0.0).astype(F32)

    def window(cur_ref, prev_ref, m):
        if m == 0:
            return jnp.concatenate([prev_ref[0], cur_ref[0, 0:BAND_TILE, :]], axis=0)
        return cur_ref[0, (m - 1) * BAND_TILE:(m + 1) * BAND_TILE, :]

    def scores(n):
        m, h = divmod(n, HEADS)
        q = q_ref[0, m * BAND_TILE:(m + 1) * BAND_TILE, :]
        qh = jnp.where(head_q == h, q, jnp.zeros_like(q))
        bias = bias_ref[h, 0] + no_prev if m == 0 else bias_ref[h, 0]
        return lax.dot_general(qh, window(kc_ref, kp_ref, m), NT_DIMS, preferred_element_type=F32) + bias

    lse_tiles = {}

    def update(n, s):
        m, h = divmod(n, HEADS)
        mx = jnp.max(s, axis=1, keepdims=True)
        if use_sink:
            mx = jnp.maximum(mx, sink_ref[h])
        p = jnp.exp2(s - mx)
        den = jnp.sum(p, axis=1, keepdims=True)
        if use_sink:
            den = den + jnp.exp2(sink_ref[h] - mx)
        p_ref[m % 2, :, h * 2 * BAND_TILE:(h + 1) * 2 * BAND_TILE] = (p * (1.0 / den)).astype(BF16)
        lse_tiles[m] = jnp.where(lane == h, mx + jnp.log2(den), lse_tiles.get(m, jnp.zeros((BAND_TILE, LANES), F32)))
        if h == HEADS - 1:
            v = window(vc_ref, vp_ref, m)
            v_heads = jnp.concatenate([jnp.where(head_v == hh, v, jnp.zeros_like(v)) for hh in range(HEADS)], axis=0)
            rows = slice(m * BAND_TILE, (m + 1) * BAND_TILE)
            o_ref[0, rows, :] = jnp.dot(p_ref[m % 2], v_heads, preferred_element_type=F32).astype(o_ref.dtype)
            lse_ref[0, rows, :] = lse_tiles.pop(m)

    _staggered(n_blocks * HEADS, scores, update, ahead=BAND_LOOKAHEAD)


def _banded(q, k, v, bias, sink, rate, use_sink):
    b, ln, _ = q.shape
    step = min(BAND_STEP, ln)
    per_step = step // BAND_TILE
    cur = pl.BlockSpec((1, step, GROUP), lambda bb, r, i: (bb, i, r))
    prev = pl.BlockSpec((1, BAND_TILE, GROUP), lambda bb, r, i: (bb, jnp.maximum(i * per_step - 1, 0), r))
    o, lse = pl.pallas_call(
        functools.partial(_band_kernel, use_sink=use_sink),
        grid=(b, rate, ln // step),
        in_specs=[pl.BlockSpec(memory_space=pltpu.SMEM), cur, prev, cur, prev, cur,
                  pl.BlockSpec((HEADS, 1, BAND_TILE, 2 * BAND_TILE), lambda bb, r, i: (0, 0, 0, 0))],
        out_specs=[cur, pl.BlockSpec((1, step, LANES), lambda bb, r, i: (bb, i, r))],
        out_shape=[jax.ShapeDtypeStruct((b, ln, rate * GROUP), BF16), jax.ShapeDtypeStruct((b, ln, rate * LANES), F32)],
        scratch_shapes=[pltpu.VMEM((2, BAND_TILE, HEADS * 2 * BAND_TILE), BF16)],
        compiler_params=pltpu.CompilerParams(dimension_semantics=("arbitrary",) * 3),
        name=f"banded_attention_r{rate}",
    )(sink, q, k, k, v, v, bias)
    return o, lse


def _flash_reset(m_ref, acc_ref):
    m_ref[...] = jnp.full(m_ref.shape, NEG_INF, F32)
    acc_ref[...] = jnp.zeros(acc_ref.shape, F32)


def _flash_update(n, s, v_t, m_ref, acc_ref, shift=None):
    m_old = m_ref[n]
    if shift is None:
        m_new = jnp.maximum(m_old, jnp.max(s, axis=0, keepdims=True))
        p = jnp.exp2(s - m_new)
    else:
        m_new = jnp.maximum(m_old, jnp.max(s, axis=0, keepdims=True) + shift)
        p = jnp.exp2(s - (m_new - shift))
    alpha = jnp.exp2(m_old - m_new)
    acc_ref[n] = alpha * acc_ref[n] + jnp.dot(v_t, p.astype(BF16), preferred_element_type=F32)
    m_ref[n] = m_new


def _flash_result(n, acc_ref):
    return acc_ref[n, 0:HEAD_DIM, :] / acc_ref[n, HEAD_DIM:HEAD_DIM + 1, :]


def _staggered(n_items, scores, update, ahead=MXU_LOOKAHEAD):
    pending = {n: scores(n) for n in range(min(ahead, n_items))}
    for n in range(n_items):
        if n + ahead < n_items:
            pending[n + ahead] = scores(n + ahead)
        update(n, pending.pop(n))


def _pipelined_tiles(first, n_tiles, n_chains, group, load_tile, scores, update, next_ref, left_by_previous=None,
                     last_of_sweep=False):
    ahead = next_ref.shape[0]
    n_items = group * n_chains
    assert ahead <= n_chains

    def body(trip, _, issue_next=True):
        base = first + trip * group
        tiles, pending = {}, {}
        for n in range(n_items):
            cur = next_ref[n] if n < ahead else pending.pop(n)
            if n + ahead < n_items or issue_next:
                g, c = divmod(n + ahead, n_chains)
                if g not in tiles:
                    tiles[g] = load_tile(base + g)
                new = scores(tiles[g], base + g, c)
                if n + ahead < n_items:
                    pending[n + ahead] = new
                else:
                    next_ref[n + ahead - n_items] = new
            update(base + n // n_chains, n % n_chains, cur)

    if left_by_previous is None:
        first_tile = load_tile(first)
        for n in range(ahead):
            next_ref[n] = scores(first_tile, first, n)
    else:
        for n in range(ahead):
            next_ref[n] = next_ref[n] + left_by_previous(first, n)
    n_trips = (n_tiles + group - 1) // group
    if last_of_sweep:
        lax.fori_loop(0, n_trips - 1, body, None)
        body(n_trips - 1, None, issue_next=False)
    else:
        lax.fori_loop(0, n_trips, body, None)


def _flash_scratch(chains, ahead):
    return [pltpu.VMEM((chains, 1, QT), F32), pltpu.VMEM((chains, V_ROWS, QT), F32),
            pltpu.VMEM((ahead, KT, QT), F32)]


N_NEAR = -(-(FAR_DIST + KT - 1) // QT)
N_BIAS_TILES = N_NEAR + 2
DIFF_TILE_GROUP = 2
SLC_TILE_GROUP = 2
FAR_TILE_GROUP = 4


def _bias_tile_index(i, j):
    return jnp.where(j > i, N_NEAR + 1, jnp.minimum(i - j, N_NEAR))


def _whole_far_groups(i, group):
    return jnp.maximum(i - (N_NEAR - 1), 0) // group * group


def _diff_kernel(far_ref, q_ref, k_ref, v_ref, bias_ref, lam_ref, subln_ref, o_ref, qz_ref, m_ref, acc_ref, next_ref,
                 ot_ref, *, lambda_init):
    i = pl.program_id(1)
    q = q_ref[0]
    row = lax.broadcasted_iota(jnp.int32, (GROUP, QT), 0) // DIFF_QK_DIM
    for n in range(2 * HEADS):
        qz_ref[n] = jnp.where(row == n, q, jnp.zeros_like(q))
    _flash_reset(m_ref, acc_ref)

    def load_tile(j):
        return k_ref[0, pl.ds(pl.multiple_of(jnp.minimum(j, i) * KT, KT), KT), :]

    def values(j, n):
        h = n // 2
        return v_ref[0, jnp.minimum(j, i), h * V_ROWS:(h + 1) * V_ROWS, :]

    n_far = _whole_far_groups(i, FAR_TILE_GROUP)
    _pipelined_tiles(0, n_far, 2 * HEADS, FAR_TILE_GROUP, load_tile,
                     lambda k, j, n: jnp.dot(k, qz_ref[n], preferred_element_type=F32),
                     lambda j, n, s: _flash_update(n, s, values(j, n), m_ref, acc_ref, shift=far_ref[n // 2]),
                     next_ref)

    def scores(k, j, n):
        return jnp.dot(k, qz_ref[n], preferred_element_type=F32) + bias_ref[n // 2, _bias_tile_index(i, j)]

    _pipelined_tiles(n_far, i + 1 - n_far, 2 * HEADS, DIFF_TILE_GROUP, load_tile, scores,
                     lambda j, n, s: _flash_update(n, s, values(j, n), m_ref, acc_ref), next_ref,
                     left_by_previous=lambda j, n: bias_ref[n // 2, _bias_tile_index(i, j)], last_of_sweep=True)

    lam_p = lam_ref[...]
    lam = (jnp.exp(jnp.sum(lam_p[0:1] * lam_p[1:2], axis=1, keepdims=True))
           - jnp.exp(jnp.sum(lam_p[2:3] * lam_p[3:4], axis=1, keepdims=True)) + lambda_init)
    for h in range(HEADS):
        o = _flash_result(2 * h, acc_ref) - lam * _flash_result(2 * h + 1, acc_ref)
        msq = jnp.mean(o * o, axis=0, keepdims=True)
        ot_ref[h * HEAD_DIM:(h + 1) * HEAD_DIM, :] = (o * lax.rsqrt(msq + RMS_EPS) * subln_ref[...]
                                                      * (1.0 - lambda_init))
    o_ref[0] = ot_ref[...].T.astype(o_ref.dtype)


def _diff_attention(far, q_t, k, v_t, bias, lam_p, subln, lambda_init):
    b, s, _ = k.shape
    nq = s // QT
    nkv = s // KT
    v4 = v_t.reshape(b, nkv, HEADS * V_ROWS, KT)
    return pl.pallas_call(
        functools.partial(_diff_kernel, lambda_init=lambda_init),
        grid=(b, nq),
        in_specs=[pl.BlockSpec(memory_space=pltpu.SMEM),
                  pl.BlockSpec((1, GROUP, QT), lambda bb, i: (bb * nq + i, 0, 0)),
                  pl.BlockSpec((1, s, GROUP), lambda bb, i: (bb, 0, 0)),
                  pl.BlockSpec((1, nkv, HEADS * V_ROWS, KT), lambda bb, i: (bb, 0, 0, 0)),
                  pl.BlockSpec((HEADS, N_BIAS_TILES, KT, QT), lambda bb, i: (0, 0, 0, 0)),
                  pl.BlockSpec((4, DIFF_QK_DIM), lambda bb, i: (0, 0)),
                  pl.BlockSpec((HEAD_DIM, 1), lambda bb, i: (0, 0))],
        out_specs=pl.BlockSpec((1, QT, GROUP), lambda bb, i: (bb, i, 0)),
        out_shape=jax.ShapeDtypeStruct((b, s, GROUP), BF16),
        scratch_shapes=[pltpu.VMEM((2 * HEADS, GROUP, QT), BF16)] + _flash_scratch(2 * HEADS, MXU_LOOKAHEAD)
        + [pltpu.VMEM((GROUP, QT), F32)],
        compiler_params=pltpu.CompilerParams(dimension_semantics=("arbitrary", "arbitrary"),
                                             vmem_limit_bytes=VMEM_LIMIT),
        name="diff_attention",
    )(far, q_t, k, v4, bias, lam_p, subln)


def _compress_kernel(ch_ref, ptop_ref, pbot_ref, w1t_ref, w1b_ref, b1_ref, w2k_ref, b2k_ref, w2v_ref, b2v_ref,
                     gk_ref, kc_ref, vct_ref):
    ch = ch_ref[0]
    n_c = ch.shape[0]
    u = jnp.dot((ch + ptop_ref[...]).astype(BF16), w1t_ref[...], preferred_element_type=F32)
    v = jnp.dot((ch + pbot_ref[...]).astype(BF16), w1b_ref[...], preferred_element_type=F32)
    v_next = pltpu.roll(v, n_c - 1, 0)
    hid = jax.nn.gelu(u + v_next + b1_ref[...])
    hk = hid[:, :CMP_HIDDEN].astype(BF16)
    hv = hid[:, CMP_HIDDEN:].astype(BF16)
    kc = jnp.dot(hk, w2k_ref[...], preferred_element_type=F32) + b2k_ref[...]
    msq = jnp.mean(kc * kc, axis=-1, keepdims=True)
    kc_ref[0] = (kc * lax.rsqrt(msq + RMS_EPS) * gk_ref[...]).astype(kc_ref.dtype)
    vct = lax.dot_general(w2v_ref[...], hv, NT_DIMS, preferred_element_type=F32) + b2v_ref[...]
    vct_ref[0] = vct.astype(vct_ref.dtype)


def _compress(chunks, ptop, pbot, w1t, w1b, b1, w2k, b2k, w2v, b2v, gk):
    b, n_c, width = chunks.shape
    const = lambda a: pl.BlockSpec(a.shape, lambda bb: (0,) * a.ndim)
    params = (ptop, pbot, w1t, w1b, b1, w2k, b2k, w2v, b2v, gk)
    return pl.pallas_call(
        _compress_kernel,
        grid=(b,),
        in_specs=[pl.BlockSpec((1, n_c, width), lambda bb: (bb, 0, 0))] + [const(a) for a in params],
        out_specs=[pl.BlockSpec((1, n_c, HEAD_DIM), lambda bb: (bb, 0, 0)),
                   pl.BlockSpec((1, HEAD_DIM, n_c), lambda bb: (bb, 0, 0))],
        out_shape=[jax.ShapeDtypeStruct((b, n_c, HEAD_DIM), BF16), jax.ShapeDtypeStruct((b, HEAD_DIM, n_c), BF16)],
        compiler_params=pltpu.CompilerParams(dimension_semantics=("arbitrary",), vmem_limit_bytes=VMEM_LIMIT),
        name="nsa_compress",
    )(chunks, *params)


def _cmp_attn_kernel(q_ref, kc_ref, vct_ref, bias_ref, o_ref, sel_ref, p_ref, *, n_sel):
    i = pl.program_id(0)
    n_tiles = pl.num_programs(0)
    for part in range(1, CMP_PARTS + 1):
        @pl.when((i * CMP_PARTS >= (part - 1) * n_tiles) & (i * CMP_PARTS < part * n_tiles))
        def _(part=part):
            _cmp_attn_body(i, kc_ref.shape[1] * part // CMP_PARTS, q_ref, kc_ref, vct_ref, bias_ref, o_ref, sel_ref,
                           p_ref, n_sel)


def _cmp_attn_body(i, n_c, q_ref, kc_ref, vct_ref, bias_ref, o_ref, sel_ref, p_ref, n_sel):
    kc = kc_ref[0, 0:n_c, :]
    vct = vct_ref[0, :, 0:n_c]
    n_blk = n_c * CMP_STRIDE // SLC_BLOCK
    probs = []

    def scores(h):
        return (jnp.dot(kc, q_ref[0, h * HEAD_DIM:(h + 1) * HEAD_DIM, :], preferred_element_type=F32)
                + bias_ref[h, 0, 0:n_c, :])

    def update(h, s):
        m = jnp.maximum(jnp.max(s, axis=0, keepdims=True), 0.5 * NEG_INF)
        p = jnp.exp2(s - m)
        den = jnp.sum(p, axis=0, keepdims=True)
        p = p * (1.0 / jnp.maximum(den, TINY))
        o_ref[0, h * HEAD_DIM:(h + 1) * HEAD_DIM, :] = jnp.dot(vct, p.astype(BF16),
                                                               preferred_element_type=F32).astype(o_ref.dtype)
        probs.append(p)

    _staggered(HEADS, scores, update)
    psum = (probs[0] + probs[1]) + (probs[2] + probs[3])
    per_blk = SLC_BLOCK // CMP_STRIDE
    halves = []
    for half in range(QT // LANES):
        p_ref[half, 0:8, :] = jnp.zeros((8, LANES), F32)
        p_ref[half, 8:8 + n_c, :] = psum[:, half * LANES:(half + 1) * LANES]
        p_ref[half, 8 + n_c:16 + n_c, :] = jnp.zeros((8, LANES), F32)
        acc = p_ref[half, pl.ds(7, n_blk, stride=per_blk), :]
        for t in range(per_blk):
            acc = acc + p_ref[half, pl.ds(8 + t, n_blk, stride=per_blk), :]
        halves.append(acc)
    imp = jnp.concatenate(halves, axis=1)
    blk = lax.broadcasted_iota(jnp.int32, (n_blk, QT), 0)
    cur = (i * QT + lax.broadcasted_iota(jnp.int32, (n_blk, QT), 1)) // SLC_BLOCK
    forced = (blk == 0) | (blk == cur) | (blk == cur - 1)
    val = jnp.where(forced, FORCE_SELECT, jnp.where(blk <= cur, imp, NEG_INF))
    sel = jnp.zeros((n_blk, QT), jnp.bool_)
    for _ in range(n_sel):
        top = jnp.max(val, axis=0, keepdims=True)
        idx = jnp.min(jnp.where(val == top, blk, n_blk), axis=0, keepdims=True)
        hit = blk == idx
        sel = sel | hit
        val = jnp.where(hit, -3.0e38, val)
    sel_ref[0, 0:n_blk, :] = jnp.where(sel, 1.0, 0.0).astype(sel_ref.dtype)
    if n_blk < sel_ref.shape[1]:
        sel_ref[0, n_blk:, :] = jnp.zeros((sel_ref.shape[1] - n_blk, QT), sel_ref.dtype)


def _cmp_attention(q_t, kc, vct, bias, b):
    nt = q_t.shape[0]
    nq = nt // b
    n_c = kc.shape[1]
    n_blk = nq * QT // SLC_BLOCK
    return pl.pallas_call(
        functools.partial(_cmp_attn_kernel, n_sel=min(SLC_TOPK, n_blk)),
        grid=(nq, b),
        in_specs=[pl.BlockSpec((1, GROUP, QT), lambda i, bb: (bb * nq + i, 0, 0)),
                  pl.BlockSpec((1, n_c, HEAD_DIM), lambda i, bb: (bb, 0, 0)),
                  pl.BlockSpec((1, HEAD_DIM, n_c), lambda i, bb: (bb, 0, 0)),
                  pl.BlockSpec((HEADS, 1, n_c, QT), lambda i, bb: (0, 0, 0, i))],
        out_specs=[pl.BlockSpec((1, GROUP, QT), lambda i, bb: (bb * nq + i, 0, 0)),
                   pl.BlockSpec((1, n_blk, QT), lambda i, bb: (bb * nq + i, 0, 0))],
        out_shape=[jax.ShapeDtypeStruct((nt, GROUP, QT), BF16), jax.ShapeDtypeStruct((nt, n_blk, QT), BF16)],
        scratch_shapes=[pltpu.VMEM((QT // LANES, n_c + 16, LANES), F32)],
        compiler_params=pltpu.CompilerParams(dimension_semantics=("arbitrary", "arbitrary"),
                                             vmem_limit_bytes=VMEM_LIMIT),
        name="nsa_compressed_attention",
    )(q_t, kc, vct, bias)


SEL_REP = 8
N_WIN = -(-(NSA_WINDOW - 1 + KT - 1) // QT)


def _slc_win_kernel(far_ref, q_ref, ksw_ref, vs_ref, vw_ref, sel_ref, rep_ref, ocmp_ref, g_ref, bslc_ref, bwin_ref,
                    o_ref, qz_ref, m_ref, acc_ref, next_ref, ot_ref, mask_ref):
    i = pl.program_id(1)
    sel8 = jnp.dot(rep_ref[...], sel_ref[0], preferred_element_type=F32)
    mask_ref[...] = (sel8 - 1.0) * (-NEG_INF)
    blocks_per_tile = KT // SLC_BLOCK
    mrows = blocks_per_tile * SEL_REP
    zeros = jnp.zeros((HEAD_DIM, QT), BF16)
    for h in range(HEADS):
        qh = q_ref[0, h * HEAD_DIM:(h + 1) * HEAD_DIM, :]
        qz_ref[h] = jnp.concatenate([qh, zeros], axis=0)
        qz_ref[HEADS + h] = jnp.concatenate([zeros, qh], axis=0)
    _flash_reset(m_ref, acc_ref)

    def load_keys(j):
        return ksw_ref[0, pl.ds(pl.multiple_of(j * KT, KT), KT), :]

    def load_tile(j):
        j = jnp.minimum(j, i)
        m8 = mask_ref[pl.ds(pl.multiple_of(j * mrows, mrows), mrows), :]
        mask = jnp.broadcast_to(m8.reshape(blocks_per_tile, 1, SEL_REP, QT),
                                (blocks_per_tile, SLC_BLOCK // SEL_REP, SEL_REP, QT)).reshape(KT, QT)
        return load_keys(j), mask

    def values(j):
        return vs_ref[0, jnp.minimum(j, i)]

    n_far = _whole_far_groups(i, FAR_TILE_GROUP)
    _pipelined_tiles(0, n_far, HEADS, FAR_TILE_GROUP, load_tile,
                     lambda tile, j, h: jnp.dot(tile[0], qz_ref[h], preferred_element_type=F32) + tile[1],
                     lambda j, h, s: _flash_update(h, s, values(j), m_ref, acc_ref, shift=far_ref[h]), next_ref)

    def slc_scores(tile, j, h):
        k, mask = tile
        return jnp.dot(k, qz_ref[h], preferred_element_type=F32) + mask + bslc_ref[h, _bias_tile_index(i, j)]

    _pipelined_tiles(n_far, i + 1 - n_far, HEADS, SLC_TILE_GROUP, load_tile, slc_scores,
                     lambda j, h, s: _flash_update(h, s, values(j), m_ref, acc_ref), next_ref,
                     left_by_previous=lambda j, h: bslc_ref[h, _bias_tile_index(i, j)], last_of_sweep=True)

    def win_tile(n):
        d = N_WIN - 1 - n // HEADS
        return d, n % HEADS, jnp.maximum(i - d, 0)

    def win_scores(n):
        d, h, j = win_tile(n)
        missing = jnp.where(i < d, NEG_INF, 0.0).astype(F32)
        return jnp.dot(load_keys(j), qz_ref[HEADS + h], preferred_element_type=F32) + (bwin_ref[h, d] + missing)

    def win_update(n, s):
        _, h, j = win_tile(n)
        _flash_update(HEADS + h, s, vw_ref[0, j], m_ref, acc_ref)

    _staggered(N_WIN * HEADS, win_scores, win_update)

    for h in range(HEADS):
        g = jax.nn.sigmoid(g_ref[0, 3 * h:3 * h + 3, :])
        ot_ref[h * HEAD_DIM:(h + 1) * HEAD_DIM, :] = (g[0:1] * ocmp_ref[0, h * HEAD_DIM:(h + 1) * HEAD_DIM, :]
                                                      + g[1:2] * _flash_result(h, acc_ref)
                                                      + g[2:3] * _flash_result(HEADS + h, acc_ref))
    o_ref[0] = ot_ref[...].T.astype(o_ref.dtype)


def _slc_win_attention(far, q_t, ksw, vs_t, vw_t, sel, rep, ocmp, g_t, bslc, bwin):
    b, s, _ = ksw.shape
    nq = s // QT
    nkv = s // KT
    n_blk = s // SLC_BLOCK
    tile = lambda height: pl.BlockSpec((1, height, QT), lambda bb, i: (bb * nq + i, 0, 0))
    whole = lambda a: pl.BlockSpec(a.shape, lambda bb, i: (0,) * a.ndim)
    return pl.pallas_call(
        _slc_win_kernel,
        grid=(b, nq),
        in_specs=[pl.BlockSpec(memory_space=pltpu.SMEM), tile(GROUP),
                  pl.BlockSpec((1, s, 2 * HEAD_DIM), lambda bb, i: (bb, 0, 0)),
                  pl.BlockSpec((1, nkv, V_ROWS, KT), lambda bb, i: (bb, 0, 0, 0)),
                  pl.BlockSpec((1, nkv, V_ROWS, KT), lambda bb, i: (bb, 0, 0, 0)),
                  tile(n_blk), whole(rep), tile(GROUP), tile(GATE_ROWS), whole(bslc), whole(bwin)],
        out_specs=pl.BlockSpec((1, QT, GROUP), lambda bb, i: (bb, i, 0)),
        out_shape=jax.ShapeDtypeStruct((b, s, GROUP), BF16),
        scratch_shapes=[pltpu.VMEM((2 * HEADS, 2 * HEAD_DIM, QT), BF16)] + _flash_scratch(2 * HEADS, MXU_LOOKAHEAD)
        + [pltpu.VMEM((GROUP, QT), F32), pltpu.VMEM((n_blk * SEL_REP, QT), F32)],
        compiler_params=pltpu.CompilerParams(dimension_semantics=("arbitrary", "arbitrary"),
                                             vmem_limit_bytes=VMEM_LIMIT),
        name="nsa_selected_window_attention",
    )(far, q_t, ksw, vs_t.reshape(b, nkv, V_ROWS, KT), vw_t.reshape(b, nkv, V_ROWS, KT), sel, rep, ocmp, g_t,
      bslc, bwin)


def _out_kernel(x_ref, a0_ref, a1_ref, a2_ref, l0_ref, l1_ref, l2_ref, ob_ref, oc_ref, od_ref, gate_ref, e_ref,
                w_ref, o_ref, unfold_ref):
    rows = x_ref.shape[0]

    def unfolded(ref, rate):
        width = ref.shape[1] // rate
        for rho in range(rate):
            for part in range(width // LANES):
                c0 = rho * width + part * LANES
                unfold_ref[part, pl.ds(rho, rows // rate, stride=rate), :] = ref[:, c0:c0 + LANES].astype(F32)
        return jnp.concatenate([unfold_ref[part] for part in range(width // LANES)], axis=1)

    a0, l0 = a0_ref[...], l0_ref[...]
    a1, l1 = unfolded(a1_ref, FOLD_RATES[0]), unfolded(l1_ref, FOLD_RATES[0])
    a2, l2 = unfolded(a2_ref, FOLD_RATES[1]), unfolded(l2_ref, FOLD_RATES[1])
    mx = jnp.maximum(jnp.maximum(l0, l1), l2)
    e0, e1, e2 = jnp.exp2(l0 - mx), jnp.exp2(l1 - mx), jnp.exp2(l2 - mx)
    den = e0 + e1 + e2

    def per_head_lanes(w):
        hi = w.astype(BF16)
        lo = (w - hi.astype(F32)).astype(BF16)
        return (jnp.dot(hi, e_ref[...], preferred_element_type=F32)
                + jnp.dot(lo, e_ref[...], preferred_element_type=F32))

    o_a = per_head_lanes(e0 / den) * a0 + per_head_lanes(e1 / den) * a1 + per_head_lanes(e2 / den) * a2
    y = jnp.concatenate([o_a, ob_ref[...].astype(F32), oc_ref[...].astype(F32), od_ref[...].astype(F32)], axis=1)
    g = gate_ref[...].astype(F32)
    y = y * (g * jax.nn.sigmoid(g))
    o_ref[...] = x_ref[...] + jnp.dot(y.astype(BF16), w_ref[...], preferred_element_type=F32)


def _out_projection(x2, a_outs, a_lses, o_b, o_c, o_d, gate, w_out):
    m, d = x2.shape
    rowblk = lambda width: pl.BlockSpec((PROJ_ROWS, width), lambda i: (i, 0))
    folded = lambda width, rate: pl.BlockSpec((PROJ_ROWS // rate, rate * width), lambda i: (i, 0))
    head_of_lane = np.arange(GROUP) // HEAD_DIM
    expand = jnp.asarray((np.arange(LANES)[:, None] == head_of_lane[None, :]).astype(np.float32), BF16)
    return pl.pallas_call(
        _out_kernel,
        grid=(m // PROJ_ROWS,),
        in_specs=[rowblk(d)] + [folded(GROUP, rate) for rate in (1,) + FOLD_RATES]
        + [folded(LANES, rate) for rate in (1,) + FOLD_RATES] + [rowblk(GROUP)] * 3
        + [rowblk(N_MIXERS * GROUP), pl.BlockSpec((LANES, GROUP), lambda i: (0, 0)),
           pl.BlockSpec((N_MIXERS * GROUP, d), lambda i: (0, 0))],
        out_specs=rowblk(d),
        out_shape=jax.ShapeDtypeStruct((m, d), F32),
        scratch_shapes=[pltpu.VMEM((GROUP // LANES, PROJ_ROWS, LANES), F32)],
        compiler_params=pltpu.CompilerParams(dimension_semantics=("arbitrary",), vmem_limit_bytes=VMEM_LIMIT),
        name="out_projection",
    )(x2, *a_outs, *a_lses, o_b, o_c, o_d, gate, expand, w_out)


def _block_diag_mean(group):
    idx = np.arange(GROUP) // group
    return jnp.asarray((idx[:, None] == idx[None, :]).astype(np.float32) / group, BF16)


def _layer_weights(w_in, qk_gain, qk_gain_diff):
    d = w_in.shape[0]
    sizes = (GROUP,) * 3 + (GROUP, GROUP // 2, GROUP // 2) + (GROUP,) * 3 + (GROUP,) + (HEAD_DIM,) * 6 \
        + (HEADS * 3, N_MIXERS * GROUP)
    offs = np.concatenate([[0], np.cumsum(sizes)])
    col = lambda n: w_in[:, offs[n]:offs[n + 1]]
    (a_q, a_k, a_v, b_q, b_k, b_v, c_q, c_k, c_v, d_q, d_kc, d_vc, d_ks, d_vs, d_kw, d_vw, d_g, gate) = \
        [col(n) for n in range(18)]
    rep_kv = lambda w: jnp.repeat(w.reshape(d, 2, HEAD_DIM), 2, axis=1).reshape(d, GROUP)
    wrm = jnp.concatenate([a_q, a_k, a_v, b_q, rep_kv(b_k), rep_kv(b_v), c_k, d_kc, d_vc, d_ks, d_kw, gate], axis=1)
    wt = jnp.concatenate([c_q, c_v, d_q, d_vs, d_vw, d_g, jnp.zeros((d, GATE_ROWS - HEADS * 3), w_in.dtype)], axis=1).T
    g = qk_gain
    ones = lambda n: jnp.ones((n,), F32)
    tile4 = lambda v: jnp.tile(v, HEADS)
    scale = HEAD_DIM ** -0.5 * LOG2E
    grm = jnp.concatenate([tile4(g[0]) * scale, tile4(g[1]), ones(GROUP), tile4(g[2]) * scale, tile4(g[3]),
                           ones(GROUP), jnp.tile(qk_gain_diff[1], 2 * HEADS), ones(2 * HEAD_DIM), g[6], g[7],
                           ones(N_MIXERS * GROUP)])
    gt = jnp.concatenate([jnp.tile(qk_gain_diff[0], 2 * HEADS) * (DIFF_QK_DIM ** -0.5 * LOG2E), ones(GROUP),
                          tile4(g[4]) * scale, ones(2 * HEAD_DIM + GATE_ROWS)])
    return wrm.astype(BF16), wt.astype(BF16), grm.reshape(1, -1), gt.reshape(-1, 1)


def _compress_weights(cmp_pos, cmp_w1, cmp_b1, cmp_w2, cmp_b2):
    half = CMP_LEN // 2
    pos = jnp.concatenate([cmp_pos[0], cmp_pos[1]], axis=-1)
    ptop = pos[:half].reshape(1, -1)
    pbot = pos[half:].reshape(1, -1)
    w1 = cmp_w1.reshape(2, CMP_LEN, HEAD_DIM, CMP_HIDDEN)
    zeros = jnp.zeros_like(w1[0])
    w1cat = jnp.concatenate([jnp.concatenate([w1[0], zeros], axis=-1),
                             jnp.concatenate([zeros, w1[1]], axis=-1)], axis=1)
    w1t = w1cat[:half].reshape(half * 2 * HEAD_DIM, 2 * CMP_HIDDEN).astype(BF16)
    w1b = w1cat[half:].reshape(half * 2 * HEAD_DIM, 2 * CMP_HIDDEN).astype(BF16)
    b1 = jnp.concatenate([cmp_b1[0], cmp_b1[1]]).reshape(1, -1)
    return (ptop, pbot, w1t, w1b, b1, cmp_w2[0].astype(BF16), cmp_b2[0].reshape(1, -1),
            cmp_w2[1].T.astype(BF16), cmp_b2[1].reshape(-1, 1))


def kernel(x, rel_bias_table, norm_w, w_in, w_out, qk_gain, qk_gain_diff, attn_sinks, diff_lambda, diff_subln,
           cmp_pos, cmp_w1, cmp_b1, cmp_w2, cmp_b2):
    b, s, d = x.shape
    depth = w_in.shape[0]
    n_c = s // CMP_STRIDE
    n_blk = s // SLC_BLOCK
    assert s % (BAND_TILE * DILATED_CONFIGS[-1][1]) == 0 and s % PROJ_ROWS == 0 and d == N_MIXERS * GROUP

    table = rel_bias_table.astype(F32)
    band_bias = [_build_bias(table, head0=0, n_d=1, rows=BAND_TILE, cols=2 * BAND_TILE, base0=BAND_TILE, dstep=0,
                             rs=1, cs=-1, dscale=rate, max_dist=window // rate, scale=LOG2E)
                 for window, rate in DILATED_CONFIGS]
    swa_bias = _build_bias(table, head0=HEADS, n_d=1, rows=BAND_TILE, cols=2 * BAND_TILE, base0=BAND_TILE, dstep=0,
                           rs=1, cs=-1, max_dist=SWA_WINDOW - 1, scale=LOG2E)
    flash_tiles = dict(rows=KT, cols=QT, base0=0, dstep=QT, rs=-1, cs=1, scale=LOG2E)
    diff_bias = _build_bias(table, head0=2 * HEADS, n_d=N_BIAS_TILES, d_valid=N_NEAR + 1, **flash_tiles)
    slc_bias = _build_bias(table, head0=3 * HEADS, n_d=N_BIAS_TILES, d_valid=N_NEAR + 1, **flash_tiles)
    win_bias = _build_bias(table, head0=3 * HEADS, n_d=N_WIN, max_dist=NSA_WINDOW - 1, **flash_tiles)
    far_bias = table[NUM_BUCKETS - 1] * LOG2E
    cmp_bias = _build_bias(table, head0=3 * HEADS, n_d=1, rows=n_c, cols=s, base0=-(CMP_LEN - 1), dstep=0,
                           rs=-CMP_STRIDE, cs=1, r_valid=n_c - 1, col_tile=4 * QT, scale=LOG2E)
    e64, e32 = _block_diag_mean(HEAD_DIM), _block_diag_mean(DIFF_QK_DIM)
    rep_idx = np.arange(n_blk * SEL_REP) // SEL_REP
    rep = jnp.asarray((rep_idx[:, None] == np.arange(n_blk)[None, :]).astype(np.float32), BF16)
    no_sink = jnp.zeros((HEADS,), F32)

    x2 = x.reshape(b * s, d)
    w_in_bf16 = w_in.astype(BF16)
    for layer in range(depth):
        wrm, wt, grm, gt = _layer_weights(w_in_bf16[layer], qk_gain[layer], qk_gain_diff[layer])
        (a_q, a_k, a_v, b_q, b_k, b_v, c_k, kvc, ksw, gate, a_q4, a_k4, a_v4, a_q16, a_k16, a_v16,
         c_qt, c_vt, d_qt, d_vst, d_vwt, d_gt) = _project(x2, norm_w[layer].reshape(1, d), wrm, wt, grm, gt, e64, e32)
        seq = lambda t: t.reshape(b, s, t.shape[-1])
        per_batch = lambda t: t.reshape(b, t.shape[0] // b, t.shape[1])
        flat = lambda t: t.reshape(b * t.shape[1], t.shape[2])
        a_in = ((a_q, a_k, a_v), (a_q4, a_k4, a_v4), (a_q16, a_k16, a_v16))
        a_res = [_banded(*map(per_batch, a_in[n]), band_bias[n], no_sink, rate, False)
                 for n, (_, rate) in enumerate(DILATED_CONFIGS)]
        o_b, _ = _banded(seq(b_q), seq(b_k), seq(b_v), swa_bias, attn_sinks[layer].astype(F32) * LOG2E, 1, True)
        lambda_init = 0.8 - 0.6 * math.exp(-0.3 * layer)
        o_c = _diff_attention(far_bias[2 * HEADS:3 * HEADS], c_qt, seq(c_k), c_vt, diff_bias, diff_lambda[layer].astype(F32),
                              diff_subln[layer].reshape(HEAD_DIM, 1).astype(F32), lambda_init)
        cw = _compress_weights(cmp_pos[layer], cmp_w1[layer], cmp_b1[layer], cmp_w2[layer], cmp_b2[layer])
        kc, vct = _compress(kvc.reshape(b, n_c, CMP_STRIDE * 2 * HEAD_DIM), *cw, qk_gain[layer, 5].reshape(1, -1))
        o_cmp, sel = _cmp_attention(d_qt, kc, vct, cmp_bias, b)
        o_d = _slc_win_attention(far_bias[3 * HEADS:4 * HEADS], d_qt, seq(ksw), d_vst, d_vwt, sel, rep, o_cmp, d_gt,
                                 slc_bias, win_bias)
        x2 = _out_projection(x2, [flat(r[0]) for r in a_res], [flat(r[1]) for r in a_res], o_b.reshape(b * s, GROUP),
                             o_c.reshape(b * s, GROUP), o_d.reshape(b * s, GROUP), gate, w_out[layer].astype(BF16))
    return x2.reshape(b, s, d)
```

```python
import functools
import itertools
import math

import numpy as np
import jax
import jax.numpy as jnp
from jax import lax
from jax.experimental import pallas as pl
from jax.experimental.pallas import tpu as pltpu

F32 = jnp.float32
BF16 = jnp.bfloat16

HEAD_DIM = 64
HEADS = 4
GROUP = HEADS * HEAD_DIM
N_MIXERS = 4
NUM_BUCKETS = 32
REL_MAX_DIST = 2048
DILATED_CONFIGS = ((128, 1), (512, 4), (2048, 16))
FOLD_RATES = tuple(rate for _, rate in DILATED_CONFIGS if rate > 1)
SWA_WINDOW = 128
DIFF_QK_DIM = HEAD_DIM // 2
CMP_LEN = 32
CMP_STRIDE = 16
CMP_HIDDEN = 256
SLC_BLOCK = 64
SLC_TOPK = 16
CMP_PARTS = 4
NSA_WINDOW = 512
RMS_EPS = 1e-6
NEG_INF = -1e30
FORCE_SELECT = 1e9
TINY = 1e-30
LOG2E = math.log2(math.e)

PROJ_ROWS = 512
BAND_TILE = 128
BAND_STEP = 1024
BAND_LOOKAHEAD = 8
BIAS_ROW_CHUNK = 64
LANES = 128
QT = 256
KT = 256
VMEM_LIMIT = 56 * 1024 * 1024
MXU_LOOKAHEAD = 4
V_ROWS = HEAD_DIM + 16

NT_DIMS = (((1,), (1,)), ((), ()))


def _t5_thresholds():
    n = np.arange(0, 4 * REL_MAX_DIST)
    max_exact = NUM_BUCKETS // 2
    nf = np.maximum(n, 1).astype(np.float32)
    large = max_exact + (np.log(nf / np.float32(max_exact)) / np.float32(math.log(REL_MAX_DIST / max_exact))
                         * np.float32(NUM_BUCKETS - max_exact)).astype(np.int32)
    bucket = np.where(n < max_exact, n, np.minimum(large, NUM_BUCKETS - 1))
    return [int(np.argmax(bucket >= b)) for b in range(NUM_BUCKETS)]


T5_THRESHOLDS = _t5_thresholds()
FAR_DIST = T5_THRESHOLDS[-1]


def _bias_kernel(tbl_ref, out_ref, *, head0, base0, dstep, rs, cs, dscale, max_dist, r_valid, d_valid, col_tile,
                 scale):
    h = pl.program_id(0)
    n_d, rows, cols = out_ref.shape[1:]
    chunk = BIAS_ROW_CHUNK if rows % BIAS_ROW_CHUNK == 0 else rows
    bucket_of = lambda n: max(b for b in range(NUM_BUCKETS) if T5_THRESHOLDS[b] <= n)
    for d, c0, r0 in itertools.product(range(n_d), range(0, cols, col_tile), range(0, rows, chunk)):
        out = out_ref.at[0, d, r0:r0 + chunk, c0:c0 + col_tile]
        origin = base0 + d * dstep + r0 * rs + c0 * cs
        corners = [origin + dr * rs + dc * cs for dr in (0, chunk - 1) for dc in (0, col_tile - 1)]
        lo, hi = min(corners), max(corners)
        rows_valid = min(chunk, r_valid - r0)
        if hi < 0 or lo > max_dist or d >= d_valid or rows_valid <= 0:
            out[...] = jnp.full((chunk, col_tile), NEG_INF, F32)
            continue
        r = lax.broadcasted_iota(jnp.int32, (chunk, col_tile), 0)
        c = lax.broadcasted_iota(jnp.int32, (chunk, col_tile), 1)
        dist = origin + r * rs + c * cs
        first, last = bucket_of(max(lo, 0) * dscale), bucket_of(min(hi, max_dist) * dscale)
        val = jnp.full((chunk, col_tile), tbl_ref[first, head0 + h] * scale, F32)
        for b in range(first + 1, last + 1):
            val = jnp.where(dist * dscale >= T5_THRESHOLDS[b], tbl_ref[b, head0 + h] * scale, val)
        valid = [cond for needed, cond in ((lo < 0, dist >= 0), (hi > max_dist, dist <= max_dist),
                                           (rows_valid < chunk, r < rows_valid)) if needed]
        if valid:
            val = jnp.where(functools.reduce(jnp.logical_and, valid), val, NEG_INF)
        out[...] = val


def _build_bias(table, *, head0, n_d, rows, cols, base0, dstep, rs, cs, dscale=1, max_dist=1 << 30,
                r_valid=1 << 30, d_valid=1 << 30, col_tile=None, scale=1.0):
    col_tile = cols if col_tile is None else col_tile
    kern = functools.partial(_bias_kernel, head0=head0, base0=base0, dstep=dstep, rs=rs, cs=cs, dscale=dscale,
                             max_dist=max_dist, r_valid=r_valid, d_valid=d_valid, col_tile=col_tile, scale=scale)
    return pl.pallas_call(
        kern,
        grid=(HEADS,),
        in_specs=[pl.BlockSpec(memory_space=pltpu.SMEM)],
        out_specs=pl.BlockSpec((1, n_d, rows, cols), lambda h: (h, 0, 0, 0)),
        out_shape=jax.ShapeDtypeStruct((HEADS, n_d, rows, cols), F32),
        compiler_params=pltpu.CompilerParams(dimension_semantics=("arbitrary",), vmem_limit_bytes=VMEM_LIMIT),
        name="rel_bias_tiles",
    )(table)


RM_AQ, RM_AK, RM_AV = 0, 256, 512
RM_BQ, RM_BK, RM_BV = 768, 1024, 1280
RM_CK = 1536
RM_KVC = 1792
RM_KSW = 1920
RM_GATE = 2048
RM_COLS = 3072
TR_CQ, TR_CV, TR_DQ, TR_DVS, TR_DVW, TR_DG = 0, 256, 512, 768, 832, 896
GATE_ROWS = 16
TR_ROWS = TR_DG + GATE_ROWS


def _proj_kernel(x_ref, nw_ref, wrm_ref, wt_ref, grm_ref, gt_ref, e64_ref, e32_ref,
                 aq_ref, ak_ref, av_ref, bq_ref, bk_ref, bv_ref, ck_ref, kvc_ref, ksw_ref, gate_ref,
                 aq4_ref, ak4_ref, av4_ref, aq16_ref, ak16_ref, av16_ref,
                 cq_ref, cv_ref, dq_ref, dvs_ref, dvw_ref, dg_ref, fold_ref):
    x = x_ref[...]
    ms = jnp.mean(x * x, axis=-1, keepdims=True)
    xn = (x * lax.rsqrt(ms + RMS_EPS) * nw_ref[...]).astype(BF16)
    rows = x.shape[0]

    def rm(c0, width):
        return jnp.dot(xn, wrm_ref[:, c0:c0 + width], preferred_element_type=F32)

    def rm_normed(c0, width, e_ref):
        h = rm(c0, width)
        msq = jnp.dot((h * h).astype(BF16), e_ref[0:width, 0:width], preferred_element_type=F32)
        return h * lax.rsqrt(msq + RMS_EPS) * grm_ref[:, c0:c0 + width]

    def put_folded(val, ref, folded_refs):
        ref[...] = val.astype(ref.dtype)
        for half in range(GROUP // LANES):
            fold_ref[half] = val[:, half * LANES:(half + 1) * LANES]
        for rate, fref in zip(FOLD_RATES, folded_refs):
            for rho in range(rate):
                for half in range(GROUP // LANES):
                    c0 = rho * GROUP + half * LANES
                    fref[:, c0:c0 + LANES] = fold_ref[half, pl.ds(rho, rows // rate, stride=rate), :].astype(fref.dtype)

    put_folded(rm_normed(RM_AQ, GROUP, e64_ref), aq_ref, (aq4_ref, aq16_ref))
    put_folded(rm_normed(RM_AK, GROUP, e64_ref), ak_ref, (ak4_ref, ak16_ref))
    put_folded(rm(RM_AV, GROUP), av_ref, (av4_ref, av16_ref))
    bq_ref[...] = rm_normed(RM_BQ, GROUP, e64_ref).astype(bq_ref.dtype)
    bk_ref[...] = rm_normed(RM_BK, GROUP, e64_ref).astype(bk_ref.dtype)
    bv_ref[...] = rm(RM_BV, GROUP).astype(bv_ref.dtype)
    ck_ref[...] = rm_normed(RM_CK, GROUP, e32_ref).astype(ck_ref.dtype)
    kvc_ref[...] = rm(RM_KVC, 2 * HEAD_DIM).astype(kvc_ref.dtype)
    ksw_ref[...] = rm_normed(RM_KSW, 2 * HEAD_DIM, e64_ref).astype(ksw_ref.dtype)
    gate_ref[...] = rm(RM_GATE, N_MIXERS * GROUP).astype(gate_ref.dtype)

    key_major = lax.dot_general(wt_ref[...], xn, NT_DIMS, preferred_element_type=F32)

    def tr(r0, height):
        return key_major[r0:r0 + height]

    def tr_normed(r0, height, group):
        h3 = tr(r0, height).reshape(height // group, group, rows)
        msq = jnp.mean(h3 * h3, axis=1, keepdims=True)
        return (h3 * lax.rsqrt(msq + RMS_EPS)).reshape(height, rows) * gt_ref[r0:r0 + height, :]

    def put(ref, val):
        for t in range(rows // QT):
            ref[t] = val[:, t * QT:(t + 1) * QT].astype(ref.dtype)

    def with_ones(v):
        ones = jnp.ones((V_ROWS - HEAD_DIM, rows), F32)
        parts = []
        for h in range(v.shape[0] // HEAD_DIM):
            parts += [v[h * HEAD_DIM:(h + 1) * HEAD_DIM], ones]
        return jnp.concatenate(parts, axis=0)

    put(cq_ref, tr_normed(TR_CQ, GROUP, DIFF_QK_DIM))
    put(cv_ref, with_ones(tr(TR_CV, GROUP)))
    put(dq_ref, tr_normed(TR_DQ, GROUP, HEAD_DIM))
    put(dvs_ref, with_ones(tr(TR_DVS, HEAD_DIM)))
    put(dvw_ref, with_ones(tr(TR_DVW, HEAD_DIM)))
    put(dg_ref, tr(TR_DG, GATE_ROWS))


def _project(x2, nw, wrm, wt, grm, gt, e64, e32):
    m, d = x2.shape
    nt = m // QT
    tpr = PROJ_ROWS // QT
    const = lambda shape: pl.BlockSpec(shape, lambda i: (0,) * len(shape))
    rm_out = lambda width, dtype: (jax.ShapeDtypeStruct((m, width), dtype),
                                   pl.BlockSpec((PROJ_ROWS, width), lambda i: (i, 0)))
    tr_out = lambda height, dtype: (jax.ShapeDtypeStruct((nt, height, QT), dtype),
                                    pl.BlockSpec((tpr, height, QT), lambda i: (i, 0, 0)))
    outs = [rm_out(GROUP, BF16)] * 7 + [rm_out(2 * HEAD_DIM, F32), rm_out(2 * HEAD_DIM, BF16),
                                        rm_out(N_MIXERS * GROUP, BF16)]
    fold_out = lambda rate: (jax.ShapeDtypeStruct((m // rate, rate * GROUP), BF16),
                             pl.BlockSpec((PROJ_ROWS // rate, rate * GROUP), lambda i: (i, 0)))
    outs += [fold_out(rate) for rate in FOLD_RATES for _ in range(3)]
    outs += [tr_out(GROUP, BF16), tr_out(HEADS * V_ROWS, BF16), tr_out(GROUP, BF16), tr_out(V_ROWS, BF16),
             tr_out(V_ROWS, BF16), tr_out(GATE_ROWS, F32)]
    return pl.pallas_call(
        _proj_kernel,
        grid=(m // PROJ_ROWS,),
        in_specs=[pl.BlockSpec((PROJ_ROWS, d), lambda i: (i, 0)), const((1, d)), const((d, RM_COLS)),
                  const((TR_ROWS, d)), const((1, RM_COLS)), const((TR_ROWS, 1)), const((GROUP, GROUP)),
                  const((GROUP, GROUP))],
        out_specs=[o[1] for o in outs],
        out_shape=[o[0] for o in outs],
        scratch_shapes=[pltpu.VMEM((GROUP // LANES, PROJ_ROWS, LANES), F32)],
        compiler_params=pltpu.CompilerParams(dimension_semantics=("arbitrary",), vmem_limit_bytes=VMEM_LIMIT),
        name="in_projection",
    )(x2, nw, wrm, wt, grm, gt, e64, e32)


def _band_kernel(sink_ref, q_ref, kp_ref, kc_ref, vp_ref, vc_ref, bias_ref, o_ref, lse_ref, p_ref, *, use_sink):
    i = pl.program_id(2)
    n_blocks = q_ref.shape[1] // BAND_TILE
    head_q = lax.broadcasted_iota(jnp.int32, (BAND_TILE, GROUP), 1) // HEAD_DIM
    head_v = lax.broadcasted_iota(jnp.int32, (2 * BAND_TILE, GROUP), 1) // HEAD_DIM
    lane = lax.broadcasted_iota(jnp.int32, (BAND_TILE, LANES), 1)
    in_prev = lax.broadcasted_iota(jnp.int32, (1, 2 * BAND_TILE), 1) < BAND_TILE
    no_prev = jnp.where(in_prev & (i == 0), NEG_INF, 0.0).astype(F32)

    def window(cur_ref, prev_ref, m):
        if m == 0:
            return jnp.concatenate([prev_ref[0], cur_ref[0, 0:BAND_TILE, :]], axis=0)
        return cur_ref[0, (m - 1) * BAND_TILE:(m + 1) * BAND_TILE, :]

    def scores(n):
        m, h = divmod(n, HEADS)
        q = q_ref[0, m * BAND_TILE:(m + 1) * BAND_TILE, :]
        qh = jnp.where(head_q == h, q, jnp.zeros_like(q))
        bias = bias_ref[h, 0] + no_prev if m == 0 else bias_ref[h, 0]
        return lax.dot_general(qh, window(kc_ref, kp_ref, m), NT_DIMS, preferred_element_type=F32) + bias

    lse_tiles = {}

    def update(n, s):
        m, h = divmod(n, HEADS)
        mx = jnp.max(s, axis=1, keepdims=True)
        if use_sink:
            mx = jnp.maximum(mx, sink_ref[h])
        p = jnp.exp2(s - mx)
        den = jnp.sum(p, axis=1, keepdims=True)
        if use_sink:
            den = den + jnp.exp2(sink_ref[h] - mx)
        p_ref[m % 2, :, h * 2 * BAND_TILE:(h + 1) * 2 * BAND_TILE] = (p * (1.0 / den)).astype(BF16)
        lse_tiles[m] = jnp.where(lane == h, mx + jnp.log2(den), lse_tiles.get(m, jnp.zeros((BAND_TILE, LANES), F32)))
        if h == HEADS - 1:
            v = window(vc_ref, vp_ref, m)
            v_heads = jnp.concatenate([jnp.where(head_v == hh, v, jnp.zeros_like(v)) for hh in range(HEADS)], axis=0)
            rows = slice(m * BAND_TILE, (m + 1) * BAND_TILE)
            o_ref[0, rows, :] = jnp.dot(p_ref[m % 2], v_heads, preferred_element_type=F32).astype(o_ref.dtype)
            lse_ref[0, rows, :] = lse_tiles.pop(m)

    _staggered(n_blocks * HEADS, scores, update, ahead=BAND_LOOKAHEAD)


def _banded(q, k, v, bias, sink, rate, use_sink):
    b, ln, _ = q.shape
    step = min(BAND_STEP, ln)
    per_step = step // BAND_TILE
    cur = pl.BlockSpec((1, step, GROUP), lambda bb, r, i: (bb, i, r))
    prev = pl.BlockSpec((1, BAND_TILE, GROUP), lambda bb, r, i: (bb, jnp.maximum(i * per_step - 1, 0), r))
    o, lse = pl.pallas_call(
        functools.partial(_band_kernel, use_sink=use_sink),
        grid=(b, rate, ln // step),
        in_specs=[pl.BlockSpec(memory_space=pltpu.SMEM), cur, prev, cur, prev, cur,
                  pl.BlockSpec((HEADS, 1, BAND_TILE, 2 * BAND_TILE), lambda bb, r, i: (0, 0, 0, 0))],
        out_specs=[cur, pl.BlockSpec((1, step, LANES), lambda bb, r, i: (bb, i, r))],
        out_shape=[jax.ShapeDtypeStruct((b, ln, rate * GROUP), BF16), jax.ShapeDtypeStruct((b, ln, rate * LANES), F32)],
        scratch_shapes=[pltpu.VMEM((2, BAND_TILE, HEADS * 2 * BAND_TILE), BF16)],
        compiler_params=pltpu.CompilerParams(dimension_semantics=("arbitrary",) * 3),
        name=f"banded_attention_r{rate}",
    )(sink, q, k, k, v, v, bias)
    return o, lse


def _flash_reset(m_ref, acc_ref):
    m_ref[...] = jnp.full(m_ref.shape, NEG_INF, F32)
    acc_ref[...] = jnp.zeros(acc_ref.shape, F32)


def _flash_update(n, s, v_t, m_ref, acc_ref, shift=None):
    m_old = m_ref[n]
    if shift is None:
        m_new = jnp.maximum(m_old, jnp.max(s, axis=0, keepdims=True))
        p = jnp.exp2(s - m_new)
    else:
        m_new = jnp.maximum(m_old, jnp.max(s, axis=0, keepdims=True) + shift)
        p = jnp.exp2(s - (m_new - shift))
    alpha = jnp.exp2(m_old - m_new)
    acc_ref[n] = alpha * acc_ref[n] + jnp.dot(v_t, p.astype(BF16), preferred_element_type=F32)
    m_ref[n] = m_new


def _flash_result(n, acc_ref):
    return acc_ref[n, 0:HEAD_DIM, :] / acc_ref[n, HEAD_DIM:HEAD_DIM + 1, :]


def _staggered(n_items, scores, update, ahead=MXU_LOOKAHEAD):
    pending = {n: scores(n) for n in range(min(ahead, n_items))}
    for n in range(n_items):
        if n + ahead < n_items:
            pending[n + ahead] = scores(n + ahead)
        update(n, pending.pop(n))


def _pipelined_tiles(first, n_tiles, n_chains, group, load_tile, scores, update, next_ref, left_by_previous=None,
                     last_of_sweep=False):
    ahead = next_ref.shape[0]
    n_items = group * n_chains
    assert ahead <= n_chains

    def body(trip, _, issue_next=True):
        base = first + trip * group
        tiles, pending = {}, {}
        for n in range(n_items):
            cur = next_ref[n] if n < ahead else pending.pop(n)
            if n + ahead < n_items or issue_next:
                g, c = divmod(n + ahead, n_chains)
                if g not in tiles:
                    tiles[g] = load_tile(base + g)
                new = scores(tiles[g], base + g, c)
                if n + ahead < n_items:
                    pending[n + ahead] = new
                else:
                    next_ref[n + ahead - n_items] = new
            update(base + n // n_chains, n % n_chains, cur)

    if left_by_previous is None:
        first_tile = load_tile(first)
        for n in range(ahead):
            next_ref[n] = scores(first_tile, first, n)
    else:
        for n in range(ahead):
            next_ref[n] = next_ref[n] + left_by_previous(first, n)
    n_trips = (n_tiles + group - 1) // group
    if last_of_sweep:
        lax.fori_loop(0, n_trips - 1, body, None)
        body(n_trips - 1, None, issue_next=False)
    else:
        lax.fori_loop(0, n_trips, body, None)


def _flash_scratch(chains, ahead):
    return [pltpu.VMEM((chains, 1, QT), F32), pltpu.VMEM((chains, V_ROWS, QT), F32),
            pltpu.VMEM((ahead, KT, QT), F32)]


N_NEAR = -(-(FAR_DIST + KT - 1) // QT)
N_BIAS_TILES = N_NEAR + 2
DIFF_TILE_GROUP = 2
SLC_TILE_GROUP = 2
FAR_TILE_GROUP = 4


def _bias_tile_index(i, j):
    return jnp.where(j > i, N_NEAR + 1, jnp.minimum(i - j, N_NEAR))


def _whole_far_groups(i, group):
    return jnp.maximum(i - (N_NEAR - 1), 0) // group * group


def _diff_kernel(far_ref, q_ref, k_ref, v_ref, bias_ref, lam_ref, subln_ref, o_ref, qz_ref, m_ref, acc_ref, next_ref,
                 ot_ref, *, lambda_init):
    i = pl.program_id(1)
    q = q_ref[0]
    row = lax.broadcasted_iota(jnp.int32, (GROUP, QT), 0) // DIFF_QK_DIM
    for n in range(2 * HEADS):
        qz_ref[n] = jnp.where(row == n, q, jnp.zeros_like(q))
    _flash_reset(m_ref, acc_ref)

    def load_tile(j):
        return k_ref[0, pl.ds(pl.multiple_of(jnp.minimum(j, i) * KT, KT), KT), :]

    def values(j, n):
        h = n // 2
        return v_ref[0, jnp.minimum(j, i), h * V_ROWS:(h + 1) * V_ROWS, :]

    n_far = _whole_far_groups(i, FAR_TILE_GROUP)
    _pipelined_tiles(0, n_far, 2 * HEADS, FAR_TILE_GROUP, load_tile,
                     lambda k, j, n: jnp.dot(k, qz_ref[n], preferred_element_type=F32),
                     lambda j, n, s: _flash_update(n, s, values(j, n), m_ref, acc_ref, shift=far_ref[n // 2]),
                     next_ref)

    def scores(k, j, n):
        return jnp.dot(k, qz_ref[n], preferred_element_type=F32) + bias_ref[n // 2, _bias_tile_index(i, j)]

    _pipelined_tiles(n_far, i + 1 - n_far, 2 * HEADS, DIFF_TILE_GROUP, load_tile, scores,
                     lambda j, n, s: _flash_update(n, s, values(j, n), m_ref, acc_ref), next_ref,
                     left_by_previous=lambda j, n: bias_ref[n // 2, _bias_tile_index(i, j)], last_of_sweep=True)

    lam_p = lam_ref[...]
    lam = (jnp.exp(jnp.sum(lam_p[0:1] * lam_p[1:2], axis=1, keepdims=True))
           - jnp.exp(jnp.sum(lam_p[2:3] * lam_p[3:4], axis=1, keepdims=True)) + lambda_init)
    for h in range(HEADS):
        o = _flash_result(2 * h, acc_ref) - lam * _flash_result(2 * h + 1, acc_ref)
        msq = jnp.mean(o * o, axis=0, keepdims=True)
        ot_ref[h * HEAD_DIM:(h + 1) * HEAD_DIM, :] = (o * lax.rsqrt(msq + RMS_EPS) * subln_ref[...]
                                                      * (1.0 - lambda_init))
    o_ref[0] = ot_ref[...].T.astype(o_ref.dtype)


def _diff_attention(far, q_t, k, v_t, bias, lam_p, subln, lambda_init):
    b, s, _ = k.shape
    nq = s // QT
    nkv = s // KT
    v4 = v_t.reshape(b, nkv, HEADS * V_ROWS, KT)
    return pl.pallas_call(
        functools.partial(_diff_kernel, lambda_init=lambda_init),
        grid=(b, nq),
        in_specs=[pl.BlockSpec(memory_space=pltpu.SMEM),
                  pl.BlockSpec((1, GROUP, QT), lambda bb, i: (bb * nq + i, 0, 0)),
                  pl.BlockSpec((1, s, GROUP), lambda bb, i: (bb, 0, 0)),
                  pl.BlockSpec((1, nkv, HEADS * V_ROWS, KT), lambda bb, i: (bb, 0, 0, 0)),
                  pl.BlockSpec((HEADS, N_BIAS_TILES, KT, QT), lambda bb, i: (0, 0, 0, 0)),
                  pl.BlockSpec((4, DIFF_QK_DIM), lambda bb, i: (0, 0)),
                  pl.BlockSpec((HEAD_DIM, 1), lambda bb, i: (0, 0))],
        out_specs=pl.BlockSpec((1, QT, GROUP), lambda bb, i: (bb, i, 0)),
        out_shape=jax.ShapeDtypeStruct((b, s, GROUP), BF16),
        scratch_shapes=[pltpu.VMEM((2 * HEADS, GROUP, QT), BF16)] + _flash_scratch(2 * HEADS, MXU_LOOKAHEAD)
        + [pltpu.VMEM((GROUP, QT), F32)],
        compiler_params=pltpu.CompilerParams(dimension_semantics=("arbitrary", "arbitrary"),
                                             vmem_limit_bytes=VMEM_LIMIT),
        name="diff_attention",
    )(far, q_t, k, v4, bias, lam_p, subln)


def _compress_kernel(ch_ref, ptop_ref, pbot_ref, w1t_ref, w1b_ref, b1_ref, w2k_ref, b2k_ref, w2v_ref, b2v_ref,
                     gk_ref, kc_ref, vct_ref):
    ch = ch_ref[0]
    n_c = ch.shape[0]
    u = jnp.dot((ch + ptop_ref[...]).astype(BF16), w1t_ref[...], preferred_element_type=F32)
    v = jnp.dot((ch + pbot_ref[...]).astype(BF16), w1b_ref[...], preferred_element_type=F32)
    v_next = pltpu.roll(v, n_c - 1, 0)
    hid = jax.nn.gelu(u + v_next + b1_ref[...])
    hk = hid[:, :CMP_HIDDEN].astype(BF16)
    hv = hid[:, CMP_HIDDEN:].astype(BF16)
    kc = jnp.dot(hk, w2k_ref[...], preferred_element_type=F32) + b2k_ref[...]
    msq = jnp.mean(kc * kc, axis=-1, keepdims=True)
    kc_ref[0] = (kc * lax.rsqrt(msq + RMS_EPS) * gk_ref[...]).astype(kc_ref.dtype)
    vct = lax.dot_general(w2v_ref[...], hv, NT_DIMS, preferred_element_type=F32) + b2v_ref[...]
    vct_ref[0] = vct.astype(vct_ref.dtype)


def _compress(chunks, ptop, pbot, w1t, w1b, b1, w2k, b2k, w2v, b2v, gk):
    b, n_c, width = chunks.shape
    const = lambda a: pl.BlockSpec(a.shape, lambda bb: (0,) * a.ndim)
    params = (ptop, pbot, w1t, w1b, b1, w2k, b2k, w2v, b2v, gk)
    return pl.pallas_call(
        _compress_kernel,
        grid=(b,),
        in_specs=[pl.BlockSpec((1, n_c, width), lambda bb: (bb, 0, 0))] + [const(a) for a in params],
        out_specs=[pl.BlockSpec((1, n_c, HEAD_DIM), lambda bb: (bb, 0, 0)),
                   pl.BlockSpec((1, HEAD_DIM, n_c), lambda bb: (bb, 0, 0))],
        out_shape=[jax.ShapeDtypeStruct((b, n_c, HEAD_DIM), BF16), jax.ShapeDtypeStruct((b, HEAD_DIM, n_c), BF16)],
        compiler_params=pltpu.CompilerParams(dimension_semantics=("arbitrary",), vmem_limit_bytes=VMEM_LIMIT),
        name="nsa_compress",
    )(chunks, *params)


def _cmp_attn_kernel(q_ref, kc_ref, vct_ref, bias_ref, o_ref, sel_ref, p_ref, *, n_sel):
    i = pl.program_id(0)
    n_tiles = pl.num_programs(0)
    for part in range(1, CMP_PARTS + 1):
        @pl.when((i * CMP_PARTS >= (part - 1) * n_tiles) & (i * CMP_PARTS < part * n_tiles))
        def _(part=part):
            _cmp_attn_body(i, kc_ref.shape[1] * part // CMP_PARTS, q_ref, kc_ref, vct_ref, bias_ref, o_ref, sel_ref,
                           p_ref, n_sel)


def _cmp_attn_body(i, n_c, q_ref, kc_ref, vct_ref, bias_ref, o_ref, sel_ref, p_ref, n_sel):
    kc = kc_ref[0, 0:n_c, :]
    vct = vct_ref[0, :, 0:n_c]
    n_blk = n_c * CMP_STRIDE // SLC_BLOCK
    probs = []

    def scores(h):
        return (jnp.dot(kc, q_ref[0, h * HEAD_DIM:(h + 1) * HEAD_DIM, :], preferred_element_type=F32)
                + bias_ref[h, 0, 0:n_c, :])

    def update(h, s):
        m = jnp.maximum(jnp.max(s, axis=0, keepdims=True), 0.5 * NEG_INF)
        p = jnp.exp2(s - m)
        den = jnp.sum(p, axis=0, keepdims=True)
        p = p * (1.0 / jnp.maximum(den, TINY))
        o_ref[0, h * HEAD_DIM:(h + 1) * HEAD_DIM, :] = jnp.dot(vct, p.astype(BF16),
                                                               preferred_element_type=F32).astype(o_ref.dtype)
        probs.append(p)

    _staggered(HEADS, scores, update)
    psum = (probs[0] + probs[1]) + (probs[2] + probs[3])
    per_blk = SLC_BLOCK // CMP_STRIDE
    halves = []
    for half in range(QT // LANES):
        p_ref[half, 0:8, :] = jnp.zeros((8, LANES), F32)
        p_ref[half, 8:8 + n_c, :] = psum[:, half * LANES:(half + 1) * LANES]
        p_ref[half, 8 + n_c:16 + n_c, :] = jnp.zeros((8, LANES), F32)
        acc = p_ref[half, pl.ds(7, n_blk, stride=per_blk), :]
        for t in range(per_blk):
            acc = acc + p_ref[half, pl.ds(8 + t, n_blk, stride=per_blk), :]
        halves.append(acc)
    imp = jnp.concatenate(halves, axis=1)
    blk = lax.broadcasted_iota(jnp.int32, (n_blk, QT), 0)
    cur = (i * QT + lax.broadcasted_iota(jnp.int32, (n_blk, QT), 1)) // SLC_BLOCK
    forced = (blk == 0) | (blk == cur) | (blk == cur - 1)
    val = jnp.where(forced, FORCE_SELECT, jnp.where(blk <= cur, imp, NEG_INF))
    sel = jnp.zeros((n_blk, QT), jnp.bool_)
    for _ in range(n_sel):
        top = jnp.max(val, axis=0, keepdims=True)
        idx = jnp.min(jnp.where(val == top, blk, n_blk), axis=0, keepdims=True)
        hit = blk == idx
        sel = sel | hit
        val = jnp.where(hit, -3.0e38, val)
    sel_ref[0, 0:n_blk, :] = jnp.where(sel, 1.0, 0.0).astype(sel_ref.dtype)
    if n_blk < sel_ref.shape[1]:
        sel_ref[0, n_blk:, :] = jnp.zeros((sel_ref.shape[1] - n_blk, QT), sel_ref.dtype)


def _cmp_attention(q_t, kc, vct, bias, b):
    nt = q_t.shape[0]
    nq = nt // b
    n_c = kc.shape[1]
    n_blk = nq * QT // SLC_BLOCK
    return pl.pallas_call(
        functools.partial(_cmp_attn_kernel, n_sel=min(SLC_TOPK, n_blk)),
        grid=(nq, b),
        in_specs=[pl.BlockSpec((1, GROUP, QT), lambda i, bb: (bb * nq + i, 0, 0)),
                  pl.BlockSpec((1, n_c, HEAD_DIM), lambda i, bb: (bb, 0, 0)),
                  pl.BlockSpec((1, HEAD_DIM, n_c), lambda i, bb: (bb, 0, 0)),
                  pl.BlockSpec((HEADS, 1, n_c, QT), lambda i, bb: (0, 0, 0, i))],
        out_specs=[pl.BlockSpec((1, GROUP, QT), lambda i, bb: (bb * nq + i, 0, 0)),
                   pl.BlockSpec((1, n_blk, QT), lambda i, bb: (bb * nq + i, 0, 0))],
        out_shape=[jax.ShapeDtypeStruct((nt, GROUP, QT), BF16), jax.ShapeDtypeStruct((nt, n_blk, QT), BF16)],
        scratch_shapes=[pltpu.VMEM((QT // LANES, n_c + 16, LANES), F32)],
        compiler_params=pltpu.CompilerParams(dimension_semantics=("arbitrary", "arbitrary"),
                                             vmem_limit_bytes=VMEM_LIMIT),
        name="nsa_compressed_attention",
    )(q_t, kc, vct, bias)


SEL_REP = 8
N_WIN = -(-(NSA_WINDOW - 1 + KT - 1) // QT)


def _slc_win_kernel(far_ref, q_ref, ksw_ref, vs_ref, vw_ref, sel_ref, rep_ref, ocmp_ref, g_ref, bslc_ref, bwin_ref,
                    o_ref, qz_ref, m_ref, acc_ref, next_ref, ot_ref, mask_ref):
    i = pl.program_id(1)
    sel8 = jnp.dot(rep_ref[...], sel_ref[0], preferred_element_type=F32)
    mask_ref[...] = (sel8 - 1.0) * (-NEG_INF)
    blocks_per_tile = KT // SLC_BLOCK
    mrows = blocks_per_tile * SEL_REP
    zeros = jnp.zeros((HEAD_DIM, QT), BF16)
    for h in range(HEADS):
        qh = q_ref[0, h * HEAD_DIM:(h + 1) * HEAD_DIM, :]
        qz_ref[h] = jnp.concatenate([qh, zeros], axis=0)
        qz_ref[HEADS + h] = jnp.concatenate([zeros, qh], axis=0)
    _flash_reset(m_ref, acc_ref)

    def load_keys(j):
        return ksw_ref[0, pl.ds(pl.multiple_of(j * KT, KT), KT), :]

    def load_tile(j):
        j = jnp.minimum(j, i)
        m8 = mask_ref[pl.ds(pl.multiple_of(j * mrows, mrows), mrows), :]
        mask = jnp.broadcast_to(m8.reshape(blocks_per_tile, 1, SEL_REP, QT),
                                (blocks_per_tile, SLC_BLOCK // SEL_REP, SEL_REP, QT)).reshape(KT, QT)
        return load_keys(j), mask

    def values(j):
        return vs_ref[0, jnp.minimum(j, i)]

    n_far = _whole_far_groups(i, FAR_TILE_GROUP)
    _pipelined_tiles(0, n_far, HEADS, FAR_TILE_GROUP, load_tile,
                     lambda tile, j, h: jnp.dot(tile[0], qz_ref[h], preferred_element_type=F32) + tile[1],
                     lambda j, h, s: _flash_update(h, s, values(j), m_ref, acc_ref, shift=far_ref[h]), next_ref)

    def slc_scores(tile, j, h):
        k, mask = tile
        return jnp.dot(k, qz_ref[h], preferred_element_type=F32) + mask + bslc_ref[h, _bias_tile_index(i, j)]

    _pipelined_tiles(n_far, i + 1 - n_far, HEADS, SLC_TILE_GROUP, load_tile, slc_scores,
                     lambda j, h, s: _flash_update(h, s, values(j), m_ref, acc_ref), next_ref,
                     left_by_previous=lambda j, h: bslc_ref[h, _bias_tile_index(i, j)], last_of_sweep=True)

    def win_tile(n):
        d = N_WIN - 1 - n // HEADS
        return d, n % HEADS, jnp.maximum(i - d, 0)

    def win_scores(n):
        d, h, j = win_tile(n)
        missing = jnp.where(i < d, NEG_INF, 0.0).astype(F32)
        return jnp.dot(load_keys(j), qz_ref[HEADS + h], preferred_element_type=F32) + (bwin_ref[h, d] + missing)

    def win_update(n, s):
        _, h, j = win_tile(n)
        _flash_update(HEADS + h, s, vw_ref[0, j], m_ref, acc_ref)

    _staggered(N_WIN * HEADS, win_scores, win_update)

    for h in range(HEADS):
        g = jax.nn.sigmoid(g_ref[0, 3 * h:3 * h + 3, :])
        ot_ref[h * HEAD_DIM:(h + 1) * HEAD_DIM, :] = (g[0:1] * ocmp_ref[0, h * HEAD_DIM:(h + 1) * HEAD_DIM, :]
                                                      + g[1:2] * _flash_result(h, acc_ref)
                                                      + g[2:3] * _flash_result(HEADS + h, acc_ref))
    o_ref[0] = ot_ref[...].T.astype(o_ref.dtype)


def _slc_win_attention(far, q_t, ksw, vs_t, vw_t, sel, rep, ocmp, g_t, bslc, bwin):
    b, s, _ = ksw.shape
    nq = s // QT
    nkv = s // KT
    n_blk = s // SLC_BLOCK
    tile = lambda height: pl.BlockSpec((1, height, QT), lambda bb, i: (bb * nq + i, 0, 0))
    whole = lambda a: pl.BlockSpec(a.shape, lambda bb, i: (0,) * a.ndim)
    return pl.pallas_call(
        _slc_win_kernel,
        grid=(b, nq),
        in_specs=[pl.BlockSpec(memory_space=pltpu.SMEM), tile(GROUP),
                  pl.BlockSpec((1, s, 2 * HEAD_DIM), lambda bb, i: (bb, 0, 0)),
                  pl.BlockSpec((1, nkv, V_ROWS, KT), lambda bb, i: (bb, 0, 0, 0)),
                  pl.BlockSpec((1, nkv, V_ROWS, KT), lambda bb, i: (bb, 0, 0, 0)),
                  tile(n_blk), whole(rep), tile(GROUP), tile(GATE_ROWS), whole(bslc), whole(bwin)],
        out_specs=pl.BlockSpec((1, QT, GROUP), lambda bb, i: (bb, i, 0)),
        out_shape=jax.ShapeDtypeStruct((b, s, GROUP), BF16),
        scratch_shapes=[pltpu.VMEM((2 * HEADS, 2 * HEAD_DIM, QT), BF16)] + _flash_scratch(2 * HEADS, MXU_LOOKAHEAD)
        + [pltpu.VMEM((GROUP, QT), F32), pltpu.VMEM((n_blk * SEL_REP, QT), F32)],
        compiler_params=pltpu.CompilerParams(dimension_semantics=("arbitrary", "arbitrary"),
                                             vmem_limit_bytes=VMEM_LIMIT),
        name="nsa_selected_window_attention",
    )(far, q_t, ksw, vs_t.reshape(b, nkv, V_ROWS, KT), vw_t.reshape(b, nkv, V_ROWS, KT), sel, rep, ocmp, g_t,
      bslc, bwin)


def _out_kernel(x_ref, a0_ref, a1_ref, a2_ref, l0_ref, l1_ref, l2_ref, ob_ref, oc_ref, od_ref, gate_ref, e_ref,
                w_ref, o_ref, unfold_ref):
    rows = x_ref.shape[0]

    def unfolded(ref, rate):
        width = ref.shape[1] // rate
        for rho in range(rate):
            for part in range(width // LANES):
                c0 = rho * width + part * LANES
                unfold_ref[part, pl.ds(rho, rows // rate, stride=rate), :] = ref[:, c0:c0 + LANES].astype(F32)
        return jnp.concatenate([unfold_ref[part] for part in range(width // LANES)], axis=1)

    a0, l0 = a0_ref[...], l0_ref[...]
    a1, l1 = unfolded(a1_ref, FOLD_RATES[0]), unfolded(l1_ref, FOLD_RATES[0])
    a2, l2 = unfolded(a2_ref, FOLD_RATES[1]), unfolded(l2_ref, FOLD_RATES[1])
    mx = jnp.maximum(jnp.maximum(l0, l1), l2)
    e0, e1, e2 = jnp.exp2(l0 - mx), jnp.exp2(l1 - mx), jnp.exp2(l2 - mx)
    den = e0 + e1 + e2

    def per_head_lanes(w):
        hi = w.astype(BF16)
        lo = (w - hi.astype(F32)).astype(BF16)
        return (jnp.dot(hi, e_ref[...], preferred_element_type=F32)
                + jnp.dot(lo, e_ref[...], preferred_element_type=F32))

    o_a = per_head_lanes(e0 / den) * a0 + per_head_lanes(e1 / den) * a1 + per_head_lanes(e2 / den) * a2
    y = jnp.concatenate([o_a, ob_ref[...].astype(F32), oc_ref[...].astype(F32), od_ref[...].astype(F32)], axis=1)
    g = gate_ref[...].astype(F32)
    y = y * (g * jax.nn.sigmoid(g))
    o_ref[...] = x_ref[...] + jnp.dot(y.astype(BF16), w_ref[...], preferred_element_type=F32)


def _out_projection(x2, a_outs, a_lses, o_b, o_c, o_d, gate, w_out):
    m, d = x2.shape
    rowblk = lambda width: pl.BlockSpec((PROJ_ROWS, width), lambda i: (i, 0))
    folded = lambda width, rate: pl.BlockSpec((PROJ_ROWS // rate, rate * width), lambda i: (i, 0))
    head_of_lane = np.arange(GROUP) // HEAD_DIM
    expand = jnp.asarray((np.arange(LANES)[:, None] == head_of_lane[None, :]).astype(np.float32), BF16)
    return pl.pallas_call(
        _out_kernel,
        grid=(m // PROJ_ROWS,),
        in_specs=[rowblk(d)] + [folded(GROUP, rate) for rate in (1,) + FOLD_RATES]
        + [folded(LANES, rate) for rate in (1,) + FOLD_RATES] + [rowblk(GROUP)] * 3
        + [rowblk(N_MIXERS * GROUP), pl.BlockSpec((LANES, GROUP), lambda i: (0, 0)),
           pl.BlockSpec((N_MIXERS * GROUP, d), lambda i: (0, 0))],
        out_specs=rowblk(d),
        out_shape=jax.ShapeDtypeStruct((m, d), F32),
        scratch_shapes=[pltpu.VMEM((GROUP // LANES, PROJ_ROWS, LANES), F32)],
        compiler_params=pltpu.CompilerParams(dimension_semantics=("arbitrary",), vmem_limit_bytes=VMEM_LIMIT),
        name="out_projection",
    )(x2, *a_outs, *a_lses, o_b, o_c, o_d, gate, expand, w_out)


def _block_diag_mean(group):
    idx = np.arange(GROUP) // group
    return jnp.asarray((idx[:, None] == idx[None, :]).astype(np.float32) / group, BF16)


def _layer_weights(w_in, qk_gain, qk_gain_diff):
    d = w_in.shape[0]
    sizes = (GROUP,) * 3 + (GROUP, GROUP // 2, GROUP // 2) + (GROUP,) * 3 + (GROUP,) + (HEAD_DIM,) * 6 \
        + (HEADS * 3, N_MIXERS * GROUP)
    offs = np.concatenate([[0], np.cumsum(sizes)])
    col = lambda n: w_in[:, offs[n]:offs[n + 1]]
    (a_q, a_k, a_v, b_q, b_k, b_v, c_q, c_k, c_v, d_q, d_kc, d_vc, d_ks, d_vs, d_kw, d_vw, d_g, gate) = \
        [col(n) for n in range(18)]
    rep_kv = lambda w: jnp.repeat(w.reshape(d, 2, HEAD_DIM), 2, axis=1).reshape(d, GROUP)
    wrm = jnp.concatenate([a_q, a_k, a_v, b_q, rep_kv(b_k), rep_kv(b_v), c_k, d_kc, d_vc, d_ks, d_kw, gate], axis=1)
    wt = jnp.concatenate([c_q, c_v, d_q, d_vs, d_vw, d_g, jnp.zeros((d, GATE_ROWS - HEADS * 3), w_in.dtype)], axis=1).T
    g = qk_gain
    ones = lambda n: jnp.ones((n,), F32)
    tile4 = lambda v: jnp.tile(v, HEADS)
    scale = HEAD_DIM ** -0.5 * LOG2E
    grm = jnp.concatenate([tile4(g[0]) * scale, tile4(g[1]), ones(GROUP), tile4(g[2]) * scale, tile4(g[3]),
                           ones(GROUP), jnp.tile(qk_gain_diff[1], 2 * HEADS), ones(2 * HEAD_DIM), g[6], g[7],
                           ones(N_MIXERS * GROUP)])
    gt = jnp.concatenate([jnp.tile(qk_gain_diff[0], 2 * HEADS) * (DIFF_QK_DIM ** -0.5 * LOG2E), ones(GROUP),
                          tile4(g[4]) * scale, ones(2 * HEAD_DIM + GATE_ROWS)])
    return wrm.astype(BF16), wt.astype(BF16), grm.reshape(1, -1), gt.reshape(-1, 1)


def _compress_weights(cmp_pos, cmp_w1, cmp_b1, cmp_w2, cmp_b2):
    half = CMP_LEN // 2
    pos = jnp.concatenate([cmp_pos[0], cmp_pos[1]], axis=-1)
    ptop = pos[:half].reshape(1, -1)
    pbot = pos[half:].reshape(1, -1)
    w1 = cmp_w1.reshape(2, CMP_LEN, HEAD_DIM, CMP_HIDDEN)
    zeros = jnp.zeros_like(w1[0])
    w1cat = jnp.concatenate([jnp.concatenate([w1[0], zeros], axis=-1),
                             jnp.concatenate([zeros, w1[1]], axis=-1)], axis=1)
    w1t = w1cat[:half].reshape(half * 2 * HEAD_DIM, 2 * CMP_HIDDEN).astype(BF16)
    w1b = w1cat[half:].reshape(half * 2 * HEAD_DIM, 2 * CMP_HIDDEN).astype(BF16)
    b1 = jnp.concatenate([cmp_b1[0], cmp_b1[1]]).reshape(1, -1)
    return (ptop, pbot, w1t, w1b, b1, cmp_w2[0].astype(BF16), cmp_b2[0].reshape(1, -1),
            cmp_w2[1].T.astype(BF16), cmp_b2[1].reshape(-1, 1))


def kernel(x, rel_bias_table, norm_w, w_in, w_out, qk_gain, qk_gain_diff, attn_sinks, diff_lambda, diff_subln,
           cmp_pos, cmp_w1, cmp_b1, cmp_w2, cmp_b2):
    b, s, d = x.shape
    depth = w_in.shape[0]
    n_c = s // CMP_STRIDE
    n_blk = s // SLC_BLOCK
    assert s % (BAND_TILE * DILATED_CONFIGS[-1][1]) == 0 and s % PROJ_ROWS == 0 and d == N_MIXERS * GROUP

    table = rel_bias_table.astype(F32)
    band_bias = [_build_bias(table, head0=0, n_d=1, rows=BAND_TILE, cols=2 * BAND_TILE, base0=BAND_TILE, dstep=0,
                             rs=1, cs=-1, dscale=rate, max_dist=window // rate, scale=LOG2E)
                 for window, rate in DILATED_CONFIGS]
    swa_bias = _build_bias(table, head0=HEADS, n_d=1, rows=BAND_TILE, cols=2 * BAND_TILE, base0=BAND_TILE, dstep=0,
                           rs=1, cs=-1, max_dist=SWA_WINDOW - 1, scale=LOG2E)
    flash_tiles = dict(rows=KT, cols=QT, base0=0, dstep=QT, rs=-1, cs=1, scale=LOG2E)
    diff_bias = _build_bias(table, head0=2 * HEADS, n_d=N_BIAS_TILES, d_valid=N_NEAR + 1, **flash_tiles)
    slc_bias = _build_bias(table, head0=3 * HEADS, n_d=N_BIAS_TILES, d_valid=N_NEAR + 1, **flash_tiles)
    win_bias = _build_bias(table, head0=3 * HEADS, n_d=N_WIN, max_dist=NSA_WINDOW - 1, **flash_tiles)
    far_bias = table[NUM_BUCKETS - 1] * LOG2E
    cmp_bias = _build_bias(table, head0=3 * HEADS, n_d=1, rows=n_c, cols=s, base0=-(CMP_LEN - 1), dstep=0,
                           rs=-CMP_STRIDE, cs=1, r_valid=n_c - 1, col_tile=2 * QT, scale=LOG2E)
    e64, e32 = _block_diag_mean(HEAD_DIM), _block_diag_mean(DIFF_QK_DIM)
    rep_idx = np.arange(n_blk * SEL_REP) // SEL_REP
    rep = jnp.asarray((rep_idx[:, None] == np.arange(n_blk)[None, :]).astype(np.float32), BF16)
    no_sink = jnp.zeros((HEADS,), F32)

    x2 = x.reshape(b * s, d)
    w_in_bf16 = w_in.astype(BF16)
    for layer in range(depth):
        wrm, wt, grm, gt = _layer_weights(w_in_bf16[layer], qk_gain[layer], qk_gain_diff[layer])
        (a_q, a_k, a_v, b_q, b_k, b_v, c_k, kvc, ksw, gate, a_q4, a_k4, a_v4, a_q16, a_k16, a_v16,
         c_qt, c_vt, d_qt, d_vst, d_vwt, d_gt) = _project(x2, norm_w[layer].reshape(1, d), wrm, wt, grm, gt, e64, e32)
        seq = lambda t: t.reshape(b, s, t.shape[-1])
        per_batch = lambda t: t.reshape(b, t.shape[0] // b, t.shape[1])
        flat = lambda t: t.reshape(b * t.shape[1], t.shape[2])
        a_in = ((a_q, a_k, a_v), (a_q4, a_k4, a_v4), (a_q16, a_k16, a_v16))
        a_res = [_banded(*map(per_batch, a_in[n]), band_bias[n], no_sink, rate, False)
                 for n, (_, rate) in enumerate(DILATED_CONFIGS)]
        o_b, _ = _banded(seq(b_q), seq(b_k), seq(b_v), swa_bias, attn_sinks[layer].astype(F32) * LOG2E, 1, True)
        lambda_init = 0.8 - 0.6 * math.exp(-0.3 * layer)
        o_c = _diff_attention(far_bias[2 * HEADS:3 * HEADS], c_qt, seq(c_k), c_vt, diff_bias, diff_lambda[layer].astype(F32),
                              diff_subln[layer].reshape(HEAD_DIM, 1).astype(F32), lambda_init)
        cw = _compress_weights(cmp_pos[layer], cmp_w1[layer], cmp_b1[layer], cmp_w2[layer], cmp_b2[layer])
        kc, vct = _compress(kvc.reshape(b, n_c, CMP_STRIDE * 2 * HEAD_DIM), *cw, qk_gain[layer, 5].reshape(1, -1))
        o_cmp, sel = _cmp_attention(d_qt, kc, vct, cmp_bias, b)
        o_d = _slc_win_attention(far_bias[3 * HEADS:4 * HEADS], d_qt, seq(ksw), d_vst, d_vwt, sel, rep, o_cmp, d_gt,
                                 slc_bias, win_bias)
        x2 = _out_projection(x2, [flat(r[0]) for r in a_res], [flat(r[1]) for r in a_res], o_b.reshape(b * s, GROUP),
                             o_c.reshape(b * s, GROUP), o_d.reshape(b * s, GROUP), gate, w_out[layer].astype(BF16))
    return x2.reshape(b, s, d)
```

```python
import functools
import itertools
import math

import numpy as np
import jax
import jax.numpy as jnp
from jax import lax
from jax.experimental import pallas as pl
from jax.experimental.pallas import tpu as pltpu

F32 = jnp.float32
BF16 = jnp.bfloat16

HEAD_DIM = 64
HEADS = 4
GROUP = HEADS * HEAD_DIM
N_MIXERS = 4
NUM_BUCKETS = 32
REL_MAX_DIST = 2048
DILATED_CONFIGS = ((128, 1), (512, 4), (2048, 16))
FOLD_RATES = tuple(rate for _, rate in DILATED_CONFIGS if rate > 1)
SWA_WINDOW = 128
DIFF_QK_DIM = HEAD_DIM // 2
CMP_LEN = 32
CMP_STRIDE = 16
CMP_HIDDEN = 256
SLC_BLOCK = 64
SLC_TOPK = 16
CMP_PARTS = 4
NSA_WINDOW = 512
RMS_EPS = 1e-6
NEG_INF = -1e30
FORCE_SELECT = 1e9
TINY = 1e-30
LOG2E = math.log2(math.e)

PROJ_ROWS = 512
BAND_TILE = 128
BAND_STEP = 1024
BAND_LOOKAHEAD = 2
BIAS_ROW_CHUNK = 64
LANES = 128
QT = 256
KT = 256
VMEM_LIMIT = 56 * 1024 * 1024
MXU_LOOKAHEAD = 4
V_ROWS = HEAD_DIM + 16

NT_DIMS = (((1,), (1,)), ((), ()))


def _t5_thresholds():
    n = np.arange(0, 4 * REL_MAX_DIST)
    max_exact = NUM_BUCKETS // 2
    nf = np.maximum(n, 1).astype(np.float32)
    large = max_exact + (np.log(nf / np.float32(max_exact)) / np.float32(math.log(REL_MAX_DIST / max_exact))
                         * np.float32(NUM_BUCKETS - max_exact)).astype(np.int32)
    bucket = np.where(n < max_exact, n, np.minimum(large, NUM_BUCKETS - 1))
    return [int(np.argmax(bucket >= b)) for b in range(NUM_BUCKETS)]


T5_THRESHOLDS = _t5_thresholds()
FAR_DIST = T5_THRESHOLDS[-1]


def _bias_kernel(tbl_ref, out_ref, *, head0, base0, dstep, rs, cs, dscale, max_dist, r_valid, d_valid, col_tile,
                 scale):
    h = pl.program_id(0)
    n_d, rows, cols = out_ref.shape[1:]
    chunk = BIAS_ROW_CHUNK if rows % BIAS_ROW_CHUNK == 0 else rows
    bucket_of = lambda n: max(b for b in range(NUM_BUCKETS) if T5_THRESHOLDS[b] <= n)
    for d, c0, r0 in itertools.product(range(n_d), range(0, cols, col_tile), range(0, rows, chunk)):
        out = out_ref.at[0, d, r0:r0 + chunk, c0:c0 + col_tile]
        origin = base0 + d * dstep + r0 * rs + c0 * cs
        corners = [origin + dr * rs + dc * cs for dr in (0, chunk - 1) for dc in (0, col_tile - 1)]
        lo, hi = min(corners), max(corners)
        rows_valid = min(chunk, r_valid - r0)
        if hi < 0 or lo > max_dist or d >= d_valid or rows_valid <= 0:
            out[...] = jnp.full((chunk, col_tile), NEG_INF, F32)
            continue
        r = lax.broadcasted_iota(jnp.int32, (chunk, col_tile), 0)
        c = lax.broadcasted_iota(jnp.int32, (chunk, col_tile), 1)
        dist = origin + r * rs + c * cs
        first, last = bucket_of(max(lo, 0) * dscale), bucket_of(min(hi, max_dist) * dscale)
        val = jnp.full((chunk, col_tile), tbl_ref[first, head0 + h] * scale, F32)
        for b in range(first + 1, last + 1):
            val = jnp.where(dist * dscale >= T5_THRESHOLDS[b], tbl_ref[b, head0 + h] * scale, val)
        valid = [cond for needed, cond in ((lo < 0, dist >= 0), (hi > max_dist, dist <= max_dist),
                                           (rows_valid < chunk, r < rows_valid)) if needed]
        if valid:
            val = jnp.where(functools.reduce(jnp.logical_and, valid), val, NEG_INF)
        out[...] = val


def _build_bias(table, *, head0, n_d, rows, cols, base0, dstep, rs, cs, dscale=1, max_dist=1 << 30,
                r_valid=1 << 30, d_valid=1 << 30, col_tile=None, scale=1.0):
    col_tile = cols if col_tile is None else col_tile
    kern = functools.partial(_bias_kernel, head0=head0, base0=base0, dstep=dstep, rs=rs, cs=cs, dscale=dscale,
                             max_dist=max_dist, r_valid=r_valid, d_valid=d_valid, col_tile=col_tile, scale=scale)
    return pl.pallas_call(
        kern,
        grid=(HEADS,),
        in_specs=[pl.BlockSpec(memory_space=pltpu.SMEM)],
        out_specs=pl.BlockSpec((1, n_d, rows, cols), lambda h: (h, 0, 0, 0)),
        out_shape=jax.ShapeDtypeStruct((HEADS, n_d, rows, cols), F32),
        compiler_params=pltpu.CompilerParams(dimension_semantics=("arbitrary",), vmem_limit_bytes=VMEM_LIMIT),
        name="rel_bias_tiles",
    )(table)


RM_AQ, RM_AK, RM_AV = 0, 256, 512
RM_BQ, RM_BK, RM_BV = 768, 1024, 1280
RM_CK = 1536
RM_KVC = 1792
RM_KSW = 1920
RM_GATE = 2048
RM_COLS = 3072
TR_CQ, TR_CV, TR_DQ, TR_DVS, TR_DVW, TR_DG = 0, 256, 512, 768, 832, 896
GATE_ROWS = 16
TR_ROWS = TR_DG + GATE_ROWS


def _proj_kernel(x_ref, nw_ref, wrm_ref, wt_ref, grm_ref, gt_ref, e64_ref, e32_ref,
                 aq_ref, ak_ref, av_ref, bq_ref, bk_ref, bv_ref, ck_ref, kvc_ref, ksw_ref, gate_ref,
                 aq4_ref, ak4_ref, av4_ref, aq16_ref, ak16_ref, av16_ref,
                 cq_ref, cv_ref, dq_ref, dvs_ref, dvw_ref, dg_ref, fold_ref):
    x = x_ref[...]
    ms = jnp.mean(x * x, axis=-1, keepdims=True)
    xn = (x * lax.rsqrt(ms + RMS_EPS) * nw_ref[...]).astype(BF16)
    rows = x.shape[0]

    def rm(c0, width):
        return jnp.dot(xn, wrm_ref[:, c0:c0 + width], preferred_element_type=F32)

    def rm_normed(c0, width, e_ref):
        h = rm(c0, width)
        msq = jnp.dot((h * h).astype(BF16), e_ref[0:width, 0:width], preferred_element_type=F32)
        return h * lax.rsqrt(msq + RMS_EPS) * grm_ref[:, c0:c0 + width]

    def put_folded(val, ref, folded_refs):
        ref[...] = val.astype(ref.dtype)
        for half in range(GROUP // LANES):
            fold_ref[half] = val[:, half * LANES:(half + 1) * LANES]
        for rate, fref in zip(FOLD_RATES, folded_refs):
            for rho in range(rate):
                for half in range(GROUP // LANES):
                    c0 = rho * GROUP + half * LANES
                    fref[:, c0:c0 + LANES] = fold_ref[half, pl.ds(rho, rows // rate, stride=rate), :].astype(fref.dtype)

    put_folded(rm_normed(RM_AQ, GROUP, e64_ref), aq_ref, (aq4_ref, aq16_ref))
    put_folded(rm_normed(RM_AK, GROUP, e64_ref), ak_ref, (ak4_ref, ak16_ref))
    put_folded(rm(RM_AV, GROUP), av_ref, (av4_ref, av16_ref))
    bq_ref[...] = rm_normed(RM_BQ, GROUP, e64_ref).astype(bq_ref.dtype)
    bk_ref[...] = rm_normed(RM_BK, GROUP, e64_ref).astype(bk_ref.dtype)
    bv_ref[...] = rm(RM_BV, GROUP).astype(bv_ref.dtype)
    ck_ref[...] = rm_normed(RM_CK, GROUP, e32_ref).astype(ck_ref.dtype)
    kvc_ref[...] = rm(RM_KVC, 2 * HEAD_DIM).astype(kvc_ref.dtype)
    ksw_ref[...] = rm_normed(RM_KSW, 2 * HEAD_DIM, e64_ref).astype(ksw_ref.dtype)
    gate_ref[...] = rm(RM_GATE, N_MIXERS * GROUP).astype(gate_ref.dtype)

    key_major = lax.dot_general(wt_ref[...], xn, NT_DIMS, preferred_element_type=F32)

    def tr(r0, height):
        return key_major[r0:r0 + height]

    def tr_normed(r0, height, group):
        h3 = tr(r0, height).reshape(height // group, group, rows)
        msq = jnp.mean(h3 * h3, axis=1, keepdims=True)
        return (h3 * lax.rsqrt(msq + RMS_EPS)).reshape(height, rows) * gt_ref[r0:r0 + height, :]

    def put(ref, val):
        for t in range(rows // QT):
            ref[t] = val[:, t * QT:(t + 1) * QT].astype(ref.dtype)

    def with_ones(v):
        ones = jnp.ones((V_ROWS - HEAD_DIM, rows), F32)
        parts = []
        for h in range(v.shape[0] // HEAD_DIM):
            parts += [v[h * HEAD_DIM:(h + 1) * HEAD_DIM], ones]
        return jnp.concatenate(parts, axis=0)

    put(cq_ref, tr_normed(TR_CQ, GROUP, DIFF_QK_DIM))
    put(cv_ref, with_ones(tr(TR_CV, GROUP)))
    put(dq_ref, tr_normed(TR_DQ, GROUP, HEAD_DIM))
    put(dvs_ref, with_ones(tr(TR_DVS, HEAD_DIM)))
    put(dvw_ref, with_ones(tr(TR_DVW, HEAD_DIM)))
    put(dg_ref, tr(TR_DG, GATE_ROWS))


def _project(x2, nw, wrm, wt, grm, gt, e64, e32):
    m, d = x2.shape
    nt = m // QT
    tpr = PROJ_ROWS // QT
    const = lambda shape: pl.BlockSpec(shape, lambda i: (0,) * len(shape))
    rm_out = lambda width, dtype: (jax.ShapeDtypeStruct((m, width), dtype),
                                   pl.BlockSpec((PROJ_ROWS, width), lambda i: (i, 0)))
    tr_out = lambda height, dtype: (jax.ShapeDtypeStruct((nt, height, QT), dtype),
                                    pl.BlockSpec((tpr, height, QT), lambda i: (i, 0, 0)))
    outs = [rm_out(GROUP, BF16)] * 7 + [rm_out(2 * HEAD_DIM, F32), rm_out(2 * HEAD_DIM, BF16),
                                        rm_out(N_MIXERS * GROUP, BF16)]
    fold_out = lambda rate: (jax.ShapeDtypeStruct((m // rate, rate * GROUP), BF16),
                             pl.BlockSpec((PROJ_ROWS // rate, rate * GROUP), lambda i: (i, 0)))
    outs += [fold_out(rate) for rate in FOLD_RATES for _ in range(3)]
    outs += [tr_out(GROUP, BF16), tr_out(HEADS * V_ROWS, BF16), tr_out(GROUP, BF16), tr_out(V_ROWS, BF16),
             tr_out(V_ROWS, BF16), tr_out(GATE_ROWS, F32)]
    return pl.pallas_call(
        _proj_kernel,
        grid=(m // PROJ_ROWS,),
        in_specs=[pl.BlockSpec((PROJ_ROWS, d), lambda i: (i, 0)), const((1, d)), const((d, RM_COLS)),
                  const((TR_ROWS, d)), const((1, RM_COLS)), const((TR_ROWS, 1)), const((GROUP, GROUP)),
                  const((GROUP, GROUP))],
        out_specs=[o[1] for o in outs],
        out_shape=[o[0] for o in outs],
        scratch_shapes=[pltpu.VMEM((GROUP // LANES, PROJ_ROWS, LANES), F32)],
        compiler_params=pltpu.CompilerParams(dimension_semantics=("arbitrary",), vmem_limit_bytes=VMEM_LIMIT),
        name="in_projection",
    )(x2, nw, wrm, wt, grm, gt, e64, e32)


def _band_kernel(sink_ref, q_ref, kp_ref, kc_ref, vp_ref, vc_ref, bias_ref, o_ref, lse_ref, p_ref, *, use_sink):
    i = pl.program_id(2)
    n_blocks = q_ref.shape[1] // BAND_TILE
    head_q = lax.broadcasted_iota(jnp.int32, (BAND_TILE, GROUP), 1) // HEAD_DIM
    head_v = lax.broadcasted_iota(jnp.int32, (2 * BAND_TILE, GROUP), 1) // HEAD_DIM
    lane = lax.broadcasted_iota(jnp.int32, (BAND_TILE, LANES), 1)
    in_prev = lax.broadcasted_iota(jnp.int32, (1, 2 * BAND_TILE), 1) < BAND_TILE
    no_prev = jnp.where(in_prev & (i == 0), NEG_INF, 0.0).astype(F32)

    def window(cur_ref, prev_ref, m):
        if m == 0:
            return jnp.concatenate([prev_ref[0], cur_ref[0, 0:BAND_TILE, :]], axis=0)
        return cur_ref[0, (m - 1) * BAND_TILE:(m + 1) * BAND_TILE, :]

    def scores(m):
        q = q_ref[0, m * BAND_TILE:(m + 1) * BAND_TILE, :]
        q_heads = jnp.concatenate([jnp.where(head_q == h, q, jnp.zeros_like(q)) for h in range(HEADS)], axis=0)
        bias = bias_ref[:, 0].reshape(HEADS * BAND_TILE, 2 * BAND_TILE)
        if m == 0:
            bias = bias + no_prev
        return lax.dot_general(q_heads, window(kc_ref, kp_ref, m), NT_DIMS, preferred_element_type=F32) + bias

    def update(m, s_heads):
        lse_tile = jnp.zeros((BAND_TILE, LANES), F32)
        for h in range(HEADS):
            s = s_heads[h * BAND_TILE:(h + 1) * BAND_TILE]
            mx = jnp.max(s, axis=1, keepdims=True)
            if use_sink:
                mx = jnp.maximum(mx, sink_ref[h])
            p = jnp.exp2(s - mx)
            den = jnp.sum(p, axis=1, keepdims=True)
            if use_sink:
                den = den + jnp.exp2(sink_ref[h] - mx)
            p_ref[m % 2, :, h * 2 * BAND_TILE:(h + 1) * 2 * BAND_TILE] = (p * (1.0 / den)).astype(BF16)
            lse_tile = jnp.where(lane == h, mx + jnp.log2(den), lse_tile)
        v = window(vc_ref, vp_ref, m)
        v_heads = jnp.concatenate([jnp.where(head_v == h, v, jnp.zeros_like(v)) for h in range(HEADS)], axis=0)
        rows = slice(m * BAND_TILE, (m + 1) * BAND_TILE)
        o_ref[0, rows, :] = jnp.dot(p_ref[m % 2], v_heads, preferred_element_type=F32).astype(o_ref.dtype)
        lse_ref[0, rows, :] = lse_tile

    _staggered(n_blocks, scores, update, ahead=BAND_LOOKAHEAD)


def _banded(q, k, v, bias, sink, rate, use_sink):
    b, ln, _ = q.shape
    step = min(BAND_STEP, ln)
    per_step = step // BAND_TILE
    cur = pl.BlockSpec((1, step, GROUP), lambda bb, r, i: (bb, i, r))
    prev = pl.BlockSpec((1, BAND_TILE, GROUP), lambda bb, r, i: (bb, jnp.maximum(i * per_step - 1, 0), r))
    o, lse = pl.pallas_call(
        functools.partial(_band_kernel, use_sink=use_sink),
        grid=(b, rate, ln // step),
        in_specs=[pl.BlockSpec(memory_space=pltpu.SMEM), cur, prev, cur, prev, cur,
                  pl.BlockSpec((HEADS, 1, BAND_TILE, 2 * BAND_TILE), lambda bb, r, i: (0, 0, 0, 0))],
        out_specs=[cur, pl.BlockSpec((1, step, LANES), lambda bb, r, i: (bb, i, r))],
        out_shape=[jax.ShapeDtypeStruct((b, ln, rate * GROUP), BF16), jax.ShapeDtypeStruct((b, ln, rate * LANES), F32)],
        scratch_shapes=[pltpu.VMEM((2, BAND_TILE, HEADS * 2 * BAND_TILE), BF16)],
        compiler_params=pltpu.CompilerParams(dimension_semantics=("arbitrary",) * 3),
        name=f"banded_attention_r{rate}",
    )(sink, q, k, k, v, v, bias)
    return o, lse


def _flash_reset(m_ref, acc_ref):
    m_ref[...] = jnp.full(m_ref.shape, NEG_INF, F32)
    acc_ref[...] = jnp.zeros(acc_ref.shape, F32)


def _flash_update(n, s, v_t, m_ref, acc_ref, shift=None):
    m_old = m_ref[n]
    if shift is None:
        m_new = jnp.maximum(m_old, jnp.max(s, axis=0, keepdims=True))
        p = jnp.exp2(s - m_new)
    else:
        m_new = jnp.maximum(m_old, jnp.max(s, axis=0, keepdims=True) + shift)
        p = jnp.exp2(s - (m_new - shift))
    alpha = jnp.exp2(m_old - m_new)
    acc_ref[n] = alpha * acc_ref[n] + jnp.dot(v_t, p.astype(BF16), preferred_element_type=F32)
    m_ref[n] = m_new


def _flash_result(n, acc_ref):
    return acc_ref[n, 0:HEAD_DIM, :] / acc_ref[n, HEAD_DIM:HEAD_DIM + 1, :]


def _staggered(n_items, scores, update, ahead=MXU_LOOKAHEAD):
    pending = {n: scores(n) for n in range(min(ahead, n_items))}
    for n in range(n_items):
        if n + ahead < n_items:
            pending[n + ahead] = scores(n + ahead)
        update(n, pending.pop(n))


def _pipelined_tiles(first, n_tiles, n_chains, group, load_tile, scores, update, next_ref, left_by_previous=None,
                     last_of_sweep=False):
    ahead = next_ref.shape[0]
    n_items = group * n_chains
    assert ahead <= n_chains

    def body(trip, _, issue_next=True):
        base = first + trip * group
        tiles, pending = {}, {}
        for n in range(n_items):
            cur = next_ref[n] if n < ahead else pending.pop(n)
            if n + ahead < n_items or issue_next:
                g, c = divmod(n + ahead, n_chains)
                if g not in tiles:
                    tiles[g] = load_tile(base + g)
                new = scores(tiles[g], base + g, c)
                if n + ahead < n_items:
                    pending[n + ahead] = new
                else:
                    next_ref[n + ahead - n_items] = new
            update(base + n // n_chains, n % n_chains, cur)

    if left_by_previous is None:
        first_tile = load_tile(first)
        for n in range(ahead):
            next_ref[n] = scores(first_tile, first, n)
    else:
        for n in range(ahead):
            next_ref[n] = next_ref[n] + left_by_previous(first, n)
    n_trips = (n_tiles + group - 1) // group
    if last_of_sweep:
        lax.fori_loop(0, n_trips - 1, body, None)
        body(n_trips - 1, None, issue_next=False)
    else:
        lax.fori_loop(0, n_trips, body, None)


def _flash_scratch(chains, ahead):
    return [pltpu.VMEM((chains, 1, QT), F32), pltpu.VMEM((chains, V_ROWS, QT), F32),
            pltpu.VMEM((ahead, KT, QT), F32)]


N_NEAR = -(-(FAR_DIST + KT - 1) // QT)
N_BIAS_TILES = N_NEAR + 2
DIFF_TILE_GROUP = 2
SLC_TILE_GROUP = 2
FAR_TILE_GROUP = 4


def _bias_tile_index(i, j):
    return jnp.where(j > i, N_NEAR + 1, jnp.minimum(i - j, N_NEAR))


def _whole_far_groups(i, group):
    return jnp.maximum(i - (N_NEAR - 1), 0) // group * group


def _diff_kernel(far_ref, q_ref, k_ref, v_ref, bias_ref, lam_ref, subln_ref, o_ref, qz_ref, m_ref, acc_ref, next_ref,
                 ot_ref, *, lambda_init):
    i = pl.program_id(1)
    q = q_ref[0]
    row = lax.broadcasted_iota(jnp.int32, (GROUP, QT), 0) // DIFF_QK_DIM
    for n in range(2 * HEADS):
        qz_ref[n] = jnp.where(row == n, q, jnp.zeros_like(q))
    _flash_reset(m_ref, acc_ref)

    def load_tile(j):
        return k_ref[0, pl.ds(pl.multiple_of(jnp.minimum(j, i) * KT, KT), KT), :]

    def values(j, n):
        h = n // 2
        return v_ref[0, jnp.minimum(j, i), h * V_ROWS:(h + 1) * V_ROWS, :]

    n_far = _whole_far_groups(i, FAR_TILE_GROUP)
    _pipelined_tiles(0, n_far, 2 * HEADS, FAR_TILE_GROUP, load_tile,
                     lambda k, j, n: jnp.dot(k, qz_ref[n], preferred_element_type=F32),
                     lambda j, n, s: _flash_update(n, s, values(j, n), m_ref, acc_ref, shift=far_ref[n // 2]),
                     next_ref)

    def scores(k, j, n):
        return jnp.dot(k, qz_ref[n], preferred_element_type=F32) + bias_ref[n // 2, _bias_tile_index(i, j)]

    _pipelined_tiles(n_far, i + 1 - n_far, 2 * HEADS, DIFF_TILE_GROUP, load_tile, scores,
                     lambda j, n, s: _flash_update(n, s, values(j, n), m_ref, acc_ref), next_ref,
                     left_by_previous=lambda j, n: bias_ref[n // 2, _bias_tile_index(i, j)], last_of_sweep=True)

    lam_p = lam_ref[...]
    lam = (jnp.exp(jnp.sum(lam_p[0:1] * lam_p[1:2], axis=1, keepdims=True))
           - jnp.exp(jnp.sum(lam_p[2:3] * lam_p[3:4], axis=1, keepdims=True)) + lambda_init)
    for h in range(HEADS):
        o = _flash_result(2 * h, acc_ref) - lam * _flash_result(2 * h + 1, acc_ref)
        msq = jnp.mean(o * o, axis=0, keepdims=True)
        ot_ref[h * HEAD_DIM:(h + 1) * HEAD_DIM, :] = (o * lax.rsqrt(msq + RMS_EPS) * subln_ref[...]
                                                      * (1.0 - lambda_init))
    o_ref[0] = ot_ref[...].T.astype(o_ref.dtype)


def _diff_attention(far, q_t, k, v_t, bias, lam_p, subln, lambda_init):
    b, s, _ = k.shape
    nq = s // QT
    nkv = s // KT
    v4 = v_t.reshape(b, nkv, HEADS * V_ROWS, KT)
    return pl.pallas_call(
        functools.partial(_diff_kernel, lambda_init=lambda_init),
        grid=(b, nq),
        in_specs=[pl.BlockSpec(memory_space=pltpu.SMEM),
                  pl.BlockSpec((1, GROUP, QT), lambda bb, i: (bb * nq + i, 0, 0)),
                  pl.BlockSpec((1, s, GROUP), lambda bb, i: (bb, 0, 0)),
                  pl.BlockSpec((1, nkv, HEADS * V_ROWS, KT), lambda bb, i: (bb, 0, 0, 0)),
                  pl.BlockSpec((HEADS, N_BIAS_TILES, KT, QT), lambda bb, i: (0, 0, 0, 0)),
                  pl.BlockSpec((4, DIFF_QK_DIM), lambda bb, i: (0, 0)),
                  pl.BlockSpec((HEAD_DIM, 1), lambda bb, i: (0, 0))],
        out_specs=pl.BlockSpec((1, QT, GROUP), lambda bb, i: (bb, i, 0)),
        out_shape=jax.ShapeDtypeStruct((b, s, GROUP), BF16),
        scratch_shapes=[pltpu.VMEM((2 * HEADS, GROUP, QT), BF16)] + _flash_scratch(2 * HEADS, MXU_LOOKAHEAD)
        + [pltpu.VMEM((GROUP, QT), F32)],
        compiler_params=pltpu.CompilerParams(dimension_semantics=("arbitrary", "arbitrary"),
                                             vmem_limit_bytes=VMEM_LIMIT),
        name="diff_attention",
    )(far, q_t, k, v4, bias, lam_p, subln)


def _compress_kernel(ch_ref, ptop_ref, pbot_ref, w1t_ref, w1b_ref, b1_ref, w2k_ref, b2k_ref, w2v_ref, b2v_ref,
                     gk_ref, kc_ref, vct_ref):
    ch = ch_ref[0]
    n_c = ch.shape[0]
    u = jnp.dot((ch + ptop_ref[...]).astype(BF16), w1t_ref[...], preferred_element_type=F32)
    v = jnp.dot((ch + pbot_ref[...]).astype(BF16), w1b_ref[...], preferred_element_type=F32)
    v_next = pltpu.roll(v, n_c - 1, 0)
    hid = jax.nn.gelu(u + v_next + b1_ref[...])
    hk = hid[:, :CMP_HIDDEN].astype(BF16)
    hv = hid[:, CMP_HIDDEN:].astype(BF16)
    kc = jnp.dot(hk, w2k_ref[...], preferred_element_type=F32) + b2k_ref[...]
    msq = jnp.mean(kc * kc, axis=-1, keepdims=True)
    kc_ref[0] = (kc * lax.rsqrt(msq + RMS_EPS) * gk_ref[...]).astype(kc_ref.dtype)
    vct = lax.dot_general(w2v_ref[...], hv, NT_DIMS, preferred_element_type=F32) + b2v_ref[...]
    vct_ref[0] = vct.astype(vct_ref.dtype)


def _compress(chunks, ptop, pbot, w1t, w1b, b1, w2k, b2k, w2v, b2v, gk):
    b, n_c, width = chunks.shape
    const = lambda a: pl.BlockSpec(a.shape, lambda bb: (0,) * a.ndim)
    params = (ptop, pbot, w1t, w1b, b1, w2k, b2k, w2v, b2v, gk)
    return pl.pallas_call(
        _compress_kernel,
        grid=(b,),
        in_specs=[pl.BlockSpec((1, n_c, width), lambda bb: (bb, 0, 0))] + [const(a) for a in params],
        out_specs=[pl.BlockSpec((1, n_c, HEAD_DIM), lambda bb: (bb, 0, 0)),
                   pl.BlockSpec((1, HEAD_DIM, n_c), lambda bb: (bb, 0, 0))],
        out_shape=[jax.ShapeDtypeStruct((b, n_c, HEAD_DIM), BF16), jax.ShapeDtypeStruct((b, HEAD_DIM, n_c), BF16)],
        compiler_params=pltpu.CompilerParams(dimension_semantics=("arbitrary",), vmem_limit_bytes=VMEM_LIMIT),
        name="nsa_compress",
    )(chunks, *params)


def _cmp_attn_kernel(q_ref, kc_ref, vct_ref, bias_ref, o_ref, sel_ref, p_ref, *, n_sel):
    i = pl.program_id(0)
    n_tiles = pl.num_programs(0)
    for part in range(1, CMP_PARTS + 1):
        @pl.when((i * CMP_PARTS >= (part - 1) * n_tiles) & (i * CMP_PARTS < part * n_tiles))
        def _(part=part):
            _cmp_attn_body(i, kc_ref.shape[1] * part // CMP_PARTS, q_ref, kc_ref, vct_ref, bias_ref, o_ref, sel_ref,
                           p_ref, n_sel)


def _cmp_attn_body(i, n_c, q_ref, kc_ref, vct_ref, bias_ref, o_ref, sel_ref, p_ref, n_sel):
    kc = kc_ref[0, 0:n_c, :]
    vct = vct_ref[0, :, 0:n_c]
    n_blk = n_c * CMP_STRIDE // SLC_BLOCK
    probs = []

    def scores(h):
        return (jnp.dot(kc, q_ref[0, h * HEAD_DIM:(h + 1) * HEAD_DIM, :], preferred_element_type=F32)
                + bias_ref[h, 0, 0:n_c, :])

    def update(h, s):
        m = jnp.maximum(jnp.max(s, axis=0, keepdims=True), 0.5 * NEG_INF)
        p = jnp.exp2(s - m)
        den = jnp.sum(p, axis=0, keepdims=True)
        p = p * (1.0 / jnp.maximum(den, TINY))
        o_ref[0, h * HEAD_DIM:(h + 1) * HEAD_DIM, :] = jnp.dot(vct, p.astype(BF16),
                                                               preferred_element_type=F32).astype(o_ref.dtype)
        probs.append(p)

    _staggered(HEADS, scores, update)
    psum = (probs[0] + probs[1]) + (probs[2] + probs[3])
    per_blk = SLC_BLOCK // CMP_STRIDE
    halves = []
    for half in range(QT // LANES):
        p_ref[half, 0:8, :] = jnp.zeros((8, LANES), F32)
        p_ref[half, 8:8 + n_c, :] = psum[:, half * LANES:(half + 1) * LANES]
        p_ref[half, 8 + n_c:16 + n_c, :] = jnp.zeros((8, LANES), F32)
        acc = p_ref[half, pl.ds(7, n_blk, stride=per_blk), :]
        for t in range(per_blk):
            acc = acc + p_ref[half, pl.ds(8 + t, n_blk, stride=per_blk), :]
        halves.append(acc)
    imp = jnp.concatenate(halves, axis=1)
    blk = lax.broadcasted_iota(jnp.int32, (n_blk, QT), 0)
    cur = (i * QT + lax.broadcasted_iota(jnp.int32, (n_blk, QT), 1)) // SLC_BLOCK
    forced = (blk == 0) | (blk == cur) | (blk == cur - 1)
    val = jnp.where(forced, FORCE_SELECT, jnp.where(blk <= cur, imp, NEG_INF))
    sel = jnp.zeros((n_blk, QT), jnp.bool_)
    for _ in range(n_sel):
        top = jnp.max(val, axis=0, keepdims=True)
        idx = jnp.min(jnp.where(val == top, blk, n_blk), axis=0, keepdims=True)
        hit = blk == idx
        sel = sel | hit
        val = jnp.where(hit, -3.0e38, val)
    sel_ref[0, 0:n_blk, :] = jnp.where(sel, 1.0, 0.0).astype(sel_ref.dtype)
    if n_blk < sel_ref.shape[1]:
        sel_ref[0, n_blk:, :] = jnp.zeros((sel_ref.shape[1] - n_blk, QT), sel_ref.dtype)


def _cmp_attention(q_t, kc, vct, bias, b):
    nt = q_t.shape[0]
    nq = nt // b
    n_c = kc.shape[1]
    n_blk = nq * QT // SLC_BLOCK
    return pl.pallas_call(
        functools.partial(_cmp_attn_kernel, n_sel=min(SLC_TOPK, n_blk)),
        grid=(nq, b),
        in_specs=[pl.BlockSpec((1, GROUP, QT), lambda i, bb: (bb * nq + i, 0, 0)),
                  pl.BlockSpec((1, n_c, HEAD_DIM), lambda i, bb: (bb, 0, 0)),
                  pl.BlockSpec((1, HEAD_DIM, n_c), lambda i, bb: (bb, 0, 0)),
                  pl.BlockSpec((HEADS, 1, n_c, QT), lambda i, bb: (0, 0, 0, i))],
        out_specs=[pl.BlockSpec((1, GROUP, QT), lambda i, bb: (bb * nq + i, 0, 0)),
                   pl.BlockSpec((1, n_blk, QT), lambda i, bb: (bb * nq + i, 0, 0))],
        out_shape=[jax.ShapeDtypeStruct((nt, GROUP, QT), BF16), jax.ShapeDtypeStruct((nt, n_blk, QT), BF16)],
        scratch_shapes=[pltpu.VMEM((QT // LANES, n_c + 16, LANES), F32)],
        compiler_params=pltpu.CompilerParams(dimension_semantics=("arbitrary", "arbitrary"),
                                             vmem_limit_bytes=VMEM_LIMIT),
        name="nsa_compressed_attention",
    )(q_t, kc, vct, bias)


SEL_REP = 8
N_WIN = -(-(NSA_WINDOW - 1 + KT - 1) // QT)


def _slc_win_kernel(far_ref, q_ref, ksw_ref, vs_ref, vw_ref, sel_ref, rep_ref, ocmp_ref, g_ref, bslc_ref, bwin_ref,
                    o_ref, qz_ref, m_ref, acc_ref, next_ref, ot_ref, mask_ref):
    i = pl.program_id(1)
    sel8 = jnp.dot(rep_ref[...], sel_ref[0], preferred_element_type=F32)
    mask_ref[...] = (sel8 - 1.0) * (-NEG_INF)
    blocks_per_tile = KT // SLC_BLOCK
    mrows = blocks_per_tile * SEL_REP
    zeros = jnp.zeros((HEAD_DIM, QT), BF16)
    for h in range(HEADS):
        qh = q_ref[0, h * HEAD_DIM:(h + 1) * HEAD_DIM, :]
        qz_ref[h] = jnp.concatenate([qh, zeros], axis=0)
        qz_ref[HEADS + h] = jnp.concatenate([zeros, qh], axis=0)
    _flash_reset(m_ref, acc_ref)

    def load_keys(j):
        return ksw_ref[0, pl.ds(pl.multiple_of(j * KT, KT), KT), :]

    def load_tile(j):
        j = jnp.minimum(j, i)
        m8 = mask_ref[pl.ds(pl.multiple_of(j * mrows, mrows), mrows), :]
        mask = jnp.broadcast_to(m8.reshape(blocks_per_tile, 1, SEL_REP, QT),
                                (blocks_per_tile, SLC_BLOCK // SEL_REP, SEL_REP, QT)).reshape(KT, QT)
        return load_keys(j), mask

    def values(j):
        return vs_ref[0, jnp.minimum(j, i)]

    n_far = _whole_far_groups(i, FAR_TILE_GROUP)
    _pipelined_tiles(0, n_far, HEADS, FAR_TILE_GROUP, load_tile,
                     lambda tile, j, h: jnp.dot(tile[0], qz_ref[h], preferred_element_type=F32) + tile[1],
                     lambda j, h, s: _flash_update(h, s, values(j), m_ref, acc_ref, shift=far_ref[h]), next_ref)

    def slc_scores(tile, j, h):
        k, mask = tile
        return jnp.dot(k, qz_ref[h], preferred_element_type=F32) + mask + bslc_ref[h, _bias_tile_index(i, j)]

    _pipelined_tiles(n_far, i + 1 - n_far, HEADS, SLC_TILE_GROUP, load_tile, slc_scores,
                     lambda j, h, s: _flash_update(h, s, values(j), m_ref, acc_ref), next_ref,
                     left_by_previous=lambda j, h: bslc_ref[h, _bias_tile_index(i, j)], last_of_sweep=True)

    def win_tile(n):
        d = N_WIN - 1 - n // HEADS
        return d, n % HEADS, jnp.maximum(i - d, 0)

    def win_scores(n):
        d, h, j = win_tile(n)
        missing = jnp.where(i < d, NEG_INF, 0.0).astype(F32)
        return jnp.dot(load_keys(j), qz_ref[HEADS + h], preferred_element_type=F32) + (bwin_ref[h, d] + missing)

    def win_update(n, s):
        _, h, j = win_tile(n)
        _flash_update(HEADS + h, s, vw_ref[0, j], m_ref, acc_ref)

    _staggered(N_WIN * HEADS, win_scores, win_update)

    for h in range(HEADS):
        g = jax.nn.sigmoid(g_ref[0, 3 * h:3 * h + 3, :])
        ot_ref[h * HEAD_DIM:(h + 1) * HEAD_DIM, :] = (g[0:1] * ocmp_ref[0, h * HEAD_DIM:(h + 1) * HEAD_DIM, :]
                                                      + g[1:2] * _flash_result(h, acc_ref)
                                                      + g[2:3] * _flash_result(HEADS + h, acc_ref))
    o_ref[0] = ot_ref[...].T.astype(o_ref.dtype)


def _slc_win_attention(far, q_t, ksw, vs_t, vw_t, sel, rep, ocmp, g_t, bslc, bwin):
    b, s, _ = ksw.shape
    nq = s // QT
    nkv = s // KT
    n_blk = s // SLC_BLOCK
    tile = lambda height: pl.BlockSpec((1, height, QT), lambda bb, i: (bb * nq + i, 0, 0))
    whole = lambda a: pl.BlockSpec(a.shape, lambda bb, i: (0,) * a.ndim)
    return pl.pallas_call(
        _slc_win_kernel,
        grid=(b, nq),
        in_specs=[pl.BlockSpec(memory_space=pltpu.SMEM), tile(GROUP),
                  pl.BlockSpec((1, s, 2 * HEAD_DIM), lambda bb, i: (bb, 0, 0)),
                  pl.BlockSpec((1, nkv, V_ROWS, KT), lambda bb, i: (bb, 0, 0, 0)),
                  pl.BlockSpec((1, nkv, V_ROWS, KT), lambda bb, i: (bb, 0, 0, 0)),
                  tile(n_blk), whole(rep), tile(GROUP), tile(GATE_ROWS), whole(bslc), whole(bwin)],
        out_specs=pl.BlockSpec((1, QT, GROUP), lambda bb, i: (bb, i, 0)),
        out_shape=jax.ShapeDtypeStruct((b, s, GROUP), BF16),
        scratch_shapes=[pltpu.VMEM((2 * HEADS, 2 * HEAD_DIM, QT), BF16)] + _flash_scratch(2 * HEADS, MXU_LOOKAHEAD)
        + [pltpu.VMEM((GROUP, QT), F32), pltpu.VMEM((n_blk * SEL_REP, QT), F32)],
        compiler_params=pltpu.CompilerParams(dimension_semantics=("arbitrary", "arbitrary"),
                                             vmem_limit_bytes=VMEM_LIMIT),
        name="nsa_selected_window_attention",
    )(far, q_t, ksw, vs_t.reshape(b, nkv, V_ROWS, KT), vw_t.reshape(b, nkv, V_ROWS, KT), sel, rep, ocmp, g_t,
      bslc, bwin)


def _out_kernel(x_ref, a0_ref, a1_ref, a2_ref, l0_ref, l1_ref, l2_ref, ob_ref, oc_ref, od_ref, gate_ref, e_ref,
                w_ref, o_ref, unfold_ref):
    rows = x_ref.shape[0]

    def unfolded(ref, rate):
        width = ref.shape[1] // rate
        for rho in range(rate):
            for part in range(width // LANES):
                c0 = rho * width + part * LANES
                unfold_ref[part, pl.ds(rho, rows // rate, stride=rate), :] = ref[:, c0:c0 + LANES].astype(F32)
        return jnp.concatenate([unfold_ref[part] for part in range(width // LANES)], axis=1)

    a0, l0 = a0_ref[...], l0_ref[...]
    a1, l1 = unfolded(a1_ref, FOLD_RATES[0]), unfolded(l1_ref, FOLD_RATES[0])
    a2, l2 = unfolded(a2_ref, FOLD_RATES[1]), unfolded(l2_ref, FOLD_RATES[1])
    mx = jnp.maximum(jnp.maximum(l0, l1), l2)
    e0, e1, e2 = jnp.exp2(l0 - mx), jnp.exp2(l1 - mx), jnp.exp2(l2 - mx)
    den = e0 + e1 + e2

    def per_head_lanes(w):
        hi = w.astype(BF16)
        lo = (w - hi.astype(F32)).astype(BF16)
        return (jnp.dot(hi, e_ref[...], preferred_element_type=F32)
                + jnp.dot(lo, e_ref[...], preferred_element_type=F32))

    o_a = per_head_lanes(e0 / den) * a0 + per_head_lanes(e1 / den) * a1 + per_head_lanes(e2 / den) * a2
    y = jnp.concatenate([o_a, ob_ref[...].astype(F32), oc_ref[...].astype(F32), od_ref[...].astype(F32)], axis=1)
    g = gate_ref[...].astype(F32)
    y = y * (g * jax.nn.sigmoid(g))
    o_ref[...] = x_ref[...] + jnp.dot(y.astype(BF16), w_ref[...], preferred_element_type=F32)


def _out_projection(x2, a_outs, a_lses, o_b, o_c, o_d, gate, w_out):
    m, d = x2.shape
    rowblk = lambda width: pl.BlockSpec((PROJ_ROWS, width), lambda i: (i, 0))
    folded = lambda width, rate: pl.BlockSpec((PROJ_ROWS // rate, rate * width), lambda i: (i, 0))
    head_of_lane = np.arange(GROUP) // HEAD_DIM
    expand = jnp.asarray((np.arange(LANES)[:, None] == head_of_lane[None, :]).astype(np.float32), BF16)
    return pl.pallas_call(
        _out_kernel,
        grid=(m // PROJ_ROWS,),
        in_specs=[rowblk(d)] + [folded(GROUP, rate) for rate in (1,) + FOLD_RATES]
        + [folded(LANES, rate) for rate in (1,) + FOLD_RATES] + [rowblk(GROUP)] * 3
        + [rowblk(N_MIXERS * GROUP), pl.BlockSpec((LANES, GROUP), lambda i: (0, 0)),
           pl.BlockSpec((N_MIXERS * GROUP, d), lambda i: (0, 0))],
        out_specs=rowblk(d),
        out_shape=jax.ShapeDtypeStruct((m, d), F32),
        scratch_shapes=[pltpu.VMEM((GROUP // LANES, PROJ_ROWS, LANES), F32)],
        compiler_params=pltpu.CompilerParams(dimension_semantics=("arbitrary",), vmem_limit_bytes=VMEM_LIMIT),
        name="out_projection",
    )(x2, *a_outs, *a_lses, o_b, o_c, o_d, gate, expand, w_out)


def _block_diag_mean(group):
    idx = np.arange(GROUP) // group
    return jnp.asarray((idx[:, None] == idx[None, :]).astype(np.float32) / group, BF16)


def _layer_weights(w_in, qk_gain, qk_gain_diff):
    d = w_in.shape[0]
    sizes = (GROUP,) * 3 + (GROUP, GROUP // 2, GROUP // 2) + (GROUP,) * 3 + (GROUP,) + (HEAD_DIM,) * 6 \
        + (HEADS * 3, N_MIXERS * GROUP)
    offs = np.concatenate([[0], np.cumsum(sizes)])
    col = lambda n: w_in[:, offs[n]:offs[n + 1]]
    (a_q, a_k, a_v, b_q, b_k, b_v, c_q, c_k, c_v, d_q, d_kc, d_vc, d_ks, d_vs, d_kw, d_vw, d_g, gate) = \
        [col(n) for n in range(18)]
    rep_kv = lambda w: jnp.repeat(w.reshape(d, 2, HEAD_DIM), 2, axis=1).reshape(d, GROUP)
    wrm = jnp.concatenate([a_q, a_k, a_v, b_q, rep_kv(b_k), rep_kv(b_v), c_k, d_kc, d_vc, d_ks, d_kw, gate], axis=1)
    wt = jnp.concatenate([c_q, c_v, d_q, d_vs, d_vw, d_g, jnp.zeros((d, GATE_ROWS - HEADS * 3), w_in.dtype)], axis=1).T
    g = qk_gain
    ones = lambda n: jnp.ones((n,), F32)
    tile4 = lambda v: jnp.tile(v, HEADS)
    scale = HEAD_DIM ** -0.5 * LOG2E
    grm = jnp.concatenate([tile4(g[0]) * scale, tile4(g[1]), ones(GROUP), tile4(g[2]) * scale, tile4(g[3]),
                           ones(GROUP), jnp.tile(qk_gain_diff[1], 2 * HEADS), ones(2 * HEAD_DIM), g[6], g[7],
                           ones(N_MIXERS * GROUP)])
    gt = jnp.concatenate([jnp.tile(qk_gain_diff[0], 2 * HEADS) * (DIFF_QK_DIM ** -0.5 * LOG2E), ones(GROUP),
                          tile4(g[4]) * scale, ones(2 * HEAD_DIM + GATE_ROWS)])
    return wrm.astype(BF16), wt.astype(BF16), grm.reshape(1, -1), gt.reshape(-1, 1)


def _compress_weights(cmp_pos, cmp_w1, cmp_b1, cmp_w2, cmp_b2):
    half = CMP_LEN // 2
    pos = jnp.concatenate([cmp_pos[0], cmp_pos[1]], axis=-1)
    ptop = pos[:half].reshape(1, -1)
    pbot = pos[half:].reshape(1, -1)
    w1 = cmp_w1.reshape(2, CMP_LEN, HEAD_DIM, CMP_HIDDEN)
    zeros = jnp.zeros_like(w1[0])
    w1cat = jnp.concatenate([jnp.concatenate([w1[0], zeros], axis=-1),
                             jnp.concatenate([zeros, w1[1]], axis=-1)], axis=1)
    w1t = w1cat[:half].reshape(half * 2 * HEAD_DIM, 2 * CMP_HIDDEN).astype(BF16)
    w1b = w1cat[half:].reshape(half * 2 * HEAD_DIM, 2 * CMP_HIDDEN).astype(BF16)
    b1 = jnp.concatenate([cmp_b1[0], cmp_b1[1]]).reshape(1, -1)
    return (ptop, pbot, w1t, w1b, b1, cmp_w2[0].astype(BF16), cmp_b2[0].reshape(1, -1),
            cmp_w2[1].T.astype(BF16), cmp_b2[1].reshape(-1, 1))


def kernel(x, rel_bias_table, norm_w, w_in, w_out, qk_gain, qk_gain_diff, attn_sinks, diff_lambda, diff_subln,
           cmp_pos, cmp_w1, cmp_b1, cmp_w2, cmp_b2):
    b, s, d = x.shape
    depth = w_in.shape[0]
    n_c = s // CMP_STRIDE
    n_blk = s // SLC_BLOCK
    assert s % (BAND_TILE * DILATED_CONFIGS[-1][1]) == 0 and s % PROJ_ROWS == 0 and d == N_MIXERS * GROUP

    table = rel_bias_table.astype(F32)
    band_bias = [_build_bias(table, head0=0, n_d=1, rows=BAND_TILE, cols=2 * BAND_TILE, base0=BAND_TILE, dstep=0,
                             rs=1, cs=-1, dscale=rate, max_dist=window // rate, scale=LOG2E)
                 for window, rate in DILATED_CONFIGS]
    swa_bias = _build_bias(table, head0=HEADS, n_d=1, rows=BAND_TILE, cols=2 * BAND_TILE, base0=BAND_TILE, dstep=0,
                           rs=1, cs=-1, max_dist=SWA_WINDOW - 1, scale=LOG2E)
    flash_tiles = dict(rows=KT, cols=QT, base0=0, dstep=QT, rs=-1, cs=1, scale=LOG2E)
    diff_bias = _build_bias(table, head0=2 * HEADS, n_d=N_BIAS_TILES, d_valid=N_NEAR + 1, **flash_tiles)
    slc_bias = _build_bias(table, head0=3 * HEADS, n_d=N_BIAS_TILES, d_valid=N_NEAR + 1, **flash_tiles)
    win_bias = _build_bias(table, head0=3 * HEADS, n_d=N_WIN, max_dist=NSA_WINDOW - 1, **flash_tiles)
    far_bias = table[NUM_BUCKETS - 1] * LOG2E
    cmp_bias = _build_bias(table, head0=3 * HEADS, n_d=1, rows=n_c, cols=s, base0=-(CMP_LEN - 1), dstep=0,
                           rs=-CMP_STRIDE, cs=1, r_valid=n_c - 1, col_tile=2 * QT, scale=LOG2E)
    e64, e32 = _block_diag_mean(HEAD_DIM), _block_diag_mean(DIFF_QK_DIM)
    rep_idx = np.arange(n_blk * SEL_REP) // SEL_REP
    rep = jnp.asarray((rep_idx[:, None] == np.arange(n_blk)[None, :]).astype(np.float32), BF16)
    no_sink = jnp.zeros((HEADS,), F32)

    x2 = x.reshape(b * s, d)
    w_in_bf16 = w_in.astype(BF16)
    for layer in range(depth):
        wrm, wt, grm, gt = _layer_weights(w_in_bf16[layer], qk_gain[layer], qk_gain_diff[layer])
        (a_q, a_k, a_v, b_q, b_k, b_v, c_k, kvc, ksw, gate, a_q4, a_k4, a_v4, a_q16, a_k16, a_v16,
         c_qt, c_vt, d_qt, d_vst, d_vwt, d_gt) = _project(x2, norm_w[layer].reshape(1, d), wrm, wt, grm, gt, e64, e32)
        seq = lambda t: t.reshape(b, s, t.shape[-1])
        per_batch = lambda t: t.reshape(b, t.shape[0] // b, t.shape[1])
        flat = lambda t: t.reshape(b * t.shape[1], t.shape[2])
        a_in = ((a_q, a_k, a_v), (a_q4, a_k4, a_v4), (a_q16, a_k16, a_v16))
        a_res = [_banded(*map(per_batch, a_in[n]), band_bias[n], no_sink, rate, False)
                 for n, (_, rate) in enumerate(DILATED_CONFIGS)]
        o_b, _ = _banded(seq(b_q), seq(b_k), seq(b_v), swa_bias, attn_sinks[layer].astype(F32) * LOG2E, 1, True)
        lambda_init = 0.8 - 0.6 * math.exp(-0.3 * layer)
        o_c = _diff_attention(far_bias[2 * HEADS:3 * HEADS], c_qt, seq(c_k), c_vt, diff_bias, diff_lambda[layer].astype(F32),
                              diff_subln[layer].reshape(HEAD_DIM, 1).astype(F32), lambda_init)
        cw = _compress_weights(cmp_pos[layer], cmp_w1[layer], cmp_b1[layer], cmp_w2[layer], cmp_b2[layer])
        kc, vct = _compress(kvc.reshape(b, n_c, CMP_STRIDE * 2 * HEAD_DIM), *cw, qk_gain[layer, 5].reshape(1, -1))
        o_cmp, sel = _cmp_attention(d_qt, kc, vct, cmp_bias, b)
        o_d = _slc_win_attention(far_bias[3 * HEADS:4 * HEADS], d_qt, seq(ksw), d_vst, d_vwt, sel, rep, o_cmp, d_gt,
                                 slc_bias, win_bias)
        x2 = _out_projection(x2, [flat(r[0]) for r in a_res], [flat(r[1]) for r in a_res], o_b.reshape(b * s, GROUP),
                             o_c.reshape(b * s, GROUP), o_d.reshape(b * s, GROUP), gate, w_out[layer].astype(BF16))
    return x2.reshape(b, s, d)
```

```python
import functools
import itertools
import math

import numpy as np
import jax
import jax.numpy as jnp
from jax import lax
from jax.experimental import pallas as pl
from jax.experimental.pallas import tpu as pltpu

F32 = jnp.float32
BF16 = jnp.bfloat16

HEAD_DIM = 64
HEADS = 4
GROUP = HEADS * HEAD_DIM
N_MIXERS = 4
NUM_BUCKETS = 32
REL_MAX_DIST = 2048
DILATED_CONFIGS = ((128, 1), (512, 4), (2048, 16))
FOLD_RATES = tuple(rate for _, rate in DILATED_CONFIGS if rate > 1)
SWA_WINDOW = 128
DIFF_QK_DIM = HEAD_DIM // 2
CMP_LEN = 32
CMP_STRIDE = 16
CMP_HIDDEN = 256
SLC_BLOCK = 64
SLC_TOPK = 16
CMP_PARTS = 4
NSA_WINDOW = 512
RMS_EPS = 1e-6
NEG_INF = -1e30
FORCE_SELECT = 1e9
TINY = 1e-30
LOG2E = math.log2(math.e)

PROJ_ROWS = 512
BAND_TILE = 128
BAND_STEP = 1024
BAND_LOOKAHEAD = 2
BIAS_ROW_CHUNK = 64
LANES = 128
QT = 256
KT = 256
VMEM_LIMIT = 56 * 1024 * 1024
MXU_LOOKAHEAD = 4
V_ROWS = HEAD_DIM + 16

NT_DIMS = (((1,), (1,)), ((), ()))


def _t5_thresholds():
    n = np.arange(0, 4 * REL_MAX_DIST)
    max_exact = NUM_BUCKETS // 2
    nf = np.maximum(n, 1).astype(np.float32)
    large = max_exact + (np.log(nf / np.float32(max_exact)) / np.float32(math.log(REL_MAX_DIST / max_exact))
                         * np.float32(NUM_BUCKETS - max_exact)).astype(np.int32)
    bucket = np.where(n < max_exact, n, np.minimum(large, NUM_BUCKETS - 1))
    return [int(np.argmax(bucket >= b)) for b in range(NUM_BUCKETS)]


T5_THRESHOLDS = _t5_thresholds()
FAR_DIST = T5_THRESHOLDS[-1]


def _bias_kernel(tbl_ref, out_ref, *, head0, base0, dstep, rs, cs, dscale, max_dist, r_valid, d_valid, col_tile,
                 scale):
    h = pl.program_id(0)
    n_d, rows, cols = out_ref.shape[1:]
    chunk = BIAS_ROW_CHUNK if rows % BIAS_ROW_CHUNK == 0 else rows
    bucket_of = lambda n: max(b for b in range(NUM_BUCKETS) if T5_THRESHOLDS[b] <= n)
    for d, c0, r0 in itertools.product(range(n_d), range(0, cols, col_tile), range(0, rows, chunk)):
        out = out_ref.at[0, d, r0:r0 + chunk, c0:c0 + col_tile]
        origin = base0 + d * dstep + r0 * rs + c0 * cs
        corners = [origin + dr * rs + dc * cs for dr in (0, chunk - 1) for dc in (0, col_tile - 1)]
        lo, hi = min(corners), max(corners)
        rows_valid = min(chunk, r_valid - r0)
        if hi < 0 or lo > max_dist or d >= d_valid or rows_valid <= 0:
            out[...] = jnp.full((chunk, col_tile), NEG_INF, F32)
            continue
        r = lax.broadcasted_iota(jnp.int32, (chunk, col_tile), 0)
        c = lax.broadcasted_iota(jnp.int32, (chunk, col_tile), 1)
        dist = origin + r * rs + c * cs
        first, last = bucket_of(max(lo, 0) * dscale), bucket_of(min(hi, max_dist) * dscale)
        val = jnp.full((chunk, col_tile), tbl_ref[first, head0 + h] * scale, F32)
        for b in range(first + 1, last + 1):
            val = jnp.where(dist * dscale >= T5_THRESHOLDS[b], tbl_ref[b, head0 + h] * scale, val)
        valid = [cond for needed, cond in ((lo < 0, dist >= 0), (hi > max_dist, dist <= max_dist),
                                           (rows_valid < chunk, r < rows_valid)) if needed]
        if valid:
            val = jnp.where(functools.reduce(jnp.logical_and, valid), val, NEG_INF)
        out[...] = val


def _build_bias(table, *, head0, n_d, rows, cols, base0, dstep, rs, cs, dscale=1, max_dist=1 << 30,
                r_valid=1 << 30, d_valid=1 << 30, col_tile=None, scale=1.0):
    col_tile = cols if col_tile is None else col_tile
    kern = functools.partial(_bias_kernel, head0=head0, base0=base0, dstep=dstep, rs=rs, cs=cs, dscale=dscale,
                             max_dist=max_dist, r_valid=r_valid, d_valid=d_valid, col_tile=col_tile, scale=scale)
    return pl.pallas_call(
        kern,
        grid=(HEADS,),
        in_specs=[pl.BlockSpec(memory_space=pltpu.SMEM)],
        out_specs=pl.BlockSpec((1, n_d, rows, cols), lambda h: (h, 0, 0, 0)),
        out_shape=jax.ShapeDtypeStruct((HEADS, n_d, rows, cols), F32),
        compiler_params=pltpu.CompilerParams(dimension_semantics=("arbitrary",), vmem_limit_bytes=VMEM_LIMIT),
        name="rel_bias_tiles",
    )(table)


RM_AQ, RM_AK, RM_AV = 0, 256, 512
RM_BQ, RM_BK, RM_BV = 768, 1024, 1280
RM_CK = 1536
RM_KVC = 1792
RM_KSW = 1920
RM_GATE = 2048
RM_COLS = 3072
TR_CQ, TR_CV, TR_DQ, TR_DVS, TR_DVW, TR_DG = 0, 256, 512, 768, 832, 896
GATE_ROWS = 16
TR_ROWS = TR_DG + GATE_ROWS


def _proj_kernel(x_ref, nw_ref, wrm_ref, wt_ref, grm_ref, gt_ref, e64_ref, e32_ref,
                 aq_ref, ak_ref, av_ref, bq_ref, bk_ref, bv_ref, ck_ref, kvc_ref, ksw_ref, gate_ref,
                 aq4_ref, ak4_ref, av4_ref, aq16_ref, ak16_ref, av16_ref,
                 cq_ref, cv_ref, dq_ref, dvs_ref, dvw_ref, dg_ref, fold_ref):
    x = x_ref[...]
    ms = jnp.mean(x * x, axis=-1, keepdims=True)
    xn = (x * lax.rsqrt(ms + RMS_EPS) * nw_ref[...]).astype(BF16)
    rows = x.shape[0]

    def rm(c0, width):
        return jnp.dot(xn, wrm_ref[:, c0:c0 + width], preferred_element_type=F32)

    def rm_normed(c0, width, e_ref):
        h = rm(c0, width)
        msq = jnp.dot((h * h).astype(BF16), e_ref[0:width, 0:width], preferred_element_type=F32)
        return h * lax.rsqrt(msq + RMS_EPS) * grm_ref[:, c0:c0 + width]

    def put_folded(val, ref, folded_refs):
        ref[...] = val.astype(ref.dtype)
        for half in range(GROUP // LANES):
            fold_ref[half] = val[:, half * LANES:(half + 1) * LANES]
        for rate, fref in zip(FOLD_RATES, folded_refs):
            for rho in range(rate):
                for half in range(GROUP // LANES):
                    c0 = rho * GROUP + half * LANES
                    fref[:, c0:c0 + LANES] = fold_ref[half, pl.ds(rho, rows // rate, stride=rate), :].astype(fref.dtype)

    put_folded(rm_normed(RM_AQ, GROUP, e64_ref), aq_ref, (aq4_ref, aq16_ref))
    put_folded(rm_normed(RM_AK, GROUP, e64_ref), ak_ref, (ak4_ref, ak16_ref))
    put_folded(rm(RM_AV, GROUP), av_ref, (av4_ref, av16_ref))
    bq_ref[...] = rm_normed(RM_BQ, GROUP, e64_ref).astype(bq_ref.dtype)
    bk_ref[...] = rm_normed(RM_BK, GROUP, e64_ref).astype(bk_ref.dtype)
    bv_ref[...] = rm(RM_BV, GROUP).astype(bv_ref.dtype)
    ck_ref[...] = rm_normed(RM_CK, GROUP, e32_ref).astype(ck_ref.dtype)
    kvc_ref[...] = rm(RM_KVC, 2 * HEAD_DIM).astype(kvc_ref.dtype)
    ksw_ref[...] = rm_normed(RM_KSW, 2 * HEAD_DIM, e64_ref).astype(ksw_ref.dtype)
    gate_ref[...] = rm(RM_GATE, N_MIXERS * GROUP).astype(gate_ref.dtype)

    key_major = lax.dot_general(wt_ref[...], xn, NT_DIMS, preferred_element_type=F32)

    def tr(r0, height):
        return key_major[r0:r0 + height]

    def tr_normed(r0, height, group):
        h3 = tr(r0, height).reshape(height // group, group, rows)
        msq = jnp.mean(h3 * h3, axis=1, keepdims=True)
        return (h3 * lax.rsqrt(msq + RMS_EPS)).reshape(height, rows) * gt_ref[r0:r0 + height, :]

    def put(ref, val):
        for t in range(rows // QT):
            ref[t] = val[:, t * QT:(t + 1) * QT].astype(ref.dtype)

    def with_ones(v):
        ones = jnp.ones((V_ROWS - HEAD_DIM, rows), F32)
        parts = []
        for h in range(v.shape[0] // HEAD_DIM):
            parts += [v[h * HEAD_DIM:(h + 1) * HEAD_DIM], ones]
        return jnp.concatenate(parts, axis=0)

    put(cq_ref, tr_normed(TR_CQ, GROUP, DIFF_QK_DIM))
    put(cv_ref, with_ones(tr(TR_CV, GROUP)))
    put(dq_ref, tr_normed(TR_DQ, GROUP, HEAD_DIM))
    put(dvs_ref, with_ones(tr(TR_DVS, HEAD_DIM)))
    put(dvw_ref, with_ones(tr(TR_DVW, HEAD_DIM)))
    put(dg_ref, tr(TR_DG, GATE_ROWS))


def _project(x2, nw, wrm, wt, grm, gt, e64, e32):
    m, d = x2.shape
    nt = m // QT
    tpr = PROJ_ROWS // QT
    const = lambda shape: pl.BlockSpec(shape, lambda i: (0,) * len(shape))
    rm_out = lambda width, dtype: (jax.ShapeDtypeStruct((m, width), dtype),
                                   pl.BlockSpec((PROJ_ROWS, width), lambda i: (i, 0)))
    tr_out = lambda height, dtype: (jax.ShapeDtypeStruct((nt, height, QT), dtype),
                                    pl.BlockSpec((tpr, height, QT), lambda i: (i, 0, 0)))
    outs = [rm_out(GROUP, BF16)] * 7 + [rm_out(2 * HEAD_DIM, F32), rm_out(2 * HEAD_DIM, BF16),
                                        rm_out(N_MIXERS * GROUP, BF16)]
    fold_out = lambda rate: (jax.ShapeDtypeStruct((m // rate, rate * GROUP), BF16),
                             pl.BlockSpec((PROJ_ROWS // rate, rate * GROUP), lambda i: (i, 0)))
    outs += [fold_out(rate) for rate in FOLD_RATES for _ in range(3)]
    outs += [tr_out(GROUP, BF16), tr_out(HEADS * V_ROWS, BF16), tr_out(GROUP, BF16), tr_out(V_ROWS, BF16),
             tr_out(V_ROWS, BF16), tr_out(GATE_ROWS, F32)]
    return pl.pallas_call(
        _proj_kernel,
        grid=(m // PROJ_ROWS,),
        in_specs=[pl.BlockSpec((PROJ_ROWS, d), lambda i: (i, 0)), const((1, d)), const((d, RM_COLS)),
                  const((TR_ROWS, d)), const((1, RM_COLS)), const((TR_ROWS, 1)), const((GROUP, GROUP)),
                  const((GROUP, GROUP))],
        out_specs=[o[1] for o in outs],
        out_shape=[o[0] for o in outs],
        scratch_shapes=[pltpu.VMEM((GROUP // LANES, PROJ_ROWS, LANES), F32)],
        compiler_params=pltpu.CompilerParams(dimension_semantics=("arbitrary",), vmem_limit_bytes=VMEM_LIMIT),
        name="in_projection",
    )(x2, nw, wrm, wt, grm, gt, e64, e32)


def _band_kernel(sink_ref, q_ref, kp_ref, kc_ref, vp_ref, vc_ref, bias_ref, o_ref, lse_ref, p_ref, *, use_sink):
    i = pl.program_id(2)
    n_blocks = q_ref.shape[1] // BAND_TILE
    head_q = lax.broadcasted_iota(jnp.int32, (BAND_TILE, GROUP), 1) // HEAD_DIM
    head_v = lax.broadcasted_iota(jnp.int32, (2 * BAND_TILE, GROUP), 1) // HEAD_DIM
    lane = lax.broadcasted_iota(jnp.int32, (BAND_TILE, LANES), 1)
    in_prev = lax.broadcasted_iota(jnp.int32, (1, 2 * BAND_TILE), 1) < BAND_TILE
    no_prev = jnp.where(in_prev & (i == 0), NEG_INF, 0.0).astype(F32)

    def window(cur_ref, prev_ref, m):
        if m == 0:
            return jnp.concatenate([prev_ref[0], cur_ref[0, 0:BAND_TILE, :]], axis=0)
        return cur_ref[0, (m - 1) * BAND_TILE:(m + 1) * BAND_TILE, :]

    def scores(m):
        q = q_ref[0, m * BAND_TILE:(m + 1) * BAND_TILE, :]
        q_heads = jnp.concatenate([jnp.where(head_q == h, q, jnp.zeros_like(q)) for h in range(HEADS)], axis=0)
        bias = bias_ref[:, 0].reshape(HEADS * BAND_TILE, 2 * BAND_TILE)
        if m == 0:
            bias = bias + no_prev
        return lax.dot_general(q_heads, window(kc_ref, kp_ref, m), NT_DIMS, preferred_element_type=F32) + bias

    def update(m, s_heads):
        lse_tile = jnp.zeros((BAND_TILE, LANES), F32)
        for h in range(HEADS):
            s = s_heads[h * BAND_TILE:(h + 1) * BAND_TILE]
            mx = jnp.max(s, axis=1, keepdims=True)
            if use_sink:
                mx = jnp.maximum(mx, sink_ref[h])
            p = jnp.exp2(s - mx)
            den = jnp.sum(p, axis=1, keepdims=True)
            if use_sink:
                den = den + jnp.exp2(sink_ref[h] - mx)
            p_ref[m % 2, :, h * 2 * BAND_TILE:(h + 1) * 2 * BAND_TILE] = (p * (1.0 / den)).astype(BF16)
            lse_tile = jnp.where(lane == h, mx + jnp.log2(den), lse_tile)
        v = window(vc_ref, vp_ref, m)
        v_heads = jnp.concatenate([jnp.where(head_v == h, v, jnp.zeros_like(v)) for h in range(HEADS)], axis=0)
        rows = slice(m * BAND_TILE, (m + 1) * BAND_TILE)
        o_ref[0, rows, :] = jnp.dot(p_ref[m % 2], v_heads, preferred_element_type=F32).astype(o_ref.dtype)
        lse_ref[0, rows, :] = lse_tile

    _staggered(n_blocks, scores, update, ahead=BAND_LOOKAHEAD)


def _banded(q, k, v, bias, sink, rate, use_sink):
    b, ln, _ = q.shape
    step = min(BAND_STEP, ln)
    per_step = step // BAND_TILE
    cur = pl.BlockSpec((1, step, GROUP), lambda bb, r, i: (bb, i, r))
    prev = pl.BlockSpec((1, BAND_TILE, GROUP), lambda bb, r, i: (bb, jnp.maximum(i * per_step - 1, 0), r))
    o, lse = pl.pallas_call(
        functools.partial(_band_kernel, use_sink=use_sink),
        grid=(b, rate, ln // step),
        in_specs=[pl.BlockSpec(memory_space=pltpu.SMEM), cur, prev, cur, prev, cur,
                  pl.BlockSpec((HEADS, 1, BAND_TILE, 2 * BAND_TILE), lambda bb, r, i: (0, 0, 0, 0))],
        out_specs=[cur, pl.BlockSpec((1, step, LANES), lambda bb, r, i: (bb, i, r))],
        out_shape=[jax.ShapeDtypeStruct((b, ln, rate * GROUP), BF16), jax.ShapeDtypeStruct((b, ln, rate * LANES), F32)],
        scratch_shapes=[pltpu.VMEM((2, BAND_TILE, HEADS * 2 * BAND_TILE), BF16)],
        compiler_params=pltpu.CompilerParams(dimension_semantics=("arbitrary",) * 3),
        name=f"banded_attention_r{rate}",
    )(sink, q, k, k, v, v, bias)
    return o, lse


def _flash_reset(m_ref, acc_ref):
    m_ref[...] = jnp.full(m_ref.shape, NEG_INF, F32)
    acc_ref[...] = jnp.zeros(acc_ref.shape, F32)


def _flash_update(n, s, v_t, m_ref, acc_ref, shift=None):
    m_old = m_ref[n]
    if shift is None:
        m_new = jnp.maximum(m_old, jnp.max(s, axis=0, keepdims=True))
        p = jnp.exp2(s - m_new)
    else:
        m_new = jnp.maximum(m_old, jnp.max(s, axis=0, keepdims=True) + shift)
        p = jnp.exp2(s - (m_new - shift))
    alpha = jnp.exp2(m_old - m_new)
    acc_ref[n] = alpha * acc_ref[n] + jnp.dot(v_t, p.astype(BF16), preferred_element_type=F32)
    m_ref[n] = m_new


def _flash_result(n, acc_ref):
    return acc_ref[n, 0:HEAD_DIM, :] / acc_ref[n, HEAD_DIM:HEAD_DIM + 1, :]


def _staggered(n_items, scores, update, ahead=MXU_LOOKAHEAD):
    pending = {n: scores(n) for n in range(min(ahead, n_items))}
    for n in range(n_items):
        if n + ahead < n_items:
            pending[n + ahead] = scores(n + ahead)
        update(n, pending.pop(n))


def _pipelined_tiles(first, n_tiles, n_chains, group, load_tile, scores, update, next_ref, left_by_previous=None,
                     last_of_sweep=False):
    ahead = next_ref.shape[0]
    n_items = group * n_chains
    assert ahead <= n_chains

    def body(trip, _, issue_next=True):
        base = first + trip * group
        tiles, pending = {}, {}
        for n in range(n_items):
            cur = next_ref[n] if n < ahead else pending.pop(n)
            if n + ahead < n_items or issue_next:
                g, c = divmod(n + ahead, n_chains)
                if g not in tiles:
                    tiles[g] = load_tile(base + g)
                new = scores(tiles[g], base + g, c)
                if n + ahead < n_items:
                    pending[n + ahead] = new
                else:
                    next_ref[n + ahead - n_items] = new
            update(base + n // n_chains, n % n_chains, cur)

    if left_by_previous is None:
        first_tile = load_tile(first)
        for n in range(ahead):
            next_ref[n] = scores(first_tile, first, n)
    else:
        for n in range(ahead):
            next_ref[n] = next_ref[n] + left_by_previous(first, n)
    n_trips = (n_tiles + group - 1) // group
    if last_of_sweep:
        lax.fori_loop(0, n_trips - 1, body, None)
        body(n_trips - 1, None, issue_next=False)
    else:
        lax.fori_loop(0, n_trips, body, None)


def _flash_scratch(chains, ahead):
    return [pltpu.VMEM((chains, 1, QT), F32), pltpu.VMEM((chains, V_ROWS, QT), F32),
            pltpu.VMEM((ahead, KT, QT), F32)]


N_NEAR = -(-(FAR_DIST + KT - 1) // QT)
N_BIAS_TILES = N_NEAR + 2
DIFF_TILE_GROUP = 2
SLC_TILE_GROUP = 2
FAR_TILE_GROUP = 4


def _bias_tile_index(i, j):
    return jnp.where(j > i, N_NEAR + 1, jnp.minimum(i - j, N_NEAR))


def _whole_far_groups(i, group):
    return jnp.maximum(i - (N_NEAR - 1), 0) // group * group


def _diff_kernel(far_ref, q_ref, k_ref, v_ref, bias_ref, lam_ref, subln_ref, o_ref, qz_ref, m_ref, acc_ref, next_ref,
                 ot_ref, *, lambda_init):
    i = pl.program_id(1)
    q = q_ref[0]
    row = lax.broadcasted_iota(jnp.int32, (GROUP, QT), 0) // DIFF_QK_DIM
    for n in range(2 * HEADS):
        qz_ref[n] = jnp.where(row == n, q, jnp.zeros_like(q))
    _flash_reset(m_ref, acc_ref)

    def load_tile(j):
        return k_ref[0, pl.ds(pl.multiple_of(jnp.minimum(j, i) * KT, KT), KT), :]

    def values(j, n):
        h = n // 2
        return v_ref[0, jnp.minimum(j, i), h * V_ROWS:(h + 1) * V_ROWS, :]

    n_far = _whole_far_groups(i, FAR_TILE_GROUP)
    _pipelined_tiles(0, n_far, 2 * HEADS, FAR_TILE_GROUP, load_tile,
                     lambda k, j, n: jnp.dot(k, qz_ref[n], preferred_element_type=F32),
                     lambda j, n, s: _flash_update(n, s, values(j, n), m_ref, acc_ref, shift=far_ref[n // 2]),
                     next_ref)

    def scores(k, j, n):
        return jnp.dot(k, qz_ref[n], preferred_element_type=F32) + bias_ref[n // 2, _bias_tile_index(i, j)]

    _pipelined_tiles(n_far, i + 1 - n_far, 2 * HEADS, DIFF_TILE_GROUP, load_tile, scores,
                     lambda j, n, s: _flash_update(n, s, values(j, n), m_ref, acc_ref), next_ref,
                     left_by_previous=lambda j, n: bias_ref[n // 2, _bias_tile_index(i, j)], last_of_sweep=True)

    lam_p = lam_ref[...]
    lam = (jnp.exp(jnp.sum(lam_p[0:1] * lam_p[1:2], axis=1, keepdims=True))
           - jnp.exp(jnp.sum(lam_p[2:3] * lam_p[3:4], axis=1, keepdims=True)) + lambda_init)
    for h in range(HEADS):
        o = _flash_result(2 * h, acc_ref) - lam * _flash_result(2 * h + 1, acc_ref)
        msq = jnp.mean(o * o, axis=0, keepdims=True)
        ot_ref[h * HEAD_DIM:(h + 1) * HEAD_DIM, :] = (o * lax.rsqrt(msq + RMS_EPS) * subln_ref[...]
                                                      * (1.0 - lambda_init))
    o_ref[0] = ot_ref[...].T.astype(o_ref.dtype)


def _diff_attention(far, q_t, k, v_t, bias, lam_p, subln, lambda_init):
    b, s, _ = k.shape
    nq = s // QT
    nkv = s // KT
    v4 = v_t.reshape(b, nkv, HEADS * V_ROWS, KT)
    return pl.pallas_call(
        functools.partial(_diff_kernel, lambda_init=lambda_init),
        grid=(b, nq),
        in_specs=[pl.BlockSpec(memory_space=pltpu.SMEM),
                  pl.BlockSpec((1, GROUP, QT), lambda bb, i: (bb * nq + i, 0, 0)),
                  pl.BlockSpec((1, s, GROUP), lambda bb, i: (bb, 0, 0)),
                  pl.BlockSpec((1, nkv, HEADS * V_ROWS, KT), lambda bb, i: (bb, 0, 0, 0)),
                  pl.BlockSpec((HEADS, N_BIAS_TILES, KT, QT), lambda bb, i: (0, 0, 0, 0)),
                  pl.BlockSpec((4, DIFF_QK_DIM), lambda bb, i: (0, 0)),
                  pl.BlockSpec((HEAD_DIM, 1), lambda bb, i: (0, 0))],
        out_specs=pl.BlockSpec((1, QT, GROUP), lambda bb, i: (bb, i, 0)),
        out_shape=jax.ShapeDtypeStruct((b, s, GROUP), BF16),
        scratch_shapes=[pltpu.VMEM((2 * HEADS, GROUP, QT), BF16)] + _flash_scratch(2 * HEADS, MXU_LOOKAHEAD)
        + [pltpu.VMEM((GROUP, QT), F32)],
        compiler_params=pltpu.CompilerParams(dimension_semantics=("arbitrary", "arbitrary"),
                                             vmem_limit_bytes=VMEM_LIMIT),
        name="diff_attention",
    )(far, q_t, k, v4, bias, lam_p, subln)


def _compress_kernel(ch_ref, ptop_ref, pbot_ref, w1t_ref, w1b_ref, b1_ref, w2k_ref, b2k_ref, w2v_ref, b2v_ref,
                     gk_ref, kc_ref, vct_ref):
    ch = ch_ref[0]
    n_c = ch.shape[0]
    u = jnp.dot((ch + ptop_ref[...]).astype(BF16), w1t_ref[...], preferred_element_type=F32)
    v = jnp.dot((ch + pbot_ref[...]).astype(BF16), w1b_ref[...], preferred_element_type=F32)
    v_next = pltpu.roll(v, n_c - 1, 0)
    hid = jax.nn.gelu(u + v_next + b1_ref[...])
    hk = hid[:, :CMP_HIDDEN].astype(BF16)
    hv = hid[:, CMP_HIDDEN:].astype(BF16)
    kc = jnp.dot(hk, w2k_ref[...], preferred_element_type=F32) + b2k_ref[...]
    msq = jnp.mean(kc * kc, axis=-1, keepdims=True)
    kc_ref[0] = (kc * lax.rsqrt(msq + RMS_EPS) * gk_ref[...]).astype(kc_ref.dtype)
    vct = lax.dot_general(w2v_ref[...], hv, NT_DIMS, preferred_element_type=F32) + b2v_ref[...]
    vct_ref[0] = vct.astype(vct_ref.dtype)


def _compress(chunks, ptop, pbot, w1t, w1b, b1, w2k, b2k, w2v, b2v, gk):
    b, n_c, width = chunks.shape
    const = lambda a: pl.BlockSpec(a.shape, lambda bb: (0,) * a.ndim)
    params = (ptop, pbot, w1t, w1b, b1, w2k, b2k, w2v, b2v, gk)
    return pl.pallas_call(
        _compress_kernel,
        grid=(b,),
        in_specs=[pl.BlockSpec((1, n_c, width), lambda bb: (bb, 0, 0))] + [const(a) for a in params],
        out_specs=[pl.BlockSpec((1, n_c, HEAD_DIM), lambda bb: (bb, 0, 0)),
                   pl.BlockSpec((1, HEAD_DIM, n_c), lambda bb: (bb, 0, 0))],
        out_shape=[jax.ShapeDtypeStruct((b, n_c, HEAD_DIM), BF16), jax.ShapeDtypeStruct((b, HEAD_DIM, n_c), BF16)],
        compiler_params=pltpu.CompilerParams(dimension_semantics=("arbitrary",), vmem_limit_bytes=VMEM_LIMIT),
        name="nsa_compress",
    )(chunks, *params)


def _cmp_attn_kernel(q_ref, kc_ref, vct_ref, bias_ref, o_ref, sel_ref, p_ref, *, n_sel):
    i = pl.program_id(0)
    n_tiles = pl.num_programs(0)
    for part in range(1, CMP_PARTS + 1):
        @pl.when((i * CMP_PARTS >= (part - 1) * n_tiles) & (i * CMP_PARTS < part * n_tiles))
        def _(part=part):
            _cmp_attn_body(i, kc_ref.shape[1] * part // CMP_PARTS, q_ref, kc_ref, vct_ref, bias_ref, o_ref, sel_ref,
                           p_ref, n_sel)


def _cmp_attn_body(i, n_c, q_ref, kc_ref, vct_ref, bias_ref, o_ref, sel_ref, p_ref, n_sel):
    kc = kc_ref[0, 0:n_c, :]
    vct = vct_ref[0, :, 0:n_c]
    n_blk = n_c * CMP_STRIDE // SLC_BLOCK
    probs = []

    def scores(h):
        return (jnp.dot(kc, q_ref[0, h * HEAD_DIM:(h + 1) * HEAD_DIM, :], preferred_element_type=F32)
                + bias_ref[h, 0, 0:n_c, :])

    def update(h, s):
        m = jnp.maximum(jnp.max(s, axis=0, keepdims=True), 0.5 * NEG_INF)
        p = jnp.exp2(s - m)
        den = jnp.sum(p, axis=0, keepdims=True)
        p = p * (1.0 / jnp.maximum(den, TINY))
        o_ref[0, h * HEAD_DIM:(h + 1) * HEAD_DIM, :] = jnp.dot(vct, p.astype(BF16),
                                                               preferred_element_type=F32).astype(o_ref.dtype)
        probs.append(p)

    _staggered(HEADS, scores, update)
    psum = (probs[0] + probs[1]) + (probs[2] + probs[3])
    per_blk = SLC_BLOCK // CMP_STRIDE
    halves = []
    for half in range(QT // LANES):
        p_ref[half, 0:8, :] = jnp.zeros((8, LANES), F32)
        p_ref[half, 8:8 + n_c, :] = psum[:, half * LANES:(half + 1) * LANES]
        p_ref[half, 8 + n_c:16 + n_c, :] = jnp.zeros((8, LANES), F32)
        acc = p_ref[half, pl.ds(7, n_blk, stride=per_blk), :]
        for t in range(per_blk):
            acc = acc + p_ref[half, pl.ds(8 + t, n_blk, stride=per_blk), :]
        halves.append(acc)
    imp = jnp.concatenate(halves, axis=1)
    blk = lax.broadcasted_iota(jnp.int32, (n_blk, QT), 0)
    cur = (i * QT + lax.broadcasted_iota(jnp.int32, (n_blk, QT), 1)) // SLC_BLOCK
    forced = (blk == 0) | (blk == cur) | (blk == cur - 1)
    val = jnp.where(forced, FORCE_SELECT, jnp.where(blk <= cur, imp, NEG_INF))
    sel = jnp.zeros((n_blk, QT), jnp.bool_)
    for _ in range(n_sel):
        top = jnp.max(val, axis=0, keepdims=True)
        idx = jnp.min(jnp.where(val == top, blk, n_blk), axis=0, keepdims=True)
        hit = blk == idx
        sel = sel | hit
        val = jnp.where(hit, -3.0e38, val)
    sel_ref[0, 0:n_blk, :] = jnp.where(sel, 1.0, 0.0).astype(sel_ref.dtype)
    if n_blk < sel_ref.shape[1]:
        sel_ref[0, n_blk:, :] = jnp.zeros((sel_ref.shape[1] - n_blk, QT), sel_ref.dtype)


def _cmp_attention(q_t, kc, vct, bias, b):
    nt = q_t.shape[0]
    nq = nt // b
    n_c = kc.shape[1]
    n_blk = nq * QT // SLC_BLOCK
    return pl.pallas_call(
        functools.partial(_cmp_attn_kernel, n_sel=min(SLC_TOPK, n_blk)),
        grid=(nq, b),
        in_specs=[pl.BlockSpec((1, GROUP, QT), lambda i, bb: (bb * nq + i, 0, 0)),
                  pl.BlockSpec((1, n_c, HEAD_DIM), lambda i, bb: (bb, 0, 0)),
                  pl.BlockSpec((1, HEAD_DIM, n_c), lambda i, bb: (bb, 0, 0)),
                  pl.BlockSpec((HEADS, 1, n_c, QT), lambda i, bb: (0, 0, 0, i))],
        out_specs=[pl.BlockSpec((1, GROUP, QT), lambda i, bb: (bb * nq + i, 0, 0)),
                   pl.BlockSpec((1, n_blk, QT), lambda i, bb: (bb * nq + i, 0, 0))],
        out_shape=[jax.ShapeDtypeStruct((nt, GROUP, QT), BF16), jax.ShapeDtypeStruct((nt, n_blk, QT), BF16)],
        scratch_shapes=[pltpu.VMEM((QT // LANES, n_c + 16, LANES), F32)],
        compiler_params=pltpu.CompilerParams(dimension_semantics=("arbitrary", "arbitrary"),
                                             vmem_limit_bytes=VMEM_LIMIT),
        name="nsa_compressed_attention",
    )(q_t, kc, vct, bias)


SEL_REP = 8
N_WIN = -(-(NSA_WINDOW - 1 + KT - 1) // QT)


def _slc_win_kernel(far_ref, q_ref, ksw_ref, vs_ref, vw_ref, sel_ref, rep_ref, ocmp_ref, g_ref, bslc_ref, bwin_ref,
                    o_ref, qz_ref, m_ref, acc_ref, next_ref, ot_ref, mask_ref):
    i = pl.program_id(1)
    sel8 = jnp.dot(rep_ref[...], sel_ref[0], preferred_element_type=F32)
    mask_ref[...] = (sel8 - 1.0) * (-NEG_INF)
    blocks_per_tile = KT // SLC_BLOCK
    mrows = blocks_per_tile * SEL_REP
    zeros = jnp.zeros((HEAD_DIM, QT), BF16)
    for h in range(HEADS):
        qh = q_ref[0, h * HEAD_DIM:(h + 1) * HEAD_DIM, :]
        qz_ref[h] = jnp.concatenate([qh, zeros], axis=0)
        qz_ref[HEADS + h] = jnp.concatenate([zeros, qh], axis=0)
    _flash_reset(m_ref, acc_ref)

    def load_keys(j):
        return ksw_ref[0, pl.ds(pl.multiple_of(j * KT, KT), KT), :]

    def load_tile(j):
        j = jnp.minimum(j, i)
        m8 = mask_ref[pl.ds(pl.multiple_of(j * mrows, mrows), mrows), :]
        mask = jnp.broadcast_to(m8.reshape(blocks_per_tile, 1, SEL_REP, QT),
                                (blocks_per_tile, SLC_BLOCK // SEL_REP, SEL_REP, QT)).reshape(KT, QT)
        return load_keys(j), mask

    def values(j):
        return vs_ref[0, jnp.minimum(j, i)]

    n_far = _whole_far_groups(i, FAR_TILE_GROUP)
    _pipelined_tiles(0, n_far, HEADS, FAR_TILE_GROUP, load_tile,
                     lambda tile, j, h: jnp.dot(tile[0], qz_ref[h], preferred_element_type=F32) + tile[1],
                     lambda j, h, s: _flash_update(h, s, values(j), m_ref, acc_ref, shift=far_ref[h]), next_ref)

    def slc_scores(tile, j, h):
        k, mask = tile
        return jnp.dot(k, qz_ref[h], preferred_element_type=F32) + mask + bslc_ref[h, _bias_tile_index(i, j)]

    _pipelined_tiles(n_far, i + 1 - n_far, HEADS, SLC_TILE_GROUP, load_tile, slc_scores,
                     lambda j, h, s: _flash_update(h, s, values(j), m_ref, acc_ref), next_ref,
                     left_by_previous=lambda j, h: bslc_ref[h, _bias_tile_index(i, j)], last_of_sweep=True)

    def win_tile(n):
        d = N_WIN - 1 - n // HEADS
        return d, n % HEADS, jnp.maximum(i - d, 0)

    def win_scores(n):
        d, h, j = win_tile(n)
        missing = jnp.where(i < d, NEG_INF, 0.0).astype(F32)
        return jnp.dot(load_keys(j), qz_ref[HEADS + h], preferred_element_type=F32) + (bwin_ref[h, d] + missing)

    def win_update(n, s):
        _, h, j = win_tile(n)
        _flash_update(HEADS + h, s, vw_ref[0, j], m_ref, acc_ref)

    _staggered(N_WIN * HEADS, win_scores, win_update)

    for h in range(HEADS):
        g = jax.nn.sigmoid(g_ref[0, 3 * h:3 * h + 3, :])
        ot_ref[h * HEAD_DIM:(h + 1) * HEAD_DIM, :] = (g[0:1] * ocmp_ref[0, h * HEAD_DIM:(h + 1) * HEAD_DIM, :]
                                                      + g[1:2] * _flash_result(h, acc_ref)
                                                      + g[2:3] * _flash_result(HEADS + h, acc_ref))
    o_ref[0] = ot_ref[...].T.astype(o_ref.dtype)


def _slc_win_attention(far, q_t, ksw, vs_t, vw_t, sel, rep, ocmp, g_t, bslc, bwin):
    b, s, _ = ksw.shape
    nq = s // QT
    nkv = s // KT
    n_blk = s // SLC_BLOCK
    tile = lambda height: pl.BlockSpec((1, height, QT), lambda bb, i: (bb * nq + i, 0, 0))
    whole = lambda a: pl.BlockSpec(a.shape, lambda bb, i: (0,) * a.ndim)
    return pl.pallas_call(
        _slc_win_kernel,
        grid=(b, nq),
        in_specs=[pl.BlockSpec(memory_space=pltpu.SMEM), tile(GROUP),
                  pl.BlockSpec((1, s, 2 * HEAD_DIM), lambda bb, i: (bb, 0, 0)),
                  pl.BlockSpec((1, nkv, V_ROWS, KT), lambda bb, i: (bb, 0, 0, 0)),
                  pl.BlockSpec((1, nkv, V_ROWS, KT), lambda bb, i: (bb, 0, 0, 0)),
                  tile(n_blk), whole(rep), tile(GROUP), tile(GATE_ROWS), whole(bslc), whole(bwin)],
        out_specs=pl.BlockSpec((1, QT, GROUP), lambda bb, i: (bb, i, 0)),
        out_shape=jax.ShapeDtypeStruct((b, s, GROUP), BF16),
        scratch_shapes=[pltpu.VMEM((2 * HEADS, 2 * HEAD_DIM, QT), BF16)] + _flash_scratch(2 * HEADS, MXU_LOOKAHEAD)
        + [pltpu.VMEM((GROUP, QT), F32), pltpu.VMEM((n_blk * SEL_REP, QT), F32)],
        compiler_params=pltpu.CompilerParams(dimension_semantics=("arbitrary", "arbitrary"),
                                             vmem_limit_bytes=VMEM_LIMIT),
        name="nsa_selected_window_attention",
    )(far, q_t, ksw, vs_t.reshape(b, nkv, V_ROWS, KT), vw_t.reshape(b, nkv, V_ROWS, KT), sel, rep, ocmp, g_t,
      bslc, bwin)


def _out_kernel(x_ref, a0_ref, a1_ref, a2_ref, l0_ref, l1_ref, l2_ref, ob_ref, oc_ref, od_ref, gate_ref, e_ref,
                w_ref, o_ref, unfold_ref):
    rows = x_ref.shape[0]

    def unfolded(ref, rate):
        width = ref.shape[1] // rate
        for rho in range(rate):
            for part in range(width // LANES):
                c0 = rho * width + part * LANES
                unfold_ref[part, pl.ds(rho, rows // rate, stride=rate), :] = ref[:, c0:c0 + LANES].astype(F32)
        return jnp.concatenate([unfold_ref[part] for part in range(width // LANES)], axis=1)

    a0, l0 = a0_ref[...], l0_ref[...]
    a1, l1 = unfolded(a1_ref, FOLD_RATES[0]), unfolded(l1_ref, FOLD_RATES[0])
    a2, l2 = unfolded(a2_ref, FOLD_RATES[1]), unfolded(l2_ref, FOLD_RATES[1])
    mx = jnp.maximum(jnp.maximum(l0, l1), l2)
    e0, e1, e2 = jnp.exp2(l0 - mx), jnp.exp2(l1 - mx), jnp.exp2(l2 - mx)
    den = e0 + e1 + e2

    w = jnp.concatenate([e0 / den, e1 / den, e2 / den], axis=0)
    hi = w.astype(BF16)
    lo = (w - hi.astype(F32)).astype(BF16)
    w = jnp.dot(jnp.concatenate([hi, lo], axis=1), e_ref[...], preferred_element_type=F32)
    o_a = w[0:rows] * a0 + w[rows:2 * rows] * a1 + w[2 * rows:] * a2
    y = jnp.concatenate([o_a, ob_ref[...].astype(F32), oc_ref[...].astype(F32), od_ref[...].astype(F32)], axis=1)
    g = gate_ref[...].astype(F32)
    y = y * (g * jax.nn.sigmoid(g))
    o_ref[...] = x_ref[...] + jnp.dot(y.astype(BF16), w_ref[...], preferred_element_type=F32)


def _out_projection(x2, a_outs, a_lses, o_b, o_c, o_d, gate, w_out):
    m, d = x2.shape
    rowblk = lambda width: pl.BlockSpec((PROJ_ROWS, width), lambda i: (i, 0))
    folded = lambda width, rate: pl.BlockSpec((PROJ_ROWS // rate, rate * width), lambda i: (i, 0))
    head_of_lane = np.arange(GROUP) // HEAD_DIM
    expand = jnp.asarray((np.arange(2 * LANES)[:, None] % LANES == head_of_lane[None, :]).astype(np.float32), BF16)
    return pl.pallas_call(
        _out_kernel,
        grid=(m // PROJ_ROWS,),
        in_specs=[rowblk(d)] + [folded(GROUP, rate) for rate in (1,) + FOLD_RATES]
        + [folded(LANES, rate) for rate in (1,) + FOLD_RATES] + [rowblk(GROUP)] * 3
        + [rowblk(N_MIXERS * GROUP), pl.BlockSpec((2 * LANES, GROUP), lambda i: (0, 0)),
           pl.BlockSpec((N_MIXERS * GROUP, d), lambda i: (0, 0))],
        out_specs=rowblk(d),
        out_shape=jax.ShapeDtypeStruct((m, d), F32),
        scratch_shapes=[pltpu.VMEM((GROUP // LANES, PROJ_ROWS, LANES), F32)],
        compiler_params=pltpu.CompilerParams(dimension_semantics=("arbitrary",), vmem_limit_bytes=VMEM_LIMIT),
        name="out_projection",
    )(x2, *a_outs, *a_lses, o_b, o_c, o_d, gate, expand, w_out)


def _block_diag_mean(group):
    idx = np.arange(GROUP) // group
    return jnp.asarray((idx[:, None] == idx[None, :]).astype(np.float32) / group, BF16)


def _layer_weights(w_in, qk_gain, qk_gain_diff):
    d = w_in.shape[0]
    sizes = (GROUP,) * 3 + (GROUP, GROUP // 2, GROUP // 2) + (GROUP,) * 3 + (GROUP,) + (HEAD_DIM,) * 6 \
        + (HEADS * 3, N_MIXERS * GROUP)
    offs = np.concatenate([[0], np.cumsum(sizes)])
    col = lambda n: w_in[:, offs[n]:offs[n + 1]]
    (a_q, a_k, a_v, b_q, b_k, b_v, c_q, c_k, c_v, d_q, d_kc, d_vc, d_ks, d_vs, d_kw, d_vw, d_g, gate) = \
        [col(n) for n in range(18)]
    rep_kv = lambda w: jnp.repeat(w.reshape(d, 2, HEAD_DIM), 2, axis=1).reshape(d, GROUP)
    wrm = jnp.concatenate([a_q, a_k, a_v, b_q, rep_kv(b_k), rep_kv(b_v), c_k, d_kc, d_vc, d_ks, d_kw, gate], axis=1)
    wt = jnp.concatenate([c_q, c_v, d_q, d_vs, d_vw, d_g, jnp.zeros((d, GATE_ROWS - HEADS * 3), w_in.dtype)], axis=1).T
    g = qk_gain
    ones = lambda n: jnp.ones((n,), F32)
    tile4 = lambda v: jnp.tile(v, HEADS)
    scale = HEAD_DIM ** -0.5 * LOG2E
    grm = jnp.concatenate([tile4(g[0]) * scale, tile4(g[1]), ones(GROUP), tile4(g[2]) * scale, tile4(g[3]),
                           ones(GROUP), jnp.tile(qk_gain_diff[1], 2 * HEADS), ones(2 * HEAD_DIM), g[6], g[7],
                           ones(N_MIXERS * GROUP)])
    gt = jnp.concatenate([jnp.tile(qk_gain_diff[0], 2 * HEADS) * (DIFF_QK_DIM ** -0.5 * LOG2E), ones(GROUP),
                          tile4(g[4]) * scale, ones(2 * HEAD_DIM + GATE_ROWS)])
    return wrm.astype(BF16), wt.astype(BF16), grm.reshape(1, -1), gt.reshape(-1, 1)


def _compress_weights(cmp_pos, cmp_w1, cmp_b1, cmp_w2, cmp_b2):
    half = CMP_LEN // 2
    pos = jnp.concatenate([cmp_pos[0], cmp_pos[1]], axis=-1)
    ptop = pos[:half].reshape(1, -1)
    pbot = pos[half:].reshape(1, -1)
    w1 = cmp_w1.reshape(2, CMP_LEN, HEAD_DIM, CMP_HIDDEN)
    zeros = jnp.zeros_like(w1[0])
    w1cat = jnp.concatenate([jnp.concatenate([w1[0], zeros], axis=-1),
                             jnp.concatenate([zeros, w1[1]], axis=-1)], axis=1)
    w1t = w1cat[:half].reshape(half * 2 * HEAD_DIM, 2 * CMP_HIDDEN).astype(BF16)
    w1b = w1cat[half:].reshape(half * 2 * HEAD_DIM, 2 * CMP_HIDDEN).astype(BF16)
    b1 = jnp.concatenate([cmp_b1[0], cmp_b1[1]]).reshape(1, -1)
    return (ptop, pbot, w1t, w1b, b1, cmp_w2[0].astype(BF16), cmp_b2[0].reshape(1, -1),
            cmp_w2[1].T.astype(BF16), cmp_b2[1].reshape(-1, 1))


def kernel(x, rel_bias_table, norm_w, w_in, w_out, qk_gain, qk_gain_diff, attn_sinks, diff_lambda, diff_subln,
           cmp_pos, cmp_w1, cmp_b1, cmp_w2, cmp_b2):
    b, s, d = x.shape
    depth = w_in.shape[0]
    n_c = s // CMP_STRIDE
    n_blk = s // SLC_BLOCK
    assert s % (BAND_TILE * DILATED_CONFIGS[-1][1]) == 0 and s % PROJ_ROWS == 0 and d == N_MIXERS * GROUP

    table = rel_bias_table.astype(F32)
    band_bias = [_build_bias(table, head0=0, n_d=1, rows=BAND_TILE, cols=2 * BAND_TILE, base0=BAND_TILE, dstep=0,
                             rs=1, cs=-1, dscale=rate, max_dist=window // rate, scale=LOG2E)
                 for window, rate in DILATED_CONFIGS]
    swa_bias = _build_bias(table, head0=HEADS, n_d=1, rows=BAND_TILE, cols=2 * BAND_TILE, base0=BAND_TILE, dstep=0,
                           rs=1, cs=-1, max_dist=SWA_WINDOW - 1, scale=LOG2E)
    flash_tiles = dict(rows=KT, cols=QT, base0=0, dstep=QT, rs=-1, cs=1, scale=LOG2E)
    diff_bias = _build_bias(table, head0=2 * HEADS, n_d=N_BIAS_TILES, d_valid=N_NEAR + 1, **flash_tiles)
    slc_bias = _build_bias(table, head0=3 * HEADS, n_d=N_BIAS_TILES, d_valid=N_NEAR + 1, **flash_tiles)
    win_bias = _build_bias(table, head0=3 * HEADS, n_d=N_WIN, max_dist=NSA_WINDOW - 1, **flash_tiles)
    far_bias = table[NUM_BUCKETS - 1] * LOG2E
    cmp_bias = _build_bias(table, head0=3 * HEADS, n_d=1, rows=n_c, cols=s, base0=-(CMP_LEN - 1), dstep=0,
                           rs=-CMP_STRIDE, cs=1, r_valid=n_c - 1, col_tile=2 * QT, scale=LOG2E)
    e64, e32 = _block_diag_mean(HEAD_DIM), _block_diag_mean(DIFF_QK_DIM)
    rep_idx = np.arange(n_blk * SEL_REP) // SEL_REP
    rep = jnp.asarray((rep_idx[:, None] == np.arange(n_blk)[None, :]).astype(np.float32), BF16)
    no_sink = jnp.zeros((HEADS,), F32)

    x2 = x.reshape(b * s, d)
    w_in_bf16 = w_in.astype(BF16)
    for layer in range(depth):
        wrm, wt, grm, gt = _layer_weights(w_in_bf16[layer], qk_gain[layer], qk_gain_diff[layer])
        (a_q, a_k, a_v, b_q, b_k, b_v, c_k, kvc, ksw, gate, a_q4, a_k4, a_v4, a_q16, a_k16, a_v16,
         c_qt, c_vt, d_qt, d_vst, d_vwt, d_gt) = _project(x2, norm_w[layer].reshape(1, d), wrm, wt, grm, gt, e64, e32)
        seq = lambda t: t.reshape(b, s, t.shape[-1])
        per_batch = lambda t: t.reshape(b, t.shape[0] // b, t.shape[1])
        flat = lambda t: t.reshape(b * t.shape[1], t.shape[2])
        a_in = ((a_q, a_k, a_v), (a_q4, a_k4, a_v4), (a_q16, a_k16, a_v16))
        a_res = [_banded(*map(per_batch, a_in[n]), band_bias[n], no_sink, rate, False)
                 for n, (_, rate) in enumerate(DILATED_CONFIGS)]
        o_b, _ = _banded(seq(b_q), seq(b_k), seq(b_v), swa_bias, attn_sinks[layer].astype(F32) * LOG2E, 1, True)
        lambda_init = 0.8 - 0.6 * math.exp(-0.3 * layer)
        o_c = _diff_attention(far_bias[2 * HEADS:3 * HEADS], c_qt, seq(c_k), c_vt, diff_bias, diff_lambda[layer].astype(F32),
                              diff_subln[layer].reshape(HEAD_DIM, 1).astype(F32), lambda_init)
        cw = _compress_weights(cmp_pos[layer], cmp_w1[layer], cmp_b1[layer], cmp_w2[layer], cmp_b2[layer])
        kc, vct = _compress(kvc.reshape(b, n_c, CMP_STRIDE * 2 * HEAD_DIM), *cw, qk_gain[layer, 5].reshape(1, -1))
        o_cmp, sel = _cmp_attention(d_qt, kc, vct, cmp_bias, b)
        o_d = _slc_win_attention(far_bias[3 * HEADS:4 * HEADS], d_qt, seq(ksw), d_vst, d_vwt, sel, rep, o_cmp, d_gt,
                                 slc_bias, win_bias)
        x2 = _out_projection(x2, [flat(r[0]) for r in a_res], [flat(r[1]) for r in a_res], o_b.reshape(b * s, GROUP),
                             o_c.reshape(b * s, GROUP), o_d.reshape(b * s, GROUP), gate, w_out[layer].astype(BF16))
    return x2.reshape(b, s, d)
```

```python
import functools
import itertools
import math

import numpy as np
import jax
import jax.numpy as jnp
from jax import lax
from jax.experimental import pallas as pl
from jax.experimental.pallas import tpu as pltpu

F32 = jnp.float32
BF16 = jnp.bfloat16

HEAD_DIM = 64
HEADS = 4
GROUP = HEADS * HEAD_DIM
N_MIXERS = 4
NUM_BUCKETS = 32
REL_MAX_DIST = 2048
DILATED_CONFIGS = ((128, 1), (512, 4), (2048, 16))
FOLD_RATES = tuple(rate for _, rate in DILATED_CONFIGS if rate > 1)
SWA_WINDOW = 128
DIFF_QK_DIM = HEAD_DIM // 2
CMP_LEN = 32
CMP_STRIDE = 16
CMP_HIDDEN = 256
SLC_BLOCK = 64
SLC_TOPK = 16
CMP_PARTS = 4
NSA_WINDOW = 512
RMS_EPS = 1e-6
NEG_INF = -1e30
FORCE_SELECT = 1e9
TINY = 1e-30
LOG2E = math.log2(math.e)

PROJ_ROWS = 512
BAND_TILE = 128
BAND_STEP = 1024
BAND_LOOKAHEAD = 2
BIAS_ROW_CHUNK = 64
LANES = 128
QT = 256
KT = 256
VMEM_LIMIT = 56 * 1024 * 1024
MXU_LOOKAHEAD = 4
V_ROWS = HEAD_DIM + 16

NT_DIMS = (((1,), (1,)), ((), ()))


def _t5_thresholds():
    n = np.arange(0, 4 * REL_MAX_DIST)
    max_exact = NUM_BUCKETS // 2
    nf = np.maximum(n, 1).astype(np.float32)
    large = max_exact + (np.log(nf / np.float32(max_exact)) / np.float32(math.log(REL_MAX_DIST / max_exact))
                         * np.float32(NUM_BUCKETS - max_exact)).astype(np.int32)
    bucket = np.where(n < max_exact, n, np.minimum(large, NUM_BUCKETS - 1))
    return [int(np.argmax(bucket >= b)) for b in range(NUM_BUCKETS)]


T5_THRESHOLDS = _t5_thresholds()
FAR_DIST = T5_THRESHOLDS[-1]


def _bias_kernel(tbl_ref, out_ref, *, head0, base0, dstep, rs, cs, dscale, max_dist, r_valid, d_valid, col_tile,
                 scale):
    h = pl.program_id(0)
    n_d, rows, cols = out_ref.shape[1:]
    chunk = BIAS_ROW_CHUNK if rows % BIAS_ROW_CHUNK == 0 else rows
    bucket_of = lambda n: max(b for b in range(NUM_BUCKETS) if T5_THRESHOLDS[b] <= n)
    for d, c0, r0 in itertools.product(range(n_d), range(0, cols, col_tile), range(0, rows, chunk)):
        out = out_ref.at[0, d, r0:r0 + chunk, c0:c0 + col_tile]
        origin = base0 + d * dstep + r0 * rs + c0 * cs
        corners = [origin + dr * rs + dc * cs for dr in (0, chunk - 1) for dc in (0, col_tile - 1)]
        lo, hi = min(corners), max(corners)
        rows_valid = min(chunk, r_valid - r0)
        if hi < 0 or lo > max_dist or d >= d_valid or rows_valid <= 0:
            out[...] = jnp.full((chunk, col_tile), NEG_INF, F32)
            continue
        r = lax.broadcasted_iota(jnp.int32, (chunk, col_tile), 0)
        c = lax.broadcasted_iota(jnp.int32, (chunk, col_tile), 1)
        dist = origin + r * rs + c * cs
        first, last = bucket_of(max(lo, 0) * dscale), bucket_of(min(hi, max_dist) * dscale)
        val = jnp.full((chunk, col_tile), tbl_ref[first, head0 + h] * scale, F32)
        for b in range(first + 1, last + 1):
            val = jnp.where(dist * dscale >= T5_THRESHOLDS[b], tbl_ref[b, head0 + h] * scale, val)
        valid = [cond for needed, cond in ((lo < 0, dist >= 0), (hi > max_dist, dist <= max_dist),
                                           (rows_valid < chunk, r < rows_valid)) if needed]
        if valid:
            val = jnp.where(functools.reduce(jnp.logical_and, valid), val, NEG_INF)
        out[...] = val


def _build_bias(table, *, head0, n_d, rows, cols, base0, dstep, rs, cs, dscale=1, max_dist=1 << 30,
                r_valid=1 << 30, d_valid=1 << 30, col_tile=None, scale=1.0):
    col_tile = cols if col_tile is None else col_tile
    kern = functools.partial(_bias_kernel, head0=head0, base0=base0, dstep=dstep, rs=rs, cs=cs, dscale=dscale,
                             max_dist=max_dist, r_valid=r_valid, d_valid=d_valid, col_tile=col_tile, scale=scale)
    return pl.pallas_call(
        kern,
        grid=(HEADS,),
        in_specs=[pl.BlockSpec(memory_space=pltpu.SMEM)],
        out_specs=pl.BlockSpec((1, n_d, rows, cols), lambda h: (h, 0, 0, 0)),
        out_shape=jax.ShapeDtypeStruct((HEADS, n_d, rows, cols), F32),
        compiler_params=pltpu.CompilerParams(dimension_semantics=("arbitrary",), vmem_limit_bytes=VMEM_LIMIT),
        name="rel_bias_tiles",
    )(table)


RM_AQ, RM_AK, RM_AV = 0, 256, 512
RM_BQ, RM_BK, RM_BV = 768, 1024, 1280
RM_CK = 1536
RM_KVC = 1792
RM_KSW = 1920
RM_GATE = 2048
RM_COLS = 3072
TR_CQ, TR_CV, TR_DQ, TR_DVS, TR_DVW, TR_DG = 0, 256, 512, 768, 832, 896
GATE_ROWS = 16
TR_ROWS = TR_DG + GATE_ROWS


def _proj_kernel(x_ref, nw_ref, wrm_ref, wt_ref, grm_ref, gt_ref, e64_ref, e32_ref,
                 aq_ref, ak_ref, av_ref, bq_ref, bk_ref, bv_ref, ck_ref, kvc_ref, ksw_ref, gate_ref,
                 aq4_ref, ak4_ref, av4_ref, aq16_ref, ak16_ref, av16_ref,
                 cq_ref, cv_ref, dq_ref, dvs_ref, dvw_ref, dg_ref, fold_ref):
    x = x_ref[...]
    ms = jnp.mean(x * x, axis=-1, keepdims=True)
    xn = (x * lax.rsqrt(ms + RMS_EPS) * nw_ref[...]).astype(BF16)
    rows = x.shape[0]

    def rm(c0, width):
        return jnp.dot(xn, wrm_ref[:, c0:c0 + width], preferred_element_type=F32)

    def rm_normed(c0, width, e_ref):
        h = rm(c0, width)
        msq = jnp.dot((h * h).astype(BF16), e_ref[0:width, 0:width], preferred_element_type=F32)
        return h * lax.rsqrt(msq + RMS_EPS) * grm_ref[:, c0:c0 + width]

    def put_folded(val, ref, folded_refs):
        ref[...] = val.astype(ref.dtype)
        for half in range(GROUP // LANES):
            fold_ref[half] = val[:, half * LANES:(half + 1) * LANES]
        for rate, fref in zip(FOLD_RATES, folded_refs):
            for rho in range(rate):
                for half in range(GROUP // LANES):
                    c0 = rho * GROUP + half * LANES
                    fref[:, c0:c0 + LANES] = fold_ref[half, pl.ds(rho, rows // rate, stride=rate), :].astype(fref.dtype)

    put_folded(rm_normed(RM_AQ, GROUP, e64_ref), aq_ref, (aq4_ref, aq16_ref))
    put_folded(rm_normed(RM_AK, GROUP, e64_ref), ak_ref, (ak4_ref, ak16_ref))
    put_folded(rm(RM_AV, GROUP), av_ref, (av4_ref, av16_ref))
    bq_ref[...] = rm_normed(RM_BQ, GROUP, e64_ref).astype(bq_ref.dtype)
    bk_ref[...] = rm_normed(RM_BK, GROUP, e64_ref).astype(bk_ref.dtype)
    bv_ref[...] = rm(RM_BV, GROUP).astype(bv_ref.dtype)
    ck_ref[...] = rm_normed(RM_CK, GROUP, e32_ref).astype(ck_ref.dtype)
    kvc_ref[...] = rm(RM_KVC, 2 * HEAD_DIM).astype(kvc_ref.dtype)
    ksw_ref[...] = rm_normed(RM_KSW, 2 * HEAD_DIM, e64_ref).astype(ksw_ref.dtype)
    gate_ref[...] = rm(RM_GATE, N_MIXERS * GROUP).astype(gate_ref.dtype)

    key_major = lax.dot_general(wt_ref[...], xn, NT_DIMS, preferred_element_type=F32)

    def tr(r0, height):
        return key_major[r0:r0 + height]

    def tr_normed(r0, height, group):
        h3 = tr(r0, height).reshape(height // group, group, rows)
        msq = jnp.mean(h3 * h3, axis=1, keepdims=True)
        return (h3 * lax.rsqrt(msq + RMS_EPS)).reshape(height, rows) * gt_ref[r0:r0 + height, :]

    def put(ref, val):
        for t in range(rows // QT):
            ref[t] = val[:, t * QT:(t + 1) * QT].astype(ref.dtype)

    def with_ones(v):
        ones = jnp.ones((V_ROWS - HEAD_DIM, rows), F32)
        parts = []
        for h in range(v.shape[0] // HEAD_DIM):
            parts += [v[h * HEAD_DIM:(h + 1) * HEAD_DIM], ones]
        return jnp.concatenate(parts, axis=0)

    put(cq_ref, tr_normed(TR_CQ, GROUP, DIFF_QK_DIM))
    put(cv_ref, with_ones(tr(TR_CV, GROUP)))
    put(dq_ref, tr_normed(TR_DQ, GROUP, HEAD_DIM))
    put(dvs_ref, with_ones(tr(TR_DVS, HEAD_DIM)))
    put(dvw_ref, with_ones(tr(TR_DVW, HEAD_DIM)))
    put(dg_ref, tr(TR_DG, GATE_ROWS))


def _project(x2, nw, wrm, wt, grm, gt, e64, e32):
    m, d = x2.shape
    nt = m // QT
    tpr = PROJ_ROWS // QT
    const = lambda shape: pl.BlockSpec(shape, lambda i: (0,) * len(shape))
    rm_out = lambda width, dtype: (jax.ShapeDtypeStruct((m, width), dtype),
                                   pl.BlockSpec((PROJ_ROWS, width), lambda i: (i, 0)))
    tr_out = lambda height, dtype: (jax.ShapeDtypeStruct((nt, height, QT), dtype),
                                    pl.BlockSpec((tpr, height, QT), lambda i: (i, 0, 0)))
    outs = [rm_out(GROUP, BF16)] * 7 + [rm_out(2 * HEAD_DIM, F32), rm_out(2 * HEAD_DIM, BF16),
                                        rm_out(N_MIXERS * GROUP, BF16)]
    fold_out = lambda rate: (jax.ShapeDtypeStruct((m // rate, rate * GROUP), BF16),
                             pl.BlockSpec((PROJ_ROWS // rate, rate * GROUP), lambda i: (i, 0)))
    outs += [fold_out(rate) for rate in FOLD_RATES for _ in range(3)]
    outs += [tr_out(GROUP, BF16), tr_out(HEADS * V_ROWS, BF16), tr_out(GROUP, BF16), tr_out(V_ROWS, BF16),
             tr_out(V_ROWS, BF16), tr_out(GATE_ROWS, F32)]
    return pl.pallas_call(
        _proj_kernel,
        grid=(m // PROJ_ROWS,),
        in_specs=[pl.BlockSpec((PROJ_ROWS, d), lambda i: (i, 0)), const((1, d)), const((d, RM_COLS)),
                  const((TR_ROWS, d)), const((1, RM_COLS)), const((TR_ROWS, 1)), const((GROUP, GROUP)),
                  const((GROUP, GROUP))],
        out_specs=[o[1] for o in outs],
        out_shape=[o[0] for o in outs],
        scratch_shapes=[pltpu.VMEM((GROUP // LANES, PROJ_ROWS, LANES), F32)],
        compiler_params=pltpu.CompilerParams(dimension_semantics=("arbitrary",), vmem_limit_bytes=VMEM_LIMIT),
        name="in_projection",
    )(x2, nw, wrm, wt, grm, gt, e64, e32)


def _band_kernel(sink_ref, q_ref, kp_ref, kc_ref, vp_ref, vc_ref, bias_ref, o_ref, lse_ref, p_ref, *, use_sink):
    i = pl.program_id(2)
    n_blocks = q_ref.shape[1] // BAND_TILE
    head_q = lax.broadcasted_iota(jnp.int32, (BAND_TILE, GROUP), 1) // HEAD_DIM
    head_v = lax.broadcasted_iota(jnp.int32, (2 * BAND_TILE, GROUP), 1) // HEAD_DIM
    lane = lax.broadcasted_iota(jnp.int32, (BAND_TILE, LANES), 1)
    in_prev = lax.broadcasted_iota(jnp.int32, (1, 2 * BAND_TILE), 1) < BAND_TILE
    no_prev = jnp.where(in_prev & (i == 0), NEG_INF, 0.0).astype(F32)

    def window(cur_ref, prev_ref, m):
        if m == 0:
            return jnp.concatenate([prev_ref[0], cur_ref[0, 0:BAND_TILE, :]], axis=0)
        return cur_ref[0, (m - 1) * BAND_TILE:(m + 1) * BAND_TILE, :]

    def scores(m):
        q = q_ref[0, m * BAND_TILE:(m + 1) * BAND_TILE, :]
        q_heads = jnp.concatenate([jnp.where(head_q == h, q, jnp.zeros_like(q)) for h in range(HEADS)], axis=0)
        bias = bias_ref[:, 0].reshape(HEADS * BAND_TILE, 2 * BAND_TILE)
        if m == 0:
            bias = bias + no_prev
        return lax.dot_general(q_heads, window(kc_ref, kp_ref, m), NT_DIMS, preferred_element_type=F32) + bias

    def update(m, s_heads):
        lse_tile = jnp.zeros((BAND_TILE, LANES), F32)
        for h in range(HEADS):
            s = s_heads[h * BAND_TILE:(h + 1) * BAND_TILE]
            mx = jnp.max(s, axis=1, keepdims=True)
            if use_sink:
                mx = jnp.maximum(mx, sink_ref[h])
            p = jnp.exp2(s - mx)
            den = jnp.sum(p, axis=1, keepdims=True)
            if use_sink:
                den = den + jnp.exp2(sink_ref[h] - mx)
            p_ref[m % 2, :, h * 2 * BAND_TILE:(h + 1) * 2 * BAND_TILE] = (p * (1.0 / den)).astype(BF16)
            lse_tile = jnp.where(lane == h, mx + jnp.log2(den), lse_tile)
        v = window(vc_ref, vp_ref, m)
        v_heads = jnp.concatenate([jnp.where(head_v == h, v, jnp.zeros_like(v)) for h in range(HEADS)], axis=0)
        rows = slice(m * BAND_TILE, (m + 1) * BAND_TILE)
        o_ref[0, rows, :] = jnp.dot(p_ref[m % 2], v_heads, preferred_element_type=F32).astype(o_ref.dtype)
        lse_ref[0, rows, :] = lse_tile

    _staggered(n_blocks, scores, update, ahead=BAND_LOOKAHEAD)


def _banded(q, k, v, bias, sink, rate, use_sink):
    b, ln, _ = q.shape
    step = min(BAND_STEP, ln)
    per_step = step // BAND_TILE
    cur = pl.BlockSpec((1, step, GROUP), lambda bb, r, i: (bb, i, r))
    prev = pl.BlockSpec((1, BAND_TILE, GROUP), lambda bb, r, i: (bb, jnp.maximum(i * per_step - 1, 0), r))
    o, lse = pl.pallas_call(
        functools.partial(_band_kernel, use_sink=use_sink),
        grid=(b, rate, ln // step),
        in_specs=[pl.BlockSpec(memory_space=pltpu.SMEM), cur, prev, cur, prev, cur,
                  pl.BlockSpec((HEADS, 1, BAND_TILE, 2 * BAND_TILE), lambda bb, r, i: (0, 0, 0, 0))],
        out_specs=[cur, pl.BlockSpec((1, step, LANES), lambda bb, r, i: (bb, i, r))],
        out_shape=[jax.ShapeDtypeStruct((b, ln, rate * GROUP), BF16), jax.ShapeDtypeStruct((b, ln, rate * LANES), F32)],
        scratch_shapes=[pltpu.VMEM((2, BAND_TILE, HEADS * 2 * BAND_TILE), BF16)],
        compiler_params=pltpu.CompilerParams(dimension_semantics=("arbitrary",) * 3),
        name=f"banded_attention_r{rate}",
    )(sink, q, k, k, v, v, bias)
    return o, lse


def _flash_reset(m_ref, acc_ref):
    m_ref[...] = jnp.full(m_ref.shape, NEG_INF, F32)
    acc_ref[...] = jnp.zeros(acc_ref.shape, F32)


def _flash_update(n, s, v_t, m_ref, acc_ref, shift=None):
    m_old = m_ref[n]
    if shift is None:
        m_new = jnp.maximum(m_old, jnp.max(s, axis=0, keepdims=True))
        p = jnp.exp2(s - m_new)
    else:
        m_new = jnp.maximum(m_old, jnp.max(s, axis=0, keepdims=True) + shift)
        p = jnp.exp2(s - (m_new - shift))
    alpha = jnp.exp2(m_old - m_new)
    acc_ref[n] = alpha * acc_ref[n] + jnp.dot(v_t, p.astype(BF16), preferred_element_type=F32)
    m_ref[n] = m_new


def _flash_result(n, acc_ref):
    return acc_ref[n, 0:HEAD_DIM, :] / acc_ref[n, HEAD_DIM:HEAD_DIM + 1, :]


def _staggered(n_items, scores, update, ahead=MXU_LOOKAHEAD):
    pending = {n: scores(n) for n in range(min(ahead, n_items))}
    for n in range(n_items):
        if n + ahead < n_items:
            pending[n + ahead] = scores(n + ahead)
        update(n, pending.pop(n))


def _pipelined_tiles(first, n_tiles, n_chains, group, load_tile, scores, update, next_ref, left_by_previous=None,
                     last_of_sweep=False):
    ahead = next_ref.shape[0]
    n_items = group * n_chains
    assert ahead <= n_chains

    def body(trip, _, issue_next=True):
        base = first + trip * group
        tiles, pending = {}, {}
        for n in range(n_items):
            cur = next_ref[n] if n < ahead else pending.pop(n)
            if n + ahead < n_items or issue_next:
                g, c = divmod(n + ahead, n_chains)
                if g not in tiles:
                    tiles[g] = load_tile(base + g)
                new = scores(tiles[g], base + g, c)
                if n + ahead < n_items:
                    pending[n + ahead] = new
                else:
                    next_ref[n + ahead - n_items] = new
            update(base + n // n_chains, n % n_chains, cur)

    if left_by_previous is None:
        first_tile = load_tile(first)
        for n in range(ahead):
            next_ref[n] = scores(first_tile, first, n)
    else:
        for n in range(ahead):
            next_ref[n] = next_ref[n] + left_by_previous(first, n)
    n_trips = (n_tiles + group - 1) // group
    if last_of_sweep:
        lax.fori_loop(0, n_trips - 1, body, None)
        body(n_trips - 1, None, issue_next=False)
    else:
        lax.fori_loop(0, n_trips, body, None)


def _flash_scratch(chains, ahead):
    return [pltpu.VMEM((chains, 1, QT), F32), pltpu.VMEM((chains, V_ROWS, QT), F32),
            pltpu.VMEM((ahead, KT, QT), F32)]


N_NEAR = -(-(FAR_DIST + KT - 1) // QT)
N_BIAS_TILES = N_NEAR + 2
DIFF_TILE_GROUP = 2
SLC_TILE_GROUP = 2
FAR_TILE_GROUP = 4


def _bias_tile_index(i, j):
    return jnp.where(j > i, N_NEAR + 1, jnp.minimum(i - j, N_NEAR))


def _whole_far_groups(i, group):
    return jnp.maximum(i - (N_NEAR - 1), 0) // group * group


def _diff_kernel(far_ref, q_ref, k_ref, v_ref, bias_ref, lam_ref, subln_ref, o_ref, qz_ref, m_ref, acc_ref, next_ref,
                 ot_ref, *, lambda_init):
    i = pl.program_id(1)
    q = q_ref[0]
    row = lax.broadcasted_iota(jnp.int32, (GROUP, QT), 0) // DIFF_QK_DIM
    for n in range(2 * HEADS):
        qz_ref[n] = jnp.where(row == n, q, jnp.zeros_like(q))
    _flash_reset(m_ref, acc_ref)

    def load_tile(j):
        return k_ref[0, pl.ds(pl.multiple_of(jnp.minimum(j, i) * KT, KT), KT), :]

    def values(j, n):
        h = n // 2
        return v_ref[0, jnp.minimum(j, i), h * V_ROWS:(h + 1) * V_ROWS, :]

    n_far = _whole_far_groups(i, FAR_TILE_GROUP)
    _pipelined_tiles(0, n_far, 2 * HEADS, FAR_TILE_GROUP, load_tile,
                     lambda k, j, n: jnp.dot(k, qz_ref[n], preferred_element_type=F32),
                     lambda j, n, s: _flash_update(n, s, values(j, n), m_ref, acc_ref, shift=far_ref[n // 2]),
                     next_ref)

    def scores(k, j, n):
        return jnp.dot(k, qz_ref[n], preferred_element_type=F32) + bias_ref[n // 2, _bias_tile_index(i, j)]

    _pipelined_tiles(n_far, i + 1 - n_far, 2 * HEADS, DIFF_TILE_GROUP, load_tile, scores,
                     lambda j, n, s: _flash_update(n, s, values(j, n), m_ref, acc_ref), next_ref,
                     left_by_previous=lambda j, n: bias_ref[n // 2, _bias_tile_index(i, j)], last_of_sweep=True)

    lam_p = lam_ref[...]
    lam = (jnp.exp(jnp.sum(lam_p[0:1] * lam_p[1:2], axis=1, keepdims=True))
           - jnp.exp(jnp.sum(lam_p[2:3] * lam_p[3:4], axis=1, keepdims=True)) + lambda_init)
    for h in range(HEADS):
        o = _flash_result(2 * h, acc_ref) - lam * _flash_result(2 * h + 1, acc_ref)
        msq = jnp.mean(o * o, axis=0, keepdims=True)
        ot_ref[h * HEAD_DIM:(h + 1) * HEAD_DIM, :] = (o * lax.rsqrt(msq + RMS_EPS) * subln_ref[...]
                                                      * (1.0 - lambda_init))
    o_ref[0] = ot_ref[...].T.astype(o_ref.dtype)


def _diff_attention(far, q_t, k, v_t, bias, lam_p, subln, lambda_init):
    b, s, _ = k.shape
    nq = s // QT
    nkv = s // KT
    v4 = v_t.reshape(b, nkv, HEADS * V_ROWS, KT)
    return pl.pallas_call(
        functools.partial(_diff_kernel, lambda_init=lambda_init),
        grid=(b, nq),
        in_specs=[pl.BlockSpec(memory_space=pltpu.SMEM),
                  pl.BlockSpec((1, GROUP, QT), lambda bb, i: (bb * nq + i, 0, 0)),
                  pl.BlockSpec((1, s, GROUP), lambda bb, i: (bb, 0, 0)),
                  pl.BlockSpec((1, nkv, HEADS * V_ROWS, KT), lambda bb, i: (bb, 0, 0, 0)),
                  pl.BlockSpec((HEADS, N_BIAS_TILES, KT, QT), lambda bb, i: (0, 0, 0, 0)),
                  pl.BlockSpec((4, DIFF_QK_DIM), lambda bb, i: (0, 0)),
                  pl.BlockSpec((HEAD_DIM, 1), lambda bb, i: (0, 0))],
        out_specs=pl.BlockSpec((1, QT, GROUP), lambda bb, i: (bb, i, 0)),
        out_shape=jax.ShapeDtypeStruct((b, s, GROUP), BF16),
        scratch_shapes=[pltpu.VMEM((2 * HEADS, GROUP, QT), BF16)] + _flash_scratch(2 * HEADS, MXU_LOOKAHEAD)
        + [pltpu.VMEM((GROUP, QT), F32)],
        compiler_params=pltpu.CompilerParams(dimension_semantics=("arbitrary", "arbitrary"),
                                             vmem_limit_bytes=VMEM_LIMIT),
        name="diff_attention",
    )(far, q_t, k, v4, bias, lam_p, subln)


def _compress_kernel(ch_ref, ptop_ref, pbot_ref, w1t_ref, w1b_ref, b1_ref, w2k_ref, b2k_ref, w2v_ref, b2v_ref,
                     gk_ref, kc_ref, vct_ref):
    ch = ch_ref[0]
    n_c = ch.shape[0]
    u = jnp.dot((ch + ptop_ref[...]).astype(BF16), w1t_ref[...], preferred_element_type=F32)
    v = jnp.dot((ch + pbot_ref[...]).astype(BF16), w1b_ref[...], preferred_element_type=F32)
    v_next = pltpu.roll(v, n_c - 1, 0)
    hid = jax.nn.gelu(u + v_next + b1_ref[...])
    hk = hid[:, :CMP_HIDDEN].astype(BF16)
    hv = hid[:, CMP_HIDDEN:].astype(BF16)
    kc = jnp.dot(hk, w2k_ref[...], preferred_element_type=F32) + b2k_ref[...]
    msq = jnp.mean(kc * kc, axis=-1, keepdims=True)
    kc_ref[0] = (kc * lax.rsqrt(msq + RMS_EPS) * gk_ref[...]).astype(kc_ref.dtype)
    vct = lax.dot_general(w2v_ref[...], hv, NT_DIMS, preferred_element_type=F32) + b2v_ref[...]
    vct_ref[0] = vct.astype(vct_ref.dtype)


def _compress(chunks, ptop, pbot, w1t, w1b, b1, w2k, b2k, w2v, b2v, gk):
    b, n_c, width = chunks.shape
    const = lambda a: pl.BlockSpec(a.shape, lambda bb: (0,) * a.ndim)
    params = (ptop, pbot, w1t, w1b, b1, w2k, b2k, w2v, b2v, gk)
    return pl.pallas_call(
        _compress_kernel,
        grid=(b,),
        in_specs=[pl.BlockSpec((1, n_c, width), lambda bb: (bb, 0, 0))] + [const(a) for a in params],
        out_specs=[pl.BlockSpec((1, n_c, HEAD_DIM), lambda bb: (bb, 0, 0)),
                   pl.BlockSpec((1, HEAD_DIM, n_c), lambda bb: (bb, 0, 0))],
        out_shape=[jax.ShapeDtypeStruct((b, n_c, HEAD_DIM), BF16), jax.ShapeDtypeStruct((b, HEAD_DIM, n_c), BF16)],
        compiler_params=pltpu.CompilerParams(dimension_semantics=("arbitrary",), vmem_limit_bytes=VMEM_LIMIT),
        name="nsa_compress",
    )(chunks, *params)


def _cmp_attn_kernel(q_ref, kc_ref, vct_ref, bias_ref, o_ref, sel_ref, p_ref, *, n_sel):
    i = pl.program_id(0)
    n_tiles = pl.num_programs(0)
    for part in range(1, CMP_PARTS + 1):
        @pl.when((i * CMP_PARTS >= (part - 1) * n_tiles) & (i * CMP_PARTS < part * n_tiles))
        def _(part=part):
            for bb in range(q_ref.shape[0]):
                one = pl.ds(bb, 1)
                _cmp_attn_body(i, kc_ref.shape[1] * part // CMP_PARTS, q_ref.at[bb], kc_ref.at[one], vct_ref.at[one],
                               bias_ref, o_ref.at[bb], sel_ref.at[bb], p_ref, n_sel)


def _cmp_attn_body(i, n_c, q_ref, kc_ref, vct_ref, bias_ref, o_ref, sel_ref, p_ref, n_sel):
    kc = kc_ref[0, 0:n_c, :]
    vct = vct_ref[0, :, 0:n_c]
    n_blk = n_c * CMP_STRIDE // SLC_BLOCK
    probs = []

    def scores(h):
        return (jnp.dot(kc, q_ref[0, h * HEAD_DIM:(h + 1) * HEAD_DIM, :], preferred_element_type=F32)
                + bias_ref[h, 0, 0:n_c, :])

    def update(h, s):
        m = jnp.maximum(jnp.max(s, axis=0, keepdims=True), 0.5 * NEG_INF)
        p = jnp.exp2(s - m)
        den = jnp.sum(p, axis=0, keepdims=True)
        p = p * (1.0 / jnp.maximum(den, TINY))
        o_ref[0, h * HEAD_DIM:(h + 1) * HEAD_DIM, :] = jnp.dot(vct, p.astype(BF16),
                                                               preferred_element_type=F32).astype(o_ref.dtype)
        probs.append(p)

    _staggered(HEADS, scores, update)
    psum = (probs[0] + probs[1]) + (probs[2] + probs[3])
    per_blk = SLC_BLOCK // CMP_STRIDE
    halves = []
    for half in range(QT // LANES):
        p_ref[half, 0:8, :] = jnp.zeros((8, LANES), F32)
        p_ref[half, 8:8 + n_c, :] = psum[:, half * LANES:(half + 1) * LANES]
        p_ref[half, 8 + n_c:16 + n_c, :] = jnp.zeros((8, LANES), F32)
        acc = p_ref[half, pl.ds(7, n_blk, stride=per_blk), :]
        for t in range(per_blk):
            acc = acc + p_ref[half, pl.ds(8 + t, n_blk, stride=per_blk), :]
        halves.append(acc)
    imp = jnp.concatenate(halves, axis=1)
    blk = lax.broadcasted_iota(jnp.int32, (n_blk, QT), 0)
    cur = (i * QT + lax.broadcasted_iota(jnp.int32, (n_blk, QT), 1)) // SLC_BLOCK
    forced = (blk == 0) | (blk == cur) | (blk == cur - 1)
    val = jnp.where(forced, FORCE_SELECT, jnp.where(blk <= cur, imp, NEG_INF))
    sel = jnp.zeros((n_blk, QT), jnp.bool_)
    for _ in range(n_sel):
        top = jnp.max(val, axis=0, keepdims=True)
        idx = jnp.min(jnp.where(val == top, blk, n_blk), axis=0, keepdims=True)
        hit = blk == idx
        sel = sel | hit
        val = jnp.where(hit, -3.0e38, val)
    sel_ref[0, 0:n_blk, :] = jnp.where(sel, 1.0, 0.0).astype(sel_ref.dtype)
    if n_blk < sel_ref.shape[1]:
        sel_ref[0, n_blk:, :] = jnp.zeros((sel_ref.shape[1] - n_blk, QT), sel_ref.dtype)


def _cmp_attention(q_t, kc, vct, bias, b):
    nt = q_t.shape[0]
    nq = nt // b
    n_c = kc.shape[1]
    n_blk = nq * QT // SLC_BLOCK
    o_cmp, sel = pl.pallas_call(
        functools.partial(_cmp_attn_kernel, n_sel=min(SLC_TOPK, n_blk)),
        grid=(nq,),
        in_specs=[pl.BlockSpec((b, 1, GROUP, QT), lambda i: (0, i, 0, 0)),
                  pl.BlockSpec((b, n_c, HEAD_DIM), lambda i: (0, 0, 0)),
                  pl.BlockSpec((b, HEAD_DIM, n_c), lambda i: (0, 0, 0)),
                  pl.BlockSpec((HEADS, 1, n_c, QT), lambda i: (0, 0, 0, i))],
        out_specs=[pl.BlockSpec((b, 1, GROUP, QT), lambda i: (0, i, 0, 0)),
                   pl.BlockSpec((b, 1, n_blk, QT), lambda i: (0, i, 0, 0))],
        out_shape=[jax.ShapeDtypeStruct((b, nq, GROUP, QT), BF16), jax.ShapeDtypeStruct((b, nq, n_blk, QT), BF16)],
        scratch_shapes=[pltpu.VMEM((QT // LANES, n_c + 16, LANES), F32)],
        compiler_params=pltpu.CompilerParams(dimension_semantics=("arbitrary",), vmem_limit_bytes=VMEM_LIMIT),
        name="nsa_compressed_attention",
    )(q_t.reshape(b, nq, GROUP, QT), kc, vct, bias)
    return o_cmp.reshape(nt, GROUP, QT), sel.reshape(nt, n_blk, QT)


SEL_REP = 8
N_WIN = -(-(NSA_WINDOW - 1 + KT - 1) // QT)


def _slc_win_kernel(far_ref, q_ref, ksw_ref, vs_ref, vw_ref, sel_ref, rep_ref, ocmp_ref, g_ref, bslc_ref, bwin_ref,
                    o_ref, qz_ref, m_ref, acc_ref, next_ref, ot_ref, mask_ref):
    i = pl.program_id(1)
    sel8 = jnp.dot(rep_ref[...], sel_ref[0], preferred_element_type=F32)
    mask_ref[...] = (sel8 - 1.0) * (-NEG_INF)
    blocks_per_tile = KT // SLC_BLOCK
    mrows = blocks_per_tile * SEL_REP
    zeros = jnp.zeros((HEAD_DIM, QT), BF16)
    for h in range(HEADS):
        qh = q_ref[0, h * HEAD_DIM:(h + 1) * HEAD_DIM, :]
        qz_ref[h] = jnp.concatenate([qh, zeros], axis=0)
        qz_ref[HEADS + h] = jnp.concatenate([zeros, qh], axis=0)
    _flash_reset(m_ref, acc_ref)

    def load_keys(j):
        return ksw_ref[0, pl.ds(pl.multiple_of(j * KT, KT), KT), :]

    def load_tile(j):
        j = jnp.minimum(j, i)
        m8 = mask_ref[pl.ds(pl.multiple_of(j * mrows, mrows), mrows), :]
        mask = jnp.broadcast_to(m8.reshape(blocks_per_tile, 1, SEL_REP, QT),
                                (blocks_per_tile, SLC_BLOCK // SEL_REP, SEL_REP, QT)).reshape(KT, QT)
        return load_keys(j), mask

    def values(j):
        return vs_ref[0, jnp.minimum(j, i)]

    n_far = _whole_far_groups(i, FAR_TILE_GROUP)
    _pipelined_tiles(0, n_far, HEADS, FAR_TILE_GROUP, load_tile,
                     lambda tile, j, h: jnp.dot(tile[0], qz_ref[h], preferred_element_type=F32) + tile[1],
                     lambda j, h, s: _flash_update(h, s, values(j), m_ref, acc_ref, shift=far_ref[h]), next_ref)

    def slc_scores(tile, j, h):
        k, mask = tile
        return jnp.dot(k, qz_ref[h], preferred_element_type=F32) + mask + bslc_ref[h, _bias_tile_index(i, j)]

    _pipelined_tiles(n_far, i + 1 - n_far, HEADS, SLC_TILE_GROUP, load_tile, slc_scores,
                     lambda j, h, s: _flash_update(h, s, values(j), m_ref, acc_ref), next_ref,
                     left_by_previous=lambda j, h: bslc_ref[h, _bias_tile_index(i, j)], last_of_sweep=True)

    def win_tile(n):
        d = N_WIN - 1 - n // HEADS
        return d, n % HEADS, jnp.maximum(i - d, 0)

    def win_scores(n):
        d, h, j = win_tile(n)
        missing = jnp.where(i < d, NEG_INF, 0.0).astype(F32)
        return jnp.dot(load_keys(j), qz_ref[HEADS + h], preferred_element_type=F32) + (bwin_ref[h, d] + missing)

    def win_update(n, s):
        _, h, j = win_tile(n)
        _flash_update(HEADS + h, s, vw_ref[0, j], m_ref, acc_ref)

    _staggered(N_WIN * HEADS, win_scores, win_update)

    for h in range(HEADS):
        g = jax.nn.sigmoid(g_ref[0, 3 * h:3 * h + 3, :])
        ot_ref[h * HEAD_DIM:(h + 1) * HEAD_DIM, :] = (g[0:1] * ocmp_ref[0, h * HEAD_DIM:(h + 1) * HEAD_DIM, :]
                                                      + g[1:2] * _flash_result(h, acc_ref)
                                                      + g[2:3] * _flash_result(HEADS + h, acc_ref))
    o_ref[0] = ot_ref[...].T.astype(o_ref.dtype)


def _slc_win_attention(far, q_t, ksw, vs_t, vw_t, sel, rep, ocmp, g_t, bslc, bwin):
    b, s, _ = ksw.shape
    nq = s // QT
    nkv = s // KT
    n_blk = s // SLC_BLOCK
    tile = lambda height: pl.BlockSpec((1, height, QT), lambda bb, i: (bb * nq + i, 0, 0))
    whole = lambda a: pl.BlockSpec(a.shape, lambda bb, i: (0,) * a.ndim)
    return pl.pallas_call(
        _slc_win_kernel,
        grid=(b, nq),
        in_specs=[pl.BlockSpec(memory_space=pltpu.SMEM), tile(GROUP),
                  pl.BlockSpec((1, s, 2 * HEAD_DIM), lambda bb, i: (bb, 0, 0)),
                  pl.BlockSpec((1, nkv, V_ROWS, KT), lambda bb, i: (bb, 0, 0, 0)),
                  pl.BlockSpec((1, nkv, V_ROWS, KT), lambda bb, i: (bb, 0, 0, 0)),
                  tile(n_blk), whole(rep), tile(GROUP), tile(GATE_ROWS), whole(bslc), whole(bwin)],
        out_specs=pl.BlockSpec((1, QT, GROUP), lambda bb, i: (bb, i, 0)),
        out_shape=jax.ShapeDtypeStruct((b, s, GROUP), BF16),
        scratch_shapes=[pltpu.VMEM((2 * HEADS, 2 * HEAD_DIM, QT), BF16)] + _flash_scratch(2 * HEADS, MXU_LOOKAHEAD)
        + [pltpu.VMEM((GROUP, QT), F32), pltpu.VMEM((n_blk * SEL_REP, QT), F32)],
        compiler_params=pltpu.CompilerParams(dimension_semantics=("arbitrary", "arbitrary"),
                                             vmem_limit_bytes=VMEM_LIMIT),
        name="nsa_selected_window_attention",
    )(far, q_t, ksw, vs_t.reshape(b, nkv, V_ROWS, KT), vw_t.reshape(b, nkv, V_ROWS, KT), sel, rep, ocmp, g_t,
      bslc, bwin)


def _out_kernel(x_ref, a0_ref, a1_ref, a2_ref, l0_ref, l1_ref, l2_ref, ob_ref, oc_ref, od_ref, gate_ref, e_ref,
                w_ref, o_ref, unfold_ref):
    rows = x_ref.shape[0]

    def unfolded(ref, rate):
        width = ref.shape[1] // rate
        for rho in range(rate):
            for part in range(width // LANES):
                c0 = rho * width + part * LANES
                unfold_ref[part, pl.ds(rho, rows // rate, stride=rate), :] = ref[:, c0:c0 + LANES].astype(F32)
        return jnp.concatenate([unfold_ref[part] for part in range(width // LANES)], axis=1)

    a0, l0 = a0_ref[...], l0_ref[...]
    a1, l1 = unfolded(a1_ref, FOLD_RATES[0]), unfolded(l1_ref, FOLD_RATES[0])
    a2, l2 = unfolded(a2_ref, FOLD_RATES[1]), unfolded(l2_ref, FOLD_RATES[1])
    mx = jnp.maximum(jnp.maximum(l0, l1), l2)
    e0, e1, e2 = jnp.exp2(l0 - mx), jnp.exp2(l1 - mx), jnp.exp2(l2 - mx)
    den = e0 + e1 + e2

    def per_head_lanes(w):
        hi = w.astype(BF16)
        lo = (w - hi.astype(F32)).astype(BF16)
        return (jnp.dot(hi, e_ref[...], preferred_element_type=F32)
                + jnp.dot(lo, e_ref[...], preferred_element_type=F32))

    o_a = per_head_lanes(e0 / den) * a0 + per_head_lanes(e1 / den) * a1 + per_head_lanes(e2 / den) * a2
    y = jnp.concatenate([o_a, ob_ref[...].astype(F32), oc_ref[...].astype(F32), od_ref[...].astype(F32)], axis=1)
    g = gate_ref[...].astype(F32)
    y = y * (g * jax.nn.sigmoid(g))
    o_ref[...] = x_ref[...] + jnp.dot(y.astype(BF16), w_ref[...], preferred_element_type=F32)


def _out_projection(x2, a_outs, a_lses, o_b, o_c, o_d, gate, w_out):
    m, d = x2.shape
    rowblk = lambda width: pl.BlockSpec((PROJ_ROWS, width), lambda i: (i, 0))
    folded = lambda width, rate: pl.BlockSpec((PROJ_ROWS // rate, rate * width), lambda i: (i, 0))
    head_of_lane = np.arange(GROUP) // HEAD_DIM
    expand = jnp.asarray((np.arange(LANES)[:, None] == head_of_lane[None, :]).astype(np.float32), BF16)
    return pl.pallas_call(
        _out_kernel,
        grid=(m // PROJ_ROWS,),
        in_specs=[rowblk(d)] + [folded(GROUP, rate) for rate in (1,) + FOLD_RATES]
        + [folded(LANES, rate) for rate in (1,) + FOLD_RATES] + [rowblk(GROUP)] * 3
        + [rowblk(N_MIXERS * GROUP), pl.BlockSpec((LANES, GROUP), lambda i: (0, 0)),
           pl.BlockSpec((N_MIXERS * GROUP, d), lambda i: (0, 0))],
        out_specs=rowblk(d),
        out_shape=jax.ShapeDtypeStruct((m, d), F32),
        scratch_shapes=[pltpu.VMEM((GROUP // LANES, PROJ_ROWS, LANES), F32)],
        compiler_params=pltpu.CompilerParams(dimension_semantics=("arbitrary",), vmem_limit_bytes=VMEM_LIMIT),
        name="out_projection",
    )(x2, *a_outs, *a_lses, o_b, o_c, o_d, gate, expand, w_out)


def _block_diag_mean(group):
    idx = np.arange(GROUP) // group
    return jnp.asarray((idx[:, None] == idx[None, :]).astype(np.float32) / group, BF16)


def _layer_weights(w_in, qk_gain, qk_gain_diff):
    d = w_in.shape[0]
    sizes = (GROUP,) * 3 + (GROUP, GROUP // 2, GROUP // 2) + (GROUP,) * 3 + (GROUP,) + (HEAD_DIM,) * 6 \
        + (HEADS * 3, N_MIXERS * GROUP)
    offs = np.concatenate([[0], np.cumsum(sizes)])
    col = lambda n: w_in[:, offs[n]:offs[n + 1]]
    (a_q, a_k, a_v, b_q, b_k, b_v, c_q, c_k, c_v, d_q, d_kc, d_vc, d_ks, d_vs, d_kw, d_vw, d_g, gate) = \
        [col(n) for n in range(18)]
    rep_kv = lambda w: jnp.repeat(w.reshape(d, 2, HEAD_DIM), 2, axis=1).reshape(d, GROUP)
    wrm = jnp.concatenate([a_q, a_k, a_v, b_q, rep_kv(b_k), rep_kv(b_v), c_k, d_kc, d_vc, d_ks, d_kw, gate], axis=1)
    wt = jnp.concatenate([c_q, c_v, d_q, d_vs, d_vw, d_g, jnp.zeros((d, GATE_ROWS - HEADS * 3), w_in.dtype)], axis=1).T
    g = qk_gain
    ones = lambda n: jnp.ones((n,), F32)
    tile4 = lambda v: jnp.tile(v, HEADS)
    scale = HEAD_DIM ** -0.5 * LOG2E
    grm = jnp.concatenate([tile4(g[0]) * scale, tile4(g[1]), ones(GROUP), tile4(g[2]) * scale, tile4(g[3]),
                           ones(GROUP), jnp.tile(qk_gain_diff[1], 2 * HEADS), ones(2 * HEAD_DIM), g[6], g[7],
                           ones(N_MIXERS * GROUP)])
    gt = jnp.concatenate([jnp.tile(qk_gain_diff[0], 2 * HEADS) * (DIFF_QK_DIM ** -0.5 * LOG2E), ones(GROUP),
                          tile4(g[4]) * scale, ones(2 * HEAD_DIM + GATE_ROWS)])
    return wrm.astype(BF16), wt.astype(BF16), grm.reshape(1, -1), gt.reshape(-1, 1)


def _compress_weights(cmp_pos, cmp_w1, cmp_b1, cmp_w2, cmp_b2):
    half = CMP_LEN // 2
    pos = jnp.concatenate([cmp_pos[0], cmp_pos[1]], axis=-1)
    ptop = pos[:half].reshape(1, -1)
    pbot = pos[half:].reshape(1, -1)
    w1 = cmp_w1.reshape(2, CMP_LEN, HEAD_DIM, CMP_HIDDEN)
    zeros = jnp.zeros_like(w1[0])
    w1cat = jnp.concatenate([jnp.concatenate([w1[0], zeros], axis=-1),
                             jnp.concatenate([zeros, w1[1]], axis=-1)], axis=1)
    w1t = w1cat[:half].reshape(half * 2 * HEAD_DIM, 2 * CMP_HIDDEN).astype(BF16)
    w1b = w1cat[half:].reshape(half * 2 * HEAD_DIM, 2 * CMP_HIDDEN).astype(BF16)
    b1 = jnp.concatenate([cmp_b1[0], cmp_b1[1]]).reshape(1, -1)
    return (ptop, pbot, w1t, w1b, b1, cmp_w2[0].astype(BF16), cmp_b2[0].reshape(1, -1),
            cmp_w2[1].T.astype(BF16), cmp_b2[1].reshape(-1, 1))


def kernel(x, rel_bias_table, norm_w, w_in, w_out, qk_gain, qk_gain_diff, attn_sinks, diff_lambda, diff_subln,
           cmp_pos, cmp_w1, cmp_b1, cmp_w2, cmp_b2):
    b, s, d = x.shape
    depth = w_in.shape[0]
    n_c = s // CMP_STRIDE
    n_blk = s // SLC_BLOCK
    assert s % (BAND_TILE * DILATED_CONFIGS[-1][1]) == 0 and s % PROJ_ROWS == 0 and d == N_MIXERS * GROUP

    table = rel_bias_table.astype(F32)
    band_bias = [_build_bias(table, head0=0, n_d=1, rows=BAND_TILE, cols=2 * BAND_TILE, base0=BAND_TILE, dstep=0,
                             rs=1, cs=-1, dscale=rate, max_dist=window // rate, scale=LOG2E)
                 for window, rate in DILATED_CONFIGS]
    swa_bias = _build_bias(table, head0=HEADS, n_d=1, rows=BAND_TILE, cols=2 * BAND_TILE, base0=BAND_TILE, dstep=0,
                           rs=1, cs=-1, max_dist=SWA_WINDOW - 1, scale=LOG2E)
    flash_tiles = dict(rows=KT, cols=QT, base0=0, dstep=QT, rs=-1, cs=1, scale=LOG2E)
    diff_bias = _build_bias(table, head0=2 * HEADS, n_d=N_BIAS_TILES, d_valid=N_NEAR + 1, **flash_tiles)
    slc_bias = _build_bias(table, head0=3 * HEADS, n_d=N_BIAS_TILES, d_valid=N_NEAR + 1, **flash_tiles)
    win_bias = _build_bias(table, head0=3 * HEADS, n_d=N_WIN, max_dist=NSA_WINDOW - 1, **flash_tiles)
    far_bias = table[NUM_BUCKETS - 1] * LOG2E
    cmp_bias = _build_bias(table, head0=3 * HEADS, n_d=1, rows=n_c, cols=s, base0=-(CMP_LEN - 1), dstep=0,
                           rs=-CMP_STRIDE, cs=1, r_valid=n_c - 1, col_tile=2 * QT, scale=LOG2E)
    e64, e32 = _block_diag_mean(HEAD_DIM), _block_diag_mean(DIFF_QK_DIM)
    rep_idx = np.arange(n_blk * SEL_REP) // SEL_REP
    rep = jnp.asarray((rep_idx[:, None] == np.arange(n_blk)[None, :]).astype(np.float32), BF16)
    no_sink = jnp.zeros((HEADS,), F32)

    x2 = x.reshape(b * s, d)
    w_in_bf16 = w_in.astype(BF16)
    for layer in range(depth):
        wrm, wt, grm, gt = _layer_weights(w_in_bf16[layer], qk_gain[layer], qk_gain_diff[layer])
        (a_q, a_k, a_v, b_q, b_k, b_v, c_k, kvc, ksw, gate, a_q4, a_k4, a_v4, a_q16, a_k16, a_v16,
         c_qt, c_vt, d_qt, d_vst, d_vwt, d_gt) = _project(x2, norm_w[layer].reshape(1, d), wrm, wt, grm, gt, e64, e32)
        seq = lambda t: t.reshape(b, s, t.shape[-1])
        per_batch = lambda t: t.reshape(b, t.shape[0] // b, t.shape[1])
        flat = lambda t: t.reshape(b * t.shape[1], t.shape[2])
        a_in = ((a_q, a_k, a_v), (a_q4, a_k4, a_v4), (a_q16, a_k16, a_v16))
        a_res = [_banded(*map(per_batch, a_in[n]), band_bias[n], no_sink, rate, False)
                 for n, (_, rate) in enumerate(DILATED_CONFIGS)]
        o_b, _ = _banded(seq(b_q), seq(b_k), seq(b_v), swa_bias, attn_sinks[layer].astype(F32) * LOG2E, 1, True)
        lambda_init = 0.8 - 0.6 * math.exp(-0.3 * layer)
        o_c = _diff_attention(far_bias[2 * HEADS:3 * HEADS], c_qt, seq(c_k), c_vt, diff_bias, diff_lambda[layer].astype(F32),
                              diff_subln[layer].reshape(HEAD_DIM, 1).astype(F32), lambda_init)
        cw = _compress_weights(cmp_pos[layer], cmp_w1[layer], cmp_b1[layer], cmp_w2[layer], cmp_b2[layer])
        kc, vct = _compress(kvc.reshape(b, n_c, CMP_STRIDE * 2 * HEAD_DIM), *cw, qk_gain[layer, 5].reshape(1, -1))
        o_cmp, sel = _cmp_attention(d_qt, kc, vct, cmp_bias, b)
        o_d = _slc_win_attention(far_bias[3 * HEADS:4 * HEADS], d_qt, seq(ksw), d_vst, d_vwt, sel, rep, o_cmp, d_gt,
                                 slc_bias, win_bias)
        x2 = _out_projection(x2, [flat(r[0]) for r in a_res], [flat(r[1]) for r in a_res], o_b.reshape(b * s, GROUP),
                             o_c.reshape(b * s, GROUP), o_d.reshape(b * s, GROUP), gate, w_out[layer].astype(BF16))
    return x2.reshape(b, s, d)
```

```python
import functools
import itertools
import math

import numpy as np
import jax
import jax.numpy as jnp
from jax import lax
from jax.experimental import pallas as pl
from jax.experimental.pallas import tpu as pltpu

F32 = jnp.float32
BF16 = jnp.bfloat16

HEAD_DIM = 64
HEADS = 4
GROUP = HEADS * HEAD_DIM
N_MIXERS = 4
NUM_BUCKETS = 32
REL_MAX_DIST = 2048
DILATED_CONFIGS = ((128, 1), (512, 4), (2048, 16))
FOLD_RATES = tuple(rate for _, rate in DILATED_CONFIGS if rate > 1)
SWA_WINDOW = 128
DIFF_QK_DIM = HEAD_DIM // 2
CMP_LEN = 32
CMP_STRIDE = 16
CMP_HIDDEN = 256
SLC_BLOCK = 64
SLC_TOPK = 16
CMP_PARTS = 4
NSA_WINDOW = 512
RMS_EPS = 1e-6
NEG_INF = -1e30
FORCE_SELECT = 1e9
TINY = 1e-30
LOG2E = math.log2(math.e)

PROJ_ROWS = 512
BAND_TILE = 128
BAND_STEP = 1024
BAND_LOOKAHEAD = 2
BIAS_ROW_CHUNK = 64
LANES = 128
QT = 256
KT = 256
VMEM_LIMIT = 56 * 1024 * 1024
MXU_LOOKAHEAD = 4
V_ROWS = HEAD_DIM + 16

NT_DIMS = (((1,), (1,)), ((), ()))


def _t5_thresholds():
    n = np.arange(0, 4 * REL_MAX_DIST)
    max_exact = NUM_BUCKETS // 2
    nf = np.maximum(n, 1).astype(np.float32)
    large = max_exact + (np.log(nf / np.float32(max_exact)) / np.float32(math.log(REL_MAX_DIST / max_exact))
                         * np.float32(NUM_BUCKETS - max_exact)).astype(np.int32)
    bucket = np.where(n < max_exact, n, np.minimum(large, NUM_BUCKETS - 1))
    return [int(np.argmax(bucket >= b)) for b in range(NUM_BUCKETS)]


T5_THRESHOLDS = _t5_thresholds()
FAR_DIST = T5_THRESHOLDS[-1]


def _bias_kernel(tbl_ref, out_ref, *, head0, base0, dstep, rs, cs, dscale, max_dist, r_valid, d_valid, col_tile,
                 scale):
    h = pl.program_id(0)
    n_d, rows, cols = out_ref.shape[1:]
    chunk = BIAS_ROW_CHUNK if rows % BIAS_ROW_CHUNK == 0 else rows
    bucket_of = lambda n: max(b for b in range(NUM_BUCKETS) if T5_THRESHOLDS[b] <= n)
    for d, c0, r0 in itertools.product(range(n_d), range(0, cols, col_tile), range(0, rows, chunk)):
        out = out_ref.at[0, d, r0:r0 + chunk, c0:c0 + col_tile]
        origin = base0 + d * dstep + r0 * rs + c0 * cs
        corners = [origin + dr * rs + dc * cs for dr in (0, chunk - 1) for dc in (0, col_tile - 1)]
        lo, hi = min(corners), max(corners)
        rows_valid = min(chunk, r_valid - r0)
        if hi < 0 or lo > max_dist or d >= d_valid or rows_valid <= 0:
            out[...] = jnp.full((chunk, col_tile), NEG_INF, F32)
            continue
        r = lax.broadcasted_iota(jnp.int32, (chunk, col_tile), 0)
        c = lax.broadcasted_iota(jnp.int32, (chunk, col_tile), 1)
        dist = origin + r * rs + c * cs
        first, last = bucket_of(max(lo, 0) * dscale), bucket_of(min(hi, max_dist) * dscale)
        val = jnp.full((chunk, col_tile), tbl_ref[first, head0 + h] * scale, F32)
        for b in range(first + 1, last + 1):
            val = jnp.where(dist * dscale >= T5_THRESHOLDS[b], tbl_ref[b, head0 + h] * scale, val)
        valid = [cond for needed, cond in ((lo < 0, dist >= 0), (hi > max_dist, dist <= max_dist),
                                           (rows_valid < chunk, r < rows_valid)) if needed]
        if valid:
            val = jnp.where(functools.reduce(jnp.logical_and, valid), val, NEG_INF)
        out[...] = val


def _build_bias(table, *, head0, n_d, rows, cols, base0, dstep, rs, cs, dscale=1, max_dist=1 << 30,
                r_valid=1 << 30, d_valid=1 << 30, col_tile=None, scale=1.0):
    col_tile = cols if col_tile is None else col_tile
    kern = functools.partial(_bias_kernel, head0=head0, base0=base0, dstep=dstep, rs=rs, cs=cs, dscale=dscale,
                             max_dist=max_dist, r_valid=r_valid, d_valid=d_valid, col_tile=col_tile, scale=scale)
    return pl.pallas_call(
        kern,
        grid=(HEADS,),
        in_specs=[pl.BlockSpec(memory_space=pltpu.SMEM)],
        out_specs=pl.BlockSpec((1, n_d, rows, cols), lambda h: (h, 0, 0, 0)),
        out_shape=jax.ShapeDtypeStruct((HEADS, n_d, rows, cols), F32),
        compiler_params=pltpu.CompilerParams(dimension_semantics=("arbitrary",), vmem_limit_bytes=VMEM_LIMIT),
        name="rel_bias_tiles",
    )(table)


RM_AQ, RM_AK, RM_AV = 0, 256, 512
RM_BQ, RM_BK, RM_BV = 768, 1024, 1280
RM_CK = 1536
RM_KVC = 1792
RM_KSW = 1920
RM_GATE = 2048
RM_COLS = 3072
TR_CQ, TR_CV, TR_DQ, TR_DVS, TR_DVW, TR_DG = 0, 256, 512, 768, 832, 896
GATE_ROWS = 16
TR_ROWS = TR_DG + GATE_ROWS


def _proj_kernel(x_ref, nw_ref, wrm_ref, wt_ref, grm_ref, gt_ref, e64_ref, e32_ref,
                 aq_ref, ak_ref, av_ref, bq_ref, bk_ref, bv_ref, ck_ref, kvc_ref, ksw_ref, gate_ref,
                 aq4_ref, ak4_ref, av4_ref, aq16_ref, ak16_ref, av16_ref,
                 cq_ref, cv_ref, dq_ref, dvs_ref, dvw_ref, dg_ref, fold_ref):
    x = x_ref[...]
    ms = jnp.mean(x * x, axis=-1, keepdims=True)
    xn = (x * lax.rsqrt(ms + RMS_EPS) * nw_ref[...]).astype(BF16)
    rows = x.shape[0]

    def rm(c0, width):
        return jnp.dot(xn, wrm_ref[:, c0:c0 + width], preferred_element_type=F32)

    def rm_normed(c0, width, e_ref):
        h = rm(c0, width)
        msq = jnp.dot((h * h).astype(BF16), e_ref[0:width, 0:width], preferred_element_type=F32)
        return h * lax.rsqrt(msq + RMS_EPS) * grm_ref[:, c0:c0 + width]

    def put_folded(val, ref, folded_refs):
        ref[...] = val.astype(ref.dtype)
        for half in range(GROUP // LANES):
            fold_ref[half] = val[:, half * LANES:(half + 1) * LANES]
        for rate, fref in zip(FOLD_RATES, folded_refs):
            for rho in range(rate):
                for half in range(GROUP // LANES):
                    c0 = rho * GROUP + half * LANES
                    fref[:, c0:c0 + LANES] = fold_ref[half, pl.ds(rho, rows // rate, stride=rate), :].astype(fref.dtype)

    put_folded(rm_normed(RM_AQ, GROUP, e64_ref), aq_ref, (aq4_ref, aq16_ref))
    put_folded(rm_normed(RM_AK, GROUP, e64_ref), ak_ref, (ak4_ref, ak16_ref))
    put_folded(rm(RM_AV, GROUP), av_ref, (av4_ref, av16_ref))
    bq_ref[...] = rm_normed(RM_BQ, GROUP, e64_ref).astype(bq_ref.dtype)
    bk_ref[...] = rm_normed(RM_BK, GROUP, e64_ref).astype(bk_ref.dtype)
    bv_ref[...] = rm(RM_BV, GROUP).astype(bv_ref.dtype)
    ck_ref[...] = rm_normed(RM_CK, GROUP, e32_ref).astype(ck_ref.dtype)
    kvc_ref[...] = rm(RM_KVC, 2 * HEAD_DIM).astype(kvc_ref.dtype)
    ksw_ref[...] = rm_normed(RM_KSW, 2 * HEAD_DIM, e64_ref).astype(ksw_ref.dtype)
    gate_ref[...] = rm(RM_GATE, N_MIXERS * GROUP).astype(gate_ref.dtype)

    key_major = lax.dot_general(wt_ref[...], xn, NT_DIMS, preferred_element_type=F32)

    def tr(r0, height):
        return key_major[r0:r0 + height]

    def tr_normed(r0, height, group):
        h3 = tr(r0, height).reshape(height // group, group, rows)
        msq = jnp.mean(h3 * h3, axis=1, keepdims=True)
        return (h3 * lax.rsqrt(msq + RMS_EPS)).reshape(height, rows) * gt_ref[r0:r0 + height, :]

    def put(ref, val):
        for t in range(rows // QT):
            ref[t] = val[:, t * QT:(t + 1) * QT].astype(ref.dtype)

    def with_ones(v):
        ones = jnp.ones((V_ROWS - HEAD_DIM, rows), F32)
        parts = []
        for h in range(v.shape[0] // HEAD_DIM):
            parts += [v[h * HEAD_DIM:(h + 1) * HEAD_DIM], ones]
        return jnp.concatenate(parts, axis=0)

    put(cq_ref, tr_normed(TR_CQ, GROUP, DIFF_QK_DIM))
    put(cv_ref, with_ones(tr(TR_CV, GROUP)))
    put(dq_ref, tr_normed(TR_DQ, GROUP, HEAD_DIM))
    put(dvs_ref, with_ones(tr(TR_DVS, HEAD_DIM)))
    put(dvw_ref, with_ones(tr(TR_DVW, HEAD_DIM)))
    put(dg_ref, tr(TR_DG, GATE_ROWS))


def _project(x2, nw, wrm, wt, grm, gt, e64, e32):
    m, d = x2.shape
    nt = m // QT
    tpr = PROJ_ROWS // QT
    const = lambda shape: pl.BlockSpec(shape, lambda i: (0,) * len(shape))
    rm_out = lambda width, dtype: (jax.ShapeDtypeStruct((m, width), dtype),
                                   pl.BlockSpec((PROJ_ROWS, width), lambda i: (i, 0)))
    tr_out = lambda height, dtype: (jax.ShapeDtypeStruct((nt, height, QT), dtype),
                                    pl.BlockSpec((tpr, height, QT), lambda i: (i, 0, 0)))
    outs = [rm_out(GROUP, BF16)] * 7 + [rm_out(2 * HEAD_DIM, F32), rm_out(2 * HEAD_DIM, BF16),
                                        rm_out(N_MIXERS * GROUP, BF16)]
    fold_out = lambda rate: (jax.ShapeDtypeStruct((m // rate, rate * GROUP), BF16),
                             pl.BlockSpec((PROJ_ROWS // rate, rate * GROUP), lambda i: (i, 0)))
    outs += [fold_out(rate) for rate in FOLD_RATES for _ in range(3)]
    outs += [tr_out(GROUP, BF16), tr_out(HEADS * V_ROWS, BF16), tr_out(GROUP, BF16), tr_out(V_ROWS, BF16),
             tr_out(V_ROWS, BF16), tr_out(GATE_ROWS, F32)]
    return pl.pallas_call(
        _proj_kernel,
        grid=(m // PROJ_ROWS,),
        in_specs=[pl.BlockSpec((PROJ_ROWS, d), lambda i: (i, 0)), const((1, d)), const((d, RM_COLS)),
                  const((TR_ROWS, d)), const((1, RM_COLS)), const((TR_ROWS, 1)), const((GROUP, GROUP)),
                  const((GROUP, GROUP))],
        out_specs=[o[1] for o in outs],
        out_shape=[o[0] for o in outs],
        scratch_shapes=[pltpu.VMEM((GROUP // LANES, PROJ_ROWS, LANES), F32)],
        compiler_params=pltpu.CompilerParams(dimension_semantics=("arbitrary",), vmem_limit_bytes=VMEM_LIMIT),
        name="in_projection",
    )(x2, nw, wrm, wt, grm, gt, e64, e32)


def _band_kernel(sink_ref, q_ref, kp_ref, kc_ref, vp_ref, vc_ref, bias_ref, o_ref, lse_ref, p_ref, *, use_sink):
    i = pl.program_id(1)
    n_blocks = q_ref.shape[1] // BAND_TILE
    head_q = lax.broadcasted_iota(jnp.int32, (BAND_TILE, GROUP), 1) // HEAD_DIM
    head_v = lax.broadcasted_iota(jnp.int32, (2 * BAND_TILE, GROUP), 1) // HEAD_DIM
    lane = lax.broadcasted_iota(jnp.int32, (BAND_TILE, LANES), 1)
    in_prev = lax.broadcasted_iota(jnp.int32, (1, 2 * BAND_TILE), 1) < BAND_TILE
    no_prev = jnp.where(in_prev & (i == 0), NEG_INF, 0.0).astype(F32)

    def window(cur_ref, prev_ref, bb, m):
        if m == 0:
            return jnp.concatenate([prev_ref[bb], cur_ref[bb, 0:BAND_TILE, :]], axis=0)
        return cur_ref[bb, (m - 1) * BAND_TILE:(m + 1) * BAND_TILE, :]

    def scores(n):
        bb, m = divmod(n, n_blocks)
        q = q_ref[bb, m * BAND_TILE:(m + 1) * BAND_TILE, :]
        q_heads = jnp.concatenate([jnp.where(head_q == h, q, jnp.zeros_like(q)) for h in range(HEADS)], axis=0)
        bias = bias_ref[:, 0].reshape(HEADS * BAND_TILE, 2 * BAND_TILE)
        if m == 0:
            bias = bias + no_prev
        return lax.dot_general(q_heads, window(kc_ref, kp_ref, bb, m), NT_DIMS, preferred_element_type=F32) + bias

    def update(n, s_heads):
        bb, m = divmod(n, n_blocks)
        lse_tile = jnp.zeros((BAND_TILE, LANES), F32)
        for h in range(HEADS):
            s = s_heads[h * BAND_TILE:(h + 1) * BAND_TILE]
            mx = jnp.max(s, axis=1, keepdims=True)
            if use_sink:
                mx = jnp.maximum(mx, sink_ref[h])
            p = jnp.exp2(s - mx)
            den = jnp.sum(p, axis=1, keepdims=True)
            if use_sink:
                den = den + jnp.exp2(sink_ref[h] - mx)
            p_ref[n % 2, :, h * 2 * BAND_TILE:(h + 1) * 2 * BAND_TILE] = (p * (1.0 / den)).astype(BF16)
            lse_tile = jnp.where(lane == h, mx + jnp.log2(den), lse_tile)
        v = window(vc_ref, vp_ref, bb, m)
        v_heads = jnp.concatenate([jnp.where(head_v == h, v, jnp.zeros_like(v)) for h in range(HEADS)], axis=0)
        rows = slice(m * BAND_TILE, (m + 1) * BAND_TILE)
        o_ref[bb, rows, :] = jnp.dot(p_ref[n % 2], v_heads, preferred_element_type=F32).astype(o_ref.dtype)
        lse_ref[bb, rows, :] = lse_tile

    _staggered(q_ref.shape[0] * n_blocks, scores, update, ahead=BAND_LOOKAHEAD)


def _banded(q, k, v, bias, sink, rate, use_sink):
    b, ln, _ = q.shape
    step = min(BAND_STEP, ln)
    per_step = step // BAND_TILE
    cur = pl.BlockSpec((b, step, GROUP), lambda r, i: (0, i, r))
    prev = pl.BlockSpec((b, BAND_TILE, GROUP), lambda r, i: (0, jnp.maximum(i * per_step - 1, 0), r))
    o, lse = pl.pallas_call(
        functools.partial(_band_kernel, use_sink=use_sink),
        grid=(rate, ln // step),
        in_specs=[pl.BlockSpec(memory_space=pltpu.SMEM), cur, prev, cur, prev, cur,
                  pl.BlockSpec((HEADS, 1, BAND_TILE, 2 * BAND_TILE), lambda r, i: (0, 0, 0, 0))],
        out_specs=[cur, pl.BlockSpec((b, step, LANES), lambda r, i: (0, i, r))],
        out_shape=[jax.ShapeDtypeStruct((b, ln, rate * GROUP), BF16), jax.ShapeDtypeStruct((b, ln, rate * LANES), F32)],
        scratch_shapes=[pltpu.VMEM((2, BAND_TILE, HEADS * 2 * BAND_TILE), BF16)],
        compiler_params=pltpu.CompilerParams(dimension_semantics=("arbitrary",) * 2),
        name=f"banded_attention_r{rate}",
    )(sink, q, k, k, v, v, bias)
    return o, lse


def _flash_reset(m_ref, acc_ref):
    m_ref[...] = jnp.full(m_ref.shape, NEG_INF, F32)
    acc_ref[...] = jnp.zeros(acc_ref.shape, F32)


def _flash_update(n, s, v_t, m_ref, acc_ref, shift=None):
    m_old = m_ref[n]
    if shift is None:
        m_new = jnp.maximum(m_old, jnp.max(s, axis=0, keepdims=True))
        p = jnp.exp2(s - m_new)
    else:
        m_new = jnp.maximum(m_old, jnp.max(s, axis=0, keepdims=True) + shift)
        p = jnp.exp2(s - (m_new - shift))
    alpha = jnp.exp2(m_old - m_new)
    acc_ref[n] = alpha * acc_ref[n] + jnp.dot(v_t, p.astype(BF16), preferred_element_type=F32)
    m_ref[n] = m_new


def _flash_result(n, acc_ref):
    return acc_ref[n, 0:HEAD_DIM, :] / acc_ref[n, HEAD_DIM:HEAD_DIM + 1, :]


def _staggered(n_items, scores, update, ahead=MXU_LOOKAHEAD):
    pending = {n: scores(n) for n in range(min(ahead, n_items))}
    for n in range(n_items):
        if n + ahead < n_items:
            pending[n + ahead] = scores(n + ahead)
        update(n, pending.pop(n))


def _pipelined_tiles(first, n_tiles, n_chains, group, load_tile, scores, update, next_ref, left_by_previous=None,
                     last_of_sweep=False):
    ahead = next_ref.shape[0]
    n_items = group * n_chains
    assert ahead <= n_chains

    def body(trip, _, issue_next=True):
        base = first + trip * group
        tiles, pending = {}, {}
        for n in range(n_items):
            cur = next_ref[n] if n < ahead else pending.pop(n)
            if n + ahead < n_items or issue_next:
                g, c = divmod(n + ahead, n_chains)
                if g not in tiles:
                    tiles[g] = load_tile(base + g)
                new = scores(tiles[g], base + g, c)
                if n + ahead < n_items:
                    pending[n + ahead] = new
                else:
                    next_ref[n + ahead - n_items] = new
            update(base + n // n_chains, n % n_chains, cur)

    if left_by_previous is None:
        first_tile = load_tile(first)
        for n in range(ahead):
            next_ref[n] = scores(first_tile, first, n)
    else:
        for n in range(ahead):
            next_ref[n] = next_ref[n] + left_by_previous(first, n)
    n_trips = (n_tiles + group - 1) // group
    if last_of_sweep:
        lax.fori_loop(0, n_trips - 1, body, None)
        body(n_trips - 1, None, issue_next=False)
    else:
        lax.fori_loop(0, n_trips, body, None)


def _flash_scratch(chains, ahead):
    return [pltpu.VMEM((chains, 1, QT), F32), pltpu.VMEM((chains, V_ROWS, QT), F32),
            pltpu.VMEM((ahead, KT, QT), F32)]


N_NEAR = -(-(FAR_DIST + KT - 1) // QT)
N_BIAS_TILES = N_NEAR + 2
DIFF_TILE_GROUP = 2
SLC_TILE_GROUP = 2
FAR_TILE_GROUP = 4


def _bias_tile_index(i, j):
    return jnp.where(j > i, N_NEAR + 1, jnp.minimum(i - j, N_NEAR))


def _whole_far_groups(i, group):
    return jnp.maximum(i - (N_NEAR - 1), 0) // group * group


def _diff_kernel(far_ref, q_ref, k_ref, v_ref, bias_ref, lam_ref, subln_ref, o_ref, qz_ref, m_ref, acc_ref, next_ref,
                 ot_ref, *, lambda_init):
    i = pl.program_id(1)
    q = q_ref[0]
    row = lax.broadcasted_iota(jnp.int32, (GROUP, QT), 0) // DIFF_QK_DIM
    for n in range(2 * HEADS):
        qz_ref[n] = jnp.where(row == n, q, jnp.zeros_like(q))
    _flash_reset(m_ref, acc_ref)

    def load_tile(j):
        return k_ref[0, pl.ds(pl.multiple_of(jnp.minimum(j, i) * KT, KT), KT), :]

    def values(j, n):
        h = n // 2
        return v_ref[0, jnp.minimum(j, i), h * V_ROWS:(h + 1) * V_ROWS, :]

    n_far = _whole_far_groups(i, FAR_TILE_GROUP)
    _pipelined_tiles(0, n_far, 2 * HEADS, FAR_TILE_GROUP, load_tile,
                     lambda k, j, n: jnp.dot(k, qz_ref[n], preferred_element_type=F32),
                     lambda j, n, s: _flash_update(n, s, values(j, n), m_ref, acc_ref, shift=far_ref[n // 2]),
                     next_ref)

    def scores(k, j, n):
        return jnp.dot(k, qz_ref[n], preferred_element_type=F32) + bias_ref[n // 2, _bias_tile_index(i, j)]

    _pipelined_tiles(n_far, i + 1 - n_far, 2 * HEADS, DIFF_TILE_GROUP, load_tile, scores,
                     lambda j, n, s: _flash_update(n, s, values(j, n), m_ref, acc_ref), next_ref,
                     left_by_previous=lambda j, n: bias_ref[n // 2, _bias_tile_index(i, j)], last_of_sweep=True)

    lam_p = lam_ref[...]
    lam = (jnp.exp(jnp.sum(lam_p[0:1] * lam_p[1:2], axis=1, keepdims=True))
           - jnp.exp(jnp.sum(lam_p[2:3] * lam_p[3:4], axis=1, keepdims=True)) + lambda_init)
    for h in range(HEADS):
        o = _flash_result(2 * h, acc_ref) - lam * _flash_result(2 * h + 1, acc_ref)
        msq = jnp.mean(o * o, axis=0, keepdims=True)
        ot_ref[h * HEAD_DIM:(h + 1) * HEAD_DIM, :] = (o * lax.rsqrt(msq + RMS_EPS) * subln_ref[...]
                                                      * (1.0 - lambda_init))
    o_ref[0] = ot_ref[...].T.astype(o_ref.dtype)


def _diff_attention(far, q_t, k, v_t, bias, lam_p, subln, lambda_init):
    b, s, _ = k.shape
    nq = s // QT
    nkv = s // KT
    v4 = v_t.reshape(b, nkv, HEADS * V_ROWS, KT)
    return pl.pallas_call(
        functools.partial(_diff_kernel, lambda_init=lambda_init),
        grid=(b, nq),
        in_specs=[pl.BlockSpec(memory_space=pltpu.SMEM),
                  pl.BlockSpec((1, GROUP, QT), lambda bb, i: (bb * nq + i, 0, 0)),
                  pl.BlockSpec((1, s, GROUP), lambda bb, i: (bb, 0, 0)),
                  pl.BlockSpec((1, nkv, HEADS * V_ROWS, KT), lambda bb, i: (bb, 0, 0, 0)),
                  pl.BlockSpec((HEADS, N_BIAS_TILES, KT, QT), lambda bb, i: (0, 0, 0, 0)),
                  pl.BlockSpec((4, DIFF_QK_DIM), lambda bb, i: (0, 0)),
                  pl.BlockSpec((HEAD_DIM, 1), lambda bb, i: (0, 0))],
        out_specs=pl.BlockSpec((1, QT, GROUP), lambda bb, i: (bb, i, 0)),
        out_shape=jax.ShapeDtypeStruct((b, s, GROUP), BF16),
        scratch_shapes=[pltpu.VMEM((2 * HEADS, GROUP, QT), BF16)] + _flash_scratch(2 * HEADS, MXU_LOOKAHEAD)
        + [pltpu.VMEM((GROUP, QT), F32)],
        compiler_params=pltpu.CompilerParams(dimension_semantics=("arbitrary", "arbitrary"),
                                             vmem_limit_bytes=VMEM_LIMIT),
        name="diff_attention",
    )(far, q_t, k, v4, bias, lam_p, subln)


def _compress_kernel(ch_ref, ptop_ref, pbot_ref, w1t_ref, w1b_ref, b1_ref, w2k_ref, b2k_ref, w2v_ref, b2v_ref,
                     gk_ref, kc_ref, vct_ref):
    ch = ch_ref[0]
    n_c = ch.shape[0]
    u = jnp.dot((ch + ptop_ref[...]).astype(BF16), w1t_ref[...], preferred_element_type=F32)
    v = jnp.dot((ch + pbot_ref[...]).astype(BF16), w1b_ref[...], preferred_element_type=F32)
    v_next = pltpu.roll(v, n_c - 1, 0)
    hid = jax.nn.gelu(u + v_next + b1_ref[...])
    hk = hid[:, :CMP_HIDDEN].astype(BF16)
    hv = hid[:, CMP_HIDDEN:].astype(BF16)
    kc = jnp.dot(hk, w2k_ref[...], preferred_element_type=F32) + b2k_ref[...]
    msq = jnp.mean(kc * kc, axis=-1, keepdims=True)
    kc_ref[0] = (kc * lax.rsqrt(msq + RMS_EPS) * gk_ref[...]).astype(kc_ref.dtype)
    vct = lax.dot_general(w2v_ref[...], hv, NT_DIMS, preferred_element_type=F32) + b2v_ref[...]
    vct_ref[0] = vct.astype(vct_ref.dtype)


def _compress(chunks, ptop, pbot, w1t, w1b, b1, w2k, b2k, w2v, b2v, gk):
    b, n_c, width = chunks.shape
    const = lambda a: pl.BlockSpec(a.shape, lambda bb: (0,) * a.ndim)
    params = (ptop, pbot, w1t, w1b, b1, w2k, b2k, w2v, b2v, gk)
    return pl.pallas_call(
        _compress_kernel,
        grid=(b,),
        in_specs=[pl.BlockSpec((1, n_c, width), lambda bb: (bb, 0, 0))] + [const(a) for a in params],
        out_specs=[pl.BlockSpec((1, n_c, HEAD_DIM), lambda bb: (bb, 0, 0)),
                   pl.BlockSpec((1, HEAD_DIM, n_c), lambda bb: (bb, 0, 0))],
        out_shape=[jax.ShapeDtypeStruct((b, n_c, HEAD_DIM), BF16), jax.ShapeDtypeStruct((b, HEAD_DIM, n_c), BF16)],
        compiler_params=pltpu.CompilerParams(dimension_semantics=("arbitrary",), vmem_limit_bytes=VMEM_LIMIT),
        name="nsa_compress",
    )(chunks, *params)


def _cmp_attn_kernel(q_ref, kc_ref, vct_ref, bias_ref, o_ref, sel_ref, p_ref, *, n_sel):
    i = pl.program_id(0)
    n_tiles = pl.num_programs(0)
    for part in range(1, CMP_PARTS + 1):
        @pl.when((i * CMP_PARTS >= (part - 1) * n_tiles) & (i * CMP_PARTS < part * n_tiles))
        def _(part=part):
            for bb in range(q_ref.shape[0]):
                one = pl.ds(bb, 1)
                _cmp_attn_body(i, kc_ref.shape[1] * part // CMP_PARTS, q_ref.at[bb], kc_ref.at[one], vct_ref.at[one],
                               bias_ref, o_ref.at[bb], sel_ref.at[bb], p_ref, n_sel)


def _cmp_attn_body(i, n_c, q_ref, kc_ref, vct_ref, bias_ref, o_ref, sel_ref, p_ref, n_sel):
    kc = kc_ref[0, 0:n_c, :]
    vct = vct_ref[0, :, 0:n_c]
    n_blk = n_c * CMP_STRIDE // SLC_BLOCK
    probs = []

    def scores(h):
        return (jnp.dot(kc, q_ref[0, h * HEAD_DIM:(h + 1) * HEAD_DIM, :], preferred_element_type=F32)
                + bias_ref[h, 0, 0:n_c, :])

    def update(h, s):
        m = jnp.maximum(jnp.max(s, axis=0, keepdims=True), 0.5 * NEG_INF)
        p = jnp.exp2(s - m)
        den = jnp.sum(p, axis=0, keepdims=True)
        p = p * (1.0 / jnp.maximum(den, TINY))
        o_ref[0, h * HEAD_DIM:(h + 1) * HEAD_DIM, :] = jnp.dot(vct, p.astype(BF16),
                                                               preferred_element_type=F32).astype(o_ref.dtype)
        probs.append(p)

    _staggered(HEADS, scores, update)
    psum = (probs[0] + probs[1]) + (probs[2] + probs[3])
    per_blk = SLC_BLOCK // CMP_STRIDE
    halves = []
    for half in range(QT // LANES):
        p_ref[half, 0:8, :] = jnp.zeros((8, LANES), F32)
        p_ref[half, 8:8 + n_c, :] = psum[:, half * LANES:(half + 1) * LANES]
        p_ref[half, 8 + n_c:16 + n_c, :] = jnp.zeros((8, LANES), F32)
        acc = p_ref[half, pl.ds(7, n_blk, stride=per_blk), :]
        for t in range(per_blk):
            acc = acc + p_ref[half, pl.ds(8 + t, n_blk, stride=per_blk), :]
        halves.append(acc)
    imp = jnp.concatenate(halves, axis=1)
    blk = lax.broadcasted_iota(jnp.int32, (n_blk, QT), 0)
    cur = (i * QT + lax.broadcasted_iota(jnp.int32, (n_blk, QT), 1)) // SLC_BLOCK
    forced = (blk == 0) | (blk == cur) | (blk == cur - 1)
    val = jnp.where(forced, FORCE_SELECT, jnp.where(blk <= cur, imp, NEG_INF))
    sel = jnp.zeros((n_blk, QT), jnp.bool_)
    for _ in range(n_sel):
        top = jnp.max(val, axis=0, keepdims=True)
        idx = jnp.min(jnp.where(val == top, blk, n_blk), axis=0, keepdims=True)
        hit = blk == idx
        sel = sel | hit
        val = jnp.where(hit, -3.0e38, val)
    sel_ref[0, 0:n_blk, :] = jnp.where(sel, 1.0, 0.0).astype(sel_ref.dtype)
    if n_blk < sel_ref.shape[1]:
        sel_ref[0, n_blk:, :] = jnp.zeros((sel_ref.shape[1] - n_blk, QT), sel_ref.dtype)


def _cmp_attention(q_t, kc, vct, bias, b):
    nt = q_t.shape[0]
    nq = nt // b
    n_c = kc.shape[1]
    n_blk = nq * QT // SLC_BLOCK
    o_cmp, sel = pl.pallas_call(
        functools.partial(_cmp_attn_kernel, n_sel=min(SLC_TOPK, n_blk)),
        grid=(nq,),
        in_specs=[pl.BlockSpec((b, 1, GROUP, QT), lambda i: (0, i, 0, 0)),
                  pl.BlockSpec((b, n_c, HEAD_DIM), lambda i: (0, 0, 0)),
                  pl.BlockSpec((b, HEAD_DIM, n_c), lambda i: (0, 0, 0)),
                  pl.BlockSpec((HEADS, 1, n_c, QT), lambda i: (0, 0, 0, i))],
        out_specs=[pl.BlockSpec((b, 1, GROUP, QT), lambda i: (0, i, 0, 0)),
                   pl.BlockSpec((b, 1, n_blk, QT), lambda i: (0, i, 0, 0))],
        out_shape=[jax.ShapeDtypeStruct((b, nq, GROUP, QT), BF16), jax.ShapeDtypeStruct((b, nq, n_blk, QT), BF16)],
        scratch_shapes=[pltpu.VMEM((QT // LANES, n_c + 16, LANES), F32)],
        compiler_params=pltpu.CompilerParams(dimension_semantics=("arbitrary",), vmem_limit_bytes=VMEM_LIMIT),
        name="nsa_compressed_attention",
    )(q_t.reshape(b, nq, GROUP, QT), kc, vct, bias)
    return o_cmp.reshape(nt, GROUP, QT), sel.reshape(nt, n_blk, QT)


SEL_REP = 8
N_WIN = -(-(NSA_WINDOW - 1 + KT - 1) // QT)


def _slc_win_kernel(far_ref, q_ref, ksw_ref, vs_ref, vw_ref, sel_ref, rep_ref, ocmp_ref, g_ref, bslc_ref, bwin_ref,
                    o_ref, qz_ref, m_ref, acc_ref, next_ref, ot_ref, mask_ref):
    i = pl.program_id(1)
    sel8 = jnp.dot(rep_ref[...], sel_ref[0], preferred_element_type=F32)
    mask_ref[...] = (sel8 - 1.0) * (-NEG_INF)
    blocks_per_tile = KT // SLC_BLOCK
    mrows = blocks_per_tile * SEL_REP
    zeros = jnp.zeros((HEAD_DIM, QT), BF16)
    for h in range(HEADS):
        qh = q_ref[0, h * HEAD_DIM:(h + 1) * HEAD_DIM, :]
        qz_ref[h] = jnp.concatenate([qh, zeros], axis=0)
        qz_ref[HEADS + h] = jnp.concatenate([zeros, qh], axis=0)
    _flash_reset(m_ref, acc_ref)

    def load_keys(j):
        return ksw_ref[0, pl.ds(pl.multiple_of(j * KT, KT), KT), :]

    def load_tile(j):
        j = jnp.minimum(j, i)
        m8 = mask_ref[pl.ds(pl.multiple_of(j * mrows, mrows), mrows), :]
        mask = jnp.broadcast_to(m8.reshape(blocks_per_tile, 1, SEL_REP, QT),
                                (blocks_per_tile, SLC_BLOCK // SEL_REP, SEL_REP, QT)).reshape(KT, QT)
        return load_keys(j), mask

    def values(j):
        return vs_ref[0, jnp.minimum(j, i)]

    n_far = _whole_far_groups(i, FAR_TILE_GROUP)
    _pipelined_tiles(0, n_far, HEADS, FAR_TILE_GROUP, load_tile,
                     lambda tile, j, h: jnp.dot(tile[0], qz_ref[h], preferred_element_type=F32) + tile[1],
                     lambda j, h, s: _flash_update(h, s, values(j), m_ref, acc_ref, shift=far_ref[h]), next_ref)

    def slc_scores(tile, j, h):
        k, mask = tile
        return jnp.dot(k, qz_ref[h], preferred_element_type=F32) + mask + bslc_ref[h, _bias_tile_index(i, j)]

    _pipelined_tiles(n_far, i + 1 - n_far, HEADS, SLC_TILE_GROUP, load_tile, slc_scores,
                     lambda j, h, s: _flash_update(h, s, values(j), m_ref, acc_ref), next_ref,
                     left_by_previous=lambda j, h: bslc_ref[h, _bias_tile_index(i, j)], last_of_sweep=True)

    def win_tile(n):
        d = N_WIN - 1 - n // HEADS
        return d, n % HEADS, jnp.maximum(i - d, 0)

    def win_scores(n):
        d, h, j = win_tile(n)
        missing = jnp.where(i < d, NEG_INF, 0.0).astype(F32)
        return jnp.dot(load_keys(j), qz_ref[HEADS + h], preferred_element_type=F32) + (bwin_ref[h, d] + missing)

    def win_update(n, s):
        _, h, j = win_tile(n)
        _flash_update(HEADS + h, s, vw_ref[0, j], m_ref, acc_ref)

    _staggered(N_WIN * HEADS, win_scores, win_update)

    for h in range(HEADS):
        g = jax.nn.sigmoid(g_ref[0, 3 * h:3 * h + 3, :])
        ot_ref[h * HEAD_DIM:(h + 1) * HEAD_DIM, :] = (g[0:1] * ocmp_ref[0, h * HEAD_DIM:(h + 1) * HEAD_DIM, :]
                                                      + g[1:2] * _flash_result(h, acc_ref)
                                                      + g[2:3] * _flash_result(HEADS + h, acc_ref))
    o_ref[0] = ot_ref[...].T.astype(o_ref.dtype)


def _slc_win_attention(far, q_t, ksw, vs_t, vw_t, sel, rep, ocmp, g_t, bslc, bwin):
    b, s, _ = ksw.shape
    nq = s // QT
    nkv = s // KT
    n_blk = s // SLC_BLOCK
    tile = lambda height: pl.BlockSpec((1, height, QT), lambda bb, i: (bb * nq + i, 0, 0))
    whole = lambda a: pl.BlockSpec(a.shape, lambda bb, i: (0,) * a.ndim)
    return pl.pallas_call(
        _slc_win_kernel,
        grid=(b, nq),
        in_specs=[pl.BlockSpec(memory_space=pltpu.SMEM), tile(GROUP),
                  pl.BlockSpec((1, s, 2 * HEAD_DIM), lambda bb, i: (bb, 0, 0)),
                  pl.BlockSpec((1, nkv, V_ROWS, KT), lambda bb, i: (bb, 0, 0, 0)),
                  pl.BlockSpec((1, nkv, V_ROWS, KT), lambda bb, i: (bb, 0, 0, 0)),
                  tile(n_blk), whole(rep), tile(GROUP), tile(GATE_ROWS), whole(bslc), whole(bwin)],
        out_specs=pl.BlockSpec((1, QT, GROUP), lambda bb, i: (bb, i, 0)),
        out_shape=jax.ShapeDtypeStruct((b, s, GROUP), BF16),
        scratch_shapes=[pltpu.VMEM((2 * HEADS, 2 * HEAD_DIM, QT), BF16)] + _flash_scratch(2 * HEADS, MXU_LOOKAHEAD)
        + [pltpu.VMEM((GROUP, QT), F32), pltpu.VMEM((n_blk * SEL_REP, QT), F32)],
        compiler_params=pltpu.CompilerParams(dimension_semantics=("arbitrary", "arbitrary"),
                                             vmem_limit_bytes=VMEM_LIMIT),
        name="nsa_selected_window_attention",
    )(far, q_t, ksw, vs_t.reshape(b, nkv, V_ROWS, KT), vw_t.reshape(b, nkv, V_ROWS, KT), sel, rep, ocmp, g_t,
      bslc, bwin)


def _out_kernel(x_ref, a0_ref, a1_ref, a2_ref, l0_ref, l1_ref, l2_ref, ob_ref, oc_ref, od_ref, gate_ref, e_ref,
                w_ref, o_ref, unfold_ref):
    rows = x_ref.shape[0]

    def unfolded(ref, rate):
        width = ref.shape[1] // rate
        for rho in range(rate):
            for part in range(width // LANES):
                c0 = rho * width + part * LANES
                unfold_ref[part, pl.ds(rho, rows // rate, stride=rate), :] = ref[:, c0:c0 + LANES].astype(F32)
        return jnp.concatenate([unfold_ref[part] for part in range(width // LANES)], axis=1)

    a0, l0 = a0_ref[...], l0_ref[...]
    a1, l1 = unfolded(a1_ref, FOLD_RATES[0]), unfolded(l1_ref, FOLD_RATES[0])
    a2, l2 = unfolded(a2_ref, FOLD_RATES[1]), unfolded(l2_ref, FOLD_RATES[1])
    mx = jnp.maximum(jnp.maximum(l0, l1), l2)
    e0, e1, e2 = jnp.exp2(l0 - mx), jnp.exp2(l1 - mx), jnp.exp2(l2 - mx)
    den = e0 + e1 + e2

    def per_head_lanes(w):
        hi = w.astype(BF16)
        lo = (w - hi.astype(F32)).astype(BF16)
        return (jnp.dot(hi, e_ref[...], preferred_element_type=F32)
                + jnp.dot(lo, e_ref[...], preferred_element_type=F32))

    o_a = per_head_lanes(e0 / den) * a0 + per_head_lanes(e1 / den) * a1 + per_head_lanes(e2 / den) * a2
    y = jnp.concatenate([o_a, ob_ref[...].astype(F32), oc_ref[...].astype(F32), od_ref[...].astype(F32)], axis=1)
    g = gate_ref[...].astype(F32)
    y = y * (g * jax.nn.sigmoid(g))
    o_ref[...] = x_ref[...] + jnp.dot(y.astype(BF16), w_ref[...], preferred_element_type=F32)


def _out_projection(x2, a_outs, a_lses, o_b, o_c, o_d, gate, w_out):
    m, d = x2.shape
    rowblk = lambda width: pl.BlockSpec((PROJ_ROWS, width), lambda i: (i, 0))
    folded = lambda width, rate: pl.BlockSpec((PROJ_ROWS // rate, rate * width), lambda i: (i, 0))
    head_of_lane = np.arange(GROUP) // HEAD_DIM
    expand = jnp.asarray((np.arange(LANES)[:, None] == head_of_lane[None, :]).astype(np.float32), BF16)
    return pl.pallas_call(
        _out_kernel,
        grid=(m // PROJ_ROWS,),
        in_specs=[rowblk(d)] + [folded(GROUP, rate) for rate in (1,) + FOLD_RATES]
        + [folded(LANES, rate) for rate in (1,) + FOLD_RATES] + [rowblk(GROUP)] * 3
        + [rowblk(N_MIXERS * GROUP), pl.BlockSpec((LANES, GROUP), lambda i: (0, 0)),
           pl.BlockSpec((N_MIXERS * GROUP, d), lambda i: (0, 0))],
        out_specs=rowblk(d),
        out_shape=jax.ShapeDtypeStruct((m, d), F32),
        scratch_shapes=[pltpu.VMEM((GROUP // LANES, PROJ_ROWS, LANES), F32)],
        compiler_params=pltpu.CompilerParams(dimension_semantics=("arbitrary",), vmem_limit_bytes=VMEM_LIMIT),
        name="out_projection",
    )(x2, *a_outs, *a_lses, o_b, o_c, o_d, gate, expand, w_out)


def _block_diag_mean(group):
    idx = np.arange(GROUP) // group
    return jnp.asarray((idx[:, None] == idx[None, :]).astype(np.float32) / group, BF16)


def _layer_weights(w_in, qk_gain, qk_gain_diff):
    d = w_in.shape[0]
    sizes = (GROUP,) * 3 + (GROUP, GROUP // 2, GROUP // 2) + (GROUP,) * 3 + (GROUP,) + (HEAD_DIM,) * 6 \
        + (HEADS * 3, N_MIXERS * GROUP)
    offs = np.concatenate([[0], np.cumsum(sizes)])
    col = lambda n: w_in[:, offs[n]:offs[n + 1]]
    (a_q, a_k, a_v, b_q, b_k, b_v, c_q, c_k, c_v, d_q, d_kc, d_vc, d_ks, d_vs, d_kw, d_vw, d_g, gate) = \
        [col(n) for n in range(18)]
    rep_kv = lambda w: jnp.repeat(w.reshape(d, 2, HEAD_DIM), 2, axis=1).reshape(d, GROUP)
    wrm = jnp.concatenate([a_q, a_k, a_v, b_q, rep_kv(b_k), rep_kv(b_v), c_k, d_kc, d_vc, d_ks, d_kw, gate], axis=1)
    wt = jnp.concatenate([c_q, c_v, d_q, d_vs, d_vw, d_g, jnp.zeros((d, GATE_ROWS - HEADS * 3), w_in.dtype)], axis=1).T
    g = qk_gain
    ones = lambda n: jnp.ones((n,), F32)
    tile4 = lambda v: jnp.tile(v, HEADS)
    scale = HEAD_DIM ** -0.5 * LOG2E
    grm = jnp.concatenate([tile4(g[0]) * scale, tile4(g[1]), ones(GROUP), tile4(g[2]) * scale, tile4(g[3]),
                           ones(GROUP), jnp.tile(qk_gain_diff[1], 2 * HEADS), ones(2 * HEAD_DIM), g[6], g[7],
                           ones(N_MIXERS * GROUP)])
    gt = jnp.concatenate([jnp.tile(qk_gain_diff[0], 2 * HEADS) * (DIFF_QK_DIM ** -0.5 * LOG2E), ones(GROUP),
                          tile4(g[4]) * scale, ones(2 * HEAD_DIM + GATE_ROWS)])
    return wrm.astype(BF16), wt.astype(BF16), grm.reshape(1, -1), gt.reshape(-1, 1)


def _compress_weights(cmp_pos, cmp_w1, cmp_b1, cmp_w2, cmp_b2):
    half = CMP_LEN // 2
    pos = jnp.concatenate([cmp_pos[0], cmp_pos[1]], axis=-1)
    ptop = pos[:half].reshape(1, -1)
    pbot = pos[half:].reshape(1, -1)
    w1 = cmp_w1.reshape(2, CMP_LEN, HEAD_DIM, CMP_HIDDEN)
    zeros = jnp.zeros_like(w1[0])
    w1cat = jnp.concatenate([jnp.concatenate([w1[0], zeros], axis=-1),
                             jnp.concatenate([zeros, w1[1]], axis=-1)], axis=1)
    w1t = w1cat[:half].reshape(half * 2 * HEAD_DIM, 2 * CMP_HIDDEN).astype(BF16)
    w1b = w1cat[half:].reshape(half * 2 * HEAD_DIM, 2 * CMP_HIDDEN).astype(BF16)
    b1 = jnp.concatenate([cmp_b1[0], cmp_b1[1]]).reshape(1, -1)
    return (ptop, pbot, w1t, w1b, b1, cmp_w2[0].astype(BF16), cmp_b2[0].reshape(1, -1),
            cmp_w2[1].T.astype(BF16), cmp_b2[1].reshape(-1, 1))


def kernel(x, rel_bias_table, norm_w, w_in, w_out, qk_gain, qk_gain_diff, attn_sinks, diff_lambda, diff_subln,
           cmp_pos, cmp_w1, cmp_b1, cmp_w2, cmp_b2):
    b, s, d = x.shape
    depth = w_in.shape[0]
    n_c = s // CMP_STRIDE
    n_blk = s // SLC_BLOCK
    assert s % (BAND_TILE * DILATED_CONFIGS[-1][1]) == 0 and s % PROJ_ROWS == 0 and d == N_MIXERS * GROUP

    table = rel_bias_table.astype(F32)
    band_bias = [_build_bias(table, head0=0, n_d=1, rows=BAND_TILE, cols=2 * BAND_TILE, base0=BAND_TILE, dstep=0,
                             rs=1, cs=-1, dscale=rate, max_dist=window // rate, scale=LOG2E)
                 for window, rate in DILATED_CONFIGS]
    swa_bias = _build_bias(table, head0=HEADS, n_d=1, rows=BAND_TILE, cols=2 * BAND_TILE, base0=BAND_TILE, dstep=0,
                           rs=1, cs=-1, max_dist=SWA_WINDOW - 1, scale=LOG2E)
    flash_tiles = dict(rows=KT, cols=QT, base0=0, dstep=QT, rs=-1, cs=1, scale=LOG2E)
    diff_bias = _build_bias(table, head0=2 * HEADS, n_d=N_BIAS_TILES, d_valid=N_NEAR + 1, **flash_tiles)
    slc_bias = _build_bias(table, head0=3 * HEADS, n_d=N_BIAS_TILES, d_valid=N_NEAR + 1, **flash_tiles)
    win_bias = _build_bias(table, head0=3 * HEADS, n_d=N_WIN, max_dist=NSA_WINDOW - 1, **flash_tiles)
    far_bias = table[NUM_BUCKETS - 1] * LOG2E
    cmp_bias = _build_bias(table, head0=3 * HEADS, n_d=1, rows=n_c, cols=s, base0=-(CMP_LEN - 1), dstep=0,
                           rs=-CMP_STRIDE, cs=1, r_valid=n_c - 1, col_tile=2 * QT, scale=LOG2E)
    e64, e32 = _block_diag_mean(HEAD_DIM), _block_diag_mean(DIFF_QK_DIM)
    rep_idx = np.arange(n_blk * SEL_REP) // SEL_REP
    rep = jnp.asarray((rep_idx[:, None] == np.arange(n_blk)[None, :]).astype(np.float32), BF16)
    no_sink = jnp.zeros((HEADS,), F32)

    x2 = x.reshape(b * s, d)
    w_in_bf16 = w_in.astype(BF16)
    for layer in range(depth):
        wrm, wt, grm, gt = _layer_weights(w_in_bf16[layer], qk_gain[layer], qk_gain_diff[layer])
        (a_q, a_k, a_v, b_q, b_k, b_v, c_k, kvc, ksw, gate, a_q4, a_k4, a_v4, a_q16, a_k16, a_v16,
         c_qt, c_vt, d_qt, d_vst, d_vwt, d_gt) = _project(x2, norm_w[layer].reshape(1, d), wrm, wt, grm, gt, e64, e32)
        seq = lambda t: t.reshape(b, s, t.shape[-1])
        per_batch = lambda t: t.reshape(b, t.shape[0] // b, t.shape[1])
        flat = lambda t: t.reshape(b * t.shape[1], t.shape[2])
        a_in = ((a_q, a_k, a_v), (a_q4, a_k4, a_v4), (a_q16, a_k16, a_v16))
        a_res = [_banded(*map(per_batch, a_in[n]), band_bias[n], no_sink, rate, False)
                 for n, (_, rate) in enumerate(DILATED_CONFIGS)]
        o_b, _ = _banded(seq(b_q), seq(b_k), seq(b_v), swa_bias, attn_sinks[layer].astype(F32) * LOG2E, 1, True)
        lambda_init = 0.8 - 0.6 * math.exp(-0.3 * layer)
        o_c = _diff_attention(far_bias[2 * HEADS:3 * HEADS], c_qt, seq(c_k), c_vt, diff_bias, diff_lambda[layer].astype(F32),
                              diff_subln[layer].reshape(HEAD_DIM, 1).astype(F32), lambda_init)
        cw = _compress_weights(cmp_pos[layer], cmp_w1[layer], cmp_b1[layer], cmp_w2[layer], cmp_b2[layer])
        kc, vct = _compress(kvc.reshape(b, n_c, CMP_STRIDE * 2 * HEAD_DIM), *cw, qk_gain[layer, 5].reshape(1, -1))
        o_cmp, sel = _cmp_attention(d_qt, kc, vct, cmp_bias, b)
        o_d = _slc_win_attention(far_bias[3 * HEADS:4 * HEADS], d_qt, seq(ksw), d_vst, d_vwt, sel, rep, o_cmp, d_gt,
                                 slc_bias, win_bias)
        x2 = _out_projection(x2, [flat(r[0]) for r in a_res], [flat(r[1]) for r in a_res], o_b.reshape(b * s, GROUP),
                             o_c.reshape(b * s, GROUP), o_d.reshape(b * s, GROUP), gate, w_out[layer].astype(BF16))
    return x2.reshape(b, s, d)
```

```python
import functools
import itertools
import math

import numpy as np
import jax
import jax.numpy as jnp
from jax import lax
from jax.experimental import pallas as pl
from jax.experimental.pallas import tpu as pltpu

F32 = jnp.float32
BF16 = jnp.bfloat16

HEAD_DIM = 64
HEADS = 4
GROUP = HEADS * HEAD_DIM
N_MIXERS = 4
NUM_BUCKETS = 32
REL_MAX_DIST = 2048
DILATED_CONFIGS = ((128, 1), (512, 4), (2048, 16))
FOLD_RATES = tuple(rate for _, rate in DILATED_CONFIGS if rate > 1)
SWA_WINDOW = 128
DIFF_QK_DIM = HEAD_DIM // 2
CMP_LEN = 32
CMP_STRIDE = 16
CMP_HIDDEN = 256
SLC_BLOCK = 64
SLC_TOPK = 16
CMP_PARTS = 4
NSA_WINDOW = 512
RMS_EPS = 1e-6
NEG_INF = -1e30
FORCE_SELECT = 1e9
TINY = 1e-30
LOG2E = math.log2(math.e)

PROJ_ROWS = 512
PROJ_LOOKAHEAD = 1
BAND_TILE = 128
BAND_STEP = 1024
BAND_LOOKAHEAD = 2
BIAS_ROW_CHUNK = 64
LANES = 128
QT = 256
KT = 256
VMEM_LIMIT = 56 * 1024 * 1024
MXU_LOOKAHEAD = 4
V_ROWS = HEAD_DIM + 16

NT_DIMS = (((1,), (1,)), ((), ()))


def _t5_thresholds():
    n = np.arange(0, 4 * REL_MAX_DIST)
    max_exact = NUM_BUCKETS // 2
    nf = np.maximum(n, 1).astype(np.float32)
    large = max_exact + (np.log(nf / np.float32(max_exact)) / np.float32(math.log(REL_MAX_DIST / max_exact))
                         * np.float32(NUM_BUCKETS - max_exact)).astype(np.int32)
    bucket = np.where(n < max_exact, n, np.minimum(large, NUM_BUCKETS - 1))
    return [int(np.argmax(bucket >= b)) for b in range(NUM_BUCKETS)]


T5_THRESHOLDS = _t5_thresholds()
FAR_DIST = T5_THRESHOLDS[-1]


def _bias_kernel(tbl_ref, out_ref, *, head0, base0, dstep, rs, cs, dscale, max_dist, r_valid, d_valid, col_tile,
                 scale):
    h = pl.program_id(0)
    n_d, rows, cols = out_ref.shape[1:]
    chunk = BIAS_ROW_CHUNK if rows % BIAS_ROW_CHUNK == 0 else rows
    bucket_of = lambda n: max(b for b in range(NUM_BUCKETS) if T5_THRESHOLDS[b] <= n)
    for d, c0, r0 in itertools.product(range(n_d), range(0, cols, col_tile), range(0, rows, chunk)):
        out = out_ref.at[0, d, r0:r0 + chunk, c0:c0 + col_tile]
        origin = base0 + d * dstep + r0 * rs + c0 * cs
        corners = [origin + dr * rs + dc * cs for dr in (0, chunk - 1) for dc in (0, col_tile - 1)]
        lo, hi = min(corners), max(corners)
        rows_valid = min(chunk, r_valid - r0)
        if hi < 0 or lo > max_dist or d >= d_valid or rows_valid <= 0:
            out[...] = jnp.full((chunk, col_tile), NEG_INF, F32)
            continue
        r = lax.broadcasted_iota(jnp.int32, (chunk, col_tile), 0)
        c = lax.broadcasted_iota(jnp.int32, (chunk, col_tile), 1)
        dist = origin + r * rs + c * cs
        first, last = bucket_of(max(lo, 0) * dscale), bucket_of(min(hi, max_dist) * dscale)
        val = jnp.full((chunk, col_tile), tbl_ref[first, head0 + h] * scale, F32)
        for b in range(first + 1, last + 1):
            val = jnp.where(dist * dscale >= T5_THRESHOLDS[b], tbl_ref[b, head0 + h] * scale, val)
        valid = [cond for needed, cond in ((lo < 0, dist >= 0), (hi > max_dist, dist <= max_dist),
                                           (rows_valid < chunk, r < rows_valid)) if needed]
        if valid:
            val = jnp.where(functools.reduce(jnp.logical_and, valid), val, NEG_INF)
        out[...] = val


def _build_bias(table, *, head0, n_d, rows, cols, base0, dstep, rs, cs, dscale=1, max_dist=1 << 30,
                r_valid=1 << 30, d_valid=1 << 30, col_tile=None, scale=1.0):
    col_tile = cols if col_tile is None else col_tile
    kern = functools.partial(_bias_kernel, head0=head0, base0=base0, dstep=dstep, rs=rs, cs=cs, dscale=dscale,
                             max_dist=max_dist, r_valid=r_valid, d_valid=d_valid, col_tile=col_tile, scale=scale)
    return pl.pallas_call(
        kern,
        grid=(HEADS,),
        in_specs=[pl.BlockSpec(memory_space=pltpu.SMEM)],
        out_specs=pl.BlockSpec((1, n_d, rows, cols), lambda h: (h, 0, 0, 0)),
        out_shape=jax.ShapeDtypeStruct((HEADS, n_d, rows, cols), F32),
        compiler_params=pltpu.CompilerParams(dimension_semantics=("arbitrary",), vmem_limit_bytes=VMEM_LIMIT),
        name="rel_bias_tiles",
    )(table)


RM_AQ, RM_AK, RM_AV = 0, 256, 512
RM_BQ, RM_BK, RM_BV = 768, 1024, 1280
RM_CK = 1536
RM_KVC = 1792
RM_KSW = 1920
RM_GATE = 2048
RM_COLS = 3072
TR_CQ, TR_CV, TR_DQ, TR_DVS, TR_DVW, TR_DG = 0, 256, 512, 768, 832, 896
GATE_ROWS = 16
TR_ROWS = TR_DG + GATE_ROWS


def _proj_kernel(x_ref, nw_ref, wrm_ref, wt_ref, grm_ref, gt_ref, e64_ref, e32_ref,
                 aq_ref, ak_ref, av_ref, bq_ref, bk_ref, bv_ref, ck_ref, kvc_ref, ksw_ref, gate_ref,
                 aq4_ref, ak4_ref, av4_ref, aq16_ref, ak16_ref, av16_ref,
                 cq_ref, cv_ref, dq_ref, dvs_ref, dvw_ref, dg_ref, fold_ref):
    x = x_ref[...]
    ms = jnp.mean(x * x, axis=-1, keepdims=True)
    xn = (x * lax.rsqrt(ms + RMS_EPS) * nw_ref[...]).astype(BF16)
    rows = x.shape[0]

    def rm(c0, width):
        return jnp.dot(xn, wrm_ref[:, c0:c0 + width], preferred_element_type=F32)

    def normed(h, c0, e_ref):
        width = h.shape[1]
        msq = jnp.dot((h * h).astype(BF16), e_ref[0:width, 0:width], preferred_element_type=F32)
        return h * lax.rsqrt(msq + RMS_EPS) * grm_ref[:, c0:c0 + width]

    def put_folded(val, ref, folded_refs):
        ref[...] = val.astype(ref.dtype)
        for half in range(GROUP // LANES):
            fold_ref[half] = val[:, half * LANES:(half + 1) * LANES]
        for rate, fref in zip(FOLD_RATES, folded_refs):
            for rho in range(rate):
                for half in range(GROUP // LANES):
                    c0 = rho * GROUP + half * LANES
                    fref[:, c0:c0 + LANES] = fold_ref[half, pl.ds(rho, rows // rate, stride=rate), :].astype(fref.dtype)

    slabs = ((RM_AQ, GROUP, e64_ref, aq_ref, (aq4_ref, aq16_ref)), (RM_AK, GROUP, e64_ref, ak_ref, (ak4_ref, ak16_ref)),
             (RM_AV, GROUP, None, av_ref, (av4_ref, av16_ref)), (RM_BQ, GROUP, e64_ref, bq_ref, None),
             (RM_BK, GROUP, e64_ref, bk_ref, None), (RM_BV, GROUP, None, bv_ref, None),
             (RM_CK, GROUP, e32_ref, ck_ref, None), (RM_KVC, 2 * HEAD_DIM, None, kvc_ref, None),
             (RM_KSW, 2 * HEAD_DIM, e64_ref, ksw_ref, None), (RM_GATE, N_MIXERS * GROUP, None, gate_ref, None))

    def finish(n, h):
        c0, _, e_ref, ref, folded_refs = slabs[n]
        if e_ref is not None:
            h = normed(h, c0, e_ref)
        if folded_refs is None:
            ref[...] = h.astype(ref.dtype)
        else:
            put_folded(h, ref, folded_refs)

    _staggered(len(slabs), lambda n: rm(slabs[n][0], slabs[n][1]), finish, ahead=PROJ_LOOKAHEAD)

    key_major = lax.dot_general(wt_ref[...], xn, NT_DIMS, preferred_element_type=F32)

    def tr(r0, height):
        return key_major[r0:r0 + height]

    def tr_normed(r0, height, group):
        h3 = tr(r0, height).reshape(height // group, group, rows)
        msq = jnp.mean(h3 * h3, axis=1, keepdims=True)
        return (h3 * lax.rsqrt(msq + RMS_EPS)).reshape(height, rows) * gt_ref[r0:r0 + height, :]

    def put(ref, val):
        for t in range(rows // QT):
            ref[t] = val[:, t * QT:(t + 1) * QT].astype(ref.dtype)

    def with_ones(v):
        ones = jnp.ones((V_ROWS - HEAD_DIM, rows), F32)
        parts = []
        for h in range(v.shape[0] // HEAD_DIM):
            parts += [v[h * HEAD_DIM:(h + 1) * HEAD_DIM], ones]
        return jnp.concatenate(parts, axis=0)

    put(cq_ref, tr_normed(TR_CQ, GROUP, DIFF_QK_DIM))
    put(cv_ref, with_ones(tr(TR_CV, GROUP)))
    put(dq_ref, tr_normed(TR_DQ, GROUP, HEAD_DIM))
    put(dvs_ref, with_ones(tr(TR_DVS, HEAD_DIM)))
    put(dvw_ref, with_ones(tr(TR_DVW, HEAD_DIM)))
    put(dg_ref, tr(TR_DG, GATE_ROWS))


def _project(x2, nw, wrm, wt, grm, gt, e64, e32):
    m, d = x2.shape
    nt = m // QT
    tpr = PROJ_ROWS // QT
    const = lambda shape: pl.BlockSpec(shape, lambda i: (0,) * len(shape))
    rm_out = lambda width, dtype: (jax.ShapeDtypeStruct((m, width), dtype),
                                   pl.BlockSpec((PROJ_ROWS, width), lambda i: (i, 0)))
    tr_out = lambda height, dtype: (jax.ShapeDtypeStruct((nt, height, QT), dtype),
                                    pl.BlockSpec((tpr, height, QT), lambda i: (i, 0, 0)))
    outs = [rm_out(GROUP, BF16)] * 7 + [rm_out(2 * HEAD_DIM, F32), rm_out(2 * HEAD_DIM, BF16),
                                        rm_out(N_MIXERS * GROUP, BF16)]
    fold_out = lambda rate: (jax.ShapeDtypeStruct((m // rate, rate * GROUP), BF16),
                             pl.BlockSpec((PROJ_ROWS // rate, rate * GROUP), lambda i: (i, 0)))
    outs += [fold_out(rate) for rate in FOLD_RATES for _ in range(3)]
    outs += [tr_out(GROUP, BF16), tr_out(HEADS * V_ROWS, BF16), tr_out(GROUP, BF16), tr_out(V_ROWS, BF16),
             tr_out(V_ROWS, BF16), tr_out(GATE_ROWS, F32)]
    return pl.pallas_call(
        _proj_kernel,
        grid=(m // PROJ_ROWS,),
        in_specs=[pl.BlockSpec((PROJ_ROWS, d), lambda i: (i, 0)), const((1, d)), const((d, RM_COLS)),
                  const((TR_ROWS, d)), const((1, RM_COLS)), const((TR_ROWS, 1)), const((GROUP, GROUP)),
                  const((GROUP, GROUP))],
        out_specs=[o[1] for o in outs],
        out_shape=[o[0] for o in outs],
        scratch_shapes=[pltpu.VMEM((GROUP // LANES, PROJ_ROWS, LANES), F32)],
        compiler_params=pltpu.CompilerParams(dimension_semantics=("arbitrary",), vmem_limit_bytes=VMEM_LIMIT),
        name="in_projection",
    )(x2, nw, wrm, wt, grm, gt, e64, e32)


def _band_kernel(sink_ref, q_ref, kp_ref, kc_ref, vp_ref, vc_ref, bias_ref, o_ref, lse_ref, p_ref, *, use_sink):
    i = pl.program_id(1)
    n_blocks = q_ref.shape[1] // BAND_TILE
    head_q = lax.broadcasted_iota(jnp.int32, (BAND_TILE, GROUP), 1) // HEAD_DIM
    head_v = lax.broadcasted_iota(jnp.int32, (2 * BAND_TILE, GROUP), 1) // HEAD_DIM
    lane = lax.broadcasted_iota(jnp.int32, (BAND_TILE, LANES), 1)
    in_prev = lax.broadcasted_iota(jnp.int32, (1, 2 * BAND_TILE), 1) < BAND_TILE
    no_prev = jnp.where(in_prev & (i == 0), NEG_INF, 0.0).astype(F32)

    def window(cur_ref, prev_ref, bb, m):
        if m == 0:
            return jnp.concatenate([prev_ref[bb], cur_ref[bb, 0:BAND_TILE, :]], axis=0)
        return cur_ref[bb, (m - 1) * BAND_TILE:(m + 1) * BAND_TILE, :]

    def scores(n):
        bb, m = divmod(n, n_blocks)
        q = q_ref[bb, m * BAND_TILE:(m + 1) * BAND_TILE, :]
        q_heads = jnp.concatenate([jnp.where(head_q == h, q, jnp.zeros_like(q)) for h in range(HEADS)], axis=0)
        bias = bias_ref[:, 0].reshape(HEADS * BAND_TILE, 2 * BAND_TILE)
        if m == 0:
            bias = bias + no_prev
        return lax.dot_general(q_heads, window(kc_ref, kp_ref, bb, m), NT_DIMS, preferred_element_type=F32) + bias

    def update(n, s_heads):
        bb, m = divmod(n, n_blocks)
        lse_tile = jnp.zeros((BAND_TILE, LANES), F32)
        for h in range(HEADS):
            s = s_heads[h * BAND_TILE:(h + 1) * BAND_TILE]
            mx = jnp.max(s, axis=1, keepdims=True)
            if use_sink:
                mx = jnp.maximum(mx, sink_ref[h])
            p = jnp.exp2(s - mx)
            den = jnp.sum(p, axis=1, keepdims=True)
            if use_sink:
                den = den + jnp.exp2(sink_ref[h] - mx)
            p_ref[n % 2, :, h * 2 * BAND_TILE:(h + 1) * 2 * BAND_TILE] = (p * (1.0 / den)).astype(BF16)
            lse_tile = jnp.where(lane == h, mx + jnp.log2(den), lse_tile)
        v = window(vc_ref, vp_ref, bb, m)
        v_heads = jnp.concatenate([jnp.where(head_v == h, v, jnp.zeros_like(v)) for h in range(HEADS)], axis=0)
        rows = slice(m * BAND_TILE, (m + 1) * BAND_TILE)
        o_ref[bb, rows, :] = jnp.dot(p_ref[n % 2], v_heads, preferred_element_type=F32).astype(o_ref.dtype)
        lse_ref[bb, rows, :] = lse_tile

    _staggered(q_ref.shape[0] * n_blocks, scores, update, ahead=BAND_LOOKAHEAD)


def _banded(q, k, v, bias, sink, rate, use_sink):
    b, ln, _ = q.shape
    step = min(BAND_STEP, ln)
    per_step = step // BAND_TILE
    cur = pl.BlockSpec((b, step, GROUP), lambda r, i: (0, i, r))
    prev = pl.BlockSpec((b, BAND_TILE, GROUP), lambda r, i: (0, jnp.maximum(i * per_step - 1, 0), r))
    o, lse = pl.pallas_call(
        functools.partial(_band_kernel, use_sink=use_sink),
        grid=(rate, ln // step),
        in_specs=[pl.BlockSpec(memory_space=pltpu.SMEM), cur, prev, cur, prev, cur,
                  pl.BlockSpec((HEADS, 1, BAND_TILE, 2 * BAND_TILE), lambda r, i: (0, 0, 0, 0))],
        out_specs=[cur, pl.BlockSpec((b, step, LANES), lambda r, i: (0, i, r))],
        out_shape=[jax.ShapeDtypeStruct((b, ln, rate * GROUP), BF16), jax.ShapeDtypeStruct((b, ln, rate * LANES), F32)],
        scratch_shapes=[pltpu.VMEM((2, BAND_TILE, HEADS * 2 * BAND_TILE), BF16)],
        compiler_params=pltpu.CompilerParams(dimension_semantics=("arbitrary",) * 2),
        name=f"banded_attention_r{rate}",
    )(sink, q, k, k, v, v, bias)
    return o, lse


def _flash_reset(m_ref, acc_ref):
    m_ref[...] = jnp.full(m_ref.shape, NEG_INF, F32)
    acc_ref[...] = jnp.zeros(acc_ref.shape, F32)


def _flash_update(n, s, v_t, m_ref, acc_ref, shift=None):
    m_old = m_ref[n]
    if shift is None:
        m_new = jnp.maximum(m_old, jnp.max(s, axis=0, keepdims=True))
        p = jnp.exp2(s - m_new)
    else:
        m_new = jnp.maximum(m_old, jnp.max(s, axis=0, keepdims=True) + shift)
        p = jnp.exp2(s - (m_new - shift))
    alpha = jnp.exp2(m_old - m_new)
    acc_ref[n] = alpha * acc_ref[n] + jnp.dot(v_t, p.astype(BF16), preferred_element_type=F32)
    m_ref[n] = m_new


def _flash_result(n, acc_ref):
    return acc_ref[n, 0:HEAD_DIM, :] / acc_ref[n, HEAD_DIM:HEAD_DIM + 1, :]


def _staggered(n_items, scores, update, ahead=MXU_LOOKAHEAD):
    pending = {n: scores(n) for n in range(min(ahead, n_items))}
    for n in range(n_items):
        if n + ahead < n_items:
            pending[n + ahead] = scores(n + ahead)
        update(n, pending.pop(n))


def _pipelined_tiles(first, n_tiles, n_chains, group, load_tile, scores, update, next_ref, left_by_previous=None,
                     last_of_sweep=False):
    ahead = next_ref.shape[0]
    n_items = group * n_chains
    assert ahead <= n_chains

    def body(trip, _, issue_next=True):
        base = first + trip * group
        tiles, pending = {}, {}
        for n in range(n_items):
            cur = next_ref[n] if n < ahead else pending.pop(n)
            if n + ahead < n_items or issue_next:
                g, c = divmod(n + ahead, n_chains)
                if g not in tiles:
                    tiles[g] = load_tile(base + g)
                new = scores(tiles[g], base + g, c)
                if n + ahead < n_items:
                    pending[n + ahead] = new
                else:
                    next_ref[n + ahead - n_items] = new
            update(base + n // n_chains, n % n_chains, cur)

    if left_by_previous is None:
        first_tile = load_tile(first)
        for n in range(ahead):
            next_ref[n] = scores(first_tile, first, n)
    else:
        for n in range(ahead):
            next_ref[n] = next_ref[n] + left_by_previous(first, n)
    n_trips = (n_tiles + group - 1) // group
    if last_of_sweep:
        lax.fori_loop(0, n_trips - 1, body, None)
        body(n_trips - 1, None, issue_next=False)
    else:
        lax.fori_loop(0, n_trips, body, None)


def _flash_scratch(chains, ahead):
    return [pltpu.VMEM((chains, 1, QT), F32), pltpu.VMEM((chains, V_ROWS, QT), F32),
            pltpu.VMEM((ahead, KT, QT), F32)]


N_NEAR = -(-(FAR_DIST + KT - 1) // QT)
N_BIAS_TILES = N_NEAR + 2
DIFF_TILE_GROUP = 2
SLC_TILE_GROUP = 2
FAR_TILE_GROUP = 4


def _bias_tile_index(i, j):
    return jnp.where(j > i, N_NEAR + 1, jnp.minimum(i - j, N_NEAR))


def _whole_far_groups(i, group):
    return jnp.maximum(i - (N_NEAR - 1), 0) // group * group


def _diff_kernel(far_ref, q_ref, k_ref, v_ref, bias_ref, lam_ref, subln_ref, o_ref, qz_ref, m_ref, acc_ref, next_ref,
                 ot_ref, *, lambda_init):
    i = pl.program_id(1)
    q = q_ref[0]
    row = lax.broadcasted_iota(jnp.int32, (GROUP, QT), 0) // DIFF_QK_DIM
    for n in range(2 * HEADS):
        qz_ref[n] = jnp.where(row == n, q, jnp.zeros_like(q))
    _flash_reset(m_ref, acc_ref)

    def load_tile(j):
        return k_ref[0, pl.ds(pl.multiple_of(jnp.minimum(j, i) * KT, KT), KT), :]

    def values(j, n):
        h = n // 2
        return v_ref[0, jnp.minimum(j, i), h * V_ROWS:(h + 1) * V_ROWS, :]

    n_far = _whole_far_groups(i, FAR_TILE_GROUP)
    _pipelined_tiles(0, n_far, 2 * HEADS, FAR_TILE_GROUP, load_tile,
                     lambda k, j, n: jnp.dot(k, qz_ref[n], preferred_element_type=F32),
                     lambda j, n, s: _flash_update(n, s, values(j, n), m_ref, acc_ref, shift=far_ref[n // 2]),
                     next_ref)

    def scores(k, j, n):
        return jnp.dot(k, qz_ref[n], preferred_element_type=F32) + bias_ref[n // 2, _bias_tile_index(i, j)]

    _pipelined_tiles(n_far, i + 1 - n_far, 2 * HEADS, DIFF_TILE_GROUP, load_tile, scores,
                     lambda j, n, s: _flash_update(n, s, values(j, n), m_ref, acc_ref), next_ref,
                     left_by_previous=lambda j, n: bias_ref[n // 2, _bias_tile_index(i, j)], last_of_sweep=True)

    lam_p = lam_ref[...]
    lam = (jnp.exp(jnp.sum(lam_p[0:1] * lam_p[1:2], axis=1, keepdims=True))
           - jnp.exp(jnp.sum(lam_p[2:3] * lam_p[3:4], axis=1, keepdims=True)) + lambda_init)
    for h in range(HEADS):
        o = _flash_result(2 * h, acc_ref) - lam * _flash_result(2 * h + 1, acc_ref)
        msq = jnp.mean(o * o, axis=0, keepdims=True)
        ot_ref[h * HEAD_DIM:(h + 1) * HEAD_DIM, :] = (o * lax.rsqrt(msq + RMS_EPS) * subln_ref[...]
                                                      * (1.0 - lambda_init))
    o_ref[0] = ot_ref[...].T.astype(o_ref.dtype)


def _diff_attention(far, q_t, k, v_t, bias, lam_p, subln, lambda_init):
    b, s, _ = k.shape
    nq = s // QT
    nkv = s // KT
    v4 = v_t.reshape(b, nkv, HEADS * V_ROWS, KT)
    return pl.pallas_call(
        functools.partial(_diff_kernel, lambda_init=lambda_init),
        grid=(b, nq),
        in_specs=[pl.BlockSpec(memory_space=pltpu.SMEM),
                  pl.BlockSpec((1, GROUP, QT), lambda bb, i: (bb * nq + i, 0, 0)),
                  pl.BlockSpec((1, s, GROUP), lambda bb, i: (bb, 0, 0)),
                  pl.BlockSpec((1, nkv, HEADS * V_ROWS, KT), lambda bb, i: (bb, 0, 0, 0)),
                  pl.BlockSpec((HEADS, N_BIAS_TILES, KT, QT), lambda bb, i: (0, 0, 0, 0)),
                  pl.BlockSpec((4, DIFF_QK_DIM), lambda bb, i: (0, 0)),
                  pl.BlockSpec((HEAD_DIM, 1), lambda bb, i: (0, 0))],
        out_specs=pl.BlockSpec((1, QT, GROUP), lambda bb, i: (bb, i, 0)),
        out_shape=jax.ShapeDtypeStruct((b, s, GROUP), BF16),
        scratch_shapes=[pltpu.VMEM((2 * HEADS, GROUP, QT), BF16)] + _flash_scratch(2 * HEADS, MXU_LOOKAHEAD)
        + [pltpu.VMEM((GROUP, QT), F32)],
        compiler_params=pltpu.CompilerParams(dimension_semantics=("arbitrary", "arbitrary"),
                                             vmem_limit_bytes=VMEM_LIMIT),
        name="diff_attention",
    )(far, q_t, k, v4, bias, lam_p, subln)


def _compress_kernel(ch_ref, ptop_ref, pbot_ref, w1t_ref, w1b_ref, b1_ref, w2k_ref, b2k_ref, w2v_ref, b2v_ref,
                     gk_ref, kc_ref, vct_ref):
    ch = ch_ref[0]
    n_c = ch.shape[0]
    u = jnp.dot((ch + ptop_ref[...]).astype(BF16), w1t_ref[...], preferred_element_type=F32)
    v = jnp.dot((ch + pbot_ref[...]).astype(BF16), w1b_ref[...], preferred_element_type=F32)
    v_next = pltpu.roll(v, n_c - 1, 0)
    hid = jax.nn.gelu(u + v_next + b1_ref[...])
    hk = hid[:, :CMP_HIDDEN].astype(BF16)
    hv = hid[:, CMP_HIDDEN:].astype(BF16)
    kc = jnp.dot(hk, w2k_ref[...], preferred_element_type=F32) + b2k_ref[...]
    msq = jnp.mean(kc * kc, axis=-1, keepdims=True)
    kc_ref[0] = (kc * lax.rsqrt(msq + RMS_EPS) * gk_ref[...]).astype(kc_ref.dtype)
    vct = lax.dot_general(w2v_ref[...], hv, NT_DIMS, preferred_element_type=F32) + b2v_ref[...]
    vct_ref[0] = vct.astype(vct_ref.dtype)


def _compress(chunks, ptop, pbot, w1t, w1b, b1, w2k, b2k, w2v, b2v, gk):
    b, n_c, width = chunks.shape
    const = lambda a: pl.BlockSpec(a.shape, lambda bb: (0,) * a.ndim)
    params = (ptop, pbot, w1t, w1b, b1, w2k, b2k, w2v, b2v, gk)
    return pl.pallas_call(
        _compress_kernel,
        grid=(b,),
        in_specs=[pl.BlockSpec((1, n_c, width), lambda bb: (bb, 0, 0))] + [const(a) for a in params],
        out_specs=[pl.BlockSpec((1, n_c, HEAD_DIM), lambda bb: (bb, 0, 0)),
                   pl.BlockSpec((1, HEAD_DIM, n_c), lambda bb: (bb, 0, 0))],
        out_shape=[jax.ShapeDtypeStruct((b, n_c, HEAD_DIM), BF16), jax.ShapeDtypeStruct((b, HEAD_DIM, n_c), BF16)],
        compiler_params=pltpu.CompilerParams(dimension_semantics=("arbitrary",), vmem_limit_bytes=VMEM_LIMIT),
        name="nsa_compress",
    )(chunks, *params)


def _cmp_attn_kernel(q_ref, kc_ref, vct_ref, bias_ref, o_ref, sel_ref, p_ref, *, n_sel):
    i = pl.program_id(0)
    n_tiles = pl.num_programs(0)
    for part in range(1, CMP_PARTS + 1):
        @pl.when((i * CMP_PARTS >= (part - 1) * n_tiles) & (i * CMP_PARTS < part * n_tiles))
        def _(part=part):
            for bb in range(q_ref.shape[0]):
                one = pl.ds(bb, 1)
                _cmp_attn_body(i, kc_ref.shape[1] * part // CMP_PARTS, q_ref.at[bb], kc_ref.at[one], vct_ref.at[one],
                               bias_ref, o_ref.at[bb], sel_ref.at[bb], p_ref, n_sel)


def _cmp_attn_body(i, n_c, q_ref, kc_ref, vct_ref, bias_ref, o_ref, sel_ref, p_ref, n_sel):
    kc = kc_ref[0, 0:n_c, :]
    vct = vct_ref[0, :, 0:n_c]
    n_blk = n_c * CMP_STRIDE // SLC_BLOCK
    probs = []

    def scores(h):
        return (jnp.dot(kc, q_ref[0, h * HEAD_DIM:(h + 1) * HEAD_DIM, :], preferred_element_type=F32)
                + bias_ref[h, 0, 0:n_c, :])

    def update(h, s):
        m = jnp.maximum(jnp.max(s, axis=0, keepdims=True), 0.5 * NEG_INF)
        p = jnp.exp2(s - m)
        den = jnp.sum(p, axis=0, keepdims=True)
        p = p * (1.0 / jnp.maximum(den, TINY))
        o_ref[0, h * HEAD_DIM:(h + 1) * HEAD_DIM, :] = jnp.dot(vct, p.astype(BF16),
                                                               preferred_element_type=F32).astype(o_ref.dtype)
        probs.append(p)

    _staggered(HEADS, scores, update)
    psum = (probs[0] + probs[1]) + (probs[2] + probs[3])
    per_blk = SLC_BLOCK // CMP_STRIDE
    halves = []
    for half in range(QT // LANES):
        p_ref[half, 0:8, :] = jnp.zeros((8, LANES), F32)
        p_ref[half, 8:8 + n_c, :] = psum[:, half * LANES:(half + 1) * LANES]
        p_ref[half, 8 + n_c:16 + n_c, :] = jnp.zeros((8, LANES), F32)
        acc = p_ref[half, pl.ds(7, n_blk, stride=per_blk), :]
        for t in range(per_blk):
            acc = acc + p_ref[half, pl.ds(8 + t, n_blk, stride=per_blk), :]
        halves.append(acc)
    imp = jnp.concatenate(halves, axis=1)
    blk = lax.broadcasted_iota(jnp.int32, (n_blk, QT), 0)
    cur = (i * QT + lax.broadcasted_iota(jnp.int32, (n_blk, QT), 1)) // SLC_BLOCK
    forced = (blk == 0) | (blk == cur) | (blk == cur - 1)
    val = jnp.where(forced, FORCE_SELECT, jnp.where(blk <= cur, imp, NEG_INF))
    sel = jnp.zeros((n_blk, QT), jnp.bool_)
    for _ in range(n_sel):
        top = jnp.max(val, axis=0, keepdims=True)
        idx = jnp.min(jnp.where(val == top, blk, n_blk), axis=0, keepdims=True)
        hit = blk == idx
        sel = sel | hit
        val = jnp.where(hit, -3.0e38, val)
    sel_ref[0, 0:n_blk, :] = jnp.where(sel, 1.0, 0.0).astype(sel_ref.dtype)
    if n_blk < sel_ref.shape[1]:
        sel_ref[0, n_blk:, :] = jnp.zeros((sel_ref.shape[1] - n_blk, QT), sel_ref.dtype)


def _cmp_attention(q_t, kc, vct, bias, b):
    nt = q_t.shape[0]
    nq = nt // b
    n_c = kc.shape[1]
    n_blk = nq * QT // SLC_BLOCK
    o_cmp, sel = pl.pallas_call(
        functools.partial(_cmp_attn_kernel, n_sel=min(SLC_TOPK, n_blk)),
        grid=(nq,),
        in_specs=[pl.BlockSpec((b, 1, GROUP, QT), lambda i: (0, i, 0, 0)),
                  pl.BlockSpec((b, n_c, HEAD_DIM), lambda i: (0, 0, 0)),
                  pl.BlockSpec((b, HEAD_DIM, n_c), lambda i: (0, 0, 0)),
                  pl.BlockSpec((HEADS, 1, n_c, QT), lambda i: (0, 0, 0, i))],
        out_specs=[pl.BlockSpec((b, 1, GROUP, QT), lambda i: (0, i, 0, 0)),
                   pl.BlockSpec((b, 1, n_blk, QT), lambda i: (0, i, 0, 0))],
        out_shape=[jax.ShapeDtypeStruct((b, nq, GROUP, QT), BF16), jax.ShapeDtypeStruct((b, nq, n_blk, QT), BF16)],
        scratch_shapes=[pltpu.VMEM((QT // LANES, n_c + 16, LANES), F32)],
        compiler_params=pltpu.CompilerParams(dimension_semantics=("arbitrary",), vmem_limit_bytes=VMEM_LIMIT),
        name="nsa_compressed_attention",
    )(q_t.reshape(b, nq, GROUP, QT), kc, vct, bias)
    return o_cmp.reshape(nt, GROUP, QT), sel.reshape(nt, n_blk, QT)


SEL_REP = 8
N_WIN = -(-(NSA_WINDOW - 1 + KT - 1) // QT)


def _slc_win_kernel(far_ref, q_ref, ksw_ref, vs_ref, vw_ref, sel_ref, rep_ref, ocmp_ref, g_ref, bslc_ref, bwin_ref,
                    o_ref, qz_ref, m_ref, acc_ref, next_ref, ot_ref, mask_ref):
    i = pl.program_id(1)
    sel8 = jnp.dot(rep_ref[...], sel_ref[0], preferred_element_type=F32)
    mask_ref[...] = (sel8 - 1.0) * (-NEG_INF)
    blocks_per_tile = KT // SLC_BLOCK
    mrows = blocks_per_tile * SEL_REP
    zeros = jnp.zeros((HEAD_DIM, QT), BF16)
    for h in range(HEADS):
        qh = q_ref[0, h * HEAD_DIM:(h + 1) * HEAD_DIM, :]
        qz_ref[h] = jnp.concatenate([qh, zeros], axis=0)
        qz_ref[HEADS + h] = jnp.concatenate([zeros, qh], axis=0)
    _flash_reset(m_ref, acc_ref)

    def load_keys(j):
        return ksw_ref[0, pl.ds(pl.multiple_of(j * KT, KT), KT), :]

    def load_tile(j):
        j = jnp.minimum(j, i)
        m8 = mask_ref[pl.ds(pl.multiple_of(j * mrows, mrows), mrows), :]
        mask = jnp.broadcast_to(m8.reshape(blocks_per_tile, 1, SEL_REP, QT),
                                (blocks_per_tile, SLC_BLOCK // SEL_REP, SEL_REP, QT)).reshape(KT, QT)
        return load_keys(j), mask

    def values(j):
        return vs_ref[0, jnp.minimum(j, i)]

    n_far = _whole_far_groups(i, FAR_TILE_GROUP)
    _pipelined_tiles(0, n_far, HEADS, FAR_TILE_GROUP, load_tile,
                     lambda tile, j, h: jnp.dot(tile[0], qz_ref[h], preferred_element_type=F32) + tile[1],
                     lambda j, h, s: _flash_update(h, s, values(j), m_ref, acc_ref, shift=far_ref[h]), next_ref)

    def slc_scores(tile, j, h):
        k, mask = tile
        return jnp.dot(k, qz_ref[h], preferred_element_type=F32) + mask + bslc_ref[h, _bias_tile_index(i, j)]

    _pipelined_tiles(n_far, i + 1 - n_far, HEADS, SLC_TILE_GROUP, load_tile, slc_scores,
                     lambda j, h, s: _flash_update(h, s, values(j), m_ref, acc_ref), next_ref,
                     left_by_previous=lambda j, h: bslc_ref[h, _bias_tile_index(i, j)], last_of_sweep=True)

    def win_tile(n):
        d = N_WIN - 1 - n // HEADS
        return d, n % HEADS, jnp.maximum(i - d, 0)

    def win_scores(n):
        d, h, j = win_tile(n)
        missing = jnp.where(i < d, NEG_INF, 0.0).astype(F32)
        return jnp.dot(load_keys(j), qz_ref[HEADS + h], preferred_element_type=F32) + (bwin_ref[h, d] + missing)

    def win_update(n, s):
        _, h, j = win_tile(n)
        _flash_update(HEADS + h, s, vw_ref[0, j], m_ref, acc_ref)

    _staggered(N_WIN * HEADS, win_scores, win_update)

    for h in range(HEADS):
        g = jax.nn.sigmoid(g_ref[0, 3 * h:3 * h + 3, :])
        ot_ref[h * HEAD_DIM:(h + 1) * HEAD_DIM, :] = (g[0:1] * ocmp_ref[0, h * HEAD_DIM:(h + 1) * HEAD_DIM, :]
                                                      + g[1:2] * _flash_result(h, acc_ref)
                                                      + g[2:3] * _flash_result(HEADS + h, acc_ref))
    o_ref[0] = ot_ref[...].T.astype(o_ref.dtype)


def _slc_win_attention(far, q_t, ksw, vs_t, vw_t, sel, rep, ocmp, g_t, bslc, bwin):
    b, s, _ = ksw.shape
    nq = s // QT
    nkv = s // KT
    n_blk = s // SLC_BLOCK
    tile = lambda height: pl.BlockSpec((1, height, QT), lambda bb, i: (bb * nq + i, 0, 0))
    whole = lambda a: pl.BlockSpec(a.shape, lambda bb, i: (0,) * a.ndim)
    return pl.pallas_call(
        _slc_win_kernel,
        grid=(b, nq),
        in_specs=[pl.BlockSpec(memory_space=pltpu.SMEM), tile(GROUP),
                  pl.BlockSpec((1, s, 2 * HEAD_DIM), lambda bb, i: (bb, 0, 0)),
                  pl.BlockSpec((1, nkv, V_ROWS, KT), lambda bb, i: (bb, 0, 0, 0)),
                  pl.BlockSpec((1, nkv, V_ROWS, KT), lambda bb, i: (bb, 0, 0, 0)),
                  tile(n_blk), whole(rep), tile(GROUP), tile(GATE_ROWS), whole(bslc), whole(bwin)],
        out_specs=pl.BlockSpec((1, QT, GROUP), lambda bb, i: (bb, i, 0)),
        out_shape=jax.ShapeDtypeStruct((b, s, GROUP), BF16),
        scratch_shapes=[pltpu.VMEM((2 * HEADS, 2 * HEAD_DIM, QT), BF16)] + _flash_scratch(2 * HEADS, MXU_LOOKAHEAD)
        + [pltpu.VMEM((GROUP, QT), F32), pltpu.VMEM((n_blk * SEL_REP, QT), F32)],
        compiler_params=pltpu.CompilerParams(dimension_semantics=("arbitrary", "arbitrary"),
                                             vmem_limit_bytes=VMEM_LIMIT),
        name="nsa_selected_window_attention",
    )(far, q_t, ksw, vs_t.reshape(b, nkv, V_ROWS, KT), vw_t.reshape(b, nkv, V_ROWS, KT), sel, rep, ocmp, g_t,
      bslc, bwin)


def _out_kernel(x_ref, a0_ref, a1_ref, a2_ref, l0_ref, l1_ref, l2_ref, ob_ref, oc_ref, od_ref, gate_ref, e_ref,
                w_ref, o_ref, unfold_ref):
    rows = x_ref.shape[0]

    def unfolded(ref, rate):
        width = ref.shape[1] // rate
        for rho in range(rate):
            for part in range(width // LANES):
                c0 = rho * width + part * LANES
                unfold_ref[part, pl.ds(rho, rows // rate, stride=rate), :] = ref[:, c0:c0 + LANES].astype(F32)
        return jnp.concatenate([unfold_ref[part] for part in range(width // LANES)], axis=1)

    a0, l0 = a0_ref[...], l0_ref[...]
    a1, l1 = unfolded(a1_ref, FOLD_RATES[0]), unfolded(l1_ref, FOLD_RATES[0])
    a2, l2 = unfolded(a2_ref, FOLD_RATES[1]), unfolded(l2_ref, FOLD_RATES[1])
    mx = jnp.maximum(jnp.maximum(l0, l1), l2)
    e0, e1, e2 = jnp.exp2(l0 - mx), jnp.exp2(l1 - mx), jnp.exp2(l2 - mx)
    den = e0 + e1 + e2

    def per_head_lanes(w):
        hi = w.astype(BF16)
        lo = (w - hi.astype(F32)).astype(BF16)
        return (jnp.dot(hi, e_ref[...], preferred_element_type=F32)
                + jnp.dot(lo, e_ref[...], preferred_element_type=F32))

    o_a = per_head_lanes(e0 / den) * a0 + per_head_lanes(e1 / den) * a1 + per_head_lanes(e2 / den) * a2
    y = jnp.concatenate([o_a, ob_ref[...].astype(F32), oc_ref[...].astype(F32), od_ref[...].astype(F32)], axis=1)
    g = gate_ref[...].astype(F32)
    y = y * (g * jax.nn.sigmoid(g))
    o_ref[...] = x_ref[...] + jnp.dot(y.astype(BF16), w_ref[...], preferred_element_type=F32)


def _out_projection(x2, a_outs, a_lses, o_b, o_c, o_d, gate, w_out):
    m, d = x2.shape
    rowblk = lambda width: pl.BlockSpec((PROJ_ROWS, width), lambda i: (i, 0))
    folded = lambda width, rate: pl.BlockSpec((PROJ_ROWS // rate, rate * width), lambda i: (i, 0))
    head_of_lane = np.arange(GROUP) // HEAD_DIM
    expand = jnp.asarray((np.arange(LANES)[:, None] == head_of_lane[None, :]).astype(np.float32), BF16)
    return pl.pallas_call(
        _out_kernel,
        grid=(m // PROJ_ROWS,),
        in_specs=[rowblk(d)] + [folded(GROUP, rate) for rate in (1,) + FOLD_RATES]
        + [folded(LANES, rate) for rate in (1,) + FOLD_RATES] + [rowblk(GROUP)] * 3
        + [rowblk(N_MIXERS * GROUP), pl.BlockSpec((LANES, GROUP), lambda i: (0, 0)),
           pl.BlockSpec((N_MIXERS * GROUP, d), lambda i: (0, 0))],
        out_specs=rowblk(d),
        out_shape=jax.ShapeDtypeStruct((m, d), F32),
        scratch_shapes=[pltpu.VMEM((GROUP // LANES, PROJ_ROWS, LANES), F32)],
        compiler_params=pltpu.CompilerParams(dimension_semantics=("arbitrary",), vmem_limit_bytes=VMEM_LIMIT),
        name="out_projection",
    )(x2, *a_outs, *a_lses, o_b, o_c, o_d, gate, expand, w_out)


def _block_diag_mean(group):
    idx = np.arange(GROUP) // group
    return jnp.asarray((idx[:, None] == idx[None, :]).astype(np.float32) / group, BF16)


def _layer_weights(w_in, qk_gain, qk_gain_diff):
    d = w_in.shape[0]
    sizes = (GROUP,) * 3 + (GROUP, GROUP // 2, GROUP // 2) + (GROUP,) * 3 + (GROUP,) + (HEAD_DIM,) * 6 \
        + (HEADS * 3, N_MIXERS * GROUP)
    offs = np.concatenate([[0], np.cumsum(sizes)])
    col = lambda n: w_in[:, offs[n]:offs[n + 1]]
    (a_q, a_k, a_v, b_q, b_k, b_v, c_q, c_k, c_v, d_q, d_kc, d_vc, d_ks, d_vs, d_kw, d_vw, d_g, gate) = \
        [col(n) for n in range(18)]
    rep_kv = lambda w: jnp.repeat(w.reshape(d, 2, HEAD_DIM), 2, axis=1).reshape(d, GROUP)
    wrm = jnp.concatenate([a_q, a_k, a_v, b_q, rep_kv(b_k), rep_kv(b_v), c_k, d_kc, d_vc, d_ks, d_kw, gate], axis=1)
    wt = jnp.concatenate([c_q, c_v, d_q, d_vs, d_vw, d_g, jnp.zeros((d, GATE_ROWS - HEADS * 3), w_in.dtype)], axis=1).T
    g = qk_gain
    ones = lambda n: jnp.ones((n,), F32)
    tile4 = lambda v: jnp.tile(v, HEADS)
    scale = HEAD_DIM ** -0.5 * LOG2E
    grm = jnp.concatenate([tile4(g[0]) * scale, tile4(g[1]), ones(GROUP), tile4(g[2]) * scale, tile4(g[3]),
                           ones(GROUP), jnp.tile(qk_gain_diff[1], 2 * HEADS), ones(2 * HEAD_DIM), g[6], g[7],
                           ones(N_MIXERS * GROUP)])
    gt = jnp.concatenate([jnp.tile(qk_gain_diff[0], 2 * HEADS) * (DIFF_QK_DIM ** -0.5 * LOG2E), ones(GROUP),
                          tile4(g[4]) * scale, ones(2 * HEAD_DIM + GATE_ROWS)])
    return wrm.astype(BF16), wt.astype(BF16), grm.reshape(1, -1), gt.reshape(-1, 1)


def _compress_weights(cmp_pos, cmp_w1, cmp_b1, cmp_w2, cmp_b2):
    half = CMP_LEN // 2
    pos = jnp.concatenate([cmp_pos[0], cmp_pos[1]], axis=-1)
    ptop = pos[:half].reshape(1, -1)
    pbot = pos[half:].reshape(1, -1)
    w1 = cmp_w1.reshape(2, CMP_LEN, HEAD_DIM, CMP_HIDDEN)
    zeros = jnp.zeros_like(w1[0])
    w1cat = jnp.concatenate([jnp.concatenate([w1[0], zeros], axis=-1),
                             jnp.concatenate([zeros, w1[1]], axis=-1)], axis=1)
    w1t = w1cat[:half].reshape(half * 2 * HEAD_DIM, 2 * CMP_HIDDEN).astype(BF16)
    w1b = w1cat[half:].reshape(half * 2 * HEAD_DIM, 2 * CMP_HIDDEN).astype(BF16)
    b1 = jnp.concatenate([cmp_b1[0], cmp_b1[1]]).reshape(1, -1)
    return (ptop, pbot, w1t, w1b, b1, cmp_w2[0].astype(BF16), cmp_b2[0].reshape(1, -1),
            cmp_w2[1].T.astype(BF16), cmp_b2[1].reshape(-1, 1))


def kernel(x, rel_bias_table, norm_w, w_in, w_out, qk_gain, qk_gain_diff, attn_sinks, diff_lambda, diff_subln,
           cmp_pos, cmp_w1, cmp_b1, cmp_w2, cmp_b2):
    b, s, d = x.shape
    depth = w_in.shape[0]
    n_c = s // CMP_STRIDE
    n_blk = s // SLC_BLOCK
    assert s % (BAND_TILE * DILATED_CONFIGS[-1][1]) == 0 and s % PROJ_ROWS == 0 and d == N_MIXERS * GROUP

    table = rel_bias_table.astype(F32)
    band_bias = [_build_bias(table, head0=0, n_d=1, rows=BAND_TILE, cols=2 * BAND_TILE, base0=BAND_TILE, dstep=0,
                             rs=1, cs=-1, dscale=rate, max_dist=window // rate, scale=LOG2E)
                 for window, rate in DILATED_CONFIGS]
    swa_bias = _build_bias(table, head0=HEADS, n_d=1, rows=BAND_TILE, cols=2 * BAND_TILE, base0=BAND_TILE, dstep=0,
                           rs=1, cs=-1, max_dist=SWA_WINDOW - 1, scale=LOG2E)
    flash_tiles = dict(rows=KT, cols=QT, base0=0, dstep=QT, rs=-1, cs=1, scale=LOG2E)
    diff_bias = _build_bias(table, head0=2 * HEADS, n_d=N_BIAS_TILES, d_valid=N_NEAR + 1, **flash_tiles)
    slc_bias = _build_bias(table, head0=3 * HEADS, n_d=N_BIAS_TILES, d_valid=N_NEAR + 1, **flash_tiles)
    win_bias = _build_bias(table, head0=3 * HEADS, n_d=N_WIN, max_dist=NSA_WINDOW - 1, **flash_tiles)
    far_bias = table[NUM_BUCKETS - 1] * LOG2E
    cmp_bias = _build_bias(table, head0=3 * HEADS, n_d=1, rows=n_c, cols=s, base0=-(CMP_LEN - 1), dstep=0,
                           rs=-CMP_STRIDE, cs=1, r_valid=n_c - 1, col_tile=2 * QT, scale=LOG2E)
    e64, e32 = _block_diag_mean(HEAD_DIM), _block_diag_mean(DIFF_QK_DIM)
    rep_idx = np.arange(n_blk * SEL_REP) // SEL_REP
    rep = jnp.asarray((rep_idx[:, None] == np.arange(n_blk)[None, :]).astype(np.float32), BF16)
    no_sink = jnp.zeros((HEADS,), F32)

    x2 = x.reshape(b * s, d)
    w_in_bf16 = w_in.astype(BF16)
    for layer in range(depth):
        wrm, wt, grm, gt = _layer_weights(w_in_bf16[layer], qk_gain[layer], qk_gain_diff[layer])
        (a_q, a_k, a_v, b_q, b_k, b_v, c_k, kvc, ksw, gate, a_q4, a_k4, a_v4, a_q16, a_k16, a_v16,
         c_qt, c_vt, d_qt, d_vst, d_vwt, d_gt) = _project(x2, norm_w[layer].reshape(1, d), wrm, wt, grm, gt, e64, e32)
        seq = lambda t: t.reshape(b, s, t.shape[-1])
        per_batch = lambda t: t.reshape(b, t.shape[0] // b, t.shape[1])
        flat = lambda t: t.reshape(b * t.shape[1], t.shape[2])
        a_in = ((a_q, a_k, a_v), (a_q4, a_k4, a_v4), (a_q16, a_k16, a_v16))
        a_res = [_banded(*map(per_batch, a_in[n]), band_bias[n], no_sink, rate, False)
                 for n, (_, rate) in enumerate(DILATED_CONFIGS)]
        o_b, _ = _banded(seq(b_q), seq(b_k), seq(b_v), swa_bias, attn_sinks[layer].astype(F32) * LOG2E, 1, True)
        lambda_init = 0.8 - 0.6 * math.exp(-0.3 * layer)
        o_c = _diff_attention(far_bias[2 * HEADS:3 * HEADS], c_qt, seq(c_k), c_vt, diff_bias, diff_lambda[layer].astype(F32),
                              diff_subln[layer].reshape(HEAD_DIM, 1).astype(F32), lambda_init)
        cw = _compress_weights(cmp_pos[layer], cmp_w1[layer], cmp_b1[layer], cmp_w2[layer], cmp_b2[layer])
        kc, vct = _compress(kvc.reshape(b, n_c, CMP_STRIDE * 2 * HEAD_DIM), *cw, qk_gain[layer, 5].reshape(1, -1))
        o_cmp, sel = _cmp_attention(d_qt, kc, vct, cmp_bias, b)
        o_d = _slc_win_attention(far_bias[3 * HEADS:4 * HEADS], d_qt, seq(ksw), d_vst, d_vwt, sel, rep, o_cmp, d_gt,
                                 slc_bias, win_bias)
        x2 = _out_projection(x2, [flat(r[0]) for r in a_res], [flat(r[1]) for r in a_res], o_b.reshape(b * s, GROUP),
                             o_c.reshape(b * s, GROUP), o_d.reshape(b * s, GROUP), gate, w_out[layer].astype(BF16))
    return x2.reshape(b, s, d)
```

```python
import functools
import itertools
import math

import numpy as np
import jax
import jax.numpy as jnp
from jax import lax
from jax.experimental import pallas as pl
from jax.experimental.pallas import tpu as pltpu

F32 = jnp.float32
BF16 = jnp.bfloat16

HEAD_DIM = 64
HEADS = 4
GROUP = HEADS * HEAD_DIM
N_MIXERS = 4
NUM_BUCKETS = 32
REL_MAX_DIST = 2048
DILATED_CONFIGS = ((128, 1), (512, 4), (2048, 16))
FOLD_RATES = tuple(rate for _, rate in DILATED_CONFIGS if rate > 1)
SWA_WINDOW = 128
DIFF_QK_DIM = HEAD_DIM // 2
CMP_LEN = 32
CMP_STRIDE = 16
CMP_HIDDEN = 256
SLC_BLOCK = 64
SLC_TOPK = 16
CMP_PARTS = 4
NSA_WINDOW = 512
RMS_EPS = 1e-6
NEG_INF = -1e30
FORCE_SELECT = 1e9
TINY = 1e-30
LOG2E = math.log2(math.e)

PROJ_ROWS = 512
PROJ_LOOKAHEAD = 1
BAND_TILE = 128
BAND_STEP = 1024
BAND_LOOKAHEAD = 2
BIAS_ROW_CHUNK = 64
LANES = 128
QT = 256
KT = 256
VMEM_LIMIT = 56 * 1024 * 1024
MXU_LOOKAHEAD = 4
V_ROWS = HEAD_DIM + 16

NT_DIMS = (((1,), (1,)), ((), ()))


def _t5_thresholds():
    n = np.arange(0, 4 * REL_MAX_DIST)
    max_exact = NUM_BUCKETS // 2
    nf = np.maximum(n, 1).astype(np.float32)
    large = max_exact + (np.log(nf / np.float32(max_exact)) / np.float32(math.log(REL_MAX_DIST / max_exact))
                         * np.float32(NUM_BUCKETS - max_exact)).astype(np.int32)
    bucket = np.where(n < max_exact, n, np.minimum(large, NUM_BUCKETS - 1))
    return [int(np.argmax(bucket >= b)) for b in range(NUM_BUCKETS)]


T5_THRESHOLDS = _t5_thresholds()
FAR_DIST = T5_THRESHOLDS[-1]


def _bias_kernel(tbl_ref, out_ref, *, head0, base0, dstep, rs, cs, dscale, max_dist, r_valid, d_valid, col_tile,
                 scale):
    h = pl.program_id(0)
    n_d, rows, cols = out_ref.shape[1:]
    chunk = BIAS_ROW_CHUNK if rows % BIAS_ROW_CHUNK == 0 else rows
    bucket_of = lambda n: max(b for b in range(NUM_BUCKETS) if T5_THRESHOLDS[b] <= n)
    for d, c0, r0 in itertools.product(range(n_d), range(0, cols, col_tile), range(0, rows, chunk)):
        out = out_ref.at[0, d, r0:r0 + chunk, c0:c0 + col_tile]
        origin = base0 + d * dstep + r0 * rs + c0 * cs
        corners = [origin + dr * rs + dc * cs for dr in (0, chunk - 1) for dc in (0, col_tile - 1)]
        lo, hi = min(corners), max(corners)
        rows_valid = min(chunk, r_valid - r0)
        if hi < 0 or lo > max_dist or d >= d_valid or rows_valid <= 0:
            out[...] = jnp.full((chunk, col_tile), NEG_INF, F32)
            continue
        r = lax.broadcasted_iota(jnp.int32, (chunk, col_tile), 0)
        c = lax.broadcasted_iota(jnp.int32, (chunk, col_tile), 1)
        dist = origin + r * rs + c * cs
        first, last = bucket_of(max(lo, 0) * dscale), bucket_of(min(hi, max_dist) * dscale)
        val = jnp.full((chunk, col_tile), tbl_ref[first, head0 + h] * scale, F32)
        for b in range(first + 1, last + 1):
            val = jnp.where(dist * dscale >= T5_THRESHOLDS[b], tbl_ref[b, head0 + h] * scale, val)
        valid = [cond for needed, cond in ((lo < 0, dist >= 0), (hi > max_dist, dist <= max_dist),
                                           (rows_valid < chunk, r < rows_valid)) if needed]
        if valid:
            val = jnp.where(functools.reduce(jnp.logical_and, valid), val, NEG_INF)
        out[...] = val


def _build_bias(table, *, head0, n_d, rows, cols, base0, dstep, rs, cs, dscale=1, max_dist=1 << 30,
                r_valid=1 << 30, d_valid=1 << 30, col_tile=None, scale=1.0):
    col_tile = cols if col_tile is None else col_tile
    kern = functools.partial(_bias_kernel, head0=head0, base0=base0, dstep=dstep, rs=rs, cs=cs, dscale=dscale,
                             max_dist=max_dist, r_valid=r_valid, d_valid=d_valid, col_tile=col_tile, scale=scale)
    return pl.pallas_call(
        kern,
        grid=(HEADS,),
        in_specs=[pl.BlockSpec(memory_space=pltpu.SMEM)],
        out_specs=pl.BlockSpec((1, n_d, rows, cols), lambda h: (h, 0, 0, 0)),
        out_shape=jax.ShapeDtypeStruct((HEADS, n_d, rows, cols), F32),
        compiler_params=pltpu.CompilerParams(dimension_semantics=("arbitrary",), vmem_limit_bytes=VMEM_LIMIT),
        name="rel_bias_tiles",
    )(table)


RM_AQ, RM_AK, RM_AV = 0, 256, 512
RM_BQ, RM_BK, RM_BV = 768, 1024, 1280
RM_CK = 1536
RM_KVC = 1792
RM_KSW = 1920
RM_GATE = 2048
RM_COLS = 3072
TR_CQ, TR_CV, TR_DQ, TR_DVS, TR_DVW, TR_DG = 0, 256, 512, 768, 832, 896
GATE_ROWS = 16
TR_ROWS = TR_DG + GATE_ROWS


def _proj_kernel(x_ref, nw_ref, wrm_ref, wt_ref, grm_ref, gt_ref, e64_ref, e32_ref,
                 aq_ref, ak_ref, av_ref, bq_ref, bk_ref, bv_ref, ck_ref, kvc_ref, ksw_ref, gate_ref,
                 aq4_ref, ak4_ref, av4_ref, aq16_ref, ak16_ref, av16_ref,
                 cq_ref, cv_ref, dq_ref, dvs_ref, dvw_ref, dg_ref, fold_ref):
    x = x_ref[...]
    ms = jnp.mean(x * x, axis=-1, keepdims=True)
    xn = (x * lax.rsqrt(ms + RMS_EPS) * nw_ref[...]).astype(BF16)
    rows = x.shape[0]

    def rm(c0, width):
        return jnp.dot(xn, wrm_ref[:, c0:c0 + width], preferred_element_type=F32)

    def normed(h, c0, e_ref):
        width = h.shape[1]
        msq = jnp.dot((h * h).astype(BF16), e_ref[0:width, 0:width], preferred_element_type=F32)
        return h * lax.rsqrt(msq + RMS_EPS) * grm_ref[:, c0:c0 + width]

    def put_folded(val, ref, folded_refs):
        ref[...] = val.astype(ref.dtype)
        for half in range(GROUP // LANES):
            fold_ref[half] = val[:, half * LANES:(half + 1) * LANES]
        for rate, fref in zip(FOLD_RATES, folded_refs):
            for rho in range(rate):
                for half in range(GROUP // LANES):
                    c0 = rho * GROUP + half * LANES
                    fref[:, c0:c0 + LANES] = fold_ref[half, pl.ds(rho, rows // rate, stride=rate), :].astype(fref.dtype)

    slabs = ((RM_AQ, GROUP, e64_ref, aq_ref, (aq4_ref, aq16_ref)), (RM_AK, GROUP, e64_ref, ak_ref, (ak4_ref, ak16_ref)),
             (RM_AV, GROUP, None, av_ref, (av4_ref, av16_ref)), (RM_BQ, GROUP, e64_ref, bq_ref, None),
             (RM_BK, GROUP, e64_ref, bk_ref, None), (RM_BV, GROUP, None, bv_ref, None),
             (RM_CK, GROUP, e32_ref, ck_ref, None), (RM_KVC, 2 * HEAD_DIM, None, kvc_ref, None),
             (RM_KSW, 2 * HEAD_DIM, e64_ref, ksw_ref, None), (RM_GATE, N_MIXERS * GROUP, None, gate_ref, None))

    def finish(n, h):
        c0, _, e_ref, ref, folded_refs = slabs[n]
        if e_ref is not None:
            h = normed(h, c0, e_ref)
        if folded_refs is None:
            ref[...] = h.astype(ref.dtype)
        else:
            put_folded(h, ref, folded_refs)

    _staggered(len(slabs), lambda n: rm(slabs[n][0], slabs[n][1]), finish, ahead=PROJ_LOOKAHEAD)

    key_major = lax.dot_general(wt_ref[...], xn, NT_DIMS, preferred_element_type=F32)

    def tr(r0, height):
        return key_major[r0:r0 + height]

    def tr_normed(r0, height, group):
        h3 = tr(r0, height).reshape(height // group, group, rows)
        msq = jnp.mean(h3 * h3, axis=1, keepdims=True)
        return (h3 * lax.rsqrt(msq + RMS_EPS)).reshape(height, rows) * gt_ref[r0:r0 + height, :]

    def put(ref, val):
        for t in range(rows // QT):
            ref[t] = val[:, t * QT:(t + 1) * QT].astype(ref.dtype)

    def with_ones(v):
        ones = jnp.ones((V_ROWS - HEAD_DIM, rows), F32)
        parts = []
        for h in range(v.shape[0] // HEAD_DIM):
            parts += [v[h * HEAD_DIM:(h + 1) * HEAD_DIM], ones]
        return jnp.concatenate(parts, axis=0)

    put(cq_ref, tr_normed(TR_CQ, GROUP, DIFF_QK_DIM))
    put(cv_ref, with_ones(tr(TR_CV, GROUP)))
    put(dq_ref, tr_normed(TR_DQ, GROUP, HEAD_DIM))
    put(dvs_ref, with_ones(tr(TR_DVS, HEAD_DIM)))
    put(dvw_ref, with_ones(tr(TR_DVW, HEAD_DIM)))
    put(dg_ref, tr(TR_DG, GATE_ROWS))


def _project(x2, nw, wrm, wt, grm, gt, e64, e32):
    m, d = x2.shape
    nt = m // QT
    tpr = PROJ_ROWS // QT
    const = lambda shape: pl.BlockSpec(shape, lambda i: (0,) * len(shape))
    rm_out = lambda width, dtype: (jax.ShapeDtypeStruct((m, width), dtype),
                                   pl.BlockSpec((PROJ_ROWS, width), lambda i: (i, 0)))
    tr_out = lambda height, dtype: (jax.ShapeDtypeStruct((nt, height, QT), dtype),
                                    pl.BlockSpec((tpr, height, QT), lambda i: (i, 0, 0)))
    outs = [rm_out(GROUP, BF16)] * 7 + [rm_out(2 * HEAD_DIM, F32), rm_out(2 * HEAD_DIM, BF16),
                                        rm_out(N_MIXERS * GROUP, BF16)]
    fold_out = lambda rate: (jax.ShapeDtypeStruct((m // rate, rate * GROUP), BF16),
                             pl.BlockSpec((PROJ_ROWS // rate, rate * GROUP), lambda i: (i, 0)))
    outs += [fold_out(rate) for rate in FOLD_RATES for _ in range(3)]
    outs += [tr_out(GROUP, BF16), tr_out(HEADS * V_ROWS, BF16), tr_out(GROUP, BF16), tr_out(V_ROWS, BF16),
             tr_out(V_ROWS, BF16), tr_out(GATE_ROWS, F32)]
    return pl.pallas_call(
        _proj_kernel,
        grid=(m // PROJ_ROWS,),
        in_specs=[pl.BlockSpec((PROJ_ROWS, d), lambda i: (i, 0)), const((1, d)), const((d, RM_COLS)),
                  const((TR_ROWS, d)), const((1, RM_COLS)), const((TR_ROWS, 1)), const((GROUP, GROUP)),
                  const((GROUP, GROUP))],
        out_specs=[o[1] for o in outs],
        out_shape=[o[0] for o in outs],
        scratch_shapes=[pltpu.VMEM((GROUP // LANES, PROJ_ROWS, LANES), F32)],
        compiler_params=pltpu.CompilerParams(dimension_semantics=("arbitrary",), vmem_limit_bytes=VMEM_LIMIT),
        name="in_projection",
    )(x2, nw, wrm, wt, grm, gt, e64, e32)


def _band_kernel(sink_ref, q_ref, kp_ref, kc_ref, vp_ref, vc_ref, bias_ref, o_ref, lse_ref, p_ref, *, use_sink):
    i = pl.program_id(1)
    n_blocks = q_ref.shape[1] // BAND_TILE
    head_q = lax.broadcasted_iota(jnp.int32, (BAND_TILE, GROUP), 1) // HEAD_DIM
    head_v = lax.broadcasted_iota(jnp.int32, (2 * BAND_TILE, GROUP), 1) // HEAD_DIM
    lane = lax.broadcasted_iota(jnp.int32, (BAND_TILE, LANES), 1)
    in_prev = lax.broadcasted_iota(jnp.int32, (1, 2 * BAND_TILE), 1) < BAND_TILE
    no_prev = jnp.where(in_prev & (i == 0), NEG_INF, 0.0).astype(F32)

    def window(cur_ref, prev_ref, bb, m):
        if m == 0:
            return jnp.concatenate([prev_ref[bb], cur_ref[bb, 0:BAND_TILE, :]], axis=0)
        return cur_ref[bb, (m - 1) * BAND_TILE:(m + 1) * BAND_TILE, :]

    def scores(n):
        bb, m = divmod(n, n_blocks)
        q = q_ref[bb, m * BAND_TILE:(m + 1) * BAND_TILE, :]
        q_heads = jnp.concatenate([jnp.where(head_q == h, q, jnp.zeros_like(q)) for h in range(HEADS)], axis=0)
        bias = bias_ref[:, 0].reshape(HEADS * BAND_TILE, 2 * BAND_TILE)
        if m == 0:
            bias = bias + no_prev
        return lax.dot_general(q_heads, window(kc_ref, kp_ref, bb, m), NT_DIMS, preferred_element_type=F32) + bias

    def update(n, s_heads):
        bb, m = divmod(n, n_blocks)
        lse_tile = jnp.zeros((BAND_TILE, LANES), F32)
        for h in range(HEADS):
            s = s_heads[h * BAND_TILE:(h + 1) * BAND_TILE]
            mx = jnp.max(s, axis=1, keepdims=True)
            if use_sink:
                mx = jnp.maximum(mx, sink_ref[h])
            p = jnp.exp2(s - mx)
            den = jnp.sum(p, axis=1, keepdims=True)
            if use_sink:
                den = den + jnp.exp2(sink_ref[h] - mx)
            p_ref[n % 2, :, h * 2 * BAND_TILE:(h + 1) * 2 * BAND_TILE] = (p * (1.0 / den)).astype(BF16)
            lse_tile = jnp.where(lane == h, mx + jnp.log2(den), lse_tile)
        v = window(vc_ref, vp_ref, bb, m)
        v_heads = jnp.concatenate([jnp.where(head_v == h, v, jnp.zeros_like(v)) for h in range(HEADS)], axis=0)
        rows = slice(m * BAND_TILE, (m + 1) * BAND_TILE)
        o_ref[bb, rows, :] = jnp.dot(p_ref[n % 2], v_heads, preferred_element_type=F32).astype(o_ref.dtype)
        lse_ref[bb, rows, :] = lse_tile

    _staggered(q_ref.shape[0] * n_blocks, scores, update, ahead=BAND_LOOKAHEAD)


def _banded(q, k, v, bias, sink, rate, use_sink):
    b, ln, _ = q.shape
    step = min(BAND_STEP, ln)
    per_step = step // BAND_TILE
    cur = pl.BlockSpec((b, step, GROUP), lambda r, i: (0, i, r))
    prev = pl.BlockSpec((b, BAND_TILE, GROUP), lambda r, i: (0, jnp.maximum(i * per_step - 1, 0), r))
    o, lse = pl.pallas_call(
        functools.partial(_band_kernel, use_sink=use_sink),
        grid=(rate, ln // step),
        in_specs=[pl.BlockSpec(memory_space=pltpu.SMEM), cur, prev, cur, prev, cur,
                  pl.BlockSpec((HEADS, 1, BAND_TILE, 2 * BAND_TILE), lambda r, i: (0, 0, 0, 0))],
        out_specs=[cur, pl.BlockSpec((b, step, LANES), lambda r, i: (0, i, r))],
        out_shape=[jax.ShapeDtypeStruct((b, ln, rate * GROUP), BF16), jax.ShapeDtypeStruct((b, ln, rate * LANES), F32)],
        scratch_shapes=[pltpu.VMEM((2, BAND_TILE, HEADS * 2 * BAND_TILE), BF16)],
        compiler_params=pltpu.CompilerParams(dimension_semantics=("arbitrary",) * 2),
        name=f"banded_attention_r{rate}",
    )(sink, q, k, k, v, v, bias)
    return o, lse


def _flash_reset(m_ref, acc_ref):
    m_ref[...] = jnp.full(m_ref.shape, NEG_INF, F32)
    acc_ref[...] = jnp.zeros(acc_ref.shape, F32)


def _flash_update(n, s, v_t, m_ref, acc_ref, shift=None):
    m_old = m_ref[n]
    if shift is None:
        m_new = jnp.maximum(m_old, jnp.max(s, axis=0, keepdims=True))
        p = jnp.exp2(s - m_new)
    else:
        m_new = jnp.maximum(m_old, jnp.max(s, axis=0, keepdims=True) + shift)
        p = jnp.exp2(s - (m_new - shift))
    alpha = jnp.exp2(m_old - m_new)
    acc_ref[n] = alpha * acc_ref[n] + jnp.dot(v_t, p.astype(BF16), preferred_element_type=F32)
    m_ref[n] = m_new


def _flash_result(n, acc_ref):
    return acc_ref[n, 0:HEAD_DIM, :] / acc_ref[n, HEAD_DIM:HEAD_DIM + 1, :]


def _staggered(n_items, scores, update, ahead=MXU_LOOKAHEAD):
    pending = {n: scores(n) for n in range(min(ahead, n_items))}
    for n in range(n_items):
        if n + ahead < n_items:
            pending[n + ahead] = scores(n + ahead)
        update(n, pending.pop(n))


def _pipelined_tiles(first, n_tiles, n_chains, group, load_tile, scores, update, next_ref, left_by_previous=None,
                     last_of_sweep=False):
    ahead = next_ref.shape[0]
    n_items = group * n_chains
    assert ahead <= n_chains

    def body(trip, _, issue_next=True):
        base = first + trip * group
        tiles, pending = {}, {}
        for n in range(n_items):
            cur = next_ref[n] if n < ahead else pending.pop(n)
            if n + ahead < n_items or issue_next:
                g, c = divmod(n + ahead, n_chains)
                if g not in tiles:
                    tiles[g] = load_tile(base + g)
                new = scores(tiles[g], base + g, c)
                if n + ahead < n_items:
                    pending[n + ahead] = new
                else:
                    next_ref[n + ahead - n_items] = new
            update(base + n // n_chains, n % n_chains, cur)

    if left_by_previous is None:
        first_tile = load_tile(first)
        for n in range(ahead):
            next_ref[n] = scores(first_tile, first, n)
    else:
        for n in range(ahead):
            next_ref[n] = next_ref[n] + left_by_previous(first, n)
    n_trips = (n_tiles + group - 1) // group
    if last_of_sweep:
        lax.fori_loop(0, n_trips - 1, body, None)
        body(n_trips - 1, None, issue_next=False)
    else:
        lax.fori_loop(0, n_trips, body, None)


def _flash_scratch(chains, ahead):
    return [pltpu.VMEM((chains, 1, QT), F32), pltpu.VMEM((chains, V_ROWS, QT), F32),
            pltpu.VMEM((ahead, KT, QT), F32)]


N_NEAR = -(-(FAR_DIST + KT - 1) // QT)
N_BIAS_TILES = N_NEAR + 2
DIFF_TILE_GROUP = 2
SLC_TILE_GROUP = 2
FAR_TILE_GROUP = 4


def _bias_tile_index(i, j):
    return jnp.where(j > i, N_NEAR + 1, jnp.minimum(i - j, N_NEAR))


def _whole_far_groups(i, group):
    return jnp.maximum(i - (N_NEAR - 1), 0) // group * group


def _diff_kernel(far_ref, q_ref, k_ref, v_ref, bias_ref, lam_ref, subln_ref, o_ref, qz_ref, m_ref, acc_ref, next_ref,
                 ot_ref, *, lambda_init):
    i = pl.program_id(1)
    q = q_ref[0]
    row = lax.broadcasted_iota(jnp.int32, (GROUP, QT), 0) // DIFF_QK_DIM
    for n in range(2 * HEADS):
        qz_ref[n] = jnp.where(row == n, q, jnp.zeros_like(q))
    _flash_reset(m_ref, acc_ref)

    def load_tile(j):
        return k_ref[0, pl.ds(pl.multiple_of(jnp.minimum(j, i) * KT, KT), KT), :]

    def values(j, n):
        h = n // 2
        return v_ref[0, jnp.minimum(j, i), h * V_ROWS:(h + 1) * V_ROWS, :]

    n_far = _whole_far_groups(i, FAR_TILE_GROUP)
    _pipelined_tiles(0, n_far, 2 * HEADS, FAR_TILE_GROUP, load_tile,
                     lambda k, j, n: jnp.dot(k, qz_ref[n], preferred_element_type=F32),
                     lambda j, n, s: _flash_update(n, s, values(j, n), m_ref, acc_ref, shift=far_ref[n // 2]),
                     next_ref)

    def scores(k, j, n):
        return jnp.dot(k, qz_ref[n], preferred_element_type=F32) + bias_ref[n // 2, _bias_tile_index(i, j)]

    _pipelined_tiles(n_far, i + 1 - n_far, 2 * HEADS, DIFF_TILE_GROUP, load_tile, scores,
                     lambda j, n, s: _flash_update(n, s, values(j, n), m_ref, acc_ref), next_ref,
                     left_by_previous=lambda j, n: bias_ref[n // 2, _bias_tile_index(i, j)], last_of_sweep=True)

    lam_p = lam_ref[...]
    lam = (jnp.exp(jnp.sum(lam_p[0:1] * lam_p[1:2], axis=1, keepdims=True))
           - jnp.exp(jnp.sum(lam_p[2:3] * lam_p[3:4], axis=1, keepdims=True)) + lambda_init)
    for h in range(HEADS):
        o = _flash_result(2 * h, acc_ref) - lam * _flash_result(2 * h + 1, acc_ref)
        msq = jnp.mean(o * o, axis=0, keepdims=True)
        ot_ref[h * HEAD_DIM:(h + 1) * HEAD_DIM, :] = (o * lax.rsqrt(msq + RMS_EPS) * subln_ref[...]
                                                      * (1.0 - lambda_init))
    o_ref[0] = ot_ref[...].T.astype(o_ref.dtype)


def _diff_attention(far, q_t, k, v_t, bias, lam_p, subln, lambda_init):
    b, s, _ = k.shape
    nq = s // QT
    nkv = s // KT
    v4 = v_t.reshape(b, nkv, HEADS * V_ROWS, KT)
    return pl.pallas_call(
        functools.partial(_diff_kernel, lambda_init=lambda_init),
        grid=(b, nq),
        in_specs=[pl.BlockSpec(memory_space=pltpu.SMEM),
                  pl.BlockSpec((1, GROUP, QT), lambda bb, i: (bb * nq + i, 0, 0)),
                  pl.BlockSpec((1, s, GROUP), lambda bb, i: (bb, 0, 0)),
                  pl.BlockSpec((1, nkv, HEADS * V_ROWS, KT), lambda bb, i: (bb, 0, 0, 0)),
                  pl.BlockSpec((HEADS, N_BIAS_TILES, KT, QT), lambda bb, i: (0, 0, 0, 0)),
                  pl.BlockSpec((4, DIFF_QK_DIM), lambda bb, i: (0, 0)),
                  pl.BlockSpec((HEAD_DIM, 1), lambda bb, i: (0, 0))],
        out_specs=pl.BlockSpec((1, QT, GROUP), lambda bb, i: (bb, i, 0)),
        out_shape=jax.ShapeDtypeStruct((b, s, GROUP), BF16),
        scratch_shapes=[pltpu.VMEM((2 * HEADS, GROUP, QT), BF16)] + _flash_scratch(2 * HEADS, MXU_LOOKAHEAD)
        + [pltpu.VMEM((GROUP, QT), F32)],
        compiler_params=pltpu.CompilerParams(dimension_semantics=("arbitrary", "arbitrary"),
                                             vmem_limit_bytes=VMEM_LIMIT),
        name="diff_attention",
    )(far, q_t, k, v4, bias, lam_p, subln)


def _compress_kernel(ch_ref, ptop_ref, pbot_ref, w1t_ref, w1b_ref, b1_ref, w2k_ref, b2k_ref, w2v_ref, b2v_ref,
                     gk_ref, kc_ref, vct_ref):
    ch = ch_ref[0]
    n_c = ch.shape[0]
    u = jnp.dot((ch + ptop_ref[...]).astype(BF16), w1t_ref[...], preferred_element_type=F32)
    v = jnp.dot((ch + pbot_ref[...]).astype(BF16), w1b_ref[...], preferred_element_type=F32)
    v_next = pltpu.roll(v, n_c - 1, 0)
    hid = jax.nn.gelu(u + v_next + b1_ref[...])
    hk = hid[:, :CMP_HIDDEN].astype(BF16)
    hv = hid[:, CMP_HIDDEN:].astype(BF16)
    kc = jnp.dot(hk, w2k_ref[...], preferred_element_type=F32) + b2k_ref[...]
    msq = jnp.mean(kc * kc, axis=-1, keepdims=True)
    kc_ref[0] = (kc * lax.rsqrt(msq + RMS_EPS) * gk_ref[...]).astype(kc_ref.dtype)
    vct = lax.dot_general(w2v_ref[...], hv, NT_DIMS, preferred_element_type=F32) + b2v_ref[...]
    vct_ref[0] = vct.astype(vct_ref.dtype)


def _compress(chunks, ptop, pbot, w1t, w1b, b1, w2k, b2k, w2v, b2v, gk):
    b, n_c, width = chunks.shape
    const = lambda a: pl.BlockSpec(a.shape, lambda bb: (0,) * a.ndim)
    params = (ptop, pbot, w1t, w1b, b1, w2k, b2k, w2v, b2v, gk)
    return pl.pallas_call(
        _compress_kernel,
        grid=(b,),
        in_specs=[pl.BlockSpec((1, n_c, width), lambda bb: (bb, 0, 0))] + [const(a) for a in params],
        out_specs=[pl.BlockSpec((1, n_c, HEAD_DIM), lambda bb: (bb, 0, 0)),
                   pl.BlockSpec((1, HEAD_DIM, n_c), lambda bb: (bb, 0, 0))],
        out_shape=[jax.ShapeDtypeStruct((b, n_c, HEAD_DIM), BF16), jax.ShapeDtypeStruct((b, HEAD_DIM, n_c), BF16)],
        compiler_params=pltpu.CompilerParams(dimension_semantics=("arbitrary",), vmem_limit_bytes=VMEM_LIMIT),
        name="nsa_compress",
    )(chunks, *params)


def _cmp_attn_kernel(q_ref, kc_ref, vct_ref, bias_ref, o_ref, sel_ref, p_ref, *, n_sel):
    i = pl.program_id(0)
    n_tiles = pl.num_programs(0)
    for part in range(1, CMP_PARTS + 1):
        @pl.when((i * CMP_PARTS >= (part - 1) * n_tiles) & (i * CMP_PARTS < part * n_tiles))
        def _(part=part):
            for bb in range(q_ref.shape[0]):
                one = pl.ds(bb, 1)
                _cmp_attn_body(i, kc_ref.shape[1] * part // CMP_PARTS, q_ref.at[bb], kc_ref.at[one], vct_ref.at[one],
                               bias_ref, o_ref.at[bb], sel_ref.at[bb], p_ref, n_sel)


def _cmp_attn_body(i, n_c, q_ref, kc_ref, vct_ref, bias_ref, o_ref, sel_ref, p_ref, n_sel):
    kc = kc_ref[0, 0:n_c, :]
    vct = vct_ref[0, :, 0:n_c]
    n_blk = n_c * CMP_STRIDE // SLC_BLOCK
    probs = []

    def scores(h):
        return (jnp.dot(kc, q_ref[0, h * HEAD_DIM:(h + 1) * HEAD_DIM, :], preferred_element_type=F32)
                + bias_ref[h, 0, 0:n_c, :])

    def update(h, s):
        m = jnp.maximum(jnp.max(s, axis=0, keepdims=True), 0.5 * NEG_INF)
        p = jnp.exp2(s - m)
        den = jnp.sum(p, axis=0, keepdims=True)
        p = p * (1.0 / jnp.maximum(den, TINY))
        o_ref[0, h * HEAD_DIM:(h + 1) * HEAD_DIM, :] = jnp.dot(vct, p.astype(BF16),
                                                               preferred_element_type=F32).astype(o_ref.dtype)
        probs.append(p)

    _staggered(HEADS, scores, update)
    psum = (probs[0] + probs[1]) + (probs[2] + probs[3])
    per_blk = SLC_BLOCK // CMP_STRIDE
    halves = []
    for half in range(QT // LANES):
        p_ref[half, 0:8, :] = jnp.zeros((8, LANES), F32)
        p_ref[half, 8:8 + n_c, :] = psum[:, half * LANES:(half + 1) * LANES]
        p_ref[half, 8 + n_c:16 + n_c, :] = jnp.zeros((8, LANES), F32)
        acc = p_ref[half, pl.ds(7, n_blk, stride=per_blk), :]
        for t in range(per_blk):
            acc = acc + p_ref[half, pl.ds(8 + t, n_blk, stride=per_blk), :]
        halves.append(acc)
    imp = jnp.concatenate(halves, axis=1)
    blk = lax.broadcasted_iota(jnp.int32, (n_blk, QT), 0)
    cur = (i * QT + lax.broadcasted_iota(jnp.int32, (n_blk, QT), 1)) // SLC_BLOCK
    forced = (blk == 0) | (blk == cur) | (blk == cur - 1)
    val = jnp.where(forced, FORCE_SELECT, jnp.where(blk <= cur, imp, NEG_INF))
    sel = jnp.zeros((n_blk, QT), jnp.bool_)
    for _ in range(n_sel):
        top = jnp.max(val, axis=0, keepdims=True)
        idx = jnp.min(jnp.where(val == top, blk, n_blk), axis=0, keepdims=True)
        hit = blk == idx
        sel = sel | hit
        val = jnp.where(hit, -3.0e38, val)
    sel_ref[0, 0:n_blk, :] = jnp.where(sel, 0.0, NEG_INF).astype(sel_ref.dtype)
    if n_blk < sel_ref.shape[1]:
        sel_ref[0, n_blk:, :] = jnp.full((sel_ref.shape[1] - n_blk, QT), NEG_INF, sel_ref.dtype)


def _cmp_attention(q_t, kc, vct, bias, b):
    nt = q_t.shape[0]
    nq = nt // b
    n_c = kc.shape[1]
    n_blk = nq * QT // SLC_BLOCK
    o_cmp, sel = pl.pallas_call(
        functools.partial(_cmp_attn_kernel, n_sel=min(SLC_TOPK, n_blk)),
        grid=(nq,),
        in_specs=[pl.BlockSpec((b, 1, GROUP, QT), lambda i: (0, i, 0, 0)),
                  pl.BlockSpec((b, n_c, HEAD_DIM), lambda i: (0, 0, 0)),
                  pl.BlockSpec((b, HEAD_DIM, n_c), lambda i: (0, 0, 0)),
                  pl.BlockSpec((HEADS, 1, n_c, QT), lambda i: (0, 0, 0, i))],
        out_specs=[pl.BlockSpec((b, 1, GROUP, QT), lambda i: (0, i, 0, 0)),
                   pl.BlockSpec((b, 1, n_blk, QT), lambda i: (0, i, 0, 0))],
        out_shape=[jax.ShapeDtypeStruct((b, nq, GROUP, QT), BF16), jax.ShapeDtypeStruct((b, nq, n_blk, QT), BF16)],
        scratch_shapes=[pltpu.VMEM((QT // LANES, n_c + 16, LANES), F32)],
        compiler_params=pltpu.CompilerParams(dimension_semantics=("arbitrary",), vmem_limit_bytes=VMEM_LIMIT),
        name="nsa_compressed_attention",
    )(q_t.reshape(b, nq, GROUP, QT), kc, vct, bias)
    return o_cmp.reshape(nt, GROUP, QT), sel.reshape(nt, n_blk, QT)


SEL_REP = 8
N_WIN = -(-(NSA_WINDOW - 1 + KT - 1) // QT)


def _slc_win_kernel(far_ref, q_ref, ksw_ref, vs_ref, vw_ref, sel_ref, rep_ref, ocmp_ref, g_ref, bslc_ref, bwin_ref,
                    o_ref, qz_ref, m_ref, acc_ref, next_ref, ot_ref, mask_ref):
    i = pl.program_id(1)
    mask_ref[...] = jnp.dot(rep_ref[...], sel_ref[0], preferred_element_type=F32)
    blocks_per_tile = KT // SLC_BLOCK
    mrows = blocks_per_tile * SEL_REP
    zeros = jnp.zeros((HEAD_DIM, QT), BF16)
    for h in range(HEADS):
        qh = q_ref[0, h * HEAD_DIM:(h + 1) * HEAD_DIM, :]
        qz_ref[h] = jnp.concatenate([qh, zeros], axis=0)
        qz_ref[HEADS + h] = jnp.concatenate([zeros, qh], axis=0)
    _flash_reset(m_ref, acc_ref)

    def load_keys(j):
        return ksw_ref[0, pl.ds(pl.multiple_of(j * KT, KT), KT), :]

    def load_tile(j):
        j = jnp.minimum(j, i)
        m8 = mask_ref[pl.ds(pl.multiple_of(j * mrows, mrows), mrows), :]
        mask = jnp.broadcast_to(m8.reshape(blocks_per_tile, 1, SEL_REP, QT),
                                (blocks_per_tile, SLC_BLOCK // SEL_REP, SEL_REP, QT)).reshape(KT, QT)
        return load_keys(j), mask

    def values(j):
        return vs_ref[0, jnp.minimum(j, i)]

    n_far = _whole_far_groups(i, FAR_TILE_GROUP)
    _pipelined_tiles(0, n_far, HEADS, FAR_TILE_GROUP, load_tile,
                     lambda tile, j, h: jnp.dot(tile[0], qz_ref[h], preferred_element_type=F32) + tile[1],
                     lambda j, h, s: _flash_update(h, s, values(j), m_ref, acc_ref, shift=far_ref[h]), next_ref)

    def slc_scores(tile, j, h):
        k, mask = tile
        return jnp.dot(k, qz_ref[h], preferred_element_type=F32) + mask + bslc_ref[h, _bias_tile_index(i, j)]

    _pipelined_tiles(n_far, i + 1 - n_far, HEADS, SLC_TILE_GROUP, load_tile, slc_scores,
                     lambda j, h, s: _flash_update(h, s, values(j), m_ref, acc_ref), next_ref,
                     left_by_previous=lambda j, h: bslc_ref[h, _bias_tile_index(i, j)], last_of_sweep=True)

    def win_tile(n):
        d = N_WIN - 1 - n // HEADS
        return d, n % HEADS, jnp.maximum(i - d, 0)

    def win_scores(n):
        d, h, j = win_tile(n)
        missing = jnp.where(i < d, NEG_INF, 0.0).astype(F32)
        return jnp.dot(load_keys(j), qz_ref[HEADS + h], preferred_element_type=F32) + (bwin_ref[h, d] + missing)

    def win_update(n, s):
        _, h, j = win_tile(n)
        _flash_update(HEADS + h, s, vw_ref[0, j], m_ref, acc_ref)

    _staggered(N_WIN * HEADS, win_scores, win_update)

    for h in range(HEADS):
        g = jax.nn.sigmoid(g_ref[0, 3 * h:3 * h + 3, :])
        ot_ref[h * HEAD_DIM:(h + 1) * HEAD_DIM, :] = (g[0:1] * ocmp_ref[0, h * HEAD_DIM:(h + 1) * HEAD_DIM, :]
                                                      + g[1:2] * _flash_result(h, acc_ref)
                                                      + g[2:3] * _flash_result(HEADS + h, acc_ref))
    o_ref[0] = ot_ref[...].T.astype(o_ref.dtype)


def _slc_win_attention(far, q_t, ksw, vs_t, vw_t, sel, rep, ocmp, g_t, bslc, bwin):
    b, s, _ = ksw.shape
    nq = s // QT
    nkv = s // KT
    n_blk = s // SLC_BLOCK
    tile = lambda height: pl.BlockSpec((1, height, QT), lambda bb, i: (bb * nq + i, 0, 0))
    whole = lambda a: pl.BlockSpec(a.shape, lambda bb, i: (0,) * a.ndim)
    return pl.pallas_call(
        _slc_win_kernel,
        grid=(b, nq),
        in_specs=[pl.BlockSpec(memory_space=pltpu.SMEM), tile(GROUP),
                  pl.BlockSpec((1, s, 2 * HEAD_DIM), lambda bb, i: (bb, 0, 0)),
                  pl.BlockSpec((1, nkv, V_ROWS, KT), lambda bb, i: (bb, 0, 0, 0)),
                  pl.BlockSpec((1, nkv, V_ROWS, KT), lambda bb, i: (bb, 0, 0, 0)),
                  tile(n_blk), whole(rep), tile(GROUP), tile(GATE_ROWS), whole(bslc), whole(bwin)],
        out_specs=pl.BlockSpec((1, QT, GROUP), lambda bb, i: (bb, i, 0)),
        out_shape=jax.ShapeDtypeStruct((b, s, GROUP), BF16),
        scratch_shapes=[pltpu.VMEM((2 * HEADS, 2 * HEAD_DIM, QT), BF16)] + _flash_scratch(2 * HEADS, MXU_LOOKAHEAD)
        + [pltpu.VMEM((GROUP, QT), F32), pltpu.VMEM((n_blk * SEL_REP, QT), F32)],
        compiler_params=pltpu.CompilerParams(dimension_semantics=("arbitrary", "arbitrary"),
                                             vmem_limit_bytes=VMEM_LIMIT),
        name="nsa_selected_window_attention",
    )(far, q_t, ksw, vs_t.reshape(b, nkv, V_ROWS, KT), vw_t.reshape(b, nkv, V_ROWS, KT), sel, rep, ocmp, g_t,
      bslc, bwin)


def _out_kernel(x_ref, a0_ref, a1_ref, a2_ref, l0_ref, l1_ref, l2_ref, ob_ref, oc_ref, od_ref, gate_ref, e_ref,
                w_ref, o_ref, unfold_ref):
    rows = x_ref.shape[0]

    def unfolded(ref, rate):
        width = ref.shape[1] // rate
        for rho in range(rate):
            for part in range(width // LANES):
                c0 = rho * width + part * LANES
                unfold_ref[part, pl.ds(rho, rows // rate, stride=rate), :] = ref[:, c0:c0 + LANES].astype(F32)
        return jnp.concatenate([unfold_ref[part] for part in range(width // LANES)], axis=1)

    a0, l0 = a0_ref[...], l0_ref[...]
    a1, l1 = unfolded(a1_ref, FOLD_RATES[0]), unfolded(l1_ref, FOLD_RATES[0])
    a2, l2 = unfolded(a2_ref, FOLD_RATES[1]), unfolded(l2_ref, FOLD_RATES[1])
    mx = jnp.maximum(jnp.maximum(l0, l1), l2)
    e0, e1, e2 = jnp.exp2(l0 - mx), jnp.exp2(l1 - mx), jnp.exp2(l2 - mx)
    den = e0 + e1 + e2

    def per_head_lanes(w):
        hi = w.astype(BF16)
        lo = (w - hi.astype(F32)).astype(BF16)
        return (jnp.dot(hi, e_ref[...], preferred_element_type=F32)
                + jnp.dot(lo, e_ref[...], preferred_element_type=F32))

    o_a = per_head_lanes(e0 / den) * a0 + per_head_lanes(e1 / den) * a1 + per_head_lanes(e2 / den) * a2
    y = jnp.concatenate([o_a, ob_ref[...].astype(F32), oc_ref[...].astype(F32), od_ref[...].astype(F32)], axis=1)
    g = gate_ref[...].astype(F32)
    y = y * (g * jax.nn.sigmoid(g))
    o_ref[...] = x_ref[...] + jnp.dot(y.astype(BF16), w_ref[...], preferred_element_type=F32)


def _out_projection(x2, a_outs, a_lses, o_b, o_c, o_d, gate, w_out):
    m, d = x2.shape
    rowblk = lambda width: pl.BlockSpec((PROJ_ROWS, width), lambda i: (i, 0))
    folded = lambda width, rate: pl.BlockSpec((PROJ_ROWS // rate, rate * width), lambda i: (i, 0))
    head_of_lane = np.arange(GROUP) // HEAD_DIM
    expand = jnp.asarray((np.arange(LANES)[:, None] == head_of_lane[None, :]).astype(np.float32), BF16)
    return pl.pallas_call(
        _out_kernel,
        grid=(m // PROJ_ROWS,),
        in_specs=[rowblk(d)] + [folded(GROUP, rate) for rate in (1,) + FOLD_RATES]
        + [folded(LANES, rate) for rate in (1,) + FOLD_RATES] + [rowblk(GROUP)] * 3
        + [rowblk(N_MIXERS * GROUP), pl.BlockSpec((LANES, GROUP), lambda i: (0, 0)),
           pl.BlockSpec((N_MIXERS * GROUP, d), lambda i: (0, 0))],
        out_specs=rowblk(d),
        out_shape=jax.ShapeDtypeStruct((m, d), F32),
        scratch_shapes=[pltpu.VMEM((GROUP // LANES, PROJ_ROWS, LANES), F32)],
        compiler_params=pltpu.CompilerParams(dimension_semantics=("arbitrary",), vmem_limit_bytes=VMEM_LIMIT),
        name="out_projection",
    )(x2, *a_outs, *a_lses, o_b, o_c, o_d, gate, expand, w_out)


def _block_diag_mean(group):
    idx = np.arange(GROUP) // group
    return jnp.asarray((idx[:, None] == idx[None, :]).astype(np.float32) / group, BF16)


def _layer_weights(w_in, qk_gain, qk_gain_diff):
    d = w_in.shape[0]
    sizes = (GROUP,) * 3 + (GROUP, GROUP // 2, GROUP // 2) + (GROUP,) * 3 + (GROUP,) + (HEAD_DIM,) * 6 \
        + (HEADS * 3, N_MIXERS * GROUP)
    offs = np.concatenate([[0], np.cumsum(sizes)])
    col = lambda n: w_in[:, offs[n]:offs[n + 1]]
    (a_q, a_k, a_v, b_q, b_k, b_v, c_q, c_k, c_v, d_q, d_kc, d_vc, d_ks, d_vs, d_kw, d_vw, d_g, gate) = \
        [col(n) for n in range(18)]
    rep_kv = lambda w: jnp.repeat(w.reshape(d, 2, HEAD_DIM), 2, axis=1).reshape(d, GROUP)
    wrm = jnp.concatenate([a_q, a_k, a_v, b_q, rep_kv(b_k), rep_kv(b_v), c_k, d_kc, d_vc, d_ks, d_kw, gate], axis=1)
    wt = jnp.concatenate([c_q, c_v, d_q, d_vs, d_vw, d_g, jnp.zeros((d, GATE_ROWS - HEADS * 3), w_in.dtype)], axis=1).T
    g = qk_gain
    ones = lambda n: jnp.ones((n,), F32)
    tile4 = lambda v: jnp.tile(v, HEADS)
    scale = HEAD_DIM ** -0.5 * LOG2E
    grm = jnp.concatenate([tile4(g[0]) * scale, tile4(g[1]), ones(GROUP), tile4(g[2]) * scale, tile4(g[3]),
                           ones(GROUP), jnp.tile(qk_gain_diff[1], 2 * HEADS), ones(2 * HEAD_DIM), g[6], g[7],
                           ones(N_MIXERS * GROUP)])
    gt = jnp.concatenate([jnp.tile(qk_gain_diff[0], 2 * HEADS) * (DIFF_QK_DIM ** -0.5 * LOG2E), ones(GROUP),
                          tile4(g[4]) * scale, ones(2 * HEAD_DIM + GATE_ROWS)])
    return wrm.astype(BF16), wt.astype(BF16), grm.reshape(1, -1), gt.reshape(-1, 1)


def _compress_weights(cmp_pos, cmp_w1, cmp_b1, cmp_w2, cmp_b2):
    half = CMP_LEN // 2
    pos = jnp.concatenate([cmp_pos[0], cmp_pos[1]], axis=-1)
    ptop = pos[:half].reshape(1, -1)
    pbot = pos[half:].reshape(1, -1)
    w1 = cmp_w1.reshape(2, CMP_LEN, HEAD_DIM, CMP_HIDDEN)
    zeros = jnp.zeros_like(w1[0])
    w1cat = jnp.concatenate([jnp.concatenate([w1[0], zeros], axis=-1),
                             jnp.concatenate([zeros, w1[1]], axis=-1)], axis=1)
    w1t = w1cat[:half].reshape(half * 2 * HEAD_DIM, 2 * CMP_HIDDEN).astype(BF16)
    w1b = w1cat[half:].reshape(half * 2 * HEAD_DIM, 2 * CMP_HIDDEN).astype(BF16)
    b1 = jnp.concatenate([cmp_b1[0], cmp_b1[1]]).reshape(1, -1)
    return (ptop, pbot, w1t, w1b, b1, cmp_w2[0].astype(BF16), cmp_b2[0].reshape(1, -1),
            cmp_w2[1].T.astype(BF16), cmp_b2[1].reshape(-1, 1))


def kernel(x, rel_bias_table, norm_w, w_in, w_out, qk_gain, qk_gain_diff, attn_sinks, diff_lambda, diff_subln,
           cmp_pos, cmp_w1, cmp_b1, cmp_w2, cmp_b2):
    b, s, d = x.shape
    depth = w_in.shape[0]
    n_c = s // CMP_STRIDE
    n_blk = s // SLC_BLOCK
    assert s % (BAND_TILE * DILATED_CONFIGS[-1][1]) == 0 and s % PROJ_ROWS == 0 and d == N_MIXERS * GROUP

    table = rel_bias_table.astype(F32)
    band_bias = [_build_bias(table, head0=0, n_d=1, rows=BAND_TILE, cols=2 * BAND_TILE, base0=BAND_TILE, dstep=0,
                             rs=1, cs=-1, dscale=rate, max_dist=window // rate, scale=LOG2E)
                 for window, rate in DILATED_CONFIGS]
    swa_bias = _build_bias(table, head0=HEADS, n_d=1, rows=BAND_TILE, cols=2 * BAND_TILE, base0=BAND_TILE, dstep=0,
                           rs=1, cs=-1, max_dist=SWA_WINDOW - 1, scale=LOG2E)
    flash_tiles = dict(rows=KT, cols=QT, base0=0, dstep=QT, rs=-1, cs=1, scale=LOG2E)
    diff_bias = _build_bias(table, head0=2 * HEADS, n_d=N_BIAS_TILES, d_valid=N_NEAR + 1, **flash_tiles)
    slc_bias = _build_bias(table, head0=3 * HEADS, n_d=N_BIAS_TILES, d_valid=N_NEAR + 1, **flash_tiles)
    win_bias = _build_bias(table, head0=3 * HEADS, n_d=N_WIN, max_dist=NSA_WINDOW - 1, **flash_tiles)
    far_bias = table[NUM_BUCKETS - 1] * LOG2E
    cmp_bias = _build_bias(table, head0=3 * HEADS, n_d=1, rows=n_c, cols=s, base0=-(CMP_LEN - 1), dstep=0,
                           rs=-CMP_STRIDE, cs=1, r_valid=n_c - 1, col_tile=2 * QT, scale=LOG2E)
    e64, e32 = _block_diag_mean(HEAD_DIM), _block_diag_mean(DIFF_QK_DIM)
    rep_idx = np.arange(n_blk * SEL_REP) // SEL_REP
    rep = jnp.asarray((rep_idx[:, None] == np.arange(n_blk)[None, :]).astype(np.float32), BF16)
    no_sink = jnp.zeros((HEADS,), F32)

    x2 = x.reshape(b * s, d)
    w_in_bf16 = w_in.astype(BF16)
    for layer in range(depth):
        wrm, wt, grm, gt = _layer_weights(w_in_bf16[layer], qk_gain[layer], qk_gain_diff[layer])
        (a_q, a_k, a_v, b_q, b_k, b_v, c_k, kvc, ksw, gate, a_q4, a_k4, a_v4, a_q16, a_k16, a_v16,
         c_qt, c_vt, d_qt, d_vst, d_vwt, d_gt) = _project(x2, norm_w[layer].reshape(1, d), wrm, wt, grm, gt, e64, e32)
        seq = lambda t: t.reshape(b, s, t.shape[-1])
        per_batch = lambda t: t.reshape(b, t.shape[0] // b, t.shape[1])
        flat = lambda t: t.reshape(b * t.shape[1], t.shape[2])
        a_in = ((a_q, a_k, a_v), (a_q4, a_k4, a_v4), (a_q16, a_k16, a_v16))
        a_res = [_banded(*map(per_batch, a_in[n]), band_bias[n], no_sink, rate, False)
                 for n, (_, rate) in enumerate(DILATED_CONFIGS)]
        o_b, _ = _banded(seq(b_q), seq(b_k), seq(b_v), swa_bias, attn_sinks[layer].astype(F32) * LOG2E, 1, True)
        lambda_init = 0.8 - 0.6 * math.exp(-0.3 * layer)
        o_c = _diff_attention(far_bias[2 * HEADS:3 * HEADS], c_qt, seq(c_k), c_vt, diff_bias, diff_lambda[layer].astype(F32),
                              diff_subln[layer].reshape(HEAD_DIM, 1).astype(F32), lambda_init)
        cw = _compress_weights(cmp_pos[layer], cmp_w1[layer], cmp_b1[layer], cmp_w2[layer], cmp_b2[layer])
        kc, vct = _compress(kvc.reshape(b, n_c, CMP_STRIDE * 2 * HEAD_DIM), *cw, qk_gain[layer, 5].reshape(1, -1))
        o_cmp, sel = _cmp_attention(d_qt, kc, vct, cmp_bias, b)
        o_d = _slc_win_attention(far_bias[3 * HEADS:4 * HEADS], d_qt, seq(ksw), d_vst, d_vwt, sel, rep, o_cmp, d_gt,
                                 slc_bias, win_bias)
        x2 = _out_projection(x2, [flat(r[0]) for r in a_res], [flat(r[1]) for r in a_res], o_b.reshape(b * s, GROUP),
                             o_c.reshape(b * s, GROUP), o_d.reshape(b * s, GROUP), gate, w_out[layer].astype(BF16))
    return x2.reshape(b, s, d)
```
